```python
import math
import jax
import jax.numpy as jnp
from jax import lax
import numpy as np


D_MODEL = 1024
BATCH = 4
SEQ = 4096
DEPTH = 2

A_HEADS = 8
A_HEAD_DIM = 64
A_WIDTH = A_HEADS * A_HEAD_DIM
MOBA_BLOCK = 256
MOBA_TOPK = 3
MOBA_Q_CHUNK = 64
REL_BUCKETS = 32
REL_MAX_DIST = 128
S5_WIDTH = D_MODEL - A_WIDTH
S5_GROUP = 16
S5_GROUPS = S5_WIDTH // S5_GROUP
S5_STATE = 64
C_HEADS = 8
C_HEAD_DIM = D_MODEL // C_HEADS
C_WIDTH = C_HEADS * C_HEAD_DIM
C_CONV = 4
C_CHUNK = 64
D_FF = 4 * D_MODEL
N_EVEN = (DEPTH + 1) // 2
N_ODD = DEPTH // 2
DN_ALPHA = (2 * DEPTH) ** 0.25
DN_BETA = (8 * DEPTH) ** -0.25
LN_EPS = 1e-5
RMS_EPS = 1e-6
L2_EPS = 1e-6
NEG_INF = -1e30

kernel_name = 'moba_s5_gated_deltanet_deepnorm_hybrid'


def _layer_norm(x, g, b):
    xf = x.astype(jnp.float32)
    mu = jnp.mean(xf, axis=-1, keepdims=True)
    var = jnp.mean(jnp.square(xf - mu), axis=-1, keepdims=True)
    y = (xf - mu) * lax.rsqrt(var + LN_EPS) * g.astype(jnp.float32) + b.astype(jnp.float32)
    return y.astype(x.dtype)


def _t5_bucket(dist):
    max_exact = REL_BUCKETS // 2
    n = jnp.maximum(dist, 0)
    nf = jnp.maximum(n, 1).astype(jnp.float32)
    large = max_exact + (jnp.log(nf / max_exact) / math.log(REL_MAX_DIST / max_exact)
                         * (REL_BUCKETS - max_exact)).astype(jnp.int32)
    large = jnp.minimum(large, REL_BUCKETS - 1)
    return jnp.where(n < max_exact, n, large)


def _moba_attention(q, k, v, rel_table):
    b, s, h, d = q.shape
    nb = -(-s // MOBA_BLOCK)
    s_pad = nb * MOBA_BLOCK
    pad = ((0, 0), (0, 0), (0, s_pad - s), (0, 0))
    q = q.transpose(0, 2, 1, 3) * (d ** -0.5)
    k_blocks = jnp.pad(k.transpose(0, 2, 1, 3), pad).reshape(b, h, nb, MOBA_BLOCK, d)
    v_blocks = jnp.pad(v.transpose(0, 2, 1, 3), pad).reshape(b, h, nb, MOBA_BLOCK, d)
    k_mean = jnp.mean(k_blocks, axis=3)
    gate = jnp.einsum('bhsd,bhnd->bhsn', q, k_mean).astype(jnp.float32)
    q_blk = jnp.arange(s) // MOBA_BLOCK
    past = jnp.arange(nb)[None, :] < q_blk[:, None]
    gate = jnp.where(past, gate, NEG_INF)
    n_sel = min(MOBA_TOPK, nb)
    _, sel = lax.top_k(gate, n_sel)

    qc = MOBA_Q_CHUNK
    nc = s // qc
    q_ch = q.reshape(b, h, nc, qc, d).transpose(2, 0, 1, 3, 4)
    sel_ch = sel.reshape(b, h, nc, qc, n_sel).transpose(2, 0, 1, 3, 4)
    b_idx = jnp.arange(b)[:, None, None, None]
    h_idx = jnp.arange(h)[None, :, None, None]
    h_idx5 = jnp.arange(h)[None, :, None, None, None]
    table_t = rel_table.T
    offs = jnp.arange(MOBA_BLOCK)

    def chunk_attn(args):
        c, q_c, sel_c = args
        q_pos = c * qc + jnp.arange(qc)
        blk = (c * qc) // MOBA_BLOCK
        k_own = lax.dynamic_index_in_dim(k_blocks, blk, axis=2, keepdims=False)
        v_own = lax.dynamic_index_in_dim(v_blocks, blk, axis=2, keepdims=False)
        dist_own = q_pos[:, None] - (blk * MOBA_BLOCK + offs)[None, :]
        logit_own = (jnp.einsum('bhqd,bhkd->bhqk', q_c, k_own).astype(jnp.float32)
                     + table_t[:, _t5_bucket(dist_own)][None].astype(jnp.float32))
        logit_own = jnp.where(dist_own >= 0, logit_own, NEG_INF)
        k_sel = k_blocks[b_idx, h_idx, sel_c]
        v_sel = v_blocks[b_idx, h_idx, sel_c]
        dist_sel = q_pos[None, None, :, None, None] - (sel_c[..., None] * MOBA_BLOCK + offs)
        logit_sel = (jnp.einsum('bhqd,bhqnkd->bhqnk', q_c, k_sel).astype(jnp.float32)
                     + table_t[h_idx5, _t5_bucket(dist_sel)].astype(jnp.float32))
        valid = (sel_c < blk)[..., None]
        logit_sel = jnp.where(valid, logit_sel, NEG_INF)
        logits = jnp.concatenate([logit_own, logit_sel.reshape(b, h, qc, n_sel * MOBA_BLOCK)], axis=-1)
        p = jax.nn.softmax(logits, axis=-1).astype(v_blocks.dtype)
        p_own = p[..., :MOBA_BLOCK]
        p_sel = p[..., MOBA_BLOCK:].reshape(b, h, qc, n_sel, MOBA_BLOCK)
        return (jnp.einsum('bhqk,bhkd->bhqd', p_own, v_own)
                + jnp.einsum('bhqnk,bhqnkd->bhqd', p_sel, v_sel))

    out = lax.map(chunk_attn, (jnp.arange(nc), q_ch, sel_ch))
    return out.transpose(1, 0, 3, 2, 4).reshape(b, s, h * d)


def _complex_linear_combine(e1, e2):
    a1r, a1i, b1r, b1i = e1
    a2r, a2i, b2r, b2i = e2
    ar = a2r * a1r - a2i * a1i
    ai = a2r * a1i + a2i * a1r
    br = a2r * b1r - a2i * b1i + b2r
    bi = a2r * b1i + a2i * b1r + b2i
    return (ar, ai, br, bi)


def _s5_mixer(u, lam_re, lam_im, b_re, b_im, c_re, c_im, d_skip, log_dt, w_glu, b_glu):
    f32 = jnp.float32
    bsz, s, _ = u.shape
    uf = u.astype(f32)
    ug = uf.reshape(bsz, s, S5_GROUPS, S5_GROUP)
    lr = jnp.minimum(lam_re.astype(f32), -1e-4)
    li = lam_im.astype(f32)
    dt = jnp.exp(log_dt.astype(f32))[:, None]
    mag = jnp.exp(lr * dt)
    a_re = mag * jnp.cos(li * dt)
    a_im = mag * jnp.sin(li * dt)
    den = lr * lr + li * li
    coef_re = ((a_re - 1.0) * lr + a_im * li) / den
    coef_im = (a_im * lr - (a_re - 1.0) * li) / den
    br = b_re.astype(f32)
    bi = b_im.astype(f32)
    bb_re = coef_re[..., None] * br - coef_im[..., None] * bi
    bb_im = coef_re[..., None] * bi + coef_im[..., None] * br
    x_re = jnp.einsum('gph,bsgh->sbgp', bb_re, ug)
    x_im = jnp.einsum('gph,bsgh->sbgp', bb_im, ug)
    a_re_t = jnp.broadcast_to(a_re[None, None], (s, 1) + a_re.shape)
    a_im_t = jnp.broadcast_to(a_im[None, None], (s, 1) + a_im.shape)
    _, _, h_re, h_im = lax.associative_scan(_complex_linear_combine,
                                            (a_re_t, a_im_t, x_re, x_im), axis=0)
    y = (jnp.einsum('ghp,sbgp->bsgh', c_re.astype(f32), h_re)
         - jnp.einsum('ghp,sbgp->bsgh', c_im.astype(f32), h_im))
    y = y.reshape(bsz, s, S5_WIDTH) + d_skip.astype(f32) * uf
    g = jax.nn.gelu(y)
    out = g * jax.nn.sigmoid(g @ w_glu.astype(f32) + b_glu.astype(f32))
    return out.astype(u.dtype)


def _causal_depthwise_conv(x, w):
    kw = w.shape[0]
    return lax.conv_general_dilated(x, w[:, None, :].astype(x.dtype), window_strides=(1,),
                                    padding=[(kw - 1, 0)],
                                    dimension_numbers=('NWC', 'WIO', 'NWC'),
                                    feature_group_count=x.shape[-1])


def _l2norm(t):
    return t * lax.rsqrt(jnp.sum(t * t, axis=-1, keepdims=True) + L2_EPS)


def _gated_delta_rule(q, k, v, g, beta):
    b, s, h, dk = q.shape
    dv = v.shape[-1]
    c = C_CHUNK
    n = s // c

    def chunks(t):
        return t.reshape(b, n, c, h, -1).transpose(0, 3, 1, 2, 4)

    q = chunks(q) * (dk ** -0.5)
    k = chunks(k)
    v = chunks(v)
    beta = beta.reshape(b, n, c, h).transpose(0, 3, 1, 2)
    g = jnp.cumsum(g.reshape(b, n, c, h).transpose(0, 3, 1, 2), axis=-1)
    tril_incl = jnp.tril(jnp.ones((c, c), bool))
    tril_strict = jnp.tril(jnp.ones((c, c), bool), -1)
    diff = g[..., :, None] - g[..., None, :]
    decay = jnp.where(tril_incl, jnp.exp(jnp.where(tril_incl, diff, 0.0)), 0.0)
    k_beta = k * beta[..., None]
    v_beta = v * beta[..., None]
    lower = jnp.where(tril_strict, jnp.einsum('bhnid,bhnjd->bhnij', k_beta, k) * decay, 0.0)
    a_mat = lower + jnp.eye(c, dtype=jnp.float32)
    u = lax.linalg.triangular_solve(a_mat, v_beta, left_side=True, lower=True, unit_diagonal=True)
    w = lax.linalg.triangular_solve(a_mat, k_beta * jnp.exp(g)[..., None], left_side=True,
                                    lower=True, unit_diagonal=True)
    intra = jnp.einsum('bhnid,bhnjd->bhnij', q, k) * decay
    g_last = g[..., -1]
    q_dec = q * jnp.exp(g)[..., None]
    k_dec = k * jnp.exp(g_last[..., None] - g)[..., None]

    def step(state, xs):
        q_i, k_i, u_i, w_i, a_i, gl_i = xs
        v_new = u_i - jnp.einsum('bhcd,bhde->bhce', w_i, state)
        o = jnp.einsum('bhcd,bhde->bhce', q_i, state) + jnp.einsum('bhij,bhje->bhie', a_i, v_new)
        state = state * jnp.exp(gl_i)[..., None, None] + jnp.einsum('bhcd,bhce->bhde', k_i, v_new)
        return state, o

    xs = (jnp.moveaxis(q_dec, 2, 0), jnp.moveaxis(k_dec, 2, 0), jnp.moveaxis(u, 2, 0),
          jnp.moveaxis(w, 2, 0), jnp.moveaxis(intra, 2, 0), jnp.moveaxis(g_last, 2, 0))
    state0 = jnp.zeros((b, h, dk, dv), jnp.float32)
    _, o = lax.scan(step, state0, xs)
    return o.transpose(1, 0, 3, 2, 4).reshape(b, s, h, dv)


def _gdn_mixer(hp, conv_w, a_log, dt_bias, norm_w):
    f32 = jnp.float32
    bsz, s, _ = hp.shape
    qkv = jax.nn.silu(_causal_depthwise_conv(hp[..., :3 * C_WIDTH], conv_w)).astype(f32)
    gate = hp[..., 3 * C_WIDTH:4 * C_WIDTH].astype(f32)
    beta_raw = hp[..., 4 * C_WIDTH:4 * C_WIDTH + C_HEADS].astype(f32)
    a_raw = hp[..., 4 * C_WIDTH + C_HEADS:].astype(f32)
    q = _l2norm(qkv[..., :C_WIDTH].reshape(bsz, s, C_HEADS, C_HEAD_DIM))
    k = _l2norm(qkv[..., C_WIDTH:2 * C_WIDTH].reshape(bsz, s, C_HEADS, C_HEAD_DIM))
    v = qkv[..., 2 * C_WIDTH:].reshape(bsz, s, C_HEADS, C_HEAD_DIM)
    beta = jax.nn.sigmoid(beta_raw)
    g = -jnp.exp(a_log.astype(f32)) * jax.nn.softplus(a_raw + dt_bias.astype(f32))
    o = _gated_delta_rule(q, k, v, g, beta)
    o = o * lax.rsqrt(jnp.mean(o * o, axis=-1, keepdims=True) + RMS_EPS) * norm_w.astype(f32)
    o = o.reshape(bsz, s, C_WIDTH) * jax.nn.silu(gate)
    return o.astype(hp.dtype)


def setup_inputs(seed: int = 0) -> dict:
    key = jax.random.key(seed)
    ks = jax.random.split(key, 32)
    f32 = jnp.float32

    def nrm(k, shape, scale):
        return jax.random.normal(k, shape, f32) * scale

    ev_in = 3 * A_WIDTH + S5_WIDTH
    od_in = 4 * C_WIDTH + 2 * C_HEADS
    dt_c = jnp.exp(jax.random.uniform(ks[20], (N_ODD, C_HEADS), f32, math.log(1e-3), math.log(1e-1)))
    return {
        'x': nrm(ks[0], (BATCH, SEQ, D_MODEL), 1.0),
        'rel_table': nrm(ks[1], (REL_BUCKETS, A_HEADS), 0.5),
        'ev_w_in': nrm(ks[2], (N_EVEN, D_MODEL, ev_in), D_MODEL ** -0.5),
        'ev_w_out': nrm(ks[3], (N_EVEN, A_WIDTH + S5_WIDTH, D_MODEL), (A_WIDTH + S5_WIDTH) ** -0.5 * DN_BETA),
        's5_lambda_re': -0.5 + nrm(ks[4], (N_EVEN, S5_GROUPS, S5_STATE), 0.01),
        's5_lambda_im': jnp.broadcast_to(jnp.pi * jnp.arange(S5_STATE, dtype=f32), (N_EVEN, S5_GROUPS, S5_STATE)),
        's5_b_re': nrm(ks[5], (N_EVEN, S5_GROUPS, S5_STATE, S5_GROUP), (2 * S5_GROUP) ** -0.5),
        's5_b_im': nrm(ks[6], (N_EVEN, S5_GROUPS, S5_STATE, S5_GROUP), (2 * S5_GROUP) ** -0.5),
        's5_c_re': nrm(ks[7], (N_EVEN, S5_GROUPS, S5_GROUP, S5_STATE), S5_STATE ** -0.5),
        's5_c_im': nrm(ks[8], (N_EVEN, S5_GROUPS, S5_GROUP, S5_STATE), S5_STATE ** -0.5),
        's5_d': nrm(ks[9], (N_EVEN, S5_WIDTH), 1.0),
        's5_log_dt': jax.random.uniform(ks[10], (N_EVEN, S5_GROUPS), f32, math.log(1e-3), math.log(1e-1)),
        's5_w_glu': nrm(ks[11], (N_EVEN, S5_WIDTH, S5_WIDTH), S5_WIDTH ** -0.5),
        's5_b_glu': nrm(ks[12], (N_EVEN, S5_WIDTH), 0.01),
        'od_w_in': nrm(ks[13], (N_ODD, D_MODEL, od_in), D_MODEL ** -0.5),
        'od_conv_w': nrm(ks[14], (N_ODD, C_CONV, 3 * C_WIDTH), C_CONV ** -0.5),
        'od_a_log': jnp.log(jax.random.uniform(ks[15], (N_ODD, C_HEADS), f32, 1.0, 16.0)),
        'od_dt_bias': dt_c + jnp.log(-jnp.expm1(-dt_c)),
        'od_norm_w': 1.0 + nrm(ks[16], (N_ODD, C_HEAD_DIM), 0.01),
        'od_w_out': nrm(ks[17], (N_ODD, C_WIDTH, D_MODEL), C_WIDTH ** -0.5 * DN_BETA),
        'ln_mix_g': 1.0 + nrm(ks[18], (DEPTH, D_MODEL), 0.01),
        'ln_mix_b': nrm(ks[19], (DEPTH, D_MODEL), 0.01),
        'mlp_w1': nrm(ks[21], (DEPTH, D_MODEL, D_FF), D_MODEL ** -0.5),
        'mlp_w2': nrm(ks[22], (DEPTH, D_FF, D_MODEL), D_FF ** -0.5 * DN_BETA),
        'ln_ffn_g': 1.0 + nrm(ks[23], (DEPTH, D_MODEL), 0.01),
        'ln_ffn_b': nrm(ks[24], (DEPTH, D_MODEL), 0.01),
    }


def reference(x, rel_table, ev_w_in, ev_w_out, s5_lambda_re, s5_lambda_im, s5_b_re, s5_b_im,
              s5_c_re, s5_c_im, s5_d, s5_log_dt, s5_w_glu, s5_b_glu, od_w_in, od_conv_w,
              od_a_log, od_dt_bias, od_norm_w, od_w_out, ln_mix_g, ln_mix_b, mlp_w1, mlp_w2,
              ln_ffn_g, ln_ffn_b):
    bsz, s, _ = x.shape
    for layer in range(DEPTH):
        i = layer // 2
        if layer % 2 == 0:
            hp = x @ ev_w_in[i]
            q = hp[..., :A_WIDTH].reshape(bsz, s, A_HEADS, A_HEAD_DIM)
            k = hp[..., A_WIDTH:2 * A_WIDTH].reshape(bsz, s, A_HEADS, A_HEAD_DIM)
            v = hp[..., 2 * A_WIDTH:3 * A_WIDTH].reshape(bsz, s, A_HEADS, A_HEAD_DIM)
            u = hp[..., 3 * A_WIDTH:]
            attn = _moba_attention(q, k, v, rel_table)
            ssm = _s5_mixer(u, s5_lambda_re[i], s5_lambda_im[i], s5_b_re[i], s5_b_im[i],
                            s5_c_re[i], s5_c_im[i], s5_d[i], s5_log_dt[i], s5_w_glu[i], s5_b_glu[i])
            mix = jnp.concatenate([attn, ssm.astype(attn.dtype)], axis=-1) @ ev_w_out[i]
        else:
            hp = x @ od_w_in[i]
            mix = _gdn_mixer(hp, od_conv_w[i], od_a_log[i], od_dt_bias[i], od_norm_w[i]) @ od_w_out[i]
        x = _layer_norm(DN_ALPHA * x + mix.astype(x.dtype), ln_mix_g[layer], ln_mix_b[layer])
        hid = jnp.square(jax.nn.relu(x @ mlp_w1[layer]))
        x = _layer_norm(DN_ALPHA * x + (hid @ mlp_w2[layer]).astype(x.dtype), ln_ffn_g[layer], ln_ffn_b[layer])
    return x
```

```python
import functools
import math

import jax
import jax.numpy as jnp
import numpy as np
from jax import lax
from jax.experimental import pallas as pl
from jax.experimental.pallas import tpu as pltpu

F32 = jnp.float32
MXU_DTYPE = jnp.bfloat16

LANES = 128
SUBLANES = 8
VMEM_LIMIT_BYTES = 56 * 1024 * 1024

A_HEADS = 8
A_HEAD_DIM = 64
A_WIDTH = A_HEADS * A_HEAD_DIM
MOBA_BLOCK = 256
MOBA_TOPK = 3
REL_BUCKETS = 32
REL_MAX_DIST = 128
S5_GROUP = 16
S5_GROUPS = 32
S5_STATE = 64
S5_WIDTH = S5_GROUP * S5_GROUPS
C_HEADS = 8
C_HEAD_DIM = 128
C_WIDTH = C_HEADS * C_HEAD_DIM
C_CONV = 4
C_CHUNK = 64
DEPTH = 2
DN_ALPHA = (2 * DEPTH) ** 0.25
LN_EPS = 1e-5
RMS_EPS = 1e-6
L2_EPS = 1e-6
NEG_INF = -1e30


def _mx(t):
    return t.astype(MXU_DTYPE)


def _dot(a, b):
    return jnp.dot(_mx(a), _mx(b), preferred_element_type=F32)


def _dot_nt(a, b):
    return lax.dot_general(_mx(a), _mx(b), (((1,), (1,)), ((), ())), preferred_element_type=F32)


def _sigmoid(t):
    return 1.0 / (1.0 + jnp.exp(-t))


def _layer_norm(r, g, b):
    mu = jnp.mean(r, axis=-1, keepdims=True)
    d = r - mu
    var = jnp.mean(d * d, axis=-1, keepdims=True)
    return d * lax.rsqrt(var + LN_EPS) * g + b


def _params(*sem):
    return pltpu.CompilerParams(dimension_semantics=sem, vmem_limit_bytes=VMEM_LIMIT_BYTES)


def _resident(shape):
    return pl.BlockSpec(shape, lambda *_: (0,) * len(shape), pipeline_mode=pl.Buffered(1))


def _proj_kernel(x_ref, w_ref, o_ref):
    o_ref[...] = jnp.dot(_mx(x_ref[...]), w_ref[...], preferred_element_type=F32)


def _proj(x2d, w, tm, tn):
    m, k = x2d.shape
    n = w.shape[1]
    return pl.pallas_call(
        _proj_kernel,
        grid=(n // tn, m // tm),
        in_specs=[pl.BlockSpec((tm, k), lambda j, i: (i, 0)),
                  pl.BlockSpec((k, tn), lambda j, i: (0, j))],
        out_specs=pl.BlockSpec((tm, tn), lambda j, i: (i, j)),
        out_shape=jax.ShapeDtypeStruct((m, n), F32),
        compiler_params=_params("parallel", "parallel"),
        name="in_proj",
    )(x2d, w)


def _t5_bucket_np(dist):
    max_exact = REL_BUCKETS // 2
    n = np.maximum(dist, 0)
    nf = np.maximum(n, 1).astype(np.float32)
    large = max_exact + (np.log(nf / np.float32(max_exact)) / np.float32(math.log(REL_MAX_DIST / max_exact))
                         * np.float32(REL_BUCKETS - max_exact)).astype(np.int32)
    large = np.minimum(large, REL_BUCKETS - 1)
    return np.where(n < max_exact, n, large).astype(np.int32)


def _bias_kernel(tab_ref, bkt_ref, o_ref):
    h = pl.program_id(0)
    blk = bkt_ref.shape[-1]
    row = lax.broadcasted_iota(jnp.int32, (blk, blk), 0)
    col = lax.broadcasted_iota(jnp.int32, (blk, blk), 1)
    for t in range(2):
        bk = bkt_ref[t]
        acc = jnp.zeros((blk, blk), F32)
        for i in range(REL_BUCKETS):
            acc = jnp.where(bk == i, tab_ref[h, i], acc)
        if t == 0:
            acc = jnp.where(row >= col, acc, NEG_INF)
        o_ref[0, t] = acc


def _bias_tiles(table_t):
    heads = table_t.shape[0]
    offs = np.arange(MOBA_BLOCK)
    dist_own = offs[:, None] - offs[None, :]
    buckets = np.stack([_t5_bucket_np(dist_own), _t5_bucket_np(dist_own + MOBA_BLOCK)])
    return pl.pallas_call(
        _bias_kernel,
        grid=(heads,),
        in_specs=[pl.BlockSpec(memory_space=pltpu.SMEM),
                  pl.BlockSpec((2, MOBA_BLOCK, MOBA_BLOCK), lambda h: (0, 0, 0))],
        out_specs=pl.BlockSpec((1, 2, MOBA_BLOCK, MOBA_BLOCK), lambda h: (h, 0, 0, 0)),
        out_shape=jax.ShapeDtypeStruct((heads, 2, MOBA_BLOCK, MOBA_BLOCK), F32),
        compiler_params=_params("parallel"),
        name="t5_bias",
    )(table_t, jnp.asarray(buckets))


def _moba_kernel(tab_ref, q_ref, k_ref, v_ref, bias_ref, o_ref, kmean_ref):
    hp = pl.program_id(1)
    qb = pl.program_id(2)
    blk = MOBA_BLOCK
    d = A_HEAD_DIM
    nb = k_ref.shape[1] // blk

    @pl.when(qb == 0)
    def _():
        kmean_ref[...] = jnp.mean(k_ref[0].reshape(nb, blk, LANES), axis=1)

    col = lax.broadcasted_iota(jnp.int32, (blk, nb), 1)
    row0 = pl.multiple_of(qb * blk, blk)
    outs = []
    for hh in range(LANES // d):
        lo = hh * d
        q = q_ref[0, :, lo:lo + d] * (d ** -0.5)
        gate = lax.dot_general(q, kmean_ref[:, lo:lo + d], (((1,), (1,)), ((), ())),
                               precision=lax.Precision.HIGHEST, preferred_element_type=F32)
        g = jnp.where(col < qb, gate, -jnp.inf)
        sel = jnp.zeros((blk, nb), F32)
        for _ in range(MOBA_TOPK):
            m = jnp.max(g, axis=1, keepdims=True)
            is_m = jnp.logical_and(g == m, g > -jnp.inf)
            idx = jnp.min(jnp.where(is_m, col, nb), axis=1, keepdims=True)
            pick = col == idx
            sel = jnp.where(pick, 1.0, sel)
            g = jnp.where(pick, -jnp.inf, g)

        qm = _mx(q)
        s = _dot_nt(qm, k_ref[0, pl.ds(row0, blk), lo:lo + d]) + bias_ref[hh, 0]
        m = jnp.max(s, axis=1, keepdims=True)
        p = jnp.exp(s - m)
        l = jnp.sum(p, axis=1, keepdims=True)
        acc = _dot(p, v_ref[0, pl.ds(row0, blk), lo:lo + d])
        far = tab_ref[hp * (LANES // d) + hh, REL_BUCKETS - 1]

        def body(j, carry):
            m, l, acc = carry
            r = pl.multiple_of(j * blk, blk)
            s = _dot_nt(qm, k_ref[0, pl.ds(r, blk), lo:lo + d])
            s = s + jnp.where(j == qb - 1, bias_ref[hh, 1], far)
            sel_j = jnp.max(jnp.where(col == j, sel, 0.0), axis=1, keepdims=True) > 0.0
            s = jnp.where(sel_j, s, NEG_INF)
            m_new = jnp.maximum(m, jnp.max(s, axis=1, keepdims=True))
            alpha = jnp.exp(m - m_new)
            p = jnp.exp(s - m_new)
            l = alpha * l + jnp.sum(p, axis=1, keepdims=True)
            acc = alpha * acc + _dot(p, v_ref[0, pl.ds(r, blk), lo:lo + d])
            return m_new, l, acc

        m, l, acc = lax.fori_loop(0, qb, body, (m, l, acc))
        outs.append(acc / l)
    o_ref[0] = jnp.concatenate(outs, axis=1)


def _moba(hp3d, table_t, bias):
    b, s, _ = hp3d.shape
    npair = A_WIDTH // LANES
    hpp = LANES // A_HEAD_DIM
    return pl.pallas_call(
        _moba_kernel,
        grid=(b, npair, s // MOBA_BLOCK),
        in_specs=[pl.BlockSpec(memory_space=pltpu.SMEM),
                  pl.BlockSpec((1, MOBA_BLOCK, LANES), lambda bi, p, i: (bi, i, p)),
                  pl.BlockSpec((1, s, LANES), lambda bi, p, i: (bi, 0, npair + p)),
                  pl.BlockSpec((1, s, LANES), lambda bi, p, i: (bi, 0, 2 * npair + p)),
                  pl.BlockSpec((hpp, 2, MOBA_BLOCK, MOBA_BLOCK), lambda bi, p, i: (p, 0, 0, 0))],
        out_specs=pl.BlockSpec((1, MOBA_BLOCK, LANES), lambda bi, p, i: (bi, i, p)),
        out_shape=jax.ShapeDtypeStruct((b, s, A_WIDTH), F32),
        scratch_shapes=[pltpu.VMEM((s // MOBA_BLOCK, LANES), F32)],
        compiler_params=_params("parallel", "parallel", "arbitrary"),
        name="moba",
    )(table_t, hp3d, hp3d, hp3d, bias)


S5_HALF_CH = S5_WIDTH // 2
S5_HALF_ST = (S5_GROUPS // 2) * S5_STATE
S5_TILE = 256
S5_NT = S5_HALF_ST // S5_TILE
S5_STEPS = 64


def _s5_kernel(u_ref, wx_ref, wy_ref, a_ref, o_ref, st_ref, x_scr, h_scr):
    rows = u_ref.shape[0]

    @pl.when(pl.program_id(0) == 0)
    def _():
        st_ref[...] = jnp.zeros_like(st_ref)

    def lhs(n):
        c = (n // 2) * LANES
        return jnp.concatenate([u_ref[:, c:c + LANES],
                                u_ref[:, S5_HALF_CH + c:S5_HALF_CH + c + LANES]], axis=1)

    for part in range(2):
        for n in range(S5_NT):
            c0 = part * S5_HALF_ST + n * S5_TILE
            x_scr[:, c0:c0 + S5_TILE] = jnp.dot(lhs(n), wx_ref[part, n], preferred_element_type=F32)

    ar = a_ref[0]
    ai = a_ref[1]

    def step(t, carry):
        hr, hi = carry
        r0 = pl.multiple_of(t * SUBLANES, SUBLANES)
        xr = x_scr[pl.ds(r0, SUBLANES), 0:S5_HALF_ST]
        xi = x_scr[pl.ds(r0, SUBLANES), S5_HALF_ST:2 * S5_HALF_ST]
        nhr = ar * hr - ai * hi + xr
        nhi = ar * hi + ai * hr + xi
        h_scr[pl.ds(r0, SUBLANES), 0:S5_HALF_ST] = nhr
        h_scr[pl.ds(r0, SUBLANES), S5_HALF_ST:2 * S5_HALF_ST] = nhi
        return nhr, nhi

    hr, hi = lax.fori_loop(0, rows // SUBLANES, step, (st_ref[0], st_ref[1]), unroll=2)
    st_ref[0] = hr
    st_ref[1] = hi

    first_half = lax.broadcasted_iota(jnp.int32, (rows, LANES), 0) % 2 == 0
    for m in range(S5_NT // 2):
        acc = jnp.zeros((rows, 2 * LANES), F32)
        for n in (2 * m, 2 * m + 1):
            for part in range(2):
                c0 = part * S5_HALF_ST + n * S5_TILE
                acc = acc + jnp.dot(_mx(h_scr[:, c0:c0 + S5_TILE]), wy_ref[part, n],
                                    preferred_element_type=F32)
        o_ref[:, m * LANES:(m + 1) * LANES] = jnp.where(first_half, acc[:, :LANES], acc[:, LANES:])


def _s5_weights(lam_re, lam_im, b_re, b_im, c_re, c_im, log_dt):
    lr = jnp.minimum(lam_re.astype(F32), -1e-4)
    li = lam_im.astype(F32)
    dt = jnp.exp(log_dt.astype(F32))[:, None]
    mag = jnp.exp(lr * dt)
    a_re = mag * jnp.cos(li * dt)
    a_im = mag * jnp.sin(li * dt)
    den = lr * lr + li * li
    coef_re = ((a_re - 1.0) * lr + a_im * li) / den
    coef_im = (a_im * lr - (a_re - 1.0) * li) / den
    br = b_re.astype(F32)
    bi = b_im.astype(F32)
    bb_re = coef_re[..., None] * br - coef_im[..., None] * bi
    bb_im = coef_re[..., None] * bi + coef_im[..., None] * br
    gh = S5_GROUPS // 2
    eye = jnp.eye(gh, dtype=F32)

    def b_dense(bb):
        t = bb.reshape(2, gh, S5_STATE, S5_GROUP)
        return jnp.einsum('fgph,gk->fghkp', t, eye).reshape(2, S5_HALF_CH, S5_HALF_ST)

    def c_dense(cc):
        t = cc.reshape(2, gh, S5_GROUP, S5_STATE)
        return jnp.einsum('fghp,gk->fgpkh', t, eye).reshape(2, S5_HALF_ST, S5_HALF_CH)

    def x_tiles(bd):
        tiles = []
        for n in range(S5_NT):
            c = (n // 2) * LANES
            tiles.append(jnp.concatenate([bd[0, c:c + LANES, n * S5_TILE:(n + 1) * S5_TILE],
                                          bd[1, c:c + LANES, n * S5_TILE:(n + 1) * S5_TILE]], axis=0))
        return jnp.stack(tiles)

    def y_tiles(cd):
        tiles = []
        for n in range(S5_NT):
            c = (n // 2) * LANES
            tiles.append(jnp.concatenate([cd[0, n * S5_TILE:(n + 1) * S5_TILE, c:c + LANES],
                                          cd[1, n * S5_TILE:(n + 1) * S5_TILE, c:c + LANES]], axis=1))
        return jnp.stack(tiles)

    wx = jnp.stack([x_tiles(b_dense(bb_re)), x_tiles(b_dense(bb_im))]).astype(MXU_DTYPE)
    wy = jnp.stack([y_tiles(c_dense(c_re.astype(F32))), y_tiles(c_dense(-c_im.astype(F32)))]).astype(MXU_DTYPE)
    return wx, wy, a_re.reshape(2, S5_HALF_ST), a_im.reshape(2, S5_HALF_ST)


def _s5_scan(u3d, wx, wy, a_re2, a_im2):
    b, s, _ = u3d.shape
    rows_per_t = 2 * b
    assert rows_per_t == SUBLANES
    half_mask = (jnp.arange(2)[:, None] == (jnp.arange(S5_WIDTH) // S5_HALF_CH)[None, :])
    um = jnp.where(half_mask[None, None], u3d.transpose(1, 0, 2)[:, :, None, :], 0.0)
    um = um.reshape(s * rows_per_t, S5_WIDTH).astype(MXU_DTYPE)
    sel = jnp.arange(rows_per_t) % 2
    a = jnp.stack([a_re2[sel], a_im2[sel]])
    rows = S5_STEPS * rows_per_t
    y = pl.pallas_call(
        _s5_kernel,
        grid=(s // S5_STEPS,),
        in_specs=[pl.BlockSpec((rows, S5_WIDTH), lambda i: (i, 0)),
                  _resident(wx.shape), _resident(wy.shape), _resident(a.shape)],
        out_specs=pl.BlockSpec((rows, S5_HALF_CH), lambda i: (i, 0)),
        out_shape=jax.ShapeDtypeStruct((s * rows_per_t, S5_HALF_CH), F32),
        scratch_shapes=[pltpu.VMEM((2, rows_per_t, S5_HALF_ST), F32),
                        pltpu.VMEM((rows, 2 * S5_HALF_ST), F32),
                        pltpu.VMEM((rows, 2 * S5_HALF_ST), F32)],
        compiler_params=_params("arbitrary"),
        name="s5_scan",
    )(um, wx, wy, a)
    return y.reshape(s, b, S5_WIDTH).transpose(1, 0, 2)


MLP_CHUNK = 1024


def _tail_body(mix, x_ref, g1_ref, b1_ref, w1_ref, w2_ref, g2_ref, b2_ref, o_ref):
    x1 = _layer_norm(DN_ALPHA * x_ref[...] + mix, g1_ref[...], b1_ref[...])
    x1m = _mx(x1)
    acc = jnp.zeros(x1.shape, F32)
    for f in range(w1_ref.shape[1] // MLP_CHUNK):
        hid = jnp.dot(x1m, w1_ref[:, f * MLP_CHUNK:(f + 1) * MLP_CHUNK], preferred_element_type=F32)
        hid = jnp.square(jnp.maximum(hid, 0.0))
        acc = acc + jnp.dot(_mx(hid), w2_ref[f * MLP_CHUNK:(f + 1) * MLP_CHUNK, :],
                            preferred_element_type=F32)
    o_ref[...] = _layer_norm(DN_ALPHA * x1 + acc, g2_ref[...], b2_ref[...])


def _tail_even_kernel(attn_ref, y_ref, u_ref, dsk_ref, wg_ref, bg_ref, wo_ref, x_ref,
                      g1_ref, b1_ref, w1_ref, w2_ref, g2_ref, b2_ref, o_ref):
    y = y_ref[...] + dsk_ref[...] * u_ref[...]
    g = 0.5 * y * (1.0 + jnp.tanh(math.sqrt(2.0 / math.pi) * (y + 0.044715 * (y * y * y))))
    ssm = g * _sigmoid(jnp.dot(_mx(g), wg_ref[...], preferred_element_type=F32) + bg_ref[...])
    ka = attn_ref.shape[1]
    mix = (jnp.dot(_mx(attn_ref[...]), wo_ref[0:ka, :], preferred_element_type=F32)
           + jnp.dot(_mx(ssm), wo_ref[ka:, :], preferred_element_type=F32))
    _tail_body(mix, x_ref, g1_ref, b1_ref, w1_ref, w2_ref, g2_ref, b2_ref, o_ref)


def _tail_odd_kernel(a_ref, wo_ref, x_ref, g1_ref, b1_ref, w1_ref, w2_ref, g2_ref, b2_ref, o_ref):
    mix = jnp.dot(_mx(a_ref[...]), wo_ref[...], preferred_element_type=F32)
    _tail_body(mix, x_ref, g1_ref, b1_ref, w1_ref, w2_ref, g2_ref, b2_ref, o_ref)


def _row(v):
    return v.reshape(1, -1).astype(F32)


def _tail_common_specs(tm, d, dff):
    return [_resident((d, d)),
            pl.BlockSpec((tm, d), lambda i: (i, 0)),
            _resident((1, d)), _resident((1, d)),
            _resident((d, dff)), _resident((dff, d)),
            _resident((1, d)), _resident((1, d))]


def _tail_even(attn2d, y2d, hp2d, x2d, d_skip, w_glu, b_glu, w_out, g1, b1, w1, w2, g2, b2, tm):
    m, d = x2d.shape
    dff = w1.shape[1]
    ucol = hp2d.shape[1] // S5_WIDTH - 1
    return pl.pallas_call(
        _tail_even_kernel,
        grid=(m // tm,),
        in_specs=[pl.BlockSpec((tm, A_WIDTH), lambda i: (i, 0)),
                  pl.BlockSpec((tm, S5_WIDTH), lambda i: (i, 0)),
                  pl.BlockSpec((tm, S5_WIDTH), lambda i: (i, ucol)),
                  _resident((1, S5_WIDTH)), _resident((S5_WIDTH, S5_WIDTH)), _resident((1, S5_WIDTH))]
        + _tail_common_specs(tm, d, dff),
        out_specs=pl.BlockSpec((tm, d), lambda i: (i, 0)),
        out_shape=jax.ShapeDtypeStruct((m, d), F32),
        compiler_params=_params("parallel"),
        name="tail_even",
    )(attn2d, y2d, hp2d, _row(d_skip), _mx(w_glu), _row(b_glu), _mx(w_out), x2d,
      _row(g1), _row(b1), _mx(w1), _mx(w2), _row(g2), _row(b2))


def _tail_odd(a2d, x2d, w_out, g1, b1, w1, w2, g2, b2, tm):
    m, d = x2d.shape
    dff = w1.shape[1]
    return pl.pallas_call(
        _tail_odd_kernel,
        grid=(m // tm,),
        in_specs=[pl.BlockSpec((tm, a2d.shape[1]), lambda i: (i, 0))] + _tail_common_specs(tm, d, dff),
        out_specs=pl.BlockSpec((tm, d), lambda i: (i, 0)),
        out_shape=jax.ShapeDtypeStruct((m, d), F32),
        compiler_params=_params("parallel"),
        name="tail_odd",
    )(a2d, _mx(w_out), x2d, _row(g1), _row(b1), _mx(w1), _mx(w2), _row(g2), _row(b2))


GDN_TILE = 256


def _softplus(t):
    return jnp.maximum(t, 0.0) + jnp.log1p(jnp.exp(-jnp.abs(t)))


def _gdn_prep_kernel(q_ref, k_ref, v_ref, qp_ref, kp_ref, vp_ref, cwq_ref, cwk_ref, cwv_ref,
                     ba_ref, alog_ref, dtb_ref, u_ref, w_ref, qd_ref, kd_ref, intra_ref):
    i = pl.program_id(1)
    h = pl.program_id(2)
    t = q_ref.shape[1]
    c = C_CHUNK
    dk = C_HEAD_DIM

    def conv_silu(main_ref, prev_ref, cw_ref):
        x = main_ref[0]
        pv = jnp.where(i > 0, prev_ref[0], 0.0)
        ext = jnp.concatenate([pv, x], axis=0)
        y = cw_ref[C_CONV - 1:C_CONV, :] * x
        for back in range(1, C_CONV):
            shifted = pltpu.roll(ext, back, 0)[SUBLANES:, :]
            y = y + cw_ref[C_CONV - 1 - back:C_CONV - back, :] * shifted
        return y * _sigmoid(y)

    def l2n(z):
        return z * lax.rsqrt(jnp.sum(z * z, axis=-1, keepdims=True) + L2_EPS)

    q = l2n(conv_silu(q_ref, qp_ref, cwq_ref)) * (dk ** -0.5)
    k = l2n(conv_silu(k_ref, kp_ref, cwk_ref))
    v = conv_silu(v_ref, vp_ref, cwv_ref)

    ba = ba_ref[0]
    lane = lax.broadcasted_iota(jnp.int32, (t, LANES), 1)
    beta = _sigmoid(jnp.sum(jnp.where(lane == h, ba, 0.0), axis=1, keepdims=True))
    g_all = -jnp.exp(alog_ref[...]) * _softplus(ba + dtb_ref[...])
    g_col = jnp.sum(jnp.where(lane == C_HEADS + h, g_all, 0.0), axis=1, keepdims=True)
    sub = lax.broadcasted_iota(jnp.int32, (LANES, t), 0)
    g_row = jnp.sum(jnp.where(sub == C_HEADS + h, g_all.T, 0.0), axis=0, keepdims=True)

    ii = lax.broadcasted_iota(jnp.int32, (c, c), 0)
    jj = lax.broadcasted_iota(jnp.int32, (c, c), 1)
    low = ii >= jj
    for ch in range(t // c):
        r0 = ch * c
        qc, kc, vc = q[r0:r0 + c], k[r0:r0 + c], v[r0:r0 + c]
        bc = beta[r0:r0 + c]
        gc_col = jnp.sum(jnp.where(low, g_row[:, r0:r0 + c], 0.0), axis=1, keepdims=True)
        gc_row = jnp.sum(jnp.where(ii <= jj, g_col[r0:r0 + c], 0.0), axis=0, keepdims=True)
        decay = jnp.where(low, jnp.exp(jnp.where(low, gc_col - gc_row, 0.0)), 0.0)
        kb = kc * bc
        lneg = jnp.where(ii > jj, -(_dot_nt(kb, kc) * decay), 0.0)
        tm = lneg
        pw = lneg
        for _ in range(int(math.log2(c)) - 1):
            pw = _dot(pw, pw)
            tm = tm + pw + _dot(tm, pw)
        egc = jnp.exp(gc_col)
        vb = vc * bc
        kbe = kb * egc
        u_ref[0, r0:r0 + c, :] = vb + _dot(tm, vb)
        w_ref[0, r0:r0 + c, :] = kbe + _dot(tm, kbe)
        gl = gc_col[c - 1:c, :]
        qd_ref[0, r0:r0 + c, :] = qc * egc
        kd_ref[0, r0:r0 + c, :] = kc * jnp.exp(gl - gc_col)
        intra_ref[0, 0, r0:r0 + c, 0:c] = _dot_nt(qc, kc) * decay
        intra_ref[0, 0, r0:r0 + c, c:2 * c] = jnp.broadcast_to(jnp.exp(gl), (c, c))


def _gdn_prep(hp3d, ba3d, conv_w, alog_row, dtb_row):
    b, s, _ = hp3d.shape
    t = GDN_TILE
    hds = C_HEADS
    per = t // SUBLANES

    def main(off):
        return pl.BlockSpec((1, t, LANES), lambda bi, i, h: (bi, i, off + h))

    def prev(off):
        return pl.BlockSpec((1, SUBLANES, LANES),
                            lambda bi, i, h: (bi, jnp.maximum(i * per - 1, 0), off + h))

    def cw(off):
        return pl.BlockSpec((C_CONV, LANES), lambda bi, i, h: (0, off + h))

    out_main = pl.BlockSpec((1, t, LANES), lambda bi, i, h: (bi, i, h))
    full = jax.ShapeDtypeStruct((b, s, C_WIDTH), F32)
    return pl.pallas_call(
        _gdn_prep_kernel,
        grid=(b, s // t, hds),
        in_specs=[main(0), main(hds), main(2 * hds), prev(0), prev(hds), prev(2 * hds),
                  cw(0), cw(hds), cw(2 * hds),
                  pl.BlockSpec((1, t, LANES), lambda bi, i, h: (bi, i, 0)),
                  pl.BlockSpec((1, LANES), lambda bi, i, h: (0, 0)),
                  pl.BlockSpec((1, LANES), lambda bi, i, h: (0, 0))],
        out_specs=[out_main, out_main, out_main, out_main,
                   pl.BlockSpec((1, 1, t, LANES), lambda bi, i, h: (bi, h, i, 0))],
        out_shape=[full, full, full, full, jax.ShapeDtypeStruct((b, hds, s, LANES), F32)],
        compiler_params=_params("parallel", "parallel", "parallel"),
        name="gdn_prep",
    )(hp3d, hp3d, hp3d, hp3d, hp3d, hp3d, conv_w, conv_w, conv_w, ba3d, alog_row, dtb_row)


def _gdn_scan_kernel(u_ref, w_ref, qd_ref, kd_ref, intra_ref, gate_ref, nw_ref, o_ref, st_ref):
    c = C_CHUNK
    dk = C_HEAD_DIM
    t = u_ref.shape[1]

    @pl.when(pl.program_id(1) == 0)
    def _():
        st_ref[...] = jnp.zeros_like(st_ref)

    nw = nw_ref[...]
    for ch in range(t // c):
        r0 = ch * c
        for h in range(C_HEADS):
            cs = slice(h * dk, (h + 1) * dk)
            st = st_ref[h]
            wq = jnp.concatenate([w_ref[0, r0:r0 + c, cs], qd_ref[0, r0:r0 + c, cs]], axis=0)
            ws = _dot(wq, st)
            v_new = u_ref[0, r0:r0 + c, cs] - ws[:c]
            o = ws[c:] + _dot(intra_ref[0, h, r0:r0 + c, 0:c], v_new)
            edl = intra_ref[0, h, r0:r0 + 1, c:c + 1]
            st_ref[h] = st * edl + _dot(kd_ref[0, r0:r0 + c, cs].T, v_new)
            o = o * lax.rsqrt(jnp.mean(o * o, axis=-1, keepdims=True) + RMS_EPS) * nw
            gt = gate_ref[0, r0:r0 + c, cs]
            o_ref[0, r0:r0 + c, cs] = o * (gt * _sigmoid(gt))


def _gdn_scan(u, w, qd, kd, intra, hp3d, norm_w):
    b, s, cw = u.shape
    t = GDN_TILE
    blk = pl.BlockSpec((1, t, cw), lambda bi, i: (bi, i, 0))
    return pl.pallas_call(
        _gdn_scan_kernel,
        grid=(b, s // t),
        in_specs=[blk, blk, blk, blk,
                  pl.BlockSpec((1, C_HEADS, t, LANES), lambda bi, i: (bi, 0, i, 0)),
                  pl.BlockSpec((1, t, cw), lambda bi, i: (bi, i, 3)),
                  pl.BlockSpec((1, C_HEAD_DIM), lambda bi, i: (0, 0))],
        out_specs=blk,
        out_shape=jax.ShapeDtypeStruct((b, s, cw), F32),
        scratch_shapes=[pltpu.VMEM((C_HEADS, C_HEAD_DIM, C_HEAD_DIM), F32)],
        compiler_params=_params("parallel", "arbitrary"),
        name="gdn_scan",
    )(u, w, qd, kd, intra, hp3d, _row(norm_w))


ROW_TILE = 512


def kernel(x, rel_table, ev_w_in, ev_w_out, s5_lambda_re, s5_lambda_im, s5_b_re, s5_b_im, s5_c_re, s5_c_im, s5_d, s5_log_dt, s5_w_glu, s5_b_glu, od_w_in, od_conv_w, od_a_log, od_dt_bias, od_norm_w, od_w_out, ln_mix_g, ln_mix_b, mlp_w1, mlp_w2, ln_ffn_g, ln_ffn_b):
    bsz, s, d = x.shape
    m = bsz * s
    x2d = x.reshape(m, d)
    table_t = rel_table.T.astype(F32)
    bias = _bias_tiles(table_t)
    for layer in range(DEPTH):
        i = layer // 2
        if layer % 2 == 0:
            hp = _proj(x2d, _mx(ev_w_in[i]), ROW_TILE, 1024)
            hp3d = hp.reshape(bsz, s, hp.shape[1])
            attn = _moba(hp3d, table_t, bias)
            wx, wy, a_re2, a_im2 = _s5_weights(s5_lambda_re[i], s5_lambda_im[i], s5_b_re[i], s5_b_im[i],
                                               s5_c_re[i], s5_c_im[i], s5_log_dt[i])
            y = _s5_scan(hp3d[..., 3 * A_WIDTH:], wx, wy, a_re2, a_im2)
            x2d = _tail_even(attn.reshape(m, A_WIDTH), y.reshape(m, S5_WIDTH), hp, x2d,
                             s5_d[i], s5_w_glu[i], s5_b_glu[i], ev_w_out[i],
                             ln_mix_g[layer], ln_mix_b[layer], mlp_w1[layer], mlp_w2[layer],
                             ln_ffn_g[layer], ln_ffn_b[layer], ROW_TILE)
        else:
            w_in = od_w_in[i]
            hp = _proj(x2d, _mx(w_in[:, :4 * C_WIDTH]), ROW_TILE, 1024)
            w_ba = jnp.pad(w_in[:, 4 * C_WIDTH:], ((0, 0), (0, LANES - 2 * C_HEADS)))
            ba = _proj(x2d, _mx(w_ba), ROW_TILE, LANES)
            hp3d = hp.reshape(bsz, s, 4 * C_WIDTH)
            pad8 = jnp.zeros((C_HEADS,), F32)
            padr = jnp.zeros((LANES - 2 * C_HEADS,), F32)
            alog_row = jnp.concatenate([pad8, od_a_log[i].astype(F32), padr]).reshape(1, LANES)
            dtb_row = jnp.concatenate([pad8, od_dt_bias[i].astype(F32), padr]).reshape(1, LANES)
            u, w, qd, kd, intra = _gdn_prep(hp3d, ba.reshape(bsz, s, LANES), od_conv_w[i].astype(F32),
                                            alog_row, dtb_row)
            o = _gdn_scan(u, w, qd, kd, intra, hp3d, od_norm_w[i])
            x2d = _tail_odd(o.reshape(m, C_WIDTH), x2d, od_w_out[i],
                            ln_mix_g[layer], ln_mix_b[layer], mlp_w1[layer], mlp_w2[layer],
                            ln_ffn_g[layer], ln_ffn_b[layer], ROW_TILE)
    return x2d.reshape(bsz, s, d)
```

```python
import functools
import math

import jax
import jax.numpy as jnp
import numpy as np
from jax import lax
from jax.experimental import pallas as pl
from jax.experimental.pallas import tpu as pltpu

F32 = jnp.float32
MXU_DTYPE = jnp.bfloat16

LANES = 128
SUBLANES = 8
VMEM_LIMIT_BYTES = 56 * 1024 * 1024

A_HEADS = 8
A_HEAD_DIM = 64
A_WIDTH = A_HEADS * A_HEAD_DIM
MOBA_BLOCK = 256
MOBA_TOPK = 3
REL_BUCKETS = 32
REL_MAX_DIST = 128
S5_GROUP = 16
S5_GROUPS = 32
S5_STATE = 64
S5_WIDTH = S5_GROUP * S5_GROUPS
C_HEADS = 8
C_HEAD_DIM = 128
C_WIDTH = C_HEADS * C_HEAD_DIM
C_CONV = 4
C_CHUNK = 64
DEPTH = 2
DN_ALPHA = (2 * DEPTH) ** 0.25
LN_EPS = 1e-5
RMS_EPS = 1e-6
L2_EPS = 1e-6
NEG_INF = -1e30


def _mx(t):
    return t.astype(MXU_DTYPE)


def _dot(a, b):
    return jnp.dot(_mx(a), _mx(b), preferred_element_type=F32)


def _dot_nt(a, b):
    return lax.dot_general(_mx(a), _mx(b), (((1,), (1,)), ((), ())), preferred_element_type=F32)


def _sigmoid(t):
    return 1.0 / (1.0 + jnp.exp(-t))


def _layer_norm(r, g, b):
    mu = jnp.mean(r, axis=-1, keepdims=True)
    d = r - mu
    var = jnp.mean(d * d, axis=-1, keepdims=True)
    return d * lax.rsqrt(var + LN_EPS) * g + b


def _params(*sem):
    return pltpu.CompilerParams(dimension_semantics=sem, vmem_limit_bytes=VMEM_LIMIT_BYTES)


def _resident(shape):
    return pl.BlockSpec(shape, lambda *_: (0,) * len(shape), pipeline_mode=pl.Buffered(1))


def _proj_kernel(x_ref, w_ref, o_ref):
    o_ref[...] = jnp.dot(_mx(x_ref[...]), w_ref[...], preferred_element_type=F32)


def _proj(x2d, w, tm, tn):
    m, k = x2d.shape
    n = w.shape[1]
    return pl.pallas_call(
        _proj_kernel,
        grid=(n // tn, m // tm),
        in_specs=[pl.BlockSpec((tm, k), lambda j, i: (i, 0)),
                  pl.BlockSpec((k, tn), lambda j, i: (0, j))],
        out_specs=pl.BlockSpec((tm, tn), lambda j, i: (i, j)),
        out_shape=jax.ShapeDtypeStruct((m, n), F32),
        compiler_params=_params("parallel", "parallel"),
        name="in_proj",
    )(x2d, w)


def _t5_bucket_np(dist):
    max_exact = REL_BUCKETS // 2
    n = np.maximum(dist, 0)
    nf = np.maximum(n, 1).astype(np.float32)
    large = max_exact + (np.log(nf / np.float32(max_exact)) / np.float32(math.log(REL_MAX_DIST / max_exact))
                         * np.float32(REL_BUCKETS - max_exact)).astype(np.int32)
    large = np.minimum(large, REL_BUCKETS - 1)
    return np.where(n < max_exact, n, large).astype(np.int32)


def _bias_kernel(tab_ref, bkt_ref, o_ref):
    h = pl.program_id(0)
    blk = bkt_ref.shape[-1]
    key = lax.broadcasted_iota(jnp.int32, (blk, blk), 0)
    qry = lax.broadcasted_iota(jnp.int32, (blk, blk), 1)
    far = tab_ref[h, REL_BUCKETS - 1]
    for t in range(2):
        bk = bkt_ref[t]
        acc = jnp.zeros((blk, blk), F32)
        for i in range(REL_BUCKETS):
            acc = jnp.where(bk == i, tab_ref[h, i] - far, acc)
        if t == 0:
            acc = jnp.where(qry >= key, acc, NEG_INF)
        o_ref[0, t] = acc


def _bias_tiles(table_t):
    heads = table_t.shape[0]
    offs = np.arange(MOBA_BLOCK)
    dist_own = offs[None, :] - offs[:, None]
    buckets = np.stack([_t5_bucket_np(dist_own), _t5_bucket_np(dist_own + MOBA_BLOCK)])
    return pl.pallas_call(
        _bias_kernel,
        grid=(heads,),
        in_specs=[pl.BlockSpec(memory_space=pltpu.SMEM),
                  pl.BlockSpec((2, MOBA_BLOCK, MOBA_BLOCK), lambda h: (0, 0, 0))],
        out_specs=pl.BlockSpec((1, 2, MOBA_BLOCK, MOBA_BLOCK), lambda h: (h, 0, 0, 0)),
        out_shape=jax.ShapeDtypeStruct((heads, 2, MOBA_BLOCK, MOBA_BLOCK), F32),
        compiler_params=_params("parallel"),
        name="t5_bias",
    )(table_t, jnp.asarray(buckets))


def _moba_kernel(q_ref, k_ref, v_ref, bias_ref, o_ref, kmean_ref, kb_ref, vt_ref, sel_ref, st_ref, p_ref):
    qb = pl.program_id(2)
    blk = MOBA_BLOCK
    d = A_HEAD_DIM
    nh = LANES // d
    nb = k_ref.shape[1] // blk

    @pl.when(qb == 0)
    def _():
        for j in range(nb):
            kj = k_ref[0, j * blk:(j + 1) * blk, :]
            kb_ref[j] = _mx(kj)
            kmean_ref[j:j + 1, :] = jnp.mean(kj, axis=0, keepdims=True)
            vt_ref[j] = _mx(v_ref[0, j * blk:(j + 1) * blk, :].T)

    qt = (q_ref[0] * (d ** -0.5)).T
    zero = jnp.zeros((d, blk), F32)
    wq = _mx(jnp.concatenate([jnp.concatenate([qt[:d], zero], axis=1),
                              jnp.concatenate([zero, qt[d:]], axis=1)], axis=0))

    blk_id = lax.broadcasted_iota(jnp.int32, (nb, blk), 0)
    for hh in range(nh):
        gate = lax.dot_general(kmean_ref[:, hh * d:(hh + 1) * d], qt[hh * d:(hh + 1) * d],
                               (((1,), (0,)), ((), ())),
                               precision=lax.Precision.HIGHEST, preferred_element_type=F32)
        g = jnp.where(blk_id < qb, gate, -jnp.inf)
        sel = jnp.zeros((nb, blk), F32)
        for _ in range(MOBA_TOPK):
            m = jnp.max(g, axis=0, keepdims=True)
            is_m = jnp.logical_and(g == m, g > -jnp.inf)
            idx = jnp.min(jnp.where(is_m, blk_id, nb), axis=0, keepdims=True)
            pick = blk_id == idx
            sel = jnp.where(pick, 1.0, sel)
            g = jnp.where(pick, -jnp.inf, g)
        sel_ref[:, hh * blk:(hh + 1) * blk] = sel

    def score_stage(slot, ja, jb):
        st_ref[slot, 0] = jnp.dot(kb_ref[ja], wq, preferred_element_type=F32)
        st_ref[slot, 1] = jnp.dot(kb_ref[jb], wq, preferred_element_type=F32)

    def softmax_stage(slot, blocks, stats):
        rows = [None if ok is None else jnp.where(ok, sel_ref[pl.ds(j, 1), :], 0.0) for j, _, ok in blocks]
        new = []
        for hh in range(nh):
            ss = []
            for w, ((j, bias_idx, ok), selrow) in enumerate(zip(blocks, rows)):
                s = st_ref[slot, w, :, hh * blk:(hh + 1) * blk]
                if bias_idx is not None:
                    s = s + bias_ref[hh, bias_idx]
                if ok is not None:
                    s = jnp.where(selrow[:, hh * blk:(hh + 1) * blk] > 0.0, s, NEG_INF)
                ss.append(s)
            m_new = jnp.maximum(jnp.max(ss[0], axis=0, keepdims=True), jnp.max(ss[1], axis=0, keepdims=True))
            if stats is not None:
                m, l = stats[hh]
                m_new = jnp.maximum(m, m_new)
            ps = [jnp.exp(s - m_new) for s in ss]
            p_ref[slot, hh, 0:blk, :] = _mx(ps[0])
            p_ref[slot, hh, blk:2 * blk, :] = _mx(ps[1])
            l_new = jnp.sum(ps[0], axis=0, keepdims=True) + jnp.sum(ps[1], axis=0, keepdims=True)
            if stats is None:
                new.append((m_new, l_new, None))
            else:
                alpha = jnp.exp(m - m_new)
                new.append((m_new, alpha * l + l_new, alpha))
        return new

    def value_stage(slot, ja, jb):
        return [jnp.dot(jnp.concatenate([vt_ref[ja, hh * d:(hh + 1) * d, :], vt_ref[jb, hh * d:(hh + 1) * d, :]],
                                        axis=1), p_ref[slot, hh], preferred_element_type=F32)
                for hh in range(nh)]

    nfar = qb - 1
    npairs = (nfar + 1) // 2

    def far_blocks(i):
        return jnp.minimum(2 * i, nb - 1), jnp.minimum(2 * i + 1, nb - 1)

    j_prev = jnp.maximum(qb - 1, 0)
    score_stage(0, qb, j_prev)
    score_stage(1, *far_blocks(0))
    st0 = softmax_stage(0, [(qb, 0, None), (j_prev, 1, qb > 0)], None)
    init = (tuple((m, l, jnp.zeros((d, blk), F32)) for m, l, _ in st0), qb, j_prev)

    def trip(slot, k, carry):
        stats, ja_last, jb_last = carry
        score_stage(1 - slot, *far_blocks(k))
        pv = value_stage(1 - slot, ja_last, jb_last)
        ja, jb = far_blocks(k - 1)
        new = softmax_stage(slot, [(ja, None, True), (jb, None, 2 * k - 1 < nfar)],
                            [(m, l) for m, l, _ in stats])
        stats = tuple((m, l, (acc + pv_h) * alpha)
                      for (m, l, alpha), (_, _, acc), pv_h in zip(new, stats, pv))
        return stats, ja, jb

    def trip_dyn(k, carry):
        return lax.cond(k % 2 == 1, functools.partial(trip, 1, k), functools.partial(trip, 0, k), carry)

    stats, ja_last, jb_last = lax.fori_loop(1, npairs + 1, trip_dyn, init)
    pv = lax.cond(npairs % 2 == 1, lambda: value_stage(1, ja_last, jb_last),
                  lambda: value_stage(0, ja_last, jb_last))
    out_t = jnp.concatenate([(acc + pv_h) / l for (_, l, acc), pv_h in zip(stats, pv)], axis=0)
    o_ref[0] = out_t.T


def _moba(hp3d, bias):
    b, s, _ = hp3d.shape
    npair = A_WIDTH // LANES
    hpp = LANES // A_HEAD_DIM
    nb = s // MOBA_BLOCK
    return pl.pallas_call(
        _moba_kernel,
        grid=(b, npair, nb),
        in_specs=[pl.BlockSpec((1, MOBA_BLOCK, LANES), lambda bi, p, i: (bi, i, p)),
                  pl.BlockSpec((1, s, LANES), lambda bi, p, i: (bi, 0, npair + p)),
                  pl.BlockSpec((1, s, LANES), lambda bi, p, i: (bi, 0, 2 * npair + p)),
                  pl.BlockSpec((hpp, 2, MOBA_BLOCK, MOBA_BLOCK), lambda bi, p, i: (p, 0, 0, 0))],
        out_specs=pl.BlockSpec((1, MOBA_BLOCK, LANES), lambda bi, p, i: (bi, i, p)),
        out_shape=jax.ShapeDtypeStruct((b, s, A_WIDTH), F32),
        scratch_shapes=[pltpu.VMEM((nb, LANES), F32),
                        pltpu.VMEM((nb, MOBA_BLOCK, LANES), MXU_DTYPE),
                        pltpu.VMEM((nb, LANES, MOBA_BLOCK), MXU_DTYPE),
                        pltpu.VMEM((nb, 2 * MOBA_BLOCK), F32),
                        pltpu.VMEM((2, 2, MOBA_BLOCK, hpp * MOBA_BLOCK), F32),
                        pltpu.VMEM((2, hpp, 2 * MOBA_BLOCK, MOBA_BLOCK), MXU_DTYPE)],
        compiler_params=_params("parallel", "parallel", "arbitrary"),
        name="moba",
    )(hp3d, hp3d, hp3d, bias)


S5_HALF_CH = S5_WIDTH // 2
S5_HALF_ST = (S5_GROUPS // 2) * S5_STATE
S5_TILE = 256
S5_NT = S5_HALF_ST // S5_TILE
S5_STEPS = 64


def _s5_kernel(u_ref, wx_ref, wy_ref, a_ref, o_ref, st_ref, x_scr, h_scr):
    rows = u_ref.shape[0]

    @pl.when(pl.program_id(0) == 0)
    def _():
        st_ref[...] = jnp.zeros_like(st_ref)

    def lhs(n):
        c = (n // 2) * LANES
        return jnp.concatenate([u_ref[:, c:c + LANES],
                                u_ref[:, S5_HALF_CH + c:S5_HALF_CH + c + LANES]], axis=1)

    for part in range(2):
        for n in range(S5_NT):
            c0 = part * S5_HALF_ST + n * S5_TILE
            x_scr[:, c0:c0 + S5_TILE] = jnp.dot(lhs(n), wx_ref[part, n], preferred_element_type=F32)

    ar = a_ref[0]
    ai = a_ref[1]

    def step(t, carry):
        hr, hi = carry
        r0 = pl.multiple_of(t * SUBLANES, SUBLANES)
        xr = x_scr[pl.ds(r0, SUBLANES), 0:S5_HALF_ST]
        xi = x_scr[pl.ds(r0, SUBLANES), S5_HALF_ST:2 * S5_HALF_ST]
        nhr = ar * hr - ai * hi + xr
        nhi = ar * hi + ai * hr + xi
        h_scr[pl.ds(r0, SUBLANES), 0:S5_HALF_ST] = nhr
        h_scr[pl.ds(r0, SUBLANES), S5_HALF_ST:2 * S5_HALF_ST] = nhi
        return nhr, nhi

    hr, hi = lax.fori_loop(0, rows // SUBLANES, step, (st_ref[0], st_ref[1]), unroll=2)
    st_ref[0] = hr
    st_ref[1] = hi

    first_half = lax.broadcasted_iota(jnp.int32, (rows, LANES), 0) % 2 == 0
    for m in range(S5_NT // 2):
        acc = jnp.zeros((rows, 2 * LANES), F32)
        for n in (2 * m, 2 * m + 1):
            for part in range(2):
                c0 = part * S5_HALF_ST + n * S5_TILE
                acc = acc + jnp.dot(_mx(h_scr[:, c0:c0 + S5_TILE]), wy_ref[part, n],
                                    preferred_element_type=F32)
        o_ref[:, m * LANES:(m + 1) * LANES] = jnp.where(first_half, acc[:, :LANES], acc[:, LANES:])


def _s5_weights(lam_re, lam_im, b_re, b_im, c_re, c_im, log_dt):
    lr = jnp.minimum(lam_re.astype(F32), -1e-4)
    li = lam_im.astype(F32)
    dt = jnp.exp(log_dt.astype(F32))[:, None]
    mag = jnp.exp(lr * dt)
    a_re = mag * jnp.cos(li * dt)
    a_im = mag * jnp.sin(li * dt)
    den = lr * lr + li * li
    coef_re = ((a_re - 1.0) * lr + a_im * li) / den
    coef_im = (a_im * lr - (a_re - 1.0) * li) / den
    br = b_re.astype(F32)
    bi = b_im.astype(F32)
    bb_re = coef_re[..., None] * br - coef_im[..., None] * bi
    bb_im = coef_re[..., None] * bi + coef_im[..., None] * br
    gh = S5_GROUPS // 2
    eye = jnp.eye(gh, dtype=F32)

    def b_dense(bb):
        t = bb.reshape(2, gh, S5_STATE, S5_GROUP)
        return jnp.einsum('fgph,gk->fghkp', t, eye).reshape(2, S5_HALF_CH, S5_HALF_ST)

    def c_dense(cc):
        t = cc.reshape(2, gh, S5_GROUP, S5_STATE)
        return jnp.einsum('fghp,gk->fgpkh', t, eye).reshape(2, S5_HALF_ST, S5_HALF_CH)

    def x_tiles(bd):
        tiles = []
        for n in range(S5_NT):
            c = (n // 2) * LANES
            tiles.append(jnp.concatenate([bd[0, c:c + LANES, n * S5_TILE:(n + 1) * S5_TILE],
                                          bd[1, c:c + LANES, n * S5_TILE:(n + 1) * S5_TILE]], axis=0))
        return jnp.stack(tiles)

    def y_tiles(cd):
        tiles = []
        for n in range(S5_NT):
            c = (n // 2) * LANES
            tiles.append(jnp.concatenate([cd[0, n * S5_TILE:(n + 1) * S5_TILE, c:c + LANES],
                                          cd[1, n * S5_TILE:(n + 1) * S5_TILE, c:c + LANES]], axis=1))
        return jnp.stack(tiles)

    wx = jnp.stack([x_tiles(b_dense(bb_re)), x_tiles(b_dense(bb_im))]).astype(MXU_DTYPE)
    wy = jnp.stack([y_tiles(c_dense(c_re.astype(F32))), y_tiles(c_dense(-c_im.astype(F32)))]).astype(MXU_DTYPE)
    return wx, wy, a_re.reshape(2, S5_HALF_ST), a_im.reshape(2, S5_HALF_ST)


def _s5_scan(u3d, wx, wy, a_re2, a_im2):
    b, s, _ = u3d.shape
    rows_per_t = 2 * b
    assert rows_per_t == SUBLANES
    half_mask = (jnp.arange(2)[:, None] == (jnp.arange(S5_WIDTH) // S5_HALF_CH)[None, :])
    um = jnp.where(half_mask[None, None], u3d.transpose(1, 0, 2)[:, :, None, :], 0.0)
    um = um.reshape(s * rows_per_t, S5_WIDTH).astype(MXU_DTYPE)
    sel = jnp.arange(rows_per_t) % 2
    a = jnp.stack([a_re2[sel], a_im2[sel]])
    rows = S5_STEPS * rows_per_t
    y = pl.pallas_call(
        _s5_kernel,
        grid=(s // S5_STEPS,),
        in_specs=[pl.BlockSpec((rows, S5_WIDTH), lambda i: (i, 0)),
                  _resident(wx.shape), _resident(wy.shape), _resident(a.shape)],
        out_specs=pl.BlockSpec((rows, S5_HALF_CH), lambda i: (i, 0)),
        out_shape=jax.ShapeDtypeStruct((s * rows_per_t, S5_HALF_CH), F32),
        scratch_shapes=[pltpu.VMEM((2, rows_per_t, S5_HALF_ST), F32),
                        pltpu.VMEM((rows, 2 * S5_HALF_ST), F32),
                        pltpu.VMEM((rows, 2 * S5_HALF_ST), F32)],
        compiler_params=_params("arbitrary"),
        name="s5_scan",
    )(um, wx, wy, a)
    return y.reshape(s, b, S5_WIDTH).transpose(1, 0, 2)


MLP_CHUNK = 1024


def _tail_body(mix, x_ref, g1_ref, b1_ref, w1_ref, w2_ref, g2_ref, b2_ref, o_ref):
    x1 = _layer_norm(DN_ALPHA * x_ref[...] + mix, g1_ref[...], b1_ref[...])
    x1m = _mx(x1)
    acc = jnp.zeros(x1.shape, F32)
    for f in range(w1_ref.shape[1] // MLP_CHUNK):
        hid = jnp.dot(x1m, w1_ref[:, f * MLP_CHUNK:(f + 1) * MLP_CHUNK], preferred_element_type=F32)
        hid = jnp.square(jnp.maximum(hid, 0.0))
        acc = acc + jnp.dot(_mx(hid), w2_ref[f * MLP_CHUNK:(f + 1) * MLP_CHUNK, :],
                            preferred_element_type=F32)
    o_ref[...] = _layer_norm(DN_ALPHA * x1 + acc, g2_ref[...], b2_ref[...])


def _tail_even_kernel(attn_ref, y_ref, u_ref, dsk_ref, wg_ref, bg_ref, wo_ref, x_ref,
                      g1_ref, b1_ref, w1_ref, w2_ref, g2_ref, b2_ref, o_ref):
    y = y_ref[...] + dsk_ref[...] * u_ref[...]
    g = 0.5 * y * (1.0 + jnp.tanh(math.sqrt(2.0 / math.pi) * (y + 0.044715 * (y * y * y))))
    ssm = g * _sigmoid(jnp.dot(_mx(g), wg_ref[...], preferred_element_type=F32) + bg_ref[...])
    ka = attn_ref.shape[1]
    mix = (jnp.dot(_mx(attn_ref[...]), wo_ref[0:ka, :], preferred_element_type=F32)
           + jnp.dot(_mx(ssm), wo_ref[ka:, :], preferred_element_type=F32))
    _tail_body(mix, x_ref, g1_ref, b1_ref, w1_ref, w2_ref, g2_ref, b2_ref, o_ref)


def _tail_odd_kernel(a_ref, wo_ref, x_ref, g1_ref, b1_ref, w1_ref, w2_ref, g2_ref, b2_ref, o_ref):
    mix = jnp.dot(_mx(a_ref[...]), wo_ref[...], preferred_element_type=F32)
    _tail_body(mix, x_ref, g1_ref, b1_ref, w1_ref, w2_ref, g2_ref, b2_ref, o_ref)


def _row(v):
    return v.reshape(1, -1).astype(F32)


def _tail_common_specs(tm, d, dff):
    return [_resident((d, d)),
            pl.BlockSpec((tm, d), lambda i: (i, 0)),
            _resident((1, d)), _resident((1, d)),
            _resident((d, dff)), _resident((dff, d)),
            _resident((1, d)), _resident((1, d))]


def _tail_even(attn2d, y2d, hp2d, x2d, d_skip, w_glu, b_glu, w_out, g1, b1, w1, w2, g2, b2, tm):
    m, d = x2d.shape
    dff = w1.shape[1]
    ucol = hp2d.shape[1] // S5_WIDTH - 1
    return pl.pallas_call(
        _tail_even_kernel,
        grid=(m // tm,),
        in_specs=[pl.BlockSpec((tm, A_WIDTH), lambda i: (i, 0)),
                  pl.BlockSpec((tm, S5_WIDTH), lambda i: (i, 0)),
                  pl.BlockSpec((tm, S5_WIDTH), lambda i: (i, ucol)),
                  _resident((1, S5_WIDTH)), _resident((S5_WIDTH, S5_WIDTH)), _resident((1, S5_WIDTH))]
        + _tail_common_specs(tm, d, dff),
        out_specs=pl.BlockSpec((tm, d), lambda i: (i, 0)),
        out_shape=jax.ShapeDtypeStruct((m, d), F32),
        compiler_params=_params("parallel"),
        name="tail_even",
    )(attn2d, y2d, hp2d, _row(d_skip), _mx(w_glu), _row(b_glu), _mx(w_out), x2d,
      _row(g1), _row(b1), _mx(w1), _mx(w2), _row(g2), _row(b2))


def _tail_odd(a2d, x2d, w_out, g1, b1, w1, w2, g2, b2, tm):
    m, d = x2d.shape
    dff = w1.shape[1]
    return pl.pallas_call(
        _tail_odd_kernel,
        grid=(m // tm,),
        in_specs=[pl.BlockSpec((tm, a2d.shape[1]), lambda i: (i, 0))] + _tail_common_specs(tm, d, dff),
        out_specs=pl.BlockSpec((tm, d), lambda i: (i, 0)),
        out_shape=jax.ShapeDtypeStruct((m, d), F32),
        compiler_params=_params("parallel"),
        name="tail_odd",
    )(a2d, _mx(w_out), x2d, _row(g1), _row(b1), _mx(w1), _mx(w2), _row(g2), _row(b2))


GDN_PREP_TILE = 512
GDN_SCAN_TILE = 256


def _softplus(t):
    return jnp.maximum(t, 0.0) + jnp.log1p(jnp.exp(-jnp.abs(t)))


def _gdn_prep_kernel(q_ref, k_ref, v_ref, qp_ref, kp_ref, vp_ref, cwq_ref, cwk_ref, cwv_ref,
                     ba_ref, alog_ref, dtb_ref, u_ref, w_ref, qd_ref, kd_ref, intra_ref):
    i = pl.program_id(1)
    h = pl.program_id(2)
    t = q_ref.shape[1]
    c = C_CHUNK
    dk = C_HEAD_DIM

    def conv_silu(main_ref, prev_ref, cw_ref):
        x = main_ref[0]
        pv = jnp.where(i > 0, prev_ref[0], 0.0)
        ext = jnp.concatenate([pv, x], axis=0)
        y = cw_ref[C_CONV - 1:C_CONV, :] * x
        for back in range(1, C_CONV):
            shifted = pltpu.roll(ext, back, 0)[SUBLANES:, :]
            y = y + cw_ref[C_CONV - 1 - back:C_CONV - back, :] * shifted
        return y * _sigmoid(y)

    def l2n(z):
        return z * lax.rsqrt(jnp.sum(z * z, axis=-1, keepdims=True) + L2_EPS)

    q = l2n(conv_silu(q_ref, qp_ref, cwq_ref)) * (dk ** -0.5)
    k = l2n(conv_silu(k_ref, kp_ref, cwk_ref))
    v = conv_silu(v_ref, vp_ref, cwv_ref)

    ba = ba_ref[0]
    lane = lax.broadcasted_iota(jnp.int32, (t, LANES), 1)
    beta = _sigmoid(jnp.sum(jnp.where(lane == h, ba, 0.0), axis=1, keepdims=True))
    g_all = -jnp.exp(alog_ref[...]) * _softplus(ba + dtb_ref[...])
    g_col = jnp.sum(jnp.where(lane == C_HEADS + h, g_all, 0.0), axis=1, keepdims=True)
    sub = lax.broadcasted_iota(jnp.int32, (LANES, t), 0)
    g_row = jnp.sum(jnp.where(sub == C_HEADS + h, g_all.T, 0.0), axis=0, keepdims=True)

    ii = lax.broadcasted_iota(jnp.int32, (c, c), 0)
    jj = lax.broadcasted_iota(jnp.int32, (c, c), 1)
    low = ii >= jj
    chunks = [slice(ch * c, (ch + 1) * c) for ch in range(t // c)]
    gc_cols = [jnp.sum(jnp.where(low, g_row[:, r], 0.0), axis=1, keepdims=True) for r in chunks]
    gc_rows = [jnp.sum(jnp.where(ii <= jj, g_col[r], 0.0), axis=0, keepdims=True) for r in chunks]
    decays = [jnp.where(low, jnp.exp(jnp.where(low, a - b, 0.0)), 0.0) for a, b in zip(gc_cols, gc_rows)]
    kbs = [k[r] * beta[r] for r in chunks]
    kks = [_dot_nt(kb, k[r]) for kb, r in zip(kbs, chunks)]
    qks = [_dot_nt(q[r], k[r]) for r in chunks]
    pws = [jnp.where(ii > jj, -(kk * dec), 0.0) for kk, dec in zip(kks, decays)]
    tms = pws
    for _ in range(int(math.log2(c)) - 1):
        pws = [_dot(pw, pw) for pw in pws]
        tms = [tm + pw + _dot(tm, pw) for tm, pw in zip(tms, pws)]
    egcs = [jnp.exp(a) for a in gc_cols]
    vbs = [v[r] * beta[r] for r in chunks]
    kbes = [kb * e for kb, e in zip(kbs, egcs)]
    us = [vb + _dot(tm, vb) for tm, vb in zip(tms, vbs)]
    ws = [kbe + _dot(tm, kbe) for tm, kbe in zip(tms, kbes)]
    for n, r in enumerate(chunks):
        gl = gc_cols[n][c - 1:c, :]
        u_ref[0, r, :] = us[n]
        w_ref[0, r, :] = ws[n]
        qd_ref[0, r, :] = q[r] * egcs[n]
        kd_ref[0, r, :] = k[r] * jnp.exp(gl - gc_cols[n])
        intra_ref[0, 0, r, 0:c] = qks[n] * decays[n]
        intra_ref[0, 0, r, c:2 * c] = jnp.broadcast_to(jnp.exp(gl), (c, c))


def _gdn_prep(hp3d, ba3d, conv_w, alog_row, dtb_row):
    b, s, _ = hp3d.shape
    t = GDN_PREP_TILE
    hds = C_HEADS
    per = t // SUBLANES

    def main(off):
        return pl.BlockSpec((1, t, LANES), lambda bi, i, h: (bi, i, off + h))

    def prev(off):
        return pl.BlockSpec((1, SUBLANES, LANES),
                            lambda bi, i, h: (bi, jnp.maximum(i * per - 1, 0), off + h))

    def cw(off):
        return pl.BlockSpec((C_CONV, LANES), lambda bi, i, h: (0, off + h))

    out_main = pl.BlockSpec((1, t, LANES), lambda bi, i, h: (bi, i, h))
    full = jax.ShapeDtypeStruct((b, s, C_WIDTH), F32)
    return pl.pallas_call(
        _gdn_prep_kernel,
        grid=(b, s // t, hds),
        in_specs=[main(0), main(hds), main(2 * hds), prev(0), prev(hds), prev(2 * hds),
                  cw(0), cw(hds), cw(2 * hds),
                  pl.BlockSpec((1, t, LANES), lambda bi, i, h: (bi, i, 0)),
                  pl.BlockSpec((1, LANES), lambda bi, i, h: (0, 0)),
                  pl.BlockSpec((1, LANES), lambda bi, i, h: (0, 0))],
        out_specs=[out_main, out_main, out_main, out_main,
                   pl.BlockSpec((1, 1, t, LANES), lambda bi, i, h: (bi, h, i, 0))],
        out_shape=[full, full, full, full, jax.ShapeDtypeStruct((b, hds, s, LANES), F32)],
        compiler_params=_params("parallel", "parallel", "parallel"),
        name="gdn_prep",
    )(hp3d, hp3d, hp3d, hp3d, hp3d, hp3d, conv_w, conv_w, conv_w, ba3d, alog_row, dtb_row)


def _gdn_scan_kernel(u_ref, w_ref, qd_ref, kd_ref, intra_ref, gate_ref, nw_ref, o_ref, st_ref):
    c = C_CHUNK
    dk = C_HEAD_DIM
    t = u_ref.shape[1]

    @pl.when(pl.program_id(1) == 0)
    def _():
        st_ref[...] = jnp.zeros_like(st_ref)

    nw = nw_ref[...]
    for ch in range(t // c):
        r0 = ch * c
        for h in range(C_HEADS):
            cs = slice(h * dk, (h + 1) * dk)
            st = st_ref[h]
            wq = jnp.concatenate([w_ref[0, r0:r0 + c, cs], qd_ref[0, r0:r0 + c, cs]], axis=0)
            ws = _dot(wq, st)
            v_new = u_ref[0, r0:r0 + c, cs] - ws[:c]
            o = ws[c:] + _dot(intra_ref[0, h, r0:r0 + c, 0:c], v_new)
            edl = intra_ref[0, h, r0:r0 + 1, c:c + 1]
            st_ref[h] = st * edl + _dot(kd_ref[0, r0:r0 + c, cs].T, v_new)
            o = o * lax.rsqrt(jnp.mean(o * o, axis=-1, keepdims=True) + RMS_EPS) * nw
            gt = gate_ref[0, r0:r0 + c, cs]
            o_ref[0, r0:r0 + c, cs] = o * (gt * _sigmoid(gt))


def _gdn_scan(u, w, qd, kd, intra, hp3d, norm_w):
    b, s, cw = u.shape
    t = GDN_SCAN_TILE
    blk = pl.BlockSpec((1, t, cw), lambda bi, i: (bi, i, 0))
    return pl.pallas_call(
        _gdn_scan_kernel,
        grid=(b, s // t),
        in_specs=[blk, blk, blk, blk,
                  pl.BlockSpec((1, C_HEADS, t, LANES), lambda bi, i: (bi, 0, i, 0)),
                  pl.BlockSpec((1, t, cw), lambda bi, i: (bi, i, 3)),
                  pl.BlockSpec((1, C_HEAD_DIM), lambda bi, i: (0, 0))],
        out_specs=blk,
        out_shape=jax.ShapeDtypeStruct((b, s, cw), F32),
        scratch_shapes=[pltpu.VMEM((C_HEADS, C_HEAD_DIM, C_HEAD_DIM), F32)],
        compiler_params=_params("parallel", "arbitrary"),
        name="gdn_scan",
    )(u, w, qd, kd, intra, hp3d, _row(norm_w))


ROW_TILE = 512


def kernel(x, rel_table, ev_w_in, ev_w_out, s5_lambda_re, s5_lambda_im, s5_b_re, s5_b_im, s5_c_re, s5_c_im, s5_d, s5_log_dt, s5_w_glu, s5_b_glu, od_w_in, od_conv_w, od_a_log, od_dt_bias, od_norm_w, od_w_out, ln_mix_g, ln_mix_b, mlp_w1, mlp_w2, ln_ffn_g, ln_ffn_b):
    bsz, s, d = x.shape
    m = bsz * s
    x2d = x.reshape(m, d)
    table_t = rel_table.T.astype(F32)
    bias = _bias_tiles(table_t)
    for layer in range(DEPTH):
        i = layer // 2
        if layer % 2 == 0:
            hp = _proj(x2d, _mx(ev_w_in[i]), ROW_TILE, 1024)
            hp3d = hp.reshape(bsz, s, hp.shape[1])
            attn = _moba(hp3d, bias)
            wx, wy, a_re2, a_im2 = _s5_weights(s5_lambda_re[i], s5_lambda_im[i], s5_b_re[i], s5_b_im[i],
                                               s5_c_re[i], s5_c_im[i], s5_log_dt[i])
            y = _s5_scan(hp3d[..., 3 * A_WIDTH:], wx, wy, a_re2, a_im2)
            x2d = _tail_even(attn.reshape(m, A_WIDTH), y.reshape(m, S5_WIDTH), hp, x2d,
                             s5_d[i], s5_w_glu[i], s5_b_glu[i], ev_w_out[i],
                             ln_mix_g[layer], ln_mix_b[layer], mlp_w1[layer], mlp_w2[layer],
                             ln_ffn_g[layer], ln_ffn_b[layer], ROW_TILE)
        else:
            w_in = od_w_in[i]
            hp = _proj(x2d, _mx(w_in[:, :4 * C_WIDTH]), ROW_TILE, 1024)
            w_ba = jnp.pad(w_in[:, 4 * C_WIDTH:], ((0, 0), (0, LANES - 2 * C_HEADS)))
            ba = _proj(x2d, _mx(w_ba), ROW_TILE, LANES)
            hp3d = hp.reshape(bsz, s, 4 * C_WIDTH)
            pad8 = jnp.zeros((C_HEADS,), F32)
            padr = jnp.zeros((LANES - 2 * C_HEADS,), F32)
            alog_row = jnp.concatenate([pad8, od_a_log[i].astype(F32), padr]).reshape(1, LANES)
            dtb_row = jnp.concatenate([pad8, od_dt_bias[i].astype(F32), padr]).reshape(1, LANES)
            u, w, qd, kd, intra = _gdn_prep(hp3d, ba.reshape(bsz, s, LANES), od_conv_w[i].astype(F32),
                                            alog_row, dtb_row)
            o = _gdn_scan(u, w, qd, kd, intra, hp3d, od_norm_w[i])
            x2d = _tail_odd(o.reshape(m, C_WIDTH), x2d, od_w_out[i],
                            ln_mix_g[layer], ln_mix_b[layer], mlp_w1[layer], mlp_w2[layer],
                            ln_ffn_g[layer], ln_ffn_b[layer], ROW_TILE)
    return x2d.reshape(bsz, s, d)
```

```python
import functools
import math

import jax
import jax.numpy as jnp
import numpy as np
from jax import lax
from jax.experimental import pallas as pl
from jax.experimental.pallas import tpu as pltpu

F32 = jnp.float32
MXU_DTYPE = jnp.bfloat16

LANES = 128
SUBLANES = 8
VMEM_LIMIT_BYTES = 56 * 1024 * 1024

A_HEADS = 8
A_HEAD_DIM = 64
A_WIDTH = A_HEADS * A_HEAD_DIM
MOBA_BLOCK = 256
MOBA_TOPK = 3
REL_BUCKETS = 32
REL_MAX_DIST = 128
S5_GROUP = 16
S5_GROUPS = 32
S5_STATE = 64
S5_WIDTH = S5_GROUP * S5_GROUPS
C_HEADS = 8
C_HEAD_DIM = 128
C_WIDTH = C_HEADS * C_HEAD_DIM
C_CONV = 4
C_CHUNK = 64
DEPTH = 2
DN_ALPHA = (2 * DEPTH) ** 0.25
LN_EPS = 1e-5
RMS_EPS = 1e-6
L2_EPS = 1e-6
NEG_INF = -1e30


def _mx(t):
    return t.astype(MXU_DTYPE)


def _dot(a, b):
    return jnp.dot(_mx(a), _mx(b), preferred_element_type=F32)


def _dot_nt(a, b):
    return lax.dot_general(_mx(a), _mx(b), (((1,), (1,)), ((), ())), preferred_element_type=F32)


def _sigmoid(t):
    return 1.0 / (1.0 + jnp.exp(-t))


def _layer_norm(r, g, b):
    mu = jnp.mean(r, axis=-1, keepdims=True)
    d = r - mu
    var = jnp.mean(d * d, axis=-1, keepdims=True)
    return d * lax.rsqrt(var + LN_EPS) * g + b


def _params(*sem):
    return pltpu.CompilerParams(dimension_semantics=sem, vmem_limit_bytes=VMEM_LIMIT_BYTES)


def _resident(shape):
    return pl.BlockSpec(shape, lambda *_: (0,) * len(shape), pipeline_mode=pl.Buffered(1))


PROJ_CHUNK = 1024


def _proj_kernel(x_ref, w_ref, o_ref):
    xm = _mx(x_ref[...])
    n = w_ref.shape[1]
    for c0 in range(0, n, PROJ_CHUNK):
        c1 = min(c0 + PROJ_CHUNK, n)
        o_ref[:, c0:c1] = jnp.dot(xm, w_ref[:, c0:c1], preferred_element_type=F32)


def _proj(x2d, w, tm):
    m, k = x2d.shape
    n = w.shape[1]
    return pl.pallas_call(
        _proj_kernel,
        grid=(m // tm,),
        in_specs=[pl.BlockSpec((tm, k), lambda i: (i, 0)), _resident((k, n))],
        out_specs=pl.BlockSpec((tm, n), lambda i: (i, 0)),
        out_shape=jax.ShapeDtypeStruct((m, n), F32),
        compiler_params=_params("parallel"),
        name="in_proj",
    )(x2d, w)


def _t5_bucket_np(dist):
    max_exact = REL_BUCKETS // 2
    n = np.maximum(dist, 0)
    nf = np.maximum(n, 1).astype(np.float32)
    large = max_exact + (np.log(nf / np.float32(max_exact)) / np.float32(math.log(REL_MAX_DIST / max_exact))
                         * np.float32(REL_BUCKETS - max_exact)).astype(np.int32)
    large = np.minimum(large, REL_BUCKETS - 1)
    return np.where(n < max_exact, n, large).astype(np.int32)


def _bias_kernel(tab_ref, bkt_ref, o_ref):
    h = pl.program_id(0)
    blk = bkt_ref.shape[-1]
    key = lax.broadcasted_iota(jnp.int32, (blk, blk), 0)
    qry = lax.broadcasted_iota(jnp.int32, (blk, blk), 1)
    far = tab_ref[h, REL_BUCKETS - 1]
    for t in range(2):
        bk = bkt_ref[t]
        acc = jnp.zeros((blk, blk), F32)
        for i in range(REL_BUCKETS):
            acc = jnp.where(bk == i, tab_ref[h, i] - far, acc)
        if t == 0:
            acc = jnp.where(qry >= key, acc, NEG_INF)
        o_ref[0, t] = acc


def _bias_tiles(table_t):
    heads = table_t.shape[0]
    offs = np.arange(MOBA_BLOCK)
    dist_own = offs[None, :] - offs[:, None]
    buckets = np.stack([_t5_bucket_np(dist_own), _t5_bucket_np(dist_own + MOBA_BLOCK)])
    return pl.pallas_call(
        _bias_kernel,
        grid=(heads,),
        in_specs=[pl.BlockSpec(memory_space=pltpu.SMEM),
                  pl.BlockSpec((2, MOBA_BLOCK, MOBA_BLOCK), lambda h: (0, 0, 0))],
        out_specs=pl.BlockSpec((1, 2, MOBA_BLOCK, MOBA_BLOCK), lambda h: (h, 0, 0, 0)),
        out_shape=jax.ShapeDtypeStruct((heads, 2, MOBA_BLOCK, MOBA_BLOCK), F32),
        compiler_params=_params("parallel"),
        name="t5_bias",
    )(table_t, jnp.asarray(buckets))


def _moba_kernel(q_ref, k_ref, v_ref, bias_ref, o_ref, kmean_ref, kb_ref, vt_ref, sel_ref, st_ref, p_ref):
    qb = pl.program_id(2)
    blk = MOBA_BLOCK
    d = A_HEAD_DIM
    nh = LANES // d
    nb = k_ref.shape[1] // blk

    @pl.when(qb == 0)
    def _():
        for j in range(nb):
            kj = k_ref[0, j * blk:(j + 1) * blk, :]
            kb_ref[j] = _mx(kj)
            kmean_ref[j:j + 1, :] = jnp.mean(kj, axis=0, keepdims=True)
            vt_ref[j] = _mx(v_ref[0, j * blk:(j + 1) * blk, :].T)

    qt = (q_ref[0] * (d ** -0.5)).T
    zero = jnp.zeros((d, blk), F32)
    wq = _mx(jnp.concatenate([jnp.concatenate([qt[:d], zero], axis=1),
                              jnp.concatenate([zero, qt[d:]], axis=1)], axis=0))

    blk_id = lax.broadcasted_iota(jnp.int32, (nb, blk), 0)
    for hh in range(nh):
        gate = lax.dot_general(kmean_ref[:, hh * d:(hh + 1) * d], qt[hh * d:(hh + 1) * d],
                               (((1,), (0,)), ((), ())),
                               precision=lax.Precision.HIGHEST, preferred_element_type=F32)
        g = jnp.where(blk_id < qb, gate, -jnp.inf)
        sel = jnp.zeros((nb, blk), F32)
        for _ in range(MOBA_TOPK):
            m = jnp.max(g, axis=0, keepdims=True)
            is_m = jnp.logical_and(g == m, g > -jnp.inf)
            idx = jnp.min(jnp.where(is_m, blk_id, nb), axis=0, keepdims=True)
            pick = blk_id == idx
            sel = jnp.where(pick, 1.0, sel)
            g = jnp.where(pick, -jnp.inf, g)
        sel_ref[:, hh * blk:(hh + 1) * blk] = sel

    def score_stage(slot, ja, jb):
        st_ref[slot, 0] = jnp.dot(kb_ref[ja], wq, preferred_element_type=F32)
        st_ref[slot, 1] = jnp.dot(kb_ref[jb], wq, preferred_element_type=F32)

    def softmax_stage(slot, blocks, stats):
        rows = [None if ok is None else jnp.where(ok, sel_ref[pl.ds(j, 1), :], 0.0) for j, _, ok in blocks]
        new = []
        for hh in range(nh):
            ss = []
            for w, ((j, bias_idx, ok), selrow) in enumerate(zip(blocks, rows)):
                s = st_ref[slot, w, :, hh * blk:(hh + 1) * blk]
                if bias_idx is not None:
                    s = s + bias_ref[hh, bias_idx]
                if ok is not None:
                    s = jnp.where(selrow[:, hh * blk:(hh + 1) * blk] > 0.0, s, NEG_INF)
                ss.append(s)
            m_new = jnp.maximum(jnp.max(ss[0], axis=0, keepdims=True), jnp.max(ss[1], axis=0, keepdims=True))
            if stats is not None:
                m, l = stats[hh]
                m_new = jnp.maximum(m, m_new)
            ps = [jnp.exp(s - m_new) for s in ss]
            p_ref[slot, hh, 0:blk, :] = _mx(ps[0])
            p_ref[slot, hh, blk:2 * blk, :] = _mx(ps[1])
            l_new = jnp.sum(ps[0], axis=0, keepdims=True) + jnp.sum(ps[1], axis=0, keepdims=True)
            if stats is None:
                new.append((m_new, l_new, None))
            else:
                alpha = jnp.exp(m - m_new)
                new.append((m_new, alpha * l + l_new, alpha))
        return new

    def value_stage(slot, ja, jb):
        return [jnp.dot(jnp.concatenate([vt_ref[ja, hh * d:(hh + 1) * d, :], vt_ref[jb, hh * d:(hh + 1) * d, :]],
                                        axis=1), p_ref[slot, hh], preferred_element_type=F32)
                for hh in range(nh)]

    nfar = qb - 1
    npairs = (nfar + 1) // 2

    def far_blocks(i):
        return jnp.minimum(2 * i, nb - 1), jnp.minimum(2 * i + 1, nb - 1)

    j_prev = jnp.maximum(qb - 1, 0)
    score_stage(0, qb, j_prev)
    score_stage(1, *far_blocks(0))
    st0 = softmax_stage(0, [(qb, 0, None), (j_prev, 1, qb > 0)], None)
    init = (tuple((m, l, jnp.zeros((d, blk), F32)) for m, l, _ in st0), qb, j_prev)

    def trip(slot, k, carry):
        stats, ja_last, jb_last = carry
        score_stage(1 - slot, *far_blocks(k))
        pv = value_stage(1 - slot, ja_last, jb_last)
        ja, jb = far_blocks(k - 1)
        new = softmax_stage(slot, [(ja, None, True), (jb, None, 2 * k - 1 < nfar)],
                            [(m, l) for m, l, _ in stats])
        stats = tuple((m, l, (acc + pv_h) * alpha)
                      for (m, l, alpha), (_, _, acc), pv_h in zip(new, stats, pv))
        return stats, ja, jb

    def trip_dyn(k, carry):
        return lax.cond(k % 2 == 1, functools.partial(trip, 1, k), functools.partial(trip, 0, k), carry)

    stats, ja_last, jb_last = lax.fori_loop(1, npairs + 1, trip_dyn, init)
    pv = lax.cond(npairs % 2 == 1, lambda: value_stage(1, ja_last, jb_last),
                  lambda: value_stage(0, ja_last, jb_last))
    out_t = jnp.concatenate([(acc + pv_h) / l for (_, l, acc), pv_h in zip(stats, pv)], axis=0)
    o_ref[0] = out_t.T


def _moba(hp3d, bias):
    b, s, _ = hp3d.shape
    npair = A_WIDTH // LANES
    hpp = LANES // A_HEAD_DIM
    nb = s // MOBA_BLOCK
    return pl.pallas_call(
        _moba_kernel,
        grid=(b, npair, nb),
        in_specs=[pl.BlockSpec((1, MOBA_BLOCK, LANES), lambda bi, p, i: (bi, i, p)),
                  pl.BlockSpec((1, s, LANES), lambda bi, p, i: (bi, 0, npair + p)),
                  pl.BlockSpec((1, s, LANES), lambda bi, p, i: (bi, 0, 2 * npair + p)),
                  pl.BlockSpec((hpp, 2, MOBA_BLOCK, MOBA_BLOCK), lambda bi, p, i: (p, 0, 0, 0))],
        out_specs=pl.BlockSpec((1, MOBA_BLOCK, LANES), lambda bi, p, i: (bi, i, p)),
        out_shape=jax.ShapeDtypeStruct((b, s, A_WIDTH), F32),
        scratch_shapes=[pltpu.VMEM((nb, LANES), F32),
                        pltpu.VMEM((nb, MOBA_BLOCK, LANES), MXU_DTYPE),
                        pltpu.VMEM((nb, LANES, MOBA_BLOCK), MXU_DTYPE),
                        pltpu.VMEM((nb, 2 * MOBA_BLOCK), F32),
                        pltpu.VMEM((2, 2, MOBA_BLOCK, hpp * MOBA_BLOCK), F32),
                        pltpu.VMEM((2, hpp, 2 * MOBA_BLOCK, MOBA_BLOCK), MXU_DTYPE)],
        compiler_params=_params("parallel", "parallel", "arbitrary"),
        name="moba",
    )(hp3d, hp3d, hp3d, bias)


S5_HALF_CH = S5_WIDTH // 2
S5_HALF_ST = (S5_GROUPS // 2) * S5_STATE
S5_TILE = 256
S5_NT = S5_HALF_ST // S5_TILE
S5_STEPS = 64


def _s5_kernel(u_ref, wx_ref, wy_ref, a_ref, o_ref, st_ref, x_scr, h_scr):
    rows = u_ref.shape[0]

    @pl.when(pl.program_id(0) == 0)
    def _():
        st_ref[...] = jnp.zeros_like(st_ref)

    def lhs(n):
        c = (n // 2) * LANES
        return jnp.concatenate([u_ref[:, c:c + LANES],
                                u_ref[:, S5_HALF_CH + c:S5_HALF_CH + c + LANES]], axis=1)

    for part in range(2):
        for n in range(S5_NT):
            c0 = part * S5_HALF_ST + n * S5_TILE
            x_scr[:, c0:c0 + S5_TILE] = jnp.dot(lhs(n), wx_ref[part, n], preferred_element_type=F32)

    ar = a_ref[0]
    ai = a_ref[1]

    def step(t, carry):
        hr, hi = carry
        r0 = pl.multiple_of(t * SUBLANES, SUBLANES)
        xr = x_scr[pl.ds(r0, SUBLANES), 0:S5_HALF_ST]
        xi = x_scr[pl.ds(r0, SUBLANES), S5_HALF_ST:2 * S5_HALF_ST]
        nhr = ar * hr - ai * hi + xr
        nhi = ar * hi + ai * hr + xi
        h_scr[pl.ds(r0, SUBLANES), 0:S5_HALF_ST] = nhr
        h_scr[pl.ds(r0, SUBLANES), S5_HALF_ST:2 * S5_HALF_ST] = nhi
        return nhr, nhi

    hr, hi = lax.fori_loop(0, rows // SUBLANES, step, (st_ref[0], st_ref[1]), unroll=2)
    st_ref[0] = hr
    st_ref[1] = hi

    first_half = lax.broadcasted_iota(jnp.int32, (rows, LANES), 0) % 2 == 0
    for m in range(S5_NT // 2):
        acc = jnp.zeros((rows, 2 * LANES), F32)
        for n in (2 * m, 2 * m + 1):
            for part in range(2):
                c0 = part * S5_HALF_ST + n * S5_TILE
                acc = acc + jnp.dot(_mx(h_scr[:, c0:c0 + S5_TILE]), wy_ref[part, n],
                                    preferred_element_type=F32)
        o_ref[:, m * LANES:(m + 1) * LANES] = jnp.where(first_half, acc[:, :LANES], acc[:, LANES:])


def _s5_weights(lam_re, lam_im, b_re, b_im, c_re, c_im, log_dt):
    lr = jnp.minimum(lam_re.astype(F32), -1e-4)
    li = lam_im.astype(F32)
    dt = jnp.exp(log_dt.astype(F32))[:, None]
    mag = jnp.exp(lr * dt)
    a_re = mag * jnp.cos(li * dt)
    a_im = mag * jnp.sin(li * dt)
    den = lr * lr + li * li
    coef_re = ((a_re - 1.0) * lr + a_im * li) / den
    coef_im = (a_im * lr - (a_re - 1.0) * li) / den
    br = b_re.astype(F32)
    bi = b_im.astype(F32)
    bb_re = coef_re[..., None] * br - coef_im[..., None] * bi
    bb_im = coef_re[..., None] * bi + coef_im[..., None] * br
    gh = S5_GROUPS // 2
    eye = jnp.eye(gh, dtype=F32)

    def b_dense(bb):
        t = bb.reshape(2, gh, S5_STATE, S5_GROUP)
        return jnp.einsum('fgph,gk->fghkp', t, eye).reshape(2, S5_HALF_CH, S5_HALF_ST)

    def c_dense(cc):
        t = cc.reshape(2, gh, S5_GROUP, S5_STATE)
        return jnp.einsum('fghp,gk->fgpkh', t, eye).reshape(2, S5_HALF_ST, S5_HALF_CH)

    def x_tiles(bd):
        tiles = []
        for n in range(S5_NT):
            c = (n // 2) * LANES
            tiles.append(jnp.concatenate([bd[0, c:c + LANES, n * S5_TILE:(n + 1) * S5_TILE],
                                          bd[1, c:c + LANES, n * S5_TILE:(n + 1) * S5_TILE]], axis=0))
        return jnp.stack(tiles)

    def y_tiles(cd):
        tiles = []
        for n in range(S5_NT):
            c = (n // 2) * LANES
            tiles.append(jnp.concatenate([cd[0, n * S5_TILE:(n + 1) * S5_TILE, c:c + LANES],
                                          cd[1, n * S5_TILE:(n + 1) * S5_TILE, c:c + LANES]], axis=1))
        return jnp.stack(tiles)

    wx = jnp.stack([x_tiles(b_dense(bb_re)), x_tiles(b_dense(bb_im))]).astype(MXU_DTYPE)
    wy = jnp.stack([y_tiles(c_dense(c_re.astype(F32))), y_tiles(c_dense(-c_im.astype(F32)))]).astype(MXU_DTYPE)
    return wx, wy, a_re.reshape(2, S5_HALF_ST), a_im.reshape(2, S5_HALF_ST)


def _s5_scan(u3d, wx, wy, a_re2, a_im2):
    b, s, _ = u3d.shape
    rows_per_t = 2 * b
    assert rows_per_t == SUBLANES
    half_mask = (jnp.arange(2)[:, None] == (jnp.arange(S5_WIDTH) // S5_HALF_CH)[None, :])
    um = jnp.where(half_mask[None, None], u3d.transpose(1, 0, 2)[:, :, None, :], 0.0)
    um = um.reshape(s * rows_per_t, S5_WIDTH).astype(MXU_DTYPE)
    sel = jnp.arange(rows_per_t) % 2
    a = jnp.stack([a_re2[sel], a_im2[sel]])
    rows = S5_STEPS * rows_per_t
    y = pl.pallas_call(
        _s5_kernel,
        grid=(s // S5_STEPS,),
        in_specs=[pl.BlockSpec((rows, S5_WIDTH), lambda i: (i, 0)),
                  _resident(wx.shape), _resident(wy.shape), _resident(a.shape)],
        out_specs=pl.BlockSpec((rows, S5_HALF_CH), lambda i: (i, 0)),
        out_shape=jax.ShapeDtypeStruct((s * rows_per_t, S5_HALF_CH), F32),
        scratch_shapes=[pltpu.VMEM((2, rows_per_t, S5_HALF_ST), F32),
                        pltpu.VMEM((rows, 2 * S5_HALF_ST), F32),
                        pltpu.VMEM((rows, 2 * S5_HALF_ST), F32)],
        compiler_params=_params("arbitrary"),
        name="s5_scan",
    )(um, wx, wy, a)
    return y.reshape(s, b, S5_WIDTH).transpose(1, 0, 2)


MLP_CHUNK = 1024


def _tail_body(mix, x_ref, g1_ref, b1_ref, w1_ref, w2_ref, g2_ref, b2_ref, o_ref):
    x1 = _layer_norm(DN_ALPHA * x_ref[...] + mix, g1_ref[...], b1_ref[...])
    x1m = _mx(x1)
    acc = jnp.zeros(x1.shape, F32)
    for f in range(w1_ref.shape[1] // MLP_CHUNK):
        hid = jnp.dot(x1m, w1_ref[:, f * MLP_CHUNK:(f + 1) * MLP_CHUNK], preferred_element_type=F32)
        hid = jnp.square(jnp.maximum(hid, 0.0))
        acc = acc + jnp.dot(_mx(hid), w2_ref[f * MLP_CHUNK:(f + 1) * MLP_CHUNK, :],
                            preferred_element_type=F32)
    o_ref[...] = _layer_norm(DN_ALPHA * x1 + acc, g2_ref[...], b2_ref[...])


def _tail_even_kernel(attn_ref, y_ref, u_ref, dsk_ref, wg_ref, bg_ref, wo_ref, x_ref,
                      g1_ref, b1_ref, w1_ref, w2_ref, g2_ref, b2_ref, o_ref):
    y = y_ref[...] + dsk_ref[...] * u_ref[...]
    g = 0.5 * y * (1.0 + jnp.tanh(math.sqrt(2.0 / math.pi) * (y + 0.044715 * (y * y * y))))
    ssm = g * _sigmoid(jnp.dot(_mx(g), wg_ref[...], preferred_element_type=F32) + bg_ref[...])
    ka = attn_ref.shape[1]
    mix = (jnp.dot(_mx(attn_ref[...]), wo_ref[0:ka, :], preferred_element_type=F32)
           + jnp.dot(_mx(ssm), wo_ref[ka:, :], preferred_element_type=F32))
    _tail_body(mix, x_ref, g1_ref, b1_ref, w1_ref, w2_ref, g2_ref, b2_ref, o_ref)


def _tail_odd_kernel(a_ref, wo_ref, x_ref, g1_ref, b1_ref, w1_ref, w2_ref, g2_ref, b2_ref, o_ref):
    mix = jnp.dot(_mx(a_ref[...]), wo_ref[...], preferred_element_type=F32)
    _tail_body(mix, x_ref, g1_ref, b1_ref, w1_ref, w2_ref, g2_ref, b2_ref, o_ref)


def _row(v):
    return v.reshape(1, -1).astype(F32)


def _tail_common_specs(tm, d, dff):
    return [_resident((d, d)),
            pl.BlockSpec((tm, d), lambda i: (i, 0)),
            _resident((1, d)), _resident((1, d)),
            _resident((d, dff)), _resident((dff, d)),
            _resident((1, d)), _resident((1, d))]


def _tail_even(attn2d, y2d, hp2d, x2d, d_skip, w_glu, b_glu, w_out, g1, b1, w1, w2, g2, b2, tm):
    m, d = x2d.shape
    dff = w1.shape[1]
    ucol = hp2d.shape[1] // S5_WIDTH - 1
    return pl.pallas_call(
        _tail_even_kernel,
        grid=(m // tm,),
        in_specs=[pl.BlockSpec((tm, A_WIDTH), lambda i: (i, 0)),
                  pl.BlockSpec((tm, S5_WIDTH), lambda i: (i, 0)),
                  pl.BlockSpec((tm, S5_WIDTH), lambda i: (i, ucol)),
                  _resident((1, S5_WIDTH)), _resident((S5_WIDTH, S5_WIDTH)), _resident((1, S5_WIDTH))]
        + _tail_common_specs(tm, d, dff),
        out_specs=pl.BlockSpec((tm, d), lambda i: (i, 0)),
        out_shape=jax.ShapeDtypeStruct((m, d), F32),
        compiler_params=_params("parallel"),
        name="tail_even",
    )(attn2d, y2d, hp2d, _row(d_skip), _mx(w_glu), _row(b_glu), _mx(w_out), x2d,
      _row(g1), _row(b1), _mx(w1), _mx(w2), _row(g2), _row(b2))


def _tail_odd(a2d, x2d, w_out, g1, b1, w1, w2, g2, b2, tm):
    m, d = x2d.shape
    dff = w1.shape[1]
    return pl.pallas_call(
        _tail_odd_kernel,
        grid=(m // tm,),
        in_specs=[pl.BlockSpec((tm, a2d.shape[1]), lambda i: (i, 0))] + _tail_common_specs(tm, d, dff),
        out_specs=pl.BlockSpec((tm, d), lambda i: (i, 0)),
        out_shape=jax.ShapeDtypeStruct((m, d), F32),
        compiler_params=_params("parallel"),
        name="tail_odd",
    )(a2d, _mx(w_out), x2d, _row(g1), _row(b1), _mx(w1), _mx(w2), _row(g2), _row(b2))


GDN_PREP_TILE = 512
GDN_SCAN_TILE = 256


def _softplus(t):
    return jnp.maximum(t, 0.0) + jnp.log1p(jnp.exp(-jnp.abs(t)))


def _gdn_prep_kernel(q_ref, k_ref, v_ref, qp_ref, kp_ref, vp_ref, cwq_ref, cwk_ref, cwv_ref,
                     ba_ref, alog_ref, dtb_ref, u_ref, w_ref, qd_ref, kd_ref, intra_ref):
    i = pl.program_id(1)
    h = pl.program_id(2)
    t = q_ref.shape[1]
    c = C_CHUNK
    dk = C_HEAD_DIM

    def conv_silu(main_ref, prev_ref, cw_ref):
        x = main_ref[0]
        pv = jnp.where(i > 0, prev_ref[0], 0.0)
        ext = jnp.concatenate([pv, x], axis=0)
        y = cw_ref[C_CONV - 1:C_CONV, :] * x
        for back in range(1, C_CONV):
            shifted = pltpu.roll(ext, back, 0)[SUBLANES:, :]
            y = y + cw_ref[C_CONV - 1 - back:C_CONV - back, :] * shifted
        return y * _sigmoid(y)

    def l2n(z):
        return z * lax.rsqrt(jnp.sum(z * z, axis=-1, keepdims=True) + L2_EPS)

    q = l2n(conv_silu(q_ref, qp_ref, cwq_ref)) * (dk ** -0.5)
    k = l2n(conv_silu(k_ref, kp_ref, cwk_ref))
    v = conv_silu(v_ref, vp_ref, cwv_ref)

    ba = ba_ref[0]
    lane = lax.broadcasted_iota(jnp.int32, (t, LANES), 1)
    beta = _sigmoid(jnp.sum(jnp.where(lane == h, ba, 0.0), axis=1, keepdims=True))
    g_all = -jnp.exp(alog_ref[...]) * _softplus(ba + dtb_ref[...])
    g_col = jnp.sum(jnp.where(lane == C_HEADS + h, g_all, 0.0), axis=1, keepdims=True)
    sub = lax.broadcasted_iota(jnp.int32, (LANES, t), 0)
    g_row = jnp.sum(jnp.where(sub == C_HEADS + h, g_all.T, 0.0), axis=0, keepdims=True)

    ii = lax.broadcasted_iota(jnp.int32, (c, c), 0)
    jj = lax.broadcasted_iota(jnp.int32, (c, c), 1)
    low = ii >= jj
    chunks = [slice(ch * c, (ch + 1) * c) for ch in range(t // c)]
    gc_cols = [jnp.sum(jnp.where(low, g_row[:, r], 0.0), axis=1, keepdims=True) for r in chunks]
    gc_rows = [jnp.sum(jnp.where(ii <= jj, g_col[r], 0.0), axis=0, keepdims=True) for r in chunks]
    decays = [jnp.where(low, jnp.exp(jnp.where(low, a - b, 0.0)), 0.0) for a, b in zip(gc_cols, gc_rows)]
    kbs = [k[r] * beta[r] for r in chunks]
    kks = [_dot_nt(kb, k[r]) for kb, r in zip(kbs, chunks)]
    qks = [_dot_nt(q[r], k[r]) for r in chunks]
    pws = [jnp.where(ii > jj, -(kk * dec), 0.0) for kk, dec in zip(kks, decays)]
    tms = pws
    for _ in range(int(math.log2(c)) - 1):
        pws = [_dot(pw, pw) for pw in pws]
        tms = [tm + pw + _dot(tm, pw) for tm, pw in zip(tms, pws)]
    egcs = [jnp.exp(a) for a in gc_cols]
    vbs = [v[r] * beta[r] for r in chunks]
    kbes = [kb * e for kb, e in zip(kbs, egcs)]
    us = [vb + _dot(tm, vb) for tm, vb in zip(tms, vbs)]
    ws = [kbe + _dot(tm, kbe) for tm, kbe in zip(tms, kbes)]
    for n, r in enumerate(chunks):
        gl = gc_cols[n][c - 1:c, :]
        u_ref[0, r, :] = us[n]
        w_ref[0, r, :] = ws[n]
        qd_ref[0, r, :] = q[r] * egcs[n]
        kd_ref[0, r, :] = k[r] * jnp.exp(gl - gc_cols[n])
        intra_ref[0, 0, r, 0:c] = qks[n] * decays[n]
        intra_ref[0, 0, r, c:2 * c] = jnp.broadcast_to(jnp.exp(gl), (c, c))


def _gdn_prep(hp3d, conv_w, alog_row, dtb_row):
    b, s, _ = hp3d.shape
    ba_col = 4 * C_WIDTH // LANES
    t = GDN_PREP_TILE
    hds = C_HEADS
    per = t // SUBLANES

    def main(off):
        return pl.BlockSpec((1, t, LANES), lambda bi, i, h: (bi, i, off + h))

    def prev(off):
        return pl.BlockSpec((1, SUBLANES, LANES),
                            lambda bi, i, h: (bi, jnp.maximum(i * per - 1, 0), off + h))

    def cw(off):
        return pl.BlockSpec((C_CONV, LANES), lambda bi, i, h: (0, off + h))

    out_main = pl.BlockSpec((1, t, LANES), lambda bi, i, h: (bi, i, h))
    full = jax.ShapeDtypeStruct((b, s, C_WIDTH), F32)
    return pl.pallas_call(
        _gdn_prep_kernel,
        grid=(b, s // t, hds),
        in_specs=[main(0), main(hds), main(2 * hds), prev(0), prev(hds), prev(2 * hds),
                  cw(0), cw(hds), cw(2 * hds),
                  pl.BlockSpec((1, t, LANES), lambda bi, i, h: (bi, i, ba_col)),
                  pl.BlockSpec((1, LANES), lambda bi, i, h: (0, 0)),
                  pl.BlockSpec((1, LANES), lambda bi, i, h: (0, 0))],
        out_specs=[out_main, out_main, out_main, out_main,
                   pl.BlockSpec((1, 1, t, LANES), lambda bi, i, h: (bi, h, i, 0))],
        out_shape=[full, full, full, full, jax.ShapeDtypeStruct((b, hds, s, LANES), F32)],
        compiler_params=_params("parallel", "parallel", "parallel"),
        name="gdn_prep",
    )(hp3d, hp3d, hp3d, hp3d, hp3d, hp3d, conv_w, conv_w, conv_w, hp3d, alog_row, dtb_row)


def _gdn_scan_kernel(u_ref, w_ref, qd_ref, kd_ref, intra_ref, gate_ref, nw_ref, o_ref, st_ref):
    c = C_CHUNK
    dk = C_HEAD_DIM
    t = u_ref.shape[1]

    @pl.when(pl.program_id(1) == 0)
    def _():
        st_ref[...] = jnp.zeros_like(st_ref)

    nw = nw_ref[...]
    heads = [slice(h * dk, (h + 1) * dk) for h in range(C_HEADS)]
    for ch in range(t // c):
        r = slice(ch * c, (ch + 1) * c)
        sts = [st_ref[h] for h in range(C_HEADS)]
        wss = [_dot(jnp.concatenate([w_ref[0, r, cs], qd_ref[0, r, cs]], axis=0), st)
               for cs, st in zip(heads, sts)]
        v_news = [u_ref[0, r, cs] - ws[:c] for cs, ws in zip(heads, wss)]
        kvs = [_dot(kd_ref[0, r, cs].T, v_new) for cs, v_new in zip(heads, v_news)]
        os_ = [ws[c:] + _dot(intra_ref[0, h, r, 0:c], v_new) for h, (ws, v_new) in enumerate(zip(wss, v_news))]
        for h, cs in enumerate(heads):
            st_ref[h] = sts[h] * intra_ref[0, h, ch * c:ch * c + 1, c:c + 1] + kvs[h]
            o = os_[h]
            o = o * lax.rsqrt(jnp.mean(o * o, axis=-1, keepdims=True) + RMS_EPS) * nw
            gt = gate_ref[0, r, cs]
            o_ref[0, r, cs] = o * (gt * _sigmoid(gt))


def _gdn_scan(u, w, qd, kd, intra, hp3d, norm_w):
    b, s, cw = u.shape
    t = GDN_SCAN_TILE
    blk = pl.BlockSpec((1, t, cw), lambda bi, i: (bi, i, 0))
    return pl.pallas_call(
        _gdn_scan_kernel,
        grid=(b, s // t),
        in_specs=[blk, blk, blk, blk,
                  pl.BlockSpec((1, C_HEADS, t, LANES), lambda bi, i: (bi, 0, i, 0)),
                  pl.BlockSpec((1, t, cw), lambda bi, i: (bi, i, 3)),
                  pl.BlockSpec((1, C_HEAD_DIM), lambda bi, i: (0, 0))],
        out_specs=blk,
        out_shape=jax.ShapeDtypeStruct((b, s, cw), F32),
        scratch_shapes=[pltpu.VMEM((C_HEADS, C_HEAD_DIM, C_HEAD_DIM), F32)],
        compiler_params=_params("parallel", "arbitrary"),
        name="gdn_scan",
    )(u, w, qd, kd, intra, hp3d, _row(norm_w))


ROW_TILE = 512


def kernel(x, rel_table, ev_w_in, ev_w_out, s5_lambda_re, s5_lambda_im, s5_b_re, s5_b_im, s5_c_re, s5_c_im, s5_d, s5_log_dt, s5_w_glu, s5_b_glu, od_w_in, od_conv_w, od_a_log, od_dt_bias, od_norm_w, od_w_out, ln_mix_g, ln_mix_b, mlp_w1, mlp_w2, ln_ffn_g, ln_ffn_b):
    bsz, s, d = x.shape
    m = bsz * s
    x2d = x.reshape(m, d)
    table_t = rel_table.T.astype(F32)
    bias = _bias_tiles(table_t)
    for layer in range(DEPTH):
        i = layer // 2
        if layer % 2 == 0:
            hp = _proj(x2d, _mx(ev_w_in[i]), ROW_TILE)
            hp3d = hp.reshape(bsz, s, hp.shape[1])
            attn = _moba(hp3d, bias)
            wx, wy, a_re2, a_im2 = _s5_weights(s5_lambda_re[i], s5_lambda_im[i], s5_b_re[i], s5_b_im[i],
                                               s5_c_re[i], s5_c_im[i], s5_log_dt[i])
            y = _s5_scan(hp3d[..., 3 * A_WIDTH:], wx, wy, a_re2, a_im2)
            x2d = _tail_even(attn.reshape(m, A_WIDTH), y.reshape(m, S5_WIDTH), hp, x2d,
                             s5_d[i], s5_w_glu[i], s5_b_glu[i], ev_w_out[i],
                             ln_mix_g[layer], ln_mix_b[layer], mlp_w1[layer], mlp_w2[layer],
                             ln_ffn_g[layer], ln_ffn_b[layer], ROW_TILE)
        else:
            w_in = od_w_in[i]
            w_all = jnp.pad(w_in, ((0, 0), (0, LANES - 2 * C_HEADS)))
            hp = _proj(x2d, _mx(w_all), ROW_TILE)
            hp3d = hp.reshape(bsz, s, 4 * C_WIDTH + LANES)
            pad8 = jnp.zeros((C_HEADS,), F32)
            padr = jnp.zeros((LANES - 2 * C_HEADS,), F32)
            alog_row = jnp.concatenate([pad8, od_a_log[i].astype(F32), padr]).reshape(1, LANES)
            dtb_row = jnp.concatenate([pad8, od_dt_bias[i].astype(F32), padr]).reshape(1, LANES)
            u, w, qd, kd, intra = _gdn_prep(hp3d, od_conv_w[i].astype(F32), alog_row, dtb_row)
            o = _gdn_scan(u, w, qd, kd, intra, hp3d, od_norm_w[i])
            x2d = _tail_odd(o.reshape(m, C_WIDTH), x2d, od_w_out[i],
                            ln_mix_g[layer], ln_mix_b[layer], mlp_w1[layer], mlp_w2[layer],
                            ln_ffn_g[layer], ln_ffn_b[layer], ROW_TILE)
    return x2d.reshape(bsz, s, d)
```

```python
import functools
import math

import jax
import jax.numpy as jnp
import numpy as np
from jax import lax
from jax.experimental import pallas as pl
from jax.experimental.pallas import tpu as pltpu

F32 = jnp.float32
MXU_DTYPE = jnp.bfloat16

LANES = 128
SUBLANES = 8
VMEM_LIMIT_BYTES = 56 * 1024 * 1024

A_HEADS = 8
A_HEAD_DIM = 64
A_WIDTH = A_HEADS * A_HEAD_DIM
MOBA_BLOCK = 256
MOBA_TOPK = 3
REL_BUCKETS = 32
REL_MAX_DIST = 128
S5_GROUP = 16
S5_GROUPS = 32
S5_STATE = 64
S5_WIDTH = S5_GROUP * S5_GROUPS
C_HEADS = 8
C_HEAD_DIM = 128
C_WIDTH = C_HEADS * C_HEAD_DIM
C_CONV = 4
C_CHUNK = 64
DEPTH = 2
DN_ALPHA = (2 * DEPTH) ** 0.25
LN_EPS = 1e-5
RMS_EPS = 1e-6
L2_EPS = 1e-6
NEG_INF = -1e30
LOG2E = math.log2(math.e)


def _mx(t):
    return t.astype(MXU_DTYPE)


def _dot(a, b):
    return jnp.dot(_mx(a), _mx(b), preferred_element_type=F32)


def _dot_nt(a, b):
    return lax.dot_general(_mx(a), _mx(b), (((1,), (1,)), ((), ())), preferred_element_type=F32)


def _sigmoid(t):
    return 1.0 / (1.0 + jnp.exp(-t))


def _layer_norm(r, g, b):
    mu = jnp.mean(r, axis=-1, keepdims=True)
    d = r - mu
    var = jnp.mean(d * d, axis=-1, keepdims=True)
    return d * lax.rsqrt(var + LN_EPS) * g + b


def _params(*sem):
    return pltpu.CompilerParams(dimension_semantics=sem, vmem_limit_bytes=VMEM_LIMIT_BYTES)


def _resident(shape):
    return pl.BlockSpec(shape, lambda *_: (0,) * len(shape), pipeline_mode=pl.Buffered(1))


PROJ_CHUNK = 1024


def _proj_kernel(x_ref, w_ref, o_ref):
    xm = _mx(x_ref[...])
    n = w_ref.shape[1]
    for c0 in range(0, n, PROJ_CHUNK):
        c1 = min(c0 + PROJ_CHUNK, n)
        o_ref[:, c0:c1] = jnp.dot(xm, w_ref[:, c0:c1], preferred_element_type=F32)


def _proj(x2d, w, tm):
    m, k = x2d.shape
    n = w.shape[1]
    return pl.pallas_call(
        _proj_kernel,
        grid=(m // tm,),
        in_specs=[pl.BlockSpec((tm, k), lambda i: (i, 0)), _resident((k, n))],
        out_specs=pl.BlockSpec((tm, n), lambda i: (i, 0)),
        out_shape=jax.ShapeDtypeStruct((m, n), F32),
        compiler_params=_params("parallel"),
        name="in_proj",
    )(x2d, w)


def _t5_bucket_np(dist):
    max_exact = REL_BUCKETS // 2
    n = np.maximum(dist, 0)
    nf = np.maximum(n, 1).astype(np.float32)
    large = max_exact + (np.log(nf / np.float32(max_exact)) / np.float32(math.log(REL_MAX_DIST / max_exact))
                         * np.float32(REL_BUCKETS - max_exact)).astype(np.int32)
    large = np.minimum(large, REL_BUCKETS - 1)
    return np.where(n < max_exact, n, large).astype(np.int32)


def _bias_kernel(tab_ref, bkt_ref, o_ref):
    h = pl.program_id(0)
    blk = bkt_ref.shape[-1]
    key = lax.broadcasted_iota(jnp.int32, (blk, blk), 0)
    qry = lax.broadcasted_iota(jnp.int32, (blk, blk), 1)
    far = tab_ref[h, REL_BUCKETS - 1]
    for t in range(2):
        bk = bkt_ref[t]
        acc = jnp.zeros((blk, blk), F32)
        for i in range(REL_BUCKETS):
            acc = jnp.where(bk == i, (tab_ref[h, i] - far) * LOG2E, acc)
        if t == 0:
            acc = jnp.where(qry >= key, acc, NEG_INF)
        o_ref[0, t] = acc


def _bias_tiles(table_t):
    heads = table_t.shape[0]
    offs = np.arange(MOBA_BLOCK)
    dist_own = offs[None, :] - offs[:, None]
    buckets = np.stack([_t5_bucket_np(dist_own), _t5_bucket_np(dist_own + MOBA_BLOCK)])
    return pl.pallas_call(
        _bias_kernel,
        grid=(heads,),
        in_specs=[pl.BlockSpec(memory_space=pltpu.SMEM),
                  pl.BlockSpec((2, MOBA_BLOCK, MOBA_BLOCK), lambda h: (0, 0, 0))],
        out_specs=pl.BlockSpec((1, 2, MOBA_BLOCK, MOBA_BLOCK), lambda h: (h, 0, 0, 0)),
        out_shape=jax.ShapeDtypeStruct((heads, 2, MOBA_BLOCK, MOBA_BLOCK), F32),
        compiler_params=_params("parallel"),
        name="t5_bias",
    )(table_t, jnp.asarray(buckets))


def _moba_kernel(q_ref, k_ref, v_ref, bias_ref, o_ref, kmean_ref, kb_ref, vt_ref,
                 st0_ref, st1_ref, p0_ref, p1_ref, wq_ref, acc_ref):
    st_refs = (st0_ref, st1_ref)
    p_refs = (p0_ref, p1_ref)
    qb = pl.program_id(2)
    blk = MOBA_BLOCK
    d = A_HEAD_DIM
    nh = LANES // d
    nb = k_ref.shape[1] // blk

    @pl.when(qb == 0)
    def _():
        lane = lax.broadcasted_iota(jnp.int32, (blk, LANES), 1)
        for j in range(nb + 1):
            kb_ref[j, :, LANES:2 * LANES] = jnp.where(lane == j, 1.0, 0.0).astype(MXU_DTYPE)
        for j in range(nb):
            kj = k_ref[0, j * blk:(j + 1) * blk, :]
            kb_ref[j, :, 0:LANES] = _mx(kj)
            kmean_ref[j:j + 1, :] = jnp.mean(kj, axis=0, keepdims=True)
            vt_ref[j] = _mx(v_ref[0, j * blk:(j + 1) * blk, :].T)
        kb_ref[nb, :, 0:LANES] = jnp.zeros((blk, LANES), MXU_DTYPE)
        vt_ref[nb] = jnp.zeros((LANES, blk), MXU_DTYPE)

    qt = (q_ref[0] * (d ** -0.5)).T
    zero = jnp.zeros((d, blk), F32)
    q_rows = jnp.concatenate([jnp.concatenate([qt[:d], zero], axis=1),
                              jnp.concatenate([zero, qt[d:]], axis=1)], axis=0) * LOG2E

    blk_id = lax.broadcasted_iota(jnp.int32, (nb, blk), 0)
    mask_rows = []
    for hh in range(nh):
        gate = lax.dot_general(kmean_ref[:, hh * d:(hh + 1) * d], qt[hh * d:(hh + 1) * d],
                               (((1,), (0,)), ((), ())),
                               precision=lax.Precision.HIGHEST, preferred_element_type=F32)
        g = jnp.where(blk_id < qb, gate, -jnp.inf)
        sel = jnp.zeros((nb, blk), F32)
        for _ in range(MOBA_TOPK):
            m = jnp.max(g, axis=0, keepdims=True)
            is_m = jnp.logical_and(g == m, g > -jnp.inf)
            idx = jnp.min(jnp.where(is_m, blk_id, nb), axis=0, keepdims=True)
            pick = blk_id == idx
            sel = jnp.where(pick, 1.0, sel)
            g = jnp.where(pick, -jnp.inf, g)
        keep = jnp.logical_or(sel > 0.0, blk_id == qb)
        mask_rows.append(jnp.where(keep, 0.0, NEG_INF))

    row = lax.broadcasted_iota(jnp.int32, (LANES - nb, nh * blk), 0)
    wq_ref[...] = _mx(jnp.concatenate([q_rows, jnp.concatenate(mask_rows, axis=1),
                                       jnp.where(row == 0, NEG_INF, 0.0)], axis=0))

    def score_stage(slot, ja, jb, with_bias):
        keys = jnp.concatenate([kb_ref[ja], kb_ref[jb]], axis=0)
        st = jnp.dot(keys, wq_ref[...], preferred_element_type=F32)
        if with_bias:
            st = st + jnp.concatenate([jnp.concatenate([bias_ref[hh, w] for hh in range(nh)], axis=1)
                                       for w in range(2)], axis=0)
        st_refs[slot][0] = st[:blk]
        st_refs[slot][1] = st[blk:]
        return jnp.max(st, axis=0, keepdims=True)

    def softmax_stage(slot, cmax, stats):
        new = []
        for hh in range(nh):
            cs = slice(hh * blk, (hh + 1) * blk)
            m_new = cmax[:, cs]
            if stats is not None:
                m, l = stats[hh]
                m_new = jnp.maximum(m, m_new)
            ps = [jnp.exp2(st_refs[slot][w, :, cs] - m_new) for w in range(2)]
            p_refs[slot][hh, 0:blk, :] = _mx(ps[0])
            p_refs[slot][hh, blk:2 * blk, :] = _mx(ps[1])
            l_new = jnp.sum(ps[0], axis=0, keepdims=True) + jnp.sum(ps[1], axis=0, keepdims=True)
            if stats is None:
                new.append((m_new, l_new, None))
            else:
                alpha = jnp.exp2(m - m_new)
                new.append((m_new, alpha * l + l_new, alpha))
        return new

    def value_stage(slot, ja, jb):
        return [jnp.dot(jnp.concatenate([vt_ref[ja, hh * d:(hh + 1) * d, :], vt_ref[jb, hh * d:(hh + 1) * d, :]],
                                        axis=1), p_refs[slot][hh], preferred_element_type=F32)
                for hh in range(nh)]

    nfar = qb - 1
    npairs = (nfar + 1) // 2

    def far_blocks(i):
        return jnp.where(2 * i < nfar, 2 * i, nb), jnp.where(2 * i + 1 < nfar, 2 * i + 1, nb)

    j_prev = jnp.where(qb > 0, qb - 1, nb)
    cmax0 = score_stage(0, qb, j_prev, True)
    cmax1 = score_stage(1, *far_blocks(0), False)
    st0 = softmax_stage(0, cmax0, None)
    acc_ref[...] = jnp.zeros_like(acc_ref)
    init = (tuple((m, l) for m, l, _ in st0), cmax1, qb, j_prev)

    def trip(slot, k, carry):
        stats, cmax, ja_last, jb_last = carry
        cmax_next = score_stage(1 - slot, *far_blocks(k), False)
        pv = value_stage(1 - slot, ja_last, jb_last)
        new = softmax_stage(slot, cmax, stats)
        for hh, ((_, _, alpha), pv_h) in enumerate(zip(new, pv)):
            acc_ref[hh] = (acc_ref[hh] + pv_h) * alpha
        return (tuple((m, l) for m, l, _ in new), cmax_next) + far_blocks(k - 1)

    def trip_dyn(k, carry):
        return lax.cond(k % 2 == 1, functools.partial(trip, 1, k), functools.partial(trip, 0, k), carry)

    stats, _, ja_last, jb_last = lax.fori_loop(1, npairs + 1, trip_dyn, init)
    pv = lax.cond(npairs % 2 == 1, lambda: value_stage(1, ja_last, jb_last),
                  lambda: value_stage(0, ja_last, jb_last))
    out_t = jnp.concatenate([(acc_ref[hh] + pv[hh]) / stats[hh][1] for hh in range(nh)], axis=0)
    o_ref[0] = out_t.T


def _moba(hp3d, bias):
    b, s, _ = hp3d.shape
    npair = A_WIDTH // LANES
    hpp = LANES // A_HEAD_DIM
    nb = s // MOBA_BLOCK
    return pl.pallas_call(
        _moba_kernel,
        grid=(b, npair, nb),
        in_specs=[pl.BlockSpec((1, MOBA_BLOCK, LANES), lambda bi, p, i: (bi, i, p)),
                  pl.BlockSpec((1, s, LANES), lambda bi, p, i: (bi, 0, npair + p)),
                  pl.BlockSpec((1, s, LANES), lambda bi, p, i: (bi, 0, 2 * npair + p)),
                  pl.BlockSpec((hpp, 2, MOBA_BLOCK, MOBA_BLOCK), lambda bi, p, i: (p, 0, 0, 0))],
        out_specs=pl.BlockSpec((1, MOBA_BLOCK, LANES), lambda bi, p, i: (bi, i, p)),
        out_shape=jax.ShapeDtypeStruct((b, s, A_WIDTH), F32),
        scratch_shapes=[pltpu.VMEM((nb, LANES), F32),
                        pltpu.VMEM((nb + 1, MOBA_BLOCK, 2 * LANES), MXU_DTYPE),
                        pltpu.VMEM((nb + 1, LANES, MOBA_BLOCK), MXU_DTYPE),
                        pltpu.VMEM((2, MOBA_BLOCK, hpp * MOBA_BLOCK), F32),
                        pltpu.VMEM((2, MOBA_BLOCK, hpp * MOBA_BLOCK), F32),
                        pltpu.VMEM((hpp, 2 * MOBA_BLOCK, MOBA_BLOCK), MXU_DTYPE),
                        pltpu.VMEM((hpp, 2 * MOBA_BLOCK, MOBA_BLOCK), MXU_DTYPE),
                        pltpu.VMEM((2 * LANES, hpp * MOBA_BLOCK), MXU_DTYPE),
                        pltpu.VMEM((hpp, A_HEAD_DIM, MOBA_BLOCK), F32)],
        compiler_params=_params("parallel", "parallel", "arbitrary"),
        name="moba",
    )(hp3d, hp3d, hp3d, bias)


S5_HALF_CH = S5_WIDTH // 2
S5_HALF_ST = (S5_GROUPS // 2) * S5_STATE
S5_TILE = 256
S5_NT = S5_HALF_ST // S5_TILE
S5_STEPS = 64


def _s5_kernel(u_ref, wx_ref, wy_ref, a_ref, o_ref, st_ref, x_scr, h_scr):
    rows = u_ref.shape[0]

    @pl.when(pl.program_id(0) == 0)
    def _():
        st_ref[...] = jnp.zeros_like(st_ref)

    def lhs(n):
        c = (n // 2) * LANES
        return jnp.concatenate([u_ref[:, c:c + LANES],
                                u_ref[:, S5_HALF_CH + c:S5_HALF_CH + c + LANES]], axis=1)

    for part in range(2):
        for n in range(S5_NT):
            c0 = part * S5_HALF_ST + n * S5_TILE
            x_scr[:, c0:c0 + S5_TILE] = jnp.dot(lhs(n), wx_ref[part, n], preferred_element_type=F32)

    ar = a_ref[0]
    ai = a_ref[1]

    def step(t, carry):
        hr, hi = carry
        r0 = pl.multiple_of(t * SUBLANES, SUBLANES)
        xr = x_scr[pl.ds(r0, SUBLANES), 0:S5_HALF_ST]
        xi = x_scr[pl.ds(r0, SUBLANES), S5_HALF_ST:2 * S5_HALF_ST]
        nhr = ar * hr - ai * hi + xr
        nhi = ar * hi + ai * hr + xi
        h_scr[pl.ds(r0, SUBLANES), 0:S5_HALF_ST] = nhr
        h_scr[pl.ds(r0, SUBLANES), S5_HALF_ST:2 * S5_HALF_ST] = nhi
        return nhr, nhi

    hr, hi = lax.fori_loop(0, rows // SUBLANES, step, (st_ref[0], st_ref[1]), unroll=2)
    st_ref[0] = hr
    st_ref[1] = hi

    first_half = lax.broadcasted_iota(jnp.int32, (rows, LANES), 0) % 2 == 0
    for m in range(S5_NT // 2):
        acc = jnp.zeros((rows, 2 * LANES), F32)
        for n in (2 * m, 2 * m + 1):
            for part in range(2):
                c0 = part * S5_HALF_ST + n * S5_TILE
                acc = acc + jnp.dot(_mx(h_scr[:, c0:c0 + S5_TILE]), wy_ref[part, n],
                                    preferred_element_type=F32)
        o_ref[:, m * LANES:(m + 1) * LANES] = jnp.where(first_half, acc[:, :LANES], acc[:, LANES:])


def _s5_weights(lam_re, lam_im, b_re, b_im, c_re, c_im, log_dt):
    lr = jnp.minimum(lam_re.astype(F32), -1e-4)
    li = lam_im.astype(F32)
    dt = jnp.exp(log_dt.astype(F32))[:, None]
    mag = jnp.exp(lr * dt)
    a_re = mag * jnp.cos(li * dt)
    a_im = mag * jnp.sin(li * dt)
    den = lr * lr + li * li
    coef_re = ((a_re - 1.0) * lr + a_im * li) / den
    coef_im = (a_im * lr - (a_re - 1.0) * li) / den
    br = b_re.astype(F32)
    bi = b_im.astype(F32)
    bb_re = coef_re[..., None] * br - coef_im[..., None] * bi
    bb_im = coef_re[..., None] * bi + coef_im[..., None] * br
    gh = S5_GROUPS // 2
    eye = jnp.eye(gh, dtype=F32)

    def b_dense(bb):
        t = bb.reshape(2, gh, S5_STATE, S5_GROUP)
        return jnp.einsum('fgph,gk->fghkp', t, eye).reshape(2, S5_HALF_CH, S5_HALF_ST)

    def c_dense(cc):
        t = cc.reshape(2, gh, S5_GROUP, S5_STATE)
        return jnp.einsum('fghp,gk->fgpkh', t, eye).reshape(2, S5_HALF_ST, S5_HALF_CH)

    def x_tiles(bd):
        tiles = []
        for n in range(S5_NT):
            c = (n // 2) * LANES
            tiles.append(jnp.concatenate([bd[0, c:c + LANES, n * S5_TILE:(n + 1) * S5_TILE],
                                          bd[1, c:c + LANES, n * S5_TILE:(n + 1) * S5_TILE]], axis=0))
        return jnp.stack(tiles)

    def y_tiles(cd):
        tiles = []
        for n in range(S5_NT):
            c = (n // 2) * LANES
            tiles.append(jnp.concatenate([cd[0, n * S5_TILE:(n + 1) * S5_TILE, c:c + LANES],
                                          cd[1, n * S5_TILE:(n + 1) * S5_TILE, c:c + LANES]], axis=1))
        return jnp.stack(tiles)

    wx = jnp.stack([x_tiles(b_dense(bb_re)), x_tiles(b_dense(bb_im))]).astype(MXU_DTYPE)
    wy = jnp.stack([y_tiles(c_dense(c_re.astype(F32))), y_tiles(c_dense(-c_im.astype(F32)))]).astype(MXU_DTYPE)
    return wx, wy, a_re.reshape(2, S5_HALF_ST), a_im.reshape(2, S5_HALF_ST)


def _s5_scan(u3d, wx, wy, a_re2, a_im2):
    b, s, _ = u3d.shape
    rows_per_t = 2 * b
    assert rows_per_t == SUBLANES
    half_mask = (jnp.arange(2)[:, None] == (jnp.arange(S5_WIDTH) // S5_HALF_CH)[None, :])
    um = jnp.where(half_mask[None, None], u3d.transpose(1, 0, 2)[:, :, None, :], 0.0)
    um = um.reshape(s * rows_per_t, S5_WIDTH).astype(MXU_DTYPE)
    sel = jnp.arange(rows_per_t) % 2
    a = jnp.stack([a_re2[sel], a_im2[sel]])
    rows = S5_STEPS * rows_per_t
    y = pl.pallas_call(
        _s5_kernel,
        grid=(s // S5_STEPS,),
        in_specs=[pl.BlockSpec((rows, S5_WIDTH), lambda i: (i, 0)),
                  _resident(wx.shape), _resident(wy.shape), _resident(a.shape)],
        out_specs=pl.BlockSpec((rows, S5_HALF_CH), lambda i: (i, 0)),
        out_shape=jax.ShapeDtypeStruct((s * rows_per_t, S5_HALF_CH), F32),
        scratch_shapes=[pltpu.VMEM((2, rows_per_t, S5_HALF_ST), F32),
                        pltpu.VMEM((rows, 2 * S5_HALF_ST), F32),
                        pltpu.VMEM((rows, 2 * S5_HALF_ST), F32)],
        compiler_params=_params("arbitrary"),
        name="s5_scan",
    )(um, wx, wy, a)
    return y.reshape(s, b, S5_WIDTH).transpose(1, 0, 2)


MLP_CHUNK = 1024


def _tail_body(mix, x_ref, g1_ref, b1_ref, w1_ref, w2_ref, g2_ref, b2_ref, o_ref):
    x1 = _layer_norm(DN_ALPHA * x_ref[...] + mix, g1_ref[...], b1_ref[...])
    x1m = _mx(x1)
    acc = jnp.zeros(x1.shape, F32)
    for f in range(w1_ref.shape[1] // MLP_CHUNK):
        hid = jnp.dot(x1m, w1_ref[:, f * MLP_CHUNK:(f + 1) * MLP_CHUNK], preferred_element_type=F32)
        hid = jnp.square(jnp.maximum(hid, 0.0))
        acc = acc + jnp.dot(_mx(hid), w2_ref[f * MLP_CHUNK:(f + 1) * MLP_CHUNK, :],
                            preferred_element_type=F32)
    o_ref[...] = _layer_norm(DN_ALPHA * x1 + acc, g2_ref[...], b2_ref[...])


def _tail_even_kernel(attn_ref, y_ref, u_ref, dsk_ref, wg_ref, bg_ref, wo_ref, x_ref,
                      g1_ref, b1_ref, w1_ref, w2_ref, g2_ref, b2_ref, o_ref):
    y = y_ref[...] + dsk_ref[...] * u_ref[...]
    g = 0.5 * y * (1.0 + jnp.tanh(math.sqrt(2.0 / math.pi) * (y + 0.044715 * (y * y * y))))
    ssm = g * _sigmoid(jnp.dot(_mx(g), wg_ref[...], preferred_element_type=F32) + bg_ref[...])
    ka = attn_ref.shape[1]
    mix = (jnp.dot(_mx(attn_ref[...]), wo_ref[0:ka, :], preferred_element_type=F32)
           + jnp.dot(_mx(ssm), wo_ref[ka:, :], preferred_element_type=F32))
    _tail_body(mix, x_ref, g1_ref, b1_ref, w1_ref, w2_ref, g2_ref, b2_ref, o_ref)


def _tail_odd_kernel(a_ref, wo_ref, x_ref, g1_ref, b1_ref, w1_ref, w2_ref, g2_ref, b2_ref, o_ref):
    mix = jnp.dot(_mx(a_ref[...]), wo_ref[...], preferred_element_type=F32)
    _tail_body(mix, x_ref, g1_ref, b1_ref, w1_ref, w2_ref, g2_ref, b2_ref, o_ref)


def _row(v):
    return v.reshape(1, -1).astype(F32)


def _tail_common_specs(tm, d, dff):
    return [_resident((d, d)),
            pl.BlockSpec((tm, d), lambda i: (i, 0)),
            _resident((1, d)), _resident((1, d)),
            _resident((d, dff)), _resident((dff, d)),
            _resident((1, d)), _resident((1, d))]


def _tail_even(attn2d, y2d, hp2d, x2d, d_skip, w_glu, b_glu, w_out, g1, b1, w1, w2, g2, b2, tm):
    m, d = x2d.shape
    dff = w1.shape[1]
    ucol = hp2d.shape[1] // S5_WIDTH - 1
    return pl.pallas_call(
        _tail_even_kernel,
        grid=(m // tm,),
        in_specs=[pl.BlockSpec((tm, A_WIDTH), lambda i: (i, 0)),
                  pl.BlockSpec((tm, S5_WIDTH), lambda i: (i, 0)),
                  pl.BlockSpec((tm, S5_WIDTH), lambda i: (i, ucol)),
                  _resident((1, S5_WIDTH)), _resident((S5_WIDTH, S5_WIDTH)), _resident((1, S5_WIDTH))]
        + _tail_common_specs(tm, d, dff),
        out_specs=pl.BlockSpec((tm, d), lambda i: (i, 0)),
        out_shape=jax.ShapeDtypeStruct((m, d), F32),
        compiler_params=_params("parallel"),
        name="tail_even",
    )(attn2d, y2d, hp2d, _row(d_skip), _mx(w_glu), _row(b_glu), _mx(w_out), x2d,
      _row(g1), _row(b1), _mx(w1), _mx(w2), _row(g2), _row(b2))


def _tail_odd(a2d, x2d, w_out, g1, b1, w1, w2, g2, b2, tm):
    m, d = x2d.shape
    dff = w1.shape[1]
    return pl.pallas_call(
        _tail_odd_kernel,
        grid=(m // tm,),
        in_specs=[pl.BlockSpec((tm, a2d.shape[1]), lambda i: (i, 0))] + _tail_common_specs(tm, d, dff),
        out_specs=pl.BlockSpec((tm, d), lambda i: (i, 0)),
        out_shape=jax.ShapeDtypeStruct((m, d), F32),
        compiler_params=_params("parallel"),
        name="tail_odd",
    )(a2d, _mx(w_out), x2d, _row(g1), _row(b1), _mx(w1), _mx(w2), _row(g2), _row(b2))


GDN_PREP_TILE = 512
GDN_SCAN_TILE = 256


def _softplus(t):
    return jnp.maximum(t, 0.0) + jnp.log1p(jnp.exp(-jnp.abs(t)))


def _gdn_prep_kernel(q_ref, k_ref, v_ref, qp_ref, kp_ref, vp_ref, cwq_ref, cwk_ref, cwv_ref,
                     ba_ref, alog_ref, dtb_ref, u_ref, w_ref, qd_ref, kd_ref, intra_ref):
    i = pl.program_id(1)
    h = pl.program_id(2)
    t = q_ref.shape[1]
    c = C_CHUNK
    dk = C_HEAD_DIM

    def conv_silu(main_ref, prev_ref, cw_ref):
        x = main_ref[0]
        pv = jnp.where(i > 0, prev_ref[0], 0.0)
        ext = jnp.concatenate([pv, x], axis=0)
        y = cw_ref[C_CONV - 1:C_CONV, :] * x
        for back in range(1, C_CONV):
            shifted = pltpu.roll(ext, back, 0)[SUBLANES:, :]
            y = y + cw_ref[C_CONV - 1 - back:C_CONV - back, :] * shifted
        return y * _sigmoid(y)

    def l2n(z):
        return z * lax.rsqrt(jnp.sum(z * z, axis=-1, keepdims=True) + L2_EPS)

    q = l2n(conv_silu(q_ref, qp_ref, cwq_ref)) * (dk ** -0.5)
    k = l2n(conv_silu(k_ref, kp_ref, cwk_ref))
    v = conv_silu(v_ref, vp_ref, cwv_ref)

    ba = ba_ref[0]
    lane = lax.broadcasted_iota(jnp.int32, (t, LANES), 1)
    beta = _sigmoid(jnp.sum(jnp.where(lane == h, ba, 0.0), axis=1, keepdims=True))
    g_all = -jnp.exp(alog_ref[...]) * _softplus(ba + dtb_ref[...])
    g_col = jnp.sum(jnp.where(lane == C_HEADS + h, g_all, 0.0), axis=1, keepdims=True)
    sub = lax.broadcasted_iota(jnp.int32, (LANES, t), 0)
    g_row = jnp.sum(jnp.where(sub == C_HEADS + h, g_all.T, 0.0), axis=0, keepdims=True)

    ii = lax.broadcasted_iota(jnp.int32, (c, c), 0)
    jj = lax.broadcasted_iota(jnp.int32, (c, c), 1)
    low = ii >= jj
    chunks = [slice(ch * c, (ch + 1) * c) for ch in range(t // c)]
    gc_cols = [jnp.sum(jnp.where(low, g_row[:, r], 0.0), axis=1, keepdims=True) for r in chunks]
    gc_rows = [jnp.sum(jnp.where(ii <= jj, g_col[r], 0.0), axis=0, keepdims=True) for r in chunks]
    decays = [jnp.where(low, jnp.exp(jnp.where(low, a - b, 0.0)), 0.0) for a, b in zip(gc_cols, gc_rows)]
    kbs = [k[r] * beta[r] for r in chunks]
    kks = [_dot_nt(kb, k[r]) for kb, r in zip(kbs, chunks)]
    qks = [_dot_nt(q[r], k[r]) for r in chunks]
    pws = [jnp.where(ii > jj, -(kk * dec), 0.0) for kk, dec in zip(kks, decays)]
    tms = pws
    for _ in range(int(math.log2(c)) - 1):
        pws = [_dot(pw, pw) for pw in pws]
        tms = [tm + pw + _dot(tm, pw) for tm, pw in zip(tms, pws)]
    egcs = [jnp.exp(a) for a in gc_cols]
    vbs = [v[r] * beta[r] for r in chunks]
    kbes = [kb * e for kb, e in zip(kbs, egcs)]
    us = [vb + _dot(tm, vb) for tm, vb in zip(tms, vbs)]
    ws = [kbe + _dot(tm, kbe) for tm, kbe in zip(tms, kbes)]
    for n, r in enumerate(chunks):
        gl = gc_cols[n][c - 1:c, :]
        u_ref[0, r, :] = us[n]
        w_ref[0, r, :] = ws[n]
        qd_ref[0, r, :] = q[r] * egcs[n]
        kd_ref[0, r, :] = k[r] * jnp.exp(gl - gc_cols[n])
        intra_ref[0, 0, r, 0:c] = qks[n] * decays[n]
        intra_ref[0, 0, r, c:2 * c] = jnp.broadcast_to(jnp.exp(gl), (c, c))


def _gdn_prep(hp3d, conv_w, alog_row, dtb_row):
    b, s, _ = hp3d.shape
    ba_col = 4 * C_WIDTH // LANES
    t = GDN_PREP_TILE
    hds = C_HEADS
    per = t // SUBLANES

    def main(off):
        return pl.BlockSpec((1, t, LANES), lambda bi, i, h: (bi, i, off + h))

    def prev(off):
        return pl.BlockSpec((1, SUBLANES, LANES),
                            lambda bi, i, h: (bi, jnp.maximum(i * per - 1, 0), off + h))

    def cw(off):
        return pl.BlockSpec((C_CONV, LANES), lambda bi, i, h: (0, off + h))

    out_main = pl.BlockSpec((1, t, LANES), lambda bi, i, h: (bi, i, h))
    full = jax.ShapeDtypeStruct((b, s, C_WIDTH), F32)
    return pl.pallas_call(
        _gdn_prep_kernel,
        grid=(b, s // t, hds),
        in_specs=[main(0), main(hds), main(2 * hds), prev(0), prev(hds), prev(2 * hds),
                  cw(0), cw(hds), cw(2 * hds),
                  pl.BlockSpec((1, t, LANES), lambda bi, i, h: (bi, i, ba_col)),
                  pl.BlockSpec((1, LANES), lambda bi, i, h: (0, 0)),
                  pl.BlockSpec((1, LANES), lambda bi, i, h: (0, 0))],
        out_specs=[out_main, out_main, out_main, out_main,
                   pl.BlockSpec((1, 1, t, LANES), lambda bi, i, h: (bi, h, i, 0))],
        out_shape=[full, full, full, full, jax.ShapeDtypeStruct((b, hds, s, LANES), F32)],
        compiler_params=_params("parallel", "parallel", "parallel"),
        name="gdn_prep",
    )(hp3d, hp3d, hp3d, hp3d, hp3d, hp3d, conv_w, conv_w, conv_w, hp3d, alog_row, dtb_row)


def _gdn_scan_kernel(u_ref, w_ref, qd_ref, kd_ref, intra_ref, gate_ref, nw_ref, o_ref, st_ref):
    c = C_CHUNK
    dk = C_HEAD_DIM
    t = u_ref.shape[1]

    @pl.when(pl.program_id(1) == 0)
    def _():
        st_ref[...] = jnp.zeros_like(st_ref)

    nw = nw_ref[...]
    heads = [slice(h * dk, (h + 1) * dk) for h in range(C_HEADS)]
    for ch in range(t // c):
        r = slice(ch * c, (ch + 1) * c)
        sts = [st_ref[h] for h in range(C_HEADS)]
        wss = [_dot(jnp.concatenate([w_ref[0, r, cs], qd_ref[0, r, cs]], axis=0), st)
               for cs, st in zip(heads, sts)]
        v_news = [u_ref[0, r, cs] - ws[:c] for cs, ws in zip(heads, wss)]
        kvs = [_dot(kd_ref[0, r, cs].T, v_new) for cs, v_new in zip(heads, v_news)]
        os_ = [ws[c:] + _dot(intra_ref[0, h, r, 0:c], v_new) for h, (ws, v_new) in enumerate(zip(wss, v_news))]
        for h, cs in enumerate(heads):
            st_ref[h] = sts[h] * intra_ref[0, h, ch * c:ch * c + 1, c:c + 1] + kvs[h]
            o = os_[h]
            o = o * lax.rsqrt(jnp.mean(o * o, axis=-1, keepdims=True) + RMS_EPS) * nw
            gt = gate_ref[0, r, cs]
            o_ref[0, r, cs] = o * (gt * _sigmoid(gt))


def _gdn_scan(u, w, qd, kd, intra, hp3d, norm_w):
    b, s, cw = u.shape
    t = GDN_SCAN_TILE
    blk = pl.BlockSpec((1, t, cw), lambda bi, i: (bi, i, 0))
    return pl.pallas_call(
        _gdn_scan_kernel,
        grid=(b, s // t),
        in_specs=[blk, blk, blk, blk,
                  pl.BlockSpec((1, C_HEADS, t, LANES), lambda bi, i: (bi, 0, i, 0)),
                  pl.BlockSpec((1, t, cw), lambda bi, i: (bi, i, 3)),
                  pl.BlockSpec((1, C_HEAD_DIM), lambda bi, i: (0, 0))],
        out_specs=blk,
        out_shape=jax.ShapeDtypeStruct((b, s, cw), F32),
        scratch_shapes=[pltpu.VMEM((C_HEADS, C_HEAD_DIM, C_HEAD_DIM), F32)],
        compiler_params=_params("parallel", "arbitrary"),
        name="gdn_scan",
    )(u, w, qd, kd, intra, hp3d, _row(norm_w))


ROW_TILE = 512


def kernel(x, rel_table, ev_w_in, ev_w_out, s5_lambda_re, s5_lambda_im, s5_b_re, s5_b_im, s5_c_re, s5_c_im, s5_d, s5_log_dt, s5_w_glu, s5_b_glu, od_w_in, od_conv_w, od_a_log, od_dt_bias, od_norm_w, od_w_out, ln_mix_g, ln_mix_b, mlp_w1, mlp_w2, ln_ffn_g, ln_ffn_b):
    bsz, s, d = x.shape
    m = bsz * s
    x2d = x.reshape(m, d)
    table_t = rel_table.T.astype(F32)
    bias = _bias_tiles(table_t)
    for layer in range(DEPTH):
        i = layer // 2
        if layer % 2 == 0:
            hp = _proj(x2d, _mx(ev_w_in[i]), ROW_TILE)
            hp3d = hp.reshape(bsz, s, hp.shape[1])
            attn = _moba(hp3d, bias)
            wx, wy, a_re2, a_im2 = _s5_weights(s5_lambda_re[i], s5_lambda_im[i], s5_b_re[i], s5_b_im[i],
                                               s5_c_re[i], s5_c_im[i], s5_log_dt[i])
            y = _s5_scan(hp3d[..., 3 * A_WIDTH:], wx, wy, a_re2, a_im2)
            x2d = _tail_even(attn.reshape(m, A_WIDTH), y.reshape(m, S5_WIDTH), hp, x2d,
                             s5_d[i], s5_w_glu[i], s5_b_glu[i], ev_w_out[i],
                             ln_mix_g[layer], ln_mix_b[layer], mlp_w1[layer], mlp_w2[layer],
                             ln_ffn_g[layer], ln_ffn_b[layer], ROW_TILE)
        else:
            w_in = od_w_in[i]
            w_all = jnp.pad(w_in, ((0, 0), (0, LANES - 2 * C_HEADS)))
            hp = _proj(x2d, _mx(w_all), ROW_TILE)
            hp3d = hp.reshape(bsz, s, 4 * C_WIDTH + LANES)
            pad8 = jnp.zeros((C_HEADS,), F32)
            padr = jnp.zeros((LANES - 2 * C_HEADS,), F32)
            alog_row = jnp.concatenate([pad8, od_a_log[i].astype(F32), padr]).reshape(1, LANES)
            dtb_row = jnp.concatenate([pad8, od_dt_bias[i].astype(F32), padr]).reshape(1, LANES)
            u, w, qd, kd, intra = _gdn_prep(hp3d, od_conv_w[i].astype(F32), alog_row, dtb_row)
            o = _gdn_scan(u, w, qd, kd, intra, hp3d, od_norm_w[i])
            x2d = _tail_odd(o.reshape(m, C_WIDTH), x2d, od_w_out[i],
                            ln_mix_g[layer], ln_mix_b[layer], mlp_w1[layer], mlp_w2[layer],
                            ln_ffn_g[layer], ln_ffn_b[layer], ROW_TILE)
    return x2d.reshape(bsz, s, d)
```

```python
import functools
import math

import jax
import jax.numpy as jnp
import numpy as np
from jax import lax
from jax.experimental import pallas as pl
from jax.experimental.pallas import tpu as pltpu

F32 = jnp.float32
MXU_DTYPE = jnp.bfloat16

LANES = 128
SUBLANES = 8
VMEM_LIMIT_BYTES = 56 * 1024 * 1024

A_HEADS = 8
A_HEAD_DIM = 64
A_WIDTH = A_HEADS * A_HEAD_DIM
MOBA_BLOCK = 256
MOBA_TOPK = 3
REL_BUCKETS = 32
REL_MAX_DIST = 128
S5_GROUP = 16
S5_GROUPS = 32
S5_STATE = 64
S5_WIDTH = S5_GROUP * S5_GROUPS
C_HEADS = 8
C_HEAD_DIM = 128
C_WIDTH = C_HEADS * C_HEAD_DIM
C_CONV = 4
C_CHUNK = 64
DEPTH = 2
DN_ALPHA = (2 * DEPTH) ** 0.25
LN_EPS = 1e-5
RMS_EPS = 1e-6
L2_EPS = 1e-6
NEG_INF = -1e30
LOG2E = math.log2(math.e)


def _mx(t):
    return t.astype(MXU_DTYPE)


def _dot(a, b):
    return jnp.dot(_mx(a), _mx(b), preferred_element_type=F32)


def _dot_tn(a, b):
    return lax.dot_general(_mx(a), _mx(b), (((0,), (0,)), ((), ())), preferred_element_type=F32)


def _dot_nt(a, b):
    return lax.dot_general(_mx(a), _mx(b), (((1,), (1,)), ((), ())), preferred_element_type=F32)


def _sigmoid(t):
    return 1.0 / (1.0 + jnp.exp(-t))


def _layer_norm(r, g, b):
    mu = jnp.mean(r, axis=-1, keepdims=True)
    d = r - mu
    var = jnp.mean(d * d, axis=-1, keepdims=True)
    return d * lax.rsqrt(var + LN_EPS) * g + b


def _params(*sem):
    return pltpu.CompilerParams(dimension_semantics=sem, vmem_limit_bytes=VMEM_LIMIT_BYTES)


def _resident(shape):
    return pl.BlockSpec(shape, lambda *_: (0,) * len(shape), pipeline_mode=pl.Buffered(1))


PROJ_CHUNK = 1024


def _proj_kernel(x_ref, w_ref, o_ref):
    xm = _mx(x_ref[...])
    n = w_ref.shape[1]
    for c0 in range(0, n, PROJ_CHUNK):
        c1 = min(c0 + PROJ_CHUNK, n)
        o_ref[:, c0:c1] = jnp.dot(xm, w_ref[:, c0:c1], preferred_element_type=F32)


def _proj(x2d, w, tm):
    m, k = x2d.shape
    n = w.shape[1]
    return pl.pallas_call(
        _proj_kernel,
        grid=(m // tm,),
        in_specs=[pl.BlockSpec((tm, k), lambda i: (i, 0)), _resident((k, n))],
        out_specs=pl.BlockSpec((tm, n), lambda i: (i, 0)),
        out_shape=jax.ShapeDtypeStruct((m, n), F32),
        compiler_params=_params("parallel"),
        name="in_proj",
    )(x2d, w)


def _t5_bucket_np(dist):
    max_exact = REL_BUCKETS // 2
    n = np.maximum(dist, 0)
    nf = np.maximum(n, 1).astype(np.float32)
    large = max_exact + (np.log(nf / np.float32(max_exact)) / np.float32(math.log(REL_MAX_DIST / max_exact))
                         * np.float32(REL_BUCKETS - max_exact)).astype(np.int32)
    large = np.minimum(large, REL_BUCKETS - 1)
    return np.where(n < max_exact, n, large).astype(np.int32)


def _bias_kernel(tab_ref, bkt_ref, o_ref):
    h = pl.program_id(0)
    blk = bkt_ref.shape[-1]
    key = lax.broadcasted_iota(jnp.int32, (blk, blk), 0)
    qry = lax.broadcasted_iota(jnp.int32, (blk, blk), 1)
    far = tab_ref[h, REL_BUCKETS - 1]
    for t in range(2):
        bk = bkt_ref[t]
        acc = jnp.zeros((blk, blk), F32)
        for i in range(REL_BUCKETS):
            acc = jnp.where(bk == i, (tab_ref[h, i] - far) * LOG2E, acc)
        if t == 0:
            acc = jnp.where(qry >= key, acc, NEG_INF)
        o_ref[0, t] = acc


def _bias_tiles(table_t):
    heads = table_t.shape[0]
    offs = np.arange(MOBA_BLOCK)
    dist_own = offs[None, :] - offs[:, None]
    buckets = np.stack([_t5_bucket_np(dist_own), _t5_bucket_np(dist_own + MOBA_BLOCK)])
    return pl.pallas_call(
        _bias_kernel,
        grid=(heads,),
        in_specs=[pl.BlockSpec(memory_space=pltpu.SMEM),
                  pl.BlockSpec((2, MOBA_BLOCK, MOBA_BLOCK), lambda h: (0, 0, 0))],
        out_specs=pl.BlockSpec((1, 2, MOBA_BLOCK, MOBA_BLOCK), lambda h: (h, 0, 0, 0)),
        out_shape=jax.ShapeDtypeStruct((heads, 2, MOBA_BLOCK, MOBA_BLOCK), F32),
        compiler_params=_params("parallel"),
        name="t5_bias",
    )(table_t, jnp.asarray(buckets))


def _moba_kernel(q_ref, k_ref, v_ref, bias_ref, o_ref, kmean_ref, kb_ref, vt_ref,
                 st0_ref, st1_ref, p0_ref, p1_ref, wq_ref, acc_ref):
    st_refs = (st0_ref, st1_ref)
    p_refs = (p0_ref, p1_ref)
    qb = pl.program_id(2)
    blk = MOBA_BLOCK
    d = A_HEAD_DIM
    nh = LANES // d
    nb = k_ref.shape[1] // blk

    @pl.when(qb == 0)
    def _():
        lane = lax.broadcasted_iota(jnp.int32, (blk, LANES), 1)
        for j in range(nb + 1):
            kb_ref[j, :, LANES:2 * LANES] = jnp.where(lane == j, 1.0, 0.0).astype(MXU_DTYPE)
        for j in range(nb):
            kj = k_ref[0, j * blk:(j + 1) * blk, :]
            kb_ref[j, :, 0:LANES] = _mx(kj)
            kmean_ref[j:j + 1, :] = jnp.mean(kj, axis=0, keepdims=True)
            vt_ref[j] = _mx(v_ref[0, j * blk:(j + 1) * blk, :].T)
        kb_ref[nb, :, 0:LANES] = jnp.zeros((blk, LANES), MXU_DTYPE)
        vt_ref[nb] = jnp.zeros((LANES, blk), MXU_DTYPE)

    qt = (q_ref[0] * (d ** -0.5)).T
    zero = jnp.zeros((d, blk), F32)
    q_rows = jnp.concatenate([jnp.concatenate([qt[:d], zero], axis=1),
                              jnp.concatenate([zero, qt[d:]], axis=1)], axis=0) * LOG2E

    blk_id = lax.broadcasted_iota(jnp.int32, (nb, blk), 0)
    mask_rows = []
    for hh in range(nh):
        gate = lax.dot_general(kmean_ref[:, hh * d:(hh + 1) * d], qt[hh * d:(hh + 1) * d],
                               (((1,), (0,)), ((), ())),
                               precision=lax.Precision.HIGHEST, preferred_element_type=F32)
        g = jnp.where(blk_id < qb, gate, -jnp.inf)
        sel = jnp.zeros((nb, blk), F32)
        for _ in range(MOBA_TOPK):
            m = jnp.max(g, axis=0, keepdims=True)
            is_m = jnp.logical_and(g == m, g > -jnp.inf)
            idx = jnp.min(jnp.where(is_m, blk_id, nb), axis=0, keepdims=True)
            pick = blk_id == idx
            sel = jnp.where(pick, 1.0, sel)
            g = jnp.where(pick, -jnp.inf, g)
        keep = jnp.logical_or(sel > 0.0, blk_id == qb)
        mask_rows.append(jnp.where(keep, 0.0, NEG_INF))

    row = lax.broadcasted_iota(jnp.int32, (LANES - nb, nh * blk), 0)
    wq_ref[...] = _mx(jnp.concatenate([q_rows, jnp.concatenate(mask_rows, axis=1),
                                       jnp.where(row == 0, NEG_INF, 0.0)], axis=0))

    def score_stage(slot, ja, jb, with_bias):
        keys = jnp.concatenate([kb_ref[ja], kb_ref[jb]], axis=0)
        st = jnp.dot(keys, wq_ref[...], preferred_element_type=F32)
        if with_bias:
            st = st + jnp.concatenate([jnp.concatenate([bias_ref[hh, w] for hh in range(nh)], axis=1)
                                       for w in range(2)], axis=0)
        st_refs[slot][0] = st[:blk]
        st_refs[slot][1] = st[blk:]
        return jnp.max(st, axis=0, keepdims=True)

    def softmax_stage(slot, cmax, stats):
        new = []
        for hh in range(nh):
            cs = slice(hh * blk, (hh + 1) * blk)
            m_new = cmax[:, cs]
            if stats is not None:
                m, l = stats[hh]
                m_new = jnp.maximum(m, m_new)
            ps = [jnp.exp2(st_refs[slot][w, :, cs] - m_new) for w in range(2)]
            p_refs[slot][hh, 0:blk, :] = _mx(ps[0])
            p_refs[slot][hh, blk:2 * blk, :] = _mx(ps[1])
            l_new = jnp.sum(ps[0], axis=0, keepdims=True) + jnp.sum(ps[1], axis=0, keepdims=True)
            if stats is None:
                new.append((m_new, l_new, None))
            else:
                alpha = jnp.exp2(m - m_new)
                new.append((m_new, alpha * l + l_new, alpha))
        return new

    def value_stage(slot, ja, jb):
        return [jnp.dot(jnp.concatenate([vt_ref[ja, hh * d:(hh + 1) * d, :], vt_ref[jb, hh * d:(hh + 1) * d, :]],
                                        axis=1), p_refs[slot][hh], preferred_element_type=F32)
                for hh in range(nh)]

    nfar = qb - 1
    npairs = (nfar + 1) // 2

    def far_blocks(i):
        return jnp.where(2 * i < nfar, 2 * i, nb), jnp.where(2 * i + 1 < nfar, 2 * i + 1, nb)

    j_prev = jnp.where(qb > 0, qb - 1, nb)
    cmax0 = score_stage(0, qb, j_prev, True)
    cmax1 = score_stage(1, *far_blocks(0), False)
    st0 = softmax_stage(0, cmax0, None)
    for hh, pv_h in enumerate(value_stage(0, qb, j_prev)):
        acc_ref[hh] = pv_h
    init = (tuple((m, l) for m, l, _ in st0), cmax1)

    def trip(slot, k, carry):
        stats, cmax = carry
        cmax_next = score_stage(1 - slot, *far_blocks(k), False)
        new = softmax_stage(slot, cmax, stats)
        pv = value_stage(slot, *far_blocks(k - 1))
        for hh, ((_, _, alpha), pv_h) in enumerate(zip(new, pv)):
            acc_ref[hh] = acc_ref[hh] * alpha + pv_h
        return tuple((m, l) for m, l, _ in new), cmax_next

    def trip_dyn(k, carry):
        return lax.cond(k % 2 == 1, functools.partial(trip, 1, k), functools.partial(trip, 0, k), carry)

    stats, _ = lax.fori_loop(1, npairs + 1, trip_dyn, init)
    out_t = jnp.concatenate([acc_ref[hh] / stats[hh][1] for hh in range(nh)], axis=0)
    o_ref[0] = out_t.T


def _moba(hp3d, bias):
    b, s, _ = hp3d.shape
    npair = A_WIDTH // LANES
    hpp = LANES // A_HEAD_DIM
    nb = s // MOBA_BLOCK
    return pl.pallas_call(
        _moba_kernel,
        grid=(b, npair, nb),
        in_specs=[pl.BlockSpec((1, MOBA_BLOCK, LANES), lambda bi, p, i: (bi, i, p)),
                  pl.BlockSpec((1, s, LANES), lambda bi, p, i: (bi, 0, npair + p)),
                  pl.BlockSpec((1, s, LANES), lambda bi, p, i: (bi, 0, 2 * npair + p)),
                  pl.BlockSpec((hpp, 2, MOBA_BLOCK, MOBA_BLOCK), lambda bi, p, i: (p, 0, 0, 0))],
        out_specs=pl.BlockSpec((1, MOBA_BLOCK, LANES), lambda bi, p, i: (bi, i, p)),
        out_shape=jax.ShapeDtypeStruct((b, s, A_WIDTH), F32),
        scratch_shapes=[pltpu.VMEM((nb, LANES), F32),
                        pltpu.VMEM((nb + 1, MOBA_BLOCK, 2 * LANES), MXU_DTYPE),
                        pltpu.VMEM((nb + 1, LANES, MOBA_BLOCK), MXU_DTYPE),
                        pltpu.VMEM((2, MOBA_BLOCK, hpp * MOBA_BLOCK), F32),
                        pltpu.VMEM((2, MOBA_BLOCK, hpp * MOBA_BLOCK), F32),
                        pltpu.VMEM((hpp, 2 * MOBA_BLOCK, MOBA_BLOCK), MXU_DTYPE),
                        pltpu.VMEM((hpp, 2 * MOBA_BLOCK, MOBA_BLOCK), MXU_DTYPE),
                        pltpu.VMEM((2 * LANES, hpp * MOBA_BLOCK), MXU_DTYPE),
                        pltpu.VMEM((hpp, A_HEAD_DIM, MOBA_BLOCK), F32)],
        compiler_params=_params("parallel", "parallel", "arbitrary"),
        name="moba",
    )(hp3d, hp3d, hp3d, bias)


S5_HALF_CH = S5_WIDTH // 2
S5_HALF_ST = (S5_GROUPS // 2) * S5_STATE
S5_TILE = 256
S5_NT = S5_HALF_ST // S5_TILE
S5_STEPS = 64


def _s5_kernel(u_ref, wx_ref, wy_ref, a_ref, o_ref, st_ref, x_scr, h_scr):
    rows = u_ref.shape[0]

    @pl.when(pl.program_id(0) == 0)
    def _():
        st_ref[...] = jnp.zeros_like(st_ref)

    def lhs(n):
        c = (n // 2) * LANES
        return jnp.concatenate([u_ref[:, c:c + LANES],
                                u_ref[:, S5_HALF_CH + c:S5_HALF_CH + c + LANES]], axis=1)

    for part in range(2):
        for n in range(S5_NT):
            c0 = part * S5_HALF_ST + n * S5_TILE
            x_scr[:, c0:c0 + S5_TILE] = jnp.dot(lhs(n), wx_ref[part, n], preferred_element_type=F32)

    ar = a_ref[0]
    ai = a_ref[1]

    def step(t, carry):
        hr, hi = carry
        r0 = pl.multiple_of(t * SUBLANES, SUBLANES)
        xr = x_scr[pl.ds(r0, SUBLANES), 0:S5_HALF_ST]
        xi = x_scr[pl.ds(r0, SUBLANES), S5_HALF_ST:2 * S5_HALF_ST]
        nhr = ar * hr - ai * hi + xr
        nhi = ar * hi + ai * hr + xi
        h_scr[pl.ds(r0, SUBLANES), 0:S5_HALF_ST] = nhr
        h_scr[pl.ds(r0, SUBLANES), S5_HALF_ST:2 * S5_HALF_ST] = nhi
        return nhr, nhi

    hr, hi = lax.fori_loop(0, rows // SUBLANES, step, (st_ref[0], st_ref[1]), unroll=2)
    st_ref[0] = hr
    st_ref[1] = hi

    first_half = lax.broadcasted_iota(jnp.int32, (rows, LANES), 0) % 2 == 0
    for m in range(S5_NT // 2):
        acc = jnp.zeros((rows, 2 * LANES), F32)
        for n in (2 * m, 2 * m + 1):
            for part in range(2):
                c0 = part * S5_HALF_ST + n * S5_TILE
                acc = acc + jnp.dot(_mx(h_scr[:, c0:c0 + S5_TILE]), wy_ref[part, n],
                                    preferred_element_type=F32)
        o_ref[:, m * LANES:(m + 1) * LANES] = jnp.where(first_half, acc[:, :LANES], acc[:, LANES:])


def _s5_weights(lam_re, lam_im, b_re, b_im, c_re, c_im, log_dt):
    lr = jnp.minimum(lam_re.astype(F32), -1e-4)
    li = lam_im.astype(F32)
    dt = jnp.exp(log_dt.astype(F32))[:, None]
    mag = jnp.exp(lr * dt)
    a_re = mag * jnp.cos(li * dt)
    a_im = mag * jnp.sin(li * dt)
    den = lr * lr + li * li
    coef_re = ((a_re - 1.0) * lr + a_im * li) / den
    coef_im = (a_im * lr - (a_re - 1.0) * li) / den
    br = b_re.astype(F32)
    bi = b_im.astype(F32)
    bb_re = coef_re[..., None] * br - coef_im[..., None] * bi
    bb_im = coef_re[..., None] * bi + coef_im[..., None] * br
    gh = S5_GROUPS // 2
    eye = jnp.eye(gh, dtype=F32)

    def b_dense(bb):
        t = bb.reshape(2, gh, S5_STATE, S5_GROUP)
        return jnp.einsum('fgph,gk->fghkp', t, eye).reshape(2, S5_HALF_CH, S5_HALF_ST)

    def c_dense(cc):
        t = cc.reshape(2, gh, S5_GROUP, S5_STATE)
        return jnp.einsum('fghp,gk->fgpkh', t, eye).reshape(2, S5_HALF_ST, S5_HALF_CH)

    def x_tiles(bd):
        tiles = []
        for n in range(S5_NT):
            c = (n // 2) * LANES
            tiles.append(jnp.concatenate([bd[0, c:c + LANES, n * S5_TILE:(n + 1) * S5_TILE],
                                          bd[1, c:c + LANES, n * S5_TILE:(n + 1) * S5_TILE]], axis=0))
        return jnp.stack(tiles)

    def y_tiles(cd):
        tiles = []
        for n in range(S5_NT):
            c = (n // 2) * LANES
            tiles.append(jnp.concatenate([cd[0, n * S5_TILE:(n + 1) * S5_TILE, c:c + LANES],
                                          cd[1, n * S5_TILE:(n + 1) * S5_TILE, c:c + LANES]], axis=1))
        return jnp.stack(tiles)

    wx = jnp.stack([x_tiles(b_dense(bb_re)), x_tiles(b_dense(bb_im))]).astype(MXU_DTYPE)
    wy = jnp.stack([y_tiles(c_dense(c_re.astype(F32))), y_tiles(c_dense(-c_im.astype(F32)))]).astype(MXU_DTYPE)
    return wx, wy, a_re.reshape(2, S5_HALF_ST), a_im.reshape(2, S5_HALF_ST)


def _s5_scan(u3d, wx, wy, a_re2, a_im2):
    b, s, _ = u3d.shape
    rows_per_t = 2 * b
    assert rows_per_t == SUBLANES
    half_mask = (jnp.arange(2)[:, None] == (jnp.arange(S5_WIDTH) // S5_HALF_CH)[None, :])
    um = jnp.where(half_mask[None, None], u3d.transpose(1, 0, 2)[:, :, None, :], 0.0)
    um = um.reshape(s * rows_per_t, S5_WIDTH).astype(MXU_DTYPE)
    sel = jnp.arange(rows_per_t) % 2
    a = jnp.stack([a_re2[sel], a_im2[sel]])
    rows = S5_STEPS * rows_per_t
    y = pl.pallas_call(
        _s5_kernel,
        grid=(s // S5_STEPS,),
        in_specs=[pl.BlockSpec((rows, S5_WIDTH), lambda i: (i, 0)),
                  _resident(wx.shape), _resident(wy.shape), _resident(a.shape)],
        out_specs=pl.BlockSpec((rows, S5_HALF_CH), lambda i: (i, 0)),
        out_shape=jax.ShapeDtypeStruct((s * rows_per_t, S5_HALF_CH), F32),
        scratch_shapes=[pltpu.VMEM((2, rows_per_t, S5_HALF_ST), F32),
                        pltpu.VMEM((rows, 2 * S5_HALF_ST), F32),
                        pltpu.VMEM((rows, 2 * S5_HALF_ST), F32)],
        compiler_params=_params("arbitrary"),
        name="s5_scan",
    )(um, wx, wy, a)
    return y.reshape(s, b, S5_WIDTH).transpose(1, 0, 2)


MLP_CHUNK = 1024


def _tail_body(mix, x_ref, g1_ref, b1_ref, w1_ref, w2_ref, g2_ref, b2_ref, o_ref):
    x1 = _layer_norm(DN_ALPHA * x_ref[...] + mix, g1_ref[...], b1_ref[...])
    x1m = _mx(x1)
    acc = jnp.zeros(x1.shape, F32)
    for f in range(w1_ref.shape[1] // MLP_CHUNK):
        hid = jnp.dot(x1m, w1_ref[:, f * MLP_CHUNK:(f + 1) * MLP_CHUNK], preferred_element_type=F32)
        hid = jnp.square(jnp.maximum(hid, 0.0))
        acc = acc + jnp.dot(_mx(hid), w2_ref[f * MLP_CHUNK:(f + 1) * MLP_CHUNK, :],
                            preferred_element_type=F32)
    o_ref[...] = _layer_norm(DN_ALPHA * x1 + acc, g2_ref[...], b2_ref[...])


def _tail_even_kernel(attn_ref, y_ref, u_ref, dsk_ref, wg_ref, bg_ref, wo_ref, x_ref,
                      g1_ref, b1_ref, w1_ref, w2_ref, g2_ref, b2_ref, o_ref):
    y = y_ref[...] + dsk_ref[...] * u_ref[...]
    g = 0.5 * y * (1.0 + jnp.tanh(math.sqrt(2.0 / math.pi) * (y + 0.044715 * (y * y * y))))
    ssm = g * _sigmoid(jnp.dot(_mx(g), wg_ref[...], preferred_element_type=F32) + bg_ref[...])
    ka = attn_ref.shape[1]
    mix = (jnp.dot(_mx(attn_ref[...]), wo_ref[0:ka, :], preferred_element_type=F32)
           + jnp.dot(_mx(ssm), wo_ref[ka:, :], preferred_element_type=F32))
    _tail_body(mix, x_ref, g1_ref, b1_ref, w1_ref, w2_ref, g2_ref, b2_ref, o_ref)


def _tail_odd_kernel(a_ref, wo_ref, x_ref, g1_ref, b1_ref, w1_ref, w2_ref, g2_ref, b2_ref, o_ref):
    mix = jnp.dot(_mx(a_ref[...]), wo_ref[...], preferred_element_type=F32)
    _tail_body(mix, x_ref, g1_ref, b1_ref, w1_ref, w2_ref, g2_ref, b2_ref, o_ref)


def _row(v):
    return v.reshape(1, -1).astype(F32)


def _tail_common_specs(tm, d, dff):
    return [_resident((d, d)),
            pl.BlockSpec((tm, d), lambda i: (i, 0)),
            _resident((1, d)), _resident((1, d)),
            _resident((d, dff)), _resident((dff, d)),
            _resident((1, d)), _resident((1, d))]


def _tail_even(attn2d, y2d, hp2d, x2d, d_skip, w_glu, b_glu, w_out, g1, b1, w1, w2, g2, b2, tm):
    m, d = x2d.shape
    dff = w1.shape[1]
    ucol = hp2d.shape[1] // S5_WIDTH - 1
    return pl.pallas_call(
        _tail_even_kernel,
        grid=(m // tm,),
        in_specs=[pl.BlockSpec((tm, A_WIDTH), lambda i: (i, 0)),
                  pl.BlockSpec((tm, S5_WIDTH), lambda i: (i, 0)),
                  pl.BlockSpec((tm, S5_WIDTH), lambda i: (i, ucol)),
                  _resident((1, S5_WIDTH)), _resident((S5_WIDTH, S5_WIDTH)), _resident((1, S5_WIDTH))]
        + _tail_common_specs(tm, d, dff),
        out_specs=pl.BlockSpec((tm, d), lambda i: (i, 0)),
        out_shape=jax.ShapeDtypeStruct((m, d), F32),
        compiler_params=_params("parallel"),
        name="tail_even",
    )(attn2d, y2d, hp2d, _row(d_skip), _mx(w_glu), _row(b_glu), _mx(w_out), x2d,
      _row(g1), _row(b1), _mx(w1), _mx(w2), _row(g2), _row(b2))


def _tail_odd(a2d, x2d, w_out, g1, b1, w1, w2, g2, b2, tm):
    m, d = x2d.shape
    dff = w1.shape[1]
    return pl.pallas_call(
        _tail_odd_kernel,
        grid=(m // tm,),
        in_specs=[pl.BlockSpec((tm, a2d.shape[1]), lambda i: (i, 0))] + _tail_common_specs(tm, d, dff),
        out_specs=pl.BlockSpec((tm, d), lambda i: (i, 0)),
        out_shape=jax.ShapeDtypeStruct((m, d), F32),
        compiler_params=_params("parallel"),
        name="tail_odd",
    )(a2d, _mx(w_out), x2d, _row(g1), _row(b1), _mx(w1), _mx(w2), _row(g2), _row(b2))


GDN_PREP_TILE = 512
GDN_SCAN_TILE = 256


def _softplus(t):
    return jnp.maximum(t, 0.0) + jnp.log1p(jnp.exp(-jnp.abs(t)))


def _gdn_prep_kernel(q_ref, k_ref, v_ref, qp_ref, kp_ref, vp_ref, cwq_ref, cwk_ref, cwv_ref,
                     ba_ref, alog_ref, dtb_ref, u_ref, w_ref, qd_ref, kd_ref, intra_ref, edl_ref):
    i = pl.program_id(1)
    h = pl.program_id(2)
    t = q_ref.shape[1]
    c = C_CHUNK
    dk = C_HEAD_DIM

    def conv_silu(main_ref, prev_ref, cw_ref):
        x = main_ref[0]
        pv = jnp.where(i > 0, prev_ref[0], 0.0)
        ext = jnp.concatenate([pv, x], axis=0)
        y = cw_ref[C_CONV - 1:C_CONV, :] * x
        for back in range(1, C_CONV):
            shifted = pltpu.roll(ext, back, 0)[SUBLANES:, :]
            y = y + cw_ref[C_CONV - 1 - back:C_CONV - back, :] * shifted
        return y * _sigmoid(y)

    def l2n(z):
        return z * lax.rsqrt(jnp.sum(z * z, axis=-1, keepdims=True) + L2_EPS)

    q = l2n(conv_silu(q_ref, qp_ref, cwq_ref)) * (dk ** -0.5)
    k = l2n(conv_silu(k_ref, kp_ref, cwk_ref))
    v = conv_silu(v_ref, vp_ref, cwv_ref)

    ba = ba_ref[0]
    lane = lax.broadcasted_iota(jnp.int32, (t, LANES), 1)
    beta = _sigmoid(jnp.sum(jnp.where(lane == h, ba, 0.0), axis=1, keepdims=True))
    g_all = -jnp.exp(alog_ref[...]) * _softplus(ba + dtb_ref[...])
    g_col = jnp.sum(jnp.where(lane == C_HEADS + h, g_all, 0.0), axis=1, keepdims=True)
    sub = lax.broadcasted_iota(jnp.int32, (LANES, t), 0)
    g_row = jnp.sum(jnp.where(sub == C_HEADS + h, g_all.T, 0.0), axis=0, keepdims=True)

    ii = lax.broadcasted_iota(jnp.int32, (c, c), 0)
    jj = lax.broadcasted_iota(jnp.int32, (c, c), 1)
    low = ii >= jj
    chunks = [slice(ch * c, (ch + 1) * c) for ch in range(t // c)]
    gc_cols = [jnp.sum(jnp.where(low, g_row[:, r], 0.0), axis=1, keepdims=True) for r in chunks]
    gc_rows = [jnp.sum(jnp.where(ii <= jj, g_col[r], 0.0), axis=0, keepdims=True) for r in chunks]
    decays = [jnp.where(low, jnp.exp(jnp.where(low, a - b, 0.0)), 0.0) for a, b in zip(gc_cols, gc_rows)]
    kbs = [k[r] * beta[r] for r in chunks]
    kks = [_dot_nt(kb, k[r]) for kb, r in zip(kbs, chunks)]
    qks = [_dot_nt(q[r], k[r]) for r in chunks]
    pws = [jnp.where(ii > jj, -(kk * dec), 0.0) for kk, dec in zip(kks, decays)]
    tms = pws
    for _ in range(int(math.log2(c)) - 1):
        pws = [_dot(pw, pw) for pw in pws]
        tms = [tm + pw + _dot(tm, pw) for tm, pw in zip(tms, pws)]
    egcs = [jnp.exp(a) for a in gc_cols]
    vbs = [v[r] * beta[r] for r in chunks]
    kbes = [kb * e for kb, e in zip(kbs, egcs)]
    us = [vb + _dot(tm, vb) for tm, vb in zip(tms, vbs)]
    ws = [kbe + _dot(tm, kbe) for tm, kbe in zip(tms, kbes)]
    for n, r in enumerate(chunks):
        gl = gc_cols[n][c - 1:c, :]
        u_ref[0, r, :] = us[n]
        w_ref[0, r, :] = _mx(ws[n])
        qd_ref[0, r, :] = _mx(q[r] * egcs[n])
        kd_ref[0, r, :] = _mx(k[r] * jnp.exp(gl - gc_cols[n]))
        intra_ref[0, 0, r, :] = _mx(qks[n] * decays[n])
        edl_ref[0, 0, n:n + 1, :] = jnp.broadcast_to(jnp.exp(gl), (1, LANES))


def _gdn_prep(hp3d, conv_w, alog_row, dtb_row):
    b, s, _ = hp3d.shape
    ba_col = 4 * C_WIDTH // LANES
    t = GDN_PREP_TILE
    hds = C_HEADS
    per = t // SUBLANES

    def main(off):
        return pl.BlockSpec((1, t, LANES), lambda bi, i, h: (bi, i, off + h))

    def prev(off):
        return pl.BlockSpec((1, SUBLANES, LANES),
                            lambda bi, i, h: (bi, jnp.maximum(i * per - 1, 0), off + h))

    def cw(off):
        return pl.BlockSpec((C_CONV, LANES), lambda bi, i, h: (0, off + h))

    out_main = pl.BlockSpec((1, t, LANES), lambda bi, i, h: (bi, i, h))
    full = jax.ShapeDtypeStruct((b, s, C_WIDTH), F32)
    half = jax.ShapeDtypeStruct((b, s, C_WIDTH), MXU_DTYPE)
    nc = t // C_CHUNK
    return pl.pallas_call(
        _gdn_prep_kernel,
        grid=(b, s // t, hds),
        in_specs=[main(0), main(hds), main(2 * hds), prev(0), prev(hds), prev(2 * hds),
                  cw(0), cw(hds), cw(2 * hds),
                  pl.BlockSpec((1, t, LANES), lambda bi, i, h: (bi, i, ba_col)),
                  pl.BlockSpec((1, LANES), lambda bi, i, h: (0, 0)),
                  pl.BlockSpec((1, LANES), lambda bi, i, h: (0, 0))],
        out_specs=[out_main, out_main, out_main, out_main,
                   pl.BlockSpec((1, 1, t, C_CHUNK), lambda bi, i, h: (bi, h, i, 0)),
                   pl.BlockSpec((1, 1, nc, LANES), lambda bi, i, h: (bi, h, i, 0))],
        out_shape=[full, half, half, half,
                   jax.ShapeDtypeStruct((b, hds, s, C_CHUNK), MXU_DTYPE),
                   jax.ShapeDtypeStruct((b, hds, s // C_CHUNK, LANES), F32)],
        compiler_params=_params("parallel", "parallel", "parallel"),
        name="gdn_prep",
    )(hp3d, hp3d, hp3d, hp3d, hp3d, hp3d, conv_w, conv_w, conv_w, hp3d, alog_row, dtb_row)


def _gdn_scan_kernel(u_ref, w_ref, qd_ref, kd_ref, intra_ref, edl_ref, gate_ref, nw_ref, o_ref, st_ref):
    c = C_CHUNK
    dk = C_HEAD_DIM
    t = u_ref.shape[1]
    chunk0 = pl.program_id(1) * (t // c)

    @pl.when(pl.program_id(1) == 0)
    def _():
        st_ref[...] = jnp.zeros_like(st_ref)

    nw = nw_ref[...]
    heads = [slice(h * dk, (h + 1) * dk) for h in range(C_HEADS)]
    for ch in range(t // c):
        r = slice(ch * c, (ch + 1) * c)
        sts = [st_ref[h] for h in range(C_HEADS)]
        wss = [_dot(jnp.concatenate([w_ref[0, r, cs], qd_ref[0, r, cs]], axis=0), st)
               for cs, st in zip(heads, sts)]
        v_news = [u_ref[0, r, cs] - ws[:c] for cs, ws in zip(heads, wss)]
        kvs = [_dot_tn(kd_ref[0, r, cs], v_new) for cs, v_new in zip(heads, v_news)]
        os_ = [ws[c:] + _dot(intra_ref[0, h, r, :], v_new) for h, (ws, v_new) in enumerate(zip(wss, v_news))]
        for h, cs in enumerate(heads):
            st_ref[h] = sts[h] * edl_ref[0, h, pl.ds(chunk0 + ch, 1), 0:1] + kvs[h]
            o = os_[h]
            o = o * lax.rsqrt(jnp.mean(o * o, axis=-1, keepdims=True) + RMS_EPS) * nw
            gt = gate_ref[0, r, cs]
            o_ref[0, r, cs] = o * (gt * _sigmoid(gt))


def _gdn_scan(u, w, qd, kd, intra, edl, hp3d, norm_w):
    b, s, cw = u.shape
    t = GDN_SCAN_TILE
    blk = pl.BlockSpec((1, t, cw), lambda bi, i: (bi, i, 0))
    return pl.pallas_call(
        _gdn_scan_kernel,
        grid=(b, s // t),
        in_specs=[blk, blk, blk, blk,
                  pl.BlockSpec((1, C_HEADS, t, C_CHUNK), lambda bi, i: (bi, 0, i, 0)),
                  pl.BlockSpec((1, C_HEADS, s // C_CHUNK, LANES), lambda bi, i: (bi, 0, 0, 0)),
                  pl.BlockSpec((1, t, cw), lambda bi, i: (bi, i, 3)),
                  pl.BlockSpec((1, C_HEAD_DIM), lambda bi, i: (0, 0))],
        out_specs=blk,
        out_shape=jax.ShapeDtypeStruct((b, s, cw), F32),
        scratch_shapes=[pltpu.VMEM((C_HEADS, C_HEAD_DIM, C_HEAD_DIM), F32)],
        compiler_params=_params("parallel", "arbitrary"),
        name="gdn_scan",
    )(u, w, qd, kd, intra, edl, hp3d, _row(norm_w))


ROW_TILE = 512


def kernel(x, rel_table, ev_w_in, ev_w_out, s5_lambda_re, s5_lambda_im, s5_b_re, s5_b_im, s5_c_re, s5_c_im, s5_d, s5_log_dt, s5_w_glu, s5_b_glu, od_w_in, od_conv_w, od_a_log, od_dt_bias, od_norm_w, od_w_out, ln_mix_g, ln_mix_b, mlp_w1, mlp_w2, ln_ffn_g, ln_ffn_b):
    bsz, s, d = x.shape
    m = bsz * s
    x2d = x.reshape(m, d)
    table_t = rel_table.T.astype(F32)
    bias = _bias_tiles(table_t)
    for layer in range(DEPTH):
        i = layer // 2
        if layer % 2 == 0:
            hp = _proj(x2d, _mx(ev_w_in[i]), ROW_TILE)
            hp3d = hp.reshape(bsz, s, hp.shape[1])
            attn = _moba(hp3d, bias)
            wx, wy, a_re2, a_im2 = _s5_weights(s5_lambda_re[i], s5_lambda_im[i], s5_b_re[i], s5_b_im[i],
                                               s5_c_re[i], s5_c_im[i], s5_log_dt[i])
            y = _s5_scan(hp3d[..., 3 * A_WIDTH:], wx, wy, a_re2, a_im2)
            x2d = _tail_even(attn.reshape(m, A_WIDTH), y.reshape(m, S5_WIDTH), hp, x2d,
                             s5_d[i], s5_w_glu[i], s5_b_glu[i], ev_w_out[i],
                             ln_mix_g[layer], ln_mix_b[layer], mlp_w1[layer], mlp_w2[layer],
                             ln_ffn_g[layer], ln_ffn_b[layer], ROW_TILE)
        else:
            w_in = od_w_in[i]
            w_all = jnp.pad(w_in, ((0, 0), (0, LANES - 2 * C_HEADS)))
            hp = _proj(x2d, _mx(w_all), ROW_TILE)
            hp3d = hp.reshape(bsz, s, 4 * C_WIDTH + LANES)
            pad8 = jnp.zeros((C_HEADS,), F32)
            padr = jnp.zeros((LANES - 2 * C_HEADS,), F32)
            alog_row = jnp.concatenate([pad8, od_a_log[i].astype(F32), padr]).reshape(1, LANES)
            dtb_row = jnp.concatenate([pad8, od_dt_bias[i].astype(F32), padr]).reshape(1, LANES)
            u, w, qd, kd, intra, edl = _gdn_prep(hp3d, od_conv_w[i].astype(F32), alog_row, dtb_row)
            o = _gdn_scan(u, w, qd, kd, intra, edl, hp3d, od_norm_w[i])
            x2d = _tail_odd(o.reshape(m, C_WIDTH), x2d, od_w_out[i],
                            ln_mix_g[layer], ln_mix_b[layer], mlp_w1[layer], mlp_w2[layer],
                            ln_ffn_g[layer], ln_ffn_b[layer], ROW_TILE)
    return x2d.reshape(bsz, s, d)
```

```python
import functools
import math

import jax
import jax.numpy as jnp
import numpy as np
from jax import lax
from jax.experimental import pallas as pl
from jax.experimental.pallas import tpu as pltpu

F32 = jnp.float32
MXU_DTYPE = jnp.bfloat16

LANES = 128
SUBLANES = 8
VMEM_LIMIT_BYTES = 56 * 1024 * 1024

A_HEADS = 8
A_HEAD_DIM = 64
A_WIDTH = A_HEADS * A_HEAD_DIM
MOBA_BLOCK = 256
MOBA_TOPK = 3
REL_BUCKETS = 32
REL_MAX_DIST = 128
S5_GROUP = 16
S5_GROUPS = 32
S5_STATE = 64
S5_WIDTH = S5_GROUP * S5_GROUPS
C_HEADS = 8
C_HEAD_DIM = 128
C_WIDTH = C_HEADS * C_HEAD_DIM
C_CONV = 4
C_CHUNK = 64
DEPTH = 2
DN_ALPHA = (2 * DEPTH) ** 0.25
LN_EPS = 1e-5
RMS_EPS = 1e-6
L2_EPS = 1e-6
NEG_INF = -1e30
LOG2E = math.log2(math.e)


def _mx(t):
    return t.astype(MXU_DTYPE)


def _dot(a, b):
    return jnp.dot(_mx(a), _mx(b), preferred_element_type=F32)


def _dot_tn(a, b):
    return lax.dot_general(_mx(a), _mx(b), (((0,), (0,)), ((), ())), preferred_element_type=F32)


def _dot_nt(a, b):
    return lax.dot_general(_mx(a), _mx(b), (((1,), (1,)), ((), ())), preferred_element_type=F32)


def _sigmoid(t):
    return 1.0 / (1.0 + jnp.exp(-t))


def _layer_norm(r, g, b):
    mu = jnp.mean(r, axis=-1, keepdims=True)
    d = r - mu
    var = jnp.mean(d * d, axis=-1, keepdims=True)
    return d * lax.rsqrt(var + LN_EPS) * g + b


def _params(*sem):
    return pltpu.CompilerParams(dimension_semantics=sem, vmem_limit_bytes=VMEM_LIMIT_BYTES)


def _resident(shape):
    return pl.BlockSpec(shape, lambda *_: (0,) * len(shape), pipeline_mode=pl.Buffered(1))


PROJ_CHUNK = 1024


def _proj_kernel(x_ref, w_ref, o_ref):
    xm = _mx(x_ref[...])
    n = w_ref.shape[1]
    for c0 in range(0, n, PROJ_CHUNK):
        c1 = min(c0 + PROJ_CHUNK, n)
        o_ref[:, c0:c1] = jnp.dot(xm, w_ref[:, c0:c1], preferred_element_type=F32)


def _proj(x2d, w, tm):
    m, k = x2d.shape
    n = w.shape[1]
    return pl.pallas_call(
        _proj_kernel,
        grid=(m // tm,),
        in_specs=[pl.BlockSpec((tm, k), lambda i: (i, 0)), _resident((k, n))],
        out_specs=pl.BlockSpec((tm, n), lambda i: (i, 0)),
        out_shape=jax.ShapeDtypeStruct((m, n), F32),
        compiler_params=_params("parallel"),
        name="in_proj",
    )(x2d, w)


def _t5_bucket_np(dist):
    max_exact = REL_BUCKETS // 2
    n = np.maximum(dist, 0)
    nf = np.maximum(n, 1).astype(np.float32)
    large = max_exact + (np.log(nf / np.float32(max_exact)) / np.float32(math.log(REL_MAX_DIST / max_exact))
                         * np.float32(REL_BUCKETS - max_exact)).astype(np.int32)
    large = np.minimum(large, REL_BUCKETS - 1)
    return np.where(n < max_exact, n, large).astype(np.int32)


def _bias_kernel(tab_ref, bkt_ref, o_ref):
    h = pl.program_id(0)
    blk = bkt_ref.shape[-1]
    key = lax.broadcasted_iota(jnp.int32, (blk, blk), 0)
    qry = lax.broadcasted_iota(jnp.int32, (blk, blk), 1)
    far = tab_ref[h, REL_BUCKETS - 1]
    for t in range(2):
        bk = bkt_ref[t]
        acc = jnp.zeros((blk, blk), F32)
        for i in range(REL_BUCKETS):
            acc = jnp.where(bk == i, (tab_ref[h, i] - far) * LOG2E, acc)
        if t == 0:
            acc = jnp.where(qry >= key, acc, NEG_INF)
        o_ref[0, t] = acc


def _bias_tiles(table_t):
    heads = table_t.shape[0]
    offs = np.arange(MOBA_BLOCK)
    dist_own = offs[None, :] - offs[:, None]
    buckets = np.stack([_t5_bucket_np(dist_own), _t5_bucket_np(dist_own + MOBA_BLOCK)])
    return pl.pallas_call(
        _bias_kernel,
        grid=(heads,),
        in_specs=[pl.BlockSpec(memory_space=pltpu.SMEM),
                  pl.BlockSpec((2, MOBA_BLOCK, MOBA_BLOCK), lambda h: (0, 0, 0))],
        out_specs=pl.BlockSpec((1, 2, MOBA_BLOCK, MOBA_BLOCK), lambda h: (h, 0, 0, 0)),
        out_shape=jax.ShapeDtypeStruct((heads, 2, MOBA_BLOCK, MOBA_BLOCK), F32),
        compiler_params=_params("parallel"),
        name="t5_bias",
    )(table_t, jnp.asarray(buckets))


def _moba_kernel(q_ref, k_ref, v_ref, bias_ref, o_ref, kmean_ref, kb_ref, vt_ref,
                 st0_ref, st1_ref, p0_ref, p1_ref, wq_ref, acc_ref):
    st_refs = (st0_ref, st1_ref)
    p_refs = (p0_ref, p1_ref)
    qb = pl.program_id(2)
    blk = MOBA_BLOCK
    d = A_HEAD_DIM
    nh = LANES // d
    nb = k_ref.shape[1] // blk

    @pl.when(qb == 0)
    def _():
        lane = lax.broadcasted_iota(jnp.int32, (blk, LANES), 1)
        for j in range(nb + 1):
            kb_ref[j, :, LANES:2 * LANES] = jnp.where(lane == j, 1.0, 0.0).astype(MXU_DTYPE)
        for j in range(nb):
            kj = k_ref[0, j * blk:(j + 1) * blk, :]
            kb_ref[j, :, 0:LANES] = _mx(kj)
            kmean_ref[j:j + 1, :] = jnp.mean(kj, axis=0, keepdims=True)
            vt_ref[j] = _mx(v_ref[0, j * blk:(j + 1) * blk, :].T)
        kb_ref[nb, :, 0:LANES] = jnp.zeros((blk, LANES), MXU_DTYPE)
        vt_ref[nb] = jnp.zeros((LANES, blk), MXU_DTYPE)

    qt = (q_ref[0] * (d ** -0.5)).T
    zero = jnp.zeros((d, blk), F32)
    q_rows = jnp.concatenate([jnp.concatenate([qt[:d], zero], axis=1),
                              jnp.concatenate([zero, qt[d:]], axis=1)], axis=0) * LOG2E

    blk_id = lax.broadcasted_iota(jnp.int32, (nb, blk), 0)
    mask_rows = []
    for hh in range(nh):
        gate = lax.dot_general(kmean_ref[:, hh * d:(hh + 1) * d], qt[hh * d:(hh + 1) * d],
                               (((1,), (0,)), ((), ())),
                               precision=lax.Precision.HIGHEST, preferred_element_type=F32)
        g = jnp.where(blk_id < qb, gate, -jnp.inf)
        sel = jnp.zeros((nb, blk), F32)
        for _ in range(MOBA_TOPK):
            m = jnp.max(g, axis=0, keepdims=True)
            is_m = jnp.logical_and(g == m, g > -jnp.inf)
            idx = jnp.min(jnp.where(is_m, blk_id, nb), axis=0, keepdims=True)
            pick = blk_id == idx
            sel = jnp.where(pick, 1.0, sel)
            g = jnp.where(pick, -jnp.inf, g)
        keep = jnp.logical_or(sel > 0.0, blk_id == qb)
        mask_rows.append(jnp.where(keep, 0.0, NEG_INF))

    row = lax.broadcasted_iota(jnp.int32, (LANES - nb, nh * blk), 0)
    wq_ref[...] = _mx(jnp.concatenate([q_rows, jnp.concatenate(mask_rows, axis=1),
                                       jnp.where(row == 0, NEG_INF, 0.0)], axis=0))

    def score_stage(slot, ja, jb, with_bias):
        keys = jnp.concatenate([kb_ref[ja], kb_ref[jb]], axis=0)
        st = jnp.dot(keys, wq_ref[...], preferred_element_type=F32)
        if with_bias:
            st = st + jnp.concatenate([jnp.concatenate([bias_ref[hh, w] for hh in range(nh)], axis=1)
                                       for w in range(2)], axis=0)
        st_refs[slot][0] = st[:blk]
        st_refs[slot][1] = st[blk:]
        return jnp.max(st, axis=0, keepdims=True)

    def softmax_stage(slot, cmax, stats):
        new = []
        for hh in range(nh):
            cs = slice(hh * blk, (hh + 1) * blk)
            m_new = cmax[:, cs]
            if stats is not None:
                m, l = stats[hh]
                m_new = jnp.maximum(m, m_new)
            ps = [jnp.exp2(st_refs[slot][w, :, cs] - m_new) for w in range(2)]
            p_refs[slot][hh, 0:blk, :] = _mx(ps[0])
            p_refs[slot][hh, blk:2 * blk, :] = _mx(ps[1])
            l_new = jnp.sum(ps[0], axis=0, keepdims=True) + jnp.sum(ps[1], axis=0, keepdims=True)
            if stats is None:
                new.append((m_new, l_new, None))
            else:
                alpha = jnp.exp2(m - m_new)
                new.append((m_new, alpha * l + l_new, alpha))
        return new

    def value_stage(slot, ja, jb):
        return [jnp.dot(jnp.concatenate([vt_ref[ja, hh * d:(hh + 1) * d, :], vt_ref[jb, hh * d:(hh + 1) * d, :]],
                                        axis=1), p_refs[slot][hh], preferred_element_type=F32)
                for hh in range(nh)]

    nfar = qb - 1
    npairs = (nfar + 1) // 2

    def far_blocks(i):
        return jnp.where(2 * i < nfar, 2 * i, nb), jnp.where(2 * i + 1 < nfar, 2 * i + 1, nb)

    j_prev = jnp.where(qb > 0, qb - 1, nb)
    cmax0 = score_stage(0, qb, j_prev, True)
    cmax1 = score_stage(1, *far_blocks(0), False)
    st0 = softmax_stage(0, cmax0, None)
    for hh, pv_h in enumerate(value_stage(0, qb, j_prev)):
        acc_ref[hh] = pv_h
    init = (tuple((m, l) for m, l, _ in st0), cmax1)

    def trip(slot, k, carry):
        stats, cmax = carry
        cmax_next = score_stage(1 - slot, *far_blocks(k), False)
        new = softmax_stage(slot, cmax, stats)
        pv = value_stage(slot, *far_blocks(k - 1))
        for hh, ((_, _, alpha), pv_h) in enumerate(zip(new, pv)):
            acc_ref[hh] = acc_ref[hh] * alpha + pv_h
        return tuple((m, l) for m, l, _ in new), cmax_next

    def trip_dyn(k, carry):
        return lax.cond(k % 2 == 1, functools.partial(trip, 1, k), functools.partial(trip, 0, k), carry)

    stats, _ = lax.fori_loop(1, npairs + 1, trip_dyn, init)
    out_t = jnp.concatenate([acc_ref[hh] / stats[hh][1] for hh in range(nh)], axis=0)
    o_ref[0] = out_t.T


def _moba(hp3d, bias):
    b, s, _ = hp3d.shape
    npair = A_WIDTH // LANES
    hpp = LANES // A_HEAD_DIM
    nb = s // MOBA_BLOCK
    return pl.pallas_call(
        _moba_kernel,
        grid=(b, npair, nb),
        in_specs=[pl.BlockSpec((1, MOBA_BLOCK, LANES), lambda bi, p, i: (bi, i, p)),
                  pl.BlockSpec((1, s, LANES), lambda bi, p, i: (bi, 0, npair + p)),
                  pl.BlockSpec((1, s, LANES), lambda bi, p, i: (bi, 0, 2 * npair + p)),
                  pl.BlockSpec((hpp, 2, MOBA_BLOCK, MOBA_BLOCK), lambda bi, p, i: (p, 0, 0, 0))],
        out_specs=pl.BlockSpec((1, MOBA_BLOCK, LANES), lambda bi, p, i: (bi, i, p)),
        out_shape=jax.ShapeDtypeStruct((b, s, A_WIDTH), F32),
        scratch_shapes=[pltpu.VMEM((nb, LANES), F32),
                        pltpu.VMEM((nb + 1, MOBA_BLOCK, 2 * LANES), MXU_DTYPE),
                        pltpu.VMEM((nb + 1, LANES, MOBA_BLOCK), MXU_DTYPE),
                        pltpu.VMEM((2, MOBA_BLOCK, hpp * MOBA_BLOCK), F32),
                        pltpu.VMEM((2, MOBA_BLOCK, hpp * MOBA_BLOCK), F32),
                        pltpu.VMEM((hpp, 2 * MOBA_BLOCK, MOBA_BLOCK), MXU_DTYPE),
                        pltpu.VMEM((hpp, 2 * MOBA_BLOCK, MOBA_BLOCK), MXU_DTYPE),
                        pltpu.VMEM((2 * LANES, hpp * MOBA_BLOCK), MXU_DTYPE),
                        pltpu.VMEM((hpp, A_HEAD_DIM, MOBA_BLOCK), F32)],
        compiler_params=_params("parallel", "parallel", "arbitrary"),
        name="moba",
    )(hp3d, hp3d, hp3d, bias)


S5_HALF_CH = S5_WIDTH // 2
S5_HALF_ST = (S5_GROUPS // 2) * S5_STATE
S5_TILE = 256
S5_NT = S5_HALF_ST // S5_TILE
S5_STEPS = 64
S5_PITCH = S5_STEPS + SUBLANES
S5_SLABS = 2 * S5_HALF_ST // LANES


def _s5_kernel(u_ref, wx_ref, wy_ref, a_ref, o_ref, st_ref, x_scr, h_scr):
    nb, steps, _ = u_ref.shape
    nseq = 2 * nb
    tile_slabs = S5_TILE // LANES

    @pl.when(pl.program_id(0) == 0)
    def _():
        st_ref[...] = jnp.zeros_like(st_ref)

    zeros = jnp.zeros((steps, LANES), MXU_DTYPE)
    for m in range(S5_NT // 2):
        pieces = []
        for seq in range(nseq):
            bi, half = divmod(seq, 2)
            c = half * S5_HALF_CH + m * LANES
            piece = _mx(u_ref[bi, :, c:c + LANES])
            pieces.append(jnp.concatenate([piece, zeros] if half == 0 else [zeros, piece], axis=1))
        lhs = jnp.concatenate(pieces, axis=0)
        for part in range(2):
            for n in (2 * m, 2 * m + 1):
                res = jnp.dot(lhs, wx_ref[part, n], preferred_element_type=F32)
                slab0 = (part * S5_HALF_ST + n * S5_TILE) // LANES
                for seq in range(nseq):
                    for sl in range(tile_slabs):
                        x_scr[slab0 + sl, seq * S5_PITCH:seq * S5_PITCH + steps, :] = (
                            res[seq * steps:(seq + 1) * steps, sl * LANES:(sl + 1) * LANES])

    nre = S5_SLABS // 2
    a_re = [a_ref[0, :, s * LANES:(s + 1) * LANES] for s in range(nre)]
    a_im = [a_ref[1, :, s * LANES:(s + 1) * LANES] for s in range(nre)]

    def step(t, carry):
        new_re, new_im = [], []
        for s in range(nre):
            hr, hi = carry[s], carry[nre + s]
            rows_t = pl.ds(t, nseq, stride=S5_PITCH)
            nhr = a_re[s] * hr - a_im[s] * hi + x_scr[s, rows_t, :]
            nhi = a_re[s] * hi + a_im[s] * hr + x_scr[nre + s, rows_t, :]
            h_scr[s, rows_t, :] = nhr
            h_scr[nre + s, rows_t, :] = nhi
            new_re.append(nhr)
            new_im.append(nhi)
        return tuple(new_re + new_im)

    init = tuple(st_ref[p, :, s * LANES:(s + 1) * LANES] for p in range(2) for s in range(nre))
    final = lax.fori_loop(0, steps, step, init, unroll=2)
    for p in range(2):
        for s in range(nre):
            st_ref[p, :, s * LANES:(s + 1) * LANES] = final[p * nre + s]

    for m in range(S5_NT // 2):
        acc = jnp.zeros((nseq * steps, 2 * LANES), F32)
        for n in (2 * m, 2 * m + 1):
            for part in range(2):
                slab0 = (part * S5_HALF_ST + n * S5_TILE) // LANES
                h_tile = jnp.concatenate(
                    [jnp.concatenate([h_scr[slab0 + sl, seq * S5_PITCH:seq * S5_PITCH + steps, :]
                                      for sl in range(tile_slabs)], axis=1) for seq in range(nseq)], axis=0)
                acc = acc + jnp.dot(_mx(h_tile), wy_ref[part, n], preferred_element_type=F32)
        for seq in range(nseq):
            bi, half = divmod(seq, 2)
            c = half * S5_HALF_CH + m * LANES
            o_ref[bi, :, c:c + LANES] = acc[seq * steps:(seq + 1) * steps, half * LANES:(half + 1) * LANES]


def _s5_weights(lam_re, lam_im, b_re, b_im, c_re, c_im, log_dt):
    lr = jnp.minimum(lam_re.astype(F32), -1e-4)
    li = lam_im.astype(F32)
    dt = jnp.exp(log_dt.astype(F32))[:, None]
    mag = jnp.exp(lr * dt)
    a_re = mag * jnp.cos(li * dt)
    a_im = mag * jnp.sin(li * dt)
    den = lr * lr + li * li
    coef_re = ((a_re - 1.0) * lr + a_im * li) / den
    coef_im = (a_im * lr - (a_re - 1.0) * li) / den
    br = b_re.astype(F32)
    bi = b_im.astype(F32)
    bb_re = coef_re[..., None] * br - coef_im[..., None] * bi
    bb_im = coef_re[..., None] * bi + coef_im[..., None] * br
    gh = S5_GROUPS // 2
    eye = jnp.eye(gh, dtype=F32)

    def b_dense(bb):
        t = bb.reshape(2, gh, S5_STATE, S5_GROUP)
        return jnp.einsum('fgph,gk->fghkp', t, eye).reshape(2, S5_HALF_CH, S5_HALF_ST)

    def c_dense(cc):
        t = cc.reshape(2, gh, S5_GROUP, S5_STATE)
        return jnp.einsum('fghp,gk->fgpkh', t, eye).reshape(2, S5_HALF_ST, S5_HALF_CH)

    def x_tiles(bd):
        tiles = []
        for n in range(S5_NT):
            c = (n // 2) * LANES
            tiles.append(jnp.concatenate([bd[0, c:c + LANES, n * S5_TILE:(n + 1) * S5_TILE],
                                          bd[1, c:c + LANES, n * S5_TILE:(n + 1) * S5_TILE]], axis=0))
        return jnp.stack(tiles)

    def y_tiles(cd):
        tiles = []
        for n in range(S5_NT):
            c = (n // 2) * LANES
            tiles.append(jnp.concatenate([cd[0, n * S5_TILE:(n + 1) * S5_TILE, c:c + LANES],
                                          cd[1, n * S5_TILE:(n + 1) * S5_TILE, c:c + LANES]], axis=1))
        return jnp.stack(tiles)

    wx = jnp.stack([x_tiles(b_dense(bb_re)), x_tiles(b_dense(bb_im))]).astype(MXU_DTYPE)
    wy = jnp.stack([y_tiles(c_dense(c_re.astype(F32))), y_tiles(c_dense(-c_im.astype(F32)))]).astype(MXU_DTYPE)
    return wx, wy, a_re.reshape(2, S5_HALF_ST), a_im.reshape(2, S5_HALF_ST)


def _s5_scan(hp3d, wx, wy, a_re2, a_im2):
    b, s, width = hp3d.shape
    nseq = 2 * b
    assert nseq == SUBLANES
    sel = jnp.arange(nseq) % 2
    a = jnp.stack([a_re2[sel], a_im2[sel]])
    scr = pltpu.VMEM((S5_SLABS, nseq * S5_PITCH, LANES), F32)
    return pl.pallas_call(
        _s5_kernel,
        grid=(s // S5_STEPS,),
        in_specs=[pl.BlockSpec((b, S5_STEPS, S5_WIDTH), lambda i: (0, i, width // S5_WIDTH - 1)),
                  _resident(wx.shape), _resident(wy.shape), _resident(a.shape)],
        out_specs=pl.BlockSpec((b, S5_STEPS, S5_WIDTH), lambda i: (0, i, 0)),
        out_shape=jax.ShapeDtypeStruct((b, s, S5_WIDTH), F32),
        scratch_shapes=[pltpu.VMEM((2, nseq, S5_HALF_ST), F32), scr, scr],
        compiler_params=_params("arbitrary"),
        name="s5_scan",
    )(hp3d, wx, wy, a)


MLP_CHUNK = 1024


def _tail_body(mix, x_ref, g1_ref, b1_ref, w1_ref, w2_ref, g2_ref, b2_ref, o_ref):
    x1 = _layer_norm(DN_ALPHA * x_ref[...] + mix, g1_ref[...], b1_ref[...])
    x1m = _mx(x1)
    acc = jnp.zeros(x1.shape, F32)
    for f in range(w1_ref.shape[1] // MLP_CHUNK):
        hid = jnp.dot(x1m, w1_ref[:, f * MLP_CHUNK:(f + 1) * MLP_CHUNK], preferred_element_type=F32)
        hid = jnp.square(jnp.maximum(hid, 0.0))
        acc = acc + jnp.dot(_mx(hid), w2_ref[f * MLP_CHUNK:(f + 1) * MLP_CHUNK, :],
                            preferred_element_type=F32)
    o_ref[...] = _layer_norm(DN_ALPHA * x1 + acc, g2_ref[...], b2_ref[...])


def _tail_even_kernel(attn_ref, y_ref, u_ref, dsk_ref, wg_ref, bg_ref, wo_ref, x_ref,
                      g1_ref, b1_ref, w1_ref, w2_ref, g2_ref, b2_ref, o_ref):
    y = y_ref[...] + dsk_ref[...] * u_ref[...]
    g = 0.5 * y * (1.0 + jnp.tanh(math.sqrt(2.0 / math.pi) * (y + 0.044715 * (y * y * y))))
    ssm = g * _sigmoid(jnp.dot(_mx(g), wg_ref[...], preferred_element_type=F32) + bg_ref[...])
    ka = attn_ref.shape[1]
    mix = (jnp.dot(_mx(attn_ref[...]), wo_ref[0:ka, :], preferred_element_type=F32)
           + jnp.dot(_mx(ssm), wo_ref[ka:, :], preferred_element_type=F32))
    _tail_body(mix, x_ref, g1_ref, b1_ref, w1_ref, w2_ref, g2_ref, b2_ref, o_ref)


def _tail_odd_kernel(a_ref, wo_ref, x_ref, g1_ref, b1_ref, w1_ref, w2_ref, g2_ref, b2_ref, o_ref):
    mix = jnp.dot(_mx(a_ref[...]), wo_ref[...], preferred_element_type=F32)
    _tail_body(mix, x_ref, g1_ref, b1_ref, w1_ref, w2_ref, g2_ref, b2_ref, o_ref)


def _row(v):
    return v.reshape(1, -1).astype(F32)


def _tail_common_specs(tm, d, dff):
    return [_resident((d, d)),
            pl.BlockSpec((tm, d), lambda i: (i, 0)),
            _resident((1, d)), _resident((1, d)),
            _resident((d, dff)), _resident((dff, d)),
            _resident((1, d)), _resident((1, d))]


def _tail_even(attn2d, y2d, hp2d, x2d, d_skip, w_glu, b_glu, w_out, g1, b1, w1, w2, g2, b2, tm):
    m, d = x2d.shape
    dff = w1.shape[1]
    ucol = hp2d.shape[1] // S5_WIDTH - 1
    return pl.pallas_call(
        _tail_even_kernel,
        grid=(m // tm,),
        in_specs=[pl.BlockSpec((tm, A_WIDTH), lambda i: (i, 0)),
                  pl.BlockSpec((tm, S5_WIDTH), lambda i: (i, 0)),
                  pl.BlockSpec((tm, S5_WIDTH), lambda i: (i, ucol)),
                  _resident((1, S5_WIDTH)), _resident((S5_WIDTH, S5_WIDTH)), _resident((1, S5_WIDTH))]
        + _tail_common_specs(tm, d, dff),
        out_specs=pl.BlockSpec((tm, d), lambda i: (i, 0)),
        out_shape=jax.ShapeDtypeStruct((m, d), F32),
        compiler_params=_params("parallel"),
        name="tail_even",
    )(attn2d, y2d, hp2d, _row(d_skip), _mx(w_glu), _row(b_glu), _mx(w_out), x2d,
      _row(g1), _row(b1), _mx(w1), _mx(w2), _row(g2), _row(b2))


def _tail_odd(a2d, x2d, w_out, g1, b1, w1, w2, g2, b2, tm):
    m, d = x2d.shape
    dff = w1.shape[1]
    return pl.pallas_call(
        _tail_odd_kernel,
        grid=(m // tm,),
        in_specs=[pl.BlockSpec((tm, a2d.shape[1]), lambda i: (i, 0))] + _tail_common_specs(tm, d, dff),
        out_specs=pl.BlockSpec((tm, d), lambda i: (i, 0)),
        out_shape=jax.ShapeDtypeStruct((m, d), F32),
        compiler_params=_params("parallel"),
        name="tail_odd",
    )(a2d, _mx(w_out), x2d, _row(g1), _row(b1), _mx(w1), _mx(w2), _row(g2), _row(b2))


GDN_PREP_TILE = 512
GDN_SCAN_TILE = 256


def _softplus(t):
    return jnp.maximum(t, 0.0) + jnp.log1p(jnp.exp(-jnp.abs(t)))


def _gdn_prep_kernel(q_ref, k_ref, v_ref, qp_ref, kp_ref, vp_ref, cwq_ref, cwk_ref, cwv_ref,
                     ba_ref, alog_ref, dtb_ref, u_ref, w_ref, qd_ref, kd_ref, intra_ref, edl_ref):
    i = pl.program_id(1)
    h = pl.program_id(2)
    t = q_ref.shape[1]
    c = C_CHUNK
    dk = C_HEAD_DIM

    def conv_silu(main_ref, prev_ref, cw_ref):
        x = main_ref[0]
        pv = jnp.where(i > 0, prev_ref[0], 0.0)
        ext = jnp.concatenate([pv, x], axis=0)
        y = cw_ref[C_CONV - 1:C_CONV, :] * x
        for back in range(1, C_CONV):
            shifted = pltpu.roll(ext, back, 0)[SUBLANES:, :]
            y = y + cw_ref[C_CONV - 1 - back:C_CONV - back, :] * shifted
        return y * _sigmoid(y)

    def l2n(z):
        return z * lax.rsqrt(jnp.sum(z * z, axis=-1, keepdims=True) + L2_EPS)

    q = l2n(conv_silu(q_ref, qp_ref, cwq_ref)) * (dk ** -0.5)
    k = l2n(conv_silu(k_ref, kp_ref, cwk_ref))
    v = conv_silu(v_ref, vp_ref, cwv_ref)

    ba = ba_ref[0]
    lane = lax.broadcasted_iota(jnp.int32, (t, LANES), 1)
    beta = _sigmoid(jnp.sum(jnp.where(lane == h, ba, 0.0), axis=1, keepdims=True))
    g_all = -jnp.exp(alog_ref[...]) * _softplus(ba + dtb_ref[...])
    g_col = jnp.sum(jnp.where(lane == C_HEADS + h, g_all, 0.0), axis=1, keepdims=True)
    sub = lax.broadcasted_iota(jnp.int32, (LANES, t), 0)
    g_row = jnp.sum(jnp.where(sub == C_HEADS + h, g_all.T, 0.0), axis=0, keepdims=True)

    ii = lax.broadcasted_iota(jnp.int32, (c, c), 0)
    jj = lax.broadcasted_iota(jnp.int32, (c, c), 1)
    low = ii >= jj
    chunks = [slice(ch * c, (ch + 1) * c) for ch in range(t // c)]
    gc_cols = [jnp.sum(jnp.where(low, g_row[:, r], 0.0), axis=1, keepdims=True) for r in chunks]
    gc_rows = [jnp.sum(jnp.where(ii <= jj, g_col[r], 0.0), axis=0, keepdims=True) for r in chunks]
    decays = [jnp.where(low, jnp.exp(jnp.where(low, a - b, 0.0)), 0.0) for a, b in zip(gc_cols, gc_rows)]
    kbs = [k[r] * beta[r] for r in chunks]
    kks = [_dot_nt(kb, k[r]) for kb, r in zip(kbs, chunks)]
    qks = [_dot_nt(q[r], k[r]) for r in chunks]
    pws = [jnp.where(ii > jj, -(kk * dec), 0.0) for kk, dec in zip(kks, decays)]
    tms = pws
    for _ in range(int(math.log2(c)) - 1):
        pws = [_dot(pw, pw) for pw in pws]
        tms = [tm + pw + _dot(tm, pw) for tm, pw in zip(tms, pws)]
    egcs = [jnp.exp(a) for a in gc_cols]
    vbs = [v[r] * beta[r] for r in chunks]
    kbes = [kb * e for kb, e in zip(kbs, egcs)]
    us = [vb + _dot(tm, vb) for tm, vb in zip(tms, vbs)]
    ws = [kbe + _dot(tm, kbe) for tm, kbe in zip(tms, kbes)]
    for n, r in enumerate(chunks):
        gl = gc_cols[n][c - 1:c, :]
        u_ref[0, r, :] = us[n]
        w_ref[0, r, :] = _mx(ws[n])
        qd_ref[0, r, :] = _mx(q[r] * egcs[n])
        kd_ref[0, r, :] = _mx(k[r] * jnp.exp(gl - gc_cols[n]))
        intra_ref[0, 0, r, :] = _mx(qks[n] * decays[n])
        edl_ref[0, 0, n:n + 1, :] = jnp.broadcast_to(jnp.exp(gl), (1, LANES))


def _gdn_prep(hp3d, conv_w, alog_row, dtb_row):
    b, s, _ = hp3d.shape
    ba_col = 4 * C_WIDTH // LANES
    t = GDN_PREP_TILE
    hds = C_HEADS
    per = t // SUBLANES

    def main(off):
        return pl.BlockSpec((1, t, LANES), lambda bi, i, h: (bi, i, off + h))

    def prev(off):
        return pl.BlockSpec((1, SUBLANES, LANES),
                            lambda bi, i, h: (bi, jnp.maximum(i * per - 1, 0), off + h))

    def cw(off):
        return pl.BlockSpec((C_CONV, LANES), lambda bi, i, h: (0, off + h))

    out_main = pl.BlockSpec((1, t, LANES), lambda bi, i, h: (bi, i, h))
    full = jax.ShapeDtypeStruct((b, s, C_WIDTH), F32)
    half = jax.ShapeDtypeStruct((b, s, C_WIDTH), MXU_DTYPE)
    nc = t // C_CHUNK
    return pl.pallas_call(
        _gdn_prep_kernel,
        grid=(b, s // t, hds),
        in_specs=[main(0), main(hds), main(2 * hds), prev(0), prev(hds), prev(2 * hds),
                  cw(0), cw(hds), cw(2 * hds),
                  pl.BlockSpec((1, t, LANES), lambda bi, i, h: (bi, i, ba_col)),
                  pl.BlockSpec((1, LANES), lambda bi, i, h: (0, 0)),
                  pl.BlockSpec((1, LANES), lambda bi, i, h: (0, 0))],
        out_specs=[out_main, out_main, out_main, out_main,
                   pl.BlockSpec((1, 1, t, C_CHUNK), lambda bi, i, h: (bi, h, i, 0)),
                   pl.BlockSpec((1, 1, nc, LANES), lambda bi, i, h: (bi, h, i, 0))],
        out_shape=[full, half, half, half,
                   jax.ShapeDtypeStruct((b, hds, s, C_CHUNK), MXU_DTYPE),
                   jax.ShapeDtypeStruct((b, hds, s // C_CHUNK, LANES), F32)],
        compiler_params=_params("parallel", "parallel", "parallel"),
        name="gdn_prep",
    )(hp3d, hp3d, hp3d, hp3d, hp3d, hp3d, conv_w, conv_w, conv_w, hp3d, alog_row, dtb_row)


def _gdn_scan_kernel(u_ref, w_ref, qd_ref, kd_ref, intra_ref, edl_ref, gate_ref, nw_ref, o_ref, st_ref):
    c = C_CHUNK
    dk = C_HEAD_DIM
    t = u_ref.shape[1]
    chunk0 = pl.program_id(1) * (t // c)

    @pl.when(pl.program_id(1) == 0)
    def _():
        st_ref[...] = jnp.zeros_like(st_ref)

    nw = nw_ref[...]
    heads = [slice(h * dk, (h + 1) * dk) for h in range(C_HEADS)]
    for ch in range(t // c):
        r = slice(ch * c, (ch + 1) * c)
        sts = [st_ref[h] for h in range(C_HEADS)]
        wss = [_dot(jnp.concatenate([w_ref[0, r, cs], qd_ref[0, r, cs]], axis=0), st)
               for cs, st in zip(heads, sts)]
        v_news = [u_ref[0, r, cs] - ws[:c] for cs, ws in zip(heads, wss)]
        kvs = [_dot_tn(kd_ref[0, r, cs], v_new) for cs, v_new in zip(heads, v_news)]
        os_ = [ws[c:] + _dot(intra_ref[0, h, r, :], v_new) for h, (ws, v_new) in enumerate(zip(wss, v_news))]
        for h, cs in enumerate(heads):
            st_ref[h] = sts[h] * edl_ref[0, h, pl.ds(chunk0 + ch, 1), 0:1] + kvs[h]
            o = os_[h]
            o = o * lax.rsqrt(jnp.mean(o * o, axis=-1, keepdims=True) + RMS_EPS) * nw
            gt = gate_ref[0, r, cs]
            o_ref[0, r, cs] = o * (gt * _sigmoid(gt))


def _gdn_scan(u, w, qd, kd, intra, edl, hp3d, norm_w):
    b, s, cw = u.shape
    t = GDN_SCAN_TILE
    blk = pl.BlockSpec((1, t, cw), lambda bi, i: (bi, i, 0))
    return pl.pallas_call(
        _gdn_scan_kernel,
        grid=(b, s // t),
        in_specs=[blk, blk, blk, blk,
                  pl.BlockSpec((1, C_HEADS, t, C_CHUNK), lambda bi, i: (bi, 0, i, 0)),
                  pl.BlockSpec((1, C_HEADS, s // C_CHUNK, LANES), lambda bi, i: (bi, 0, 0, 0)),
                  pl.BlockSpec((1, t, cw), lambda bi, i: (bi, i, 3)),
                  pl.BlockSpec((1, C_HEAD_DIM), lambda bi, i: (0, 0))],
        out_specs=blk,
        out_shape=jax.ShapeDtypeStruct((b, s, cw), F32),
        scratch_shapes=[pltpu.VMEM((C_HEADS, C_HEAD_DIM, C_HEAD_DIM), F32)],
        compiler_params=_params("parallel", "arbitrary"),
        name="gdn_scan",
    )(u, w, qd, kd, intra, edl, hp3d, _row(norm_w))


ROW_TILE = 512


def kernel(x, rel_table, ev_w_in, ev_w_out, s5_lambda_re, s5_lambda_im, s5_b_re, s5_b_im, s5_c_re, s5_c_im, s5_d, s5_log_dt, s5_w_glu, s5_b_glu, od_w_in, od_conv_w, od_a_log, od_dt_bias, od_norm_w, od_w_out, ln_mix_g, ln_mix_b, mlp_w1, mlp_w2, ln_ffn_g, ln_ffn_b):
    bsz, s, d = x.shape
    m = bsz * s
    x2d = x.reshape(m, d)
    table_t = rel_table.T.astype(F32)
    bias = _bias_tiles(table_t)
    for layer in range(DEPTH):
        i = layer // 2
        if layer % 2 == 0:
            hp = _proj(x2d, _mx(ev_w_in[i]), ROW_TILE)
            hp3d = hp.reshape(bsz, s, hp.shape[1])
            attn = _moba(hp3d, bias)
            wx, wy, a_re2, a_im2 = _s5_weights(s5_lambda_re[i], s5_lambda_im[i], s5_b_re[i], s5_b_im[i],
                                               s5_c_re[i], s5_c_im[i], s5_log_dt[i])
            y = _s5_scan(hp3d, wx, wy, a_re2, a_im2)
            x2d = _tail_even(attn.reshape(m, A_WIDTH), y.reshape(m, S5_WIDTH), hp, x2d,
                             s5_d[i], s5_w_glu[i], s5_b_glu[i], ev_w_out[i],
                             ln_mix_g[layer], ln_mix_b[layer], mlp_w1[layer], mlp_w2[layer],
                             ln_ffn_g[layer], ln_ffn_b[layer], ROW_TILE)
        else:
            w_in = od_w_in[i]
            w_all = jnp.pad(w_in, ((0, 0), (0, LANES - 2 * C_HEADS)))
            hp = _proj(x2d, _mx(w_all), ROW_TILE)
            hp3d = hp.reshape(bsz, s, 4 * C_WIDTH + LANES)
            pad8 = jnp.zeros((C_HEADS,), F32)
            padr = jnp.zeros((LANES - 2 * C_HEADS,), F32)
            alog_row = jnp.concatenate([pad8, od_a_log[i].astype(F32), padr]).reshape(1, LANES)
            dtb_row = jnp.concatenate([pad8, od_dt_bias[i].astype(F32), padr]).reshape(1, LANES)
            u, w, qd, kd, intra, edl = _gdn_prep(hp3d, od_conv_w[i].astype(F32), alog_row, dtb_row)
            o = _gdn_scan(u, w, qd, kd, intra, edl, hp3d, od_norm_w[i])
            x2d = _tail_odd(o.reshape(m, C_WIDTH), x2d, od_w_out[i],
                            ln_mix_g[layer], ln_mix_b[layer], mlp_w1[layer], mlp_w2[layer],
                            ln_ffn_g[layer], ln_ffn_b[layer], ROW_TILE)
    return x2d.reshape(bsz, s, d)
```

```python
import functools
import math

import jax
import jax.numpy as jnp
import numpy as np
from jax import lax
from jax.experimental import pallas as pl
from jax.experimental.pallas import tpu as pltpu

F32 = jnp.float32
MXU_DTYPE = jnp.bfloat16

LANES = 128
SUBLANES = 8
VMEM_LIMIT_BYTES = 56 * 1024 * 1024

A_HEADS = 8
A_HEAD_DIM = 64
A_WIDTH = A_HEADS * A_HEAD_DIM
MOBA_BLOCK = 256
MOBA_TOPK = 3
REL_BUCKETS = 32
REL_MAX_DIST = 128
S5_GROUP = 16
S5_GROUPS = 32
S5_STATE = 64
S5_WIDTH = S5_GROUP * S5_GROUPS
C_HEADS = 8
C_HEAD_DIM = 128
C_WIDTH = C_HEADS * C_HEAD_DIM
C_CONV = 4
C_CHUNK = 64
DEPTH = 2
DN_ALPHA = (2 * DEPTH) ** 0.25
LN_EPS = 1e-5
RMS_EPS = 1e-6
L2_EPS = 1e-6
NEG_INF = -1e30
LOG2E = math.log2(math.e)


def _mx(t):
    return t.astype(MXU_DTYPE)


def _dot(a, b):
    return jnp.dot(_mx(a), _mx(b), preferred_element_type=F32)


def _dot_tn(a, b):
    return lax.dot_general(_mx(a), _mx(b), (((0,), (0,)), ((), ())), preferred_element_type=F32)


def _dot_nt(a, b):
    return lax.dot_general(_mx(a), _mx(b), (((1,), (1,)), ((), ())), preferred_element_type=F32)


def _sigmoid(t):
    return 1.0 / (1.0 + jnp.exp(-t))


def _layer_norm(r, g, b):
    mu = jnp.mean(r, axis=-1, keepdims=True)
    d = r - mu
    var = jnp.mean(d * d, axis=-1, keepdims=True)
    return d * lax.rsqrt(var + LN_EPS) * g + b


def _params(*sem):
    return pltpu.CompilerParams(dimension_semantics=sem, vmem_limit_bytes=VMEM_LIMIT_BYTES)


def _resident(shape):
    return pl.BlockSpec(shape, lambda *_: (0,) * len(shape), pipeline_mode=pl.Buffered(1))


PROJ_CHUNK = 1024


def _proj_kernel(x_ref, w_ref, o_ref):
    xm = _mx(x_ref[...])
    n = w_ref.shape[1]
    for c0 in range(0, n, PROJ_CHUNK):
        c1 = min(c0 + PROJ_CHUNK, n)
        o_ref[:, c0:c1] = jnp.dot(xm, w_ref[:, c0:c1], preferred_element_type=F32)


def _proj(x2d, w, tm):
    m, k = x2d.shape
    n = w.shape[1]
    return pl.pallas_call(
        _proj_kernel,
        grid=(m // tm,),
        in_specs=[pl.BlockSpec((tm, k), lambda i: (i, 0)), _resident((k, n))],
        out_specs=pl.BlockSpec((tm, n), lambda i: (i, 0)),
        out_shape=jax.ShapeDtypeStruct((m, n), F32),
        compiler_params=_params("parallel"),
        name="in_proj",
    )(x2d, w)


def _t5_bucket_np(dist):
    max_exact = REL_BUCKETS // 2
    n = np.maximum(dist, 0)
    nf = np.maximum(n, 1).astype(np.float32)
    large = max_exact + (np.log(nf / np.float32(max_exact)) / np.float32(math.log(REL_MAX_DIST / max_exact))
                         * np.float32(REL_BUCKETS - max_exact)).astype(np.int32)
    large = np.minimum(large, REL_BUCKETS - 1)
    return np.where(n < max_exact, n, large).astype(np.int32)


def _bias_kernel(tab_ref, bkt_ref, o_ref):
    h = pl.program_id(0)
    blk = bkt_ref.shape[-1]
    key = lax.broadcasted_iota(jnp.int32, (blk, blk), 0)
    qry = lax.broadcasted_iota(jnp.int32, (blk, blk), 1)
    far = tab_ref[h, REL_BUCKETS - 1]
    for t in range(2):
        bk = bkt_ref[t]
        acc = jnp.zeros((blk, blk), F32)
        for i in range(REL_BUCKETS):
            acc = jnp.where(bk == i, (tab_ref[h, i] - far) * LOG2E, acc)
        if t == 0:
            acc = jnp.where(qry >= key, acc, NEG_INF)
        o_ref[0, t] = acc


def _bias_tiles(table_t):
    heads = table_t.shape[0]
    offs = np.arange(MOBA_BLOCK)
    dist_own = offs[None, :] - offs[:, None]
    buckets = np.stack([_t5_bucket_np(dist_own), _t5_bucket_np(dist_own + MOBA_BLOCK)])
    return pl.pallas_call(
        _bias_kernel,
        grid=(heads,),
        in_specs=[pl.BlockSpec(memory_space=pltpu.SMEM),
                  pl.BlockSpec((2, MOBA_BLOCK, MOBA_BLOCK), lambda h: (0, 0, 0))],
        out_specs=pl.BlockSpec((1, 2, MOBA_BLOCK, MOBA_BLOCK), lambda h: (h, 0, 0, 0)),
        out_shape=jax.ShapeDtypeStruct((heads, 2, MOBA_BLOCK, MOBA_BLOCK), F32),
        compiler_params=_params("parallel"),
        name="t5_bias",
    )(table_t, jnp.asarray(buckets))


def _moba_kernel(q_ref, k_ref, v_ref, bias_ref, o_ref, kmean_ref, kb_ref, vt_ref,
                 st0_ref, st1_ref, p0_ref, p1_ref, wq_ref, acc_ref):
    st_refs = (st0_ref, st1_ref)
    p_refs = (p0_ref, p1_ref)
    qb = pl.program_id(2)
    blk = MOBA_BLOCK
    d = A_HEAD_DIM
    nh = LANES // d
    nb = k_ref.shape[1] // blk

    @pl.when(qb == 0)
    def _():
        lane = lax.broadcasted_iota(jnp.int32, (blk, LANES), 1)
        for j in range(nb + 1):
            kb_ref[j, :, LANES:2 * LANES] = jnp.where(lane == j, 1.0, 0.0).astype(MXU_DTYPE)
        for j in range(nb):
            kj = k_ref[0, j * blk:(j + 1) * blk, :]
            kb_ref[j, :, 0:LANES] = _mx(kj)
            kmean_ref[j:j + 1, :] = jnp.mean(kj, axis=0, keepdims=True)
            vt_ref[j] = _mx(v_ref[0, j * blk:(j + 1) * blk, :].T)
        kb_ref[nb, :, 0:LANES] = jnp.zeros((blk, LANES), MXU_DTYPE)
        vt_ref[nb] = jnp.zeros((LANES, blk), MXU_DTYPE)

    qt = (q_ref[0] * (d ** -0.5)).T
    zero = jnp.zeros((d, blk), F32)
    q_rows = jnp.concatenate([jnp.concatenate([qt[:d], zero], axis=1),
                              jnp.concatenate([zero, qt[d:]], axis=1)], axis=0) * LOG2E

    blk_id = lax.broadcasted_iota(jnp.int32, (nb, blk), 0)
    mask_rows = []
    for hh in range(nh):
        gate = lax.dot_general(kmean_ref[:, hh * d:(hh + 1) * d], qt[hh * d:(hh + 1) * d],
                               (((1,), (0,)), ((), ())),
                               precision=lax.Precision.HIGHEST, preferred_element_type=F32)
        g = jnp.where(blk_id < qb, gate, -jnp.inf)
        sel = jnp.zeros((nb, blk), F32)
        for _ in range(MOBA_TOPK):
            m = jnp.max(g, axis=0, keepdims=True)
            is_m = jnp.logical_and(g == m, g > -jnp.inf)
            idx = jnp.min(jnp.where(is_m, blk_id, nb), axis=0, keepdims=True)
            pick = blk_id == idx
            sel = jnp.where(pick, 1.0, sel)
            g = jnp.where(pick, -jnp.inf, g)
        keep = jnp.logical_or(sel > 0.0, blk_id == qb)
        mask_rows.append(jnp.where(keep, 0.0, NEG_INF))

    row = lax.broadcasted_iota(jnp.int32, (LANES - nb, nh * blk), 0)
    wq_ref[...] = _mx(jnp.concatenate([q_rows, jnp.concatenate(mask_rows, axis=1),
                                       jnp.where(row == 0, NEG_INF, 0.0)], axis=0))

    def score_stage(slot, ja, jb, with_bias):
        keys = jnp.concatenate([kb_ref[ja], kb_ref[jb]], axis=0)
        st = jnp.dot(keys, wq_ref[...], preferred_element_type=F32)
        if with_bias:
            st = st + jnp.concatenate([jnp.concatenate([bias_ref[hh, w] for hh in range(nh)], axis=1)
                                       for w in range(2)], axis=0)
        st_refs[slot][0] = st[:blk]
        st_refs[slot][1] = st[blk:]
        return jnp.max(st, axis=0, keepdims=True)

    def softmax_stage(slot, cmax, stats):
        new = []
        for hh in range(nh):
            cs = slice(hh * blk, (hh + 1) * blk)
            m_new = cmax[:, cs]
            if stats is not None:
                m, l = stats[hh]
                m_new = jnp.maximum(m, m_new)
            ps = [jnp.exp2(st_refs[slot][w, :, cs] - m_new) for w in range(2)]
            p_refs[slot][hh, 0:blk, :] = _mx(ps[0])
            p_refs[slot][hh, blk:2 * blk, :] = _mx(ps[1])
            l_new = jnp.sum(ps[0], axis=0, keepdims=True) + jnp.sum(ps[1], axis=0, keepdims=True)
            if stats is None:
                new.append((m_new, l_new, None))
            else:
                alpha = jnp.exp2(m - m_new)
                new.append((m_new, alpha * l + l_new, alpha))
        return new

    def value_stage(slot, ja, jb):
        return [jnp.dot(jnp.concatenate([vt_ref[ja, hh * d:(hh + 1) * d, :], vt_ref[jb, hh * d:(hh + 1) * d, :]],
                                        axis=1), p_refs[slot][hh], preferred_element_type=F32)
                for hh in range(nh)]

    nfar = qb - 1
    npairs = (nfar + 1) // 2

    def far_blocks(i):
        return jnp.where(2 * i < nfar, 2 * i, nb), jnp.where(2 * i + 1 < nfar, 2 * i + 1, nb)

    j_prev = jnp.where(qb > 0, qb - 1, nb)
    cmax0 = score_stage(0, qb, j_prev, True)
    cmax1 = score_stage(1, *far_blocks(0), False)
    st0 = softmax_stage(0, cmax0, None)
    for hh, pv_h in enumerate(value_stage(0, qb, j_prev)):
        acc_ref[hh] = pv_h
    init = (tuple((m, l) for m, l, _ in st0), cmax1)

    def trip(slot, k, carry):
        stats, cmax = carry
        cmax_next = score_stage(1 - slot, *far_blocks(k), False)
        new = softmax_stage(slot, cmax, stats)
        pv = value_stage(slot, *far_blocks(k - 1))
        for hh, ((_, _, alpha), pv_h) in enumerate(zip(new, pv)):
            acc_ref[hh] = acc_ref[hh] * alpha + pv_h
        return tuple((m, l) for m, l, _ in new), cmax_next

    def trip_dyn(k, carry):
        return lax.cond(k % 2 == 1, functools.partial(trip, 1, k), functools.partial(trip, 0, k), carry)

    stats, _ = lax.fori_loop(1, npairs + 1, trip_dyn, init)
    out_t = jnp.concatenate([acc_ref[hh] / stats[hh][1] for hh in range(nh)], axis=0)
    o_ref[0] = out_t.T


def _moba(hp3d, bias):
    b, s, _ = hp3d.shape
    npair = A_WIDTH // LANES
    hpp = LANES // A_HEAD_DIM
    nb = s // MOBA_BLOCK
    return pl.pallas_call(
        _moba_kernel,
        grid=(b, npair, nb),
        in_specs=[pl.BlockSpec((1, MOBA_BLOCK, LANES), lambda bi, p, i: (bi, i, p)),
                  pl.BlockSpec((1, s, LANES), lambda bi, p, i: (bi, 0, npair + p)),
                  pl.BlockSpec((1, s, LANES), lambda bi, p, i: (bi, 0, 2 * npair + p)),
                  pl.BlockSpec((hpp, 2, MOBA_BLOCK, MOBA_BLOCK), lambda bi, p, i: (p, 0, 0, 0))],
        out_specs=pl.BlockSpec((1, MOBA_BLOCK, LANES), lambda bi, p, i: (bi, i, p)),
        out_shape=jax.ShapeDtypeStruct((b, s, A_WIDTH), F32),
        scratch_shapes=[pltpu.VMEM((nb, LANES), F32),
                        pltpu.VMEM((nb + 1, MOBA_BLOCK, 2 * LANES), MXU_DTYPE),
                        pltpu.VMEM((nb + 1, LANES, MOBA_BLOCK), MXU_DTYPE),
                        pltpu.VMEM((2, MOBA_BLOCK, hpp * MOBA_BLOCK), F32),
                        pltpu.VMEM((2, MOBA_BLOCK, hpp * MOBA_BLOCK), F32),
                        pltpu.VMEM((hpp, 2 * MOBA_BLOCK, MOBA_BLOCK), MXU_DTYPE),
                        pltpu.VMEM((hpp, 2 * MOBA_BLOCK, MOBA_BLOCK), MXU_DTYPE),
                        pltpu.VMEM((2 * LANES, hpp * MOBA_BLOCK), MXU_DTYPE),
                        pltpu.VMEM((hpp, A_HEAD_DIM, MOBA_BLOCK), F32)],
        compiler_params=_params("parallel", "parallel", "arbitrary"),
        name="moba",
    )(hp3d, hp3d, hp3d, bias)


S5_HALF_CH = S5_WIDTH // 2
S5_HALF_ST = (S5_GROUPS // 2) * S5_STATE
S5_TILE = 256
S5_NT = S5_HALF_ST // S5_TILE
S5_STEPS = 64
S5_PITCH = S5_STEPS + SUBLANES
S5_SLABS = 2 * S5_HALF_ST // LANES


def _s5_kernel(u_ref, wx_ref, wy_ref, a_ref, o_ref, st_ref, x_scr, h_scr):
    nb, steps, _ = u_ref.shape
    nseq = 2 * nb
    tile_slabs = S5_TILE // LANES

    @pl.when(pl.program_id(0) == 0)
    def _():
        st_ref[...] = jnp.zeros_like(st_ref)

    zeros = jnp.zeros((steps, LANES), MXU_DTYPE)
    for m in range(S5_NT // 2):
        pieces = []
        for seq in range(nseq):
            bi, half = divmod(seq, 2)
            c = half * S5_HALF_CH + m * LANES
            piece = _mx(u_ref[bi, :, c:c + LANES])
            pieces.append(jnp.concatenate([piece, zeros] if half == 0 else [zeros, piece], axis=1))
        lhs = jnp.concatenate(pieces, axis=0)
        for part in range(2):
            for n in (2 * m, 2 * m + 1):
                res = jnp.dot(lhs, wx_ref[part, n], preferred_element_type=F32)
                slab0 = (part * S5_HALF_ST + n * S5_TILE) // LANES
                for seq in range(nseq):
                    for sl in range(tile_slabs):
                        x_scr[slab0 + sl, seq * S5_PITCH:seq * S5_PITCH + steps, :] = (
                            res[seq * steps:(seq + 1) * steps, sl * LANES:(sl + 1) * LANES])

    nre = S5_SLABS // 2
    a_re = [a_ref[0, :, s * LANES:(s + 1) * LANES] for s in range(nre)]
    a_im = [a_ref[1, :, s * LANES:(s + 1) * LANES] for s in range(nre)]

    def step(t, carry):
        new_re, new_im = [], []
        for s in range(nre):
            hr, hi = carry[s], carry[nre + s]
            rows_t = pl.ds(t, nseq, stride=S5_PITCH)
            nhr = a_re[s] * hr - a_im[s] * hi + x_scr[s, rows_t, :]
            nhi = a_re[s] * hi + a_im[s] * hr + x_scr[nre + s, rows_t, :]
            h_scr[s, rows_t, :] = nhr
            h_scr[nre + s, rows_t, :] = nhi
            new_re.append(nhr)
            new_im.append(nhi)
        return tuple(new_re + new_im)

    init = tuple(st_ref[p, :, s * LANES:(s + 1) * LANES] for p in range(2) for s in range(nre))
    final = lax.fori_loop(0, steps, step, init, unroll=2)
    for p in range(2):
        for s in range(nre):
            st_ref[p, :, s * LANES:(s + 1) * LANES] = final[p * nre + s]

    for m in range(S5_NT // 2):
        acc = jnp.zeros((nseq * steps, 2 * LANES), F32)
        for n in (2 * m, 2 * m + 1):
            for part in range(2):
                slab0 = (part * S5_HALF_ST + n * S5_TILE) // LANES
                h_tile = jnp.concatenate(
                    [jnp.concatenate([h_scr[slab0 + sl, seq * S5_PITCH:seq * S5_PITCH + steps, :]
                                      for sl in range(tile_slabs)], axis=1) for seq in range(nseq)], axis=0)
                acc = acc + jnp.dot(_mx(h_tile), wy_ref[part, n], preferred_element_type=F32)
        for seq in range(nseq):
            bi, half = divmod(seq, 2)
            c = half * S5_HALF_CH + m * LANES
            o_ref[bi, :, c:c + LANES] = acc[seq * steps:(seq + 1) * steps, half * LANES:(half + 1) * LANES]


def _s5_weights(lam_re, lam_im, b_re, b_im, c_re, c_im, log_dt):
    lr = jnp.minimum(lam_re.astype(F32), -1e-4)
    li = lam_im.astype(F32)
    dt = jnp.exp(log_dt.astype(F32))[:, None]
    mag = jnp.exp(lr * dt)
    a_re = mag * jnp.cos(li * dt)
    a_im = mag * jnp.sin(li * dt)
    den = lr * lr + li * li
    coef_re = ((a_re - 1.0) * lr + a_im * li) / den
    coef_im = (a_im * lr - (a_re - 1.0) * li) / den
    br = b_re.astype(F32)
    bi = b_im.astype(F32)
    bb_re = coef_re[..., None] * br - coef_im[..., None] * bi
    bb_im = coef_re[..., None] * bi + coef_im[..., None] * br
    gh = S5_GROUPS // 2
    eye = jnp.eye(gh, dtype=F32)

    def b_dense(bb):
        t = bb.reshape(2, gh, S5_STATE, S5_GROUP)
        return jnp.einsum('fgph,gk->fghkp', t, eye).reshape(2, S5_HALF_CH, S5_HALF_ST)

    def c_dense(cc):
        t = cc.reshape(2, gh, S5_GROUP, S5_STATE)
        return jnp.einsum('fghp,gk->fgpkh', t, eye).reshape(2, S5_HALF_ST, S5_HALF_CH)

    def x_tiles(bd):
        tiles = []
        for n in range(S5_NT):
            c = (n // 2) * LANES
            tiles.append(jnp.concatenate([bd[0, c:c + LANES, n * S5_TILE:(n + 1) * S5_TILE],
                                          bd[1, c:c + LANES, n * S5_TILE:(n + 1) * S5_TILE]], axis=0))
        return jnp.stack(tiles)

    def y_tiles(cd):
        tiles = []
        for n in range(S5_NT):
            c = (n // 2) * LANES
            tiles.append(jnp.concatenate([cd[0, n * S5_TILE:(n + 1) * S5_TILE, c:c + LANES],
                                          cd[1, n * S5_TILE:(n + 1) * S5_TILE, c:c + LANES]], axis=1))
        return jnp.stack(tiles)

    wx = jnp.stack([x_tiles(b_dense(bb_re)), x_tiles(b_dense(bb_im))]).astype(MXU_DTYPE)
    wy = jnp.stack([y_tiles(c_dense(c_re.astype(F32))), y_tiles(c_dense(-c_im.astype(F32)))]).astype(MXU_DTYPE)
    return wx, wy, a_re.reshape(2, S5_HALF_ST), a_im.reshape(2, S5_HALF_ST)


def _s5_scan(hp3d, wx, wy, a_re2, a_im2):
    b, s, width = hp3d.shape
    nseq = 2 * b
    assert nseq == SUBLANES
    sel = jnp.arange(nseq) % 2
    a = jnp.stack([a_re2[sel], a_im2[sel]])
    scr = pltpu.VMEM((S5_SLABS, nseq * S5_PITCH, LANES), F32)
    return pl.pallas_call(
        _s5_kernel,
        grid=(s // S5_STEPS,),
        in_specs=[pl.BlockSpec((b, S5_STEPS, S5_WIDTH), lambda i: (0, i, width // S5_WIDTH - 1)),
                  _resident(wx.shape), _resident(wy.shape), _resident(a.shape)],
        out_specs=pl.BlockSpec((b, S5_STEPS, S5_WIDTH), lambda i: (0, i, 0)),
        out_shape=jax.ShapeDtypeStruct((b, s, S5_WIDTH), F32),
        scratch_shapes=[pltpu.VMEM((2, nseq, S5_HALF_ST), F32), scr, scr],
        compiler_params=_params("arbitrary"),
        name="s5_scan",
    )(hp3d, wx, wy, a)


MLP_CHUNK = 1024


def _tail_body(mix, x_ref, g1_ref, b1_ref, w1_ref, w2_ref, g2_ref, b2_ref, o_ref):
    x1 = _layer_norm(DN_ALPHA * x_ref[...] + mix, g1_ref[...], b1_ref[...])
    x1m = _mx(x1)
    acc = jnp.zeros(x1.shape, F32)
    for f in range(w1_ref.shape[1] // MLP_CHUNK):
        hid = jnp.dot(x1m, w1_ref[:, f * MLP_CHUNK:(f + 1) * MLP_CHUNK], preferred_element_type=F32)
        hid = jnp.square(jnp.maximum(hid, 0.0))
        acc = acc + jnp.dot(_mx(hid), w2_ref[f * MLP_CHUNK:(f + 1) * MLP_CHUNK, :],
                            preferred_element_type=F32)
    o_ref[...] = _layer_norm(DN_ALPHA * x1 + acc, g2_ref[...], b2_ref[...])


def _tail_even_kernel(attn_ref, y_ref, u_ref, dsk_ref, wg_ref, bg_ref, wo_ref, x_ref,
                      g1_ref, b1_ref, w1_ref, w2_ref, g2_ref, b2_ref, o_ref):
    y = y_ref[...] + dsk_ref[...] * u_ref[...]
    g = 0.5 * y * (1.0 + jnp.tanh(math.sqrt(2.0 / math.pi) * (y + 0.044715 * (y * y * y))))
    ssm = g * _sigmoid(jnp.dot(_mx(g), wg_ref[...], preferred_element_type=F32) + bg_ref[...])
    ka = attn_ref.shape[1]
    mix = (jnp.dot(_mx(attn_ref[...]), wo_ref[0:ka, :], preferred_element_type=F32)
           + jnp.dot(_mx(ssm), wo_ref[ka:, :], preferred_element_type=F32))
    _tail_body(mix, x_ref, g1_ref, b1_ref, w1_ref, w2_ref, g2_ref, b2_ref, o_ref)


def _tail_odd_kernel(a_ref, wo_ref, x_ref, g1_ref, b1_ref, w1_ref, w2_ref, g2_ref, b2_ref, o_ref):
    mix = jnp.dot(_mx(a_ref[...]), wo_ref[...], preferred_element_type=F32)
    _tail_body(mix, x_ref, g1_ref, b1_ref, w1_ref, w2_ref, g2_ref, b2_ref, o_ref)


def _row(v):
    return v.reshape(1, -1).astype(F32)


def _tail_common_specs(tm, d, dff):
    return [_resident((d, d)),
            pl.BlockSpec((tm, d), lambda i: (i, 0)),
            _resident((1, d)), _resident((1, d)),
            _resident((d, dff)), _resident((dff, d)),
            _resident((1, d)), _resident((1, d))]


def _tail_even(attn2d, y2d, hp2d, x2d, d_skip, w_glu, b_glu, w_out, g1, b1, w1, w2, g2, b2, tm):
    m, d = x2d.shape
    dff = w1.shape[1]
    ucol = hp2d.shape[1] // S5_WIDTH - 1
    return pl.pallas_call(
        _tail_even_kernel,
        grid=(m // tm,),
        in_specs=[pl.BlockSpec((tm, A_WIDTH), lambda i: (i, 0)),
                  pl.BlockSpec((tm, S5_WIDTH), lambda i: (i, 0)),
                  pl.BlockSpec((tm, S5_WIDTH), lambda i: (i, ucol)),
                  _resident((1, S5_WIDTH)), _resident((S5_WIDTH, S5_WIDTH)), _resident((1, S5_WIDTH))]
        + _tail_common_specs(tm, d, dff),
        out_specs=pl.BlockSpec((tm, d), lambda i: (i, 0)),
        out_shape=jax.ShapeDtypeStruct((m, d), F32),
        compiler_params=_params("parallel"),
        name="tail_even",
    )(attn2d, y2d, hp2d, _row(d_skip), _mx(w_glu), _row(b_glu), _mx(w_out), x2d,
      _row(g1), _row(b1), _mx(w1), _mx(w2), _row(g2), _row(b2))


def _tail_odd(a2d, x2d, w_out, g1, b1, w1, w2, g2, b2, tm):
    m, d = x2d.shape
    dff = w1.shape[1]
    return pl.pallas_call(
        _tail_odd_kernel,
        grid=(m // tm,),
        in_specs=[pl.BlockSpec((tm, a2d.shape[1]), lambda i: (i, 0))] + _tail_common_specs(tm, d, dff),
        out_specs=pl.BlockSpec((tm, d), lambda i: (i, 0)),
        out_shape=jax.ShapeDtypeStruct((m, d), F32),
        compiler_params=_params("parallel"),
        name="tail_odd",
    )(a2d, _mx(w_out), x2d, _row(g1), _row(b1), _mx(w1), _mx(w2), _row(g2), _row(b2))


GDN_PREP_TILE = 512
GDN_PREP_HEADS = 4
GDN_SCAN_TILE = 256


def _softplus(t):
    return jnp.maximum(t, 0.0) + jnp.log1p(jnp.exp(-jnp.abs(t)))


def _gdn_prep_kernel(q_ref, k_ref, v_ref, qp_ref, kp_ref, vp_ref, cwq_ref, cwk_ref, cwv_ref,
                     ba_ref, g_ref, gt_ref, u_ref, w_ref, qd_ref, kd_ref, intra_ref, edl_ref,
                     qx_ref, kx_ref, vx_ref, ql_ref, kl_ref, vl_ref):
    assert C_CONV == 4
    i = pl.program_id(1)
    hg = pl.program_id(2)
    t, width = q_ref.shape[1], q_ref.shape[2]
    c = C_CHUNK
    dk = C_HEAD_DIM
    hp = width // dk
    groups = t // SUBLANES

    def conv_silu(main_ref, prev_ref, cw_ref, ext_ref, low_ref):
        x = main_ref[0]
        ext_ref[0:SUBLANES, :] = jnp.zeros((SUBLANES, width), F32)
        ext_ref[SUBLANES:2 * SUBLANES, :] = jnp.where(i > 0, prev_ref[0], 0.0)
        ext_ref[2 * SUBLANES:, :] = x
        ext_rows = t + SUBLANES

        def tap(j):
            return jnp.broadcast_to(cw_ref[j:j + 1, :], (SUBLANES, width))[None]

        def grouped(a):
            return a.reshape(a.shape[0] // SUBLANES, SUBLANES, width)

        x_ext = grouped(ext_ref[SUBLANES:, :])
        x_del = grouped(ext_ref[SUBLANES - 1:SUBLANES - 1 + ext_rows, :])
        low_ref[...] = (tap(1) * x_ext + tap(0) * x_del).reshape(ext_rows, width)
        y = (tap(3) * grouped(x) + tap(2) * x_del[1:]
             + grouped(low_ref[SUBLANES - 2:SUBLANES - 2 + t, :]))
        hy = 0.5 * y.reshape(t, width)
        return hy + hy * jnp.tanh(hy)

    def l2n(z, scale):
        return z * (lax.rsqrt(jnp.sum(z * z, axis=-1, keepdims=True) + L2_EPS) * scale)

    q_all = conv_silu(q_ref, qp_ref, cwq_ref, qx_ref, ql_ref)
    k_all = conv_silu(k_ref, kp_ref, cwk_ref, kx_ref, kl_ref)
    v_all = conv_silu(v_ref, vp_ref, cwv_ref, vx_ref, vl_ref)
    heads = [slice(hh * dk, (hh + 1) * dk) for hh in range(hp)]
    qs = [l2n(q_all[:, cs], dk ** -0.5) for cs in heads]
    ks = [l2n(k_all[:, cs], 1.0) for cs in heads]
    vs = [v_all[:, cs] for cs in heads]

    ba = ba_ref[0]
    lane = lax.broadcasted_iota(jnp.int32, (t, LANES), 1)
    betas, g_cols, g_rows = [], [], []
    for hh in range(hp):
        h = hg * hp + hh
        betas.append(_sigmoid(jnp.sum(jnp.where(lane == h, ba, 0.0), axis=1, keepdims=True)))
        g_cols.append(jnp.sum(jnp.where(lane == C_HEADS + h, g_ref[0], 0.0), axis=1, keepdims=True))
        g_rows.append(gt_ref[0, pl.ds(C_HEADS + h, 1), :])

    ii = lax.broadcasted_iota(jnp.int32, (c, c), 0)
    jj = lax.broadcasted_iota(jnp.int32, (c, c), 1)
    low = ii >= jj
    units = [(hh, slice(ch * c, (ch + 1) * c)) for hh in range(hp) for ch in range(t // c)]
    gc_cols = [jnp.sum(jnp.where(low, g_rows[hh][:, r], 0.0), axis=1, keepdims=True) for hh, r in units]
    gc_rows = [jnp.sum(jnp.where(ii <= jj, g_cols[hh][r], 0.0), axis=0, keepdims=True) for hh, r in units]
    decays = [jnp.where(low, jnp.exp(jnp.where(low, a - b, 0.0)), 0.0) for a, b in zip(gc_cols, gc_rows)]
    kbs = [ks[hh][r] * betas[hh][r] for hh, r in units]
    kks = [_dot_nt(kb, ks[hh][r]) for kb, (hh, r) in zip(kbs, units)]
    qks = [_dot_nt(qs[hh][r], ks[hh][r]) for hh, r in units]
    pws = [jnp.where(ii > jj, -(kk * dec), 0.0) for kk, dec in zip(kks, decays)]
    tms = pws
    for _ in range(int(math.log2(c)) - 1):
        pbs = [_mx(pw) for pw in pws]
        pws = [_dot(pb, pb) for pb in pbs]
        pbs = [_mx(pw) for pw in pws]
        tms = [tm + pw + _dot(tm, pb) for tm, pw, pb in zip(tms, pws, pbs)]
    egcs = [jnp.exp(a) for a in gc_cols]
    vbs = [vs[hh][r] * betas[hh][r] for hh, r in units]
    kbes = [kb * e for kb, e in zip(kbs, egcs)]
    us = [vb + _dot(tm, vb) for tm, vb in zip(tms, vbs)]
    ws = [kbe + _dot(tm, kbe) for tm, kbe in zip(tms, kbes)]
    for n, (hh, r) in enumerate(units):
        gl = gc_cols[n][c - 1:c, :]
        cs = heads[hh]
        ch = r.start // c
        u_ref[0, r, cs] = us[n]
        w_ref[0, r, cs] = _mx(ws[n])
        qd_ref[0, r, cs] = _mx(qs[hh][r] * egcs[n])
        kd_ref[0, r, cs] = _mx(ks[hh][r] * jnp.exp(gl - gc_cols[n]))
        intra_ref[0, hh, r, :] = _mx(qks[n] * decays[n])
        edl_ref[0, hh, ch:ch + 1, :] = jnp.broadcast_to(jnp.exp(gl), (1, LANES))


def _gdn_gates_kernel(ba_ref, alog_ref, dtb_ref, g_ref, gt_ref):
    g_all = -jnp.exp(alog_ref[...]) * _softplus(ba_ref[0] + dtb_ref[...])
    g_ref[0] = g_all
    gt_ref[0] = g_all.T


def _gdn_gates(hp3d, alog_row, dtb_row):
    b, s, _ = hp3d.shape
    t = GDN_PREP_TILE
    ba_col = 4 * C_WIDTH // LANES
    return pl.pallas_call(
        _gdn_gates_kernel,
        grid=(b, s // t),
        in_specs=[pl.BlockSpec((1, t, LANES), lambda bi, i: (bi, i, ba_col)),
                  pl.BlockSpec((1, LANES), lambda bi, i: (0, 0)),
                  pl.BlockSpec((1, LANES), lambda bi, i: (0, 0))],
        out_specs=[pl.BlockSpec((1, t, LANES), lambda bi, i: (bi, i, 0)),
                   pl.BlockSpec((1, LANES, t), lambda bi, i: (bi, 0, i))],
        out_shape=[jax.ShapeDtypeStruct((b, s, LANES), F32), jax.ShapeDtypeStruct((b, LANES, s), F32)],
        compiler_params=_params("parallel", "parallel"),
        name="gdn_gates",
    )(hp3d, alog_row, dtb_row)


def _gdn_prep(hp3d, conv_w, alog_row, dtb_row):
    b, s, _ = hp3d.shape
    ba_col = 4 * C_WIDTH // LANES
    t = GDN_PREP_TILE
    hds = C_HEADS
    per = t // SUBLANES
    g, gt = _gdn_gates(hp3d, alog_row, dtb_row)

    hpg = GDN_PREP_HEADS
    wd = hpg * C_HEAD_DIM
    ngrp = hds // hpg

    def main(off):
        return pl.BlockSpec((1, t, wd), lambda bi, i, h: (bi, i, off + h))

    def prev(off):
        return pl.BlockSpec((1, SUBLANES, wd),
                            lambda bi, i, h: (bi, jnp.maximum(i * per - 1, 0), off + h))

    def cw(off):
        return pl.BlockSpec((C_CONV, wd), lambda bi, i, h: (0, off + h))

    out_main = pl.BlockSpec((1, t, wd), lambda bi, i, h: (bi, i, h))
    full = jax.ShapeDtypeStruct((b, s, C_WIDTH), F32)
    half = jax.ShapeDtypeStruct((b, s, C_WIDTH), MXU_DTYPE)
    nc = t // C_CHUNK
    return pl.pallas_call(
        _gdn_prep_kernel,
        grid=(b, s // t, ngrp),
        in_specs=[main(0), main(ngrp), main(2 * ngrp), prev(0), prev(ngrp), prev(2 * ngrp),
                  cw(0), cw(ngrp), cw(2 * ngrp),
                  pl.BlockSpec((1, t, LANES), lambda bi, i, h: (bi, i, ba_col)),
                  pl.BlockSpec((1, t, LANES), lambda bi, i, h: (bi, i, 0)),
                  pl.BlockSpec((1, LANES, t), lambda bi, i, h: (bi, 0, i))],
        out_specs=[out_main, out_main, out_main, out_main,
                   pl.BlockSpec((1, hpg, t, C_CHUNK), lambda bi, i, h: (bi, h, i, 0)),
                   pl.BlockSpec((1, hpg, nc, LANES), lambda bi, i, h: (bi, h, i, 0))],
        out_shape=[full, half, half, half,
                   jax.ShapeDtypeStruct((b, hds, s, C_CHUNK), MXU_DTYPE),
                   jax.ShapeDtypeStruct((b, hds, s // C_CHUNK, LANES), F32)],
        scratch_shapes=[pltpu.VMEM((t + 2 * SUBLANES, wd), F32)] * 3 + [pltpu.VMEM((t + SUBLANES, wd), F32)] * 3,
        compiler_params=_params("parallel", "parallel", "parallel"),
        name="gdn_prep",
    )(hp3d, hp3d, hp3d, hp3d, hp3d, hp3d, conv_w, conv_w, conv_w, hp3d, g, gt)


def _gdn_scan_kernel(u_ref, w_ref, qd_ref, kd_ref, intra_ref, edl_ref, gate_ref, nw_ref, o_ref, st_ref):
    c = C_CHUNK
    dk = C_HEAD_DIM
    t = u_ref.shape[1]
    chunk0 = pl.program_id(1) * (t // c)

    @pl.when(pl.program_id(1) == 0)
    def _():
        st_ref[...] = jnp.zeros_like(st_ref)

    nw = nw_ref[...]
    heads = [slice(h * dk, (h + 1) * dk) for h in range(C_HEADS)]
    for ch in range(t // c):
        r = slice(ch * c, (ch + 1) * c)
        sts = [st_ref[h] for h in range(C_HEADS)]
        wss = [_dot(jnp.concatenate([w_ref[0, r, cs], qd_ref[0, r, cs]], axis=0), st)
               for cs, st in zip(heads, sts)]
        v_news = [u_ref[0, r, cs] - ws[:c] for cs, ws in zip(heads, wss)]
        kvs = [_dot_tn(kd_ref[0, r, cs], v_new) for cs, v_new in zip(heads, v_news)]
        os_ = [ws[c:] + _dot(intra_ref[0, h, r, :], v_new) for h, (ws, v_new) in enumerate(zip(wss, v_news))]
        for h, cs in enumerate(heads):
            st_ref[h] = sts[h] * edl_ref[0, h, pl.ds(chunk0 + ch, 1), 0:1] + kvs[h]
            o = os_[h]
            o = o * lax.rsqrt(jnp.mean(o * o, axis=-1, keepdims=True) + RMS_EPS) * nw
            gt = gate_ref[0, r, cs]
            o_ref[0, r, cs] = o * (gt * _sigmoid(gt))


def _gdn_scan(u, w, qd, kd, intra, edl, hp3d, norm_w):
    b, s, cw = u.shape
    t = GDN_SCAN_TILE
    blk = pl.BlockSpec((1, t, cw), lambda bi, i: (bi, i, 0))
    return pl.pallas_call(
        _gdn_scan_kernel,
        grid=(b, s // t),
        in_specs=[blk, blk, blk, blk,
                  pl.BlockSpec((1, C_HEADS, t, C_CHUNK), lambda bi, i: (bi, 0, i, 0)),
                  pl.BlockSpec((1, C_HEADS, s // C_CHUNK, LANES), lambda bi, i: (bi, 0, 0, 0)),
                  pl.BlockSpec((1, t, cw), lambda bi, i: (bi, i, 3)),
                  pl.BlockSpec((1, C_HEAD_DIM), lambda bi, i: (0, 0))],
        out_specs=blk,
        out_shape=jax.ShapeDtypeStruct((b, s, cw), F32),
        scratch_shapes=[pltpu.VMEM((C_HEADS, C_HEAD_DIM, C_HEAD_DIM), F32)],
        compiler_params=_params("parallel", "arbitrary"),
        name="gdn_scan",
    )(u, w, qd, kd, intra, edl, hp3d, _row(norm_w))


ROW_TILE = 512


def kernel(x, rel_table, ev_w_in, ev_w_out, s5_lambda_re, s5_lambda_im, s5_b_re, s5_b_im, s5_c_re, s5_c_im, s5_d, s5_log_dt, s5_w_glu, s5_b_glu, od_w_in, od_conv_w, od_a_log, od_dt_bias, od_norm_w, od_w_out, ln_mix_g, ln_mix_b, mlp_w1, mlp_w2, ln_ffn_g, ln_ffn_b):
    bsz, s, d = x.shape
    m = bsz * s
    x2d = x.reshape(m, d)
    table_t = rel_table.T.astype(F32)
    bias = _bias_tiles(table_t)
    for layer in range(DEPTH):
        i = layer // 2
        if layer % 2 == 0:
            hp = _proj(x2d, _mx(ev_w_in[i]), ROW_TILE)
            hp3d = hp.reshape(bsz, s, hp.shape[1])
            attn = _moba(hp3d, bias)
            wx, wy, a_re2, a_im2 = _s5_weights(s5_lambda_re[i], s5_lambda_im[i], s5_b_re[i], s5_b_im[i],
                                               s5_c_re[i], s5_c_im[i], s5_log_dt[i])
            y = _s5_scan(hp3d, wx, wy, a_re2, a_im2)
            x2d = _tail_even(attn.reshape(m, A_WIDTH), y.reshape(m, S5_WIDTH), hp, x2d,
                             s5_d[i], s5_w_glu[i], s5_b_glu[i], ev_w_out[i],
                             ln_mix_g[layer], ln_mix_b[layer], mlp_w1[layer], mlp_w2[layer],
                             ln_ffn_g[layer], ln_ffn_b[layer], ROW_TILE)
        else:
            w_in = od_w_in[i]
            w_all = jnp.pad(w_in, ((0, 0), (0, LANES - 2 * C_HEADS)))
            hp = _proj(x2d, _mx(w_all), ROW_TILE)
            hp3d = hp.reshape(bsz, s, 4 * C_WIDTH + LANES)
            pad8 = jnp.zeros((C_HEADS,), F32)
            padr = jnp.zeros((LANES - 2 * C_HEADS,), F32)
            alog_row = jnp.concatenate([pad8, od_a_log[i].astype(F32), padr]).reshape(1, LANES)
            dtb_row = jnp.concatenate([pad8, od_dt_bias[i].astype(F32), padr]).reshape(1, LANES)
            u, w, qd, kd, intra, edl = _gdn_prep(hp3d, od_conv_w[i].astype(F32), alog_row, dtb_row)
            o = _gdn_scan(u, w, qd, kd, intra, edl, hp3d, od_norm_w[i])
            x2d = _tail_odd(o.reshape(m, C_WIDTH), x2d, od_w_out[i],
                            ln_mix_g[layer], ln_mix_b[layer], mlp_w1[layer], mlp_w2[layer],
                            ln_ffn_g[layer], ln_ffn_b[layer], ROW_TILE)
    return x2d.reshape(bsz, s, d)
```

```python
import functools
import math

import jax
import jax.numpy as jnp
import numpy as np
from jax import lax
from jax.experimental import pallas as pl
from jax.experimental.pallas import tpu as pltpu

F32 = jnp.float32
MXU_DTYPE = jnp.bfloat16

LANES = 128
SUBLANES = 8
VMEM_LIMIT_BYTES = 56 * 1024 * 1024

A_HEADS = 8
A_HEAD_DIM = 64
A_WIDTH = A_HEADS * A_HEAD_DIM
MOBA_BLOCK = 256
MOBA_TOPK = 3
REL_BUCKETS = 32
REL_MAX_DIST = 128
S5_GROUP = 16
S5_GROUPS = 32
S5_STATE = 64
S5_WIDTH = S5_GROUP * S5_GROUPS
C_HEADS = 8
C_HEAD_DIM = 128
C_WIDTH = C_HEADS * C_HEAD_DIM
C_CONV = 4
C_CHUNK = 64
DEPTH = 2
DN_ALPHA = (2 * DEPTH) ** 0.25
LN_EPS = 1e-5
RMS_EPS = 1e-6
L2_EPS = 1e-6
NEG_INF = -1e30
LOG2E = math.log2(math.e)


def _mx(t):
    return t.astype(MXU_DTYPE)


def _dot(a, b):
    return jnp.dot(_mx(a), _mx(b), preferred_element_type=F32)


def _dot_tn(a, b):
    return lax.dot_general(_mx(a), _mx(b), (((0,), (0,)), ((), ())), preferred_element_type=F32)


def _dot_nt(a, b):
    return lax.dot_general(_mx(a), _mx(b), (((1,), (1,)), ((), ())), preferred_element_type=F32)


def _sigmoid(t):
    return 1.0 / (1.0 + jnp.exp(-t))


def _layer_norm(r, g, b):
    mu = jnp.mean(r, axis=-1, keepdims=True)
    d = r - mu
    var = jnp.mean(d * d, axis=-1, keepdims=True)
    return d * lax.rsqrt(var + LN_EPS) * g + b


def _params(*sem):
    return pltpu.CompilerParams(dimension_semantics=sem, vmem_limit_bytes=VMEM_LIMIT_BYTES)


def _resident(shape):
    return pl.BlockSpec(shape, lambda *_: (0,) * len(shape), pipeline_mode=pl.Buffered(1))


PROJ_CHUNK = 1024


def _proj_kernel(x_ref, w_ref, o_ref):
    xm = _mx(x_ref[...])
    n = w_ref.shape[1]
    for c0 in range(0, n, PROJ_CHUNK):
        c1 = min(c0 + PROJ_CHUNK, n)
        o_ref[:, c0:c1] = jnp.dot(xm, w_ref[:, c0:c1], preferred_element_type=F32)


def _proj(x2d, w, tm):
    m, k = x2d.shape
    n = w.shape[1]
    return pl.pallas_call(
        _proj_kernel,
        grid=(m // tm,),
        in_specs=[pl.BlockSpec((tm, k), lambda i: (i, 0)), _resident((k, n))],
        out_specs=pl.BlockSpec((tm, n), lambda i: (i, 0)),
        out_shape=jax.ShapeDtypeStruct((m, n), F32),
        compiler_params=_params("parallel"),
        name="in_proj",
    )(x2d, w)


def _t5_bucket_np(dist):
    max_exact = REL_BUCKETS // 2
    n = np.maximum(dist, 0)
    nf = np.maximum(n, 1).astype(np.float32)
    large = max_exact + (np.log(nf / np.float32(max_exact)) / np.float32(math.log(REL_MAX_DIST / max_exact))
                         * np.float32(REL_BUCKETS - max_exact)).astype(np.int32)
    large = np.minimum(large, REL_BUCKETS - 1)
    return np.where(n < max_exact, n, large).astype(np.int32)


def _bias_kernel(tab_ref, bkt_ref, o_ref):
    h = pl.program_id(0)
    blk = bkt_ref.shape[-1]
    key = lax.broadcasted_iota(jnp.int32, (blk, blk), 0)
    qry = lax.broadcasted_iota(jnp.int32, (blk, blk), 1)
    far = tab_ref[h, REL_BUCKETS - 1]
    for t in range(2):
        bk = bkt_ref[t]
        acc = jnp.zeros((blk, blk), F32)
        for i in range(REL_BUCKETS):
            acc = jnp.where(bk == i, (tab_ref[h, i] - far) * LOG2E, acc)
        if t == 0:
            acc = jnp.where(qry >= key, acc, NEG_INF)
        o_ref[0, t] = acc


def _bias_tiles(table_t):
    heads = table_t.shape[0]
    offs = np.arange(MOBA_BLOCK)
    dist_own = offs[None, :] - offs[:, None]
    buckets = np.stack([_t5_bucket_np(dist_own), _t5_bucket_np(dist_own + MOBA_BLOCK)])
    return pl.pallas_call(
        _bias_kernel,
        grid=(heads,),
        in_specs=[pl.BlockSpec(memory_space=pltpu.SMEM),
                  pl.BlockSpec((2, MOBA_BLOCK, MOBA_BLOCK), lambda h: (0, 0, 0))],
        out_specs=pl.BlockSpec((1, 2, MOBA_BLOCK, MOBA_BLOCK), lambda h: (h, 0, 0, 0)),
        out_shape=jax.ShapeDtypeStruct((heads, 2, MOBA_BLOCK, MOBA_BLOCK), F32),
        compiler_params=_params("parallel"),
        name="t5_bias",
    )(table_t, jnp.asarray(buckets))


def _moba_kernel(q_ref, k_ref, v_ref, bias_ref, o_ref, kmean_ref, kb_ref, vt_ref,
                 st0_ref, st1_ref, p0_ref, p1_ref, wq_ref, acc_ref):
    st_refs = (st0_ref, st1_ref)
    p_refs = (p0_ref, p1_ref)
    qb = pl.program_id(2)
    blk = MOBA_BLOCK
    d = A_HEAD_DIM
    nh = LANES // d
    nb = k_ref.shape[1] // blk
    ng = q_ref.shape[2] // LANES
    groups = range(ng)

    @pl.when(qb == 0)
    def _():
        lane = lax.broadcasted_iota(jnp.int32, (blk, LANES), 1)
        for g in groups:
            gl = slice(g * LANES, (g + 1) * LANES)
            for j in range(nb + 1):
                kb_ref[g, j, :, LANES:2 * LANES] = jnp.where(lane == j, 1.0, 0.0).astype(MXU_DTYPE)
            for j in range(nb):
                kj = k_ref[0, j * blk:(j + 1) * blk, gl]
                kb_ref[g, j, :, 0:LANES] = _mx(kj)
                kmean_ref[g, j:j + 1, :] = jnp.mean(kj, axis=0, keepdims=True)
                vt_ref[g, j] = _mx(v_ref[0, j * blk:(j + 1) * blk, gl].T)
            kb_ref[g, nb, :, 0:LANES] = jnp.zeros((blk, LANES), MXU_DTYPE)
            vt_ref[g, nb] = jnp.zeros((LANES, blk), MXU_DTYPE)

    blk_id = lax.broadcasted_iota(jnp.int32, (nb, blk), 0)
    zero = jnp.zeros((d, blk), F32)
    row = lax.broadcasted_iota(jnp.int32, (LANES - nb, nh * blk), 0)
    for g in groups:
        qt = (q_ref[0, :, g * LANES:(g + 1) * LANES] * (d ** -0.5)).T
        q_rows = jnp.concatenate([jnp.concatenate([qt[:d], zero], axis=1),
                                  jnp.concatenate([zero, qt[d:]], axis=1)], axis=0) * LOG2E
        mask_rows = []
        for hh in range(nh):
            gate = lax.dot_general(kmean_ref[g, :, hh * d:(hh + 1) * d], qt[hh * d:(hh + 1) * d],
                                   (((1,), (0,)), ((), ())),
                                   precision=lax.Precision.HIGHEST, preferred_element_type=F32)
            gt = jnp.where(blk_id < qb, gate, -jnp.inf)
            sel = jnp.zeros((nb, blk), F32)
            for _ in range(MOBA_TOPK):
                m = jnp.max(gt, axis=0, keepdims=True)
                is_m = jnp.logical_and(gt == m, gt > -jnp.inf)
                idx = jnp.min(jnp.where(is_m, blk_id, nb), axis=0, keepdims=True)
                pick = blk_id == idx
                sel = jnp.where(pick, 1.0, sel)
                gt = jnp.where(pick, -jnp.inf, gt)
            keep = jnp.logical_or(sel > 0.0, blk_id == qb)
            mask_rows.append(jnp.where(keep, 0.0, NEG_INF))
        wq_ref[g] = _mx(jnp.concatenate([q_rows, jnp.concatenate(mask_rows, axis=1),
                                         jnp.where(row == 0, NEG_INF, 0.0)], axis=0))

    def score_stage(slot, ja, jb, with_bias):
        cmaxes = []
        for g in groups:
            keys = jnp.concatenate([kb_ref[g, ja], kb_ref[g, jb]], axis=0)
            st = jnp.dot(keys, wq_ref[g], preferred_element_type=F32)
            if with_bias:
                st = st + jnp.concatenate([jnp.concatenate([bias_ref[g * nh + hh, w] for hh in range(nh)], axis=1)
                                           for w in range(2)], axis=0)
            st_refs[slot][g, 0] = st[:blk]
            st_refs[slot][g, 1] = st[blk:]
            cmaxes.append(jnp.max(st, axis=0, keepdims=True))
        return tuple(cmaxes)

    def softmax_stage(slot, cmaxes, stats):
        new = []
        for g in groups:
            for hh in range(nh):
                cs = slice(hh * blk, (hh + 1) * blk)
                m_new = cmaxes[g][:, cs]
                if stats is not None:
                    m, l = stats[g * nh + hh]
                    m_new = jnp.maximum(m, m_new)
                ps = [jnp.exp2(st_refs[slot][g, w, :, cs] - m_new) for w in range(2)]
                p_refs[slot][g, hh, 0:blk, :] = _mx(ps[0])
                p_refs[slot][g, hh, blk:2 * blk, :] = _mx(ps[1])
                l_new = jnp.sum(ps[0], axis=0, keepdims=True) + jnp.sum(ps[1], axis=0, keepdims=True)
                if stats is None:
                    new.append((m_new, l_new, None))
                else:
                    alpha = jnp.exp2(m - m_new)
                    new.append((m_new, alpha * l + l_new, alpha))
        return new

    def value_stage(slot, ja, jb):
        return [jnp.dot(jnp.concatenate([vt_ref[g, ja, hh * d:(hh + 1) * d, :],
                                         vt_ref[g, jb, hh * d:(hh + 1) * d, :]], axis=1),
                        p_refs[slot][g, hh], preferred_element_type=F32)
                for g in groups for hh in range(nh)]

    nfar = qb - 1
    npairs = (nfar + 1) // 2

    def far_blocks(i):
        return jnp.where(2 * i < nfar, 2 * i, nb), jnp.where(2 * i + 1 < nfar, 2 * i + 1, nb)

    j_prev = jnp.where(qb > 0, qb - 1, nb)
    cmax0 = score_stage(0, qb, j_prev, True)
    cmax1 = score_stage(1, *far_blocks(0), False)
    st0 = softmax_stage(0, cmax0, None)
    for n, pv_h in enumerate(value_stage(0, qb, j_prev)):
        acc_ref[n // nh, n % nh] = pv_h
    init = (tuple((m, l) for m, l, _ in st0), cmax1)

    def trip(slot, k, carry):
        stats, cmax = carry
        cmax_next = score_stage(1 - slot, *far_blocks(k), False)
        new = softmax_stage(slot, cmax, stats)
        pv = value_stage(slot, *far_blocks(k - 1))
        for n, ((_, _, alpha), pv_h) in enumerate(zip(new, pv)):
            acc_ref[n // nh, n % nh] = acc_ref[n // nh, n % nh] * alpha + pv_h
        return tuple((m, l) for m, l, _ in new), cmax_next

    def trip_dyn(k, carry):
        return lax.cond(k % 2 == 1, functools.partial(trip, 1, k), functools.partial(trip, 0, k), carry)

    stats, _ = lax.fori_loop(1, npairs + 1, trip_dyn, init)
    for g in groups:
        out_t = jnp.concatenate([acc_ref[g, hh] / stats[g * nh + hh][1] for hh in range(nh)], axis=0)
        o_ref[0, :, g * LANES:(g + 1) * LANES] = out_t.T


MOBA_GROUPS = 4


def _moba(hp3d, bias):
    b, s, _ = hp3d.shape
    ng = MOBA_GROUPS
    wd = ng * LANES
    nsteps = A_WIDTH // wd
    hpp = LANES // A_HEAD_DIM
    nb = s // MOBA_BLOCK
    return pl.pallas_call(
        _moba_kernel,
        grid=(b, nsteps, nb),
        in_specs=[pl.BlockSpec((1, MOBA_BLOCK, wd), lambda bi, p, i: (bi, i, p)),
                  pl.BlockSpec((1, s, wd), lambda bi, p, i: (bi, 0, nsteps + p), pipeline_mode=pl.Buffered(1)),
                  pl.BlockSpec((1, s, wd), lambda bi, p, i: (bi, 0, 2 * nsteps + p), pipeline_mode=pl.Buffered(1)),
                  pl.BlockSpec((ng * hpp, 2, MOBA_BLOCK, MOBA_BLOCK), lambda bi, p, i: (p, 0, 0, 0))],
        out_specs=pl.BlockSpec((1, MOBA_BLOCK, wd), lambda bi, p, i: (bi, i, p)),
        out_shape=jax.ShapeDtypeStruct((b, s, A_WIDTH), F32),
        scratch_shapes=[pltpu.VMEM((ng, nb, LANES), F32),
                        pltpu.VMEM((ng, nb + 1, MOBA_BLOCK, 2 * LANES), MXU_DTYPE),
                        pltpu.VMEM((ng, nb + 1, LANES, MOBA_BLOCK), MXU_DTYPE),
                        pltpu.VMEM((ng, 2, MOBA_BLOCK, hpp * MOBA_BLOCK), F32),
                        pltpu.VMEM((ng, 2, MOBA_BLOCK, hpp * MOBA_BLOCK), F32),
                        pltpu.VMEM((ng, hpp, 2 * MOBA_BLOCK, MOBA_BLOCK), MXU_DTYPE),
                        pltpu.VMEM((ng, hpp, 2 * MOBA_BLOCK, MOBA_BLOCK), MXU_DTYPE),
                        pltpu.VMEM((ng, 2 * LANES, hpp * MOBA_BLOCK), MXU_DTYPE),
                        pltpu.VMEM((ng, hpp, A_HEAD_DIM, MOBA_BLOCK), F32)],
        compiler_params=_params("parallel", "parallel", "arbitrary"),
        name="moba",
    )(hp3d, hp3d, hp3d, bias)


S5_HALF_CH = S5_WIDTH // 2
S5_HALF_ST = (S5_GROUPS // 2) * S5_STATE
S5_TILE = 256
S5_NT = S5_HALF_ST // S5_TILE
S5_STEPS = 64
S5_PITCH = S5_STEPS + SUBLANES
S5_SLABS = 2 * S5_HALF_ST // LANES


def _s5_kernel(u_ref, wx_ref, wy_ref, a_ref, o_ref, st_ref, x_scr, h_scr):
    nb, steps, _ = u_ref.shape
    nseq = 2 * nb
    tile_slabs = S5_TILE // LANES

    @pl.when(pl.program_id(0) == 0)
    def _():
        st_ref[...] = jnp.zeros_like(st_ref)

    zeros = jnp.zeros((steps, LANES), MXU_DTYPE)
    for m in range(S5_NT // 2):
        pieces = []
        for seq in range(nseq):
            bi, half = divmod(seq, 2)
            c = half * S5_HALF_CH + m * LANES
            piece = _mx(u_ref[bi, :, c:c + LANES])
            pieces.append(jnp.concatenate([piece, zeros] if half == 0 else [zeros, piece], axis=1))
        lhs = jnp.concatenate(pieces, axis=0)
        for part in range(2):
            for n in (2 * m, 2 * m + 1):
                res = jnp.dot(lhs, wx_ref[part, n], preferred_element_type=F32)
                slab0 = (part * S5_HALF_ST + n * S5_TILE) // LANES
                for seq in range(nseq):
                    for sl in range(tile_slabs):
                        x_scr[slab0 + sl, seq * S5_PITCH:seq * S5_PITCH + steps, :] = (
                            res[seq * steps:(seq + 1) * steps, sl * LANES:(sl + 1) * LANES])

    nre = S5_SLABS // 2
    a_re = [a_ref[0, :, s * LANES:(s + 1) * LANES] for s in range(nre)]
    a_im = [a_ref[1, :, s * LANES:(s + 1) * LANES] for s in range(nre)]

    def step(t, carry):
        new_re, new_im = [], []
        for s in range(nre):
            hr, hi = carry[s], carry[nre + s]
            rows_t = pl.ds(t, nseq, stride=S5_PITCH)
            nhr = a_re[s] * hr - a_im[s] * hi + x_scr[s, rows_t, :]
            nhi = a_re[s] * hi + a_im[s] * hr + x_scr[nre + s, rows_t, :]
            h_scr[s, rows_t, :] = nhr
            h_scr[nre + s, rows_t, :] = nhi
            new_re.append(nhr)
            new_im.append(nhi)
        return tuple(new_re + new_im)

    init = tuple(st_ref[p, :, s * LANES:(s + 1) * LANES] for p in range(2) for s in range(nre))
    final = lax.fori_loop(0, steps, step, init, unroll=2)
    for p in range(2):
        for s in range(nre):
            st_ref[p, :, s * LANES:(s + 1) * LANES] = final[p * nre + s]

    for m in range(S5_NT // 2):
        acc = jnp.zeros((nseq * steps, 2 * LANES), F32)
        for n in (2 * m, 2 * m + 1):
            for part in range(2):
                slab0 = (part * S5_HALF_ST + n * S5_TILE) // LANES
                h_tile = jnp.concatenate(
                    [jnp.concatenate([h_scr[slab0 + sl, seq * S5_PITCH:seq * S5_PITCH + steps, :]
                                      for sl in range(tile_slabs)], axis=1) for seq in range(nseq)], axis=0)
                acc = acc + jnp.dot(_mx(h_tile), wy_ref[part, n], preferred_element_type=F32)
        for seq in range(nseq):
            bi, half = divmod(seq, 2)
            c = half * S5_HALF_CH + m * LANES
            o_ref[bi, :, c:c + LANES] = acc[seq * steps:(seq + 1) * steps, half * LANES:(half + 1) * LANES]


def _s5_weights(lam_re, lam_im, b_re, b_im, c_re, c_im, log_dt):
    lr = jnp.minimum(lam_re.astype(F32), -1e-4)
    li = lam_im.astype(F32)
    dt = jnp.exp(log_dt.astype(F32))[:, None]
    mag = jnp.exp(lr * dt)
    a_re = mag * jnp.cos(li * dt)
    a_im = mag * jnp.sin(li * dt)
    den = lr * lr + li * li
    coef_re = ((a_re - 1.0) * lr + a_im * li) / den
    coef_im = (a_im * lr - (a_re - 1.0) * li) / den
    br = b_re.astype(F32)
    bi = b_im.astype(F32)
    bb_re = coef_re[..., None] * br - coef_im[..., None] * bi
    bb_im = coef_re[..., None] * bi + coef_im[..., None] * br
    gh = S5_GROUPS // 2
    eye = jnp.eye(gh, dtype=F32)

    def b_dense(bb):
        t = bb.reshape(2, gh, S5_STATE, S5_GROUP)
        return jnp.einsum('fgph,gk->fghkp', t, eye).reshape(2, S5_HALF_CH, S5_HALF_ST)

    def c_dense(cc):
        t = cc.reshape(2, gh, S5_GROUP, S5_STATE)
        return jnp.einsum('fghp,gk->fgpkh', t, eye).reshape(2, S5_HALF_ST, S5_HALF_CH)

    def x_tiles(bd):
        tiles = []
        for n in range(S5_NT):
            c = (n // 2) * LANES
            tiles.append(jnp.concatenate([bd[0, c:c + LANES, n * S5_TILE:(n + 1) * S5_TILE],
                                          bd[1, c:c + LANES, n * S5_TILE:(n + 1) * S5_TILE]], axis=0))
        return jnp.stack(tiles)

    def y_tiles(cd):
        tiles = []
        for n in range(S5_NT):
            c = (n // 2) * LANES
            tiles.append(jnp.concatenate([cd[0, n * S5_TILE:(n + 1) * S5_TILE, c:c + LANES],
                                          cd[1, n * S5_TILE:(n + 1) * S5_TILE, c:c + LANES]], axis=1))
        return jnp.stack(tiles)

    wx = jnp.stack([x_tiles(b_dense(bb_re)), x_tiles(b_dense(bb_im))]).astype(MXU_DTYPE)
    wy = jnp.stack([y_tiles(c_dense(c_re.astype(F32))), y_tiles(c_dense(-c_im.astype(F32)))]).astype(MXU_DTYPE)
    return wx, wy, a_re.reshape(2, S5_HALF_ST), a_im.reshape(2, S5_HALF_ST)


def _s5_scan(hp3d, wx, wy, a_re2, a_im2):
    b, s, width = hp3d.shape
    nseq = 2 * b
    assert nseq == SUBLANES
    sel = jnp.arange(nseq) % 2
    a = jnp.stack([a_re2[sel], a_im2[sel]])
    scr = pltpu.VMEM((S5_SLABS, nseq * S5_PITCH, LANES), F32)
    return pl.pallas_call(
        _s5_kernel,
        grid=(s // S5_STEPS,),
        in_specs=[pl.BlockSpec((b, S5_STEPS, S5_WIDTH), lambda i: (0, i, width // S5_WIDTH - 1)),
                  _resident(wx.shape), _resident(wy.shape), _resident(a.shape)],
        out_specs=pl.BlockSpec((b, S5_STEPS, S5_WIDTH), lambda i: (0, i, 0)),
        out_shape=jax.ShapeDtypeStruct((b, s, S5_WIDTH), F32),
        scratch_shapes=[pltpu.VMEM((2, nseq, S5_HALF_ST), F32), scr, scr],
        compiler_params=_params("arbitrary"),
        name="s5_scan",
    )(hp3d, wx, wy, a)


MLP_CHUNK = 1024


def _tail_body(mix, x_ref, g1_ref, b1_ref, w1_ref, w2_ref, g2_ref, b2_ref, o_ref):
    x1 = _layer_norm(DN_ALPHA * x_ref[...] + mix, g1_ref[...], b1_ref[...])
    x1m = _mx(x1)
    acc = jnp.zeros(x1.shape, F32)
    for f in range(w1_ref.shape[1] // MLP_CHUNK):
        hid = jnp.dot(x1m, w1_ref[:, f * MLP_CHUNK:(f + 1) * MLP_CHUNK], preferred_element_type=F32)
        hid = jnp.square(jnp.maximum(hid, 0.0))
        acc = acc + jnp.dot(_mx(hid), w2_ref[f * MLP_CHUNK:(f + 1) * MLP_CHUNK, :],
                            preferred_element_type=F32)
    o_ref[...] = _layer_norm(DN_ALPHA * x1 + acc, g2_ref[...], b2_ref[...])


def _tail_even_kernel(attn_ref, y_ref, u_ref, dsk_ref, wg_ref, bg_ref, wo_ref, x_ref,
                      g1_ref, b1_ref, w1_ref, w2_ref, g2_ref, b2_ref, o_ref):
    y = y_ref[...] + dsk_ref[...] * u_ref[...]
    g = 0.5 * y * (1.0 + jnp.tanh(math.sqrt(2.0 / math.pi) * (y + 0.044715 * (y * y * y))))
    ssm = g * _sigmoid(jnp.dot(_mx(g), wg_ref[...], preferred_element_type=F32) + bg_ref[...])
    ka = attn_ref.shape[1]
    mix = (jnp.dot(_mx(attn_ref[...]), wo_ref[0:ka, :], preferred_element_type=F32)
           + jnp.dot(_mx(ssm), wo_ref[ka:, :], preferred_element_type=F32))
    _tail_body(mix, x_ref, g1_ref, b1_ref, w1_ref, w2_ref, g2_ref, b2_ref, o_ref)


def _tail_odd_kernel(a_ref, wo_ref, x_ref, g1_ref, b1_ref, w1_ref, w2_ref, g2_ref, b2_ref, o_ref):
    mix = jnp.dot(_mx(a_ref[...]), wo_ref[...], preferred_element_type=F32)
    _tail_body(mix, x_ref, g1_ref, b1_ref, w1_ref, w2_ref, g2_ref, b2_ref, o_ref)


def _row(v):
    return v.reshape(1, -1).astype(F32)


def _tail_common_specs(tm, d, dff):
    return [_resident((d, d)),
            pl.BlockSpec((tm, d), lambda i: (i, 0)),
            _resident((1, d)), _resident((1, d)),
            _resident((d, dff)), _resident((dff, d)),
            _resident((1, d)), _resident((1, d))]


def _tail_even(attn2d, y2d, hp2d, x2d, d_skip, w_glu, b_glu, w_out, g1, b1, w1, w2, g2, b2, tm):
    m, d = x2d.shape
    dff = w1.shape[1]
    ucol = hp2d.shape[1] // S5_WIDTH - 1
    return pl.pallas_call(
        _tail_even_kernel,
        grid=(m // tm,),
        in_specs=[pl.BlockSpec((tm, A_WIDTH), lambda i: (i, 0)),
                  pl.BlockSpec((tm, S5_WIDTH), lambda i: (i, 0)),
                  pl.BlockSpec((tm, S5_WIDTH), lambda i: (i, ucol)),
                  _resident((1, S5_WIDTH)), _resident((S5_WIDTH, S5_WIDTH)), _resident((1, S5_WIDTH))]
        + _tail_common_specs(tm, d, dff),
        out_specs=pl.BlockSpec((tm, d), lambda i: (i, 0)),
        out_shape=jax.ShapeDtypeStruct((m, d), F32),
        compiler_params=_params("parallel"),
        name="tail_even",
    )(attn2d, y2d, hp2d, _row(d_skip), _mx(w_glu), _row(b_glu), _mx(w_out), x2d,
      _row(g1), _row(b1), _mx(w1), _mx(w2), _row(g2), _row(b2))


def _tail_odd(a2d, x2d, w_out, g1, b1, w1, w2, g2, b2, tm):
    m, d = x2d.shape
    dff = w1.shape[1]
    return pl.pallas_call(
        _tail_odd_kernel,
        grid=(m // tm,),
        in_specs=[pl.BlockSpec((tm, a2d.shape[1]), lambda i: (i, 0))] + _tail_common_specs(tm, d, dff),
        out_specs=pl.BlockSpec((tm, d), lambda i: (i, 0)),
        out_shape=jax.ShapeDtypeStruct((m, d), F32),
        compiler_params=_params("parallel"),
        name="tail_odd",
    )(a2d, _mx(w_out), x2d, _row(g1), _row(b1), _mx(w1), _mx(w2), _row(g2), _row(b2))


GDN_PREP_TILE = 512
GDN_PREP_HEADS = 4
GDN_SCAN_TILE = 256


def _softplus(t):
    return jnp.maximum(t, 0.0) + jnp.log1p(jnp.exp(-jnp.abs(t)))


def _gdn_prep_kernel(q_ref, k_ref, v_ref, qp_ref, kp_ref, vp_ref, cwq_ref, cwk_ref, cwv_ref,
                     ba_ref, g_ref, gt_ref, u_ref, w_ref, qd_ref, kd_ref, intra_ref, edl_ref,
                     qx_ref, kx_ref, vx_ref, ql_ref, kl_ref, vl_ref):
    assert C_CONV == 4
    i = pl.program_id(1)
    hg = pl.program_id(2)
    t, width = q_ref.shape[1], q_ref.shape[2]
    c = C_CHUNK
    dk = C_HEAD_DIM
    hp = width // dk
    groups = t // SUBLANES

    def conv_silu(main_ref, prev_ref, cw_ref, ext_ref, low_ref):
        x = main_ref[0]
        ext_ref[0:SUBLANES, :] = jnp.zeros((SUBLANES, width), F32)
        ext_ref[SUBLANES:2 * SUBLANES, :] = jnp.where(i > 0, prev_ref[0], 0.0)
        ext_ref[2 * SUBLANES:, :] = x
        ext_rows = t + SUBLANES

        def tap(j):
            return jnp.broadcast_to(cw_ref[j:j + 1, :], (SUBLANES, width))[None]

        def grouped(a):
            return a.reshape(a.shape[0] // SUBLANES, SUBLANES, width)

        x_ext = grouped(ext_ref[SUBLANES:, :])
        x_del = grouped(ext_ref[SUBLANES - 1:SUBLANES - 1 + ext_rows, :])
        low_ref[...] = (tap(1) * x_ext + tap(0) * x_del).reshape(ext_rows, width)
        y = (tap(3) * grouped(x) + tap(2) * x_del[1:]
             + grouped(low_ref[SUBLANES - 2:SUBLANES - 2 + t, :]))
        hy = 0.5 * y.reshape(t, width)
        return hy + hy * jnp.tanh(hy)

    def l2n(z, scale):
        return z * (lax.rsqrt(jnp.sum(z * z, axis=-1, keepdims=True) + L2_EPS) * scale)

    q_all = conv_silu(q_ref, qp_ref, cwq_ref, qx_ref, ql_ref)
    k_all = conv_silu(k_ref, kp_ref, cwk_ref, kx_ref, kl_ref)
    v_all = conv_silu(v_ref, vp_ref, cwv_ref, vx_ref, vl_ref)
    heads = [slice(hh * dk, (hh + 1) * dk) for hh in range(hp)]
    qs = [l2n(q_all[:, cs], dk ** -0.5) for cs in heads]
    ks = [l2n(k_all[:, cs], 1.0) for cs in heads]
    vs = [v_all[:, cs] for cs in heads]

    ba = ba_ref[0]
    lane = lax.broadcasted_iota(jnp.int32, (t, LANES), 1)
    betas, g_cols, g_rows = [], [], []
    for hh in range(hp):
        h = hg * hp + hh
        betas.append(_sigmoid(jnp.sum(jnp.where(lane == h, ba, 0.0), axis=1, keepdims=True)))
        g_cols.append(jnp.sum(jnp.where(lane == C_HEADS + h, g_ref[0], 0.0), axis=1, keepdims=True))
        g_rows.append(gt_ref[0, pl.ds(C_HEADS + h, 1), :])

    ii = lax.broadcasted_iota(jnp.int32, (c, c), 0)
    jj = lax.broadcasted_iota(jnp.int32, (c, c), 1)
    low = ii >= jj
    units = [(hh, slice(ch * c, (ch + 1) * c)) for hh in range(hp) for ch in range(t // c)]
    gc_cols = [jnp.sum(jnp.where(low, g_rows[hh][:, r], 0.0), axis=1, keepdims=True) for hh, r in units]
    gc_rows = [jnp.sum(jnp.where(ii <= jj, g_cols[hh][r], 0.0), axis=0, keepdims=True) for hh, r in units]
    decays = [jnp.where(low, jnp.exp(jnp.where(low, a - b, 0.0)), 0.0) for a, b in zip(gc_cols, gc_rows)]
    kbs = [ks[hh][r] * betas[hh][r] for hh, r in units]
    kks = [_dot_nt(kb, ks[hh][r]) for kb, (hh, r) in zip(kbs, units)]
    qks = [_dot_nt(qs[hh][r], ks[hh][r]) for hh, r in units]
    pws = [jnp.where(ii > jj, -(kk * dec), 0.0) for kk, dec in zip(kks, decays)]
    tms = pws
    for _ in range(int(math.log2(c)) - 1):
        pbs = [_mx(pw) for pw in pws]
        pws = [_dot(pb, pb) for pb in pbs]
        pbs = [_mx(pw) for pw in pws]
        tms = [tm + pw + _dot(tm, pb) for tm, pw, pb in zip(tms, pws, pbs)]
    egcs = [jnp.exp(a) for a in gc_cols]
    vbs = [vs[hh][r] * betas[hh][r] for hh, r in units]
    kbes = [kb * e for kb, e in zip(kbs, egcs)]
    us = [vb + _dot(tm, vb) for tm, vb in zip(tms, vbs)]
    ws = [kbe + _dot(tm, kbe) for tm, kbe in zip(tms, kbes)]
    for n, (hh, r) in enumerate(units):
        gl = gc_cols[n][c - 1:c, :]
        cs = heads[hh]
        ch = r.start // c
        u_ref[0, r, cs] = us[n]
        w_ref[0, r, cs] = _mx(ws[n])
        qd_ref[0, r, cs] = _mx(qs[hh][r] * egcs[n])
        kd_ref[0, r, cs] = _mx(ks[hh][r] * jnp.exp(gl - gc_cols[n]))
        intra_ref[0, hh, r, :] = _mx(qks[n] * decays[n])
        edl_ref[0, hh, ch:ch + 1, :] = jnp.broadcast_to(jnp.exp(gl), (1, LANES))


def _gdn_gates_kernel(ba_ref, alog_ref, dtb_ref, g_ref, gt_ref):
    g_all = -jnp.exp(alog_ref[...]) * _softplus(ba_ref[0] + dtb_ref[...])
    g_ref[0] = g_all
    gt_ref[0] = g_all.T


def _gdn_gates(hp3d, alog_row, dtb_row):
    b, s, _ = hp3d.shape
    t = GDN_PREP_TILE
    ba_col = 4 * C_WIDTH // LANES
    return pl.pallas_call(
        _gdn_gates_kernel,
        grid=(b, s // t),
        in_specs=[pl.BlockSpec((1, t, LANES), lambda bi, i: (bi, i, ba_col)),
                  pl.BlockSpec((1, LANES), lambda bi, i: (0, 0)),
                  pl.BlockSpec((1, LANES), lambda bi, i: (0, 0))],
        out_specs=[pl.BlockSpec((1, t, LANES), lambda bi, i: (bi, i, 0)),
                   pl.BlockSpec((1, LANES, t), lambda bi, i: (bi, 0, i))],
        out_shape=[jax.ShapeDtypeStruct((b, s, LANES), F32), jax.ShapeDtypeStruct((b, LANES, s), F32)],
        compiler_params=_params("parallel", "parallel"),
        name="gdn_gates",
    )(hp3d, alog_row, dtb_row)


def _gdn_prep(hp3d, conv_w, alog_row, dtb_row):
    b, s, _ = hp3d.shape
    ba_col = 4 * C_WIDTH // LANES
    t = GDN_PREP_TILE
    hds = C_HEADS
    per = t // SUBLANES
    g, gt = _gdn_gates(hp3d, alog_row, dtb_row)

    hpg = GDN_PREP_HEADS
    wd = hpg * C_HEAD_DIM
    ngrp = hds // hpg

    def main(off):
        return pl.BlockSpec((1, t, wd), lambda bi, i, h: (bi, i, off + h))

    def prev(off):
        return pl.BlockSpec((1, SUBLANES, wd),
                            lambda bi, i, h: (bi, jnp.maximum(i * per - 1, 0), off + h))

    def cw(off):
        return pl.BlockSpec((C_CONV, wd), lambda bi, i, h: (0, off + h))

    out_main = pl.BlockSpec((1, t, wd), lambda bi, i, h: (bi, i, h))
    full = jax.ShapeDtypeStruct((b, s, C_WIDTH), F32)
    half = jax.ShapeDtypeStruct((b, s, C_WIDTH), MXU_DTYPE)
    nc = t // C_CHUNK
    return pl.pallas_call(
        _gdn_prep_kernel,
        grid=(b, s // t, ngrp),
        in_specs=[main(0), main(ngrp), main(2 * ngrp), prev(0), prev(ngrp), prev(2 * ngrp),
                  cw(0), cw(ngrp), cw(2 * ngrp),
                  pl.BlockSpec((1, t, LANES), lambda bi, i, h: (bi, i, ba_col)),
                  pl.BlockSpec((1, t, LANES), lambda bi, i, h: (bi, i, 0)),
                  pl.BlockSpec((1, LANES, t), lambda bi, i, h: (bi, 0, i))],
        out_specs=[out_main, out_main, out_main, out_main,
                   pl.BlockSpec((1, hpg, t, C_CHUNK), lambda bi, i, h: (bi, h, i, 0)),
                   pl.BlockSpec((1, hpg, nc, LANES), lambda bi, i, h: (bi, h, i, 0))],
        out_shape=[full, half, half, half,
                   jax.ShapeDtypeStruct((b, hds, s, C_CHUNK), MXU_DTYPE),
                   jax.ShapeDtypeStruct((b, hds, s // C_CHUNK, LANES), F32)],
        scratch_shapes=[pltpu.VMEM((t + 2 * SUBLANES, wd), F32)] * 3 + [pltpu.VMEM((t + SUBLANES, wd), F32)] * 3,
        compiler_params=_params("parallel", "parallel", "parallel"),
        name="gdn_prep",
    )(hp3d, hp3d, hp3d, hp3d, hp3d, hp3d, conv_w, conv_w, conv_w, hp3d, g, gt)


def _gdn_scan_kernel(u_ref, w_ref, qd_ref, kd_ref, intra_ref, edl_ref, gate_ref, nw_ref, o_ref, st_ref):
    c = C_CHUNK
    dk = C_HEAD_DIM
    t = u_ref.shape[1]
    chunk0 = pl.program_id(1) * (t // c)

    @pl.when(pl.program_id(1) == 0)
    def _():
        st_ref[...] = jnp.zeros_like(st_ref)

    nw = nw_ref[...]
    heads = [slice(h * dk, (h + 1) * dk) for h in range(C_HEADS)]
    for ch in range(t // c):
        r = slice(ch * c, (ch + 1) * c)
        sts = [st_ref[h] for h in range(C_HEADS)]
        wss = [_dot(jnp.concatenate([w_ref[0, r, cs], qd_ref[0, r, cs]], axis=0), st)
               for cs, st in zip(heads, sts)]
        v_news = [u_ref[0, r, cs] - ws[:c] for cs, ws in zip(heads, wss)]
        kvs = [_dot_tn(kd_ref[0, r, cs], v_new) for cs, v_new in zip(heads, v_news)]
        os_ = [ws[c:] + _dot(intra_ref[0, h, r, :], v_new) for h, (ws, v_new) in enumerate(zip(wss, v_news))]
        for h, cs in enumerate(heads):
            st_ref[h] = sts[h] * edl_ref[0, h, pl.ds(chunk0 + ch, 1), 0:1] + kvs[h]
            o = os_[h]
            o = o * lax.rsqrt(jnp.mean(o * o, axis=-1, keepdims=True) + RMS_EPS) * nw
            gt = gate_ref[0, r, cs]
            o_ref[0, r, cs] = o * (gt * _sigmoid(gt))


def _gdn_scan(u, w, qd, kd, intra, edl, hp3d, norm_w):
    b, s, cw = u.shape
    t = GDN_SCAN_TILE
    blk = pl.BlockSpec((1, t, cw), lambda bi, i: (bi, i, 0))
    return pl.pallas_call(
        _gdn_scan_kernel,
        grid=(b, s // t),
        in_specs=[blk, blk, blk, blk,
                  pl.BlockSpec((1, C_HEADS, t, C_CHUNK), lambda bi, i: (bi, 0, i, 0)),
                  pl.BlockSpec((1, C_HEADS, s // C_CHUNK, LANES), lambda bi, i: (bi, 0, 0, 0)),
                  pl.BlockSpec((1, t, cw), lambda bi, i: (bi, i, 3)),
                  pl.BlockSpec((1, C_HEAD_DIM), lambda bi, i: (0, 0))],
        out_specs=blk,
        out_shape=jax.ShapeDtypeStruct((b, s, cw), F32),
        scratch_shapes=[pltpu.VMEM((C_HEADS, C_HEAD_DIM, C_HEAD_DIM), F32)],
        compiler_params=_params("parallel", "arbitrary"),
        name="gdn_scan",
    )(u, w, qd, kd, intra, edl, hp3d, _row(norm_w))


ROW_TILE = 512


def kernel(x, rel_table, ev_w_in, ev_w_out, s5_lambda_re, s5_lambda_im, s5_b_re, s5_b_im, s5_c_re, s5_c_im, s5_d, s5_log_dt, s5_w_glu, s5_b_glu, od_w_in, od_conv_w, od_a_log, od_dt_bias, od_norm_w, od_w_out, ln_mix_g, ln_mix_b, mlp_w1, mlp_w2, ln_ffn_g, ln_ffn_b):
    bsz, s, d = x.shape
    m = bsz * s
    x2d = x.reshape(m, d)
    table_t = rel_table.T.astype(F32)
    bias = _bias_tiles(table_t)
    for layer in range(DEPTH):
        i = layer // 2
        if layer % 2 == 0:
            hp = _proj(x2d, _mx(ev_w_in[i]), ROW_TILE)
            hp3d = hp.reshape(bsz, s, hp.shape[1])
            attn = _moba(hp3d, bias)
            wx, wy, a_re2, a_im2 = _s5_weights(s5_lambda_re[i], s5_lambda_im[i], s5_b_re[i], s5_b_im[i],
                                               s5_c_re[i], s5_c_im[i], s5_log_dt[i])
            y = _s5_scan(hp3d, wx, wy, a_re2, a_im2)
            x2d = _tail_even(attn.reshape(m, A_WIDTH), y.reshape(m, S5_WIDTH), hp, x2d,
                             s5_d[i], s5_w_glu[i], s5_b_glu[i], ev_w_out[i],
                             ln_mix_g[layer], ln_mix_b[layer], mlp_w1[layer], mlp_w2[layer],
                             ln_ffn_g[layer], ln_ffn_b[layer], ROW_TILE)
        else:
            w_in = od_w_in[i]
            w_all = jnp.pad(w_in, ((0, 0), (0, LANES - 2 * C_HEADS)))
            hp = _proj(x2d, _mx(w_all), ROW_TILE)
            hp3d = hp.reshape(bsz, s, 4 * C_WIDTH + LANES)
            pad8 = jnp.zeros((C_HEADS,), F32)
            padr = jnp.zeros((LANES - 2 * C_HEADS,), F32)
            alog_row = jnp.concatenate([pad8, od_a_log[i].astype(F32), padr]).reshape(1, LANES)
            dtb_row = jnp.concatenate([pad8, od_dt_bias[i].astype(F32), padr]).reshape(1, LANES)
            u, w, qd, kd, intra, edl = _gdn_prep(hp3d, od_conv_w[i].astype(F32), alog_row, dtb_row)
            o = _gdn_scan(u, w, qd, kd, intra, edl, hp3d, od_norm_w[i])
            x2d = _tail_odd(o.reshape(m, C_WIDTH), x2d, od_w_out[i],
                            ln_mix_g[layer], ln_mix_b[layer], mlp_w1[layer], mlp_w2[layer],
                            ln_ffn_g[layer], ln_ffn_b[layer], ROW_TILE)
    return x2d.reshape(bsz, s, d)
```

```python
import functools
import math

import jax
import jax.numpy as jnp
import numpy as np
from jax import lax
from jax.experimental import pallas as pl
from jax.experimental.pallas import tpu as pltpu

F32 = jnp.float32
MXU_DTYPE = jnp.bfloat16

LANES = 128
SUBLANES = 8
VMEM_LIMIT_BYTES = 56 * 1024 * 1024

A_HEADS = 8
A_HEAD_DIM = 64
A_WIDTH = A_HEADS * A_HEAD_DIM
MOBA_BLOCK = 256
MOBA_TOPK = 3
REL_BUCKETS = 32
REL_MAX_DIST = 128
S5_GROUP = 16
S5_GROUPS = 32
S5_STATE = 64
S5_WIDTH = S5_GROUP * S5_GROUPS
C_HEADS = 8
C_HEAD_DIM = 128
C_WIDTH = C_HEADS * C_HEAD_DIM
C_CONV = 4
C_CHUNK = 64
DEPTH = 2
DN_ALPHA = (2 * DEPTH) ** 0.25
LN_EPS = 1e-5
RMS_EPS = 1e-6
L2_EPS = 1e-6
NEG_INF = -1e30
LOG2E = math.log2(math.e)


def _mx(t):
    return t.astype(MXU_DTYPE)


def _dot(a, b):
    return jnp.dot(_mx(a), _mx(b), preferred_element_type=F32)


def _dot_tn(a, b):
    return lax.dot_general(_mx(a), _mx(b), (((0,), (0,)), ((), ())), preferred_element_type=F32)


def _dot_nt(a, b):
    return lax.dot_general(_mx(a), _mx(b), (((1,), (1,)), ((), ())), preferred_element_type=F32)


def _sigmoid(t):
    return 1.0 / (1.0 + jnp.exp(-t))


def _layer_norm(r, g, b):
    mu = jnp.mean(r, axis=-1, keepdims=True)
    d = r - mu
    var = jnp.mean(d * d, axis=-1, keepdims=True)
    return d * lax.rsqrt(var + LN_EPS) * g + b


def _params(*sem):
    return pltpu.CompilerParams(dimension_semantics=sem, vmem_limit_bytes=VMEM_LIMIT_BYTES)


def _resident(shape):
    return pl.BlockSpec(shape, lambda *_: (0,) * len(shape), pipeline_mode=pl.Buffered(1))


PROJ_CHUNK = 1024


def _proj_kernel(x_ref, w_ref, o_ref):
    xm = _mx(x_ref[...])
    n = w_ref.shape[1]
    for c0 in range(0, n, PROJ_CHUNK):
        c1 = min(c0 + PROJ_CHUNK, n)
        o_ref[:, c0:c1] = jnp.dot(xm, w_ref[:, c0:c1], preferred_element_type=F32)


def _proj(x2d, w, tm):
    m, k = x2d.shape
    n = w.shape[1]
    return pl.pallas_call(
        _proj_kernel,
        grid=(m // tm,),
        in_specs=[pl.BlockSpec((tm, k), lambda i: (i, 0)), _resident((k, n))],
        out_specs=pl.BlockSpec((tm, n), lambda i: (i, 0)),
        out_shape=jax.ShapeDtypeStruct((m, n), F32),
        compiler_params=_params("parallel"),
        name="in_proj",
    )(x2d, w)


def _t5_bucket_np(dist):
    max_exact = REL_BUCKETS // 2
    n = np.maximum(dist, 0)
    nf = np.maximum(n, 1).astype(np.float32)
    large = max_exact + (np.log(nf / np.float32(max_exact)) / np.float32(math.log(REL_MAX_DIST / max_exact))
                         * np.float32(REL_BUCKETS - max_exact)).astype(np.int32)
    large = np.minimum(large, REL_BUCKETS - 1)
    return np.where(n < max_exact, n, large).astype(np.int32)


def _bias_kernel(tab_ref, bkt_ref, o_ref):
    h = pl.program_id(0)
    blk = bkt_ref.shape[-1]
    key = lax.broadcasted_iota(jnp.int32, (blk, blk), 0)
    qry = lax.broadcasted_iota(jnp.int32, (blk, blk), 1)
    far = tab_ref[h, REL_BUCKETS - 1]
    for t in range(2):
        bk = bkt_ref[t]
        acc = jnp.zeros((blk, blk), F32)
        for i in range(REL_BUCKETS):
            acc = jnp.where(bk == i, (tab_ref[h, i] - far) * LOG2E, acc)
        if t == 0:
            acc = jnp.where(qry >= key, acc, NEG_INF)
        o_ref[0, t] = acc


def _bias_tiles(table_t):
    heads = table_t.shape[0]
    offs = np.arange(MOBA_BLOCK)
    dist_own = offs[None, :] - offs[:, None]
    buckets = np.stack([_t5_bucket_np(dist_own), _t5_bucket_np(dist_own + MOBA_BLOCK)])
    return pl.pallas_call(
        _bias_kernel,
        grid=(heads,),
        in_specs=[pl.BlockSpec(memory_space=pltpu.SMEM),
                  pl.BlockSpec((2, MOBA_BLOCK, MOBA_BLOCK), lambda h: (0, 0, 0))],
        out_specs=pl.BlockSpec((1, 2, MOBA_BLOCK, MOBA_BLOCK), lambda h: (h, 0, 0, 0)),
        out_shape=jax.ShapeDtypeStruct((heads, 2, MOBA_BLOCK, MOBA_BLOCK), F32),
        compiler_params=_params("parallel"),
        name="t5_bias",
    )(table_t, jnp.asarray(buckets))


def _moba_kernel(q_ref, k_ref, v_ref, bias_ref, o_ref, kmean_ref, kb_ref, vt_ref,
                 st0_ref, st1_ref, p0_ref, p1_ref, wq_ref, acc_ref):
    st_refs = (st0_ref, st1_ref)
    p_refs = (p0_ref, p1_ref)
    qb = pl.program_id(2)
    blk = MOBA_BLOCK
    d = A_HEAD_DIM
    nh = LANES // d
    nb = k_ref.shape[1] // blk
    ng = q_ref.shape[2] // LANES
    groups = range(ng)

    @pl.when(qb == 0)
    def _():
        lane = lax.broadcasted_iota(jnp.int32, (blk, LANES), 1)
        for g in groups:
            gl = slice(g * LANES, (g + 1) * LANES)
            for j in range(nb + 1):
                kb_ref[g, j, :, LANES:2 * LANES] = jnp.where(lane == j, 1.0, 0.0).astype(MXU_DTYPE)
            for j in range(nb):
                kj = k_ref[0, j * blk:(j + 1) * blk, gl]
                kb_ref[g, j, :, 0:LANES] = _mx(kj)
                kmean_ref[g, j:j + 1, :] = jnp.mean(kj, axis=0, keepdims=True)
                vt_ref[g, j] = _mx(v_ref[0, j * blk:(j + 1) * blk, gl].T)
            kb_ref[g, nb, :, 0:LANES] = jnp.zeros((blk, LANES), MXU_DTYPE)
            vt_ref[g, nb] = jnp.zeros((LANES, blk), MXU_DTYPE)

    blk_id = lax.broadcasted_iota(jnp.int32, (nb, blk), 0)
    zero = jnp.zeros((d, blk), F32)
    row = lax.broadcasted_iota(jnp.int32, (LANES - nb, nh * blk), 0)
    for g in groups:
        qt = (q_ref[0, :, g * LANES:(g + 1) * LANES] * (d ** -0.5)).T
        q_rows = jnp.concatenate([jnp.concatenate([qt[:d], zero], axis=1),
                                  jnp.concatenate([zero, qt[d:]], axis=1)], axis=0) * LOG2E
        mask_rows = []
        for hh in range(nh):
            gate = lax.dot_general(kmean_ref[g, :, hh * d:(hh + 1) * d], qt[hh * d:(hh + 1) * d],
                                   (((1,), (0,)), ((), ())),
                                   precision=lax.Precision.HIGHEST, preferred_element_type=F32)
            gt = jnp.where(blk_id < qb, gate, -jnp.inf)
            sel = jnp.zeros((nb, blk), F32)
            for _ in range(MOBA_TOPK):
                m = jnp.max(gt, axis=0, keepdims=True)
                is_m = jnp.logical_and(gt == m, gt > -jnp.inf)
                idx = jnp.min(jnp.where(is_m, blk_id, nb), axis=0, keepdims=True)
                pick = blk_id == idx
                sel = jnp.where(pick, 1.0, sel)
                gt = jnp.where(pick, -jnp.inf, gt)
            keep = jnp.logical_or(sel > 0.0, blk_id == qb)
            mask_rows.append(jnp.where(keep, 0.0, NEG_INF))
        wq_ref[g] = _mx(jnp.concatenate([q_rows, jnp.concatenate(mask_rows, axis=1),
                                         jnp.where(row == 0, NEG_INF, 0.0)], axis=0))

    def score_stage(slot, ja, jb, with_bias):
        cmaxes = []
        for g in groups:
            keys = jnp.concatenate([kb_ref[g, ja], kb_ref[g, jb]], axis=0)
            st = jnp.dot(keys, wq_ref[g], preferred_element_type=F32)
            if with_bias:
                st = st + jnp.concatenate([jnp.concatenate([bias_ref[g * nh + hh, w] for hh in range(nh)], axis=1)
                                           for w in range(2)], axis=0)
            st_refs[slot][g, 0] = st[:blk]
            st_refs[slot][g, 1] = st[blk:]
            cmaxes.append(jnp.max(st, axis=0, keepdims=True))
        return tuple(cmaxes)

    def softmax_stage(slot, cmaxes, stats):
        new = []
        for g in groups:
            for hh in range(nh):
                cs = slice(hh * blk, (hh + 1) * blk)
                m_new = cmaxes[g][:, cs]
                if stats is not None:
                    m, l = stats[g * nh + hh]
                    m_new = jnp.maximum(m, m_new)
                ps = [jnp.exp2(st_refs[slot][g, w, :, cs] - m_new) for w in range(2)]
                p_refs[slot][g, hh, 0:blk, :] = _mx(ps[0])
                p_refs[slot][g, hh, blk:2 * blk, :] = _mx(ps[1])
                l_new = jnp.sum(ps[0], axis=0, keepdims=True) + jnp.sum(ps[1], axis=0, keepdims=True)
                if stats is None:
                    new.append((m_new, l_new, None))
                else:
                    alpha = jnp.exp2(m - m_new)
                    new.append((m_new, alpha * l + l_new, alpha))
        return new

    def value_stage(slot, ja, jb):
        return [jnp.dot(jnp.concatenate([vt_ref[g, ja, hh * d:(hh + 1) * d, :],
                                         vt_ref[g, jb, hh * d:(hh + 1) * d, :]], axis=1),
                        p_refs[slot][g, hh], preferred_element_type=F32)
                for g in groups for hh in range(nh)]

    nfar = qb - 1
    npairs = (nfar + 1) // 2

    def far_blocks(i):
        return jnp.where(2 * i < nfar, 2 * i, nb), jnp.where(2 * i + 1 < nfar, 2 * i + 1, nb)

    j_prev = jnp.where(qb > 0, qb - 1, nb)
    cmax0 = score_stage(0, qb, j_prev, True)
    cmax1 = score_stage(1, *far_blocks(0), False)
    st0 = softmax_stage(0, cmax0, None)
    for n, pv_h in enumerate(value_stage(0, qb, j_prev)):
        acc_ref[n // nh, n % nh] = pv_h
    init = (tuple((m, l) for m, l, _ in st0), cmax1)

    def trip(slot, k, carry):
        stats, cmax = carry
        cmax_next = score_stage(1 - slot, *far_blocks(k), False)
        new = softmax_stage(slot, cmax, stats)
        pv = value_stage(slot, *far_blocks(k - 1))
        for n, ((_, _, alpha), pv_h) in enumerate(zip(new, pv)):
            acc_ref[n // nh, n % nh] = acc_ref[n // nh, n % nh] * alpha + pv_h
        return tuple((m, l) for m, l, _ in new), cmax_next

    def trip_dyn(k, carry):
        return lax.cond(k % 2 == 1, functools.partial(trip, 1, k), functools.partial(trip, 0, k), carry)

    stats, _ = lax.fori_loop(1, npairs + 1, trip_dyn, init)
    for g in groups:
        out_t = jnp.concatenate([acc_ref[g, hh] / stats[g * nh + hh][1] for hh in range(nh)], axis=0)
        o_ref[0, :, g * LANES:(g + 1) * LANES] = out_t.T


MOBA_GROUPS = 4


def _moba(hp3d, bias):
    b, s, _ = hp3d.shape
    ng = MOBA_GROUPS
    wd = ng * LANES
    nsteps = A_WIDTH // wd
    hpp = LANES // A_HEAD_DIM
    nb = s // MOBA_BLOCK
    return pl.pallas_call(
        _moba_kernel,
        grid=(b, nsteps, nb),
        in_specs=[pl.BlockSpec((1, MOBA_BLOCK, wd), lambda bi, p, i: (bi, i, p)),
                  pl.BlockSpec((1, s, wd), lambda bi, p, i: (bi, 0, nsteps + p), pipeline_mode=pl.Buffered(1)),
                  pl.BlockSpec((1, s, wd), lambda bi, p, i: (bi, 0, 2 * nsteps + p), pipeline_mode=pl.Buffered(1)),
                  pl.BlockSpec((ng * hpp, 2, MOBA_BLOCK, MOBA_BLOCK), lambda bi, p, i: (p, 0, 0, 0))],
        out_specs=pl.BlockSpec((1, MOBA_BLOCK, wd), lambda bi, p, i: (bi, i, p)),
        out_shape=jax.ShapeDtypeStruct((b, s, A_WIDTH), F32),
        scratch_shapes=[pltpu.VMEM((ng, nb, LANES), F32),
                        pltpu.VMEM((ng, nb + 1, MOBA_BLOCK, 2 * LANES), MXU_DTYPE),
                        pltpu.VMEM((ng, nb + 1, LANES, MOBA_BLOCK), MXU_DTYPE),
                        pltpu.VMEM((ng, 2, MOBA_BLOCK, hpp * MOBA_BLOCK), F32),
                        pltpu.VMEM((ng, 2, MOBA_BLOCK, hpp * MOBA_BLOCK), F32),
                        pltpu.VMEM((ng, hpp, 2 * MOBA_BLOCK, MOBA_BLOCK), MXU_DTYPE),
                        pltpu.VMEM((ng, hpp, 2 * MOBA_BLOCK, MOBA_BLOCK), MXU_DTYPE),
                        pltpu.VMEM((ng, 2 * LANES, hpp * MOBA_BLOCK), MXU_DTYPE),
                        pltpu.VMEM((ng, hpp, A_HEAD_DIM, MOBA_BLOCK), F32)],
        compiler_params=_params("parallel", "parallel", "arbitrary"),
        name="moba",
    )(hp3d, hp3d, hp3d, bias)


S5_HALF_CH = S5_WIDTH // 2
S5_HALF_ST = (S5_GROUPS // 2) * S5_STATE
S5_TILE = 256
S5_NT = S5_HALF_ST // S5_TILE
S5_STEPS = 64
S5_PITCH = S5_STEPS + SUBLANES
S5_SLABS = 2 * S5_HALF_ST // LANES


def _s5_kernel(u_ref, wx_ref, wy_ref, a_ref, o_ref, st_ref, x_scr, h_scr):
    nb, steps, _ = u_ref.shape
    nseq = 2 * nb
    tile_slabs = S5_TILE // LANES

    @pl.when(pl.program_id(0) == 0)
    def _():
        st_ref[...] = jnp.zeros_like(st_ref)

    zeros = jnp.zeros((steps, LANES), MXU_DTYPE)
    for m in range(S5_NT // 2):
        pieces = []
        for seq in range(nseq):
            bi, half = divmod(seq, 2)
            c = half * S5_HALF_CH + m * LANES
            piece = _mx(u_ref[bi, :, c:c + LANES])
            pieces.append(jnp.concatenate([piece, zeros] if half == 0 else [zeros, piece], axis=1))
        lhs = jnp.concatenate(pieces, axis=0)
        for part in range(2):
            for n in (2 * m, 2 * m + 1):
                res = jnp.dot(lhs, wx_ref[part, n], preferred_element_type=F32)
                slab0 = (part * S5_HALF_ST + n * S5_TILE) // LANES
                for seq in range(nseq):
                    for sl in range(tile_slabs):
                        x_scr[slab0 + sl, seq * S5_PITCH:seq * S5_PITCH + steps, :] = (
                            res[seq * steps:(seq + 1) * steps, sl * LANES:(sl + 1) * LANES])

    nre = S5_SLABS // 2
    a_re = [a_ref[0, :, s * LANES:(s + 1) * LANES] for s in range(nre)]
    a_im = [a_ref[1, :, s * LANES:(s + 1) * LANES] for s in range(nre)]

    def step(t, carry):
        new_re, new_im = [], []
        for s in range(nre):
            hr, hi = carry[s], carry[nre + s]
            rows_t = pl.ds(t, nseq, stride=S5_PITCH)
            nhr = a_re[s] * hr - a_im[s] * hi + x_scr[s, rows_t, :]
            nhi = a_re[s] * hi + a_im[s] * hr + x_scr[nre + s, rows_t, :]
            h_scr[s, rows_t, :] = nhr
            h_scr[nre + s, rows_t, :] = nhi
            new_re.append(nhr)
            new_im.append(nhi)
        return tuple(new_re + new_im)

    init = tuple(st_ref[p, :, s * LANES:(s + 1) * LANES] for p in range(2) for s in range(nre))
    final = lax.fori_loop(0, steps, step, init, unroll=8)
    for p in range(2):
        for s in range(nre):
            st_ref[p, :, s * LANES:(s + 1) * LANES] = final[p * nre + s]

    for m in range(S5_NT // 2):
        acc = jnp.zeros((nseq * steps, 2 * LANES), F32)
        for n in (2 * m, 2 * m + 1):
            for part in range(2):
                slab0 = (part * S5_HALF_ST + n * S5_TILE) // LANES
                h_tile = jnp.concatenate(
                    [jnp.concatenate([h_scr[slab0 + sl, seq * S5_PITCH:seq * S5_PITCH + steps, :]
                                      for sl in range(tile_slabs)], axis=1) for seq in range(nseq)], axis=0)
                acc = acc + jnp.dot(_mx(h_tile), wy_ref[part, n], preferred_element_type=F32)
        for seq in range(nseq):
            bi, half = divmod(seq, 2)
            c = half * S5_HALF_CH + m * LANES
            o_ref[bi, :, c:c + LANES] = acc[seq * steps:(seq + 1) * steps, half * LANES:(half + 1) * LANES]


def _s5_weights(lam_re, lam_im, b_re, b_im, c_re, c_im, log_dt):
    lr = jnp.minimum(lam_re.astype(F32), -1e-4)
    li = lam_im.astype(F32)
    dt = jnp.exp(log_dt.astype(F32))[:, None]
    mag = jnp.exp(lr * dt)
    a_re = mag * jnp.cos(li * dt)
    a_im = mag * jnp.sin(li * dt)
    den = lr * lr + li * li
    coef_re = ((a_re - 1.0) * lr + a_im * li) / den
    coef_im = (a_im * lr - (a_re - 1.0) * li) / den
    br = b_re.astype(F32)
    bi = b_im.astype(F32)
    bb_re = coef_re[..., None] * br - coef_im[..., None] * bi
    bb_im = coef_re[..., None] * bi + coef_im[..., None] * br
    gh = S5_GROUPS // 2
    eye = jnp.eye(gh, dtype=F32)

    def b_dense(bb):
        t = bb.reshape(2, gh, S5_STATE, S5_GROUP)
        return jnp.einsum('fgph,gk->fghkp', t, eye).reshape(2, S5_HALF_CH, S5_HALF_ST)

    def c_dense(cc):
        t = cc.reshape(2, gh, S5_GROUP, S5_STATE)
        return jnp.einsum('fghp,gk->fgpkh', t, eye).reshape(2, S5_HALF_ST, S5_HALF_CH)

    def x_tiles(bd):
        tiles = []
        for n in range(S5_NT):
            c = (n // 2) * LANES
            tiles.append(jnp.concatenate([bd[0, c:c + LANES, n * S5_TILE:(n + 1) * S5_TILE],
                                          bd[1, c:c + LANES, n * S5_TILE:(n + 1) * S5_TILE]], axis=0))
        return jnp.stack(tiles)

    def y_tiles(cd):
        tiles = []
        for n in range(S5_NT):
            c = (n // 2) * LANES
            tiles.append(jnp.concatenate([cd[0, n * S5_TILE:(n + 1) * S5_TILE, c:c + LANES],
                                          cd[1, n * S5_TILE:(n + 1) * S5_TILE, c:c + LANES]], axis=1))
        return jnp.stack(tiles)

    wx = jnp.stack([x_tiles(b_dense(bb_re)), x_tiles(b_dense(bb_im))]).astype(MXU_DTYPE)
    wy = jnp.stack([y_tiles(c_dense(c_re.astype(F32))), y_tiles(c_dense(-c_im.astype(F32)))]).astype(MXU_DTYPE)
    return wx, wy, a_re.reshape(2, S5_HALF_ST), a_im.reshape(2, S5_HALF_ST)


def _s5_scan(hp3d, wx, wy, a_re2, a_im2):
    b, s, width = hp3d.shape
    nseq = 2 * b
    assert nseq == SUBLANES
    sel = jnp.arange(nseq) % 2
    a = jnp.stack([a_re2[sel], a_im2[sel]])
    scr = pltpu.VMEM((S5_SLABS, nseq * S5_PITCH, LANES), F32)
    return pl.pallas_call(
        _s5_kernel,
        grid=(s // S5_STEPS,),
        in_specs=[pl.BlockSpec((b, S5_STEPS, S5_WIDTH), lambda i: (0, i, width // S5_WIDTH - 1)),
                  _resident(wx.shape), _resident(wy.shape), _resident(a.shape)],
        out_specs=pl.BlockSpec((b, S5_STEPS, S5_WIDTH), lambda i: (0, i, 0)),
        out_shape=jax.ShapeDtypeStruct((b, s, S5_WIDTH), F32),
        scratch_shapes=[pltpu.VMEM((2, nseq, S5_HALF_ST), F32), scr, scr],
        compiler_params=_params("arbitrary"),
        name="s5_scan",
    )(hp3d, wx, wy, a)


MLP_CHUNK = 1024


def _tail_body(mix, x_ref, g1_ref, b1_ref, w1_ref, w2_ref, g2_ref, b2_ref, o_ref):
    x1 = _layer_norm(DN_ALPHA * x_ref[...] + mix, g1_ref[...], b1_ref[...])
    x1m = _mx(x1)
    acc = jnp.zeros(x1.shape, F32)
    for f in range(w1_ref.shape[1] // MLP_CHUNK):
        hid = jnp.dot(x1m, w1_ref[:, f * MLP_CHUNK:(f + 1) * MLP_CHUNK], preferred_element_type=F32)
        hid = jnp.square(jnp.maximum(hid, 0.0))
        acc = acc + jnp.dot(_mx(hid), w2_ref[f * MLP_CHUNK:(f + 1) * MLP_CHUNK, :],
                            preferred_element_type=F32)
    o_ref[...] = _layer_norm(DN_ALPHA * x1 + acc, g2_ref[...], b2_ref[...])


def _tail_even_kernel(attn_ref, y_ref, u_ref, dsk_ref, wg_ref, bg_ref, wo_ref, x_ref,
                      g1_ref, b1_ref, w1_ref, w2_ref, g2_ref, b2_ref, o_ref):
    y = y_ref[...] + dsk_ref[...] * u_ref[...]
    g = 0.5 * y * (1.0 + jnp.tanh(math.sqrt(2.0 / math.pi) * (y + 0.044715 * (y * y * y))))
    ssm = g * _sigmoid(jnp.dot(_mx(g), wg_ref[...], preferred_element_type=F32) + bg_ref[...])
    ka = attn_ref.shape[1]
    mix = (jnp.dot(_mx(attn_ref[...]), wo_ref[0:ka, :], preferred_element_type=F32)
           + jnp.dot(_mx(ssm), wo_ref[ka:, :], preferred_element_type=F32))
    _tail_body(mix, x_ref, g1_ref, b1_ref, w1_ref, w2_ref, g2_ref, b2_ref, o_ref)


def _tail_odd_kernel(a_ref, wo_ref, x_ref, g1_ref, b1_ref, w1_ref, w2_ref, g2_ref, b2_ref, o_ref):
    mix = jnp.dot(_mx(a_ref[...]), wo_ref[...], preferred_element_type=F32)
    _tail_body(mix, x_ref, g1_ref, b1_ref, w1_ref, w2_ref, g2_ref, b2_ref, o_ref)


def _row(v):
    return v.reshape(1, -1).astype(F32)


def _tail_common_specs(tm, d, dff):
    return [_resident((d, d)),
            pl.BlockSpec((tm, d), lambda i: (i, 0)),
            _resident((1, d)), _resident((1, d)),
            _resident((d, dff)), _resident((dff, d)),
            _resident((1, d)), _resident((1, d))]


def _tail_even(attn2d, y2d, hp2d, x2d, d_skip, w_glu, b_glu, w_out, g1, b1, w1, w2, g2, b2, tm):
    m, d = x2d.shape
    dff = w1.shape[1]
    ucol = hp2d.shape[1] // S5_WIDTH - 1
    return pl.pallas_call(
        _tail_even_kernel,
        grid=(m // tm,),
        in_specs=[pl.BlockSpec((tm, A_WIDTH), lambda i: (i, 0)),
                  pl.BlockSpec((tm, S5_WIDTH), lambda i: (i, 0)),
                  pl.BlockSpec((tm, S5_WIDTH), lambda i: (i, ucol)),
                  _resident((1, S5_WIDTH)), _resident((S5_WIDTH, S5_WIDTH)), _resident((1, S5_WIDTH))]
        + _tail_common_specs(tm, d, dff),
        out_specs=pl.BlockSpec((tm, d), lambda i: (i, 0)),
        out_shape=jax.ShapeDtypeStruct((m, d), F32),
        compiler_params=_params("parallel"),
        name="tail_even",
    )(attn2d, y2d, hp2d, _row(d_skip), _mx(w_glu), _row(b_glu), _mx(w_out), x2d,
      _row(g1), _row(b1), _mx(w1), _mx(w2), _row(g2), _row(b2))


def _tail_odd(a2d, x2d, w_out, g1, b1, w1, w2, g2, b2, tm):
    m, d = x2d.shape
    dff = w1.shape[1]
    return pl.pallas_call(
        _tail_odd_kernel,
        grid=(m // tm,),
        in_specs=[pl.BlockSpec((tm, a2d.shape[1]), lambda i: (i, 0))] + _tail_common_specs(tm, d, dff),
        out_specs=pl.BlockSpec((tm, d), lambda i: (i, 0)),
        out_shape=jax.ShapeDtypeStruct((m, d), F32),
        compiler_params=_params("parallel"),
        name="tail_odd",
    )(a2d, _mx(w_out), x2d, _row(g1), _row(b1), _mx(w1), _mx(w2), _row(g2), _row(b2))


GDN_TILE = 256
GDN_GATES_TILE = 512


def _softplus(t):
    return jnp.maximum(t, 0.0) + jnp.log1p(jnp.exp(-jnp.abs(t)))


def _gdn_prep_body(i, hg, q_ref, k_ref, v_ref, qp_ref, kp_ref, vp_ref, cwq_ref, cwk_ref, cwv_ref,
                   ba_ref, g_ref, gt_ref, u_ref, w_ref, qd_ref, kd_ref, intra_ref, edl_ref,
                   qx_ref, kx_ref, vx_ref, ql_ref, kl_ref, vl_ref):
    assert C_CONV == 4
    t, width = q_ref.shape[1], q_ref.shape[2]
    c = C_CHUNK
    dk = C_HEAD_DIM
    hp = width // dk
    groups = t // SUBLANES

    def conv_silu(main_ref, prev_ref, cw_ref, ext_ref, low_ref):
        x = main_ref[0]
        ext_ref[0:SUBLANES, :] = jnp.zeros((SUBLANES, width), F32)
        ext_ref[SUBLANES:2 * SUBLANES, :] = jnp.where(i > 0, prev_ref[0], 0.0)
        ext_ref[2 * SUBLANES:, :] = x
        ext_rows = t + SUBLANES

        def tap(j):
            return jnp.broadcast_to(cw_ref[j:j + 1, :], (SUBLANES, width))[None]

        def grouped(a):
            return a.reshape(a.shape[0] // SUBLANES, SUBLANES, width)

        x_ext = grouped(ext_ref[SUBLANES:, :])
        x_del = grouped(ext_ref[SUBLANES - 1:SUBLANES - 1 + ext_rows, :])
        low_ref[...] = (tap(1) * x_ext + tap(0) * x_del).reshape(ext_rows, width)
        y = (tap(3) * grouped(x) + tap(2) * x_del[1:]
             + grouped(low_ref[SUBLANES - 2:SUBLANES - 2 + t, :]))
        hy = 0.5 * y.reshape(t, width)
        return hy + hy * jnp.tanh(hy)

    def l2n(z, scale):
        return z * (lax.rsqrt(jnp.sum(z * z, axis=-1, keepdims=True) + L2_EPS) * scale)

    q_all = conv_silu(q_ref, qp_ref, cwq_ref, qx_ref, ql_ref)
    k_all = conv_silu(k_ref, kp_ref, cwk_ref, kx_ref, kl_ref)
    v_all = conv_silu(v_ref, vp_ref, cwv_ref, vx_ref, vl_ref)
    heads = [slice(hh * dk, (hh + 1) * dk) for hh in range(hp)]
    qs = [l2n(q_all[:, cs], dk ** -0.5) for cs in heads]
    ks = [l2n(k_all[:, cs], 1.0) for cs in heads]
    vs = [v_all[:, cs] for cs in heads]

    ba = ba_ref[0]
    lane = lax.broadcasted_iota(jnp.int32, (t, LANES), 1)
    betas, g_cols, g_rows = [], [], []
    for hh in range(hp):
        h = hg * hp + hh
        betas.append(_sigmoid(jnp.sum(jnp.where(lane == h, ba, 0.0), axis=1, keepdims=True)))
        g_cols.append(jnp.sum(jnp.where(lane == C_HEADS + h, g_ref[0], 0.0), axis=1, keepdims=True))
        g_rows.append(gt_ref[0, pl.ds(C_HEADS + h, 1), :])

    ii = lax.broadcasted_iota(jnp.int32, (c, c), 0)
    jj = lax.broadcasted_iota(jnp.int32, (c, c), 1)
    low = ii >= jj
    units = [(hh, slice(ch * c, (ch + 1) * c)) for hh in range(hp) for ch in range(t // c)]
    gc_cols = [jnp.sum(jnp.where(low, g_rows[hh][:, r], 0.0), axis=1, keepdims=True) for hh, r in units]
    gc_rows = [jnp.sum(jnp.where(ii <= jj, g_cols[hh][r], 0.0), axis=0, keepdims=True) for hh, r in units]
    decays = [jnp.where(low, jnp.exp(jnp.where(low, a - b, 0.0)), 0.0) for a, b in zip(gc_cols, gc_rows)]
    kbs = [ks[hh][r] * betas[hh][r] for hh, r in units]
    kks = [_dot_nt(kb, ks[hh][r]) for kb, (hh, r) in zip(kbs, units)]
    qks = [_dot_nt(qs[hh][r], ks[hh][r]) for hh, r in units]
    pws = [jnp.where(ii > jj, -(kk * dec), 0.0) for kk, dec in zip(kks, decays)]
    tms = pws
    for _ in range(int(math.log2(c)) - 1):
        pbs = [_mx(pw) for pw in pws]
        pws = [_dot(pb, pb) for pb in pbs]
        pbs = [_mx(pw) for pw in pws]
        tms = [tm + pw + _dot(tm, pb) for tm, pw, pb in zip(tms, pws, pbs)]
    egcs = [jnp.exp(a) for a in gc_cols]
    vbs = [vs[hh][r] * betas[hh][r] for hh, r in units]
    kbes = [kb * e for kb, e in zip(kbs, egcs)]
    us = [vb + _dot(tm, vb) for tm, vb in zip(tms, vbs)]
    ws = [kbe + _dot(tm, kbe) for tm, kbe in zip(tms, kbes)]
    for n, (hh, r) in enumerate(units):
        gl = gc_cols[n][c - 1:c, :]
        cs = heads[hh]
        ch = r.start // c
        u_ref[0, r, cs] = us[n]
        w_ref[0, r, cs] = _mx(ws[n])
        qd_ref[0, r, cs] = _mx(qs[hh][r] * egcs[n])
        kd_ref[0, r, cs] = _mx(ks[hh][r] * jnp.exp(gl - gc_cols[n]))
        intra_ref[0, hh, r, :] = _mx(qks[n] * decays[n])
        edl_ref[0, hh, ch:ch + 1, :] = jnp.broadcast_to(jnp.exp(gl), (1, LANES))


def _gdn_gates_kernel(ba_ref, alog_ref, dtb_ref, g_ref, gt_ref):
    g_all = -jnp.exp(alog_ref[...]) * _softplus(ba_ref[0] + dtb_ref[...])
    g_ref[0] = g_all
    gt_ref[0] = g_all.T


def _gdn_gates(hp3d, alog_row, dtb_row):
    b, s, _ = hp3d.shape
    t = GDN_GATES_TILE
    ba_col = 4 * C_WIDTH // LANES
    return pl.pallas_call(
        _gdn_gates_kernel,
        grid=(b, s // t),
        in_specs=[pl.BlockSpec((1, t, LANES), lambda bi, i: (bi, i, ba_col)),
                  pl.BlockSpec((1, LANES), lambda bi, i: (0, 0)),
                  pl.BlockSpec((1, LANES), lambda bi, i: (0, 0))],
        out_specs=[pl.BlockSpec((1, t, LANES), lambda bi, i: (bi, i, 0)),
                   pl.BlockSpec((1, LANES, t), lambda bi, i: (bi, 0, i))],
        out_shape=[jax.ShapeDtypeStruct((b, s, LANES), F32), jax.ShapeDtypeStruct((b, LANES, s), F32)],
        compiler_params=_params("parallel", "parallel"),
        name="gdn_gates",
    )(hp3d, alog_row, dtb_row)


def _gdn_scan_body(chunk0, u_ref, w_ref, qd_ref, kd_ref, intra_ref, edl_ref, gate_ref, nw_ref, o_ref, st_ref):
    c = C_CHUNK
    dk = C_HEAD_DIM
    t = u_ref.shape[1]
    nw = nw_ref[...]
    heads = [slice(h * dk, (h + 1) * dk) for h in range(C_HEADS)]
    for ch in range(t // c):
        r = slice(ch * c, (ch + 1) * c)
        sts = [st_ref[h] for h in range(C_HEADS)]
        wss = [_dot(jnp.concatenate([w_ref[0, r, cs], qd_ref[0, r, cs]], axis=0), st)
               for cs, st in zip(heads, sts)]
        v_news = [u_ref[0, r, cs] - ws[:c] for cs, ws in zip(heads, wss)]
        kvs = [_dot_tn(kd_ref[0, r, cs], v_new) for cs, v_new in zip(heads, v_news)]
        os_ = [ws[c:] + _dot(intra_ref[0, h, r, :], v_new) for h, (ws, v_new) in enumerate(zip(wss, v_news))]
        for h, cs in enumerate(heads):
            st_ref[h] = sts[h] * edl_ref[0, h, pl.ds(chunk0 + ch, 1), 0:1] + kvs[h]
            o = os_[h]
            o = o * lax.rsqrt(jnp.mean(o * o, axis=-1, keepdims=True) + RMS_EPS) * nw
            gt = gate_ref[0, r, cs]
            o_ref[0, r, cs] = o * (gt * _sigmoid(gt))


def _gdn_kernel(q_ref, k_ref, v_ref, qp_ref, kp_ref, vp_ref, cwq_ref, cwk_ref, cwv_ref, ba_ref, g_ref, gt_ref,
                gate_ref, nw_ref, o_ref, st_ref, qx_ref, kx_ref, vx_ref, ql_ref, kl_ref, vl_ref, *bufs):
    j = pl.program_id(1)
    last = pl.num_programs(1) - 2
    cur, nxt = bufs[:len(bufs) // 2], bufs[len(bufs) // 2:]

    @pl.when(j == 0)
    def _():
        st_ref[...] = jnp.zeros_like(st_ref)
        for ref in bufs:
            ref[...] = jnp.zeros_like(ref)

    _gdn_prep_body(jnp.minimum(j, last), 0, q_ref, k_ref, v_ref, qp_ref, kp_ref, vp_ref, cwq_ref, cwk_ref, cwv_ref,
                   ba_ref, g_ref, gt_ref, *nxt, qx_ref, kx_ref, vx_ref, ql_ref, kl_ref, vl_ref)
    _gdn_scan_body(0, *cur, gate_ref, nw_ref, o_ref, st_ref)
    for dst, src in zip(cur, nxt):
        dst[...] = src[...]


def _gdn(hp3d, conv_w, alog_row, dtb_row, norm_w):
    b, s, _ = hp3d.shape
    ba_col = 4 * C_WIDTH // LANES
    t = GDN_TILE
    nt = s // t
    per = t // SUBLANES
    wd = C_WIDTH
    nc = t // C_CHUNK
    g, gt = _gdn_gates(hp3d, alog_row, dtb_row)

    def prep_tile(j):
        return jnp.minimum(j, nt - 1)

    def scan_tile(j):
        return jnp.maximum(j - 1, 0)

    def main(off):
        return pl.BlockSpec((1, t, wd), lambda bi, j: (bi, prep_tile(j), off))

    def prev(off):
        return pl.BlockSpec((1, SUBLANES, wd), lambda bi, j: (bi, jnp.maximum(prep_tile(j) * per - 1, 0), off))

    def cw(off):
        return pl.BlockSpec((C_CONV, wd), lambda bi, j: (0, off))

    bufs = [pltpu.VMEM((1, t, wd), F32)] + [pltpu.VMEM((1, t, wd), MXU_DTYPE)] * 3 + [
        pltpu.VMEM((1, C_HEADS, t, C_CHUNK), MXU_DTYPE), pltpu.VMEM((1, C_HEADS, max(nc, SUBLANES), LANES), F32)]
    return pl.pallas_call(
        _gdn_kernel,
        grid=(b, nt + 1),
        in_specs=[main(0), main(1), main(2), prev(0), prev(1), prev(2), cw(0), cw(1), cw(2),
                  pl.BlockSpec((1, t, LANES), lambda bi, j: (bi, prep_tile(j), ba_col)),
                  pl.BlockSpec((1, t, LANES), lambda bi, j: (bi, prep_tile(j), 0)),
                  pl.BlockSpec((1, LANES, t), lambda bi, j: (bi, 0, prep_tile(j))),
                  pl.BlockSpec((1, t, wd), lambda bi, j: (bi, scan_tile(j), 3)),
                  pl.BlockSpec((1, C_HEAD_DIM), lambda bi, j: (0, 0))],
        out_specs=pl.BlockSpec((1, t, wd), lambda bi, j: (bi, scan_tile(j), 0)),
        out_shape=jax.ShapeDtypeStruct((b, s, wd), F32),
        scratch_shapes=[pltpu.VMEM((C_HEADS, C_HEAD_DIM, C_HEAD_DIM), F32)]
        + [pltpu.VMEM((t + 2 * SUBLANES, wd), F32)] * 3 + [pltpu.VMEM((t + SUBLANES, wd), F32)] * 3
        + bufs + bufs,
        compiler_params=_params("parallel", "arbitrary"),
        name="gdn",
    )(hp3d, hp3d, hp3d, hp3d, hp3d, hp3d, conv_w, conv_w, conv_w, hp3d, g, gt, hp3d, _row(norm_w))


ROW_TILE = 512


def kernel(x, rel_table, ev_w_in, ev_w_out, s5_lambda_re, s5_lambda_im, s5_b_re, s5_b_im, s5_c_re, s5_c_im, s5_d, s5_log_dt, s5_w_glu, s5_b_glu, od_w_in, od_conv_w, od_a_log, od_dt_bias, od_norm_w, od_w_out, ln_mix_g, ln_mix_b, mlp_w1, mlp_w2, ln_ffn_g, ln_ffn_b):
    bsz, s, d = x.shape
    m = bsz * s
    x2d = x.reshape(m, d)
    table_t = rel_table.T.astype(F32)
    bias = _bias_tiles(table_t)
    for layer in range(DEPTH):
        i = layer // 2
        if layer % 2 == 0:
            hp = _proj(x2d, _mx(ev_w_in[i]), ROW_TILE)
            hp3d = hp.reshape(bsz, s, hp.shape[1])
            attn = _moba(hp3d, bias)
            wx, wy, a_re2, a_im2 = _s5_weights(s5_lambda_re[i], s5_lambda_im[i], s5_b_re[i], s5_b_im[i],
                                               s5_c_re[i], s5_c_im[i], s5_log_dt[i])
            y = _s5_scan(hp3d, wx, wy, a_re2, a_im2)
            x2d = _tail_even(attn.reshape(m, A_WIDTH), y.reshape(m, S5_WIDTH), hp, x2d,
                             s5_d[i], s5_w_glu[i], s5_b_glu[i], ev_w_out[i],
                             ln_mix_g[layer], ln_mix_b[layer], mlp_w1[layer], mlp_w2[layer],
                             ln_ffn_g[layer], ln_ffn_b[layer], ROW_TILE)
        else:
            w_in = od_w_in[i]
            w_all = jnp.pad(w_in, ((0, 0), (0, LANES - 2 * C_HEADS)))
            hp = _proj(x2d, _mx(w_all), ROW_TILE)
            hp3d = hp.reshape(bsz, s, 4 * C_WIDTH + LANES)
            pad8 = jnp.zeros((C_HEADS,), F32)
            padr = jnp.zeros((LANES - 2 * C_HEADS,), F32)
            alog_row = jnp.concatenate([pad8, od_a_log[i].astype(F32), padr]).reshape(1, LANES)
            dtb_row = jnp.concatenate([pad8, od_dt_bias[i].astype(F32), padr]).reshape(1, LANES)
            o = _gdn(hp3d, od_conv_w[i].astype(F32), alog_row, dtb_row, od_norm_w[i])
            x2d = _tail_odd(o.reshape(m, C_WIDTH), x2d, od_w_out[i],
                            ln_mix_g[layer], ln_mix_b[layer], mlp_w1[layer], mlp_w2[layer],
                            ln_ffn_g[layer], ln_ffn_b[layer], ROW_TILE)
    return x2d.reshape(bsz, s, d)
```

```python
import functools
import math

import jax
import jax.numpy as jnp
import numpy as np
from jax import lax
from jax.experimental import pallas as pl
from jax.experimental.pallas import tpu as pltpu

F32 = jnp.float32
MXU_DTYPE = jnp.bfloat16

LANES = 128
SUBLANES = 8
VMEM_LIMIT_BYTES = 56 * 1024 * 1024

A_HEADS = 8
A_HEAD_DIM = 64
A_WIDTH = A_HEADS * A_HEAD_DIM
MOBA_BLOCK = 256
MOBA_TOPK = 3
REL_BUCKETS = 32
REL_MAX_DIST = 128
S5_GROUP = 16
S5_GROUPS = 32
S5_STATE = 64
S5_WIDTH = S5_GROUP * S5_GROUPS
C_HEADS = 8
C_HEAD_DIM = 128
C_WIDTH = C_HEADS * C_HEAD_DIM
C_CONV = 4
C_CHUNK = 64
DEPTH = 2
DN_ALPHA = (2 * DEPTH) ** 0.25
LN_EPS = 1e-5
RMS_EPS = 1e-6
L2_EPS = 1e-6
NEG_INF = -1e30
LOG2E = math.log2(math.e)


def _mx(t):
    return t.astype(MXU_DTYPE)


def _dot(a, b):
    return jnp.dot(_mx(a), _mx(b), preferred_element_type=F32)


def _dot_tn(a, b):
    return lax.dot_general(_mx(a), _mx(b), (((0,), (0,)), ((), ())), preferred_element_type=F32)


def _dot_nt(a, b):
    return lax.dot_general(_mx(a), _mx(b), (((1,), (1,)), ((), ())), preferred_element_type=F32)


def _sigmoid(t):
    return 1.0 / (1.0 + jnp.exp(-t))


def _layer_norm(r, g, b):
    mu = jnp.mean(r, axis=-1, keepdims=True)
    d = r - mu
    var = jnp.mean(d * d, axis=-1, keepdims=True)
    return d * lax.rsqrt(var + LN_EPS) * g + b


def _params(*sem):
    return pltpu.CompilerParams(dimension_semantics=sem, vmem_limit_bytes=VMEM_LIMIT_BYTES)


def _resident(shape):
    return pl.BlockSpec(shape, lambda *_: (0,) * len(shape), pipeline_mode=pl.Buffered(1))


PROJ_CHUNK = 1024


def _proj_kernel(x_ref, w_ref, o_ref):
    xm = _mx(x_ref[...])
    n = w_ref.shape[1]
    for c0 in range(0, n, PROJ_CHUNK):
        c1 = min(c0 + PROJ_CHUNK, n)
        o_ref[:, c0:c1] = jnp.dot(xm, w_ref[:, c0:c1], preferred_element_type=F32)


def _proj(x2d, w, tm):
    m, k = x2d.shape
    n = w.shape[1]
    return pl.pallas_call(
        _proj_kernel,
        grid=(m // tm,),
        in_specs=[pl.BlockSpec((tm, k), lambda i: (i, 0)), _resident((k, n))],
        out_specs=pl.BlockSpec((tm, n), lambda i: (i, 0)),
        out_shape=jax.ShapeDtypeStruct((m, n), F32),
        compiler_params=_params("parallel"),
        name="in_proj",
    )(x2d, w)


def _t5_bucket_np(dist):
    max_exact = REL_BUCKETS // 2
    n = np.maximum(dist, 0)
    nf = np.maximum(n, 1).astype(np.float32)
    large = max_exact + (np.log(nf / np.float32(max_exact)) / np.float32(math.log(REL_MAX_DIST / max_exact))
                         * np.float32(REL_BUCKETS - max_exact)).astype(np.int32)
    large = np.minimum(large, REL_BUCKETS - 1)
    return np.where(n < max_exact, n, large).astype(np.int32)


def _bias_kernel(tab_ref, bkt_ref, o_ref):
    h = pl.program_id(0)
    blk = bkt_ref.shape[-1]
    key = lax.broadcasted_iota(jnp.int32, (blk, blk), 0)
    qry = lax.broadcasted_iota(jnp.int32, (blk, blk), 1)
    far = tab_ref[h, REL_BUCKETS - 1]
    for t in range(2):
        bk = bkt_ref[t]
        acc = jnp.zeros((blk, blk), F32)
        for i in range(REL_BUCKETS):
            acc = jnp.where(bk == i, (tab_ref[h, i] - far) * LOG2E, acc)
        if t == 0:
            acc = jnp.where(qry >= key, acc, NEG_INF)
        o_ref[0, t] = acc


def _bias_tiles(table_t):
    heads = table_t.shape[0]
    offs = np.arange(MOBA_BLOCK)
    dist_own = offs[None, :] - offs[:, None]
    buckets = np.stack([_t5_bucket_np(dist_own), _t5_bucket_np(dist_own + MOBA_BLOCK)])
    return pl.pallas_call(
        _bias_kernel,
        grid=(heads,),
        in_specs=[pl.BlockSpec(memory_space=pltpu.SMEM),
                  pl.BlockSpec((2, MOBA_BLOCK, MOBA_BLOCK), lambda h: (0, 0, 0))],
        out_specs=pl.BlockSpec((1, 2, MOBA_BLOCK, MOBA_BLOCK), lambda h: (h, 0, 0, 0)),
        out_shape=jax.ShapeDtypeStruct((heads, 2, MOBA_BLOCK, MOBA_BLOCK), F32),
        compiler_params=_params("parallel"),
        name="t5_bias",
    )(table_t, jnp.asarray(buckets))


def _moba_kernel(q_ref, k_ref, v_ref, bias_ref, o_ref, kmean_ref, kb_ref, vt_ref,
                 st0_ref, st1_ref, p0_ref, p1_ref, wq_ref, acc_ref):
    st_refs = (st0_ref, st1_ref)
    p_refs = (p0_ref, p1_ref)
    qb = pl.program_id(2)
    blk = MOBA_BLOCK
    d = A_HEAD_DIM
    nh = LANES // d
    nb = k_ref.shape[1] // blk
    ng = q_ref.shape[2] // LANES
    groups = range(ng)

    @pl.when(qb == 0)
    def _():
        lane = lax.broadcasted_iota(jnp.int32, (blk, LANES - d), 1)
        for g in groups:
            gl = slice(g * LANES, (g + 1) * LANES)
            for j in range(nb + 1):
                onehot = jnp.where(lane == j, 1.0, 0.0).astype(MXU_DTYPE)
                for hh in range(nh):
                    kb_ref[g, hh, j, :, d:LANES] = onehot
            for j in range(nb):
                kj = k_ref[0, j * blk:(j + 1) * blk, gl]
                for hh in range(nh):
                    kb_ref[g, hh, j, :, 0:d] = _mx(kj[:, hh * d:(hh + 1) * d])
                kmean_ref[g, j:j + 1, :] = jnp.mean(kj, axis=0, keepdims=True)
                vt_ref[g, j] = _mx(v_ref[0, j * blk:(j + 1) * blk, gl].T)
            for hh in range(nh):
                kb_ref[g, hh, nb, :, 0:d] = jnp.zeros((blk, d), MXU_DTYPE)
            vt_ref[g, nb] = jnp.zeros((LANES, blk), MXU_DTYPE)

    blk_id = lax.broadcasted_iota(jnp.int32, (nb, blk), 0)
    row = lax.broadcasted_iota(jnp.int32, (LANES - d - nb, blk), 0)
    for g in groups:
        qt = (q_ref[0, :, g * LANES:(g + 1) * LANES] * (d ** -0.5)).T
        for hh in range(nh):
            gate = lax.dot_general(kmean_ref[g, :, hh * d:(hh + 1) * d], qt[hh * d:(hh + 1) * d],
                                   (((1,), (0,)), ((), ())),
                                   precision=lax.Precision.HIGHEST, preferred_element_type=F32)
            gt = jnp.where(blk_id < qb, gate, -jnp.inf)
            sel = jnp.zeros((nb, blk), F32)
            for _ in range(MOBA_TOPK):
                m = jnp.max(gt, axis=0, keepdims=True)
                is_m = jnp.logical_and(gt == m, gt > -jnp.inf)
                idx = jnp.min(jnp.where(is_m, blk_id, nb), axis=0, keepdims=True)
                pick = blk_id == idx
                sel = jnp.where(pick, 1.0, sel)
                gt = jnp.where(pick, -jnp.inf, gt)
            keep = jnp.logical_or(sel > 0.0, blk_id == qb)
            wq_ref[g, hh] = _mx(jnp.concatenate([qt[hh * d:(hh + 1) * d] * LOG2E, jnp.where(keep, 0.0, NEG_INF),
                                                 jnp.where(row == 0, NEG_INF, 0.0)], axis=0))

    def score_stage(slot, ja, jb, with_bias):
        cmaxes = []
        for g in groups:
            for hh in range(nh):
                keys = jnp.concatenate([kb_ref[g, hh, ja], kb_ref[g, hh, jb]], axis=0)
                st = jnp.dot(keys, wq_ref[g, hh], preferred_element_type=F32)
                if with_bias:
                    st = st + jnp.concatenate([bias_ref[g * nh + hh, w] for w in range(2)], axis=0)
                st_refs[slot][g, 0, :, hh * blk:(hh + 1) * blk] = st[:blk]
                st_refs[slot][g, 1, :, hh * blk:(hh + 1) * blk] = st[blk:]
                cmaxes.append(jnp.max(st, axis=0, keepdims=True))
        return tuple(cmaxes)

    def softmax_stage(slot, cmaxes, stats):
        new = []
        for g in groups:
            for hh in range(nh):
                cs = slice(hh * blk, (hh + 1) * blk)
                m_new = cmaxes[g * nh + hh]
                if stats is not None:
                    m, l = stats[g * nh + hh]
                    m_new = jnp.maximum(m, m_new)
                ps = [jnp.exp2(st_refs[slot][g, w, :, cs] - m_new) for w in range(2)]
                p_refs[slot][g, hh, 0:blk, :] = _mx(ps[0])
                p_refs[slot][g, hh, blk:2 * blk, :] = _mx(ps[1])
                l_new = jnp.sum(ps[0], axis=0, keepdims=True) + jnp.sum(ps[1], axis=0, keepdims=True)
                if stats is None:
                    new.append((m_new, l_new, None))
                else:
                    alpha = jnp.exp2(m - m_new)
                    new.append((m_new, alpha * l + l_new, alpha))
        return new

    def value_stage(slot, ja, jb):
        return [jnp.dot(jnp.concatenate([vt_ref[g, ja, hh * d:(hh + 1) * d, :],
                                         vt_ref[g, jb, hh * d:(hh + 1) * d, :]], axis=1),
                        p_refs[slot][g, hh], preferred_element_type=F32)
                for g in groups for hh in range(nh)]

    nfar = qb - 1
    npairs = (nfar + 1) // 2

    def far_blocks(i):
        return jnp.where(2 * i < nfar, 2 * i, nb), jnp.where(2 * i + 1 < nfar, 2 * i + 1, nb)

    j_prev = jnp.where(qb > 0, qb - 1, nb)
    cmax0 = score_stage(0, qb, j_prev, True)
    cmax1 = score_stage(1, *far_blocks(0), False)
    st0 = softmax_stage(0, cmax0, None)
    for n, pv_h in enumerate(value_stage(0, qb, j_prev)):
        acc_ref[n // nh, n % nh] = pv_h
    init = (tuple((m, l) for m, l, _ in st0), cmax1)

    def trip(slot, k, carry):
        stats, cmax = carry
        cmax_next = score_stage(1 - slot, *far_blocks(k), False)
        new = softmax_stage(slot, cmax, stats)
        pv = value_stage(slot, *far_blocks(k - 1))
        for n, ((_, _, alpha), pv_h) in enumerate(zip(new, pv)):
            acc_ref[n // nh, n % nh] = acc_ref[n // nh, n % nh] * alpha + pv_h
        return tuple((m, l) for m, l, _ in new), cmax_next

    def trip_dyn(k, carry):
        return lax.cond(k % 2 == 1, functools.partial(trip, 1, k), functools.partial(trip, 0, k), carry)

    stats, _ = lax.fori_loop(1, npairs + 1, trip_dyn, init)
    for g in groups:
        out_t = jnp.concatenate([acc_ref[g, hh] / stats[g * nh + hh][1] for hh in range(nh)], axis=0)
        o_ref[0, :, g * LANES:(g + 1) * LANES] = out_t.T


MOBA_GROUPS = 4


def _moba(hp3d, bias):
    b, s, _ = hp3d.shape
    ng = MOBA_GROUPS
    wd = ng * LANES
    nsteps = A_WIDTH // wd
    hpp = LANES // A_HEAD_DIM
    nb = s // MOBA_BLOCK
    return pl.pallas_call(
        _moba_kernel,
        grid=(b, nsteps, nb),
        in_specs=[pl.BlockSpec((1, MOBA_BLOCK, wd), lambda bi, p, i: (bi, i, p)),
                  pl.BlockSpec((1, s, wd), lambda bi, p, i: (bi, 0, nsteps + p), pipeline_mode=pl.Buffered(1)),
                  pl.BlockSpec((1, s, wd), lambda bi, p, i: (bi, 0, 2 * nsteps + p), pipeline_mode=pl.Buffered(1)),
                  pl.BlockSpec((ng * hpp, 2, MOBA_BLOCK, MOBA_BLOCK), lambda bi, p, i: (p, 0, 0, 0))],
        out_specs=pl.BlockSpec((1, MOBA_BLOCK, wd), lambda bi, p, i: (bi, i, p)),
        out_shape=jax.ShapeDtypeStruct((b, s, A_WIDTH), F32),
        scratch_shapes=[pltpu.VMEM((ng, nb, LANES), F32),
                        pltpu.VMEM((ng, hpp, nb + 1, MOBA_BLOCK, LANES), MXU_DTYPE),
                        pltpu.VMEM((ng, nb + 1, LANES, MOBA_BLOCK), MXU_DTYPE),
                        pltpu.VMEM((ng, 2, MOBA_BLOCK, hpp * MOBA_BLOCK), F32),
                        pltpu.VMEM((ng, 2, MOBA_BLOCK, hpp * MOBA_BLOCK), F32),
                        pltpu.VMEM((ng, hpp, 2 * MOBA_BLOCK, MOBA_BLOCK), MXU_DTYPE),
                        pltpu.VMEM((ng, hpp, 2 * MOBA_BLOCK, MOBA_BLOCK), MXU_DTYPE),
                        pltpu.VMEM((ng, hpp, LANES, MOBA_BLOCK), MXU_DTYPE),
                        pltpu.VMEM((ng, hpp, A_HEAD_DIM, MOBA_BLOCK), F32)],
        compiler_params=_params("parallel", "parallel", "arbitrary"),
        name="moba",
    )(hp3d, hp3d, hp3d, bias)


S5_HALF_CH = S5_WIDTH // 2
S5_HALF_ST = (S5_GROUPS // 2) * S5_STATE
S5_TILE = 256
S5_NT = S5_HALF_ST // S5_TILE
S5_STEPS = 64
S5_PITCH = S5_STEPS + SUBLANES
S5_SLABS = 2 * S5_HALF_ST // LANES


def _s5_kernel(u_ref, wx_ref, wy_ref, a_ref, o_ref, st_ref, x_scr, h_scr):
    nb, steps, _ = u_ref.shape
    nseq = 2 * nb
    tile_slabs = S5_TILE // LANES

    @pl.when(pl.program_id(0) == 0)
    def _():
        st_ref[...] = jnp.zeros_like(st_ref)

    zeros = jnp.zeros((steps, LANES), MXU_DTYPE)
    for m in range(S5_NT // 2):
        pieces = []
        for seq in range(nseq):
            bi, half = divmod(seq, 2)
            c = half * S5_HALF_CH + m * LANES
            piece = _mx(u_ref[bi, :, c:c + LANES])
            pieces.append(jnp.concatenate([piece, zeros] if half == 0 else [zeros, piece], axis=1))
        lhs = jnp.concatenate(pieces, axis=0)
        for part in range(2):
            for n in (2 * m, 2 * m + 1):
                res = jnp.dot(lhs, wx_ref[part, n], preferred_element_type=F32)
                slab0 = (part * S5_HALF_ST + n * S5_TILE) // LANES
                for seq in range(nseq):
                    for sl in range(tile_slabs):
                        x_scr[slab0 + sl, seq * S5_PITCH:seq * S5_PITCH + steps, :] = (
                            res[seq * steps:(seq + 1) * steps, sl * LANES:(sl + 1) * LANES])

    nre = S5_SLABS // 2
    a_re = [a_ref[0, :, s * LANES:(s + 1) * LANES] for s in range(nre)]
    a_im = [a_ref[1, :, s * LANES:(s + 1) * LANES] for s in range(nre)]

    def step(t, carry):
        new_re, new_im = [], []
        for s in range(nre):
            hr, hi = carry[s], carry[nre + s]
            rows_t = pl.ds(t, nseq, stride=S5_PITCH)
            nhr = a_re[s] * hr - a_im[s] * hi + x_scr[s, rows_t, :]
            nhi = a_re[s] * hi + a_im[s] * hr + x_scr[nre + s, rows_t, :]
            h_scr[s, rows_t, :] = nhr
            h_scr[nre + s, rows_t, :] = nhi
            new_re.append(nhr)
            new_im.append(nhi)
        return tuple(new_re + new_im)

    init = tuple(st_ref[p, :, s * LANES:(s + 1) * LANES] for p in range(2) for s in range(nre))
    final = lax.fori_loop(0, steps, step, init, unroll=8)
    for p in range(2):
        for s in range(nre):
            st_ref[p, :, s * LANES:(s + 1) * LANES] = final[p * nre + s]

    for m in range(S5_NT // 2):
        acc = jnp.zeros((nseq * steps, 2 * LANES), F32)
        for n in (2 * m, 2 * m + 1):
            for part in range(2):
                slab0 = (part * S5_HALF_ST + n * S5_TILE) // LANES
                h_tile = jnp.concatenate(
                    [jnp.concatenate([h_scr[slab0 + sl, seq * S5_PITCH:seq * S5_PITCH + steps, :]
                                      for sl in range(tile_slabs)], axis=1) for seq in range(nseq)], axis=0)
                acc = acc + jnp.dot(_mx(h_tile), wy_ref[part, n], preferred_element_type=F32)
        for seq in range(nseq):
            bi, half = divmod(seq, 2)
            c = half * S5_HALF_CH + m * LANES
            o_ref[bi, :, c:c + LANES] = acc[seq * steps:(seq + 1) * steps, half * LANES:(half + 1) * LANES]


def _s5_weights(lam_re, lam_im, b_re, b_im, c_re, c_im, log_dt):
    lr = jnp.minimum(lam_re.astype(F32), -1e-4)
    li = lam_im.astype(F32)
    dt = jnp.exp(log_dt.astype(F32))[:, None]
    mag = jnp.exp(lr * dt)
    a_re = mag * jnp.cos(li * dt)
    a_im = mag * jnp.sin(li * dt)
    den = lr * lr + li * li
    coef_re = ((a_re - 1.0) * lr + a_im * li) / den
    coef_im = (a_im * lr - (a_re - 1.0) * li) / den
    br = b_re.astype(F32)
    bi = b_im.astype(F32)
    bb_re = coef_re[..., None] * br - coef_im[..., None] * bi
    bb_im = coef_re[..., None] * bi + coef_im[..., None] * br
    gh = S5_GROUPS // 2
    eye = jnp.eye(gh, dtype=F32)

    def b_dense(bb):
        t = bb.reshape(2, gh, S5_STATE, S5_GROUP)
        return jnp.einsum('fgph,gk->fghkp', t, eye).reshape(2, S5_HALF_CH, S5_HALF_ST)

    def c_dense(cc):
        t = cc.reshape(2, gh, S5_GROUP, S5_STATE)
        return jnp.einsum('fghp,gk->fgpkh', t, eye).reshape(2, S5_HALF_ST, S5_HALF_CH)

    def x_tiles(bd):
        tiles = []
        for n in range(S5_NT):
            c = (n // 2) * LANES
            tiles.append(jnp.concatenate([bd[0, c:c + LANES, n * S5_TILE:(n + 1) * S5_TILE],
                                          bd[1, c:c + LANES, n * S5_TILE:(n + 1) * S5_TILE]], axis=0))
        return jnp.stack(tiles)

    def y_tiles(cd):
        tiles = []
        for n in range(S5_NT):
            c = (n // 2) * LANES
            tiles.append(jnp.concatenate([cd[0, n * S5_TILE:(n + 1) * S5_TILE, c:c + LANES],
                                          cd[1, n * S5_TILE:(n + 1) * S5_TILE, c:c + LANES]], axis=1))
        return jnp.stack(tiles)

    wx = jnp.stack([x_tiles(b_dense(bb_re)), x_tiles(b_dense(bb_im))]).astype(MXU_DTYPE)
    wy = jnp.stack([y_tiles(c_dense(c_re.astype(F32))), y_tiles(c_dense(-c_im.astype(F32)))]).astype(MXU_DTYPE)
    return wx, wy, a_re.reshape(2, S5_HALF_ST), a_im.reshape(2, S5_HALF_ST)


def _s5_scan(hp3d, wx, wy, a_re2, a_im2):
    b, s, width = hp3d.shape
    nseq = 2 * b
    assert nseq == SUBLANES
    sel = jnp.arange(nseq) % 2
    a = jnp.stack([a_re2[sel], a_im2[sel]])
    scr = pltpu.VMEM((S5_SLABS, nseq * S5_PITCH, LANES), F32)
    return pl.pallas_call(
        _s5_kernel,
        grid=(s // S5_STEPS,),
        in_specs=[pl.BlockSpec((b, S5_STEPS, S5_WIDTH), lambda i: (0, i, width // S5_WIDTH - 1)),
                  _resident(wx.shape), _resident(wy.shape), _resident(a.shape)],
        out_specs=pl.BlockSpec((b, S5_STEPS, S5_WIDTH), lambda i: (0, i, 0)),
        out_shape=jax.ShapeDtypeStruct((b, s, S5_WIDTH), F32),
        scratch_shapes=[pltpu.VMEM((2, nseq, S5_HALF_ST), F32), scr, scr],
        compiler_params=_params("arbitrary"),
        name="s5_scan",
    )(hp3d, wx, wy, a)


MLP_CHUNK = 1024


def _tail_body(mix, x_ref, g1_ref, b1_ref, w1_ref, w2_ref, g2_ref, b2_ref, o_ref):
    x1 = _layer_norm(DN_ALPHA * x_ref[...] + mix, g1_ref[...], b1_ref[...])
    x1m = _mx(x1)
    acc = jnp.zeros(x1.shape, F32)
    for f in range(w1_ref.shape[1] // MLP_CHUNK):
        hid = jnp.dot(x1m, w1_ref[:, f * MLP_CHUNK:(f + 1) * MLP_CHUNK], preferred_element_type=F32)
        hid = jnp.square(jnp.maximum(hid, 0.0))
        acc = acc + jnp.dot(_mx(hid), w2_ref[f * MLP_CHUNK:(f + 1) * MLP_CHUNK, :],
                            preferred_element_type=F32)
    o_ref[...] = _layer_norm(DN_ALPHA * x1 + acc, g2_ref[...], b2_ref[...])


def _tail_even_kernel(attn_ref, y_ref, u_ref, dsk_ref, wg_ref, bg_ref, wo_ref, x_ref,
                      g1_ref, b1_ref, w1_ref, w2_ref, g2_ref, b2_ref, o_ref):
    y = y_ref[...] + dsk_ref[...] * u_ref[...]
    g = 0.5 * y * (1.0 + jnp.tanh(math.sqrt(2.0 / math.pi) * (y + 0.044715 * (y * y * y))))
    ssm = g * _sigmoid(jnp.dot(_mx(g), wg_ref[...], preferred_element_type=F32) + bg_ref[...])
    ka = attn_ref.shape[1]
    mix = (jnp.dot(_mx(attn_ref[...]), wo_ref[0:ka, :], preferred_element_type=F32)
           + jnp.dot(_mx(ssm), wo_ref[ka:, :], preferred_element_type=F32))
    _tail_body(mix, x_ref, g1_ref, b1_ref, w1_ref, w2_ref, g2_ref, b2_ref, o_ref)


def _tail_odd_kernel(a_ref, wo_ref, x_ref, g1_ref, b1_ref, w1_ref, w2_ref, g2_ref, b2_ref, o_ref):
    mix = jnp.dot(_mx(a_ref[...]), wo_ref[...], preferred_element_type=F32)
    _tail_body(mix, x_ref, g1_ref, b1_ref, w1_ref, w2_ref, g2_ref, b2_ref, o_ref)


def _row(v):
    return v.reshape(1, -1).astype(F32)


def _tail_common_specs(tm, d, dff):
    return [_resident((d, d)),
            pl.BlockSpec((tm, d), lambda i: (i, 0)),
            _resident((1, d)), _resident((1, d)),
            _resident((d, dff)), _resident((dff, d)),
            _resident((1, d)), _resident((1, d))]


def _tail_even(attn2d, y2d, hp2d, x2d, d_skip, w_glu, b_glu, w_out, g1, b1, w1, w2, g2, b2, tm):
    m, d = x2d.shape
    dff = w1.shape[1]
    ucol = hp2d.shape[1] // S5_WIDTH - 1
    return pl.pallas_call(
        _tail_even_kernel,
        grid=(m // tm,),
        in_specs=[pl.BlockSpec((tm, A_WIDTH), lambda i: (i, 0)),
                  pl.BlockSpec((tm, S5_WIDTH), lambda i: (i, 0)),
                  pl.BlockSpec((tm, S5_WIDTH), lambda i: (i, ucol)),
                  _resident((1, S5_WIDTH)), _resident((S5_WIDTH, S5_WIDTH)), _resident((1, S5_WIDTH))]
        + _tail_common_specs(tm, d, dff),
        out_specs=pl.BlockSpec((tm, d), lambda i: (i, 0)),
        out_shape=jax.ShapeDtypeStruct((m, d), F32),
        compiler_params=_params("parallel"),
        name="tail_even",
    )(attn2d, y2d, hp2d, _row(d_skip), _mx(w_glu), _row(b_glu), _mx(w_out), x2d,
      _row(g1), _row(b1), _mx(w1), _mx(w2), _row(g2), _row(b2))


def _tail_odd(a2d, x2d, w_out, g1, b1, w1, w2, g2, b2, tm):
    m, d = x2d.shape
    dff = w1.shape[1]
    return pl.pallas_call(
        _tail_odd_kernel,
        grid=(m // tm,),
        in_specs=[pl.BlockSpec((tm, a2d.shape[1]), lambda i: (i, 0))] + _tail_common_specs(tm, d, dff),
        out_specs=pl.BlockSpec((tm, d), lambda i: (i, 0)),
        out_shape=jax.ShapeDtypeStruct((m, d), F32),
        compiler_params=_params("parallel"),
        name="tail_odd",
    )(a2d, _mx(w_out), x2d, _row(g1), _row(b1), _mx(w1), _mx(w2), _row(g2), _row(b2))


GDN_TILE = 256
GDN_GATES_TILE = 512


def _softplus(t):
    return jnp.maximum(t, 0.0) + jnp.log1p(jnp.exp(-jnp.abs(t)))


def _gdn_prep_body(i, hg, q_ref, k_ref, v_ref, qp_ref, kp_ref, vp_ref, cwq_ref, cwk_ref, cwv_ref,
                   ba_ref, g_ref, gt_ref, u_ref, w_ref, qd_ref, kd_ref, intra_ref, edl_ref,
                   qx_ref, kx_ref, vx_ref, ql_ref, kl_ref, vl_ref):
    assert C_CONV == 4
    t, width = q_ref.shape[1], q_ref.shape[2]
    c = C_CHUNK
    dk = C_HEAD_DIM
    hp = width // dk
    groups = t // SUBLANES

    def conv_silu(main_ref, prev_ref, cw_ref, ext_ref, low_ref):
        x = main_ref[0]
        ext_ref[0:SUBLANES, :] = jnp.zeros((SUBLANES, width), F32)
        ext_ref[SUBLANES:2 * SUBLANES, :] = jnp.where(i > 0, prev_ref[0], 0.0)
        ext_ref[2 * SUBLANES:, :] = x
        ext_rows = t + SUBLANES

        def tap(j):
            return jnp.broadcast_to(cw_ref[j:j + 1, :], (SUBLANES, width))[None]

        def grouped(a):
            return a.reshape(a.shape[0] // SUBLANES, SUBLANES, width)

        x_ext = grouped(ext_ref[SUBLANES:, :])
        x_del = grouped(ext_ref[SUBLANES - 1:SUBLANES - 1 + ext_rows, :])
        low_ref[...] = (tap(1) * x_ext + tap(0) * x_del).reshape(ext_rows, width)
        y = (tap(3) * grouped(x) + tap(2) * x_del[1:]
             + grouped(low_ref[SUBLANES - 2:SUBLANES - 2 + t, :]))
        hy = 0.5 * y.reshape(t, width)
        return hy + hy * jnp.tanh(hy)

    def l2n(z, scale):
        return z * (lax.rsqrt(jnp.sum(z * z, axis=-1, keepdims=True) + L2_EPS) * scale)

    q_all = conv_silu(q_ref, qp_ref, cwq_ref, qx_ref, ql_ref)
    k_all = conv_silu(k_ref, kp_ref, cwk_ref, kx_ref, kl_ref)
    v_all = conv_silu(v_ref, vp_ref, cwv_ref, vx_ref, vl_ref)
    heads = [slice(hh * dk, (hh + 1) * dk) for hh in range(hp)]
    qs = [l2n(q_all[:, cs], dk ** -0.5) for cs in heads]
    ks = [l2n(k_all[:, cs], 1.0) for cs in heads]
    vs = [v_all[:, cs] for cs in heads]

    ba = ba_ref[0]
    lane = lax.broadcasted_iota(jnp.int32, (t, LANES), 1)
    betas, g_cols, g_rows = [], [], []
    for hh in range(hp):
        h = hg * hp + hh
        betas.append(_sigmoid(jnp.sum(jnp.where(lane == h, ba, 0.0), axis=1, keepdims=True)))
        g_cols.append(jnp.sum(jnp.where(lane == C_HEADS + h, g_ref[0], 0.0), axis=1, keepdims=True))
        g_rows.append(gt_ref[0, pl.ds(C_HEADS + h, 1), :])

    ii = lax.broadcasted_iota(jnp.int32, (c, c), 0)
    jj = lax.broadcasted_iota(jnp.int32, (c, c), 1)
    low = ii >= jj
    units = [(hh, slice(ch * c, (ch + 1) * c)) for hh in range(hp) for ch in range(t // c)]
    gc_cols = [jnp.sum(jnp.where(low, g_rows[hh][:, r], 0.0), axis=1, keepdims=True) for hh, r in units]
    gc_rows = [jnp.sum(jnp.where(ii <= jj, g_cols[hh][r], 0.0), axis=0, keepdims=True) for hh, r in units]
    decays = [jnp.where(low, jnp.exp(jnp.where(low, a - b, 0.0)), 0.0) for a, b in zip(gc_cols, gc_rows)]
    kbs = [ks[hh][r] * betas[hh][r] for hh, r in units]
    kks = [_dot_nt(kb, ks[hh][r]) for kb, (hh, r) in zip(kbs, units)]
    qks = [_dot_nt(qs[hh][r], ks[hh][r]) for hh, r in units]
    pws = [jnp.where(ii > jj, -(kk * dec), 0.0) for kk, dec in zip(kks, decays)]
    tms = pws
    for _ in range(int(math.log2(c)) - 1):
        pbs = [_mx(pw) for pw in pws]
        pws = [_dot(pb, pb) for pb in pbs]
        pbs = [_mx(pw) for pw in pws]
        tms = [tm + pw + _dot(tm, pb) for tm, pw, pb in zip(tms, pws, pbs)]
    egcs = [jnp.exp(a) for a in gc_cols]
    vbs = [vs[hh][r] * betas[hh][r] for hh, r in units]
    kbes = [kb * e for kb, e in zip(kbs, egcs)]
    us = [vb + _dot(tm, vb) for tm, vb in zip(tms, vbs)]
    ws = [kbe + _dot(tm, kbe) for tm, kbe in zip(tms, kbes)]
    for n, (hh, r) in enumerate(units):
        gl = gc_cols[n][c - 1:c, :]
        cs = heads[hh]
        ch = r.start // c
        u_ref[0, r, cs] = us[n]
        w_ref[0, r, cs] = _mx(ws[n])
        qd_ref[0, r, cs] = _mx(qs[hh][r] * egcs[n])
        kd_ref[0, r, cs] = _mx(ks[hh][r] * jnp.exp(gl - gc_cols[n]))
        intra_ref[0, hh, r, :] = _mx(qks[n] * decays[n])
        edl_ref[0, hh, ch:ch + 1, :] = jnp.broadcast_to(jnp.exp(gl), (1, LANES))


def _gdn_gates_kernel(ba_ref, alog_ref, dtb_ref, g_ref, gt_ref):
    g_all = -jnp.exp(alog_ref[...]) * _softplus(ba_ref[0] + dtb_ref[...])
    g_ref[0] = g_all
    gt_ref[0] = g_all.T


def _gdn_gates(hp3d, alog_row, dtb_row):
    b, s, _ = hp3d.shape
    t = GDN_GATES_TILE
    ba_col = 4 * C_WIDTH // LANES
    return pl.pallas_call(
        _gdn_gates_kernel,
        grid=(b, s // t),
        in_specs=[pl.BlockSpec((1, t, LANES), lambda bi, i: (bi, i, ba_col)),
                  pl.BlockSpec((1, LANES), lambda bi, i: (0, 0)),
                  pl.BlockSpec((1, LANES), lambda bi, i: (0, 0))],
        out_specs=[pl.BlockSpec((1, t, LANES), lambda bi, i: (bi, i, 0)),
                   pl.BlockSpec((1, LANES, t), lambda bi, i: (bi, 0, i))],
        out_shape=[jax.ShapeDtypeStruct((b, s, LANES), F32), jax.ShapeDtypeStruct((b, LANES, s), F32)],
        compiler_params=_params("parallel", "parallel"),
        name="gdn_gates",
    )(hp3d, alog_row, dtb_row)


def _gdn_scan_body(chunk0, u_ref, w_ref, qd_ref, kd_ref, intra_ref, edl_ref, gate_ref, nw_ref, o_ref, st_ref):
    c = C_CHUNK
    dk = C_HEAD_DIM
    t = u_ref.shape[1]
    nw = nw_ref[...]
    heads = [slice(h * dk, (h + 1) * dk) for h in range(C_HEADS)]
    for ch in range(t // c):
        r = slice(ch * c, (ch + 1) * c)
        sts = [st_ref[h] for h in range(C_HEADS)]
        wss = [_dot(jnp.concatenate([w_ref[0, r, cs], qd_ref[0, r, cs]], axis=0), st)
               for cs, st in zip(heads, sts)]
        v_news = [u_ref[0, r, cs] - ws[:c] for cs, ws in zip(heads, wss)]
        kvs = [_dot_tn(kd_ref[0, r, cs], v_new) for cs, v_new in zip(heads, v_news)]
        os_ = [ws[c:] + _dot(intra_ref[0, h, r, :], v_new) for h, (ws, v_new) in enumerate(zip(wss, v_news))]
        for h, cs in enumerate(heads):
            st_ref[h] = sts[h] * edl_ref[0, h, pl.ds(chunk0 + ch, 1), 0:1] + kvs[h]
            o = os_[h]
            o = o * lax.rsqrt(jnp.mean(o * o, axis=-1, keepdims=True) + RMS_EPS) * nw
            gt = gate_ref[0, r, cs]
            o_ref[0, r, cs] = o * (gt * _sigmoid(gt))


def _gdn_kernel(q_ref, k_ref, v_ref, qp_ref, kp_ref, vp_ref, cwq_ref, cwk_ref, cwv_ref, ba_ref, g_ref, gt_ref,
                gate_ref, nw_ref, o_ref, st_ref, qx_ref, kx_ref, vx_ref, ql_ref, kl_ref, vl_ref, *bufs):
    j = pl.program_id(1)
    last = pl.num_programs(1) - 2
    cur, nxt = bufs[:len(bufs) // 2], bufs[len(bufs) // 2:]

    @pl.when(j == 0)
    def _():
        st_ref[...] = jnp.zeros_like(st_ref)
        for ref in bufs:
            ref[...] = jnp.zeros_like(ref)

    _gdn_prep_body(jnp.minimum(j, last), 0, q_ref, k_ref, v_ref, qp_ref, kp_ref, vp_ref, cwq_ref, cwk_ref, cwv_ref,
                   ba_ref, g_ref, gt_ref, *nxt, qx_ref, kx_ref, vx_ref, ql_ref, kl_ref, vl_ref)
    _gdn_scan_body(0, *cur, gate_ref, nw_ref, o_ref, st_ref)
    for dst, src in zip(cur, nxt):
        dst[...] = src[...]


def _gdn(hp3d, conv_w, alog_row, dtb_row, norm_w):
    b, s, _ = hp3d.shape
    ba_col = 4 * C_WIDTH // LANES
    t = GDN_TILE
    nt = s // t
    per = t // SUBLANES
    wd = C_WIDTH
    nc = t // C_CHUNK
    g, gt = _gdn_gates(hp3d, alog_row, dtb_row)

    def prep_tile(j):
        return jnp.minimum(j, nt - 1)

    def scan_tile(j):
        return jnp.maximum(j - 1, 0)

    def main(off):
        return pl.BlockSpec((1, t, wd), lambda bi, j: (bi, prep_tile(j), off))

    def prev(off):
        return pl.BlockSpec((1, SUBLANES, wd), lambda bi, j: (bi, jnp.maximum(prep_tile(j) * per - 1, 0), off))

    def cw(off):
        return pl.BlockSpec((C_CONV, wd), lambda bi, j: (0, off))

    bufs = [pltpu.VMEM((1, t, wd), F32)] + [pltpu.VMEM((1, t, wd), MXU_DTYPE)] * 3 + [
        pltpu.VMEM((1, C_HEADS, t, C_CHUNK), MXU_DTYPE), pltpu.VMEM((1, C_HEADS, max(nc, SUBLANES), LANES), F32)]
    return pl.pallas_call(
        _gdn_kernel,
        grid=(b, nt + 1),
        in_specs=[main(0), main(1), main(2), prev(0), prev(1), prev(2), cw(0), cw(1), cw(2),
                  pl.BlockSpec((1, t, LANES), lambda bi, j: (bi, prep_tile(j), ba_col)),
                  pl.BlockSpec((1, t, LANES), lambda bi, j: (bi, prep_tile(j), 0)),
                  pl.BlockSpec((1, LANES, t), lambda bi, j: (bi, 0, prep_tile(j))),
                  pl.BlockSpec((1, t, wd), lambda bi, j: (bi, scan_tile(j), 3)),
                  pl.BlockSpec((1, C_HEAD_DIM), lambda bi, j: (0, 0))],
        out_specs=pl.BlockSpec((1, t, wd), lambda bi, j: (bi, scan_tile(j), 0)),
        out_shape=jax.ShapeDtypeStruct((b, s, wd), F32),
        scratch_shapes=[pltpu.VMEM((C_HEADS, C_HEAD_DIM, C_HEAD_DIM), F32)]
        + [pltpu.VMEM((t + 2 * SUBLANES, wd), F32)] * 3 + [pltpu.VMEM((t + SUBLANES, wd), F32)] * 3
        + bufs + bufs,
        compiler_params=_params("parallel", "arbitrary"),
        name="gdn",
    )(hp3d, hp3d, hp3d, hp3d, hp3d, hp3d, conv_w, conv_w, conv_w, hp3d, g, gt, hp3d, _row(norm_w))


ROW_TILE = 512


def kernel(x, rel_table, ev_w_in, ev_w_out, s5_lambda_re, s5_lambda_im, s5_b_re, s5_b_im, s5_c_re, s5_c_im, s5_d, s5_log_dt, s5_w_glu, s5_b_glu, od_w_in, od_conv_w, od_a_log, od_dt_bias, od_norm_w, od_w_out, ln_mix_g, ln_mix_b, mlp_w1, mlp_w2, ln_ffn_g, ln_ffn_b):
    bsz, s, d = x.shape
    m = bsz * s
    x2d = x.reshape(m, d)
    table_t = rel_table.T.astype(F32)
    bias = _bias_tiles(table_t)
    for layer in range(DEPTH):
        i = layer // 2
        if layer % 2 == 0:
            hp = _proj(x2d, _mx(ev_w_in[i]), ROW_TILE)
            hp3d = hp.reshape(bsz, s, hp.shape[1])
            attn = _moba(hp3d, bias)
            wx, wy, a_re2, a_im2 = _s5_weights(s5_lambda_re[i], s5_lambda_im[i], s5_b_re[i], s5_b_im[i],
                                               s5_c_re[i], s5_c_im[i], s5_log_dt[i])
            y = _s5_scan(hp3d, wx, wy, a_re2, a_im2)
            x2d = _tail_even(attn.reshape(m, A_WIDTH), y.reshape(m, S5_WIDTH), hp, x2d,
                             s5_d[i], s5_w_glu[i], s5_b_glu[i], ev_w_out[i],
                             ln_mix_g[layer], ln_mix_b[layer], mlp_w1[layer], mlp_w2[layer],
                             ln_ffn_g[layer], ln_ffn_b[layer], ROW_TILE)
        else:
            w_in = od_w_in[i]
            w_all = jnp.pad(w_in, ((0, 0), (0, LANES - 2 * C_HEADS)))
            hp = _proj(x2d, _mx(w_all), ROW_TILE)
            hp3d = hp.reshape(bsz, s, 4 * C_WIDTH + LANES)
            pad8 = jnp.zeros((C_HEADS,), F32)
            padr = jnp.zeros((LANES - 2 * C_HEADS,), F32)
            alog_row = jnp.concatenate([pad8, od_a_log[i].astype(F32), padr]).reshape(1, LANES)
            dtb_row = jnp.concatenate([pad8, od_dt_bias[i].astype(F32), padr]).reshape(1, LANES)
            o = _gdn(hp3d, od_conv_w[i].astype(F32), alog_row, dtb_row, od_norm_w[i])
            x2d = _tail_odd(o.reshape(m, C_WIDTH), x2d, od_w_out[i],
                            ln_mix_g[layer], ln_mix_b[layer], mlp_w1[layer], mlp_w2[layer],
                            ln_ffn_g[layer], ln_ffn_b[layer], ROW_TILE)
    return x2d.reshape(bsz, s, d)
```

```python
import functools
import math

import jax
import jax.numpy as jnp
import numpy as np
from jax import lax
from jax.experimental import pallas as pl
from jax.experimental.pallas import tpu as pltpu

F32 = jnp.float32
MXU_DTYPE = jnp.bfloat16

LANES = 128
SUBLANES = 8
VMEM_LIMIT_BYTES = 56 * 1024 * 1024

A_HEADS = 8
A_HEAD_DIM = 64
A_WIDTH = A_HEADS * A_HEAD_DIM
MOBA_BLOCK = 256
MOBA_TOPK = 3
REL_BUCKETS = 32
REL_MAX_DIST = 128
S5_GROUP = 16
S5_GROUPS = 32
S5_STATE = 64
S5_WIDTH = S5_GROUP * S5_GROUPS
C_HEADS = 8
C_HEAD_DIM = 128
C_WIDTH = C_HEADS * C_HEAD_DIM
C_CONV = 4
C_CHUNK = 64
DEPTH = 2
DN_ALPHA = (2 * DEPTH) ** 0.25
LN_EPS = 1e-5
RMS_EPS = 1e-6
L2_EPS = 1e-6
NEG_INF = -1e30
LOG2E = math.log2(math.e)


def _mx(t):
    return t.astype(MXU_DTYPE)


def _dot(a, b):
    return jnp.dot(_mx(a), _mx(b), preferred_element_type=F32)


def _dot_tn(a, b):
    return lax.dot_general(_mx(a), _mx(b), (((0,), (0,)), ((), ())), preferred_element_type=F32)


def _dot_nt(a, b):
    return lax.dot_general(_mx(a), _mx(b), (((1,), (1,)), ((), ())), preferred_element_type=F32)


def _sigmoid(t):
    return 1.0 / (1.0 + jnp.exp(-t))


def _layer_norm(r, g, b):
    mu = jnp.mean(r, axis=-1, keepdims=True)
    d = r - mu
    var = jnp.mean(d * d, axis=-1, keepdims=True)
    return d * lax.rsqrt(var + LN_EPS) * g + b


def _params(*sem):
    return pltpu.CompilerParams(dimension_semantics=sem, vmem_limit_bytes=VMEM_LIMIT_BYTES)


def _resident(shape):
    return pl.BlockSpec(shape, lambda *_: (0,) * len(shape), pipeline_mode=pl.Buffered(1))


PROJ_CHUNK = 1024


def _proj_kernel(x_ref, w_ref, o_ref):
    xm = _mx(x_ref[...])
    n = w_ref.shape[1]
    for c0 in range(0, n, PROJ_CHUNK):
        c1 = min(c0 + PROJ_CHUNK, n)
        o_ref[:, c0:c1] = jnp.dot(xm, w_ref[:, c0:c1], preferred_element_type=F32)


def _proj(x2d, w, tm):
    m, k = x2d.shape
    n = w.shape[1]
    return pl.pallas_call(
        _proj_kernel,
        grid=(m // tm,),
        in_specs=[pl.BlockSpec((tm, k), lambda i: (i, 0)), _resident((k, n))],
        out_specs=pl.BlockSpec((tm, n), lambda i: (i, 0)),
        out_shape=jax.ShapeDtypeStruct((m, n), F32),
        compiler_params=_params("parallel"),
        name="in_proj",
    )(x2d, w)


def _t5_bucket_np(dist):
    max_exact = REL_BUCKETS // 2
    n = np.maximum(dist, 0)
    nf = np.maximum(n, 1).astype(np.float32)
    large = max_exact + (np.log(nf / np.float32(max_exact)) / np.float32(math.log(REL_MAX_DIST / max_exact))
                         * np.float32(REL_BUCKETS - max_exact)).astype(np.int32)
    large = np.minimum(large, REL_BUCKETS - 1)
    return np.where(n < max_exact, n, large).astype(np.int32)


def _bias_kernel(tab_ref, bkt_ref, o_ref):
    h = pl.program_id(0)
    blk = bkt_ref.shape[-1]
    key = lax.broadcasted_iota(jnp.int32, (blk, blk), 0)
    qry = lax.broadcasted_iota(jnp.int32, (blk, blk), 1)
    far = tab_ref[h, REL_BUCKETS - 1]
    for t in range(2):
        bk = bkt_ref[t]
        acc = jnp.zeros((blk, blk), F32)
        for i in range(REL_BUCKETS):
            acc = jnp.where(bk == i, (tab_ref[h, i] - far) * LOG2E, acc)
        if t == 0:
            acc = jnp.where(qry >= key, acc, NEG_INF)
        o_ref[0, t] = acc


def _bias_tiles(table_t):
    heads = table_t.shape[0]
    offs = np.arange(MOBA_BLOCK)
    dist_own = offs[None, :] - offs[:, None]
    buckets = np.stack([_t5_bucket_np(dist_own), _t5_bucket_np(dist_own + MOBA_BLOCK)])
    return pl.pallas_call(
        _bias_kernel,
        grid=(heads,),
        in_specs=[pl.BlockSpec(memory_space=pltpu.SMEM),
                  pl.BlockSpec((2, MOBA_BLOCK, MOBA_BLOCK), lambda h: (0, 0, 0))],
        out_specs=pl.BlockSpec((1, 2, MOBA_BLOCK, MOBA_BLOCK), lambda h: (h, 0, 0, 0)),
        out_shape=jax.ShapeDtypeStruct((heads, 2, MOBA_BLOCK, MOBA_BLOCK), F32),
        compiler_params=_params("parallel"),
        name="t5_bias",
    )(table_t, jnp.asarray(buckets))


def _moba_kernel(q_ref, k_ref, v_ref, bias_ref, o_ref, kmean_ref, kb_ref, vt_ref,
                 st0_ref, st1_ref, p0_ref, p1_ref, wq_ref, acc_ref):
    st_refs = (st0_ref, st1_ref)
    p_refs = (p0_ref, p1_ref)
    qb = pl.program_id(2)
    blk = MOBA_BLOCK
    d = A_HEAD_DIM
    nh = LANES // d
    nb = k_ref.shape[1] // blk
    ng = q_ref.shape[2] // LANES
    groups = range(ng)

    @pl.when(qb == 0)
    def _():
        lane = lax.broadcasted_iota(jnp.int32, (blk, LANES - d), 1)
        for g in groups:
            gl = slice(g * LANES, (g + 1) * LANES)
            for j in range(nb + 1):
                onehot = jnp.where(lane == j, 1.0, 0.0).astype(MXU_DTYPE)
                for hh in range(nh):
                    kb_ref[g, hh, j, :, d:LANES] = onehot
            for j in range(nb):
                kj = k_ref[0, j * blk:(j + 1) * blk, gl]
                for hh in range(nh):
                    kb_ref[g, hh, j, :, 0:d] = _mx(kj[:, hh * d:(hh + 1) * d])
                kmean_ref[g, j:j + 1, :] = jnp.mean(kj, axis=0, keepdims=True)
                vt_ref[g, j] = _mx(v_ref[0, j * blk:(j + 1) * blk, gl].T)
            for hh in range(nh):
                kb_ref[g, hh, nb, :, 0:d] = jnp.zeros((blk, d), MXU_DTYPE)
            vt_ref[g, nb] = jnp.zeros((LANES, blk), MXU_DTYPE)

    blk_id = lax.broadcasted_iota(jnp.int32, (nb, blk), 0)
    row = lax.broadcasted_iota(jnp.int32, (LANES - d - nb, blk), 0)
    for g in groups:
        qt = (q_ref[0, :, g * LANES:(g + 1) * LANES] * (d ** -0.5)).T
        for hh in range(nh):
            gate = lax.dot_general(kmean_ref[g, :, hh * d:(hh + 1) * d], qt[hh * d:(hh + 1) * d],
                                   (((1,), (0,)), ((), ())),
                                   precision=lax.Precision.HIGHEST, preferred_element_type=F32)
            gt = jnp.where(blk_id < qb, gate, -jnp.inf)
            sel = jnp.zeros((nb, blk), F32)
            for _ in range(MOBA_TOPK):
                m = jnp.max(gt, axis=0, keepdims=True)
                is_m = jnp.logical_and(gt == m, gt > -jnp.inf)
                idx = jnp.min(jnp.where(is_m, blk_id, nb), axis=0, keepdims=True)
                pick = blk_id == idx
                sel = jnp.where(pick, 1.0, sel)
                gt = jnp.where(pick, -jnp.inf, gt)
            keep = jnp.logical_or(sel > 0.0, blk_id == qb)
            wq_ref[g, hh] = _mx(jnp.concatenate([qt[hh * d:(hh + 1) * d] * LOG2E, jnp.where(keep, 0.0, NEG_INF),
                                                 jnp.where(row == 0, NEG_INF, 0.0)], axis=0))

    def score_stage(slot, ja, jb, with_bias):
        cmaxes = []
        for g in groups:
            for hh in range(nh):
                keys = jnp.concatenate([kb_ref[g, hh, ja], kb_ref[g, hh, jb]], axis=0)
                st = jnp.dot(keys, wq_ref[g, hh], preferred_element_type=F32)
                if with_bias:
                    st = st + jnp.concatenate([bias_ref[g * nh + hh, w] for w in range(2)], axis=0)
                st_refs[slot][g, 0, :, hh * blk:(hh + 1) * blk] = st[:blk]
                st_refs[slot][g, 1, :, hh * blk:(hh + 1) * blk] = st[blk:]
                cmaxes.append(jnp.max(st, axis=0, keepdims=True))
        return tuple(cmaxes)

    def softmax_stage(slot, cmaxes, stats):
        new = []
        for g in groups:
            for hh in range(nh):
                cs = slice(hh * blk, (hh + 1) * blk)
                m_new = cmaxes[g * nh + hh]
                if stats is not None:
                    m, l = stats[g * nh + hh]
                    m_new = jnp.maximum(m, m_new)
                ps = [jnp.exp2(st_refs[slot][g, w, :, cs] - m_new) for w in range(2)]
                p_refs[slot][g, hh, 0:blk, :] = _mx(ps[0])
                p_refs[slot][g, hh, blk:2 * blk, :] = _mx(ps[1])
                l_new = jnp.sum(ps[0], axis=0, keepdims=True) + jnp.sum(ps[1], axis=0, keepdims=True)
                if stats is None:
                    new.append((m_new, l_new, None))
                else:
                    alpha = jnp.exp2(m - m_new)
                    new.append((m_new, alpha * l + l_new, alpha))
        return new

    def value_stage(slot, ja, jb):
        return [jnp.dot(jnp.concatenate([vt_ref[g, ja, hh * d:(hh + 1) * d, :],
                                         vt_ref[g, jb, hh * d:(hh + 1) * d, :]], axis=1),
                        p_refs[slot][g, hh], preferred_element_type=F32)
                for g in groups for hh in range(nh)]

    nfar = qb - 1
    npairs = (nfar + 1) // 2

    def far_blocks(i):
        return jnp.where(2 * i < nfar, 2 * i, nb), jnp.where(2 * i + 1 < nfar, 2 * i + 1, nb)

    j_prev = jnp.where(qb > 0, qb - 1, nb)
    cmax0 = score_stage(0, qb, j_prev, True)
    cmax1 = score_stage(1, *far_blocks(0), False)
    st0 = softmax_stage(0, cmax0, None)
    for n, pv_h in enumerate(value_stage(0, qb, j_prev)):
        acc_ref[n // nh, n % nh] = pv_h
    init = (tuple((m, l) for m, l, _ in st0), cmax1)

    def trip(slot, k, carry):
        stats, cmax = carry
        cmax_next = score_stage(1 - slot, *far_blocks(k), False)
        new = softmax_stage(slot, cmax, stats)
        pv = value_stage(slot, *far_blocks(k - 1))
        for n, ((_, _, alpha), pv_h) in enumerate(zip(new, pv)):
            acc_ref[n // nh, n % nh] = acc_ref[n // nh, n % nh] * alpha + pv_h
        return tuple((m, l) for m, l, _ in new), cmax_next

    def trip_dyn(k, carry):
        return lax.cond(k % 2 == 1, functools.partial(trip, 1, k), functools.partial(trip, 0, k), carry)

    stats, _ = lax.fori_loop(1, npairs + 1, trip_dyn, init)
    for g in groups:
        out_t = jnp.concatenate([acc_ref[g, hh] / stats[g * nh + hh][1] for hh in range(nh)], axis=0)
        o_ref[0, :, g * LANES:(g + 1) * LANES] = out_t.T


MOBA_GROUPS = 4


def _moba(hp3d, bias):
    b, s, _ = hp3d.shape
    ng = MOBA_GROUPS
    wd = ng * LANES
    nsteps = A_WIDTH // wd
    hpp = LANES // A_HEAD_DIM
    nb = s // MOBA_BLOCK
    return pl.pallas_call(
        _moba_kernel,
        grid=(b, nsteps, nb),
        in_specs=[pl.BlockSpec((1, MOBA_BLOCK, wd), lambda bi, p, i: (bi, i, p)),
                  pl.BlockSpec((1, s, wd), lambda bi, p, i: (bi, 0, nsteps + p), pipeline_mode=pl.Buffered(1)),
                  pl.BlockSpec((1, s, wd), lambda bi, p, i: (bi, 0, 2 * nsteps + p), pipeline_mode=pl.Buffered(1)),
                  pl.BlockSpec((ng * hpp, 2, MOBA_BLOCK, MOBA_BLOCK), lambda bi, p, i: (p, 0, 0, 0))],
        out_specs=pl.BlockSpec((1, MOBA_BLOCK, wd), lambda bi, p, i: (bi, i, p)),
        out_shape=jax.ShapeDtypeStruct((b, s, A_WIDTH), F32),
        scratch_shapes=[pltpu.VMEM((ng, nb, LANES), F32),
                        pltpu.VMEM((ng, hpp, nb + 1, MOBA_BLOCK, LANES), MXU_DTYPE),
                        pltpu.VMEM((ng, nb + 1, LANES, MOBA_BLOCK), MXU_DTYPE),
                        pltpu.VMEM((ng, 2, MOBA_BLOCK, hpp * MOBA_BLOCK), F32),
                        pltpu.VMEM((ng, 2, MOBA_BLOCK, hpp * MOBA_BLOCK), F32),
                        pltpu.VMEM((ng, hpp, 2 * MOBA_BLOCK, MOBA_BLOCK), MXU_DTYPE),
                        pltpu.VMEM((ng, hpp, 2 * MOBA_BLOCK, MOBA_BLOCK), MXU_DTYPE),
                        pltpu.VMEM((ng, hpp, LANES, MOBA_BLOCK), MXU_DTYPE),
                        pltpu.VMEM((ng, hpp, A_HEAD_DIM, MOBA_BLOCK), F32)],
        compiler_params=_params("parallel", "parallel", "arbitrary"),
        name="moba",
    )(hp3d, hp3d, hp3d, bias)


S5_HALF_CH = S5_WIDTH // 2
S5_HALF_ST = (S5_GROUPS // 2) * S5_STATE
S5_TILE = 256
S5_NT = S5_HALF_ST // S5_TILE
S5_STEPS = 64
S5_PITCH = S5_STEPS + SUBLANES
S5_SLABS = 2 * S5_HALF_ST // LANES


def _s5_kernel(u_ref, wx_ref, wy_ref, a_ref, o_ref, st_ref, x_scr, h_scr):
    nb, steps, _ = u_ref.shape
    nseq = 2 * nb
    tile_slabs = S5_TILE // LANES

    @pl.when(pl.program_id(0) == 0)
    def _():
        st_ref[...] = jnp.zeros_like(st_ref)

    for m in range(S5_NT // 2):
        lhs = [jnp.concatenate([_mx(u_ref[bi, :, half * S5_HALF_CH + m * LANES:half * S5_HALF_CH + (m + 1) * LANES])
                                for bi in range(nb)], axis=0) for half in range(2)]
        for part in range(2):
            for n in (2 * m, 2 * m + 1):
                slab0 = (part * S5_HALF_ST + n * S5_TILE) // LANES
                for half in range(2):
                    res = jnp.dot(lhs[half], wx_ref[part, n, half * LANES:(half + 1) * LANES, :],
                                  preferred_element_type=F32)
                    for bi in range(nb):
                        seq = 2 * bi + half
                        for sl in range(tile_slabs):
                            x_scr[slab0 + sl, seq * S5_PITCH:seq * S5_PITCH + steps, :] = (
                                res[bi * steps:(bi + 1) * steps, sl * LANES:(sl + 1) * LANES])

    nre = S5_SLABS // 2
    a_re = [a_ref[0, :, s * LANES:(s + 1) * LANES] for s in range(nre)]
    a_im = [a_ref[1, :, s * LANES:(s + 1) * LANES] for s in range(nre)]

    def step(t, carry):
        new_re, new_im = [], []
        for s in range(nre):
            hr, hi = carry[s], carry[nre + s]
            rows_t = pl.ds(t, nseq, stride=S5_PITCH)
            nhr = a_re[s] * hr - a_im[s] * hi + x_scr[s, rows_t, :]
            nhi = a_re[s] * hi + a_im[s] * hr + x_scr[nre + s, rows_t, :]
            h_scr[s, rows_t, :] = nhr
            h_scr[nre + s, rows_t, :] = nhi
            new_re.append(nhr)
            new_im.append(nhi)
        return tuple(new_re + new_im)

    init = tuple(st_ref[p, :, s * LANES:(s + 1) * LANES] for p in range(2) for s in range(nre))
    final = lax.fori_loop(0, steps, step, init, unroll=8)
    for p in range(2):
        for s in range(nre):
            st_ref[p, :, s * LANES:(s + 1) * LANES] = final[p * nre + s]

    for m in range(S5_NT // 2):
        for half in range(2):
            acc = jnp.zeros((nb * steps, LANES), F32)
            for n in (2 * m, 2 * m + 1):
                for part in range(2):
                    slab0 = (part * S5_HALF_ST + n * S5_TILE) // LANES
                    h_tile = jnp.concatenate(
                        [jnp.concatenate([h_scr[slab0 + sl, (2 * bi + half) * S5_PITCH:(2 * bi + half) * S5_PITCH + steps, :]
                                          for sl in range(tile_slabs)], axis=1) for bi in range(nb)], axis=0)
                    acc = acc + jnp.dot(_mx(h_tile), wy_ref[part, n, :, half * LANES:(half + 1) * LANES],
                                        preferred_element_type=F32)
            c = half * S5_HALF_CH + m * LANES
            for bi in range(nb):
                o_ref[bi, :, c:c + LANES] = acc[bi * steps:(bi + 1) * steps, :]


def _s5_weights(lam_re, lam_im, b_re, b_im, c_re, c_im, log_dt):
    lr = jnp.minimum(lam_re.astype(F32), -1e-4)
    li = lam_im.astype(F32)
    dt = jnp.exp(log_dt.astype(F32))[:, None]
    mag = jnp.exp(lr * dt)
    a_re = mag * jnp.cos(li * dt)
    a_im = mag * jnp.sin(li * dt)
    den = lr * lr + li * li
    coef_re = ((a_re - 1.0) * lr + a_im * li) / den
    coef_im = (a_im * lr - (a_re - 1.0) * li) / den
    br = b_re.astype(F32)
    bi = b_im.astype(F32)
    bb_re = coef_re[..., None] * br - coef_im[..., None] * bi
    bb_im = coef_re[..., None] * bi + coef_im[..., None] * br
    gh = S5_GROUPS // 2
    eye = jnp.eye(gh, dtype=F32)

    def b_dense(bb):
        t = bb.reshape(2, gh, S5_STATE, S5_GROUP)
        return jnp.einsum('fgph,gk->fghkp', t, eye).reshape(2, S5_HALF_CH, S5_HALF_ST)

    def c_dense(cc):
        t = cc.reshape(2, gh, S5_GROUP, S5_STATE)
        return jnp.einsum('fghp,gk->fgpkh', t, eye).reshape(2, S5_HALF_ST, S5_HALF_CH)

    def x_tiles(bd):
        tiles = []
        for n in range(S5_NT):
            c = (n // 2) * LANES
            tiles.append(jnp.concatenate([bd[0, c:c + LANES, n * S5_TILE:(n + 1) * S5_TILE],
                                          bd[1, c:c + LANES, n * S5_TILE:(n + 1) * S5_TILE]], axis=0))
        return jnp.stack(tiles)

    def y_tiles(cd):
        tiles = []
        for n in range(S5_NT):
            c = (n // 2) * LANES
            tiles.append(jnp.concatenate([cd[0, n * S5_TILE:(n + 1) * S5_TILE, c:c + LANES],
                                          cd[1, n * S5_TILE:(n + 1) * S5_TILE, c:c + LANES]], axis=1))
        return jnp.stack(tiles)

    wx = jnp.stack([x_tiles(b_dense(bb_re)), x_tiles(b_dense(bb_im))]).astype(MXU_DTYPE)
    wy = jnp.stack([y_tiles(c_dense(c_re.astype(F32))), y_tiles(c_dense(-c_im.astype(F32)))]).astype(MXU_DTYPE)
    return wx, wy, a_re.reshape(2, S5_HALF_ST), a_im.reshape(2, S5_HALF_ST)


def _s5_scan(hp3d, wx, wy, a_re2, a_im2):
    b, s, width = hp3d.shape
    nseq = 2 * b
    assert nseq == SUBLANES
    sel = jnp.arange(nseq) % 2
    a = jnp.stack([a_re2[sel], a_im2[sel]])
    scr = pltpu.VMEM((S5_SLABS, nseq * S5_PITCH, LANES), F32)
    return pl.pallas_call(
        _s5_kernel,
        grid=(s // S5_STEPS,),
        in_specs=[pl.BlockSpec((b, S5_STEPS, S5_WIDTH), lambda i: (0, i, width // S5_WIDTH - 1)),
                  _resident(wx.shape), _resident(wy.shape), _resident(a.shape)],
        out_specs=pl.BlockSpec((b, S5_STEPS, S5_WIDTH), lambda i: (0, i, 0)),
        out_shape=jax.ShapeDtypeStruct((b, s, S5_WIDTH), F32),
        scratch_shapes=[pltpu.VMEM((2, nseq, S5_HALF_ST), F32), scr, scr],
        compiler_params=_params("arbitrary"),
        name="s5_scan",
    )(hp3d, wx, wy, a)


MLP_CHUNK = 1024


def _tail_body(mix, x_ref, g1_ref, b1_ref, w1_ref, w2_ref, g2_ref, b2_ref, o_ref):
    x1 = _layer_norm(DN_ALPHA * x_ref[...] + mix, g1_ref[...], b1_ref[...])
    x1m = _mx(x1)
    acc = jnp.zeros(x1.shape, F32)
    for f in range(w1_ref.shape[1] // MLP_CHUNK):
        hid = jnp.dot(x1m, w1_ref[:, f * MLP_CHUNK:(f + 1) * MLP_CHUNK], preferred_element_type=F32)
        hid = jnp.square(jnp.maximum(hid, 0.0))
        acc = acc + jnp.dot(_mx(hid), w2_ref[f * MLP_CHUNK:(f + 1) * MLP_CHUNK, :],
                            preferred_element_type=F32)
    o_ref[...] = _layer_norm(DN_ALPHA * x1 + acc, g2_ref[...], b2_ref[...])


def _tail_even_kernel(attn_ref, y_ref, u_ref, dsk_ref, wg_ref, bg_ref, wo_ref, x_ref,
                      g1_ref, b1_ref, w1_ref, w2_ref, g2_ref, b2_ref, o_ref):
    y = y_ref[...] + dsk_ref[...] * u_ref[...]
    g = 0.5 * y * (1.0 + jnp.tanh(math.sqrt(2.0 / math.pi) * (y + 0.044715 * (y * y * y))))
    ssm = g * _sigmoid(jnp.dot(_mx(g), wg_ref[...], preferred_element_type=F32) + bg_ref[...])
    ka = attn_ref.shape[1]
    mix = (jnp.dot(_mx(attn_ref[...]), wo_ref[0:ka, :], preferred_element_type=F32)
           + jnp.dot(_mx(ssm), wo_ref[ka:, :], preferred_element_type=F32))
    _tail_body(mix, x_ref, g1_ref, b1_ref, w1_ref, w2_ref, g2_ref, b2_ref, o_ref)


def _tail_odd_kernel(a_ref, wo_ref, x_ref, g1_ref, b1_ref, w1_ref, w2_ref, g2_ref, b2_ref, o_ref):
    mix = jnp.dot(_mx(a_ref[...]), wo_ref[...], preferred_element_type=F32)
    _tail_body(mix, x_ref, g1_ref, b1_ref, w1_ref, w2_ref, g2_ref, b2_ref, o_ref)


def _row(v):
    return v.reshape(1, -1).astype(F32)


def _tail_common_specs(tm, d, dff):
    return [_resident((d, d)),
            pl.BlockSpec((tm, d), lambda i: (i, 0)),
            _resident((1, d)), _resident((1, d)),
            _resident((d, dff)), _resident((dff, d)),
            _resident((1, d)), _resident((1, d))]


def _tail_even(attn2d, y2d, hp2d, x2d, d_skip, w_glu, b_glu, w_out, g1, b1, w1, w2, g2, b2, tm):
    m, d = x2d.shape
    dff = w1.shape[1]
    ucol = hp2d.shape[1] // S5_WIDTH - 1
    return pl.pallas_call(
        _tail_even_kernel,
        grid=(m // tm,),
        in_specs=[pl.BlockSpec((tm, A_WIDTH), lambda i: (i, 0)),
                  pl.BlockSpec((tm, S5_WIDTH), lambda i: (i, 0)),
                  pl.BlockSpec((tm, S5_WIDTH), lambda i: (i, ucol)),
                  _resident((1, S5_WIDTH)), _resident((S5_WIDTH, S5_WIDTH)), _resident((1, S5_WIDTH))]
        + _tail_common_specs(tm, d, dff),
        out_specs=pl.BlockSpec((tm, d), lambda i: (i, 0)),
        out_shape=jax.ShapeDtypeStruct((m, d), F32),
        compiler_params=_params("parallel"),
        name="tail_even",
    )(attn2d, y2d, hp2d, _row(d_skip), _mx(w_glu), _row(b_glu), _mx(w_out), x2d,
      _row(g1), _row(b1), _mx(w1), _mx(w2), _row(g2), _row(b2))


def _tail_odd(a2d, x2d, w_out, g1, b1, w1, w2, g2, b2, tm):
    m, d = x2d.shape
    dff = w1.shape[1]
    return pl.pallas_call(
        _tail_odd_kernel,
        grid=(m // tm,),
        in_specs=[pl.BlockSpec((tm, a2d.shape[1]), lambda i: (i, 0))] + _tail_common_specs(tm, d, dff),
        out_specs=pl.BlockSpec((tm, d), lambda i: (i, 0)),
        out_shape=jax.ShapeDtypeStruct((m, d), F32),
        compiler_params=_params("parallel"),
        name="tail_odd",
    )(a2d, _mx(w_out), x2d, _row(g1), _row(b1), _mx(w1), _mx(w2), _row(g2), _row(b2))


GDN_TILE = 256
GDN_GATES_TILE = 512


def _softplus(t):
    return jnp.maximum(t, 0.0) + jnp.log1p(jnp.exp(-jnp.abs(t)))


def _gdn_prep_body(i, hg, q_ref, k_ref, v_ref, qp_ref, kp_ref, vp_ref, cwq_ref, cwk_ref, cwv_ref,
                   ba_ref, g_ref, gt_ref, u_ref, w_ref, qd_ref, kd_ref, intra_ref, edl_ref,
                   qx_ref, kx_ref, vx_ref, ql_ref, kl_ref, vl_ref):
    assert C_CONV == 4
    t, width = q_ref.shape[1], q_ref.shape[2]
    c = C_CHUNK
    dk = C_HEAD_DIM
    hp = width // dk
    groups = t // SUBLANES

    def conv_silu(main_ref, prev_ref, cw_ref, ext_ref, low_ref):
        x = main_ref[0]
        ext_ref[0:SUBLANES, :] = jnp.zeros((SUBLANES, width), F32)
        ext_ref[SUBLANES:2 * SUBLANES, :] = jnp.where(i > 0, prev_ref[0], 0.0)
        ext_ref[2 * SUBLANES:, :] = x
        ext_rows = t + SUBLANES

        def tap(j):
            return jnp.broadcast_to(cw_ref[j:j + 1, :], (SUBLANES, width))[None]

        def grouped(a):
            return a.reshape(a.shape[0] // SUBLANES, SUBLANES, width)

        x_ext = grouped(ext_ref[SUBLANES:, :])
        x_del = grouped(ext_ref[SUBLANES - 1:SUBLANES - 1 + ext_rows, :])
        low_ref[...] = (tap(1) * x_ext + tap(0) * x_del).reshape(ext_rows, width)
        y = (tap(3) * grouped(x) + tap(2) * x_del[1:]
             + grouped(low_ref[SUBLANES - 2:SUBLANES - 2 + t, :]))
        hy = 0.5 * y.reshape(t, width)
        return hy + hy * jnp.tanh(hy)

    def l2n(z, scale):
        return z * (lax.rsqrt(jnp.sum(z * z, axis=-1, keepdims=True) + L2_EPS) * scale)

    q_all = conv_silu(q_ref, qp_ref, cwq_ref, qx_ref, ql_ref)
    k_all = conv_silu(k_ref, kp_ref, cwk_ref, kx_ref, kl_ref)
    v_all = conv_silu(v_ref, vp_ref, cwv_ref, vx_ref, vl_ref)
    heads = [slice(hh * dk, (hh + 1) * dk) for hh in range(hp)]
    qs = [l2n(q_all[:, cs], dk ** -0.5) for cs in heads]
    ks = [l2n(k_all[:, cs], 1.0) for cs in heads]
    vs = [v_all[:, cs] for cs in heads]

    ba = ba_ref[0]
    lane = lax.broadcasted_iota(jnp.int32, (t, LANES), 1)
    betas, g_cols, g_rows = [], [], []
    for hh in range(hp):
        h = hg * hp + hh
        betas.append(_sigmoid(jnp.sum(jnp.where(lane == h, ba, 0.0), axis=1, keepdims=True)))
        g_cols.append(jnp.sum(jnp.where(lane == C_HEADS + h, g_ref[0], 0.0), axis=1, keepdims=True))
        g_rows.append(gt_ref[0, pl.ds(C_HEADS + h, 1), :])

    ii = lax.broadcasted_iota(jnp.int32, (c, c), 0)
    jj = lax.broadcasted_iota(jnp.int32, (c, c), 1)
    low = ii >= jj
    units = [(hh, slice(ch * c, (ch + 1) * c)) for hh in range(hp) for ch in range(t // c)]
    gc_cols = [jnp.sum(jnp.where(low, g_rows[hh][:, r], 0.0), axis=1, keepdims=True) for hh, r in units]
    gc_rows = [jnp.sum(jnp.where(ii <= jj, g_cols[hh][r], 0.0), axis=0, keepdims=True) for hh, r in units]
    decays = [jnp.where(low, jnp.exp(jnp.where(low, a - b, 0.0)), 0.0) for a, b in zip(gc_cols, gc_rows)]
    beta_bs = [jnp.broadcast_to(betas[hh][r], (c, dk)) for hh, r in units]
    kbs = [ks[hh][r] * bb for bb, (hh, r) in zip(beta_bs, units)]
    kks = [_dot_nt(kb, ks[hh][r]) for kb, (hh, r) in zip(kbs, units)]
    qks = [_dot_nt(qs[hh][r], ks[hh][r]) for hh, r in units]
    pws = [jnp.where(ii > jj, -(kk * dec), 0.0) for kk, dec in zip(kks, decays)]
    tms = pws
    for _ in range(int(math.log2(c)) - 1):
        pbs = [_mx(pw) for pw in pws]
        pws = [_dot(pb, pb) for pb in pbs]
        pbs = [_mx(pw) for pw in pws]
        tms = [tm + pw + _dot(tm, pb) for tm, pw, pb in zip(tms, pws, pbs)]
    egcs = [jnp.broadcast_to(jnp.exp(a), (c, dk)) for a in gc_cols]
    vbs = [vs[hh][r] * bb for bb, (hh, r) in zip(beta_bs, units)]
    kbes = [kb * e for kb, e in zip(kbs, egcs)]
    us = [vb + _dot(tm, vb) for tm, vb in zip(tms, vbs)]
    ws = [kbe + _dot(tm, kbe) for tm, kbe in zip(tms, kbes)]
    for n, (hh, r) in enumerate(units):
        gl = gc_cols[n][c - 1:c, :]
        cs = heads[hh]
        ch = r.start // c
        u_ref[0, r, cs] = us[n]
        w_ref[0, r, cs] = _mx(ws[n])
        qd_ref[0, r, cs] = _mx(qs[hh][r] * egcs[n])
        kd_ref[0, r, cs] = _mx(ks[hh][r] * jnp.exp(gl - gc_cols[n]))
        intra_ref[0, hh, r, :] = _mx(qks[n] * decays[n])
        edl_ref[0, hh, ch:ch + 1, :] = jnp.broadcast_to(jnp.exp(gl), (1, LANES))


def _gdn_gates_kernel(ba_ref, alog_ref, dtb_ref, g_ref, gt_ref):
    g_all = -jnp.exp(alog_ref[...]) * _softplus(ba_ref[0] + dtb_ref[...])
    g_ref[0] = g_all
    gt_ref[0] = g_all.T


def _gdn_gates(hp3d, alog_row, dtb_row):
    b, s, _ = hp3d.shape
    t = GDN_GATES_TILE
    ba_col = 4 * C_WIDTH // LANES
    return pl.pallas_call(
        _gdn_gates_kernel,
        grid=(b, s // t),
        in_specs=[pl.BlockSpec((1, t, LANES), lambda bi, i: (bi, i, ba_col)),
                  pl.BlockSpec((1, LANES), lambda bi, i: (0, 0)),
                  pl.BlockSpec((1, LANES), lambda bi, i: (0, 0))],
        out_specs=[pl.BlockSpec((1, t, LANES), lambda bi, i: (bi, i, 0)),
                   pl.BlockSpec((1, LANES, t), lambda bi, i: (bi, 0, i))],
        out_shape=[jax.ShapeDtypeStruct((b, s, LANES), F32), jax.ShapeDtypeStruct((b, LANES, s), F32)],
        compiler_params=_params("parallel", "parallel"),
        name="gdn_gates",
    )(hp3d, alog_row, dtb_row)


def _gdn_scan_body(chunk0, u_ref, w_ref, qd_ref, kd_ref, intra_ref, edl_ref, gate_ref, nw_ref, o_ref, st_ref):
    c = C_CHUNK
    dk = C_HEAD_DIM
    t = u_ref.shape[1]
    nw = nw_ref[...]
    heads = [slice(h * dk, (h + 1) * dk) for h in range(C_HEADS)]
    for ch in range(t // c):
        r = slice(ch * c, (ch + 1) * c)
        sts = [st_ref[h] for h in range(C_HEADS)]
        wss = [_dot(jnp.concatenate([w_ref[0, r, cs], qd_ref[0, r, cs]], axis=0), st)
               for cs, st in zip(heads, sts)]
        v_news = [u_ref[0, r, cs] - ws[:c] for cs, ws in zip(heads, wss)]
        kvs = [_dot_tn(kd_ref[0, r, cs], v_new) for cs, v_new in zip(heads, v_news)]
        os_ = [ws[c:] + _dot(intra_ref[0, h, r, :], v_new) for h, (ws, v_new) in enumerate(zip(wss, v_news))]
        for h, cs in enumerate(heads):
            st_ref[h] = sts[h] * edl_ref[0, h, pl.ds(chunk0 + ch, 1), 0:1] + kvs[h]
            o = os_[h]
            o = o * lax.rsqrt(jnp.mean(o * o, axis=-1, keepdims=True) + RMS_EPS) * nw
            hg = 0.5 * gate_ref[0, r, cs]
            o_ref[0, r, cs] = o * (hg + hg * jnp.tanh(hg))


def _gdn_kernel(q_ref, k_ref, v_ref, qp_ref, kp_ref, vp_ref, cwq_ref, cwk_ref, cwv_ref, ba_ref, g_ref, gt_ref,
                gate_ref, nw_ref, o_ref, st_ref, qx_ref, kx_ref, vx_ref, ql_ref, kl_ref, vl_ref, *bufs):
    j = pl.program_id(1)
    last = pl.num_programs(1) - 2
    cur, nxt = bufs[:len(bufs) // 2], bufs[len(bufs) // 2:]

    @pl.when(j == 0)
    def _():
        st_ref[...] = jnp.zeros_like(st_ref)
        for ref in bufs:
            ref[...] = jnp.zeros_like(ref)

    _gdn_prep_body(jnp.minimum(j, last), 0, q_ref, k_ref, v_ref, qp_ref, kp_ref, vp_ref, cwq_ref, cwk_ref, cwv_ref,
                   ba_ref, g_ref, gt_ref, *nxt, qx_ref, kx_ref, vx_ref, ql_ref, kl_ref, vl_ref)
    _gdn_scan_body(0, *cur, gate_ref, nw_ref, o_ref, st_ref)
    for dst, src in zip(cur, nxt):
        dst[...] = src[...]


def _gdn(hp3d, conv_w, alog_row, dtb_row, norm_w):
    b, s, _ = hp3d.shape
    ba_col = 4 * C_WIDTH // LANES
    t = GDN_TILE
    nt = s // t
    per = t // SUBLANES
    wd = C_WIDTH
    nc = t // C_CHUNK
    g, gt = _gdn_gates(hp3d, alog_row, dtb_row)

    def prep_tile(j):
        return jnp.minimum(j, nt - 1)

    def scan_tile(j):
        return jnp.maximum(j - 1, 0)

    def main(off):
        return pl.BlockSpec((1, t, wd), lambda bi, j: (bi, prep_tile(j), off))

    def prev(off):
        return pl.BlockSpec((1, SUBLANES, wd), lambda bi, j: (bi, jnp.maximum(prep_tile(j) * per - 1, 0), off))

    def cw(off):
        return pl.BlockSpec((C_CONV, wd), lambda bi, j: (0, off))

    bufs = [pltpu.VMEM((1, t, wd), F32)] + [pltpu.VMEM((1, t, wd), MXU_DTYPE)] * 3 + [
        pltpu.VMEM((1, C_HEADS, t, C_CHUNK), MXU_DTYPE), pltpu.VMEM((1, C_HEADS, max(nc, SUBLANES), LANES), F32)]
    return pl.pallas_call(
        _gdn_kernel,
        grid=(b, nt + 1),
        in_specs=[main(0), main(1), main(2), prev(0), prev(1), prev(2), cw(0), cw(1), cw(2),
                  pl.BlockSpec((1, t, LANES), lambda bi, j: (bi, prep_tile(j), ba_col)),
                  pl.BlockSpec((1, t, LANES), lambda bi, j: (bi, prep_tile(j), 0)),
                  pl.BlockSpec((1, LANES, t), lambda bi, j: (bi, 0, prep_tile(j))),
                  pl.BlockSpec((1, t, wd), lambda bi, j: (bi, scan_tile(j), 3)),
                  pl.BlockSpec((1, C_HEAD_DIM), lambda bi, j: (0, 0))],
        out_specs=pl.BlockSpec((1, t, wd), lambda bi, j: (bi, scan_tile(j), 0)),
        out_shape=jax.ShapeDtypeStruct((b, s, wd), F32),
        scratch_shapes=[pltpu.VMEM((C_HEADS, C_HEAD_DIM, C_HEAD_DIM), F32)]
        + [pltpu.VMEM((t + 2 * SUBLANES, wd), F32)] * 3 + [pltpu.VMEM((t + SUBLANES, wd), F32)] * 3
        + bufs + bufs,
        compiler_params=_params("parallel", "arbitrary"),
        name="gdn",
    )(hp3d, hp3d, hp3d, hp3d, hp3d, hp3d, conv_w, conv_w, conv_w, hp3d, g, gt, hp3d, _row(norm_w))


ROW_TILE = 512


def kernel(x, rel_table, ev_w_in, ev_w_out, s5_lambda_re, s5_lambda_im, s5_b_re, s5_b_im, s5_c_re, s5_c_im, s5_d, s5_log_dt, s5_w_glu, s5_b_glu, od_w_in, od_conv_w, od_a_log, od_dt_bias, od_norm_w, od_w_out, ln_mix_g, ln_mix_b, mlp_w1, mlp_w2, ln_ffn_g, ln_ffn_b):
    bsz, s, d = x.shape
    m = bsz * s
    x2d = x.reshape(m, d)
    table_t = rel_table.T.astype(F32)
    bias = _bias_tiles(table_t)
    for layer in range(DEPTH):
        i = layer // 2
        if layer % 2 == 0:
            hp = _proj(x2d, _mx(ev_w_in[i]), ROW_TILE)
            hp3d = hp.reshape(bsz, s, hp.shape[1])
            attn = _moba(hp3d, bias)
            wx, wy, a_re2, a_im2 = _s5_weights(s5_lambda_re[i], s5_lambda_im[i], s5_b_re[i], s5_b_im[i],
                                               s5_c_re[i], s5_c_im[i], s5_log_dt[i])
            y = _s5_scan(hp3d, wx, wy, a_re2, a_im2)
            x2d = _tail_even(attn.reshape(m, A_WIDTH), y.reshape(m, S5_WIDTH), hp, x2d,
                             s5_d[i], s5_w_glu[i], s5_b_glu[i], ev_w_out[i],
                             ln_mix_g[layer], ln_mix_b[layer], mlp_w1[layer], mlp_w2[layer],
                             ln_ffn_g[layer], ln_ffn_b[layer], ROW_TILE)
        else:
            w_in = od_w_in[i]
            w_all = jnp.pad(w_in, ((0, 0), (0, LANES - 2 * C_HEADS)))
            hp = _proj(x2d, _mx(w_all), ROW_TILE)
            hp3d = hp.reshape(bsz, s, 4 * C_WIDTH + LANES)
            pad8 = jnp.zeros((C_HEADS,), F32)
            padr = jnp.zeros((LANES - 2 * C_HEADS,), F32)
            alog_row = jnp.concatenate([pad8, od_a_log[i].astype(F32), padr]).reshape(1, LANES)
            dtb_row = jnp.concatenate([pad8, od_dt_bias[i].astype(F32), padr]).reshape(1, LANES)
            o = _gdn(hp3d, od_conv_w[i].astype(F32), alog_row, dtb_row, od_norm_w[i])
            x2d = _tail_odd(o.reshape(m, C_WIDTH), x2d, od_w_out[i],
                            ln_mix_g[layer], ln_mix_b[layer], mlp_w1[layer], mlp_w2[layer],
                            ln_ffn_g[layer], ln_ffn_b[layer], ROW_TILE)
    return x2d.reshape(bsz, s, d)
```

```python
import functools
import math

import jax
import jax.numpy as jnp
import numpy as np
from jax import lax
from jax.experimental import pallas as pl
from jax.experimental.pallas import tpu as pltpu

F32 = jnp.float32
MXU_DTYPE = jnp.bfloat16

LANES = 128
SUBLANES = 8
VMEM_LIMIT_BYTES = 56 * 1024 * 1024

A_HEADS = 8
A_HEAD_DIM = 64
A_WIDTH = A_HEADS * A_HEAD_DIM
MOBA_BLOCK = 256
MOBA_TOPK = 3
REL_BUCKETS = 32
REL_MAX_DIST = 128
S5_GROUP = 16
S5_GROUPS = 32
S5_STATE = 64
S5_WIDTH = S5_GROUP * S5_GROUPS
C_HEADS = 8
C_HEAD_DIM = 128
C_WIDTH = C_HEADS * C_HEAD_DIM
C_CONV = 4
C_CHUNK = 64
DEPTH = 2
DN_ALPHA = (2 * DEPTH) ** 0.25
LN_EPS = 1e-5
RMS_EPS = 1e-6
L2_EPS = 1e-6
NEG_INF = -1e30
LOG2E = math.log2(math.e)


def _mx(t):
    return t.astype(MXU_DTYPE)


def _dot(a, b):
    return jnp.dot(_mx(a), _mx(b), preferred_element_type=F32)


def _dot_tn(a, b):
    return lax.dot_general(_mx(a), _mx(b), (((0,), (0,)), ((), ())), preferred_element_type=F32)


def _dot_nt(a, b):
    return lax.dot_general(_mx(a), _mx(b), (((1,), (1,)), ((), ())), preferred_element_type=F32)


def _sigmoid(t):
    return 1.0 / (1.0 + jnp.exp(-t))


def _layer_norm(r, g, b):
    mu = jnp.mean(r, axis=-1, keepdims=True)
    d = r - mu
    var = jnp.mean(d * d, axis=-1, keepdims=True)
    return d * lax.rsqrt(var + LN_EPS) * g + b


def _params(*sem):
    return pltpu.CompilerParams(dimension_semantics=sem, vmem_limit_bytes=VMEM_LIMIT_BYTES)


def _resident(shape):
    return pl.BlockSpec(shape, lambda *_: (0,) * len(shape), pipeline_mode=pl.Buffered(1))


PROJ_CHUNK = 1024


def _proj_kernel(x_ref, w_ref, o_ref):
    xm = _mx(x_ref[...])
    n = w_ref.shape[1]
    for c0 in range(0, n, PROJ_CHUNK):
        c1 = min(c0 + PROJ_CHUNK, n)
        o_ref[:, c0:c1] = jnp.dot(xm, w_ref[:, c0:c1], preferred_element_type=F32)


def _proj(x2d, w, tm):
    m, k = x2d.shape
    n = w.shape[1]
    return pl.pallas_call(
        _proj_kernel,
        grid=(m // tm,),
        in_specs=[pl.BlockSpec((tm, k), lambda i: (i, 0)), _resident((k, n))],
        out_specs=pl.BlockSpec((tm, n), lambda i: (i, 0)),
        out_shape=jax.ShapeDtypeStruct((m, n), F32),
        compiler_params=_params("parallel"),
        name="in_proj",
    )(x2d, w)


def _t5_bucket_np(dist):
    max_exact = REL_BUCKETS // 2
    n = np.maximum(dist, 0)
    nf = np.maximum(n, 1).astype(np.float32)
    large = max_exact + (np.log(nf / np.float32(max_exact)) / np.float32(math.log(REL_MAX_DIST / max_exact))
                         * np.float32(REL_BUCKETS - max_exact)).astype(np.int32)
    large = np.minimum(large, REL_BUCKETS - 1)
    return np.where(n < max_exact, n, large).astype(np.int32)


def _bias_kernel(tab_ref, bkt_ref, o_ref):
    h = pl.program_id(0)
    blk = bkt_ref.shape[-1]
    key = lax.broadcasted_iota(jnp.int32, (blk, blk), 0)
    qry = lax.broadcasted_iota(jnp.int32, (blk, blk), 1)
    far = tab_ref[h, REL_BUCKETS - 1]
    for t in range(2):
        bk = bkt_ref[t]
        acc = jnp.zeros((blk, blk), F32)
        for i in range(REL_BUCKETS):
            acc = jnp.where(bk == i, (tab_ref[h, i] - far) * LOG2E, acc)
        if t == 0:
            acc = jnp.where(qry >= key, acc, NEG_INF)
        o_ref[0, t] = acc


def _bias_tiles(table_t):
    heads = table_t.shape[0]
    offs = np.arange(MOBA_BLOCK)
    dist_own = offs[None, :] - offs[:, None]
    buckets = np.stack([_t5_bucket_np(dist_own), _t5_bucket_np(dist_own + MOBA_BLOCK)])
    return pl.pallas_call(
        _bias_kernel,
        grid=(heads,),
        in_specs=[pl.BlockSpec(memory_space=pltpu.SMEM),
                  pl.BlockSpec((2, MOBA_BLOCK, MOBA_BLOCK), lambda h: (0, 0, 0))],
        out_specs=pl.BlockSpec((1, 2, MOBA_BLOCK, MOBA_BLOCK), lambda h: (h, 0, 0, 0)),
        out_shape=jax.ShapeDtypeStruct((heads, 2, MOBA_BLOCK, MOBA_BLOCK), F32),
        compiler_params=_params("parallel"),
        name="t5_bias",
    )(table_t, jnp.asarray(buckets))


def _moba_kernel(q_ref, k_ref, v_ref, bias_ref, o_ref, kmean_ref, kb_ref, vt_ref,
                 st0_ref, st1_ref, wq_ref, acc_ref):
    st_refs = (st0_ref, st1_ref)
    qb = pl.program_id(2)
    blk = MOBA_BLOCK
    d = A_HEAD_DIM
    nh = LANES // d
    nb = k_ref.shape[1] // blk
    ng = q_ref.shape[2] // LANES
    groups = range(ng)

    @pl.when(qb == 0)
    def _():
        lane = lax.broadcasted_iota(jnp.int32, (blk, LANES - d), 1)
        for g in groups:
            gl = slice(g * LANES, (g + 1) * LANES)
            for j in range(nb + 1):
                onehot = jnp.where(lane == j, 1.0, 0.0).astype(MXU_DTYPE)
                for hh in range(nh):
                    kb_ref[g, hh, j, :, d:LANES] = onehot
            for j in range(nb):
                kj = k_ref[0, j * blk:(j + 1) * blk, gl]
                for hh in range(nh):
                    kb_ref[g, hh, j, :, 0:d] = _mx(kj[:, hh * d:(hh + 1) * d])
                kmean_ref[g, j:j + 1, :] = jnp.mean(kj, axis=0, keepdims=True)
                vt_ref[g, j] = _mx(v_ref[0, j * blk:(j + 1) * blk, gl].T)
            for hh in range(nh):
                kb_ref[g, hh, nb, :, 0:d] = jnp.zeros((blk, d), MXU_DTYPE)
            vt_ref[g, nb] = jnp.zeros((LANES, blk), MXU_DTYPE)

    blk_id = lax.broadcasted_iota(jnp.int32, (nb, blk), 0)
    row = lax.broadcasted_iota(jnp.int32, (LANES - d - nb, blk), 0)
    for g in groups:
        qt = (q_ref[0, :, g * LANES:(g + 1) * LANES] * (d ** -0.5)).T
        for hh in range(nh):
            gate = lax.dot_general(kmean_ref[g, :, hh * d:(hh + 1) * d], qt[hh * d:(hh + 1) * d],
                                   (((1,), (0,)), ((), ())),
                                   precision=lax.Precision.HIGHEST, preferred_element_type=F32)
            gt = jnp.where(blk_id < qb, gate, -jnp.inf)
            sel = jnp.zeros((nb, blk), F32)
            for _ in range(MOBA_TOPK):
                m = jnp.max(gt, axis=0, keepdims=True)
                is_m = jnp.logical_and(gt == m, gt > -jnp.inf)
                idx = jnp.min(jnp.where(is_m, blk_id, nb), axis=0, keepdims=True)
                pick = blk_id == idx
                sel = jnp.where(pick, 1.0, sel)
                gt = jnp.where(pick, -jnp.inf, gt)
            keep = jnp.logical_or(sel > 0.0, blk_id == qb)
            wq_ref[g, hh] = _mx(jnp.concatenate([qt[hh * d:(hh + 1) * d] * LOG2E, jnp.where(keep, 0.0, NEG_INF),
                                                 jnp.where(row == 0, NEG_INF, 0.0)], axis=0))

    def score_stage(slot, ja, jb, with_bias):
        cmaxes = []
        for g in groups:
            for hh in range(nh):
                keys = jnp.concatenate([kb_ref[g, hh, ja], kb_ref[g, hh, jb]], axis=0)
                st = jnp.dot(keys, wq_ref[g, hh], preferred_element_type=F32)
                if with_bias:
                    st = st + jnp.concatenate([bias_ref[g * nh + hh, w] for w in range(2)], axis=0)
                st_refs[slot][g, 0, :, hh * blk:(hh + 1) * blk] = st[:blk]
                st_refs[slot][g, 1, :, hh * blk:(hh + 1) * blk] = st[blk:]
                cmaxes.append(jnp.max(st, axis=0, keepdims=True))
        return tuple(cmaxes)

    def attend_stage(slot, cmaxes, stats, ja, jb):
        new = []
        for g in groups:
            for hh in range(nh):
                cs = slice(hh * blk, (hh + 1) * blk)
                m_new = cmaxes[g * nh + hh]
                if stats is not None:
                    m, l = stats[g * nh + hh]
                    m_new = jnp.maximum(m, m_new)
                ps = [jnp.exp2(st_refs[slot][g, w, :, cs] - m_new) for w in range(2)]
                l_new = jnp.sum(ps[0], axis=0, keepdims=True) + jnp.sum(ps[1], axis=0, keepdims=True)
                vt = jnp.concatenate([vt_ref[g, ja, hh * d:(hh + 1) * d, :],
                                      vt_ref[g, jb, hh * d:(hh + 1) * d, :]], axis=1)
                pv = jnp.dot(vt, jnp.concatenate([_mx(ps[0]), _mx(ps[1])], axis=0),
                             preferred_element_type=F32)
                if stats is None:
                    acc_ref[g, hh] = pv
                    new.append((m_new, l_new))
                else:
                    alpha = jnp.exp2(m - m_new)
                    acc_ref[g, hh] = acc_ref[g, hh] * alpha + pv
                    new.append((m_new, alpha * l + l_new))
        return tuple(new)

    nfar = qb - 1
    npairs = (nfar + 1) // 2

    def far_blocks(i):
        return jnp.where(2 * i < nfar, 2 * i, nb), jnp.where(2 * i + 1 < nfar, 2 * i + 1, nb)

    j_prev = jnp.where(qb > 0, qb - 1, nb)
    cmax0 = score_stage(0, qb, j_prev, True)
    cmax1 = score_stage(1, *far_blocks(0), False)
    init = (attend_stage(0, cmax0, None, qb, j_prev), cmax1)

    def trip(slot, k, carry):
        stats, cmax = carry
        cmax_next = score_stage(1 - slot, *far_blocks(k), False)
        return attend_stage(slot, cmax, stats, *far_blocks(k - 1)), cmax_next

    def trip_dyn(k, carry):
        return lax.cond(k % 2 == 1, functools.partial(trip, 1, k), functools.partial(trip, 0, k), carry)

    stats, _ = lax.fori_loop(1, npairs + 1, trip_dyn, init)
    for g in groups:
        out_t = jnp.concatenate([acc_ref[g, hh] / stats[g * nh + hh][1] for hh in range(nh)], axis=0)
        o_ref[0, :, g * LANES:(g + 1) * LANES] = out_t.T


MOBA_GROUPS = 4


def _moba(hp3d, bias):
    b, s, _ = hp3d.shape
    ng = MOBA_GROUPS
    wd = ng * LANES
    nsteps = A_WIDTH // wd
    hpp = LANES // A_HEAD_DIM
    nb = s // MOBA_BLOCK
    return pl.pallas_call(
        _moba_kernel,
        grid=(b, nsteps, nb),
        in_specs=[pl.BlockSpec((1, MOBA_BLOCK, wd), lambda bi, p, i: (bi, i, p)),
                  pl.BlockSpec((1, s, wd), lambda bi, p, i: (bi, 0, nsteps + p), pipeline_mode=pl.Buffered(1)),
                  pl.BlockSpec((1, s, wd), lambda bi, p, i: (bi, 0, 2 * nsteps + p), pipeline_mode=pl.Buffered(1)),
                  pl.BlockSpec((ng * hpp, 2, MOBA_BLOCK, MOBA_BLOCK), lambda bi, p, i: (p, 0, 0, 0))],
        out_specs=pl.BlockSpec((1, MOBA_BLOCK, wd), lambda bi, p, i: (bi, i, p)),
        out_shape=jax.ShapeDtypeStruct((b, s, A_WIDTH), F32),
        scratch_shapes=[pltpu.VMEM((ng, nb, LANES), F32),
                        pltpu.VMEM((ng, hpp, nb + 1, MOBA_BLOCK, LANES), MXU_DTYPE),
                        pltpu.VMEM((ng, nb + 1, LANES, MOBA_BLOCK), MXU_DTYPE),
                        pltpu.VMEM((ng, 2, MOBA_BLOCK, hpp * MOBA_BLOCK), F32),
                        pltpu.VMEM((ng, 2, MOBA_BLOCK, hpp * MOBA_BLOCK), F32),
                        pltpu.VMEM((ng, hpp, LANES, MOBA_BLOCK), MXU_DTYPE),
                        pltpu.VMEM((ng, hpp, A_HEAD_DIM, MOBA_BLOCK), F32)],
        compiler_params=_params("parallel", "parallel", "arbitrary"),
        name="moba",
    )(hp3d, hp3d, hp3d, bias)


S5_HALF_CH = S5_WIDTH // 2
S5_HALF_ST = (S5_GROUPS // 2) * S5_STATE
S5_TILE = 256
S5_NT = S5_HALF_ST // S5_TILE
S5_STEPS = 64
S5_PITCH = S5_STEPS + SUBLANES
S5_SLABS = 2 * S5_HALF_ST // LANES


def _s5_kernel(u_ref, wx_ref, wy_ref, a_ref, o_ref, st_ref, x_scr, h_scr):
    nb, steps, _ = u_ref.shape
    nseq = 2 * nb
    tile_slabs = S5_TILE // LANES

    @pl.when(pl.program_id(0) == 0)
    def _():
        st_ref[...] = jnp.zeros_like(st_ref)

    for m in range(S5_NT // 2):
        lhs = [jnp.concatenate([_mx(u_ref[bi, :, half * S5_HALF_CH + m * LANES:half * S5_HALF_CH + (m + 1) * LANES])
                                for bi in range(nb)], axis=0) for half in range(2)]
        for part in range(2):
            for n in (2 * m, 2 * m + 1):
                slab0 = (part * S5_HALF_ST + n * S5_TILE) // LANES
                for half in range(2):
                    res = jnp.dot(lhs[half], wx_ref[part, n, half * LANES:(half + 1) * LANES, :],
                                  preferred_element_type=F32)
                    for bi in range(nb):
                        seq = 2 * bi + half
                        for sl in range(tile_slabs):
                            x_scr[slab0 + sl, seq * S5_PITCH:seq * S5_PITCH + steps, :] = (
                                res[bi * steps:(bi + 1) * steps, sl * LANES:(sl + 1) * LANES])

    nre = S5_SLABS // 2
    a_re = [a_ref[0, :, s * LANES:(s + 1) * LANES] for s in range(nre)]
    a_im = [a_ref[1, :, s * LANES:(s + 1) * LANES] for s in range(nre)]

    def step(t, carry):
        new_re, new_im = [], []
        for s in range(nre):
            hr, hi = carry[s], carry[nre + s]
            rows_t = pl.ds(t, nseq, stride=S5_PITCH)
            nhr = a_re[s] * hr - a_im[s] * hi + x_scr[s, rows_t, :]
            nhi = a_re[s] * hi + a_im[s] * hr + x_scr[nre + s, rows_t, :]
            h_scr[s, rows_t, :] = nhr
            h_scr[nre + s, rows_t, :] = nhi
            new_re.append(nhr)
            new_im.append(nhi)
        return tuple(new_re + new_im)

    init = tuple(st_ref[p, :, s * LANES:(s + 1) * LANES] for p in range(2) for s in range(nre))
    final = lax.fori_loop(0, steps, step, init, unroll=8)
    for p in range(2):
        for s in range(nre):
            st_ref[p, :, s * LANES:(s + 1) * LANES] = final[p * nre + s]

    for m in range(S5_NT // 2):
        for half in range(2):
            acc = jnp.zeros((nb * steps, LANES), F32)
            for n in (2 * m, 2 * m + 1):
                for part in range(2):
                    slab0 = (part * S5_HALF_ST + n * S5_TILE) // LANES
                    h_tile = jnp.concatenate(
                        [jnp.concatenate([h_scr[slab0 + sl, (2 * bi + half) * S5_PITCH:(2 * bi + half) * S5_PITCH + steps, :]
                                          for sl in range(tile_slabs)], axis=1) for bi in range(nb)], axis=0)
                    acc = acc + jnp.dot(_mx(h_tile), wy_ref[part, n, :, half * LANES:(half + 1) * LANES],
                                        preferred_element_type=F32)
            c = half * S5_HALF_CH + m * LANES
            for bi in range(nb):
                o_ref[bi, :, c:c + LANES] = acc[bi * steps:(bi + 1) * steps, :]


def _s5_weights(lam_re, lam_im, b_re, b_im, c_re, c_im, log_dt):
    lr = jnp.minimum(lam_re.astype(F32), -1e-4)
    li = lam_im.astype(F32)
    dt = jnp.exp(log_dt.astype(F32))[:, None]
    mag = jnp.exp(lr * dt)
    a_re = mag * jnp.cos(li * dt)
    a_im = mag * jnp.sin(li * dt)
    den = lr * lr + li * li
    coef_re = ((a_re - 1.0) * lr + a_im * li) / den
    coef_im = (a_im * lr - (a_re - 1.0) * li) / den
    br = b_re.astype(F32)
    bi = b_im.astype(F32)
    bb_re = coef_re[..., None] * br - coef_im[..., None] * bi
    bb_im = coef_re[..., None] * bi + coef_im[..., None] * br
    gh = S5_GROUPS // 2
    eye = jnp.eye(gh, dtype=F32)

    def b_dense(bb):
        t = bb.reshape(2, gh, S5_STATE, S5_GROUP)
        return jnp.einsum('fgph,gk->fghkp', t, eye).reshape(2, S5_HALF_CH, S5_HALF_ST)

    def c_dense(cc):
        t = cc.reshape(2, gh, S5_GROUP, S5_STATE)
        return jnp.einsum('fghp,gk->fgpkh', t, eye).reshape(2, S5_HALF_ST, S5_HALF_CH)

    def x_tiles(bd):
        tiles = []
        for n in range(S5_NT):
            c = (n // 2) * LANES
            tiles.append(jnp.concatenate([bd[0, c:c + LANES, n * S5_TILE:(n + 1) * S5_TILE],
                                          bd[1, c:c + LANES, n * S5_TILE:(n + 1) * S5_TILE]], axis=0))
        return jnp.stack(tiles)

    def y_tiles(cd):
        tiles = []
        for n in range(S5_NT):
            c = (n // 2) * LANES
            tiles.append(jnp.concatenate([cd[0, n * S5_TILE:(n + 1) * S5_TILE, c:c + LANES],
                                          cd[1, n * S5_TILE:(n + 1) * S5_TILE, c:c + LANES]], axis=1))
        return jnp.stack(tiles)

    wx = jnp.stack([x_tiles(b_dense(bb_re)), x_tiles(b_dense(bb_im))]).astype(MXU_DTYPE)
    wy = jnp.stack([y_tiles(c_dense(c_re.astype(F32))), y_tiles(c_dense(-c_im.astype(F32)))]).astype(MXU_DTYPE)
    return wx, wy, a_re.reshape(2, S5_HALF_ST), a_im.reshape(2, S5_HALF_ST)


def _s5_scan(hp3d, wx, wy, a_re2, a_im2):
    b, s, width = hp3d.shape
    nseq = 2 * b
    assert nseq == SUBLANES
    sel = jnp.arange(nseq) % 2
    a = jnp.stack([a_re2[sel], a_im2[sel]])
    scr = pltpu.VMEM((S5_SLABS, nseq * S5_PITCH, LANES), F32)
    return pl.pallas_call(
        _s5_kernel,
        grid=(s // S5_STEPS,),
        in_specs=[pl.BlockSpec((b, S5_STEPS, S5_WIDTH), lambda i: (0, i, width // S5_WIDTH - 1)),
                  _resident(wx.shape), _resident(wy.shape), _resident(a.shape)],
        out_specs=pl.BlockSpec((b, S5_STEPS, S5_WIDTH), lambda i: (0, i, 0)),
        out_shape=jax.ShapeDtypeStruct((b, s, S5_WIDTH), F32),
        scratch_shapes=[pltpu.VMEM((2, nseq, S5_HALF_ST), F32), scr, scr],
        compiler_params=_params("arbitrary"),
        name="s5_scan",
    )(hp3d, wx, wy, a)


MLP_CHUNK = 1024


def _tail_body(mix, x_ref, g1_ref, b1_ref, w1_ref, w2_ref, g2_ref, b2_ref, o_ref):
    x1 = _layer_norm(DN_ALPHA * x_ref[...] + mix, g1_ref[...], b1_ref[...])
    x1m = _mx(x1)
    acc = jnp.zeros(x1.shape, F32)
    for f in range(w1_ref.shape[1] // MLP_CHUNK):
        hid = jnp.dot(x1m, w1_ref[:, f * MLP_CHUNK:(f + 1) * MLP_CHUNK], preferred_element_type=F32)
        hid = jnp.square(jnp.maximum(hid, 0.0))
        acc = acc + jnp.dot(_mx(hid), w2_ref[f * MLP_CHUNK:(f + 1) * MLP_CHUNK, :],
                            preferred_element_type=F32)
    o_ref[...] = _layer_norm(DN_ALPHA * x1 + acc, g2_ref[...], b2_ref[...])


def _tail_even_kernel(attn_ref, y_ref, u_ref, dsk_ref, wg_ref, bg_ref, wo_ref, x_ref,
                      g1_ref, b1_ref, w1_ref, w2_ref, g2_ref, b2_ref, o_ref):
    y = y_ref[...] + dsk_ref[...] * u_ref[...]
    g = 0.5 * y * (1.0 + jnp.tanh(math.sqrt(2.0 / math.pi) * (y + 0.044715 * (y * y * y))))
    ssm = g * _sigmoid(jnp.dot(_mx(g), wg_ref[...], preferred_element_type=F32) + bg_ref[...])
    ka = attn_ref.shape[1]
    mix = (jnp.dot(_mx(attn_ref[...]), wo_ref[0:ka, :], preferred_element_type=F32)
           + jnp.dot(_mx(ssm), wo_ref[ka:, :], preferred_element_type=F32))
    _tail_body(mix, x_ref, g1_ref, b1_ref, w1_ref, w2_ref, g2_ref, b2_ref, o_ref)


def _tail_odd_kernel(a_ref, wo_ref, x_ref, g1_ref, b1_ref, w1_ref, w2_ref, g2_ref, b2_ref, o_ref):
    mix = jnp.dot(_mx(a_ref[...]), wo_ref[...], preferred_element_type=F32)
    _tail_body(mix, x_ref, g1_ref, b1_ref, w1_ref, w2_ref, g2_ref, b2_ref, o_ref)


def _row(v):
    return v.reshape(1, -1).astype(F32)


def _tail_common_specs(tm, d, dff):
    return [_resident((d, d)),
            pl.BlockSpec((tm, d), lambda i: (i, 0)),
            _resident((1, d)), _resident((1, d)),
            _resident((d, dff)), _resident((dff, d)),
            _resident((1, d)), _resident((1, d))]


def _tail_even(attn2d, y2d, hp2d, x2d, d_skip, w_glu, b_glu, w_out, g1, b1, w1, w2, g2, b2, tm):
    m, d = x2d.shape
    dff = w1.shape[1]
    ucol = hp2d.shape[1] // S5_WIDTH - 1
    return pl.pallas_call(
        _tail_even_kernel,
        grid=(m // tm,),
        in_specs=[pl.BlockSpec((tm, A_WIDTH), lambda i: (i, 0)),
                  pl.BlockSpec((tm, S5_WIDTH), lambda i: (i, 0)),
                  pl.BlockSpec((tm, S5_WIDTH), lambda i: (i, ucol)),
                  _resident((1, S5_WIDTH)), _resident((S5_WIDTH, S5_WIDTH)), _resident((1, S5_WIDTH))]
        + _tail_common_specs(tm, d, dff),
        out_specs=pl.BlockSpec((tm, d), lambda i: (i, 0)),
        out_shape=jax.ShapeDtypeStruct((m, d), F32),
        compiler_params=_params("parallel"),
        name="tail_even",
    )(attn2d, y2d, hp2d, _row(d_skip), _mx(w_glu), _row(b_glu), _mx(w_out), x2d,
      _row(g1), _row(b1), _mx(w1), _mx(w2), _row(g2), _row(b2))


def _tail_odd(a2d, x2d, w_out, g1, b1, w1, w2, g2, b2, tm):
    m, d = x2d.shape
    dff = w1.shape[1]
    return pl.pallas_call(
        _tail_odd_kernel,
        grid=(m // tm,),
        in_specs=[pl.BlockSpec((tm, a2d.shape[1]), lambda i: (i, 0))] + _tail_common_specs(tm, d, dff),
        out_specs=pl.BlockSpec((tm, d), lambda i: (i, 0)),
        out_shape=jax.ShapeDtypeStruct((m, d), F32),
        compiler_params=_params("parallel"),
        name="tail_odd",
    )(a2d, _mx(w_out), x2d, _row(g1), _row(b1), _mx(w1), _mx(w2), _row(g2), _row(b2))


GDN_TILE = 256
GDN_GATES_TILE = 512


def _softplus(t):
    return jnp.maximum(t, 0.0) + jnp.log1p(jnp.exp(-jnp.abs(t)))


def _gdn_prep_body(i, hg, q_ref, k_ref, v_ref, qp_ref, kp_ref, vp_ref, cwq_ref, cwk_ref, cwv_ref,
                   ba_ref, g_ref, gt_ref, u_ref, w_ref, qd_ref, kd_ref, intra_ref, edl_ref,
                   qx_ref, kx_ref, vx_ref, ql_ref, kl_ref, vl_ref):
    assert C_CONV == 4
    t, width = q_ref.shape[1], q_ref.shape[2]
    c = C_CHUNK
    dk = C_HEAD_DIM
    hp = width // dk
    groups = t // SUBLANES

    def conv_silu(main_ref, prev_ref, cw_ref, ext_ref, low_ref):
        x = main_ref[0]
        ext_ref[0:SUBLANES, :] = jnp.zeros((SUBLANES, width), F32)
        ext_ref[SUBLANES:2 * SUBLANES, :] = jnp.where(i > 0, prev_ref[0], 0.0)
        ext_ref[2 * SUBLANES:, :] = x
        ext_rows = t + SUBLANES

        def tap(j):
            return jnp.broadcast_to(cw_ref[j:j + 1, :], (SUBLANES, width))[None]

        def grouped(a):
            return a.reshape(a.shape[0] // SUBLANES, SUBLANES, width)

        x_ext = grouped(ext_ref[SUBLANES:, :])
        x_del = grouped(ext_ref[SUBLANES - 1:SUBLANES - 1 + ext_rows, :])
        low_ref[...] = (tap(1) * x_ext + tap(0) * x_del).reshape(ext_rows, width)
        y = (tap(3) * grouped(x) + tap(2) * x_del[1:]
             + grouped(low_ref[SUBLANES - 2:SUBLANES - 2 + t, :]))
        hy = 0.5 * y.reshape(t, width)
        return hy + hy * jnp.tanh(hy)

    def l2n(z, scale):
        return z * (lax.rsqrt(jnp.sum(z * z, axis=-1, keepdims=True) + L2_EPS) * scale)

    q_all = conv_silu(q_ref, qp_ref, cwq_ref, qx_ref, ql_ref)
    k_all = conv_silu(k_ref, kp_ref, cwk_ref, kx_ref, kl_ref)
    v_all = conv_silu(v_ref, vp_ref, cwv_ref, vx_ref, vl_ref)
    heads = [slice(hh * dk, (hh + 1) * dk) for hh in range(hp)]
    qs = [l2n(q_all[:, cs], dk ** -0.5) for cs in heads]
    ks = [l2n(k_all[:, cs], 1.0) for cs in heads]
    vs = [v_all[:, cs] for cs in heads]

    ba = ba_ref[0]
    lane = lax.broadcasted_iota(jnp.int32, (t, LANES), 1)
    betas, g_cols, g_rows = [], [], []
    for hh in range(hp):
        h = hg * hp + hh
        betas.append(_sigmoid(jnp.sum(jnp.where(lane == h, ba, 0.0), axis=1, keepdims=True)))
        g_cols.append(jnp.sum(jnp.where(lane == C_HEADS + h, g_ref[0], 0.0), axis=1, keepdims=True))
        g_rows.append(gt_ref[0, pl.ds(C_HEADS + h, 1), :])

    ii = lax.broadcasted_iota(jnp.int32, (c, c), 0)
    jj = lax.broadcasted_iota(jnp.int32, (c, c), 1)
    low = ii >= jj
    units = [(hh, slice(ch * c, (ch + 1) * c)) for hh in range(hp) for ch in range(t // c)]
    gc_cols = [jnp.sum(jnp.where(low, g_rows[hh][:, r], 0.0), axis=1, keepdims=True) for hh, r in units]
    gc_rows = [jnp.sum(jnp.where(ii <= jj, g_cols[hh][r], 0.0), axis=0, keepdims=True) for hh, r in units]
    decays = [jnp.where(low, jnp.exp(jnp.where(low, a - b, 0.0)), 0.0) for a, b in zip(gc_cols, gc_rows)]
    beta_bs = [jnp.broadcast_to(betas[hh][r], (c, dk)) for hh, r in units]
    kbs = [ks[hh][r] * bb for bb, (hh, r) in zip(beta_bs, units)]
    kks = [_dot_nt(kb, ks[hh][r]) for kb, (hh, r) in zip(kbs, units)]
    qks = [_dot_nt(qs[hh][r], ks[hh][r]) for hh, r in units]
    pws = [jnp.where(ii > jj, -(kk * dec), 0.0) for kk, dec in zip(kks, decays)]
    tms = pws
    for _ in range(int(math.log2(c)) - 1):
        pbs = [_mx(pw) for pw in pws]
        pws = [_dot(pb, pb) for pb in pbs]
        pbs = [_mx(pw) for pw in pws]
        tms = [tm + pw + _dot(tm, pb) for tm, pw, pb in zip(tms, pws, pbs)]
    egcs = [jnp.broadcast_to(jnp.exp(a), (c, dk)) for a in gc_cols]
    vbs = [vs[hh][r] * bb for bb, (hh, r) in zip(beta_bs, units)]
    kbes = [kb * e for kb, e in zip(kbs, egcs)]
    us = [vb + _dot(tm, vb) for tm, vb in zip(tms, vbs)]
    ws = [kbe + _dot(tm, kbe) for tm, kbe in zip(tms, kbes)]
    for n, (hh, r) in enumerate(units):
        gl = gc_cols[n][c - 1:c, :]
        cs = heads[hh]
        ch = r.start // c
        u_ref[0, r, cs] = us[n]
        w_ref[0, r, cs] = _mx(ws[n])
        qd_ref[0, r, cs] = _mx(qs[hh][r] * egcs[n])
        kd_ref[0, r, cs] = _mx(ks[hh][r] * jnp.exp(gl - gc_cols[n]))
        intra_ref[0, hh, r, :] = _mx(qks[n] * decays[n])
        edl_ref[0, hh, ch:ch + 1, :] = jnp.broadcast_to(jnp.exp(gl), (1, LANES))


def _gdn_gates_kernel(ba_ref, alog_ref, dtb_ref, g_ref, gt_ref):
    g_all = -jnp.exp(alog_ref[...]) * _softplus(ba_ref[0] + dtb_ref[...])
    g_ref[0] = g_all
    gt_ref[0] = g_all.T


def _gdn_gates(hp3d, alog_row, dtb_row):
    b, s, _ = hp3d.shape
    t = GDN_GATES_TILE
    ba_col = 4 * C_WIDTH // LANES
    return pl.pallas_call(
        _gdn_gates_kernel,
        grid=(b, s // t),
        in_specs=[pl.BlockSpec((1, t, LANES), lambda bi, i: (bi, i, ba_col)),
                  pl.BlockSpec((1, LANES), lambda bi, i: (0, 0)),
                  pl.BlockSpec((1, LANES), lambda bi, i: (0, 0))],
        out_specs=[pl.BlockSpec((1, t, LANES), lambda bi, i: (bi, i, 0)),
                   pl.BlockSpec((1, LANES, t), lambda bi, i: (bi, 0, i))],
        out_shape=[jax.ShapeDtypeStruct((b, s, LANES), F32), jax.ShapeDtypeStruct((b, LANES, s), F32)],
        compiler_params=_params("parallel", "parallel"),
        name="gdn_gates",
    )(hp3d, alog_row, dtb_row)


def _gdn_scan_body(chunk0, u_ref, w_ref, qd_ref, kd_ref, intra_ref, edl_ref, gate_ref, nw_ref, o_ref, st_ref):
    c = C_CHUNK
    dk = C_HEAD_DIM
    t = u_ref.shape[1]
    nw = nw_ref[...]
    heads = [slice(h * dk, (h + 1) * dk) for h in range(C_HEADS)]
    for ch in range(t // c):
        r = slice(ch * c, (ch + 1) * c)
        sts = [st_ref[h] for h in range(C_HEADS)]
        wss = [_dot(jnp.concatenate([w_ref[0, r, cs], qd_ref[0, r, cs]], axis=0), st)
               for cs, st in zip(heads, sts)]
        v_news = [u_ref[0, r, cs] - ws[:c] for cs, ws in zip(heads, wss)]
        kvs = [_dot_tn(kd_ref[0, r, cs], v_new) for cs, v_new in zip(heads, v_news)]
        os_ = [ws[c:] + _dot(intra_ref[0, h, r, :], v_new) for h, (ws, v_new) in enumerate(zip(wss, v_news))]
        for h, cs in enumerate(heads):
            st_ref[h] = sts[h] * edl_ref[0, h, pl.ds(chunk0 + ch, 1), 0:1] + kvs[h]
            o = os_[h]
            o = o * lax.rsqrt(jnp.mean(o * o, axis=-1, keepdims=True) + RMS_EPS) * nw
            hg = 0.5 * gate_ref[0, r, cs]
            o_ref[0, r, cs] = o * (hg + hg * jnp.tanh(hg))


def _gdn_kernel(q_ref, k_ref, v_ref, qp_ref, kp_ref, vp_ref, cwq_ref, cwk_ref, cwv_ref, ba_ref, g_ref, gt_ref,
                gate_ref, nw_ref, o_ref, st_ref, qx_ref, kx_ref, vx_ref, ql_ref, kl_ref, vl_ref, *bufs):
    j = pl.program_id(1)
    last = pl.num_programs(1) - 2
    cur, nxt = bufs[:len(bufs) // 2], bufs[len(bufs) // 2:]

    @pl.when(j == 0)
    def _():
        st_ref[...] = jnp.zeros_like(st_ref)
        for ref in bufs:
            ref[...] = jnp.zeros_like(ref)

    _gdn_prep_body(jnp.minimum(j, last), 0, q_ref, k_ref, v_ref, qp_ref, kp_ref, vp_ref, cwq_ref, cwk_ref, cwv_ref,
                   ba_ref, g_ref, gt_ref, *nxt, qx_ref, kx_ref, vx_ref, ql_ref, kl_ref, vl_ref)
    _gdn_scan_body(0, *cur, gate_ref, nw_ref, o_ref, st_ref)
    for dst, src in zip(cur, nxt):
        dst[...] = src[...]


def _gdn(hp3d, conv_w, alog_row, dtb_row, norm_w):
    b, s, _ = hp3d.shape
    ba_col = 4 * C_WIDTH // LANES
    t = GDN_TILE
    nt = s // t
    per = t // SUBLANES
    wd = C_WIDTH
    nc = t // C_CHUNK
    g, gt = _gdn_gates(hp3d, alog_row, dtb_row)

    def prep_tile(j):
        return jnp.minimum(j, nt - 1)

    def scan_tile(j):
        return jnp.maximum(j - 1, 0)

    def main(off):
        return pl.BlockSpec((1, t, wd), lambda bi, j: (bi, prep_tile(j), off))

    def prev(off):
        return pl.BlockSpec((1, SUBLANES, wd), lambda bi, j: (bi, jnp.maximum(prep_tile(j) * per - 1, 0), off))

    def cw(off):
        return pl.BlockSpec((C_CONV, wd), lambda bi, j: (0, off))

    bufs = [pltpu.VMEM((1, t, wd), F32)] + [pltpu.VMEM((1, t, wd), MXU_DTYPE)] * 3 + [
        pltpu.VMEM((1, C_HEADS, t, C_CHUNK), MXU_DTYPE), pltpu.VMEM((1, C_HEADS, max(nc, SUBLANES), LANES), F32)]
    return pl.pallas_call(
        _gdn_kernel,
        grid=(b, nt + 1),
        in_specs=[main(0), main(1), main(2), prev(0), prev(1), prev(2), cw(0), cw(1), cw(2),
                  pl.BlockSpec((1, t, LANES), lambda bi, j: (bi, prep_tile(j), ba_col)),
                  pl.BlockSpec((1, t, LANES), lambda bi, j: (bi, prep_tile(j), 0)),
                  pl.BlockSpec((1, LANES, t), lambda bi, j: (bi, 0, prep_tile(j))),
                  pl.BlockSpec((1, t, wd), lambda bi, j: (bi, scan_tile(j), 3)),
                  pl.BlockSpec((1, C_HEAD_DIM), lambda bi, j: (0, 0))],
        out_specs=pl.BlockSpec((1, t, wd), lambda bi, j: (bi, scan_tile(j), 0)),
        out_shape=jax.ShapeDtypeStruct((b, s, wd), F32),
        scratch_shapes=[pltpu.VMEM((C_HEADS, C_HEAD_DIM, C_HEAD_DIM), F32)]
        + [pltpu.VMEM((t + 2 * SUBLANES, wd), F32)] * 3 + [pltpu.VMEM((t + SUBLANES, wd), F32)] * 3
        + bufs + bufs,
        compiler_params=_params("parallel", "arbitrary"),
        name="gdn",
    )(hp3d, hp3d, hp3d, hp3d, hp3d, hp3d, conv_w, conv_w, conv_w, hp3d, g, gt, hp3d, _row(norm_w))


ROW_TILE = 512


def kernel(x, rel_table, ev_w_in, ev_w_out, s5_lambda_re, s5_lambda_im, s5_b_re, s5_b_im, s5_c_re, s5_c_im, s5_d, s5_log_dt, s5_w_glu, s5_b_glu, od_w_in, od_conv_w, od_a_log, od_dt_bias, od_norm_w, od_w_out, ln_mix_g, ln_mix_b, mlp_w1, mlp_w2, ln_ffn_g, ln_ffn_b):
    bsz, s, d = x.shape
    m = bsz * s
    x2d = x.reshape(m, d)
    table_t = rel_table.T.astype(F32)
    bias = _bias_tiles(table_t)
    for layer in range(DEPTH):
        i = layer // 2
        if layer % 2 == 0:
            hp = _proj(x2d, _mx(ev_w_in[i]), ROW_TILE)
            hp3d = hp.reshape(bsz, s, hp.shape[1])
            attn = _moba(hp3d, bias)
            wx, wy, a_re2, a_im2 = _s5_weights(s5_lambda_re[i], s5_lambda_im[i], s5_b_re[i], s5_b_im[i],
                                               s5_c_re[i], s5_c_im[i], s5_log_dt[i])
            y = _s5_scan(hp3d, wx, wy, a_re2, a_im2)
            x2d = _tail_even(attn.reshape(m, A_WIDTH), y.reshape(m, S5_WIDTH), hp, x2d,
                             s5_d[i], s5_w_glu[i], s5_b_glu[i], ev_w_out[i],
                             ln_mix_g[layer], ln_mix_b[layer], mlp_w1[layer], mlp_w2[layer],
                             ln_ffn_g[layer], ln_ffn_b[layer], ROW_TILE)
        else:
            w_in = od_w_in[i]
            w_all = jnp.pad(w_in, ((0, 0), (0, LANES - 2 * C_HEADS)))
            hp = _proj(x2d, _mx(w_all), ROW_TILE)
            hp3d = hp.reshape(bsz, s, 4 * C_WIDTH + LANES)
            pad8 = jnp.zeros((C_HEADS,), F32)
            padr = jnp.zeros((LANES - 2 * C_HEADS,), F32)
            alog_row = jnp.concatenate([pad8, od_a_log[i].astype(F32), padr]).reshape(1, LANES)
            dtb_row = jnp.concatenate([pad8, od_dt_bias[i].astype(F32), padr]).reshape(1, LANES)
            o = _gdn(hp3d, od_conv_w[i].astype(F32), alog_row, dtb_row, od_norm_w[i])
            x2d = _tail_odd(o.reshape(m, C_WIDTH), x2d, od_w_out[i],
                            ln_mix_g[layer], ln_mix_b[layer], mlp_w1[layer], mlp_w2[layer],
                            ln_ffn_g[layer], ln_ffn_b[layer], ROW_TILE)
    return x2d.reshape(bsz, s, d)
```

```python
import functools
import math

import jax
import jax.numpy as jnp
import numpy as np
from jax import lax
from jax.experimental import pallas as pl
from jax.experimental.pallas import tpu as pltpu

F32 = jnp.float32
MXU_DTYPE = jnp.bfloat16

LANES = 128
SUBLANES = 8
VMEM_LIMIT_BYTES = 56 * 1024 * 1024

A_HEADS = 8
A_HEAD_DIM = 64
A_WIDTH = A_HEADS * A_HEAD_DIM
MOBA_BLOCK = 256
MOBA_TOPK = 3
REL_BUCKETS = 32
REL_MAX_DIST = 128
S5_GROUP = 16
S5_GROUPS = 32
S5_STATE = 64
S5_WIDTH = S5_GROUP * S5_GROUPS
C_HEADS = 8
C_HEAD_DIM = 128
C_WIDTH = C_HEADS * C_HEAD_DIM
C_CONV = 4
C_CHUNK = 64
DEPTH = 2
DN_ALPHA = (2 * DEPTH) ** 0.25
LN_EPS = 1e-5
RMS_EPS = 1e-6
L2_EPS = 1e-6
NEG_INF = -1e30
LOG2E = math.log2(math.e)


def _mx(t):
    return t.astype(MXU_DTYPE)


def _dot(a, b):
    return jnp.dot(_mx(a), _mx(b), preferred_element_type=F32)


def _dot_tn(a, b):
    return lax.dot_general(_mx(a), _mx(b), (((0,), (0,)), ((), ())), preferred_element_type=F32)


def _dot_nt(a, b):
    return lax.dot_general(_mx(a), _mx(b), (((1,), (1,)), ((), ())), preferred_element_type=F32)


def _sigmoid(t):
    return 1.0 / (1.0 + jnp.exp(-t))


def _layer_norm(r, g, b):
    mu = jnp.mean(r, axis=-1, keepdims=True)
    d = r - mu
    var = jnp.mean(d * d, axis=-1, keepdims=True)
    return d * lax.rsqrt(var + LN_EPS) * g + b


def _params(*sem):
    return pltpu.CompilerParams(dimension_semantics=sem, vmem_limit_bytes=VMEM_LIMIT_BYTES)


def _resident(shape):
    return pl.BlockSpec(shape, lambda *_: (0,) * len(shape), pipeline_mode=pl.Buffered(1))


PROJ_CHUNK = 1024


def _proj_kernel(x_ref, w_ref, o_ref):
    xm = _mx(x_ref[...])
    n = w_ref.shape[1]
    for c0 in range(0, n, PROJ_CHUNK):
        c1 = min(c0 + PROJ_CHUNK, n)
        o_ref[:, c0:c1] = jnp.dot(xm, w_ref[:, c0:c1], preferred_element_type=F32)


def _proj(x2d, w, tm):
    m, k = x2d.shape
    n = w.shape[1]
    return pl.pallas_call(
        _proj_kernel,
        grid=(m // tm,),
        in_specs=[pl.BlockSpec((tm, k), lambda i: (i, 0)), _resident((k, n))],
        out_specs=pl.BlockSpec((tm, n), lambda i: (i, 0)),
        out_shape=jax.ShapeDtypeStruct((m, n), F32),
        compiler_params=_params("parallel"),
        name="in_proj",
    )(x2d, w)


def _t5_bucket_np(dist):
    max_exact = REL_BUCKETS // 2
    n = np.maximum(dist, 0)
    nf = np.maximum(n, 1).astype(np.float32)
    large = max_exact + (np.log(nf / np.float32(max_exact)) / np.float32(math.log(REL_MAX_DIST / max_exact))
                         * np.float32(REL_BUCKETS - max_exact)).astype(np.int32)
    large = np.minimum(large, REL_BUCKETS - 1)
    return np.where(n < max_exact, n, large).astype(np.int32)


def _bias_kernel(tab_ref, bkt_ref, o_ref):
    h = pl.program_id(0)
    blk = bkt_ref.shape[-1]
    key = lax.broadcasted_iota(jnp.int32, (blk, blk), 0)
    qry = lax.broadcasted_iota(jnp.int32, (blk, blk), 1)
    far = tab_ref[h, REL_BUCKETS - 1]
    for t in range(2):
        bk = bkt_ref[t]
        acc = jnp.zeros((blk, blk), F32)
        for i in range(REL_BUCKETS):
            acc = jnp.where(bk == i, (tab_ref[h, i] - far) * LOG2E, acc)
        if t == 0:
            acc = jnp.where(qry >= key, acc, NEG_INF)
        o_ref[0, t] = acc


def _bias_tiles(table_t):
    heads = table_t.shape[0]
    offs = np.arange(MOBA_BLOCK)
    dist_own = offs[None, :] - offs[:, None]
    buckets = np.stack([_t5_bucket_np(dist_own), _t5_bucket_np(dist_own + MOBA_BLOCK)])
    return pl.pallas_call(
        _bias_kernel,
        grid=(heads,),
        in_specs=[pl.BlockSpec(memory_space=pltpu.SMEM),
                  pl.BlockSpec((2, MOBA_BLOCK, MOBA_BLOCK), lambda h: (0, 0, 0))],
        out_specs=pl.BlockSpec((1, 2, MOBA_BLOCK, MOBA_BLOCK), lambda h: (h, 0, 0, 0)),
        out_shape=jax.ShapeDtypeStruct((heads, 2, MOBA_BLOCK, MOBA_BLOCK), F32),
        compiler_params=_params("parallel"),
        name="t5_bias",
    )(table_t, jnp.asarray(buckets))


def _moba_kernel(q_ref, k_ref, v_ref, bias_ref, o_ref, kmean_ref, kb_ref, vt_ref,
                 st0_ref, st1_ref, wq_ref, acc_ref):
    st_refs = (st0_ref, st1_ref)
    qb = pl.program_id(2)
    blk = MOBA_BLOCK
    d = A_HEAD_DIM
    nh = LANES // d
    nb = k_ref.shape[1] // blk
    ng = q_ref.shape[2] // LANES
    groups = range(ng)

    @pl.when(qb == 0)
    def _():
        lane = lax.broadcasted_iota(jnp.int32, (blk, LANES - d), 1)
        for g in groups:
            gl = slice(g * LANES, (g + 1) * LANES)
            for j in range(nb + 1):
                onehot = jnp.where(lane == j, 1.0, 0.0).astype(MXU_DTYPE)
                for hh in range(nh):
                    kb_ref[g, hh, j, :, d:LANES] = onehot
            for j in range(nb):
                kj = k_ref[0, j * blk:(j + 1) * blk, gl]
                for hh in range(nh):
                    kb_ref[g, hh, j, :, 0:d] = _mx(kj[:, hh * d:(hh + 1) * d])
                kmean_ref[g, j:j + 1, :] = jnp.mean(kj, axis=0, keepdims=True)
                vt_ref[g, j] = _mx(v_ref[0, j * blk:(j + 1) * blk, gl].T)
            for hh in range(nh):
                kb_ref[g, hh, nb, :, 0:d] = jnp.zeros((blk, d), MXU_DTYPE)
            vt_ref[g, nb] = jnp.zeros((LANES, blk), MXU_DTYPE)

    blk_id = lax.broadcasted_iota(jnp.int32, (nb, blk), 0)
    row = lax.broadcasted_iota(jnp.int32, (LANES - d - nb, blk), 0)
    for g in groups:
        qt = (q_ref[0, :, g * LANES:(g + 1) * LANES] * (d ** -0.5)).T
        for hh in range(nh):
            gate = lax.dot_general(kmean_ref[g, :, hh * d:(hh + 1) * d], qt[hh * d:(hh + 1) * d],
                                   (((1,), (0,)), ((), ())),
                                   precision=lax.Precision.HIGHEST, preferred_element_type=F32)
            gt = jnp.where(blk_id < qb, gate, -jnp.inf)
            sel = jnp.zeros((nb, blk), F32)
            for _ in range(MOBA_TOPK):
                m = jnp.max(gt, axis=0, keepdims=True)
                is_m = jnp.logical_and(gt == m, gt > -jnp.inf)
                idx = jnp.min(jnp.where(is_m, blk_id, nb), axis=0, keepdims=True)
                pick = blk_id == idx
                sel = jnp.where(pick, 1.0, sel)
                gt = jnp.where(pick, -jnp.inf, gt)
            keep = jnp.logical_or(sel > 0.0, blk_id == qb)
            wq_ref[g, hh] = _mx(jnp.concatenate([qt[hh * d:(hh + 1) * d] * LOG2E, jnp.where(keep, 0.0, NEG_INF),
                                                 jnp.where(row == 0, NEG_INF, 0.0)], axis=0))

    def score_stage(slot, ja, jb, with_bias):
        cmaxes = []
        for g in groups:
            for hh in range(nh):
                keys = jnp.concatenate([kb_ref[g, hh, ja], kb_ref[g, hh, jb]], axis=0)
                st = jnp.dot(keys, wq_ref[g, hh], preferred_element_type=F32)
                if with_bias:
                    st = st + jnp.concatenate([bias_ref[g * nh + hh, w] for w in range(2)], axis=0)
                st_refs[slot][g, 0, :, hh * blk:(hh + 1) * blk] = st[:blk]
                st_refs[slot][g, 1, :, hh * blk:(hh + 1) * blk] = st[blk:]
                cmaxes.append(jnp.max(st, axis=0, keepdims=True))
        return tuple(cmaxes)

    def attend_stage(slot, cmaxes, stats, ja, jb):
        new = []
        for g in groups:
            for hh in range(nh):
                cs = slice(hh * blk, (hh + 1) * blk)
                m_new = cmaxes[g * nh + hh]
                if stats is not None:
                    m, l = stats[g * nh + hh]
                    m_new = jnp.maximum(m, m_new)
                ps = [jnp.exp2(st_refs[slot][g, w, :, cs] - m_new) for w in range(2)]
                l_new = jnp.sum(ps[0], axis=0, keepdims=True) + jnp.sum(ps[1], axis=0, keepdims=True)
                vt = jnp.concatenate([vt_ref[g, ja, hh * d:(hh + 1) * d, :],
                                      vt_ref[g, jb, hh * d:(hh + 1) * d, :]], axis=1)
                pv = jnp.dot(vt, jnp.concatenate([_mx(ps[0]), _mx(ps[1])], axis=0),
                             preferred_element_type=F32)
                if stats is None:
                    acc_ref[g, hh] = pv
                    new.append((m_new, l_new))
                else:
                    alpha = jnp.exp2(m - m_new)
                    acc_ref[g, hh] = acc_ref[g, hh] * alpha + pv
                    new.append((m_new, alpha * l + l_new))
        return tuple(new)

    nfar = qb - 1
    npairs = (nfar + 1) // 2

    def far_blocks(i):
        return jnp.where(2 * i < nfar, 2 * i, nb), jnp.where(2 * i + 1 < nfar, 2 * i + 1, nb)

    j_prev = jnp.where(qb > 0, qb - 1, nb)
    cmax0 = score_stage(0, qb, j_prev, True)
    cmax1 = score_stage(1, *far_blocks(0), False)
    init = (attend_stage(0, cmax0, None, qb, j_prev), cmax1)

    def trip(slot, k, carry):
        stats, cmax = carry
        cmax_next = score_stage(1 - slot, *far_blocks(k), False)
        return attend_stage(slot, cmax, stats, *far_blocks(k - 1)), cmax_next

    def trip_dyn(k, carry):
        return lax.cond(k % 2 == 1, functools.partial(trip, 1, k), functools.partial(trip, 0, k), carry)

    stats, _ = lax.fori_loop(1, npairs + 1, trip_dyn, init)
    for g in groups:
        out_t = jnp.concatenate([acc_ref[g, hh] / stats[g * nh + hh][1] for hh in range(nh)], axis=0)
        o_ref[0, :, g * LANES:(g + 1) * LANES] = out_t.T


MOBA_GROUPS = 4


def _moba(hp3d, bias):
    b, s, _ = hp3d.shape
    ng = MOBA_GROUPS
    wd = ng * LANES
    nsteps = A_WIDTH // wd
    hpp = LANES // A_HEAD_DIM
    nb = s // MOBA_BLOCK
    return pl.pallas_call(
        _moba_kernel,
        grid=(b, nsteps, nb),
        in_specs=[pl.BlockSpec((1, MOBA_BLOCK, wd), lambda bi, p, i: (bi, i, p)),
                  pl.BlockSpec((1, s, wd), lambda bi, p, i: (bi, 0, nsteps + p), pipeline_mode=pl.Buffered(1)),
                  pl.BlockSpec((1, s, wd), lambda bi, p, i: (bi, 0, 2 * nsteps + p), pipeline_mode=pl.Buffered(1)),
                  pl.BlockSpec((ng * hpp, 2, MOBA_BLOCK, MOBA_BLOCK), lambda bi, p, i: (p, 0, 0, 0))],
        out_specs=pl.BlockSpec((1, MOBA_BLOCK, wd), lambda bi, p, i: (bi, i, p)),
        out_shape=jax.ShapeDtypeStruct((b, s, A_WIDTH), F32),
        scratch_shapes=[pltpu.VMEM((ng, nb, LANES), F32),
                        pltpu.VMEM((ng, hpp, nb + 1, MOBA_BLOCK, LANES), MXU_DTYPE),
                        pltpu.VMEM((ng, nb + 1, LANES, MOBA_BLOCK), MXU_DTYPE),
                        pltpu.VMEM((ng, 2, MOBA_BLOCK, hpp * MOBA_BLOCK), F32),
                        pltpu.VMEM((ng, 2, MOBA_BLOCK, hpp * MOBA_BLOCK), F32),
                        pltpu.VMEM((ng, hpp, LANES, MOBA_BLOCK), MXU_DTYPE),
                        pltpu.VMEM((ng, hpp, A_HEAD_DIM, MOBA_BLOCK), F32)],
        compiler_params=_params("parallel", "parallel", "arbitrary"),
        name="moba",
    )(hp3d, hp3d, hp3d, bias)


S5_HALF_CH = S5_WIDTH // 2
S5_HALF_ST = (S5_GROUPS // 2) * S5_STATE
S5_TILE = 256
S5_NT = S5_HALF_ST // S5_TILE
S5_STEPS = 64
S5_PITCH = S5_STEPS + SUBLANES
S5_H_PITCH = S5_STEPS + SUBLANES // 2
S5_SLABS = 2 * S5_HALF_ST // LANES


def _s5_kernel(u0_ref, u_ref, wx_ref, wy_ref, a_ref, o_ref, st_ref, x0_scr, x1_scr, h0_scr, h1_scr):
    nb, steps, _ = u_ref.shape
    nseq = 2 * nb
    tile_slabs = S5_TILE // LANES
    nre = S5_SLABS // 2
    i = pl.program_id(0)

    def project_in_items(src_ref, x_scr):
        def item(m, part, n):
            def run():
                slab0 = (part * S5_HALF_ST + n * S5_TILE) // LANES
                for half in range(2):
                    lhs = jnp.concatenate(
                        [_mx(src_ref[bi, :, half * S5_HALF_CH + m * LANES:half * S5_HALF_CH + (m + 1) * LANES])
                         for bi in range(nb)], axis=0)
                    res = jnp.dot(lhs, wx_ref[part, n, half * LANES:(half + 1) * LANES, :],
                                  preferred_element_type=F32)
                    for bi in range(nb):
                        seq = 2 * bi + half
                        for sl in range(tile_slabs):
                            x_scr[slab0 + sl, seq * S5_PITCH:seq * S5_PITCH + steps, :] = (
                                res[bi * steps:(bi + 1) * steps, sl * LANES:(sl + 1) * LANES])
            return run
        return [item(m, part, n) for m in range(S5_NT // 2) for part in range(2) for n in (2 * m, 2 * m + 1)]

    def recur_items(x_scr, h_scr, nitems):
        state = {}

        def item(t0, t1):
            def run():
                a_re = [a_ref[0, :, s * LANES:(s + 1) * LANES] for s in range(nre)]
                a_im = [a_ref[1, :, s * LANES:(s + 1) * LANES] for s in range(nre)]
                if t0 == 0:
                    state["re"] = [st_ref[0, :, s * LANES:(s + 1) * LANES] for s in range(nre)]
                    state["im"] = [st_ref[1, :, s * LANES:(s + 1) * LANES] for s in range(nre)]
                h_re, h_im = state["re"], state["im"]
                for t in range(t0, t1):
                    rows_t = pl.ds(t, nseq, stride=S5_PITCH)
                    for s in range(nre):
                        hr, hi = h_re[s], h_im[s]
                        h_re[s] = a_re[s] * hr - a_im[s] * hi + x_scr[s, rows_t, :]
                        h_im[s] = a_re[s] * hi + a_im[s] * hr + x_scr[nre + s, rows_t, :]
                        h_scr[s, pl.ds(t, nseq, stride=S5_H_PITCH), :] = h_re[s]
                        h_scr[nre + s, pl.ds(t, nseq, stride=S5_H_PITCH), :] = h_im[s]
                if t1 == steps:
                    for s in range(nre):
                        st_ref[0, :, s * LANES:(s + 1) * LANES] = h_re[s]
                        st_ref[1, :, s * LANES:(s + 1) * LANES] = h_im[s]
            return run
        per = steps // nitems
        return [item(k * per, (k + 1) * per) for k in range(nitems)]

    def project_out_items(h_scr):
        def item(m, half):
            def run():
                acc = jnp.zeros((nb * steps, LANES), F32)
                for n in (2 * m, 2 * m + 1):
                    for part in range(2):
                        slab0 = (part * S5_HALF_ST + n * S5_TILE) // LANES
                        h_tile = jnp.concatenate(
                            [jnp.concatenate([h_scr[slab0 + sl,
                                                    (2 * bi + half) * S5_H_PITCH:(2 * bi + half) * S5_H_PITCH + steps, :]
                                              for sl in range(tile_slabs)], axis=1) for bi in range(nb)], axis=0)
                        acc = acc + jnp.dot(_mx(h_tile), wy_ref[part, n, :, half * LANES:(half + 1) * LANES],
                                            preferred_element_type=F32)
                c = half * S5_HALF_CH + m * LANES
                for bi in range(nb):
                    o_ref[bi, :, c:c + LANES] = acc[bi * steps:(bi + 1) * steps, :]
            return run
        return [item(m, half) for m in range(S5_NT // 2) for half in range(2)]

    @pl.when(i == 0)
    def _():
        st_ref[...] = jnp.zeros_like(st_ref)
        h1_scr[...] = jnp.zeros_like(h1_scr)
        for run in project_in_items(u0_ref, x0_scr):
            run()

    def step(x_cur, h_cur, x_nxt, h_prev):
        ins = project_in_items(u_ref, x_nxt)
        outs = project_out_items(h_prev)
        recs = recur_items(x_cur, h_cur, len(ins))
        for k, rec in enumerate(recs):
            rec()
            ins[k]()
            if k % 2 == 1:
                outs[k // 2]()

    @pl.when(i % 2 == 0)
    def _():
        step(x0_scr, h0_scr, x1_scr, h1_scr)

    @pl.when(i % 2 == 1)
    def _():
        step(x1_scr, h1_scr, x0_scr, h0_scr)


def _s5_weights(lam_re, lam_im, b_re, b_im, c_re, c_im, log_dt):
    lr = jnp.minimum(lam_re.astype(F32), -1e-4)
    li = lam_im.astype(F32)
    dt = jnp.exp(log_dt.astype(F32))[:, None]
    mag = jnp.exp(lr * dt)
    a_re = mag * jnp.cos(li * dt)
    a_im = mag * jnp.sin(li * dt)
    den = lr * lr + li * li
    coef_re = ((a_re - 1.0) * lr + a_im * li) / den
    coef_im = (a_im * lr - (a_re - 1.0) * li) / den
    br = b_re.astype(F32)
    bi = b_im.astype(F32)
    bb_re = coef_re[..., None] * br - coef_im[..., None] * bi
    bb_im = coef_re[..., None] * bi + coef_im[..., None] * br
    gh = S5_GROUPS // 2
    eye = jnp.eye(gh, dtype=F32)

    def b_dense(bb):
        t = bb.reshape(2, gh, S5_STATE, S5_GROUP)
        return jnp.einsum('fgph,gk->fghkp', t, eye).reshape(2, S5_HALF_CH, S5_HALF_ST)

    def c_dense(cc):
        t = cc.reshape(2, gh, S5_GROUP, S5_STATE)
        return jnp.einsum('fghp,gk->fgpkh', t, eye).reshape(2, S5_HALF_ST, S5_HALF_CH)

    def x_tiles(bd):
        tiles = []
        for n in range(S5_NT):
            c = (n // 2) * LANES
            tiles.append(jnp.concatenate([bd[0, c:c + LANES, n * S5_TILE:(n + 1) * S5_TILE],
                                          bd[1, c:c + LANES, n * S5_TILE:(n + 1) * S5_TILE]], axis=0))
        return jnp.stack(tiles)

    def y_tiles(cd):
        tiles = []
        for n in range(S5_NT):
            c = (n // 2) * LANES
            tiles.append(jnp.concatenate([cd[0, n * S5_TILE:(n + 1) * S5_TILE, c:c + LANES],
                                          cd[1, n * S5_TILE:(n + 1) * S5_TILE, c:c + LANES]], axis=1))
        return jnp.stack(tiles)

    wx = jnp.stack([x_tiles(b_dense(bb_re)), x_tiles(b_dense(bb_im))]).astype(MXU_DTYPE)
    wy = jnp.stack([y_tiles(c_dense(c_re.astype(F32))), y_tiles(c_dense(-c_im.astype(F32)))]).astype(MXU_DTYPE)
    return wx, wy, a_re.reshape(2, S5_HALF_ST), a_im.reshape(2, S5_HALF_ST)


def _s5_scan(hp3d, wx, wy, a_re2, a_im2):
    b, s, width = hp3d.shape
    nseq = 2 * b
    assert nseq == SUBLANES
    sel = jnp.arange(nseq) % 2
    a = jnp.stack([a_re2[sel], a_im2[sel]])
    scr = pltpu.VMEM((S5_SLABS, nseq * S5_PITCH, LANES), F32)
    hscr = pltpu.VMEM((S5_SLABS, nseq * S5_H_PITCH, LANES), F32)
    last = s // S5_STEPS - 1
    ucol = width // S5_WIDTH - 1
    return pl.pallas_call(
        _s5_kernel,
        grid=(last + 2,),
        in_specs=[pl.BlockSpec((b, S5_STEPS, S5_WIDTH), lambda i: (0, jnp.minimum(i, last), ucol)),
                  pl.BlockSpec((b, S5_STEPS, S5_WIDTH), lambda i: (0, jnp.minimum(i + 1, last), ucol)),
                  _resident(wx.shape), _resident(wy.shape), _resident(a.shape)],
        out_specs=pl.BlockSpec((b, S5_STEPS, S5_WIDTH), lambda i: (0, jnp.maximum(i - 1, 0), 0)),
        out_shape=jax.ShapeDtypeStruct((b, s, S5_WIDTH), F32),
        scratch_shapes=[pltpu.VMEM((2, nseq, S5_HALF_ST), F32), scr, scr, hscr, hscr],
        compiler_params=_params("arbitrary"),
        name="s5_scan",
    )(hp3d, hp3d, wx, wy, a)


MLP_CHUNK = 1024


def _tail_body(mix, x_ref, g1_ref, b1_ref, w1_ref, w2_ref, g2_ref, b2_ref, o_ref):
    x1 = _layer_norm(DN_ALPHA * x_ref[...] + mix, g1_ref[...], b1_ref[...])
    x1m = _mx(x1)
    acc = jnp.zeros(x1.shape, F32)
    for f in range(w1_ref.shape[1] // MLP_CHUNK):
        hid = jnp.dot(x1m, w1_ref[:, f * MLP_CHUNK:(f + 1) * MLP_CHUNK], preferred_element_type=F32)
        hid = jnp.square(jnp.maximum(hid, 0.0))
        acc = acc + jnp.dot(_mx(hid), w2_ref[f * MLP_CHUNK:(f + 1) * MLP_CHUNK, :],
                            preferred_element_type=F32)
    o_ref[...] = _layer_norm(DN_ALPHA * x1 + acc, g2_ref[...], b2_ref[...])


def _tail_even_kernel(attn_ref, y_ref, u_ref, dsk_ref, wg_ref, bg_ref, wo_ref, x_ref,
                      g1_ref, b1_ref, w1_ref, w2_ref, g2_ref, b2_ref, o_ref):
    y = y_ref[...] + dsk_ref[...] * u_ref[...]
    g = 0.5 * y * (1.0 + jnp.tanh(math.sqrt(2.0 / math.pi) * (y + 0.044715 * (y * y * y))))
    ssm = g * _sigmoid(jnp.dot(_mx(g), wg_ref[...], preferred_element_type=F32) + bg_ref[...])
    ka = attn_ref.shape[1]
    mix = (jnp.dot(_mx(attn_ref[...]), wo_ref[0:ka, :], preferred_element_type=F32)
           + jnp.dot(_mx(ssm), wo_ref[ka:, :], preferred_element_type=F32))
    _tail_body(mix, x_ref, g1_ref, b1_ref, w1_ref, w2_ref, g2_ref, b2_ref, o_ref)


def _tail_odd_kernel(a_ref, wo_ref, x_ref, g1_ref, b1_ref, w1_ref, w2_ref, g2_ref, b2_ref, o_ref):
    mix = jnp.dot(_mx(a_ref[...]), wo_ref[...], preferred_element_type=F32)
    _tail_body(mix, x_ref, g1_ref, b1_ref, w1_ref, w2_ref, g2_ref, b2_ref, o_ref)


def _row(v):
    return v.reshape(1, -1).astype(F32)


def _tail_common_specs(tm, d, dff):
    return [_resident((d, d)),
            pl.BlockSpec((tm, d), lambda i: (i, 0)),
            _resident((1, d)), _resident((1, d)),
            _resident((d, dff)), _resident((dff, d)),
            _resident((1, d)), _resident((1, d))]


def _tail_even(attn2d, y2d, hp2d, x2d, d_skip, w_glu, b_glu, w_out, g1, b1, w1, w2, g2, b2, tm):
    m, d = x2d.shape
    dff = w1.shape[1]
    ucol = hp2d.shape[1] // S5_WIDTH - 1
    return pl.pallas_call(
        _tail_even_kernel,
        grid=(m // tm,),
        in_specs=[pl.BlockSpec((tm, A_WIDTH), lambda i: (i, 0)),
                  pl.BlockSpec((tm, S5_WIDTH), lambda i: (i, 0)),
                  pl.BlockSpec((tm, S5_WIDTH), lambda i: (i, ucol)),
                  _resident((1, S5_WIDTH)), _resident((S5_WIDTH, S5_WIDTH)), _resident((1, S5_WIDTH))]
        + _tail_common_specs(tm, d, dff),
        out_specs=pl.BlockSpec((tm, d), lambda i: (i, 0)),
        out_shape=jax.ShapeDtypeStruct((m, d), F32),
        compiler_params=_params("parallel"),
        name="tail_even",
    )(attn2d, y2d, hp2d, _row(d_skip), _mx(w_glu), _row(b_glu), _mx(w_out), x2d,
      _row(g1), _row(b1), _mx(w1), _mx(w2), _row(g2), _row(b2))


def _tail_odd(a2d, x2d, w_out, g1, b1, w1, w2, g2, b2, tm):
    m, d = x2d.shape
    dff = w1.shape[1]
    return pl.pallas_call(
        _tail_odd_kernel,
        grid=(m // tm,),
        in_specs=[pl.BlockSpec((tm, a2d.shape[1]), lambda i: (i, 0))] + _tail_common_specs(tm, d, dff),
        out_specs=pl.BlockSpec((tm, d), lambda i: (i, 0)),
        out_shape=jax.ShapeDtypeStruct((m, d), F32),
        compiler_params=_params("parallel"),
        name="tail_odd",
    )(a2d, _mx(w_out), x2d, _row(g1), _row(b1), _mx(w1), _mx(w2), _row(g2), _row(b2))


GDN_TILE = 256
GDN_GATES_TILE = 512


def _softplus(t):
    return jnp.maximum(t, 0.0) + jnp.log1p(jnp.exp(-jnp.abs(t)))


def _gdn_prep_body(i, hg, q_ref, k_ref, v_ref, qp_ref, kp_ref, vp_ref, cwq_ref, cwk_ref, cwv_ref,
                   ba_ref, g_ref, gt_ref, u_ref, w_ref, qd_ref, kd_ref, intra_ref, edl_ref,
                   qx_ref, kx_ref, vx_ref, ql_ref, kl_ref, vl_ref):
    assert C_CONV == 4
    t, width = q_ref.shape[1], q_ref.shape[2]
    c = C_CHUNK
    dk = C_HEAD_DIM
    hp = width // dk
    groups = t // SUBLANES

    def conv_silu(main_ref, prev_ref, cw_ref, ext_ref, low_ref):
        x = main_ref[0]
        ext_ref[0:SUBLANES, :] = jnp.zeros((SUBLANES, width), F32)
        ext_ref[SUBLANES:2 * SUBLANES, :] = jnp.where(i > 0, prev_ref[0], 0.0)
        ext_ref[2 * SUBLANES:, :] = x
        ext_rows = t + SUBLANES

        def tap(j):
            return jnp.broadcast_to(cw_ref[j:j + 1, :], (SUBLANES, width))[None]

        def grouped(a):
            return a.reshape(a.shape[0] // SUBLANES, SUBLANES, width)

        x_ext = grouped(ext_ref[SUBLANES:, :])
        x_del = grouped(ext_ref[SUBLANES - 1:SUBLANES - 1 + ext_rows, :])
        low_ref[...] = (tap(1) * x_ext + tap(0) * x_del).reshape(ext_rows, width)
        y = (tap(3) * grouped(x) + tap(2) * x_del[1:]
             + grouped(low_ref[SUBLANES - 2:SUBLANES - 2 + t, :]))
        hy = 0.5 * y.reshape(t, width)
        return hy + hy * jnp.tanh(hy)

    def l2n(z, scale):
        return z * (lax.rsqrt(jnp.sum(z * z, axis=-1, keepdims=True) + L2_EPS) * scale)

    q_all = conv_silu(q_ref, qp_ref, cwq_ref, qx_ref, ql_ref)
    k_all = conv_silu(k_ref, kp_ref, cwk_ref, kx_ref, kl_ref)
    v_all = conv_silu(v_ref, vp_ref, cwv_ref, vx_ref, vl_ref)
    heads = [slice(hh * dk, (hh + 1) * dk) for hh in range(hp)]
    qs = [l2n(q_all[:, cs], dk ** -0.5) for cs in heads]
    ks = [l2n(k_all[:, cs], 1.0) for cs in heads]
    vs = [v_all[:, cs] for cs in heads]

    ba = ba_ref[0]
    lane = lax.broadcasted_iota(jnp.int32, (t, LANES), 1)
    betas, g_cols, g_rows = [], [], []
    for hh in range(hp):
        h = hg * hp + hh
        betas.append(_sigmoid(jnp.sum(jnp.where(lane == h, ba, 0.0), axis=1, keepdims=True)))
        g_cols.append(jnp.sum(jnp.where(lane == C_HEADS + h, g_ref[0], 0.0), axis=1, keepdims=True))
        g_rows.append(gt_ref[0, pl.ds(C_HEADS + h, 1), :])

    ii = lax.broadcasted_iota(jnp.int32, (c, c), 0)
    jj = lax.broadcasted_iota(jnp.int32, (c, c), 1)
    low = ii >= jj
    units = [(hh, slice(ch * c, (ch + 1) * c)) for hh in range(hp) for ch in range(t // c)]
    gc_cols = [jnp.sum(jnp.where(low, g_rows[hh][:, r], 0.0), axis=1, keepdims=True) for hh, r in units]
    gc_rows = [jnp.sum(jnp.where(ii <= jj, g_cols[hh][r], 0.0), axis=0, keepdims=True) for hh, r in units]
    decays = [jnp.where(low, jnp.exp(jnp.where(low, a - b, 0.0)), 0.0) for a, b in zip(gc_cols, gc_rows)]
    beta_bs = [jnp.broadcast_to(betas[hh][r], (c, dk)) for hh, r in units]
    kbs = [ks[hh][r] * bb for bb, (hh, r) in zip(beta_bs, units)]
    kks = [_dot_nt(kb, ks[hh][r]) for kb, (hh, r) in zip(kbs, units)]
    qks = [_dot_nt(qs[hh][r], ks[hh][r]) for hh, r in units]
    pws = [jnp.where(ii > jj, -(kk * dec), 0.0) for kk, dec in zip(kks, decays)]
    tms = pws
    for _ in range(int(math.log2(c)) - 1):
        pbs = [_mx(pw) for pw in pws]
        pws = [_dot(pb, pb) for pb in pbs]
        pbs = [_mx(pw) for pw in pws]
        tms = [tm + pw + _dot(tm, pb) for tm, pw, pb in zip(tms, pws, pbs)]
    egcs = [jnp.broadcast_to(jnp.exp(a), (c, dk)) for a in gc_cols]
    vbs = [vs[hh][r] * bb for bb, (hh, r) in zip(beta_bs, units)]
    kbes = [kb * e for kb, e in zip(kbs, egcs)]
    us = [vb + _dot(tm, vb) for tm, vb in zip(tms, vbs)]
    ws = [kbe + _dot(tm, kbe) for tm, kbe in zip(tms, kbes)]
    for n, (hh, r) in enumerate(units):
        gl = gc_cols[n][c - 1:c, :]
        cs = heads[hh]
        ch = r.start // c
        u_ref[0, r, cs] = us[n]
        w_ref[0, r, cs] = _mx(ws[n])
        qd_ref[0, r, cs] = _mx(qs[hh][r] * egcs[n])
        kd_ref[0, r, cs] = _mx(ks[hh][r] * jnp.exp(gl - gc_cols[n]))
        intra_ref[0, hh, r, :] = _mx(qks[n] * decays[n])
        edl_ref[0, hh, ch:ch + 1, :] = jnp.broadcast_to(jnp.exp(gl), (1, LANES))


def _gdn_gates_kernel(ba_ref, alog_ref, dtb_ref, g_ref, gt_ref):
    g_all = -jnp.exp(alog_ref[...]) * _softplus(ba_ref[0] + dtb_ref[...])
    g_ref[0] = g_all
    gt_ref[0] = g_all.T


def _gdn_gates(hp3d, alog_row, dtb_row):
    b, s, _ = hp3d.shape
    t = GDN_GATES_TILE
    ba_col = 4 * C_WIDTH // LANES
    return pl.pallas_call(
        _gdn_gates_kernel,
        grid=(b, s // t),
        in_specs=[pl.BlockSpec((1, t, LANES), lambda bi, i: (bi, i, ba_col)),
                  pl.BlockSpec((1, LANES), lambda bi, i: (0, 0)),
                  pl.BlockSpec((1, LANES), lambda bi, i: (0, 0))],
        out_specs=[pl.BlockSpec((1, t, LANES), lambda bi, i: (bi, i, 0)),
                   pl.BlockSpec((1, LANES, t), lambda bi, i: (bi, 0, i))],
        out_shape=[jax.ShapeDtypeStruct((b, s, LANES), F32), jax.ShapeDtypeStruct((b, LANES, s), F32)],
        compiler_params=_params("parallel", "parallel"),
        name="gdn_gates",
    )(hp3d, alog_row, dtb_row)


def _gdn_scan_body(chunk0, u_ref, w_ref, qd_ref, kd_ref, intra_ref, edl_ref, gate_ref, nw_ref, o_ref, st_ref):
    c = C_CHUNK
    dk = C_HEAD_DIM
    t = u_ref.shape[1]
    nw = nw_ref[...]
    heads = [slice(h * dk, (h + 1) * dk) for h in range(C_HEADS)]
    for ch in range(t // c):
        r = slice(ch * c, (ch + 1) * c)
        sts = [st_ref[h] for h in range(C_HEADS)]
        wss = [_dot(jnp.concatenate([w_ref[0, r, cs], qd_ref[0, r, cs]], axis=0), st)
               for cs, st in zip(heads, sts)]
        v_news = [u_ref[0, r, cs] - ws[:c] for cs, ws in zip(heads, wss)]
        kvs = [_dot_tn(kd_ref[0, r, cs], v_new) for cs, v_new in zip(heads, v_news)]
        os_ = [ws[c:] + _dot(intra_ref[0, h, r, :], v_new) for h, (ws, v_new) in enumerate(zip(wss, v_news))]
        for h, cs in enumerate(heads):
            st_ref[h] = sts[h] * edl_ref[0, h, pl.ds(chunk0 + ch, 1), 0:1] + kvs[h]
            o = os_[h]
            o = o * lax.rsqrt(jnp.mean(o * o, axis=-1, keepdims=True) + RMS_EPS) * nw
            hg = 0.5 * gate_ref[0, r, cs]
            o_ref[0, r, cs] = o * (hg + hg * jnp.tanh(hg))


def _gdn_kernel(q_ref, k_ref, v_ref, qp_ref, kp_ref, vp_ref, cwq_ref, cwk_ref, cwv_ref, ba_ref, g_ref, gt_ref,
                gate_ref, nw_ref, o_ref, st_ref, qx_ref, kx_ref, vx_ref, ql_ref, kl_ref, vl_ref, *bufs):
    j = pl.program_id(1)
    last = pl.num_programs(1) - 2
    cur, nxt = bufs[:len(bufs) // 2], bufs[len(bufs) // 2:]

    @pl.when(j == 0)
    def _():
        st_ref[...] = jnp.zeros_like(st_ref)
        for ref in bufs:
            ref[...] = jnp.zeros_like(ref)

    _gdn_prep_body(jnp.minimum(j, last), 0, q_ref, k_ref, v_ref, qp_ref, kp_ref, vp_ref, cwq_ref, cwk_ref, cwv_ref,
                   ba_ref, g_ref, gt_ref, *nxt, qx_ref, kx_ref, vx_ref, ql_ref, kl_ref, vl_ref)
    _gdn_scan_body(0, *cur, gate_ref, nw_ref, o_ref, st_ref)
    for dst, src in zip(cur, nxt):
        dst[...] = src[...]


def _gdn(hp3d, conv_w, alog_row, dtb_row, norm_w):
    b, s, _ = hp3d.shape
    ba_col = 4 * C_WIDTH // LANES
    t = GDN_TILE
    nt = s // t
    per = t // SUBLANES
    wd = C_WIDTH
    nc = t // C_CHUNK
    g, gt = _gdn_gates(hp3d, alog_row, dtb_row)

    def prep_tile(j):
        return jnp.minimum(j, nt - 1)

    def scan_tile(j):
        return jnp.maximum(j - 1, 0)

    def main(off):
        return pl.BlockSpec((1, t, wd), lambda bi, j: (bi, prep_tile(j), off))

    def prev(off):
        return pl.BlockSpec((1, SUBLANES, wd), lambda bi, j: (bi, jnp.maximum(prep_tile(j) * per - 1, 0), off))

    def cw(off):
        return pl.BlockSpec((C_CONV, wd), lambda bi, j: (0, off))

    bufs = [pltpu.VMEM((1, t, wd), F32)] + [pltpu.VMEM((1, t, wd), MXU_DTYPE)] * 3 + [
        pltpu.VMEM((1, C_HEADS, t, C_CHUNK), MXU_DTYPE), pltpu.VMEM((1, C_HEADS, max(nc, SUBLANES), LANES), F32)]
    return pl.pallas_call(
        _gdn_kernel,
        grid=(b, nt + 1),
        in_specs=[main(0), main(1), main(2), prev(0), prev(1), prev(2), cw(0), cw(1), cw(2),
                  pl.BlockSpec((1, t, LANES), lambda bi, j: (bi, prep_tile(j), ba_col)),
                  pl.BlockSpec((1, t, LANES), lambda bi, j: (bi, prep_tile(j), 0)),
                  pl.BlockSpec((1, LANES, t), lambda bi, j: (bi, 0, prep_tile(j))),
                  pl.BlockSpec((1, t, wd), lambda bi, j: (bi, scan_tile(j), 3)),
                  pl.BlockSpec((1, C_HEAD_DIM), lambda bi, j: (0, 0))],
        out_specs=pl.BlockSpec((1, t, wd), lambda bi, j: (bi, scan_tile(j), 0)),
        out_shape=jax.ShapeDtypeStruct((b, s, wd), F32),
        scratch_shapes=[pltpu.VMEM((C_HEADS, C_HEAD_DIM, C_HEAD_DIM), F32)]
        + [pltpu.VMEM((t + 2 * SUBLANES, wd), F32)] * 3 + [pltpu.VMEM((t + SUBLANES, wd), F32)] * 3
        + bufs + bufs,
        compiler_params=_params("parallel", "arbitrary"),
        name="gdn",
    )(hp3d, hp3d, hp3d, hp3d, hp3d, hp3d, conv_w, conv_w, conv_w, hp3d, g, gt, hp3d, _row(norm_w))


ROW_TILE = 512


def kernel(x, rel_table, ev_w_in, ev_w_out, s5_lambda_re, s5_lambda_im, s5_b_re, s5_b_im, s5_c_re, s5_c_im, s5_d, s5_log_dt, s5_w_glu, s5_b_glu, od_w_in, od_conv_w, od_a_log, od_dt_bias, od_norm_w, od_w_out, ln_mix_g, ln_mix_b, mlp_w1, mlp_w2, ln_ffn_g, ln_ffn_b):
    bsz, s, d = x.shape
    m = bsz * s
    x2d = x.reshape(m, d)
    table_t = rel_table.T.astype(F32)
    bias = _bias_tiles(table_t)
    for layer in range(DEPTH):
        i = layer // 2
        if layer % 2 == 0:
            hp = _proj(x2d, _mx(ev_w_in[i]), ROW_TILE)
            hp3d = hp.reshape(bsz, s, hp.shape[1])
            attn = _moba(hp3d, bias)
            wx, wy, a_re2, a_im2 = _s5_weights(s5_lambda_re[i], s5_lambda_im[i], s5_b_re[i], s5_b_im[i],
                                               s5_c_re[i], s5_c_im[i], s5_log_dt[i])
            y = _s5_scan(hp3d, wx, wy, a_re2, a_im2)
            x2d = _tail_even(attn.reshape(m, A_WIDTH), y.reshape(m, S5_WIDTH), hp, x2d,
                             s5_d[i], s5_w_glu[i], s5_b_glu[i], ev_w_out[i],
                             ln_mix_g[layer], ln_mix_b[layer], mlp_w1[layer], mlp_w2[layer],
                             ln_ffn_g[layer], ln_ffn_b[layer], ROW_TILE)
        else:
            w_in = od_w_in[i]
            w_all = jnp.pad(w_in, ((0, 0), (0, LANES - 2 * C_HEADS)))
            hp = _proj(x2d, _mx(w_all), ROW_TILE)
            hp3d = hp.reshape(bsz, s, 4 * C_WIDTH + LANES)
            pad8 = jnp.zeros((C_HEADS,), F32)
            padr = jnp.zeros((LANES - 2 * C_HEADS,), F32)
            alog_row = jnp.concatenate([pad8, od_a_log[i].astype(F32), padr]).reshape(1, LANES)
            dtb_row = jnp.concatenate([pad8, od_dt_bias[i].astype(F32), padr]).reshape(1, LANES)
            o = _gdn(hp3d, od_conv_w[i].astype(F32), alog_row, dtb_row, od_norm_w[i])
            x2d = _tail_odd(o.reshape(m, C_WIDTH), x2d, od_w_out[i],
                            ln_mix_g[layer], ln_mix_b[layer], mlp_w1[layer], mlp_w2[layer],
                            ln_ffn_g[layer], ln_ffn_b[layer], ROW_TILE)
    return x2d.reshape(bsz, s, d)
```

```python
import functools
import math

import jax
import jax.numpy as jnp
import numpy as np
from jax import lax
from jax.experimental import pallas as pl
from jax.experimental.pallas import tpu as pltpu

F32 = jnp.float32
MXU_DTYPE = jnp.bfloat16

LANES = 128
SUBLANES = 8
VMEM_LIMIT_BYTES = 56 * 1024 * 1024

A_HEADS = 8
A_HEAD_DIM = 64
A_WIDTH = A_HEADS * A_HEAD_DIM
MOBA_BLOCK = 256
MOBA_TOPK = 3
REL_BUCKETS = 32
REL_MAX_DIST = 128
S5_GROUP = 16
S5_GROUPS = 32
S5_STATE = 64
S5_WIDTH = S5_GROUP * S5_GROUPS
C_HEADS = 8
C_HEAD_DIM = 128
C_WIDTH = C_HEADS * C_HEAD_DIM
C_CONV = 4
C_CHUNK = 64
DEPTH = 2
DN_ALPHA = (2 * DEPTH) ** 0.25
LN_EPS = 1e-5
RMS_EPS = 1e-6
L2_EPS = 1e-6
NEG_INF = -1e30
LOG2E = math.log2(math.e)


def _mx(t):
    return t.astype(MXU_DTYPE)


def _dot(a, b):
    return jnp.dot(_mx(a), _mx(b), preferred_element_type=F32)


def _dot_tn(a, b):
    return lax.dot_general(_mx(a), _mx(b), (((0,), (0,)), ((), ())), preferred_element_type=F32)


def _dot_nt(a, b):
    return lax.dot_general(_mx(a), _mx(b), (((1,), (1,)), ((), ())), preferred_element_type=F32)


def _sigmoid(t):
    return 1.0 / (1.0 + jnp.exp(-t))


def _layer_norm(r, g, b):
    mu = jnp.mean(r, axis=-1, keepdims=True)
    d = r - mu
    var = jnp.mean(d * d, axis=-1, keepdims=True)
    return d * lax.rsqrt(var + LN_EPS) * g + b


def _params(*sem):
    return pltpu.CompilerParams(dimension_semantics=sem, vmem_limit_bytes=VMEM_LIMIT_BYTES)


def _resident(shape):
    return pl.BlockSpec(shape, lambda *_: (0,) * len(shape), pipeline_mode=pl.Buffered(1))


PROJ_CHUNK = 1024


def _proj_kernel(x_ref, w_ref, o_ref):
    xm = _mx(x_ref[...])
    n = w_ref.shape[1]
    for c0 in range(0, n, PROJ_CHUNK):
        c1 = min(c0 + PROJ_CHUNK, n)
        o_ref[:, c0:c1] = jnp.dot(xm, w_ref[:, c0:c1], preferred_element_type=F32)


def _proj(x2d, w, tm):
    m, k = x2d.shape
    n = w.shape[1]
    return pl.pallas_call(
        _proj_kernel,
        grid=(m // tm,),
        in_specs=[pl.BlockSpec((tm, k), lambda i: (i, 0)), _resident((k, n))],
        out_specs=pl.BlockSpec((tm, n), lambda i: (i, 0)),
        out_shape=jax.ShapeDtypeStruct((m, n), F32),
        compiler_params=_params("parallel"),
        name="in_proj",
    )(x2d, w)


def _t5_bucket_np(dist):
    max_exact = REL_BUCKETS // 2
    n = np.maximum(dist, 0)
    nf = np.maximum(n, 1).astype(np.float32)
    large = max_exact + (np.log(nf / np.float32(max_exact)) / np.float32(math.log(REL_MAX_DIST / max_exact))
                         * np.float32(REL_BUCKETS - max_exact)).astype(np.int32)
    large = np.minimum(large, REL_BUCKETS - 1)
    return np.where(n < max_exact, n, large).astype(np.int32)


def _bias_kernel(tab_ref, bkt_ref, o_ref):
    h = pl.program_id(0)
    blk = bkt_ref.shape[-1]
    key = lax.broadcasted_iota(jnp.int32, (blk, blk), 0)
    qry = lax.broadcasted_iota(jnp.int32, (blk, blk), 1)
    far = tab_ref[h, REL_BUCKETS - 1]
    for t in range(2):
        bk = bkt_ref[t]
        acc = jnp.zeros((blk, blk), F32)
        for i in range(REL_BUCKETS):
            acc = jnp.where(bk == i, (tab_ref[h, i] - far) * LOG2E, acc)
        if t == 0:
            acc = jnp.where(qry >= key, acc, NEG_INF)
        o_ref[0, t] = acc


def _bias_tiles(table_t):
    heads = table_t.shape[0]
    offs = np.arange(MOBA_BLOCK)
    dist_own = offs[None, :] - offs[:, None]
    buckets = np.stack([_t5_bucket_np(dist_own), _t5_bucket_np(dist_own + MOBA_BLOCK)])
    return pl.pallas_call(
        _bias_kernel,
        grid=(heads,),
        in_specs=[pl.BlockSpec(memory_space=pltpu.SMEM),
                  pl.BlockSpec((2, MOBA_BLOCK, MOBA_BLOCK), lambda h: (0, 0, 0))],
        out_specs=pl.BlockSpec((1, 2, MOBA_BLOCK, MOBA_BLOCK), lambda h: (h, 0, 0, 0)),
        out_shape=jax.ShapeDtypeStruct((heads, 2, MOBA_BLOCK, MOBA_BLOCK), F32),
        compiler_params=_params("parallel"),
        name="t5_bias",
    )(table_t, jnp.asarray(buckets))


def _moba_kernel(q_ref, k_ref, v_ref, bias_ref, o_ref, kmean_ref, kb_ref, vt_ref,
                 st0_ref, st1_ref, wq_ref, acc_ref):
    st_refs = (st0_ref, st1_ref)
    qb = pl.program_id(2)
    blk = MOBA_BLOCK
    d = A_HEAD_DIM
    nh = LANES // d
    nb = k_ref.shape[1] // blk
    ng = q_ref.shape[2] // LANES
    groups = range(ng)

    @pl.when(qb == 0)
    def _():
        lane = lax.broadcasted_iota(jnp.int32, (blk, LANES - d), 1)
        for g in groups:
            gl = slice(g * LANES, (g + 1) * LANES)
            for j in range(nb + 1):
                onehot = jnp.where(lane == j, 1.0, 0.0).astype(MXU_DTYPE)
                for hh in range(nh):
                    kb_ref[g, hh, j, :, d:LANES] = onehot
            for j in range(nb):
                kj = k_ref[0, j * blk:(j + 1) * blk, gl]
                for hh in range(nh):
                    kb_ref[g, hh, j, :, 0:d] = _mx(kj[:, hh * d:(hh + 1) * d])
                kmean_ref[g, j:j + 1, :] = jnp.mean(kj, axis=0, keepdims=True)
                vt_ref[g, j] = _mx(v_ref[0, j * blk:(j + 1) * blk, gl].T)
            for hh in range(nh):
                kb_ref[g, hh, nb, :, 0:d] = jnp.zeros((blk, d), MXU_DTYPE)
            vt_ref[g, nb] = jnp.zeros((LANES, blk), MXU_DTYPE)

    blk_id = lax.broadcasted_iota(jnp.int32, (nb, blk), 0)
    row = lax.broadcasted_iota(jnp.int32, (LANES - d - nb, blk), 0)
    for g in groups:
        qt = (q_ref[0, :, g * LANES:(g + 1) * LANES] * (d ** -0.5)).T
        for hh in range(nh):
            gate = lax.dot_general(kmean_ref[g, :, hh * d:(hh + 1) * d], qt[hh * d:(hh + 1) * d],
                                   (((1,), (0,)), ((), ())),
                                   precision=lax.Precision.HIGHEST, preferred_element_type=F32)
            gt = jnp.where(blk_id < qb, gate, -jnp.inf)
            sel = jnp.zeros((nb, blk), F32)
            for _ in range(MOBA_TOPK):
                m = jnp.max(gt, axis=0, keepdims=True)
                is_m = jnp.logical_and(gt == m, gt > -jnp.inf)
                idx = jnp.min(jnp.where(is_m, blk_id, nb), axis=0, keepdims=True)
                pick = blk_id == idx
                sel = jnp.where(pick, 1.0, sel)
                gt = jnp.where(pick, -jnp.inf, gt)
            keep = jnp.logical_or(sel > 0.0, blk_id == qb)
            wq_ref[g, hh] = _mx(jnp.concatenate([qt[hh * d:(hh + 1) * d] * LOG2E, jnp.where(keep, 0.0, NEG_INF),
                                                 jnp.where(row == 0, NEG_INF, 0.0)], axis=0))

    def score_stage(slot, ja, jb, with_bias):
        cmaxes = []
        for g in groups:
            for hh in range(nh):
                keys = jnp.concatenate([kb_ref[g, hh, ja], kb_ref[g, hh, jb]], axis=0)
                st = jnp.dot(keys, wq_ref[g, hh], preferred_element_type=F32)
                if with_bias:
                    st = st + jnp.concatenate([bias_ref[g * nh + hh, w] for w in range(2)], axis=0)
                st_refs[slot][g, 0, :, hh * blk:(hh + 1) * blk] = st[:blk]
                st_refs[slot][g, 1, :, hh * blk:(hh + 1) * blk] = st[blk:]
                cmaxes.append(jnp.max(st, axis=0, keepdims=True))
        return tuple(cmaxes)

    def attend_stage(slot, cmaxes, stats, ja, jb):
        new = []
        for g in groups:
            for hh in range(nh):
                cs = slice(hh * blk, (hh + 1) * blk)
                m_new = cmaxes[g * nh + hh]
                if stats is not None:
                    m, l = stats[g * nh + hh]
                    m_new = jnp.maximum(m, m_new)
                ps = [jnp.exp2(st_refs[slot][g, w, :, cs] - m_new) for w in range(2)]
                l_new = jnp.sum(ps[0], axis=0, keepdims=True) + jnp.sum(ps[1], axis=0, keepdims=True)
                vt = jnp.concatenate([vt_ref[g, ja, hh * d:(hh + 1) * d, :],
                                      vt_ref[g, jb, hh * d:(hh + 1) * d, :]], axis=1)
                pv = jnp.dot(vt, jnp.concatenate([_mx(ps[0]), _mx(ps[1])], axis=0),
                             preferred_element_type=F32)
                if stats is None:
                    acc_ref[g, hh] = pv
                    new.append((m_new, l_new))
                else:
                    alpha = jnp.exp2(m - m_new)
                    acc_ref[g, hh] = acc_ref[g, hh] * alpha + pv
                    new.append((m_new, alpha * l + l_new))
        return tuple(new)

    nfar = qb - 1
    npairs = (nfar + 1) // 2

    def far_blocks(i):
        return jnp.where(2 * i < nfar, 2 * i, nb), jnp.where(2 * i + 1 < nfar, 2 * i + 1, nb)

    j_prev = jnp.where(qb > 0, qb - 1, nb)
    cmax0 = score_stage(0, qb, j_prev, True)
    cmax1 = score_stage(1, *far_blocks(0), False)
    init = (attend_stage(0, cmax0, None, qb, j_prev), cmax1)

    def trip(slot, k, carry):
        stats, cmax = carry
        cmax_next = score_stage(1 - slot, *far_blocks(k), False)
        return attend_stage(slot, cmax, stats, *far_blocks(k - 1)), cmax_next

    def trip_dyn(k, carry):
        return lax.cond(k % 2 == 1, functools.partial(trip, 1, k), functools.partial(trip, 0, k), carry)

    stats, _ = lax.fori_loop(1, npairs + 1, trip_dyn, init)
    for g in groups:
        out_t = jnp.concatenate([acc_ref[g, hh] / stats[g * nh + hh][1] for hh in range(nh)], axis=0)
        o_ref[0, :, g * LANES:(g + 1) * LANES] = out_t.T


MOBA_GROUPS = 4


def _moba(hp3d, bias):
    b, s, _ = hp3d.shape
    ng = MOBA_GROUPS
    wd = ng * LANES
    nsteps = A_WIDTH // wd
    hpp = LANES // A_HEAD_DIM
    nb = s // MOBA_BLOCK
    return pl.pallas_call(
        _moba_kernel,
        grid=(b, nsteps, nb),
        in_specs=[pl.BlockSpec((1, MOBA_BLOCK, wd), lambda bi, p, i: (bi, i, p)),
                  pl.BlockSpec((1, s, wd), lambda bi, p, i: (bi, 0, nsteps + p), pipeline_mode=pl.Buffered(1)),
                  pl.BlockSpec((1, s, wd), lambda bi, p, i: (bi, 0, 2 * nsteps + p), pipeline_mode=pl.Buffered(1)),
                  pl.BlockSpec((ng * hpp, 2, MOBA_BLOCK, MOBA_BLOCK), lambda bi, p, i: (p, 0, 0, 0))],
        out_specs=pl.BlockSpec((1, MOBA_BLOCK, wd), lambda bi, p, i: (bi, i, p)),
        out_shape=jax.ShapeDtypeStruct((b, s, A_WIDTH), F32),
        scratch_shapes=[pltpu.VMEM((ng, nb, LANES), F32),
                        pltpu.VMEM((ng, hpp, nb + 1, MOBA_BLOCK, LANES), MXU_DTYPE),
                        pltpu.VMEM((ng, nb + 1, LANES, MOBA_BLOCK), MXU_DTYPE),
                        pltpu.VMEM((ng, 2, MOBA_BLOCK, hpp * MOBA_BLOCK), F32),
                        pltpu.VMEM((ng, 2, MOBA_BLOCK, hpp * MOBA_BLOCK), F32),
                        pltpu.VMEM((ng, hpp, LANES, MOBA_BLOCK), MXU_DTYPE),
                        pltpu.VMEM((ng, hpp, A_HEAD_DIM, MOBA_BLOCK), F32)],
        compiler_params=_params("parallel", "parallel", "arbitrary"),
        name="moba",
    )(hp3d, hp3d, hp3d, bias)


S5_HALF_CH = S5_WIDTH // 2
S5_HALF_ST = (S5_GROUPS // 2) * S5_STATE
S5_TILE = 256
S5_NT = S5_HALF_ST // S5_TILE
S5_STEPS = 64
S5_PITCH = S5_STEPS + SUBLANES
S5_H_PITCH = S5_STEPS + SUBLANES // 2
S5_SLABS = 2 * S5_HALF_ST // LANES


def _s5_kernel(u0_ref, u_ref, wx_ref, wy_ref, a_ref, o_ref, st_ref, x0_scr, x1_scr, h0_scr, h1_scr):
    nb, steps, _ = u_ref.shape
    nseq = 2 * nb
    tile_slabs = S5_TILE // LANES
    nre = S5_SLABS // 2
    i = pl.program_id(0)

    def project_in_items(src_ref, x_scr):
        def item(m, part, n):
            def run():
                slab0 = (part * S5_HALF_ST + n * S5_TILE) // LANES
                for half in range(2):
                    lhs = jnp.concatenate(
                        [_mx(src_ref[bi, :, half * S5_HALF_CH + m * LANES:half * S5_HALF_CH + (m + 1) * LANES])
                         for bi in range(nb)], axis=0)
                    res = jnp.dot(lhs, wx_ref[part, n, half * LANES:(half + 1) * LANES, :],
                                  preferred_element_type=F32)
                    for bi in range(nb):
                        seq = 2 * bi + half
                        for sl in range(tile_slabs):
                            x_scr[slab0 + sl, seq * S5_PITCH:seq * S5_PITCH + steps, :] = (
                                res[bi * steps:(bi + 1) * steps, sl * LANES:(sl + 1) * LANES])
            return run
        return [item(m, part, n) for m in range(S5_NT // 2) for part in range(2) for n in (2 * m, 2 * m + 1)]

    def recur_items(x_scr, h_scr, nitems):
        state = {}

        def item(t0, t1):
            def run():
                a_re = [a_ref[0, :, s * LANES:(s + 1) * LANES] for s in range(nre)]
                a_im = [a_ref[1, :, s * LANES:(s + 1) * LANES] for s in range(nre)]
                if t0 == 0:
                    state["re"] = [st_ref[0, :, s * LANES:(s + 1) * LANES] for s in range(nre)]
                    state["im"] = [st_ref[1, :, s * LANES:(s + 1) * LANES] for s in range(nre)]
                h_re, h_im = state["re"], state["im"]
                for t in range(t0, t1):
                    rows_t = pl.ds(t, nseq, stride=S5_PITCH)
                    for s in range(nre):
                        hr, hi = h_re[s], h_im[s]
                        h_re[s] = a_re[s] * hr - a_im[s] * hi + x_scr[s, rows_t, :]
                        h_im[s] = a_re[s] * hi + a_im[s] * hr + x_scr[nre + s, rows_t, :]
                        h_scr[s, pl.ds(t, nseq, stride=S5_H_PITCH), :] = h_re[s]
                        h_scr[nre + s, pl.ds(t, nseq, stride=S5_H_PITCH), :] = h_im[s]
                if t1 == steps:
                    for s in range(nre):
                        st_ref[0, :, s * LANES:(s + 1) * LANES] = h_re[s]
                        st_ref[1, :, s * LANES:(s + 1) * LANES] = h_im[s]
            return run
        per = steps // nitems
        return [item(k * per, (k + 1) * per) for k in range(nitems)]

    def project_out_items(h_scr):
        def item(m, half):
            def run():
                acc = jnp.zeros((nb * steps, LANES), F32)
                for n in (2 * m, 2 * m + 1):
                    for part in range(2):
                        slab0 = (part * S5_HALF_ST + n * S5_TILE) // LANES
                        h_tile = jnp.concatenate(
                            [jnp.concatenate([h_scr[slab0 + sl,
                                                    (2 * bi + half) * S5_H_PITCH:(2 * bi + half) * S5_H_PITCH + steps, :]
                                              for sl in range(tile_slabs)], axis=1) for bi in range(nb)], axis=0)
                        acc = acc + jnp.dot(_mx(h_tile), wy_ref[part, n, :, half * LANES:(half + 1) * LANES],
                                            preferred_element_type=F32)
                c = half * S5_HALF_CH + m * LANES
                for bi in range(nb):
                    o_ref[bi, :, c:c + LANES] = acc[bi * steps:(bi + 1) * steps, :]
            return run
        return [item(m, half) for m in range(S5_NT // 2) for half in range(2)]

    @pl.when(i == 0)
    def _():
        st_ref[...] = jnp.zeros_like(st_ref)
        h1_scr[...] = jnp.zeros_like(h1_scr)
        for run in project_in_items(u0_ref, x0_scr):
            run()

    def step(x_cur, h_cur, x_nxt, h_prev):
        ins = project_in_items(u_ref, x_nxt)
        outs = project_out_items(h_prev)
        recs = recur_items(x_cur, h_cur, len(ins))
        for k, rec in enumerate(recs):
            rec()
            ins[k]()
            if k % 2 == 1:
                outs[k // 2]()

    @pl.when(i % 2 == 0)
    def _():
        step(x0_scr, h0_scr, x1_scr, h1_scr)

    @pl.when(i % 2 == 1)
    def _():
        step(x1_scr, h1_scr, x0_scr, h0_scr)


def _s5_weights(lam_re, lam_im, b_re, b_im, c_re, c_im, log_dt):
    lr = jnp.minimum(lam_re.astype(F32), -1e-4)
    li = lam_im.astype(F32)
    dt = jnp.exp(log_dt.astype(F32))[:, None]
    mag = jnp.exp(lr * dt)
    a_re = mag * jnp.cos(li * dt)
    a_im = mag * jnp.sin(li * dt)
    den = lr * lr + li * li
    coef_re = ((a_re - 1.0) * lr + a_im * li) / den
    coef_im = (a_im * lr - (a_re - 1.0) * li) / den
    br = b_re.astype(F32)
    bi = b_im.astype(F32)
    bb_re = coef_re[..., None] * br - coef_im[..., None] * bi
    bb_im = coef_re[..., None] * bi + coef_im[..., None] * br
    gh = S5_GROUPS // 2
    eye = jnp.eye(gh, dtype=F32)

    def b_dense(bb):
        t = bb.reshape(2, gh, S5_STATE, S5_GROUP)
        return jnp.einsum('fgph,gk->fghkp', t, eye).reshape(2, S5_HALF_CH, S5_HALF_ST)

    def c_dense(cc):
        t = cc.reshape(2, gh, S5_GROUP, S5_STATE)
        return jnp.einsum('fghp,gk->fgpkh', t, eye).reshape(2, S5_HALF_ST, S5_HALF_CH)

    def x_tiles(bd):
        tiles = []
        for n in range(S5_NT):
            c = (n // 2) * LANES
            tiles.append(jnp.concatenate([bd[0, c:c + LANES, n * S5_TILE:(n + 1) * S5_TILE],
                                          bd[1, c:c + LANES, n * S5_TILE:(n + 1) * S5_TILE]], axis=0))
        return jnp.stack(tiles)

    def y_tiles(cd):
        tiles = []
        for n in range(S5_NT):
            c = (n // 2) * LANES
            tiles.append(jnp.concatenate([cd[0, n * S5_TILE:(n + 1) * S5_TILE, c:c + LANES],
                                          cd[1, n * S5_TILE:(n + 1) * S5_TILE, c:c + LANES]], axis=1))
        return jnp.stack(tiles)

    wx = jnp.stack([x_tiles(b_dense(bb_re)), x_tiles(b_dense(bb_im))]).astype(MXU_DTYPE)
    wy = jnp.stack([y_tiles(c_dense(c_re.astype(F32))), y_tiles(c_dense(-c_im.astype(F32)))]).astype(MXU_DTYPE)
    return wx, wy, a_re.reshape(2, S5_HALF_ST), a_im.reshape(2, S5_HALF_ST)


def _s5_scan(hp3d, wx, wy, a_re2, a_im2):
    b, s, width = hp3d.shape
    nseq = 2 * b
    assert nseq == SUBLANES
    sel = jnp.arange(nseq) % 2
    a = jnp.stack([a_re2[sel], a_im2[sel]])
    scr = pltpu.VMEM((S5_SLABS, nseq * S5_PITCH, LANES), F32)
    hscr = pltpu.VMEM((S5_SLABS, nseq * S5_H_PITCH, LANES), F32)
    last = s // S5_STEPS - 1
    ucol = width // S5_WIDTH - 1
    return pl.pallas_call(
        _s5_kernel,
        grid=(last + 2,),
        in_specs=[pl.BlockSpec((b, S5_STEPS, S5_WIDTH), lambda i: (0, jnp.minimum(i, last), ucol)),
                  pl.BlockSpec((b, S5_STEPS, S5_WIDTH), lambda i: (0, jnp.minimum(i + 1, last), ucol)),
                  _resident(wx.shape), _resident(wy.shape), _resident(a.shape)],
        out_specs=pl.BlockSpec((b, S5_STEPS, S5_WIDTH), lambda i: (0, jnp.maximum(i - 1, 0), 0)),
        out_shape=jax.ShapeDtypeStruct((b, s, S5_WIDTH), F32),
        scratch_shapes=[pltpu.VMEM((2, nseq, S5_HALF_ST), F32), scr, scr, hscr, hscr],
        compiler_params=_params("arbitrary"),
        name="s5_scan",
    )(hp3d, hp3d, wx, wy, a)


MLP_CHUNK = 1024


def _tail_body(mix, x_ref, g1_ref, b1_ref, w1_ref, w2_ref, g2_ref, b2_ref, o_ref):
    x1 = _layer_norm(DN_ALPHA * x_ref[...] + mix, g1_ref[...], b1_ref[...])
    x1m = _mx(x1)
    acc = jnp.zeros(x1.shape, F32)
    for f in range(w1_ref.shape[1] // MLP_CHUNK):
        hid = jnp.dot(x1m, w1_ref[:, f * MLP_CHUNK:(f + 1) * MLP_CHUNK], preferred_element_type=F32)
        hid = jnp.square(jnp.maximum(hid, 0.0))
        acc = acc + jnp.dot(_mx(hid), w2_ref[f * MLP_CHUNK:(f + 1) * MLP_CHUNK, :],
                            preferred_element_type=F32)
    o_ref[...] = _layer_norm(DN_ALPHA * x1 + acc, g2_ref[...], b2_ref[...])


def _tail_even_kernel(attn_ref, y_ref, u_ref, dsk_ref, wg_ref, bg_ref, wo_ref, x_ref,
                      g1_ref, b1_ref, w1_ref, w2_ref, g2_ref, b2_ref, o_ref):
    y = y_ref[...] + dsk_ref[...] * u_ref[...]
    g = 0.5 * y * (1.0 + jnp.tanh(math.sqrt(2.0 / math.pi) * (y + 0.044715 * (y * y * y))))
    ssm = g * _sigmoid(jnp.dot(_mx(g), wg_ref[...], preferred_element_type=F32) + bg_ref[...])
    ka = attn_ref.shape[1]
    mix = (jnp.dot(_mx(attn_ref[...]), wo_ref[0:ka, :], preferred_element_type=F32)
           + jnp.dot(_mx(ssm), wo_ref[ka:, :], preferred_element_type=F32))
    _tail_body(mix, x_ref, g1_ref, b1_ref, w1_ref, w2_ref, g2_ref, b2_ref, o_ref)


def _tail_odd_kernel(a_ref, wo_ref, x_ref, g1_ref, b1_ref, w1_ref, w2_ref, g2_ref, b2_ref, o_ref):
    mix = jnp.dot(_mx(a_ref[...]), wo_ref[...], preferred_element_type=F32)
    _tail_body(mix, x_ref, g1_ref, b1_ref, w1_ref, w2_ref, g2_ref, b2_ref, o_ref)


def _row(v):
    return v.reshape(1, -1).astype(F32)


def _tail_common_specs(tm, d, dff):
    return [_resident((d, d)),
            pl.BlockSpec((tm, d), lambda i: (i, 0)),
            _resident((1, d)), _resident((1, d)),
            _resident((d, dff)), _resident((dff, d)),
            _resident((1, d)), _resident((1, d))]


def _tail_even(attn2d, y2d, hp2d, x2d, d_skip, w_glu, b_glu, w_out, g1, b1, w1, w2, g2, b2, tm):
    m, d = x2d.shape
    dff = w1.shape[1]
    ucol = hp2d.shape[1] // S5_WIDTH - 1
    return pl.pallas_call(
        _tail_even_kernel,
        grid=(m // tm,),
        in_specs=[pl.BlockSpec((tm, A_WIDTH), lambda i: (i, 0)),
                  pl.BlockSpec((tm, S5_WIDTH), lambda i: (i, 0)),
                  pl.BlockSpec((tm, S5_WIDTH), lambda i: (i, ucol)),
                  _resident((1, S5_WIDTH)), _resident((S5_WIDTH, S5_WIDTH)), _resident((1, S5_WIDTH))]
        + _tail_common_specs(tm, d, dff),
        out_specs=pl.BlockSpec((tm, d), lambda i: (i, 0)),
        out_shape=jax.ShapeDtypeStruct((m, d), F32),
        compiler_params=_params("parallel"),
        name="tail_even",
    )(attn2d, y2d, hp2d, _row(d_skip), _mx(w_glu), _row(b_glu), _mx(w_out), x2d,
      _row(g1), _row(b1), _mx(w1), _mx(w2), _row(g2), _row(b2))


def _tail_odd(a2d, x2d, w_out, g1, b1, w1, w2, g2, b2, tm):
    m, d = x2d.shape
    dff = w1.shape[1]
    return pl.pallas_call(
        _tail_odd_kernel,
        grid=(m // tm,),
        in_specs=[pl.BlockSpec((tm, a2d.shape[1]), lambda i: (i, 0))] + _tail_common_specs(tm, d, dff),
        out_specs=pl.BlockSpec((tm, d), lambda i: (i, 0)),
        out_shape=jax.ShapeDtypeStruct((m, d), F32),
        compiler_params=_params("parallel"),
        name="tail_odd",
    )(a2d, _mx(w_out), x2d, _row(g1), _row(b1), _mx(w1), _mx(w2), _row(g2), _row(b2))


GDN_TILE = 256
GDN_GATES_TILE = 512
GDN_WAVE = 32


def _softplus(t):
    return jnp.maximum(t, 0.0) + jnp.log1p(jnp.exp(-jnp.abs(t)))


def _gdn_prep_body(i, hg, q_ref, k_ref, v_ref, qp_ref, kp_ref, vp_ref, cwq_ref, cwk_ref, cwv_ref,
                   ba_ref, g_ref, gt_ref, u_ref, w_ref, qd_ref, kd_ref, intra_ref, edl_ref,
                   qx_ref, kx_ref, vx_ref, ql_ref, kl_ref, vl_ref):
    assert C_CONV == 4
    t, width = q_ref.shape[1], q_ref.shape[2]
    c = C_CHUNK
    dk = C_HEAD_DIM
    hp = width // dk
    groups = t // SUBLANES

    def conv_silu(main_ref, prev_ref, cw_ref, ext_ref, low_ref):
        x = main_ref[0]
        ext_ref[0:SUBLANES, :] = jnp.zeros((SUBLANES, width), F32)
        ext_ref[SUBLANES:2 * SUBLANES, :] = jnp.where(i > 0, prev_ref[0], 0.0)
        ext_ref[2 * SUBLANES:, :] = x
        ext_rows = t + SUBLANES

        def tap(j):
            return jnp.broadcast_to(cw_ref[j:j + 1, :], (SUBLANES, width))[None]

        def grouped(a):
            return a.reshape(a.shape[0] // SUBLANES, SUBLANES, width)

        x_ext = grouped(ext_ref[SUBLANES:, :])
        x_del = grouped(ext_ref[SUBLANES - 1:SUBLANES - 1 + ext_rows, :])
        low_ref[...] = (tap(1) * x_ext + tap(0) * x_del).reshape(ext_rows, width)
        y = (tap(3) * grouped(x) + tap(2) * x_del[1:]
             + grouped(low_ref[SUBLANES - 2:SUBLANES - 2 + t, :]))
        hy = 0.5 * y.reshape(t, width)
        return hy + hy * jnp.tanh(hy)

    def l2n(z, scale):
        return z * (lax.rsqrt(jnp.sum(z * z, axis=-1, keepdims=True) + L2_EPS) * scale)

    q_all = conv_silu(q_ref, qp_ref, cwq_ref, qx_ref, ql_ref)
    k_all = conv_silu(k_ref, kp_ref, cwk_ref, kx_ref, kl_ref)
    v_all = conv_silu(v_ref, vp_ref, cwv_ref, vx_ref, vl_ref)
    heads = [slice(hh * dk, (hh + 1) * dk) for hh in range(hp)]
    qs = [l2n(q_all[:, cs], dk ** -0.5) for cs in heads]
    ks = [l2n(k_all[:, cs], 1.0) for cs in heads]
    vs = [v_all[:, cs] for cs in heads]

    ba = ba_ref[0]
    lane = lax.broadcasted_iota(jnp.int32, (t, LANES), 1)
    betas, g_cols, g_rows = [], [], []
    for hh in range(hp):
        h = hg * hp + hh
        betas.append(_sigmoid(jnp.sum(jnp.where(lane == h, ba, 0.0), axis=1, keepdims=True)))
        g_cols.append(jnp.sum(jnp.where(lane == C_HEADS + h, g_ref[0], 0.0), axis=1, keepdims=True))
        g_rows.append(gt_ref[0, pl.ds(C_HEADS + h, 1), :])

    ii = lax.broadcasted_iota(jnp.int32, (c, c), 0)
    jj = lax.broadcasted_iota(jnp.int32, (c, c), 1)
    low = ii >= jj
    all_units = [(hh, slice(ch * c, (ch + 1) * c)) for hh in range(hp) for ch in range(t // c)]
    for w0 in range(0, len(all_units), GDN_WAVE):
        _gdn_prep_wave(all_units[w0:w0 + GDN_WAVE], qs, ks, vs, betas, g_cols, g_rows, heads,
                       u_ref, w_ref, qd_ref, kd_ref, intra_ref, edl_ref)


def _gdn_prep_wave(units, qs, ks, vs, betas, g_cols, g_rows, heads, u_ref, w_ref, qd_ref, kd_ref, intra_ref, edl_ref):
    c = C_CHUNK
    dk = C_HEAD_DIM
    ii = lax.broadcasted_iota(jnp.int32, (c, c), 0)
    jj = lax.broadcasted_iota(jnp.int32, (c, c), 1)
    low = ii >= jj
    gc_cols = [jnp.sum(jnp.where(low, g_rows[hh][:, r], 0.0), axis=1, keepdims=True) for hh, r in units]
    gc_rows = [jnp.sum(jnp.where(ii <= jj, g_cols[hh][r], 0.0), axis=0, keepdims=True) for hh, r in units]
    decays = [jnp.where(low, jnp.exp(jnp.where(low, a - b, 0.0)), 0.0) for a, b in zip(gc_cols, gc_rows)]
    beta_bs = [jnp.broadcast_to(betas[hh][r], (c, dk)) for hh, r in units]
    kbs = [ks[hh][r] * bb for bb, (hh, r) in zip(beta_bs, units)]
    kks = [_dot_nt(kb, ks[hh][r]) for kb, (hh, r) in zip(kbs, units)]
    qks = [_dot_nt(qs[hh][r], ks[hh][r]) for hh, r in units]
    pws = [jnp.where(ii > jj, -(kk * dec), 0.0) for kk, dec in zip(kks, decays)]
    tms = pws
    for _ in range(int(math.log2(c)) - 1):
        pbs = [_mx(pw) for pw in pws]
        pws = [_dot(pb, pb) for pb in pbs]
        pbs = [_mx(pw) for pw in pws]
        tms = [tm + pw + _dot(tm, pb) for tm, pw, pb in zip(tms, pws, pbs)]
    egcs = [jnp.broadcast_to(jnp.exp(a), (c, dk)) for a in gc_cols]
    vbs = [vs[hh][r] * bb for bb, (hh, r) in zip(beta_bs, units)]
    kbes = [kb * e for kb, e in zip(kbs, egcs)]
    us = [vb + _dot(tm, vb) for tm, vb in zip(tms, vbs)]
    ws = [kbe + _dot(tm, kbe) for tm, kbe in zip(tms, kbes)]
    for n, (hh, r) in enumerate(units):
        gl = gc_cols[n][c - 1:c, :]
        cs = heads[hh]
        ch = r.start // c
        u_ref[0, r, cs] = us[n]
        w_ref[0, r, cs] = _mx(ws[n])
        qd_ref[0, r, cs] = _mx(qs[hh][r] * egcs[n])
        kd_ref[0, r, cs] = _mx(ks[hh][r] * jnp.exp(gl - gc_cols[n]))
        intra_ref[0, hh, r, :] = _mx(qks[n] * decays[n])
        edl_ref[0, hh, ch:ch + 1, :] = jnp.broadcast_to(jnp.exp(gl), (1, LANES))


def _gdn_gates_kernel(ba_ref, alog_ref, dtb_ref, g_ref, gt_ref):
    g_all = -jnp.exp(alog_ref[...]) * _softplus(ba_ref[0] + dtb_ref[...])
    g_ref[0] = g_all
    gt_ref[0] = g_all.T


def _gdn_gates(hp3d, alog_row, dtb_row):
    b, s, _ = hp3d.shape
    t = GDN_GATES_TILE
    ba_col = 4 * C_WIDTH // LANES
    return pl.pallas_call(
        _gdn_gates_kernel,
        grid=(b, s // t),
        in_specs=[pl.BlockSpec((1, t, LANES), lambda bi, i: (bi, i, ba_col)),
                  pl.BlockSpec((1, LANES), lambda bi, i: (0, 0)),
                  pl.BlockSpec((1, LANES), lambda bi, i: (0, 0))],
        out_specs=[pl.BlockSpec((1, t, LANES), lambda bi, i: (bi, i, 0)),
                   pl.BlockSpec((1, LANES, t), lambda bi, i: (bi, 0, i))],
        out_shape=[jax.ShapeDtypeStruct((b, s, LANES), F32), jax.ShapeDtypeStruct((b, LANES, s), F32)],
        compiler_params=_params("parallel", "parallel"),
        name="gdn_gates",
    )(hp3d, alog_row, dtb_row)


def _gdn_scan_body(chunk0, u_ref, w_ref, qd_ref, kd_ref, intra_ref, edl_ref, gate_ref, nw_ref, o_ref, st_ref):
    c = C_CHUNK
    dk = C_HEAD_DIM
    t = u_ref.shape[1]
    nw = nw_ref[...]
    heads = [slice(h * dk, (h + 1) * dk) for h in range(C_HEADS)]
    for ch in range(t // c):
        r = slice(ch * c, (ch + 1) * c)
        sts = [st_ref[h] for h in range(C_HEADS)]
        wss = [_dot(jnp.concatenate([w_ref[0, r, cs], qd_ref[0, r, cs]], axis=0), st)
               for cs, st in zip(heads, sts)]
        v_news = [u_ref[0, r, cs] - ws[:c] for cs, ws in zip(heads, wss)]
        kvs = [_dot_tn(kd_ref[0, r, cs], v_new) for cs, v_new in zip(heads, v_news)]
        os_ = [ws[c:] + _dot(intra_ref[0, h, r, :], v_new) for h, (ws, v_new) in enumerate(zip(wss, v_news))]
        for h, cs in enumerate(heads):
            st_ref[h] = sts[h] * edl_ref[0, h, pl.ds(chunk0 + ch, 1), 0:1] + kvs[h]
            o = os_[h]
            o = o * lax.rsqrt(jnp.mean(o * o, axis=-1, keepdims=True) + RMS_EPS) * nw
            hg = 0.5 * gate_ref[0, r, cs]
            o_ref[0, r, cs] = o * (hg + hg * jnp.tanh(hg))


def _gdn_kernel(q_ref, k_ref, v_ref, qp_ref, kp_ref, vp_ref, cwq_ref, cwk_ref, cwv_ref, ba_ref, g_ref, gt_ref,
                gate_ref, nw_ref, o_ref, st_ref, qx_ref, kx_ref, vx_ref, ql_ref, kl_ref, vl_ref, *bufs):
    j = pl.program_id(1)
    last = pl.num_programs(1) - 2
    set_a, set_b = bufs[:len(bufs) // 2], bufs[len(bufs) // 2:]

    @pl.when(j == 0)
    def _():
        st_ref[...] = jnp.zeros_like(st_ref)
        for ref in bufs:
            ref[...] = jnp.zeros_like(ref)

    def step(cur, nxt):
        _gdn_prep_body(jnp.minimum(j, last), 0, q_ref, k_ref, v_ref, qp_ref, kp_ref, vp_ref,
                       cwq_ref, cwk_ref, cwv_ref, ba_ref, g_ref, gt_ref, *nxt,
                       qx_ref, kx_ref, vx_ref, ql_ref, kl_ref, vl_ref)
        _gdn_scan_body(0, *cur, gate_ref, nw_ref, o_ref, st_ref)

    @pl.when(j % 2 == 0)
    def _():
        step(set_a, set_b)

    @pl.when(j % 2 == 1)
    def _():
        step(set_b, set_a)


def _gdn(hp3d, conv_w, alog_row, dtb_row, norm_w):
    b, s, _ = hp3d.shape
    ba_col = 4 * C_WIDTH // LANES
    t = GDN_TILE
    nt = s // t
    per = t // SUBLANES
    wd = C_WIDTH
    nc = t // C_CHUNK
    g, gt = _gdn_gates(hp3d, alog_row, dtb_row)

    def prep_tile(j):
        return jnp.minimum(j, nt - 1)

    def scan_tile(j):
        return jnp.maximum(j - 1, 0)

    def main(off):
        return pl.BlockSpec((1, t, wd), lambda bi, j: (bi, prep_tile(j), off))

    def prev(off):
        return pl.BlockSpec((1, SUBLANES, wd), lambda bi, j: (bi, jnp.maximum(prep_tile(j) * per - 1, 0), off))

    def cw(off):
        return pl.BlockSpec((C_CONV, wd), lambda bi, j: (0, off))

    bufs = [pltpu.VMEM((1, t, wd), F32)] + [pltpu.VMEM((1, t, wd), MXU_DTYPE)] * 3 + [
        pltpu.VMEM((1, C_HEADS, t, C_CHUNK), MXU_DTYPE), pltpu.VMEM((1, C_HEADS, max(nc, SUBLANES), LANES), F32)]
    return pl.pallas_call(
        _gdn_kernel,
        grid=(b, nt + 1),
        in_specs=[main(0), main(1), main(2), prev(0), prev(1), prev(2), cw(0), cw(1), cw(2),
                  pl.BlockSpec((1, t, LANES), lambda bi, j: (bi, prep_tile(j), ba_col)),
                  pl.BlockSpec((1, t, LANES), lambda bi, j: (bi, prep_tile(j), 0)),
                  pl.BlockSpec((1, LANES, t), lambda bi, j: (bi, 0, prep_tile(j))),
                  pl.BlockSpec((1, t, wd), lambda bi, j: (bi, scan_tile(j), 3)),
                  pl.BlockSpec((1, C_HEAD_DIM), lambda bi, j: (0, 0))],
        out_specs=pl.BlockSpec((1, t, wd), lambda bi, j: (bi, scan_tile(j), 0)),
        out_shape=jax.ShapeDtypeStruct((b, s, wd), F32),
        scratch_shapes=[pltpu.VMEM((C_HEADS, C_HEAD_DIM, C_HEAD_DIM), F32)]
        + [pltpu.VMEM((t + 2 * SUBLANES, wd), F32)] * 3 + [pltpu.VMEM((t + SUBLANES, wd), F32)] * 3
        + bufs + bufs,
        compiler_params=_params("parallel", "arbitrary"),
        name="gdn",
    )(hp3d, hp3d, hp3d, hp3d, hp3d, hp3d, conv_w, conv_w, conv_w, hp3d, g, gt, hp3d, _row(norm_w))


ROW_TILE = 512


def kernel(x, rel_table, ev_w_in, ev_w_out, s5_lambda_re, s5_lambda_im, s5_b_re, s5_b_im, s5_c_re, s5_c_im, s5_d, s5_log_dt, s5_w_glu, s5_b_glu, od_w_in, od_conv_w, od_a_log, od_dt_bias, od_norm_w, od_w_out, ln_mix_g, ln_mix_b, mlp_w1, mlp_w2, ln_ffn_g, ln_ffn_b):
    bsz, s, d = x.shape
    m = bsz * s
    x2d = x.reshape(m, d)
    table_t = rel_table.T.astype(F32)
    bias = _bias_tiles(table_t)
    for layer in range(DEPTH):
        i = layer // 2
        if layer % 2 == 0:
            hp = _proj(x2d, _mx(ev_w_in[i]), ROW_TILE)
            hp3d = hp.reshape(bsz, s, hp.shape[1])
            attn = _moba(hp3d, bias)
            wx, wy, a_re2, a_im2 = _s5_weights(s5_lambda_re[i], s5_lambda_im[i], s5_b_re[i], s5_b_im[i],
                                               s5_c_re[i], s5_c_im[i], s5_log_dt[i])
            y = _s5_scan(hp3d, wx, wy, a_re2, a_im2)
            x2d = _tail_even(attn.reshape(m, A_WIDTH), y.reshape(m, S5_WIDTH), hp, x2d,
                             s5_d[i], s5_w_glu[i], s5_b_glu[i], ev_w_out[i],
                             ln_mix_g[layer], ln_mix_b[layer], mlp_w1[layer], mlp_w2[layer],
                             ln_ffn_g[layer], ln_ffn_b[layer], ROW_TILE)
        else:
            w_in = od_w_in[i]
            w_all = jnp.pad(w_in, ((0, 0), (0, LANES - 2 * C_HEADS)))
            hp = _proj(x2d, _mx(w_all), ROW_TILE)
            hp3d = hp.reshape(bsz, s, 4 * C_WIDTH + LANES)
            pad8 = jnp.zeros((C_HEADS,), F32)
            padr = jnp.zeros((LANES - 2 * C_HEADS,), F32)
            alog_row = jnp.concatenate([pad8, od_a_log[i].astype(F32), padr]).reshape(1, LANES)
            dtb_row = jnp.concatenate([pad8, od_dt_bias[i].astype(F32), padr]).reshape(1, LANES)
            o = _gdn(hp3d, od_conv_w[i].astype(F32), alog_row, dtb_row, od_norm_w[i])
            x2d = _tail_odd(o.reshape(m, C_WIDTH), x2d, od_w_out[i],
                            ln_mix_g[layer], ln_mix_b[layer], mlp_w1[layer], mlp_w2[layer],
                            ln_ffn_g[layer], ln_ffn_b[layer], ROW_TILE)
    return x2d.reshape(bsz, s, d)
```

```python
import functools
import math

import jax
import jax.numpy as jnp
import numpy as np
from jax import lax
from jax.experimental import pallas as pl
from jax.experimental.pallas import tpu as pltpu

F32 = jnp.float32
MXU_DTYPE = jnp.bfloat16

LANES = 128
SUBLANES = 8
VMEM_LIMIT_BYTES = 56 * 1024 * 1024

A_HEADS = 8
A_HEAD_DIM = 64
A_WIDTH = A_HEADS * A_HEAD_DIM
MOBA_BLOCK = 256
MOBA_TOPK = 3
REL_BUCKETS = 32
REL_MAX_DIST = 128
S5_GROUP = 16
S5_GROUPS = 32
S5_STATE = 64
S5_WIDTH = S5_GROUP * S5_GROUPS
C_HEADS = 8
C_HEAD_DIM = 128
C_WIDTH = C_HEADS * C_HEAD_DIM
C_CONV = 4
C_CHUNK = 64
DEPTH = 2
DN_ALPHA = (2 * DEPTH) ** 0.25
LN_EPS = 1e-5
RMS_EPS = 1e-6
L2_EPS = 1e-6
NEG_INF = -1e30
LOG2E = math.log2(math.e)


def _mx(t):
    return t.astype(MXU_DTYPE)


def _dot(a, b):
    return jnp.dot(_mx(a), _mx(b), preferred_element_type=F32)


def _dot_tn(a, b):
    return lax.dot_general(_mx(a), _mx(b), (((0,), (0,)), ((), ())), preferred_element_type=F32)


def _dot_nt(a, b):
    return lax.dot_general(_mx(a), _mx(b), (((1,), (1,)), ((), ())), preferred_element_type=F32)


def _sigmoid(t):
    return 1.0 / (1.0 + jnp.exp(-t))


def _layer_norm(r, g, b):
    mu = jnp.mean(r, axis=-1, keepdims=True)
    d = r - mu
    var = jnp.mean(d * d, axis=-1, keepdims=True)
    return d * lax.rsqrt(var + LN_EPS) * g + b


def _params(*sem):
    return pltpu.CompilerParams(dimension_semantics=sem, vmem_limit_bytes=VMEM_LIMIT_BYTES)


def _resident(shape):
    return pl.BlockSpec(shape, lambda *_: (0,) * len(shape), pipeline_mode=pl.Buffered(1))


PROJ_CHUNK = 1024


def _proj_kernel(x_ref, w_ref, o_ref):
    xm = _mx(x_ref[...])
    n = w_ref.shape[1]
    for c0 in range(0, n, PROJ_CHUNK):
        c1 = min(c0 + PROJ_CHUNK, n)
        o_ref[:, c0:c1] = jnp.dot(xm, w_ref[:, c0:c1], preferred_element_type=F32)


def _proj(x2d, w, tm):
    m, k = x2d.shape
    n = w.shape[1]
    return pl.pallas_call(
        _proj_kernel,
        grid=(m // tm,),
        in_specs=[pl.BlockSpec((tm, k), lambda i: (i, 0)), _resident((k, n))],
        out_specs=pl.BlockSpec((tm, n), lambda i: (i, 0)),
        out_shape=jax.ShapeDtypeStruct((m, n), F32),
        compiler_params=_params("parallel"),
        name="in_proj",
    )(x2d, w)


def _t5_bucket_np(dist):
    max_exact = REL_BUCKETS // 2
    n = np.maximum(dist, 0)
    nf = np.maximum(n, 1).astype(np.float32)
    large = max_exact + (np.log(nf / np.float32(max_exact)) / np.float32(math.log(REL_MAX_DIST / max_exact))
                         * np.float32(REL_BUCKETS - max_exact)).astype(np.int32)
    large = np.minimum(large, REL_BUCKETS - 1)
    return np.where(n < max_exact, n, large).astype(np.int32)


def _bias_kernel(tab_ref, bkt_ref, o_ref):
    h = pl.program_id(0)
    blk = bkt_ref.shape[-1]
    key = lax.broadcasted_iota(jnp.int32, (blk, blk), 0)
    qry = lax.broadcasted_iota(jnp.int32, (blk, blk), 1)
    far = tab_ref[h, REL_BUCKETS - 1]
    for t in range(2):
        bk = bkt_ref[t]
        acc = jnp.zeros((blk, blk), F32)
        for i in range(REL_BUCKETS):
            acc = jnp.where(bk == i, (tab_ref[h, i] - far) * LOG2E, acc)
        if t == 0:
            acc = jnp.where(qry >= key, acc, NEG_INF)
        o_ref[0, t] = acc


def _bias_tiles(table_t):
    heads = table_t.shape[0]
    offs = np.arange(MOBA_BLOCK)
    dist_own = offs[None, :] - offs[:, None]
    buckets = np.stack([_t5_bucket_np(dist_own), _t5_bucket_np(dist_own + MOBA_BLOCK)])
    return pl.pallas_call(
        _bias_kernel,
        grid=(heads,),
        in_specs=[pl.BlockSpec(memory_space=pltpu.SMEM),
                  pl.BlockSpec((2, MOBA_BLOCK, MOBA_BLOCK), lambda h: (0, 0, 0))],
        out_specs=pl.BlockSpec((1, 2, MOBA_BLOCK, MOBA_BLOCK), lambda h: (h, 0, 0, 0)),
        out_shape=jax.ShapeDtypeStruct((heads, 2, MOBA_BLOCK, MOBA_BLOCK), F32),
        compiler_params=_params("parallel"),
        name="t5_bias",
    )(table_t, jnp.asarray(buckets))


def _moba_kernel(q_ref, k_ref, v_ref, bias_ref, o_ref, kmean_ref, kb_ref, vt_ref,
                 st0_ref, st1_ref, wq_ref, acc_ref):
    st_refs = (st0_ref, st1_ref)
    qb = pl.program_id(2)
    blk = MOBA_BLOCK
    d = A_HEAD_DIM
    nh = LANES // d
    nb = k_ref.shape[1] // blk
    ng = q_ref.shape[2] // LANES
    groups = range(ng)

    @pl.when(qb == 0)
    def _():
        lane = lax.broadcasted_iota(jnp.int32, (blk, LANES - d), 1)
        for g in groups:
            gl = slice(g * LANES, (g + 1) * LANES)
            for j in range(nb + 1):
                onehot = jnp.where(lane == j, 1.0, 0.0).astype(MXU_DTYPE)
                for hh in range(nh):
                    kb_ref[g, hh, j, :, d:LANES] = onehot
            for j in range(nb):
                kj = k_ref[0, j * blk:(j + 1) * blk, gl]
                for hh in range(nh):
                    kb_ref[g, hh, j, :, 0:d] = _mx(kj[:, hh * d:(hh + 1) * d])
                kmean_ref[g, j:j + 1, :] = jnp.mean(kj, axis=0, keepdims=True)
                vt_ref[g, j] = _mx(v_ref[0, j * blk:(j + 1) * blk, gl].T)
            for hh in range(nh):
                kb_ref[g, hh, nb, :, 0:d] = jnp.zeros((blk, d), MXU_DTYPE)
            vt_ref[g, nb] = jnp.zeros((LANES, blk), MXU_DTYPE)

    blk_id = lax.broadcasted_iota(jnp.int32, (nb, blk), 0)
    row = lax.broadcasted_iota(jnp.int32, (LANES - d - nb, blk), 0)
    for g in groups:
        qt = (q_ref[0, :, g * LANES:(g + 1) * LANES] * (d ** -0.5)).T
        for hh in range(nh):
            gate = lax.dot_general(kmean_ref[g, :, hh * d:(hh + 1) * d], qt[hh * d:(hh + 1) * d],
                                   (((1,), (0,)), ((), ())),
                                   precision=lax.Precision.HIGHEST, preferred_element_type=F32)
            gt = jnp.where(blk_id < qb, gate, -jnp.inf)
            sel = jnp.zeros((nb, blk), F32)
            for _ in range(MOBA_TOPK):
                m = jnp.max(gt, axis=0, keepdims=True)
                is_m = jnp.logical_and(gt == m, gt > -jnp.inf)
                idx = jnp.min(jnp.where(is_m, blk_id, nb), axis=0, keepdims=True)
                pick = blk_id == idx
                sel = jnp.where(pick, 1.0, sel)
                gt = jnp.where(pick, -jnp.inf, gt)
            keep = jnp.logical_or(sel > 0.0, blk_id == qb)
            wq_ref[g, hh] = _mx(jnp.concatenate([qt[hh * d:(hh + 1) * d] * LOG2E, jnp.where(keep, 0.0, NEG_INF),
                                                 jnp.where(row == 0, NEG_INF, 0.0)], axis=0))

    def score_stage(slot, ja, jb, with_bias):
        cmaxes = []
        for g in groups:
            for hh in range(nh):
                keys = jnp.concatenate([kb_ref[g, hh, ja], kb_ref[g, hh, jb]], axis=0)
                st = jnp.dot(keys, wq_ref[g, hh], preferred_element_type=F32)
                if with_bias:
                    st = st + jnp.concatenate([bias_ref[g * nh + hh, w] for w in range(2)], axis=0)
                st_refs[slot][g, 0, :, hh * blk:(hh + 1) * blk] = st[:blk]
                st_refs[slot][g, 1, :, hh * blk:(hh + 1) * blk] = st[blk:]
                cmaxes.append(jnp.max(st, axis=0, keepdims=True))
        return tuple(cmaxes)

    def attend_stage(slot, cmaxes, stats, ja, jb):
        new = []
        for g in groups:
            for hh in range(nh):
                cs = slice(hh * blk, (hh + 1) * blk)
                m_new = cmaxes[g * nh + hh]
                if stats is not None:
                    m, l = stats[g * nh + hh]
                    m_new = jnp.maximum(m, m_new)
                ps = [jnp.exp2(st_refs[slot][g, w, :, cs] - m_new) for w in range(2)]
                l_new = jnp.sum(ps[0], axis=0, keepdims=True) + jnp.sum(ps[1], axis=0, keepdims=True)
                vt = jnp.concatenate([vt_ref[g, ja, hh * d:(hh + 1) * d, :],
                                      vt_ref[g, jb, hh * d:(hh + 1) * d, :]], axis=1)
                pv = jnp.dot(vt, jnp.concatenate([_mx(ps[0]), _mx(ps[1])], axis=0),
                             preferred_element_type=F32)
                if stats is None:
                    acc_ref[g, hh] = pv
                    new.append((m_new, l_new))
                else:
                    alpha = jnp.exp2(m - m_new)
                    acc_ref[g, hh] = acc_ref[g, hh] * alpha + pv
                    new.append((m_new, alpha * l + l_new))
        return tuple(new)

    nfar = qb - 1
    npairs = (nfar + 1) // 2

    def far_blocks(i):
        return jnp.where(2 * i < nfar, 2 * i, nb), jnp.where(2 * i + 1 < nfar, 2 * i + 1, nb)

    j_prev = jnp.where(qb > 0, qb - 1, nb)
    cmax0 = score_stage(0, qb, j_prev, True)
    cmax1 = score_stage(1, *far_blocks(0), False)
    init = (attend_stage(0, cmax0, None, qb, j_prev), cmax1)

    def trip(slot, k, carry):
        stats, cmax = carry
        cmax_next = score_stage(1 - slot, *far_blocks(k), False)
        return attend_stage(slot, cmax, stats, *far_blocks(k - 1)), cmax_next

    def trip_dyn(k, carry):
        return lax.cond(k % 2 == 1, functools.partial(trip, 1, k), functools.partial(trip, 0, k), carry)

    stats, _ = lax.fori_loop(1, npairs + 1, trip_dyn, init)
    for g in groups:
        out_t = jnp.concatenate([acc_ref[g, hh] / stats[g * nh + hh][1] for hh in range(nh)], axis=0)
        o_ref[0, :, g * LANES:(g + 1) * LANES] = out_t.T


MOBA_GROUPS = 4


def _moba(hp3d, bias):
    b, s, _ = hp3d.shape
    ng = MOBA_GROUPS
    wd = ng * LANES
    nsteps = A_WIDTH // wd
    hpp = LANES // A_HEAD_DIM
    nb = s // MOBA_BLOCK
    return pl.pallas_call(
        _moba_kernel,
        grid=(b, nsteps, nb),
        in_specs=[pl.BlockSpec((1, MOBA_BLOCK, wd), lambda bi, p, i: (bi, i, p)),
                  pl.BlockSpec((1, s, wd), lambda bi, p, i: (bi, 0, nsteps + p), pipeline_mode=pl.Buffered(1)),
                  pl.BlockSpec((1, s, wd), lambda bi, p, i: (bi, 0, 2 * nsteps + p), pipeline_mode=pl.Buffered(1)),
                  pl.BlockSpec((ng * hpp, 2, MOBA_BLOCK, MOBA_BLOCK), lambda bi, p, i: (p, 0, 0, 0))],
        out_specs=pl.BlockSpec((1, MOBA_BLOCK, wd), lambda bi, p, i: (bi, i, p)),
        out_shape=jax.ShapeDtypeStruct((b, s, A_WIDTH), F32),
        scratch_shapes=[pltpu.VMEM((ng, nb, LANES), F32),
                        pltpu.VMEM((ng, hpp, nb + 1, MOBA_BLOCK, LANES), MXU_DTYPE),
                        pltpu.VMEM((ng, nb + 1, LANES, MOBA_BLOCK), MXU_DTYPE),
                        pltpu.VMEM((ng, 2, MOBA_BLOCK, hpp * MOBA_BLOCK), F32),
                        pltpu.VMEM((ng, 2, MOBA_BLOCK, hpp * MOBA_BLOCK), F32),
                        pltpu.VMEM((ng, hpp, LANES, MOBA_BLOCK), MXU_DTYPE),
                        pltpu.VMEM((ng, hpp, A_HEAD_DIM, MOBA_BLOCK), F32)],
        compiler_params=_params("parallel", "parallel", "arbitrary"),
        name="moba",
    )(hp3d, hp3d, hp3d, bias)


S5_HALF_CH = S5_WIDTH // 2
S5_HALF_ST = (S5_GROUPS // 2) * S5_STATE
S5_TILE = 256
S5_NT = S5_HALF_ST // S5_TILE
S5_STEPS = 64
S5_PITCH = S5_STEPS + SUBLANES
S5_H_PITCH = S5_STEPS + SUBLANES // 2
S5_SLABS = 2 * S5_HALF_ST // LANES


def _s5_kernel(u0_ref, u_ref, wx_ref, wy_ref, a_ref, o_ref, st_ref, x0_scr, x1_scr, h0_scr, h1_scr):
    nb, steps, _ = u_ref.shape
    nseq = 2 * nb
    tile_slabs = S5_TILE // LANES
    nre = S5_SLABS // 2
    i = pl.program_id(0)

    def project_in_items(src_ref, x_scr):
        def item(m, part, n):
            def run():
                slab0 = (part * S5_HALF_ST + n * S5_TILE) // LANES
                for half in range(2):
                    lhs = jnp.concatenate(
                        [_mx(src_ref[bi, :, half * S5_HALF_CH + m * LANES:half * S5_HALF_CH + (m + 1) * LANES])
                         for bi in range(nb)], axis=0)
                    res = jnp.dot(lhs, wx_ref[part, n, half * LANES:(half + 1) * LANES, :],
                                  preferred_element_type=F32)
                    for bi in range(nb):
                        seq = 2 * bi + half
                        for sl in range(tile_slabs):
                            x_scr[slab0 + sl, seq * S5_PITCH:seq * S5_PITCH + steps, :] = (
                                res[bi * steps:(bi + 1) * steps, sl * LANES:(sl + 1) * LANES])
            return run
        return [item(m, part, n) for m in range(S5_NT // 2) for part in range(2) for n in (2 * m, 2 * m + 1)]

    def recur_items(x_scr, h_scr, nitems):
        state = {}

        def item(t0, t1):
            def run():
                a_re = [a_ref[0, :, s * LANES:(s + 1) * LANES] for s in range(nre)]
                a_im = [a_ref[1, :, s * LANES:(s + 1) * LANES] for s in range(nre)]
                if t0 == 0:
                    state["re"] = [st_ref[0, :, s * LANES:(s + 1) * LANES] for s in range(nre)]
                    state["im"] = [st_ref[1, :, s * LANES:(s + 1) * LANES] for s in range(nre)]
                h_re, h_im = state["re"], state["im"]
                for t in range(t0, t1):
                    rows_t = pl.ds(t, nseq, stride=S5_PITCH)
                    for s in range(nre):
                        hr, hi = h_re[s], h_im[s]
                        h_re[s] = a_re[s] * hr - a_im[s] * hi + x_scr[s, rows_t, :]
                        h_im[s] = a_re[s] * hi + a_im[s] * hr + x_scr[nre + s, rows_t, :]
                        h_scr[s, pl.ds(t, nseq, stride=S5_H_PITCH), :] = h_re[s]
                        h_scr[nre + s, pl.ds(t, nseq, stride=S5_H_PITCH), :] = h_im[s]
                if t1 == steps:
                    for s in range(nre):
                        st_ref[0, :, s * LANES:(s + 1) * LANES] = h_re[s]
                        st_ref[1, :, s * LANES:(s + 1) * LANES] = h_im[s]
            return run
        per = steps // nitems
        return [item(k * per, (k + 1) * per) for k in range(nitems)]

    def project_out_items(h_scr):
        def item(m, half):
            def run():
                acc = jnp.zeros((nb * steps, LANES), F32)
                for n in (2 * m, 2 * m + 1):
                    for part in range(2):
                        slab0 = (part * S5_HALF_ST + n * S5_TILE) // LANES
                        h_tile = jnp.concatenate(
                            [jnp.concatenate([h_scr[slab0 + sl,
                                                    (2 * bi + half) * S5_H_PITCH:(2 * bi + half) * S5_H_PITCH + steps, :]
                                              for sl in range(tile_slabs)], axis=1) for bi in range(nb)], axis=0)
                        acc = acc + jnp.dot(_mx(h_tile), wy_ref[part, n, :, half * LANES:(half + 1) * LANES],
                                            preferred_element_type=F32)
                c = half * S5_HALF_CH + m * LANES
                for bi in range(nb):
                    o_ref[bi, :, c:c + LANES] = acc[bi * steps:(bi + 1) * steps, :]
            return run
        return [item(m, half) for m in range(S5_NT // 2) for half in range(2)]

    @pl.when(i == 0)
    def _():
        st_ref[...] = jnp.zeros_like(st_ref)
        h1_scr[...] = jnp.zeros_like(h1_scr)
        for run in project_in_items(u0_ref, x0_scr):
            run()

    def step(x_cur, h_cur, x_nxt, h_prev):
        ins = project_in_items(u_ref, x_nxt)
        outs = project_out_items(h_prev)
        recs = recur_items(x_cur, h_cur, len(ins))
        for k, rec in enumerate(recs):
            rec()
            ins[k]()
            if k % 2 == 1:
                outs[k // 2]()

    @pl.when(i % 2 == 0)
    def _():
        step(x0_scr, h0_scr, x1_scr, h1_scr)

    @pl.when(i % 2 == 1)
    def _():
        step(x1_scr, h1_scr, x0_scr, h0_scr)


def _s5_weights(lam_re, lam_im, b_re, b_im, c_re, c_im, log_dt):
    lr = jnp.minimum(lam_re.astype(F32), -1e-4)
    li = lam_im.astype(F32)
    dt = jnp.exp(log_dt.astype(F32))[:, None]
    mag = jnp.exp(lr * dt)
    a_re = mag * jnp.cos(li * dt)
    a_im = mag * jnp.sin(li * dt)
    den = lr * lr + li * li
    coef_re = ((a_re - 1.0) * lr + a_im * li) / den
    coef_im = (a_im * lr - (a_re - 1.0) * li) / den
    br = b_re.astype(F32)
    bi = b_im.astype(F32)
    bb_re = coef_re[..., None] * br - coef_im[..., None] * bi
    bb_im = coef_re[..., None] * bi + coef_im[..., None] * br
    gh = S5_GROUPS // 2
    eye = jnp.eye(gh, dtype=F32)

    def b_dense(bb):
        t = bb.reshape(2, gh, S5_STATE, S5_GROUP)
        return jnp.einsum('fgph,gk->fghkp', t, eye).reshape(2, S5_HALF_CH, S5_HALF_ST)

    def c_dense(cc):
        t = cc.reshape(2, gh, S5_GROUP, S5_STATE)
        return jnp.einsum('fghp,gk->fgpkh', t, eye).reshape(2, S5_HALF_ST, S5_HALF_CH)

    def x_tiles(bd):
        tiles = []
        for n in range(S5_NT):
            c = (n // 2) * LANES
            tiles.append(jnp.concatenate([bd[0, c:c + LANES, n * S5_TILE:(n + 1) * S5_TILE],
                                          bd[1, c:c + LANES, n * S5_TILE:(n + 1) * S5_TILE]], axis=0))
        return jnp.stack(tiles)

    def y_tiles(cd):
        tiles = []
        for n in range(S5_NT):
            c = (n // 2) * LANES
            tiles.append(jnp.concatenate([cd[0, n * S5_TILE:(n + 1) * S5_TILE, c:c + LANES],
                                          cd[1, n * S5_TILE:(n + 1) * S5_TILE, c:c + LANES]], axis=1))
        return jnp.stack(tiles)

    wx = jnp.stack([x_tiles(b_dense(bb_re)), x_tiles(b_dense(bb_im))]).astype(MXU_DTYPE)
    wy = jnp.stack([y_tiles(c_dense(c_re.astype(F32))), y_tiles(c_dense(-c_im.astype(F32)))]).astype(MXU_DTYPE)
    return wx, wy, a_re.reshape(2, S5_HALF_ST), a_im.reshape(2, S5_HALF_ST)


def _s5_scan(hp3d, wx, wy, a_re2, a_im2):
    b, s, width = hp3d.shape
    nseq = 2 * b
    assert nseq == SUBLANES
    sel = jnp.arange(nseq) % 2
    a = jnp.stack([a_re2[sel], a_im2[sel]])
    scr = pltpu.VMEM((S5_SLABS, nseq * S5_PITCH, LANES), F32)
    hscr = pltpu.VMEM((S5_SLABS, nseq * S5_H_PITCH, LANES), F32)
    last = s // S5_STEPS - 1
    ucol = width // S5_WIDTH - 1
    return pl.pallas_call(
        _s5_kernel,
        grid=(last + 2,),
        in_specs=[pl.BlockSpec((b, S5_STEPS, S5_WIDTH), lambda i: (0, jnp.minimum(i, last), ucol)),
                  pl.BlockSpec((b, S5_STEPS, S5_WIDTH), lambda i: (0, jnp.minimum(i + 1, last), ucol)),
                  _resident(wx.shape), _resident(wy.shape), _resident(a.shape)],
        out_specs=pl.BlockSpec((b, S5_STEPS, S5_WIDTH), lambda i: (0, jnp.maximum(i - 1, 0), 0)),
        out_shape=jax.ShapeDtypeStruct((b, s, S5_WIDTH), F32),
        scratch_shapes=[pltpu.VMEM((2, nseq, S5_HALF_ST), F32), scr, scr, hscr, hscr],
        compiler_params=_params("arbitrary"),
        name="s5_scan",
    )(hp3d, hp3d, wx, wy, a)


MLP_CHUNK = 1024


def _tail_body(mix, x_ref, g1_ref, b1_ref, w1_ref, w2_ref, g2_ref, b2_ref, o_ref):
    x1 = _layer_norm(DN_ALPHA * x_ref[...] + mix, g1_ref[...], b1_ref[...])
    x1m = _mx(x1)
    acc = jnp.zeros(x1.shape, F32)
    for f in range(w1_ref.shape[1] // MLP_CHUNK):
        hid = jnp.dot(x1m, w1_ref[:, f * MLP_CHUNK:(f + 1) * MLP_CHUNK], preferred_element_type=F32)
        hid = jnp.square(jnp.maximum(hid, 0.0))
        acc = acc + jnp.dot(_mx(hid), w2_ref[f * MLP_CHUNK:(f + 1) * MLP_CHUNK, :],
                            preferred_element_type=F32)
    o_ref[...] = _layer_norm(DN_ALPHA * x1 + acc, g2_ref[...], b2_ref[...])


def _tail_even_kernel(attn_ref, y_ref, u_ref, dsk_ref, wg_ref, bg_ref, wo_ref, x_ref,
                      g1_ref, b1_ref, w1_ref, w2_ref, g2_ref, b2_ref, o_ref):
    y = y_ref[...] + dsk_ref[...] * u_ref[...]
    g = 0.5 * y * (1.0 + jnp.tanh(math.sqrt(2.0 / math.pi) * (y + 0.044715 * (y * y * y))))
    ssm = g * _sigmoid(jnp.dot(_mx(g), wg_ref[...], preferred_element_type=F32) + bg_ref[...])
    ka = attn_ref.shape[1]
    mix = (jnp.dot(_mx(attn_ref[...]), wo_ref[0:ka, :], preferred_element_type=F32)
           + jnp.dot(_mx(ssm), wo_ref[ka:, :], preferred_element_type=F32))
    _tail_body(mix, x_ref, g1_ref, b1_ref, w1_ref, w2_ref, g2_ref, b2_ref, o_ref)


def _tail_odd_kernel(a_ref, wo_ref, x_ref, g1_ref, b1_ref, w1_ref, w2_ref, g2_ref, b2_ref, o_ref):
    mix = jnp.dot(_mx(a_ref[...]), wo_ref[...], preferred_element_type=F32)
    _tail_body(mix, x_ref, g1_ref, b1_ref, w1_ref, w2_ref, g2_ref, b2_ref, o_ref)


def _row(v):
    return v.reshape(1, -1).astype(F32)


def _tail_common_specs(tm, d, dff):
    return [_resident((d, d)),
            pl.BlockSpec((tm, d), lambda i: (i, 0)),
            _resident((1, d)), _resident((1, d)),
            _resident((d, dff)), _resident((dff, d)),
            _resident((1, d)), _resident((1, d))]


def _tail_even(attn2d, y2d, hp2d, x2d, d_skip, w_glu, b_glu, w_out, g1, b1, w1, w2, g2, b2, tm):
    m, d = x2d.shape
    dff = w1.shape[1]
    ucol = hp2d.shape[1] // S5_WIDTH - 1
    return pl.pallas_call(
        _tail_even_kernel,
        grid=(m // tm,),
        in_specs=[pl.BlockSpec((tm, A_WIDTH), lambda i: (i, 0)),
                  pl.BlockSpec((tm, S5_WIDTH), lambda i: (i, 0)),
                  pl.BlockSpec((tm, S5_WIDTH), lambda i: (i, ucol)),
                  _resident((1, S5_WIDTH)), _resident((S5_WIDTH, S5_WIDTH)), _resident((1, S5_WIDTH))]
        + _tail_common_specs(tm, d, dff),
        out_specs=pl.BlockSpec((tm, d), lambda i: (i, 0)),
        out_shape=jax.ShapeDtypeStruct((m, d), F32),
        compiler_params=_params("parallel"),
        name="tail_even",
    )(attn2d, y2d, hp2d, _row(d_skip), _mx(w_glu), _row(b_glu), _mx(w_out), x2d,
      _row(g1), _row(b1), _mx(w1), _mx(w2), _row(g2), _row(b2))


def _tail_odd(a2d, x2d, w_out, g1, b1, w1, w2, g2, b2, tm):
    m, d = x2d.shape
    dff = w1.shape[1]
    return pl.pallas_call(
        _tail_odd_kernel,
        grid=(m // tm,),
        in_specs=[pl.BlockSpec((tm, a2d.shape[1]), lambda i: (i, 0))] + _tail_common_specs(tm, d, dff),
        out_specs=pl.BlockSpec((tm, d), lambda i: (i, 0)),
        out_shape=jax.ShapeDtypeStruct((m, d), F32),
        compiler_params=_params("parallel"),
        name="tail_odd",
    )(a2d, _mx(w_out), x2d, _row(g1), _row(b1), _mx(w1), _mx(w2), _row(g2), _row(b2))


GDN_TILE = 256
GDN_GATES_TILE = 2048


def _softplus(t):
    return jnp.maximum(t, 0.0) + jnp.log1p(jnp.exp(-jnp.abs(t)))


def _gdn_prep_body(i, hg, q_ref, k_ref, v_ref, qp_ref, kp_ref, vp_ref, cwq_ref, cwk_ref, cwv_ref,
                   ba_ref, g_ref, gt_ref, u_ref, w_ref, qd_ref, kd_ref, intra_ref, edl_ref,
                   qx_ref, kx_ref, vx_ref, ql_ref, kl_ref, vl_ref):
    assert C_CONV == 4
    t, width = q_ref.shape[1], q_ref.shape[2]
    c = C_CHUNK
    dk = C_HEAD_DIM
    hp = width // dk
    groups = t // SUBLANES

    def conv_silu(main_ref, prev_ref, cw_ref, ext_ref, low_ref):
        x = main_ref[0]
        ext_ref[0:SUBLANES, :] = jnp.zeros((SUBLANES, width), F32)
        ext_ref[SUBLANES:2 * SUBLANES, :] = jnp.where(i > 0, prev_ref[0], 0.0)
        ext_ref[2 * SUBLANES:, :] = x
        ext_rows = t + SUBLANES

        def tap(j):
            return jnp.broadcast_to(cw_ref[j:j + 1, :], (SUBLANES, width))[None]

        def grouped(a):
            return a.reshape(a.shape[0] // SUBLANES, SUBLANES, width)

        x_ext = grouped(ext_ref[SUBLANES:, :])
        x_del = grouped(ext_ref[SUBLANES - 1:SUBLANES - 1 + ext_rows, :])
        low_ref[...] = (tap(1) * x_ext + tap(0) * x_del).reshape(ext_rows, width)
        y = (tap(3) * grouped(x) + tap(2) * x_del[1:]
             + grouped(low_ref[SUBLANES - 2:SUBLANES - 2 + t, :]))
        hy = 0.5 * y.reshape(t, width)
        return hy + hy * jnp.tanh(hy)

    def l2n(z, scale):
        return z * (lax.rsqrt(jnp.sum(z * z, axis=-1, keepdims=True) + L2_EPS) * scale)

    q_all = conv_silu(q_ref, qp_ref, cwq_ref, qx_ref, ql_ref)
    k_all = conv_silu(k_ref, kp_ref, cwk_ref, kx_ref, kl_ref)
    v_all = conv_silu(v_ref, vp_ref, cwv_ref, vx_ref, vl_ref)
    heads = [slice(hh * dk, (hh + 1) * dk) for hh in range(hp)]
    qs = [l2n(q_all[:, cs], dk ** -0.5) for cs in heads]
    ks = [l2n(k_all[:, cs], 1.0) for cs in heads]
    vs = [v_all[:, cs] for cs in heads]

    ba = ba_ref[0]
    lane = lax.broadcasted_iota(jnp.int32, (t, LANES), 1)
    betas, g_cols, g_rows = [], [], []
    for hh in range(hp):
        h = hg * hp + hh
        betas.append(_sigmoid(jnp.sum(jnp.where(lane == h, ba, 0.0), axis=1, keepdims=True)))
        g_cols.append(jnp.sum(jnp.where(lane == C_HEADS + h, g_ref[0], 0.0), axis=1, keepdims=True))
        g_rows.append(gt_ref[0, pl.ds(C_HEADS + h, 1), :])

    units = [(hh, slice(ch * c, (ch + 1) * c)) for hh in range(hp) for ch in range(t // c)]
    _gdn_prep_units(units, qs, ks, vs, betas, g_cols, g_rows, heads, u_ref, w_ref, qd_ref, kd_ref, intra_ref, edl_ref)


def _gdn_prep_units(units, qs, ks, vs, betas, g_cols, g_rows, heads, u_ref, w_ref, qd_ref, kd_ref, intra_ref, edl_ref):
    c = C_CHUNK
    dk = C_HEAD_DIM
    ii = lax.broadcasted_iota(jnp.int32, (c, c), 0)
    jj = lax.broadcasted_iota(jnp.int32, (c, c), 1)
    low = ii >= jj
    gc_cols = [jnp.sum(jnp.where(low, g_rows[hh][:, r], 0.0), axis=1, keepdims=True) for hh, r in units]
    gc_rows = [jnp.sum(jnp.where(ii <= jj, g_cols[hh][r], 0.0), axis=0, keepdims=True) for hh, r in units]
    decays = [jnp.where(low, jnp.exp(jnp.where(low, a - b, 0.0)), 0.0) for a, b in zip(gc_cols, gc_rows)]
    beta_bs = [jnp.broadcast_to(betas[hh][r], (c, dk)) for hh, r in units]
    kbs = [ks[hh][r] * bb for bb, (hh, r) in zip(beta_bs, units)]
    kks = [_dot_nt(kb, ks[hh][r]) for kb, (hh, r) in zip(kbs, units)]
    qks = [_dot_nt(qs[hh][r], ks[hh][r]) for hh, r in units]
    pws = [jnp.where(ii > jj, -(kk * dec), 0.0) for kk, dec in zip(kks, decays)]
    tms = pws
    for _ in range(int(math.log2(c)) - 1):
        pbs = [_mx(pw) for pw in pws]
        pws = [_dot(pb, pb) for pb in pbs]
        pbs = [_mx(pw) for pw in pws]
        tms = [tm + pw + _dot(tm, pb) for tm, pw, pb in zip(tms, pws, pbs)]
    egcs = [jnp.broadcast_to(jnp.exp(a), (c, dk)) for a in gc_cols]
    vbs = [vs[hh][r] * bb for bb, (hh, r) in zip(beta_bs, units)]
    kbes = [kb * e for kb, e in zip(kbs, egcs)]
    us = [vb + _dot(tm, vb) for tm, vb in zip(tms, vbs)]
    ws = [kbe + _dot(tm, kbe) for tm, kbe in zip(tms, kbes)]
    for n, (hh, r) in enumerate(units):
        gl = gc_cols[n][c - 1:c, :]
        cs = heads[hh]
        ch = r.start // c
        u_ref[0, r, cs] = us[n]
        w_ref[0, r, cs] = _mx(ws[n])
        qd_ref[0, r, cs] = _mx(qs[hh][r] * egcs[n])
        kd_ref[0, r, cs] = _mx(ks[hh][r] * jnp.exp(gl - gc_cols[n]))
        intra_ref[0, hh, r, :] = _mx(qks[n] * decays[n])
        edl_ref[0, hh, ch:ch + 1, :] = jnp.broadcast_to(jnp.exp(gl), (1, LANES))


def _gdn_gates_kernel(ba_ref, alog_ref, dtb_ref, g_ref, gt_ref):
    g_all = -jnp.exp(alog_ref[...]) * _softplus(ba_ref[0] + dtb_ref[...])
    g_ref[0] = g_all
    gt_ref[0] = g_all.T


def _gdn_gates(hp3d, alog_row, dtb_row):
    b, s, _ = hp3d.shape
    t = min(GDN_GATES_TILE, s)
    ba_col = 4 * C_WIDTH // LANES
    return pl.pallas_call(
        _gdn_gates_kernel,
        grid=(b, s // t),
        in_specs=[pl.BlockSpec((1, t, LANES), lambda bi, i: (bi, i, ba_col)),
                  pl.BlockSpec((1, LANES), lambda bi, i: (0, 0)),
                  pl.BlockSpec((1, LANES), lambda bi, i: (0, 0))],
        out_specs=[pl.BlockSpec((1, t, LANES), lambda bi, i: (bi, i, 0)),
                   pl.BlockSpec((1, LANES, t), lambda bi, i: (bi, 0, i))],
        out_shape=[jax.ShapeDtypeStruct((b, s, LANES), F32), jax.ShapeDtypeStruct((b, LANES, s), F32)],
        compiler_params=_params("parallel", "parallel"),
        name="gdn_gates",
    )(hp3d, alog_row, dtb_row)


def _gdn_scan_body(chunk0, u_ref, w_ref, qd_ref, kd_ref, intra_ref, edl_ref, gate_ref, nw_ref, o_ref, st_ref):
    c = C_CHUNK
    dk = C_HEAD_DIM
    t = u_ref.shape[1]
    nw = nw_ref[...]
    heads = [slice(h * dk, (h + 1) * dk) for h in range(C_HEADS)]
    for ch in range(t // c):
        r = slice(ch * c, (ch + 1) * c)
        sts = [st_ref[h] for h in range(C_HEADS)]
        wss = [_dot(jnp.concatenate([w_ref[0, r, cs], qd_ref[0, r, cs]], axis=0), st)
               for cs, st in zip(heads, sts)]
        v_news = [u_ref[0, r, cs] - ws[:c] for cs, ws in zip(heads, wss)]
        kvs = [_dot_tn(kd_ref[0, r, cs], v_new) for cs, v_new in zip(heads, v_news)]
        os_ = [ws[c:] + _dot(intra_ref[0, h, r, :], v_new) for h, (ws, v_new) in enumerate(zip(wss, v_news))]
        for h, cs in enumerate(heads):
            st_ref[h] = sts[h] * edl_ref[0, h, pl.ds(chunk0 + ch, 1), 0:1] + kvs[h]
            o = os_[h]
            o = o * lax.rsqrt(jnp.mean(o * o, axis=-1, keepdims=True) + RMS_EPS) * nw
            hg = 0.5 * gate_ref[0, r, cs]
            o_ref[0, r, cs] = o * (hg + hg * jnp.tanh(hg))


def _gdn_kernel(q_ref, k_ref, v_ref, qp_ref, kp_ref, vp_ref, cwq_ref, cwk_ref, cwv_ref, ba_ref, g_ref, gt_ref,
                gate_ref, nw_ref, o_ref, st_ref, qx_ref, kx_ref, vx_ref, ql_ref, kl_ref, vl_ref, *bufs):
    j = pl.program_id(1)
    last = pl.num_programs(1) - 2
    set_a, set_b = bufs[:len(bufs) // 2], bufs[len(bufs) // 2:]

    @pl.when(j == 0)
    def _():
        st_ref[...] = jnp.zeros_like(st_ref)
        for ref in bufs:
            ref[...] = jnp.zeros_like(ref)

    def step(cur, nxt):
        _gdn_prep_body(jnp.minimum(j, last), 0, q_ref, k_ref, v_ref, qp_ref, kp_ref, vp_ref,
                       cwq_ref, cwk_ref, cwv_ref, ba_ref, g_ref, gt_ref, *nxt,
                       qx_ref, kx_ref, vx_ref, ql_ref, kl_ref, vl_ref)
        _gdn_scan_body(0, *cur, gate_ref, nw_ref, o_ref, st_ref)

    @pl.when(j % 2 == 0)
    def _():
        step(set_a, set_b)

    @pl.when(j % 2 == 1)
    def _():
        step(set_b, set_a)


def _gdn(hp3d, conv_w, alog_row, dtb_row, norm_w):
    b, s, _ = hp3d.shape
    ba_col = 4 * C_WIDTH // LANES
    t = GDN_TILE
    nt = s // t
    per = t // SUBLANES
    wd = C_WIDTH
    nc = t // C_CHUNK
    g, gt = _gdn_gates(hp3d, alog_row, dtb_row)

    def prep_tile(j):
        return jnp.minimum(j, nt - 1)

    def scan_tile(j):
        return jnp.maximum(j - 1, 0)

    def main(off):
        return pl.BlockSpec((1, t, wd), lambda bi, j: (bi, prep_tile(j), off))

    def prev(off):
        return pl.BlockSpec((1, SUBLANES, wd), lambda bi, j: (bi, jnp.maximum(prep_tile(j) * per - 1, 0), off))

    def cw(off):
        return pl.BlockSpec((C_CONV, wd), lambda bi, j: (0, off))

    bufs = [pltpu.VMEM((1, t, wd), F32)] + [pltpu.VMEM((1, t, wd), MXU_DTYPE)] * 3 + [
        pltpu.VMEM((1, C_HEADS, t, C_CHUNK), MXU_DTYPE), pltpu.VMEM((1, C_HEADS, max(nc, SUBLANES), LANES), F32)]
    return pl.pallas_call(
        _gdn_kernel,
        grid=(b, nt + 1),
        in_specs=[main(0), main(1), main(2), prev(0), prev(1), prev(2), cw(0), cw(1), cw(2),
                  pl.BlockSpec((1, t, LANES), lambda bi, j: (bi, prep_tile(j), ba_col)),
                  pl.BlockSpec((1, t, LANES), lambda bi, j: (bi, prep_tile(j), 0)),
                  pl.BlockSpec((1, LANES, t), lambda bi, j: (bi, 0, prep_tile(j))),
                  pl.BlockSpec((1, t, wd), lambda bi, j: (bi, scan_tile(j), 3)),
                  pl.BlockSpec((1, C_HEAD_DIM), lambda bi, j: (0, 0))],
        out_specs=pl.BlockSpec((1, t, wd), lambda bi, j: (bi, scan_tile(j), 0)),
        out_shape=jax.ShapeDtypeStruct((b, s, wd), F32),
        scratch_shapes=[pltpu.VMEM((C_HEADS, C_HEAD_DIM, C_HEAD_DIM), F32)]
        + [pltpu.VMEM((t + 2 * SUBLANES, wd), F32)] * 3 + [pltpu.VMEM((t + SUBLANES, wd), F32)] * 3
        + bufs + bufs,
        compiler_params=_params("parallel", "arbitrary"),
        name="gdn",
    )(hp3d, hp3d, hp3d, hp3d, hp3d, hp3d, conv_w, conv_w, conv_w, hp3d, g, gt, hp3d, _row(norm_w))


ROW_TILE = 512


def kernel(x, rel_table, ev_w_in, ev_w_out, s5_lambda_re, s5_lambda_im, s5_b_re, s5_b_im, s5_c_re, s5_c_im, s5_d, s5_log_dt, s5_w_glu, s5_b_glu, od_w_in, od_conv_w, od_a_log, od_dt_bias, od_norm_w, od_w_out, ln_mix_g, ln_mix_b, mlp_w1, mlp_w2, ln_ffn_g, ln_ffn_b):
    bsz, s, d = x.shape
    m = bsz * s
    x2d = x.reshape(m, d)
    table_t = rel_table.T.astype(F32)
    bias = _bias_tiles(table_t)
    for layer in range(DEPTH):
        i = layer // 2
        if layer % 2 == 0:
            hp = _proj(x2d, _mx(ev_w_in[i]), ROW_TILE)
            hp3d = hp.reshape(bsz, s, hp.shape[1])
            attn = _moba(hp3d, bias)
            wx, wy, a_re2, a_im2 = _s5_weights(s5_lambda_re[i], s5_lambda_im[i], s5_b_re[i], s5_b_im[i],
                                               s5_c_re[i], s5_c_im[i], s5_log_dt[i])
            y = _s5_scan(hp3d, wx, wy, a_re2, a_im2)
            x2d = _tail_even(attn.reshape(m, A_WIDTH), y.reshape(m, S5_WIDTH), hp, x2d,
                             s5_d[i], s5_w_glu[i], s5_b_glu[i], ev_w_out[i],
                             ln_mix_g[layer], ln_mix_b[layer], mlp_w1[layer], mlp_w2[layer],
                             ln_ffn_g[layer], ln_ffn_b[layer], ROW_TILE)
        else:
            w_in = od_w_in[i]
            w_all = jnp.pad(w_in, ((0, 0), (0, LANES - 2 * C_HEADS)))
            hp = _proj(x2d, _mx(w_all), ROW_TILE)
            hp3d = hp.reshape(bsz, s, 4 * C_WIDTH + LANES)
            pad8 = jnp.zeros((C_HEADS,), F32)
            padr = jnp.zeros((LANES - 2 * C_HEADS,), F32)
            alog_row = jnp.concatenate([pad8, od_a_log[i].astype(F32), padr]).reshape(1, LANES)
            dtb_row = jnp.concatenate([pad8, od_dt_bias[i].astype(F32), padr]).reshape(1, LANES)
            o = _gdn(hp3d, od_conv_w[i].astype(F32), alog_row, dtb_row, od_norm_w[i])
            x2d = _tail_odd(o.reshape(m, C_WIDTH), x2d, od_w_out[i],
                            ln_mix_g[layer], ln_mix_b[layer], mlp_w1[layer], mlp_w2[layer],
                            ln_ffn_g[layer], ln_ffn_b[layer], ROW_TILE)
    return x2d.reshape(bsz, s, d)
```

```python
import functools
import math

import jax
import jax.numpy as jnp
import numpy as np
from jax import lax
from jax.experimental import pallas as pl
from jax.experimental.pallas import tpu as pltpu

F32 = jnp.float32
MXU_DTYPE = jnp.bfloat16

LANES = 128
SUBLANES = 8
VMEM_LIMIT_BYTES = 56 * 1024 * 1024

A_HEADS = 8
A_HEAD_DIM = 64
A_WIDTH = A_HEADS * A_HEAD_DIM
MOBA_BLOCK = 256
MOBA_TOPK = 3
REL_BUCKETS = 32
REL_MAX_DIST = 128
S5_GROUP = 16
S5_GROUPS = 32
S5_STATE = 64
S5_WIDTH = S5_GROUP * S5_GROUPS
C_HEADS = 8
C_HEAD_DIM = 128
C_WIDTH = C_HEADS * C_HEAD_DIM
C_CONV = 4
C_CHUNK = 64
DEPTH = 2
DN_ALPHA = (2 * DEPTH) ** 0.25
LN_EPS = 1e-5
RMS_EPS = 1e-6
L2_EPS = 1e-6
NEG_INF = -1e30
LOG2E = math.log2(math.e)


def _mx(t):
    return t.astype(MXU_DTYPE)


def _dot(a, b):
    return jnp.dot(_mx(a), _mx(b), preferred_element_type=F32)


def _dot_tn(a, b):
    return lax.dot_general(_mx(a), _mx(b), (((0,), (0,)), ((), ())), preferred_element_type=F32)


def _dot_nt(a, b):
    return lax.dot_general(_mx(a), _mx(b), (((1,), (1,)), ((), ())), preferred_element_type=F32)


def _sigmoid(t):
    return 1.0 / (1.0 + jnp.exp(-t))


def _layer_norm(r, g, b):
    mu = jnp.mean(r, axis=-1, keepdims=True)
    d = r - mu
    var = jnp.mean(d * d, axis=-1, keepdims=True)
    return d * lax.rsqrt(var + LN_EPS) * g + b


def _params(*sem):
    return pltpu.CompilerParams(dimension_semantics=sem, vmem_limit_bytes=VMEM_LIMIT_BYTES)


def _resident(shape):
    return pl.BlockSpec(shape, lambda *_: (0,) * len(shape), pipeline_mode=pl.Buffered(1))


PROJ_CHUNK = 1024


def _proj_kernel(x_ref, w_ref, o_ref):
    xm = _mx(x_ref[...])
    n = w_ref.shape[1]
    for c0 in range(0, n, PROJ_CHUNK):
        c1 = min(c0 + PROJ_CHUNK, n)
        o_ref[:, c0:c1] = jnp.dot(xm, w_ref[:, c0:c1], preferred_element_type=F32)


def _proj(x2d, w, tm):
    m, k = x2d.shape
    n = w.shape[1]
    return pl.pallas_call(
        _proj_kernel,
        grid=(m // tm,),
        in_specs=[pl.BlockSpec((tm, k), lambda i: (i, 0)), _resident((k, n))],
        out_specs=pl.BlockSpec((tm, n), lambda i: (i, 0)),
        out_shape=jax.ShapeDtypeStruct((m, n), F32),
        compiler_params=_params("parallel"),
        name="in_proj",
    )(x2d, w)


def _t5_bucket_np(dist):
    max_exact = REL_BUCKETS // 2
    n = np.maximum(dist, 0)
    nf = np.maximum(n, 1).astype(np.float32)
    large = max_exact + (np.log(nf / np.float32(max_exact)) / np.float32(math.log(REL_MAX_DIST / max_exact))
                         * np.float32(REL_BUCKETS - max_exact)).astype(np.int32)
    large = np.minimum(large, REL_BUCKETS - 1)
    return np.where(n < max_exact, n, large).astype(np.int32)


def _bias_kernel(tab_ref, bkt_ref, o_ref):
    h = pl.program_id(0)
    blk = bkt_ref.shape[-1]
    key = lax.broadcasted_iota(jnp.int32, (blk, blk), 0)
    qry = lax.broadcasted_iota(jnp.int32, (blk, blk), 1)
    far = tab_ref[h, REL_BUCKETS - 1]
    for t in range(2):
        bk = bkt_ref[t]
        acc = jnp.zeros((blk, blk), F32)
        for i in range(REL_BUCKETS):
            acc = jnp.where(bk == i, (tab_ref[h, i] - far) * LOG2E, acc)
        if t == 0:
            acc = jnp.where(qry >= key, acc, NEG_INF)
        o_ref[0, t] = acc


def _bias_tiles(table_t):
    heads = table_t.shape[0]
    offs = np.arange(MOBA_BLOCK)
    dist_own = offs[None, :] - offs[:, None]
    buckets = np.stack([_t5_bucket_np(dist_own), _t5_bucket_np(dist_own + MOBA_BLOCK)])
    return pl.pallas_call(
        _bias_kernel,
        grid=(heads,),
        in_specs=[pl.BlockSpec(memory_space=pltpu.SMEM),
                  pl.BlockSpec((2, MOBA_BLOCK, MOBA_BLOCK), lambda h: (0, 0, 0))],
        out_specs=pl.BlockSpec((1, 2, MOBA_BLOCK, MOBA_BLOCK), lambda h: (h, 0, 0, 0)),
        out_shape=jax.ShapeDtypeStruct((heads, 2, MOBA_BLOCK, MOBA_BLOCK), F32),
        compiler_params=_params("parallel"),
        name="t5_bias",
    )(table_t, jnp.asarray(buckets))


def _moba_kernel(q_ref, k_ref, v_ref, bias_ref, o_ref, kmean_ref, kb_ref, vt_ref,
                 st0_ref, st1_ref, wq_ref, acc_ref):
    st_refs = (st0_ref, st1_ref)
    qb = pl.program_id(2)
    blk = MOBA_BLOCK
    d = A_HEAD_DIM
    nh = LANES // d
    nb = k_ref.shape[1] // blk
    ng = q_ref.shape[2] // LANES
    groups = range(ng)

    @pl.when(qb == 0)
    def _():
        lane = lax.broadcasted_iota(jnp.int32, (blk, LANES - d), 1)
        for g in groups:
            gl = slice(g * LANES, (g + 1) * LANES)
            for j in range(nb + 1):
                onehot = jnp.where(lane == j, 1.0, 0.0).astype(MXU_DTYPE)
                for hh in range(nh):
                    kb_ref[g, hh, j, :, d:LANES] = onehot
            for j in range(nb):
                kj = k_ref[0, j * blk:(j + 1) * blk, gl]
                for hh in range(nh):
                    kb_ref[g, hh, j, :, 0:d] = _mx(kj[:, hh * d:(hh + 1) * d])
                kmean_ref[g, j:j + 1, :] = jnp.mean(kj, axis=0, keepdims=True)
                vt_ref[g, j] = _mx(v_ref[0, j * blk:(j + 1) * blk, gl].T)
            for hh in range(nh):
                kb_ref[g, hh, nb, :, 0:d] = jnp.zeros((blk, d), MXU_DTYPE)
            vt_ref[g, nb] = jnp.zeros((LANES, blk), MXU_DTYPE)

    blk_id = lax.broadcasted_iota(jnp.int32, (nb, blk), 0)
    row = lax.broadcasted_iota(jnp.int32, (LANES - d - nb, blk), 0)
    for g in groups:
        qt = (q_ref[0, :, g * LANES:(g + 1) * LANES] * (d ** -0.5)).T
        for hh in range(nh):
            gate = lax.dot_general(kmean_ref[g, :, hh * d:(hh + 1) * d], qt[hh * d:(hh + 1) * d],
                                   (((1,), (0,)), ((), ())),
                                   precision=lax.Precision.HIGHEST, preferred_element_type=F32)
            gt = jnp.where(blk_id < qb, gate, -jnp.inf)
            sel = jnp.zeros((nb, blk), F32)
            for _ in range(MOBA_TOPK):
                m = jnp.max(gt, axis=0, keepdims=True)
                is_m = jnp.logical_and(gt == m, gt > -jnp.inf)
                idx = jnp.min(jnp.where(is_m, blk_id, nb), axis=0, keepdims=True)
                pick = blk_id == idx
                sel = jnp.where(pick, 1.0, sel)
                gt = jnp.where(pick, -jnp.inf, gt)
            keep = jnp.logical_or(sel > 0.0, blk_id == qb)
            wq_ref[g, hh] = _mx(jnp.concatenate([qt[hh * d:(hh + 1) * d] * LOG2E, jnp.where(keep, 0.0, NEG_INF),
                                                 jnp.where(row == 0, NEG_INF, 0.0)], axis=0))

    def score_stage(slot, ja, jb, with_bias):
        cmaxes = []
        for g in groups:
            for hh in range(nh):
                keys = jnp.concatenate([kb_ref[g, hh, ja], kb_ref[g, hh, jb]], axis=0)
                st = jnp.dot(keys, wq_ref[g, hh], preferred_element_type=F32)
                if with_bias:
                    st = st + jnp.concatenate([bias_ref[g * nh + hh, w] for w in range(2)], axis=0)
                st_refs[slot][g, 0, :, hh * blk:(hh + 1) * blk] = st[:blk]
                st_refs[slot][g, 1, :, hh * blk:(hh + 1) * blk] = st[blk:]
                cmaxes.append(jnp.max(st, axis=0, keepdims=True))
        return tuple(cmaxes)

    def attend_stage(slot, cmaxes, stats, ja, jb):
        new = []
        for g in groups:
            for hh in range(nh):
                cs = slice(hh * blk, (hh + 1) * blk)
                m_new = cmaxes[g * nh + hh]
                if stats is not None:
                    m, l = stats[g * nh + hh]
                    m_new = jnp.maximum(m, m_new)
                ps = [jnp.exp2(st_refs[slot][g, w, :, cs] - m_new) for w in range(2)]
                l_new = jnp.sum(ps[0], axis=0, keepdims=True) + jnp.sum(ps[1], axis=0, keepdims=True)
                vt = jnp.concatenate([vt_ref[g, ja, hh * d:(hh + 1) * d, :],
                                      vt_ref[g, jb, hh * d:(hh + 1) * d, :]], axis=1)
                pv = jnp.dot(vt, jnp.concatenate([_mx(ps[0]), _mx(ps[1])], axis=0),
                             preferred_element_type=F32)
                if stats is None:
                    acc_ref[g, hh] = pv
                    new.append((m_new, l_new))
                else:
                    alpha = jnp.exp2(m - m_new)
                    acc_ref[g, hh] = acc_ref[g, hh] * alpha + pv
                    new.append((m_new, alpha * l + l_new))
        return tuple(new)

    nfar = qb - 1
    npairs = (nfar + 1) // 2

    def far_blocks(i):
        return jnp.where(2 * i < nfar, 2 * i, nb), jnp.where(2 * i + 1 < nfar, 2 * i + 1, nb)

    j_prev = jnp.where(qb > 0, qb - 1, nb)
    cmax0 = score_stage(0, qb, j_prev, True)
    cmax1 = score_stage(1, *far_blocks(0), False)
    init = (attend_stage(0, cmax0, None, qb, j_prev), cmax1)

    def trip(slot, k, carry):
        stats, cmax = carry
        cmax_next = score_stage(1 - slot, *far_blocks(k), False)
        return attend_stage(slot, cmax, stats, *far_blocks(k - 1)), cmax_next

    def trip_dyn(k, carry):
        return lax.cond(k % 2 == 1, functools.partial(trip, 1, k), functools.partial(trip, 0, k), carry)

    stats, _ = lax.fori_loop(1, npairs + 1, trip_dyn, init)
    for g in groups:
        out_t = jnp.concatenate([acc_ref[g, hh] / stats[g * nh + hh][1] for hh in range(nh)], axis=0)
        o_ref[0, :, g * LANES:(g + 1) * LANES] = out_t.T


MOBA_GROUPS = 4


def _moba(hp3d, bias):
    b, s, _ = hp3d.shape
    ng = MOBA_GROUPS
    wd = ng * LANES
    nsteps = A_WIDTH // wd
    hpp = LANES // A_HEAD_DIM
    nb = s // MOBA_BLOCK
    return pl.pallas_call(
        _moba_kernel,
        grid=(b, nsteps, nb),
        in_specs=[pl.BlockSpec((1, MOBA_BLOCK, wd), lambda bi, p, i: (bi, i, p)),
                  pl.BlockSpec((1, s, wd), lambda bi, p, i: (bi, 0, nsteps + p), pipeline_mode=pl.Buffered(1)),
                  pl.BlockSpec((1, s, wd), lambda bi, p, i: (bi, 0, 2 * nsteps + p), pipeline_mode=pl.Buffered(1)),
                  pl.BlockSpec((ng * hpp, 2, MOBA_BLOCK, MOBA_BLOCK), lambda bi, p, i: (p, 0, 0, 0))],
        out_specs=pl.BlockSpec((1, MOBA_BLOCK, wd), lambda bi, p, i: (bi, i, p)),
        out_shape=jax.ShapeDtypeStruct((b, s, A_WIDTH), F32),
        scratch_shapes=[pltpu.VMEM((ng, nb, LANES), F32),
                        pltpu.VMEM((ng, hpp, nb + 1, MOBA_BLOCK, LANES), MXU_DTYPE),
                        pltpu.VMEM((ng, nb + 1, LANES, MOBA_BLOCK), MXU_DTYPE),
                        pltpu.VMEM((ng, 2, MOBA_BLOCK, hpp * MOBA_BLOCK), F32),
                        pltpu.VMEM((ng, 2, MOBA_BLOCK, hpp * MOBA_BLOCK), F32),
                        pltpu.VMEM((ng, hpp, LANES, MOBA_BLOCK), MXU_DTYPE),
                        pltpu.VMEM((ng, hpp, A_HEAD_DIM, MOBA_BLOCK), F32)],
        compiler_params=_params("parallel", "parallel", "arbitrary"),
        name="moba",
    )(hp3d, hp3d, hp3d, bias)


S5_HALF_CH = S5_WIDTH // 2
S5_HALF_ST = (S5_GROUPS // 2) * S5_STATE
S5_TILE = 256
S5_NT = S5_HALF_ST // S5_TILE
S5_STEPS = 64
S5_PITCH = S5_STEPS + SUBLANES
S5_H_PITCH = S5_STEPS + SUBLANES // 2
S5_SLABS = 2 * S5_HALF_ST // LANES


def _s5_kernel(u0_ref, u_ref, wx_ref, wy_ref, a_ref, o_ref, st_ref, x0_scr, x1_scr, h0_scr, h1_scr):
    nb, steps, _ = u_ref.shape
    nseq = 2 * nb
    tile_slabs = S5_TILE // LANES
    nre = S5_SLABS // 2
    i = pl.program_id(0)

    def project_in_items(src_ref, x_scr):
        def item(m, part, n):
            def run():
                slab0 = (part * S5_HALF_ST + n * S5_TILE) // LANES
                for half in range(2):
                    lhs = jnp.concatenate(
                        [_mx(src_ref[bi, :, half * S5_HALF_CH + m * LANES:half * S5_HALF_CH + (m + 1) * LANES])
                         for bi in range(nb)], axis=0)
                    res = jnp.dot(lhs, wx_ref[part, n, half * LANES:(half + 1) * LANES, :],
                                  preferred_element_type=F32)
                    for bi in range(nb):
                        seq = 2 * bi + half
                        for sl in range(tile_slabs):
                            x_scr[slab0 + sl, seq * S5_PITCH:seq * S5_PITCH + steps, :] = (
                                res[bi * steps:(bi + 1) * steps, sl * LANES:(sl + 1) * LANES])
            return run
        return [item(m, part, n) for m in range(S5_NT // 2) for part in range(2) for n in (2 * m, 2 * m + 1)]

    def recur_items(x_scr, h_scr, nitems):
        state = {}

        def item(t0, t1):
            def run():
                a_re = [a_ref[0, :, s * LANES:(s + 1) * LANES] for s in range(nre)]
                a_im = [a_ref[1, :, s * LANES:(s + 1) * LANES] for s in range(nre)]
                if t0 == 0:
                    state["re"] = [st_ref[0, :, s * LANES:(s + 1) * LANES] for s in range(nre)]
                    state["im"] = [st_ref[1, :, s * LANES:(s + 1) * LANES] for s in range(nre)]
                h_re, h_im = state["re"], state["im"]
                for t in range(t0, t1):
                    rows_t = pl.ds(t, nseq, stride=S5_PITCH)
                    for s in range(nre):
                        hr, hi = h_re[s], h_im[s]
                        h_re[s] = a_re[s] * hr - a_im[s] * hi + x_scr[s, rows_t, :]
                        h_im[s] = a_re[s] * hi + a_im[s] * hr + x_scr[nre + s, rows_t, :]
                        h_scr[s, pl.ds(t, nseq, stride=S5_H_PITCH), :] = h_re[s]
                        h_scr[nre + s, pl.ds(t, nseq, stride=S5_H_PITCH), :] = h_im[s]
                if t1 == steps:
                    for s in range(nre):
                        st_ref[0, :, s * LANES:(s + 1) * LANES] = h_re[s]
                        st_ref[1, :, s * LANES:(s + 1) * LANES] = h_im[s]
            return run
        per = steps // nitems
        return [item(k * per, (k + 1) * per) for k in range(nitems)]

    def project_out_items(h_scr):
        def item(m, half):
            def run():
                acc = jnp.zeros((nb * steps, LANES), F32)
                for n in (2 * m, 2 * m + 1):
                    for part in range(2):
                        slab0 = (part * S5_HALF_ST + n * S5_TILE) // LANES
                        h_tile = jnp.concatenate(
                            [jnp.concatenate([h_scr[slab0 + sl,
                                                    (2 * bi + half) * S5_H_PITCH:(2 * bi + half) * S5_H_PITCH + steps, :]
                                              for sl in range(tile_slabs)], axis=1) for bi in range(nb)], axis=0)
                        acc = acc + jnp.dot(_mx(h_tile), wy_ref[part, n, :, half * LANES:(half + 1) * LANES],
                                            preferred_element_type=F32)
                c = half * S5_HALF_CH + m * LANES
                for bi in range(nb):
                    o_ref[bi, :, c:c + LANES] = acc[bi * steps:(bi + 1) * steps, :]
            return run
        return [item(m, half) for m in range(S5_NT // 2) for half in range(2)]

    @pl.when(i == 0)
    def _():
        st_ref[...] = jnp.zeros_like(st_ref)
        h1_scr[...] = jnp.zeros_like(h1_scr)
        for run in project_in_items(u0_ref, x0_scr):
            run()

    def step(x_cur, h_cur, x_nxt, h_prev):
        ins = project_in_items(u_ref, x_nxt)
        outs = project_out_items(h_prev)
        recs = recur_items(x_cur, h_cur, len(ins))
        for k, rec in enumerate(recs):
            rec()
            ins[k]()
            if k % 2 == 1:
                outs[k // 2]()

    @pl.when(i % 2 == 0)
    def _():
        step(x0_scr, h0_scr, x1_scr, h1_scr)

    @pl.when(i % 2 == 1)
    def _():
        step(x1_scr, h1_scr, x0_scr, h0_scr)


def _s5_weights(lam_re, lam_im, b_re, b_im, c_re, c_im, log_dt):
    lr = jnp.minimum(lam_re.astype(F32), -1e-4)
    li = lam_im.astype(F32)
    dt = jnp.exp(log_dt.astype(F32))[:, None]
    mag = jnp.exp(lr * dt)
    a_re = mag * jnp.cos(li * dt)
    a_im = mag * jnp.sin(li * dt)
    den = lr * lr + li * li
    coef_re = ((a_re - 1.0) * lr + a_im * li) / den
    coef_im = (a_im * lr - (a_re - 1.0) * li) / den
    br = b_re.astype(F32)
    bi = b_im.astype(F32)
    bb_re = coef_re[..., None] * br - coef_im[..., None] * bi
    bb_im = coef_re[..., None] * bi + coef_im[..., None] * br
    gh = S5_GROUPS // 2
    eye = jnp.eye(gh, dtype=F32)

    def b_dense(bb):
        t = bb.reshape(2, gh, S5_STATE, S5_GROUP)
        return jnp.einsum('fgph,gk->fghkp', t, eye).reshape(2, S5_HALF_CH, S5_HALF_ST)

    def c_dense(cc):
        t = cc.reshape(2, gh, S5_GROUP, S5_STATE)
        return jnp.einsum('fghp,gk->fgpkh', t, eye).reshape(2, S5_HALF_ST, S5_HALF_CH)

    def x_tiles(bd):
        tiles = []
        for n in range(S5_NT):
            c = (n // 2) * LANES
            tiles.append(jnp.concatenate([bd[0, c:c + LANES, n * S5_TILE:(n + 1) * S5_TILE],
                                          bd[1, c:c + LANES, n * S5_TILE:(n + 1) * S5_TILE]], axis=0))
        return jnp.stack(tiles)

    def y_tiles(cd):
        tiles = []
        for n in range(S5_NT):
            c = (n // 2) * LANES
            tiles.append(jnp.concatenate([cd[0, n * S5_TILE:(n + 1) * S5_TILE, c:c + LANES],
                                          cd[1, n * S5_TILE:(n + 1) * S5_TILE, c:c + LANES]], axis=1))
        return jnp.stack(tiles)

    wx = jnp.stack([x_tiles(b_dense(bb_re)), x_tiles(b_dense(bb_im))]).astype(MXU_DTYPE)
    wy = jnp.stack([y_tiles(c_dense(c_re.astype(F32))), y_tiles(c_dense(-c_im.astype(F32)))]).astype(MXU_DTYPE)
    return wx, wy, a_re.reshape(2, S5_HALF_ST), a_im.reshape(2, S5_HALF_ST)


def _s5_scan(hp3d, wx, wy, a_re2, a_im2):
    b, s, width = hp3d.shape
    nseq = 2 * b
    assert nseq == SUBLANES
    sel = jnp.arange(nseq) % 2
    a = jnp.stack([a_re2[sel], a_im2[sel]])
    scr = pltpu.VMEM((S5_SLABS, nseq * S5_PITCH, LANES), F32)
    hscr = pltpu.VMEM((S5_SLABS, nseq * S5_H_PITCH, LANES), F32)
    last = s // S5_STEPS - 1
    ucol = width // S5_WIDTH - 1
    return pl.pallas_call(
        _s5_kernel,
        grid=(last + 2,),
        in_specs=[pl.BlockSpec((b, S5_STEPS, S5_WIDTH), lambda i: (0, jnp.minimum(i, last), ucol)),
                  pl.BlockSpec((b, S5_STEPS, S5_WIDTH), lambda i: (0, jnp.minimum(i + 1, last), ucol)),
                  _resident(wx.shape), _resident(wy.shape), _resident(a.shape)],
        out_specs=pl.BlockSpec((b, S5_STEPS, S5_WIDTH), lambda i: (0, jnp.maximum(i - 1, 0), 0)),
        out_shape=jax.ShapeDtypeStruct((b, s, S5_WIDTH), F32),
        scratch_shapes=[pltpu.VMEM((2, nseq, S5_HALF_ST), F32), scr, scr, hscr, hscr],
        compiler_params=_params("arbitrary"),
        name="s5_scan",
    )(hp3d, hp3d, wx, wy, a)


MLP_CHUNK = 1024


def _tail_body(mix, x_ref, g1_ref, b1_ref, w1_ref, w2_ref, g2_ref, b2_ref, o_ref):
    x1 = _layer_norm(DN_ALPHA * x_ref[...] + mix, g1_ref[...], b1_ref[...])
    x1m = _mx(x1)
    acc = jnp.zeros(x1.shape, F32)
    for f in range(w1_ref.shape[1] // MLP_CHUNK):
        hid = jnp.dot(x1m, w1_ref[:, f * MLP_CHUNK:(f + 1) * MLP_CHUNK], preferred_element_type=F32)
        hid = jnp.square(jnp.maximum(hid, 0.0))
        acc = acc + jnp.dot(_mx(hid), w2_ref[f * MLP_CHUNK:(f + 1) * MLP_CHUNK, :],
                            preferred_element_type=F32)
    o_ref[...] = _layer_norm(DN_ALPHA * x1 + acc, g2_ref[...], b2_ref[...])


def _tail_even_kernel(attn_ref, y_ref, u_ref, dsk_ref, wg_ref, bg_ref, wo_ref, x_ref,
                      g1_ref, b1_ref, w1_ref, w2_ref, g2_ref, b2_ref, o_ref):
    y = y_ref[...] + dsk_ref[...] * u_ref[...]
    g = 0.5 * y * (1.0 + jnp.tanh(math.sqrt(2.0 / math.pi) * (y + 0.044715 * (y * y * y))))
    ssm = g * _sigmoid(jnp.dot(_mx(g), wg_ref[...], preferred_element_type=F32) + bg_ref[...])
    ka = attn_ref.shape[1]
    mix = (jnp.dot(_mx(attn_ref[...]), wo_ref[0:ka, :], preferred_element_type=F32)
           + jnp.dot(_mx(ssm), wo_ref[ka:, :], preferred_element_type=F32))
    _tail_body(mix, x_ref, g1_ref, b1_ref, w1_ref, w2_ref, g2_ref, b2_ref, o_ref)


def _tail_odd_kernel(a_ref, wo_ref, x_ref, g1_ref, b1_ref, w1_ref, w2_ref, g2_ref, b2_ref, o_ref):
    mix = jnp.dot(_mx(a_ref[...]), wo_ref[...], preferred_element_type=F32)
    _tail_body(mix, x_ref, g1_ref, b1_ref, w1_ref, w2_ref, g2_ref, b2_ref, o_ref)


def _row(v):
    return v.reshape(1, -1).astype(F32)


def _tail_common_specs(tm, d, dff):
    return [_resident((d, d)),
            pl.BlockSpec((tm, d), lambda i: (i, 0)),
            _resident((1, d)), _resident((1, d)),
            _resident((d, dff)), _resident((dff, d)),
            _resident((1, d)), _resident((1, d))]


def _tail_even(attn2d, y2d, hp2d, x2d, d_skip, w_glu, b_glu, w_out, g1, b1, w1, w2, g2, b2, tm):
    m, d = x2d.shape
    dff = w1.shape[1]
    ucol = hp2d.shape[1] // S5_WIDTH - 1
    return pl.pallas_call(
        _tail_even_kernel,
        grid=(m // tm,),
        in_specs=[pl.BlockSpec((tm, A_WIDTH), lambda i: (i, 0)),
                  pl.BlockSpec((tm, S5_WIDTH), lambda i: (i, 0)),
                  pl.BlockSpec((tm, S5_WIDTH), lambda i: (i, ucol)),
                  _resident((1, S5_WIDTH)), _resident((S5_WIDTH, S5_WIDTH)), _resident((1, S5_WIDTH))]
        + _tail_common_specs(tm, d, dff),
        out_specs=pl.BlockSpec((tm, d), lambda i: (i, 0)),
        out_shape=jax.ShapeDtypeStruct((m, d), F32),
        compiler_params=_params("parallel"),
        name="tail_even",
    )(attn2d, y2d, hp2d, _row(d_skip), _mx(w_glu), _row(b_glu), _mx(w_out), x2d,
      _row(g1), _row(b1), _mx(w1), _mx(w2), _row(g2), _row(b2))


def _tail_odd(a2d, x2d, w_out, g1, b1, w1, w2, g2, b2, tm):
    m, d = x2d.shape
    dff = w1.shape[1]
    return pl.pallas_call(
        _tail_odd_kernel,
        grid=(m // tm,),
        in_specs=[pl.BlockSpec((tm, a2d.shape[1]), lambda i: (i, 0))] + _tail_common_specs(tm, d, dff),
        out_specs=pl.BlockSpec((tm, d), lambda i: (i, 0)),
        out_shape=jax.ShapeDtypeStruct((m, d), F32),
        compiler_params=_params("parallel"),
        name="tail_odd",
    )(a2d, _mx(w_out), x2d, _row(g1), _row(b1), _mx(w1), _mx(w2), _row(g2), _row(b2))


GDN_TILE = 256
GDN_GATES_TILE = 2048


def _softplus(t):
    return jnp.maximum(t, 0.0) + jnp.log1p(jnp.exp(-jnp.abs(t)))


def _gdn_prep_body(i, hg, q_ref, k_ref, v_ref, qp_ref, kp_ref, vp_ref, cwq_ref, cwk_ref, cwv_ref,
                   ba_ref, g_ref, gt_ref, u_ref, w_ref, qd_ref, kd_ref, intra_ref, edl_ref,
                   qx_ref, kx_ref, vx_ref, ql_ref, kl_ref, vl_ref):
    assert C_CONV == 4
    t, width = q_ref.shape[1], q_ref.shape[2]
    c = C_CHUNK
    dk = C_HEAD_DIM
    hp = width // dk
    groups = t // SUBLANES

    def conv_silu(main_ref, prev_ref, cw_ref, ext_ref, low_ref):
        x = main_ref[0]
        ext_ref[0:SUBLANES, :] = jnp.zeros((SUBLANES, width), F32)
        ext_ref[SUBLANES:2 * SUBLANES, :] = jnp.where(i > 0, prev_ref[0], 0.0)
        ext_ref[2 * SUBLANES:, :] = x
        ext_rows = t + SUBLANES

        def tap(j):
            return jnp.broadcast_to(cw_ref[j:j + 1, :], (SUBLANES, width))[None]

        def grouped(a):
            return a.reshape(a.shape[0] // SUBLANES, SUBLANES, width)

        x_ext = grouped(ext_ref[SUBLANES:, :])
        x_del = grouped(ext_ref[SUBLANES - 1:SUBLANES - 1 + ext_rows, :])
        low_ref[...] = (tap(1) * x_ext + tap(0) * x_del).reshape(ext_rows, width)
        y = (tap(3) * grouped(x) + tap(2) * x_del[1:]
             + grouped(low_ref[SUBLANES - 2:SUBLANES - 2 + t, :]))
        hy = 0.5 * y.reshape(t, width)
        return hy + hy * jnp.tanh(hy)

    def l2n(z, scale):
        return z * (lax.rsqrt(jnp.sum(z * z, axis=-1, keepdims=True) + L2_EPS) * scale)

    q_all = conv_silu(q_ref, qp_ref, cwq_ref, qx_ref, ql_ref)
    k_all = conv_silu(k_ref, kp_ref, cwk_ref, kx_ref, kl_ref)
    v_all = conv_silu(v_ref, vp_ref, cwv_ref, vx_ref, vl_ref)
    heads = [slice(hh * dk, (hh + 1) * dk) for hh in range(hp)]
    qs = [l2n(q_all[:, cs], dk ** -0.5) for cs in heads]
    ks = [l2n(k_all[:, cs], 1.0) for cs in heads]
    vs = [v_all[:, cs] for cs in heads]

    ba = ba_ref[0]
    lane = lax.broadcasted_iota(jnp.int32, (t, LANES), 1)
    betas, g_cols, g_rows = [], [], []
    for hh in range(hp):
        h = hg * hp + hh
        betas.append(_sigmoid(jnp.sum(jnp.where(lane == h, ba, 0.0), axis=1, keepdims=True)))
        g_cols.append(jnp.sum(jnp.where(lane == C_HEADS + h, g_ref[0], 0.0), axis=1, keepdims=True))
        g_rows.append(gt_ref[0, pl.ds(C_HEADS + h, 1), :])

    units = [(hh, slice(ch * c, (ch + 1) * c)) for hh in range(hp) for ch in range(t // c)]
    _gdn_prep_units(units, qs, ks, vs, betas, g_cols, g_rows, heads, u_ref, w_ref, qd_ref, kd_ref, intra_ref, edl_ref)


def _gdn_prep_units(units, qs, ks, vs, betas, g_cols, g_rows, heads, u_ref, w_ref, qd_ref, kd_ref, intra_ref, edl_ref):
    c = C_CHUNK
    dk = C_HEAD_DIM
    assert 2 * c == LANES and len(units) % 2 == 0
    ii = lax.broadcasted_iota(jnp.int32, (c, c), 0)
    jj = lax.broadcasted_iota(jnp.int32, (c, c), 1)
    row = lax.broadcasted_iota(jnp.int32, (c, 2 * c), 0)
    lane = lax.broadcasted_iota(jnp.int32, (c, 2 * c), 1)
    left = lane < c
    col = jnp.where(left, lane, lane - c)
    low = row >= col
    pairs = [(2 * n, 2 * n + 1) for n in range(len(units) // 2)]

    def side_by_side(a, b):
        za, zb = jnp.zeros_like(a), jnp.zeros_like(b)
        return jnp.concatenate([jnp.concatenate([a, zb], axis=1), jnp.concatenate([za, b], axis=1)], axis=0)

    def block_diag(p):
        return jnp.concatenate([jnp.where(left, p, 0.0), jnp.where(left, 0.0, p)], axis=0)

    gc_cols = [jnp.sum(jnp.where(ii >= jj, g_rows[hh][:, r], 0.0), axis=1, keepdims=True) for hh, r in units]
    gc_rows = [jnp.sum(jnp.where(ii <= jj, g_cols[hh][r], 0.0), axis=0, keepdims=True) for hh, r in units]
    decays = [jnp.where(low, jnp.exp(jnp.where(low, jnp.where(left, gc_cols[a], gc_cols[b])
                                               - jnp.concatenate([gc_rows[a], gc_rows[b]], axis=1), 0.0)), 0.0)
              for a, b in pairs]
    beta_bs = [jnp.broadcast_to(betas[hh][r], (c, dk)) for hh, r in units]
    kbs = [ks[hh][r] * bb for bb, (hh, r) in zip(beta_bs, units)]
    kdiag = [_mx(side_by_side(ks[units[a][0]][units[a][1]], ks[units[b][0]][units[b][1]])) for a, b in pairs]
    kks = [_dot_nt(jnp.concatenate([kbs[a], kbs[b]], axis=1), kd) for (a, b), kd in zip(pairs, kdiag)]
    qks = [_dot_nt(jnp.concatenate([qs[units[a][0]][units[a][1]], qs[units[b][0]][units[b][1]]], axis=1), kd)
           for (a, b), kd in zip(pairs, kdiag)]
    pws = [jnp.where(row > col, -(kk * dec), 0.0) for kk, dec in zip(kks, decays)]
    tms = pws
    pds = [_mx(block_diag(pw)) for pw in pws]
    for _ in range(int(math.log2(c)) - 1):
        pws = [jnp.dot(_mx(pw), pd, preferred_element_type=F32) for pw, pd in zip(pws, pds)]
        pds = [_mx(block_diag(pw)) for pw in pws]
        tms = [tm + pw + jnp.dot(_mx(tm), pd, preferred_element_type=F32) for tm, pw, pd in zip(tms, pws, pds)]
    egcs = [jnp.broadcast_to(jnp.exp(a), (c, dk)) for a in gc_cols]
    vbs = [vs[hh][r] * bb for bb, (hh, r) in zip(beta_bs, units)]
    kbes = [kb * e for kb, e in zip(kbs, egcs)]
    tbs = [_mx(tm) for tm in tms]
    tvs = [jnp.dot(tb, _mx(side_by_side(vbs[a], vbs[b])), preferred_element_type=F32) for tb, (a, b) in zip(tbs, pairs)]
    tks = [jnp.dot(tb, _mx(side_by_side(kbes[a], kbes[b])), preferred_element_type=F32) for tb, (a, b) in zip(tbs, pairs)]
    for n, (hh, r) in enumerate(units):
        gl = gc_cols[n][c - 1:c, :]
        cs = heads[hh]
        ch = r.start // c
        half = slice((n % 2) * dk, (n % 2 + 1) * dk)
        u_ref[0, r, cs] = vbs[n] + tvs[n // 2][:, half]
        w_ref[0, r, cs] = _mx(kbes[n] + tks[n // 2][:, half])
        qd_ref[0, r, cs] = _mx(qs[hh][r] * egcs[n])
        kd_ref[0, r, cs] = _mx(ks[hh][r] * jnp.exp(gl - gc_cols[n]))
        intra_ref[0, hh, r, :] = _mx((qks[n // 2] * decays[n // 2])[:, (n % 2) * c:(n % 2 + 1) * c])
        edl_ref[0, hh, ch:ch + 1, :] = jnp.broadcast_to(jnp.exp(gl), (1, LANES))


def _gdn_gates_kernel(ba_ref, alog_ref, dtb_ref, g_ref, gt_ref):
    g_all = -jnp.exp(alog_ref[...]) * _softplus(ba_ref[0] + dtb_ref[...])
    g_ref[0] = g_all
    gt_ref[0] = g_all.T


def _gdn_gates(hp3d, alog_row, dtb_row):
    b, s, _ = hp3d.shape
    t = min(GDN_GATES_TILE, s)
    ba_col = 4 * C_WIDTH // LANES
    return pl.pallas_call(
        _gdn_gates_kernel,
        grid=(b, s // t),
        in_specs=[pl.BlockSpec((1, t, LANES), lambda bi, i: (bi, i, ba_col)),
                  pl.BlockSpec((1, LANES), lambda bi, i: (0, 0)),
                  pl.BlockSpec((1, LANES), lambda bi, i: (0, 0))],
        out_specs=[pl.BlockSpec((1, t, LANES), lambda bi, i: (bi, i, 0)),
                   pl.BlockSpec((1, LANES, t), lambda bi, i: (bi, 0, i))],
        out_shape=[jax.ShapeDtypeStruct((b, s, LANES), F32), jax.ShapeDtypeStruct((b, LANES, s), F32)],
        compiler_params=_params("parallel", "parallel"),
        name="gdn_gates",
    )(hp3d, alog_row, dtb_row)


def _gdn_scan_body(chunk0, u_ref, w_ref, qd_ref, kd_ref, intra_ref, edl_ref, gate_ref, nw_ref, o_ref, st_ref):
    c = C_CHUNK
    dk = C_HEAD_DIM
    t = u_ref.shape[1]
    nw = nw_ref[...]
    heads = [slice(h * dk, (h + 1) * dk) for h in range(C_HEADS)]
    for ch in range(t // c):
        r = slice(ch * c, (ch + 1) * c)
        sts = [st_ref[h] for h in range(C_HEADS)]
        wss = [_dot(jnp.concatenate([w_ref[0, r, cs], qd_ref[0, r, cs]], axis=0), st)
               for cs, st in zip(heads, sts)]
        v_news = [u_ref[0, r, cs] - ws[:c] for cs, ws in zip(heads, wss)]
        kvs = [_dot_tn(kd_ref[0, r, cs], v_new) for cs, v_new in zip(heads, v_news)]
        os_ = [ws[c:] + _dot(intra_ref[0, h, r, :], v_new) for h, (ws, v_new) in enumerate(zip(wss, v_news))]
        for h, cs in enumerate(heads):
            st_ref[h] = sts[h] * edl_ref[0, h, pl.ds(chunk0 + ch, 1), 0:1] + kvs[h]
            o = os_[h]
            o = o * lax.rsqrt(jnp.mean(o * o, axis=-1, keepdims=True) + RMS_EPS) * nw
            hg = 0.5 * gate_ref[0, r, cs]
            o_ref[0, r, cs] = o * (hg + hg * jnp.tanh(hg))


def _gdn_kernel(q_ref, k_ref, v_ref, qp_ref, kp_ref, vp_ref, cwq_ref, cwk_ref, cwv_ref, ba_ref, g_ref, gt_ref,
                gate_ref, nw_ref, o_ref, st_ref, qx_ref, kx_ref, vx_ref, ql_ref, kl_ref, vl_ref, *bufs):
    j = pl.program_id(1)
    last = pl.num_programs(1) - 2
    set_a, set_b = bufs[:len(bufs) // 2], bufs[len(bufs) // 2:]

    @pl.when(j == 0)
    def _():
        st_ref[...] = jnp.zeros_like(st_ref)
        for ref in bufs:
            ref[...] = jnp.zeros_like(ref)

    def step(cur, nxt):
        _gdn_prep_body(jnp.minimum(j, last), 0, q_ref, k_ref, v_ref, qp_ref, kp_ref, vp_ref,
                       cwq_ref, cwk_ref, cwv_ref, ba_ref, g_ref, gt_ref, *nxt,
                       qx_ref, kx_ref, vx_ref, ql_ref, kl_ref, vl_ref)
        _gdn_scan_body(0, *cur, gate_ref, nw_ref, o_ref, st_ref)

    @pl.when(j % 2 == 0)
    def _():
        step(set_a, set_b)

    @pl.when(j % 2 == 1)
    def _():
        step(set_b, set_a)


def _gdn(hp3d, conv_w, alog_row, dtb_row, norm_w):
    b, s, _ = hp3d.shape
    ba_col = 4 * C_WIDTH // LANES
    t = GDN_TILE
    nt = s // t
    per = t // SUBLANES
    wd = C_WIDTH
    nc = t // C_CHUNK
    g, gt = _gdn_gates(hp3d, alog_row, dtb_row)

    def prep_tile(j):
        return jnp.minimum(j, nt - 1)

    def scan_tile(j):
        return jnp.maximum(j - 1, 0)

    def main(off):
        return pl.BlockSpec((1, t, wd), lambda bi, j: (bi, prep_tile(j), off))

    def prev(off):
        return pl.BlockSpec((1, SUBLANES, wd), lambda bi, j: (bi, jnp.maximum(prep_tile(j) * per - 1, 0), off))

    def cw(off):
        return pl.BlockSpec((C_CONV, wd), lambda bi, j: (0, off))

    bufs = [pltpu.VMEM((1, t, wd), F32)] + [pltpu.VMEM((1, t, wd), MXU_DTYPE)] * 3 + [
        pltpu.VMEM((1, C_HEADS, t, C_CHUNK), MXU_DTYPE), pltpu.VMEM((1, C_HEADS, max(nc, SUBLANES), LANES), F32)]
    return pl.pallas_call(
        _gdn_kernel,
        grid=(b, nt + 1),
        in_specs=[main(0), main(1), main(2), prev(0), prev(1), prev(2), cw(0), cw(1), cw(2),
                  pl.BlockSpec((1, t, LANES), lambda bi, j: (bi, prep_tile(j), ba_col)),
                  pl.BlockSpec((1, t, LANES), lambda bi, j: (bi, prep_tile(j), 0)),
                  pl.BlockSpec((1, LANES, t), lambda bi, j: (bi, 0, prep_tile(j))),
                  pl.BlockSpec((1, t, wd), lambda bi, j: (bi, scan_tile(j), 3)),
                  pl.BlockSpec((1, C_HEAD_DIM), lambda bi, j: (0, 0))],
        out_specs=pl.BlockSpec((1, t, wd), lambda bi, j: (bi, scan_tile(j), 0)),
        out_shape=jax.ShapeDtypeStruct((b, s, wd), F32),
        scratch_shapes=[pltpu.VMEM((C_HEADS, C_HEAD_DIM, C_HEAD_DIM), F32)]
        + [pltpu.VMEM((t + 2 * SUBLANES, wd), F32)] * 3 + [pltpu.VMEM((t + SUBLANES, wd), F32)] * 3
        + bufs + bufs,
        compiler_params=_params("parallel", "arbitrary"),
        name="gdn",
    )(hp3d, hp3d, hp3d, hp3d, hp3d, hp3d, conv_w, conv_w, conv_w, hp3d, g, gt, hp3d, _row(norm_w))


ROW_TILE = 512


def kernel(x, rel_table, ev_w_in, ev_w_out, s5_lambda_re, s5_lambda_im, s5_b_re, s5_b_im, s5_c_re, s5_c_im, s5_d, s5_log_dt, s5_w_glu, s5_b_glu, od_w_in, od_conv_w, od_a_log, od_dt_bias, od_norm_w, od_w_out, ln_mix_g, ln_mix_b, mlp_w1, mlp_w2, ln_ffn_g, ln_ffn_b):
    bsz, s, d = x.shape
    m = bsz * s
    x2d = x.reshape(m, d)
    table_t = rel_table.T.astype(F32)
    bias = _bias_tiles(table_t)
    for layer in range(DEPTH):
        i = layer // 2
        if layer % 2 == 0:
            hp = _proj(x2d, _mx(ev_w_in[i]), ROW_TILE)
            hp3d = hp.reshape(bsz, s, hp.shape[1])
            attn = _moba(hp3d, bias)
            wx, wy, a_re2, a_im2 = _s5_weights(s5_lambda_re[i], s5_lambda_im[i], s5_b_re[i], s5_b_im[i],
                                               s5_c_re[i], s5_c_im[i], s5_log_dt[i])
            y = _s5_scan(hp3d, wx, wy, a_re2, a_im2)
            x2d = _tail_even(attn.reshape(m, A_WIDTH), y.reshape(m, S5_WIDTH), hp, x2d,
                             s5_d[i], s5_w_glu[i], s5_b_glu[i], ev_w_out[i],
                             ln_mix_g[layer], ln_mix_b[layer], mlp_w1[layer], mlp_w2[layer],
                             ln_ffn_g[layer], ln_ffn_b[layer], ROW_TILE)
        else:
            w_in = od_w_in[i]
            w_all = jnp.pad(w_in, ((0, 0), (0, LANES - 2 * C_HEADS)))
            hp = _proj(x2d, _mx(w_all), ROW_TILE)
            hp3d = hp.reshape(bsz, s, 4 * C_WIDTH + LANES)
            pad8 = jnp.zeros((C_HEADS,), F32)
            padr = jnp.zeros((LANES - 2 * C_HEADS,), F32)
            alog_row = jnp.concatenate([pad8, od_a_log[i].astype(F32), padr]).reshape(1, LANES)
            dtb_row = jnp.concatenate([pad8, od_dt_bias[i].astype(F32), padr]).reshape(1, LANES)
            o = _gdn(hp3d, od_conv_w[i].astype(F32), alog_row, dtb_row, od_norm_w[i])
            x2d = _tail_odd(o.reshape(m, C_WIDTH), x2d, od_w_out[i],
                            ln_mix_g[layer], ln_mix_b[layer], mlp_w1[layer], mlp_w2[layer],
                            ln_ffn_g[layer], ln_ffn_b[layer], ROW_TILE)
    return x2d.reshape(bsz, s, d)
```

```python
import functools
import math

import jax
import jax.numpy as jnp
import numpy as np
from jax import lax
from jax.experimental import pallas as pl
from jax.experimental.pallas import tpu as pltpu

F32 = jnp.float32
MXU_DTYPE = jnp.bfloat16

LANES = 128
SUBLANES = 8
VMEM_LIMIT_BYTES = 56 * 1024 * 1024

A_HEADS = 8
A_HEAD_DIM = 64
A_WIDTH = A_HEADS * A_HEAD_DIM
MOBA_BLOCK = 256
MOBA_TOPK = 3
REL_BUCKETS = 32
REL_MAX_DIST = 128
S5_GROUP = 16
S5_GROUPS = 32
S5_STATE = 64
S5_WIDTH = S5_GROUP * S5_GROUPS
C_HEADS = 8
C_HEAD_DIM = 128
C_WIDTH = C_HEADS * C_HEAD_DIM
C_CONV = 4
C_CHUNK = 64
DEPTH = 2
DN_ALPHA = (2 * DEPTH) ** 0.25
LN_EPS = 1e-5
RMS_EPS = 1e-6
L2_EPS = 1e-6
NEG_INF = -1e30
LOG2E = math.log2(math.e)


def _mx(t):
    return t.astype(MXU_DTYPE)


def _dot(a, b):
    return jnp.dot(_mx(a), _mx(b), preferred_element_type=F32)


def _dot_tn(a, b):
    return lax.dot_general(_mx(a), _mx(b), (((0,), (0,)), ((), ())), preferred_element_type=F32)


def _dot_nt(a, b):
    return lax.dot_general(_mx(a), _mx(b), (((1,), (1,)), ((), ())), preferred_element_type=F32)


def _sigmoid(t):
    return 1.0 / (1.0 + jnp.exp(-t))


def _layer_norm(r, g, b):
    mu = jnp.mean(r, axis=-1, keepdims=True)
    d = r - mu
    var = jnp.mean(d * d, axis=-1, keepdims=True)
    return d * lax.rsqrt(var + LN_EPS) * g + b


def _params(*sem):
    return pltpu.CompilerParams(dimension_semantics=sem, vmem_limit_bytes=VMEM_LIMIT_BYTES)


def _resident(shape):
    return pl.BlockSpec(shape, lambda *_: (0,) * len(shape), pipeline_mode=pl.Buffered(1))


PROJ_CHUNK = 1024


def _proj_kernel(x_ref, w_ref, o_ref):
    xm = _mx(x_ref[...])
    n = w_ref.shape[1]
    for c0 in range(0, n, PROJ_CHUNK):
        c1 = min(c0 + PROJ_CHUNK, n)
        o_ref[:, c0:c1] = jnp.dot(xm, w_ref[:, c0:c1], preferred_element_type=F32)


def _proj(x2d, w, tm):
    m, k = x2d.shape
    n = w.shape[1]
    return pl.pallas_call(
        _proj_kernel,
        grid=(m // tm,),
        in_specs=[pl.BlockSpec((tm, k), lambda i: (i, 0)), _resident((k, n))],
        out_specs=pl.BlockSpec((tm, n), lambda i: (i, 0)),
        out_shape=jax.ShapeDtypeStruct((m, n), F32),
        compiler_params=_params("parallel"),
        name="in_proj",
    )(x2d, w)


def _t5_bucket_np(dist):
    max_exact = REL_BUCKETS // 2
    n = np.maximum(dist, 0)
    nf = np.maximum(n, 1).astype(np.float32)
    large = max_exact + (np.log(nf / np.float32(max_exact)) / np.float32(math.log(REL_MAX_DIST / max_exact))
                         * np.float32(REL_BUCKETS - max_exact)).astype(np.int32)
    large = np.minimum(large, REL_BUCKETS - 1)
    return np.where(n < max_exact, n, large).astype(np.int32)


def _bias_kernel(tab_ref, bkt_ref, o_ref):
    h = pl.program_id(0)
    blk = bkt_ref.shape[-1]
    key = lax.broadcasted_iota(jnp.int32, (blk, blk), 0)
    qry = lax.broadcasted_iota(jnp.int32, (blk, blk), 1)
    far = tab_ref[h, REL_BUCKETS - 1]
    for t in range(2):
        bk = bkt_ref[t]
        acc = jnp.zeros((blk, blk), F32)
        for i in range(REL_BUCKETS):
            acc = jnp.where(bk == i, (tab_ref[h, i] - far) * LOG2E, acc)
        if t == 0:
            acc = jnp.where(qry >= key, acc, NEG_INF)
        o_ref[0, t] = acc


def _bias_tiles(table_t):
    heads = table_t.shape[0]
    offs = np.arange(MOBA_BLOCK)
    dist_own = offs[None, :] - offs[:, None]
    buckets = np.stack([_t5_bucket_np(dist_own), _t5_bucket_np(dist_own + MOBA_BLOCK)])
    return pl.pallas_call(
        _bias_kernel,
        grid=(heads,),
        in_specs=[pl.BlockSpec(memory_space=pltpu.SMEM),
                  pl.BlockSpec((2, MOBA_BLOCK, MOBA_BLOCK), lambda h: (0, 0, 0))],
        out_specs=pl.BlockSpec((1, 2, MOBA_BLOCK, MOBA_BLOCK), lambda h: (h, 0, 0, 0)),
        out_shape=jax.ShapeDtypeStruct((heads, 2, MOBA_BLOCK, MOBA_BLOCK), F32),
        compiler_params=_params("parallel"),
        name="t5_bias",
    )(table_t, jnp.asarray(buckets))


def _moba_kernel(q_ref, k_ref, v_ref, bias_ref, o_ref, kmean_ref, kb_ref, vt_ref,
                 st0_ref, st1_ref, wq_ref, acc_ref):
    st_refs = (st0_ref, st1_ref)
    qb = pl.program_id(2)
    blk = MOBA_BLOCK
    d = A_HEAD_DIM
    nh = LANES // d
    nb = k_ref.shape[1] // blk
    ng = q_ref.shape[2] // LANES
    groups = range(ng)

    @pl.when(qb == 0)
    def _():
        lane = lax.broadcasted_iota(jnp.int32, (blk, LANES - d), 1)
        for g in groups:
            gl = slice(g * LANES, (g + 1) * LANES)
            for j in range(nb + 1):
                onehot = jnp.where(lane == j, 1.0, 0.0).astype(MXU_DTYPE)
                for hh in range(nh):
                    kb_ref[g, hh, j, :, d:LANES] = onehot
            for j in range(nb):
                kj = k_ref[0, j * blk:(j + 1) * blk, gl]
                for hh in range(nh):
                    kb_ref[g, hh, j, :, 0:d] = _mx(kj[:, hh * d:(hh + 1) * d])
                kmean_ref[g, j:j + 1, :] = jnp.mean(kj, axis=0, keepdims=True)
                vt_ref[g, j] = _mx(v_ref[0, j * blk:(j + 1) * blk, gl].T)
            for hh in range(nh):
                kb_ref[g, hh, nb, :, 0:d] = jnp.zeros((blk, d), MXU_DTYPE)
            vt_ref[g, nb] = jnp.zeros((LANES, blk), MXU_DTYPE)

    blk_id = lax.broadcasted_iota(jnp.int32, (nb, blk), 0)
    row = lax.broadcasted_iota(jnp.int32, (LANES - d - nb, blk), 0)
    for g in groups:
        qt = (q_ref[0, :, g * LANES:(g + 1) * LANES] * (d ** -0.5)).T
        for hh in range(nh):
            gate = lax.dot_general(kmean_ref[g, :, hh * d:(hh + 1) * d], qt[hh * d:(hh + 1) * d],
                                   (((1,), (0,)), ((), ())),
                                   precision=lax.Precision.HIGHEST, preferred_element_type=F32)
            gt = jnp.where(blk_id < qb, gate, -jnp.inf)
            sel = jnp.zeros((nb, blk), F32)
            for _ in range(MOBA_TOPK):
                m = jnp.max(gt, axis=0, keepdims=True)
                is_m = jnp.logical_and(gt == m, gt > -jnp.inf)
                idx = jnp.min(jnp.where(is_m, blk_id, nb), axis=0, keepdims=True)
                pick = blk_id == idx
                sel = jnp.where(pick, 1.0, sel)
                gt = jnp.where(pick, -jnp.inf, gt)
            keep = jnp.logical_or(sel > 0.0, blk_id == qb)
            wq_ref[g, hh] = _mx(jnp.concatenate([qt[hh * d:(hh + 1) * d] * LOG2E, jnp.where(keep, 0.0, NEG_INF),
                                                 jnp.where(row == 0, NEG_INF, 0.0)], axis=0))

    def score_stage(slot, ja, jb, with_bias):
        cmaxes = []
        for g in groups:
            for hh in range(nh):
                keys = jnp.concatenate([kb_ref[g, hh, ja], kb_ref[g, hh, jb]], axis=0)
                st = jnp.dot(keys, wq_ref[g, hh], preferred_element_type=F32)
                if with_bias:
                    st = st + jnp.concatenate([bias_ref[g * nh + hh, w] for w in range(2)], axis=0)
                st_refs[slot][g, 0, :, hh * blk:(hh + 1) * blk] = st[:blk]
                st_refs[slot][g, 1, :, hh * blk:(hh + 1) * blk] = st[blk:]
                cmaxes.append(jnp.max(st, axis=0, keepdims=True))
        return tuple(cmaxes)

    def attend_stage(slot, cmaxes, stats, ja, jb):
        new = []
        for g in groups:
            for hh in range(nh):
                cs = slice(hh * blk, (hh + 1) * blk)
                m_new = cmaxes[g * nh + hh]
                if stats is not None:
                    m, l = stats[g * nh + hh]
                    m_new = jnp.maximum(m, m_new)
                ps = [jnp.exp2(st_refs[slot][g, w, :, cs] - m_new) for w in range(2)]
                l_new = jnp.sum(ps[0], axis=0, keepdims=True) + jnp.sum(ps[1], axis=0, keepdims=True)
                vt = jnp.concatenate([vt_ref[g, ja, hh * d:(hh + 1) * d, :],
                                      vt_ref[g, jb, hh * d:(hh + 1) * d, :]], axis=1)
                pv = jnp.dot(vt, jnp.concatenate([_mx(ps[0]), _mx(ps[1])], axis=0),
                             preferred_element_type=F32)
                if stats is None:
                    acc_ref[g, hh] = pv
                    new.append((m_new, l_new))
                else:
                    alpha = jnp.exp2(m - m_new)
                    acc_ref[g, hh] = acc_ref[g, hh] * alpha + pv
                    new.append((m_new, alpha * l + l_new))
        return tuple(new)

    nfar = qb - 1
    npairs = (nfar + 1) // 2

    def far_blocks(i):
        return jnp.where(2 * i < nfar, 2 * i, nb), jnp.where(2 * i + 1 < nfar, 2 * i + 1, nb)

    j_prev = jnp.where(qb > 0, qb - 1, nb)
    cmax0 = score_stage(0, qb, j_prev, True)
    cmax1 = score_stage(1, *far_blocks(0), False)
    init = (attend_stage(0, cmax0, None, qb, j_prev), cmax1)

    def trip(slot, k, carry):
        stats, cmax = carry
        cmax_next = score_stage(1 - slot, *far_blocks(k), False)
        return attend_stage(slot, cmax, stats, *far_blocks(k - 1)), cmax_next

    def trip_dyn(k, carry):
        return lax.cond(k % 2 == 1, functools.partial(trip, 1, k), functools.partial(trip, 0, k), carry)

    stats, _ = lax.fori_loop(1, npairs + 1, trip_dyn, init)
    for g in groups:
        out_t = jnp.concatenate([acc_ref[g, hh] / stats[g * nh + hh][1] for hh in range(nh)], axis=0)
        o_ref[0, :, g * LANES:(g + 1) * LANES] = out_t.T


MOBA_GROUPS = 4


def _moba(hp3d, bias):
    b, s, _ = hp3d.shape
    ng = MOBA_GROUPS
    wd = ng * LANES
    nsteps = A_WIDTH // wd
    hpp = LANES // A_HEAD_DIM
    nb = s // MOBA_BLOCK
    return pl.pallas_call(
        _moba_kernel,
        grid=(b, nsteps, nb),
        in_specs=[pl.BlockSpec((1, MOBA_BLOCK, wd), lambda bi, p, i: (bi, i, p)),
                  pl.BlockSpec((1, s, wd), lambda bi, p, i: (bi, 0, nsteps + p), pipeline_mode=pl.Buffered(1)),
                  pl.BlockSpec((1, s, wd), lambda bi, p, i: (bi, 0, 2 * nsteps + p), pipeline_mode=pl.Buffered(1)),
                  pl.BlockSpec((ng * hpp, 2, MOBA_BLOCK, MOBA_BLOCK), lambda bi, p, i: (p, 0, 0, 0))],
        out_specs=pl.BlockSpec((1, MOBA_BLOCK, wd), lambda bi, p, i: (bi, i, p)),
        out_shape=jax.ShapeDtypeStruct((b, s, A_WIDTH), F32),
        scratch_shapes=[pltpu.VMEM((ng, nb, LANES), F32),
                        pltpu.VMEM((ng, hpp, nb + 1, MOBA_BLOCK, LANES), MXU_DTYPE),
                        pltpu.VMEM((ng, nb + 1, LANES, MOBA_BLOCK), MXU_DTYPE),
                        pltpu.VMEM((ng, 2, MOBA_BLOCK, hpp * MOBA_BLOCK), F32),
                        pltpu.VMEM((ng, 2, MOBA_BLOCK, hpp * MOBA_BLOCK), F32),
                        pltpu.VMEM((ng, hpp, LANES, MOBA_BLOCK), MXU_DTYPE),
                        pltpu.VMEM((ng, hpp, A_HEAD_DIM, MOBA_BLOCK), F32)],
        compiler_params=_params("parallel", "parallel", "arbitrary"),
        name="moba",
    )(hp3d, hp3d, hp3d, bias)


S5_HALF_CH = S5_WIDTH // 2
S5_HALF_ST = (S5_GROUPS // 2) * S5_STATE
S5_TILE = 256
S5_NT = S5_HALF_ST // S5_TILE
S5_STEPS = 128
S5_PITCH = S5_STEPS + SUBLANES
S5_H_PITCH = S5_STEPS + SUBLANES // 2
S5_SLABS = 2 * S5_HALF_ST // LANES


def _s5_kernel(u0_ref, u_ref, wx_ref, wy_ref, a_ref, o_ref, st_ref, x0_scr, x1_scr, h0_scr, h1_scr):
    nb, steps, _ = u_ref.shape
    nseq = 2 * nb
    tile_slabs = S5_TILE // LANES
    nre = S5_SLABS // 2
    i = pl.program_id(0)

    def project_in_items(src_ref, x_scr):
        def item(m, part, n):
            def run():
                slab0 = (part * S5_HALF_ST + n * S5_TILE) // LANES
                for half in range(2):
                    lhs = jnp.concatenate(
                        [_mx(src_ref[bi, :, half * S5_HALF_CH + m * LANES:half * S5_HALF_CH + (m + 1) * LANES])
                         for bi in range(nb)], axis=0)
                    res = jnp.dot(lhs, wx_ref[part, n, half * LANES:(half + 1) * LANES, :],
                                  preferred_element_type=F32)
                    for bi in range(nb):
                        seq = 2 * bi + half
                        for sl in range(tile_slabs):
                            x_scr[slab0 + sl, seq * S5_PITCH:seq * S5_PITCH + steps, :] = (
                                res[bi * steps:(bi + 1) * steps, sl * LANES:(sl + 1) * LANES])
            return run
        return [item(m, part, n) for m in range(S5_NT // 2) for part in range(2) for n in (2 * m, 2 * m + 1)]

    def recur_items(x_scr, h_scr, nitems):
        state = {}

        def item(t0, t1):
            def run():
                a_re = [a_ref[0, :, s * LANES:(s + 1) * LANES] for s in range(nre)]
                a_im = [a_ref[1, :, s * LANES:(s + 1) * LANES] for s in range(nre)]
                if t0 == 0:
                    state["re"] = [st_ref[0, :, s * LANES:(s + 1) * LANES] for s in range(nre)]
                    state["im"] = [st_ref[1, :, s * LANES:(s + 1) * LANES] for s in range(nre)]
                h_re, h_im = state["re"], state["im"]
                for t in range(t0, t1):
                    rows_t = pl.ds(t, nseq, stride=S5_PITCH)
                    for s in range(nre):
                        hr, hi = h_re[s], h_im[s]
                        h_re[s] = a_re[s] * hr - a_im[s] * hi + x_scr[s, rows_t, :]
                        h_im[s] = a_re[s] * hi + a_im[s] * hr + x_scr[nre + s, rows_t, :]
                        h_scr[s, pl.ds(t, nseq, stride=S5_H_PITCH), :] = h_re[s]
                        h_scr[nre + s, pl.ds(t, nseq, stride=S5_H_PITCH), :] = h_im[s]
                if t1 == steps:
                    for s in range(nre):
                        st_ref[0, :, s * LANES:(s + 1) * LANES] = h_re[s]
                        st_ref[1, :, s * LANES:(s + 1) * LANES] = h_im[s]
            return run
        per = steps // nitems
        return [item(k * per, (k + 1) * per) for k in range(nitems)]

    def project_out_items(h_scr):
        def item(m, half):
            def run():
                acc = jnp.zeros((nb * steps, LANES), F32)
                for n in (2 * m, 2 * m + 1):
                    for part in range(2):
                        slab0 = (part * S5_HALF_ST + n * S5_TILE) // LANES
                        h_tile = jnp.concatenate(
                            [jnp.concatenate([h_scr[slab0 + sl,
                                                    (2 * bi + half) * S5_H_PITCH:(2 * bi + half) * S5_H_PITCH + steps, :]
                                              for sl in range(tile_slabs)], axis=1) for bi in range(nb)], axis=0)
                        acc = acc + jnp.dot(_mx(h_tile), wy_ref[part, n, :, half * LANES:(half + 1) * LANES],
                                            preferred_element_type=F32)
                c = half * S5_HALF_CH + m * LANES
                for bi in range(nb):
                    o_ref[bi, :, c:c + LANES] = acc[bi * steps:(bi + 1) * steps, :]
            return run
        return [item(m, half) for m in range(S5_NT // 2) for half in range(2)]

    @pl.when(i == 0)
    def _():
        st_ref[...] = jnp.zeros_like(st_ref)
        h1_scr[...] = jnp.zeros_like(h1_scr)
        for run in project_in_items(u0_ref, x0_scr):
            run()

    def step(x_cur, h_cur, x_nxt, h_prev):
        ins = project_in_items(u_ref, x_nxt)
        outs = project_out_items(h_prev)
        recs = recur_items(x_cur, h_cur, len(ins))
        for k, rec in enumerate(recs):
            rec()
            ins[k]()
            if k % 2 == 1:
                outs[k // 2]()

    @pl.when(i % 2 == 0)
    def _():
        step(x0_scr, h0_scr, x1_scr, h1_scr)

    @pl.when(i % 2 == 1)
    def _():
        step(x1_scr, h1_scr, x0_scr, h0_scr)


def _s5_weights(lam_re, lam_im, b_re, b_im, c_re, c_im, log_dt):
    lr = jnp.minimum(lam_re.astype(F32), -1e-4)
    li = lam_im.astype(F32)
    dt = jnp.exp(log_dt.astype(F32))[:, None]
    mag = jnp.exp(lr * dt)
    a_re = mag * jnp.cos(li * dt)
    a_im = mag * jnp.sin(li * dt)
    den = lr * lr + li * li
    coef_re = ((a_re - 1.0) * lr + a_im * li) / den
    coef_im = (a_im * lr - (a_re - 1.0) * li) / den
    br = b_re.astype(F32)
    bi = b_im.astype(F32)
    bb_re = coef_re[..., None] * br - coef_im[..., None] * bi
    bb_im = coef_re[..., None] * bi + coef_im[..., None] * br
    gh = S5_GROUPS // 2
    eye = jnp.eye(gh, dtype=F32)

    def b_dense(bb):
        t = bb.reshape(2, gh, S5_STATE, S5_GROUP)
        return jnp.einsum('fgph,gk->fghkp', t, eye).reshape(2, S5_HALF_CH, S5_HALF_ST)

    def c_dense(cc):
        t = cc.reshape(2, gh, S5_GROUP, S5_STATE)
        return jnp.einsum('fghp,gk->fgpkh', t, eye).reshape(2, S5_HALF_ST, S5_HALF_CH)

    def x_tiles(bd):
        tiles = []
        for n in range(S5_NT):
            c = (n // 2) * LANES
            tiles.append(jnp.concatenate([bd[0, c:c + LANES, n * S5_TILE:(n + 1) * S5_TILE],
                                          bd[1, c:c + LANES, n * S5_TILE:(n + 1) * S5_TILE]], axis=0))
        return jnp.stack(tiles)

    def y_tiles(cd):
        tiles = []
        for n in range(S5_NT):
            c = (n // 2) * LANES
            tiles.append(jnp.concatenate([cd[0, n * S5_TILE:(n + 1) * S5_TILE, c:c + LANES],
                                          cd[1, n * S5_TILE:(n + 1) * S5_TILE, c:c + LANES]], axis=1))
        return jnp.stack(tiles)

    wx = jnp.stack([x_tiles(b_dense(bb_re)), x_tiles(b_dense(bb_im))]).astype(MXU_DTYPE)
    wy = jnp.stack([y_tiles(c_dense(c_re.astype(F32))), y_tiles(c_dense(-c_im.astype(F32)))]).astype(MXU_DTYPE)
    return wx, wy, a_re.reshape(2, S5_HALF_ST), a_im.reshape(2, S5_HALF_ST)


def _s5_scan(hp3d, wx, wy, a_re2, a_im2):
    b, s, width = hp3d.shape
    nseq = 2 * b
    assert nseq == SUBLANES
    sel = jnp.arange(nseq) % 2
    a = jnp.stack([a_re2[sel], a_im2[sel]])
    scr = pltpu.VMEM((S5_SLABS, nseq * S5_PITCH, LANES), F32)
    hscr = pltpu.VMEM((S5_SLABS, nseq * S5_H_PITCH, LANES), F32)
    last = s // S5_STEPS - 1
    ucol = width // S5_WIDTH - 1
    return pl.pallas_call(
        _s5_kernel,
        grid=(last + 2,),
        in_specs=[pl.BlockSpec((b, S5_STEPS, S5_WIDTH), lambda i: (0, jnp.minimum(i, last), ucol)),
                  pl.BlockSpec((b, S5_STEPS, S5_WIDTH), lambda i: (0, jnp.minimum(i + 1, last), ucol)),
                  _resident(wx.shape), _resident(wy.shape), _resident(a.shape)],
        out_specs=pl.BlockSpec((b, S5_STEPS, S5_WIDTH), lambda i: (0, jnp.maximum(i - 1, 0), 0)),
        out_shape=jax.ShapeDtypeStruct((b, s, S5_WIDTH), F32),
        scratch_shapes=[pltpu.VMEM((2, nseq, S5_HALF_ST), F32), scr, scr, hscr, hscr],
        compiler_params=_params("arbitrary"),
        name="s5_scan",
    )(hp3d, hp3d, wx, wy, a)


MLP_CHUNK = 1024


def _tail_body(mix, x_ref, g1_ref, b1_ref, w1_ref, w2_ref, g2_ref, b2_ref, o_ref):
    x1 = _layer_norm(DN_ALPHA * x_ref[...] + mix, g1_ref[...], b1_ref[...])
    x1m = _mx(x1)
    acc = jnp.zeros(x1.shape, F32)
    for f in range(w1_ref.shape[1] // MLP_CHUNK):
        hid = jnp.dot(x1m, w1_ref[:, f * MLP_CHUNK:(f + 1) * MLP_CHUNK], preferred_element_type=F32)
        hid = jnp.square(jnp.maximum(hid, 0.0))
        acc = acc + jnp.dot(_mx(hid), w2_ref[f * MLP_CHUNK:(f + 1) * MLP_CHUNK, :],
                            preferred_element_type=F32)
    o_ref[...] = _layer_norm(DN_ALPHA * x1 + acc, g2_ref[...], b2_ref[...])


def _tail_even_kernel(attn_ref, y_ref, u_ref, dsk_ref, wg_ref, bg_ref, wo_ref, x_ref,
                      g1_ref, b1_ref, w1_ref, w2_ref, g2_ref, b2_ref, o_ref):
    y = y_ref[...] + dsk_ref[...] * u_ref[...]
    g = 0.5 * y * (1.0 + jnp.tanh(math.sqrt(2.0 / math.pi) * (y + 0.044715 * (y * y * y))))
    ssm = g * _sigmoid(jnp.dot(_mx(g), wg_ref[...], preferred_element_type=F32) + bg_ref[...])
    ka = attn_ref.shape[1]
    mix = (jnp.dot(_mx(attn_ref[...]), wo_ref[0:ka, :], preferred_element_type=F32)
           + jnp.dot(_mx(ssm), wo_ref[ka:, :], preferred_element_type=F32))
    _tail_body(mix, x_ref, g1_ref, b1_ref, w1_ref, w2_ref, g2_ref, b2_ref, o_ref)


def _tail_odd_kernel(a_ref, wo_ref, x_ref, g1_ref, b1_ref, w1_ref, w2_ref, g2_ref, b2_ref, o_ref):
    mix = jnp.dot(_mx(a_ref[...]), wo_ref[...], preferred_element_type=F32)
    _tail_body(mix, x_ref, g1_ref, b1_ref, w1_ref, w2_ref, g2_ref, b2_ref, o_ref)


def _row(v):
    return v.reshape(1, -1).astype(F32)


def _tail_common_specs(tm, d, dff):
    return [_resident((d, d)),
            pl.BlockSpec((tm, d), lambda i: (i, 0)),
            _resident((1, d)), _resident((1, d)),
            _resident((d, dff)), _resident((dff, d)),
            _resident((1, d)), _resident((1, d))]


def _tail_even(attn2d, y2d, hp2d, x2d, d_skip, w_glu, b_glu, w_out, g1, b1, w1, w2, g2, b2, tm):
    m, d = x2d.shape
    dff = w1.shape[1]
    ucol = hp2d.shape[1] // S5_WIDTH - 1
    return pl.pallas_call(
        _tail_even_kernel,
        grid=(m // tm,),
        in_specs=[pl.BlockSpec((tm, A_WIDTH), lambda i: (i, 0)),
                  pl.BlockSpec((tm, S5_WIDTH), lambda i: (i, 0)),
                  pl.BlockSpec((tm, S5_WIDTH), lambda i: (i, ucol)),
                  _resident((1, S5_WIDTH)), _resident((S5_WIDTH, S5_WIDTH)), _resident((1, S5_WIDTH))]
        + _tail_common_specs(tm, d, dff),
        out_specs=pl.BlockSpec((tm, d), lambda i: (i, 0)),
        out_shape=jax.ShapeDtypeStruct((m, d), F32),
        compiler_params=_params("parallel"),
        name="tail_even",
    )(attn2d, y2d, hp2d, _row(d_skip), _mx(w_glu), _row(b_glu), _mx(w_out), x2d,
      _row(g1), _row(b1), _mx(w1), _mx(w2), _row(g2), _row(b2))


def _tail_odd(a2d, x2d, w_out, g1, b1, w1, w2, g2, b2, tm):
    m, d = x2d.shape
    dff = w1.shape[1]
    return pl.pallas_call(
        _tail_odd_kernel,
        grid=(m // tm,),
        in_specs=[pl.BlockSpec((tm, a2d.shape[1]), lambda i: (i, 0))] + _tail_common_specs(tm, d, dff),
        out_specs=pl.BlockSpec((tm, d), lambda i: (i, 0)),
        out_shape=jax.ShapeDtypeStruct((m, d), F32),
        compiler_params=_params("parallel"),
        name="tail_odd",
    )(a2d, _mx(w_out), x2d, _row(g1), _row(b1), _mx(w1), _mx(w2), _row(g2), _row(b2))


GDN_TILE = 256
GDN_GATES_TILE = 2048


def _softplus(t):
    return jnp.maximum(t, 0.0) + jnp.log1p(jnp.exp(-jnp.abs(t)))


def _gdn_prep_body(i, hg, q_ref, k_ref, v_ref, qp_ref, kp_ref, vp_ref, cwq_ref, cwk_ref, cwv_ref,
                   ba_ref, g_ref, gt_ref, u_ref, w_ref, qd_ref, kd_ref, intra_ref, edl_ref,
                   qx_ref, kx_ref, vx_ref, ql_ref, kl_ref, vl_ref):
    assert C_CONV == 4
    t, width = q_ref.shape[1], q_ref.shape[2]
    c = C_CHUNK
    dk = C_HEAD_DIM
    hp = width // dk
    groups = t // SUBLANES

    def conv_silu(main_ref, prev_ref, cw_ref, ext_ref, low_ref):
        x = main_ref[0]
        ext_ref[0:SUBLANES, :] = jnp.zeros((SUBLANES, width), F32)
        ext_ref[SUBLANES:2 * SUBLANES, :] = jnp.where(i > 0, prev_ref[0], 0.0)
        ext_ref[2 * SUBLANES:, :] = x
        ext_rows = t + SUBLANES

        def tap(j):
            return jnp.broadcast_to(cw_ref[j:j + 1, :], (SUBLANES, width))[None]

        def grouped(a):
            return a.reshape(a.shape[0] // SUBLANES, SUBLANES, width)

        x_ext = grouped(ext_ref[SUBLANES:, :])
        x_del = grouped(ext_ref[SUBLANES - 1:SUBLANES - 1 + ext_rows, :])
        low_ref[...] = (tap(1) * x_ext + tap(0) * x_del).reshape(ext_rows, width)
        y = (tap(3) * grouped(x) + tap(2) * x_del[1:]
             + grouped(low_ref[SUBLANES - 2:SUBLANES - 2 + t, :]))
        hy = 0.5 * y.reshape(t, width)
        return hy + hy * jnp.tanh(hy)

    def l2n(z, scale):
        return z * (lax.rsqrt(jnp.sum(z * z, axis=-1, keepdims=True) + L2_EPS) * scale)

    q_all = conv_silu(q_ref, qp_ref, cwq_ref, qx_ref, ql_ref)
    k_all = conv_silu(k_ref, kp_ref, cwk_ref, kx_ref, kl_ref)
    v_all = conv_silu(v_ref, vp_ref, cwv_ref, vx_ref, vl_ref)
    heads = [slice(hh * dk, (hh + 1) * dk) for hh in range(hp)]
    qs = [l2n(q_all[:, cs], dk ** -0.5) for cs in heads]
    ks = [l2n(k_all[:, cs], 1.0) for cs in heads]
    vs = [v_all[:, cs] for cs in heads]

    ba = ba_ref[0]
    lane = lax.broadcasted_iota(jnp.int32, (t, LANES), 1)
    betas, g_cols, g_rows = [], [], []
    for hh in range(hp):
        h = hg * hp + hh
        betas.append(_sigmoid(jnp.sum(jnp.where(lane == h, ba, 0.0), axis=1, keepdims=True)))
        g_cols.append(jnp.sum(jnp.where(lane == C_HEADS + h, g_ref[0], 0.0), axis=1, keepdims=True))
        g_rows.append(gt_ref[0, pl.ds(C_HEADS + h, 1), :])

    units = [(hh, slice(ch * c, (ch + 1) * c)) for hh in range(hp) for ch in range(t // c)]
    _gdn_prep_units(units, qs, ks, vs, betas, g_cols, g_rows, heads, u_ref, w_ref, qd_ref, kd_ref, intra_ref, edl_ref)


def _gdn_prep_units(units, qs, ks, vs, betas, g_cols, g_rows, heads, u_ref, w_ref, qd_ref, kd_ref, intra_ref, edl_ref):
    c = C_CHUNK
    dk = C_HEAD_DIM
    assert 2 * c == LANES and len(units) % 2 == 0
    ii = lax.broadcasted_iota(jnp.int32, (c, c), 0)
    jj = lax.broadcasted_iota(jnp.int32, (c, c), 1)
    row = lax.broadcasted_iota(jnp.int32, (c, 2 * c), 0)
    lane = lax.broadcasted_iota(jnp.int32, (c, 2 * c), 1)
    left = lane < c
    col = jnp.where(left, lane, lane - c)
    low = row >= col
    pairs = [(2 * n, 2 * n + 1) for n in range(len(units) // 2)]

    def side_by_side(a, b):
        za, zb = jnp.zeros_like(a), jnp.zeros_like(b)
        return jnp.concatenate([jnp.concatenate([a, zb], axis=1), jnp.concatenate([za, b], axis=1)], axis=0)

    def block_diag(p):
        return jnp.concatenate([jnp.where(left, p, 0.0), jnp.where(left, 0.0, p)], axis=0)

    gc_cols = [jnp.sum(jnp.where(ii >= jj, g_rows[hh][:, r], 0.0), axis=1, keepdims=True) for hh, r in units]
    gc_rows = [jnp.sum(jnp.where(ii <= jj, g_cols[hh][r], 0.0), axis=0, keepdims=True) for hh, r in units]
    decays = [jnp.where(low, jnp.exp(jnp.where(low, jnp.where(left, gc_cols[a], gc_cols[b])
                                               - jnp.concatenate([gc_rows[a], gc_rows[b]], axis=1), 0.0)), 0.0)
              for a, b in pairs]
    beta_bs = [jnp.broadcast_to(betas[hh][r], (c, dk)) for hh, r in units]
    kbs = [ks[hh][r] * bb for bb, (hh, r) in zip(beta_bs, units)]
    kdiag = [_mx(side_by_side(ks[units[a][0]][units[a][1]], ks[units[b][0]][units[b][1]])) for a, b in pairs]
    kks = [_dot_nt(jnp.concatenate([kbs[a], kbs[b]], axis=1), kd) for (a, b), kd in zip(pairs, kdiag)]
    qks = [_dot_nt(jnp.concatenate([qs[units[a][0]][units[a][1]], qs[units[b][0]][units[b][1]]], axis=1), kd)
           for (a, b), kd in zip(pairs, kdiag)]
    pws = [jnp.where(row > col, -(kk * dec), 0.0) for kk, dec in zip(kks, decays)]
    tms = pws
    pds = [_mx(block_diag(pw)) for pw in pws]
    for _ in range(int(math.log2(c)) - 1):
        pws = [jnp.dot(_mx(pw), pd, preferred_element_type=F32) for pw, pd in zip(pws, pds)]
        pds = [_mx(block_diag(pw)) for pw in pws]
        tms = [tm + pw + jnp.dot(_mx(tm), pd, preferred_element_type=F32) for tm, pw, pd in zip(tms, pws, pds)]
    egcs = [jnp.broadcast_to(jnp.exp(a), (c, dk)) for a in gc_cols]
    vbs = [vs[hh][r] * bb for bb, (hh, r) in zip(beta_bs, units)]
    kbes = [kb * e for kb, e in zip(kbs, egcs)]
    tbs = [_mx(tm) for tm in tms]
    tvs = [jnp.dot(tb, _mx(side_by_side(vbs[a], vbs[b])), preferred_element_type=F32) for tb, (a, b) in zip(tbs, pairs)]
    tks = [jnp.dot(tb, _mx(side_by_side(kbes[a], kbes[b])), preferred_element_type=F32) for tb, (a, b) in zip(tbs, pairs)]
    for n, (hh, r) in enumerate(units):
        gl = gc_cols[n][c - 1:c, :]
        cs = heads[hh]
        ch = r.start // c
        half = slice((n % 2) * dk, (n % 2 + 1) * dk)
        u_ref[0, r, cs] = vbs[n] + tvs[n // 2][:, half]
        w_ref[0, r, cs] = _mx(kbes[n] + tks[n // 2][:, half])
        qd_ref[0, r, cs] = _mx(qs[hh][r] * egcs[n])
        kd_ref[0, r, cs] = _mx(ks[hh][r] * jnp.exp(gl - gc_cols[n]))
        intra_ref[0, hh, r, :] = _mx((qks[n // 2] * decays[n // 2])[:, (n % 2) * c:(n % 2 + 1) * c])
        edl_ref[0, hh, ch:ch + 1, :] = jnp.broadcast_to(jnp.exp(gl), (1, LANES))


def _gdn_gates_kernel(ba_ref, alog_ref, dtb_ref, g_ref, gt_ref):
    g_all = -jnp.exp(alog_ref[...]) * _softplus(ba_ref[0] + dtb_ref[...])
    g_ref[0] = g_all
    gt_ref[0] = g_all.T


def _gdn_gates(hp3d, alog_row, dtb_row):
    b, s, _ = hp3d.shape
    t = min(GDN_GATES_TILE, s)
    ba_col = 4 * C_WIDTH // LANES
    return pl.pallas_call(
        _gdn_gates_kernel,
        grid=(b, s // t),
        in_specs=[pl.BlockSpec((1, t, LANES), lambda bi, i: (bi, i, ba_col)),
                  pl.BlockSpec((1, LANES), lambda bi, i: (0, 0)),
                  pl.BlockSpec((1, LANES), lambda bi, i: (0, 0))],
        out_specs=[pl.BlockSpec((1, t, LANES), lambda bi, i: (bi, i, 0)),
                   pl.BlockSpec((1, LANES, t), lambda bi, i: (bi, 0, i))],
        out_shape=[jax.ShapeDtypeStruct((b, s, LANES), F32), jax.ShapeDtypeStruct((b, LANES, s), F32)],
        compiler_params=_params("parallel", "parallel"),
        name="gdn_gates",
    )(hp3d, alog_row, dtb_row)


def _gdn_scan_body(chunk0, u_ref, w_ref, qd_ref, kd_ref, intra_ref, edl_ref, gate_ref, nw_ref, o_ref, st_ref):
    c = C_CHUNK
    dk = C_HEAD_DIM
    t = u_ref.shape[1]
    nw = nw_ref[...]
    heads = [slice(h * dk, (h + 1) * dk) for h in range(C_HEADS)]
    for ch in range(t // c):
        r = slice(ch * c, (ch + 1) * c)
        sts = [st_ref[h] for h in range(C_HEADS)]
        wss = [_dot(jnp.concatenate([w_ref[0, r, cs], qd_ref[0, r, cs]], axis=0), st)
               for cs, st in zip(heads, sts)]
        v_news = [u_ref[0, r, cs] - ws[:c] for cs, ws in zip(heads, wss)]
        kvs = [_dot_tn(kd_ref[0, r, cs], v_new) for cs, v_new in zip(heads, v_news)]
        os_ = [ws[c:] + _dot(intra_ref[0, h, r, :], v_new) for h, (ws, v_new) in enumerate(zip(wss, v_news))]
        for h, cs in enumerate(heads):
            st_ref[h] = sts[h] * edl_ref[0, h, pl.ds(chunk0 + ch, 1), 0:1] + kvs[h]
            o = os_[h]
            o = o * lax.rsqrt(jnp.mean(o * o, axis=-1, keepdims=True) + RMS_EPS) * nw
            hg = 0.5 * gate_ref[0, r, cs]
            o_ref[0, r, cs] = o * (hg + hg * jnp.tanh(hg))


def _gdn_kernel(q_ref, k_ref, v_ref, qp_ref, kp_ref, vp_ref, cwq_ref, cwk_ref, cwv_ref, ba_ref, g_ref, gt_ref,
                gate_ref, nw_ref, o_ref, st_ref, qx_ref, kx_ref, vx_ref, ql_ref, kl_ref, vl_ref, *bufs):
    j = pl.program_id(1)
    last = pl.num_programs(1) - 2
    set_a, set_b = bufs[:len(bufs) // 2], bufs[len(bufs) // 2:]

    @pl.when(j == 0)
    def _():
        st_ref[...] = jnp.zeros_like(st_ref)
        for ref in bufs:
            ref[...] = jnp.zeros_like(ref)

    def step(cur, nxt):
        _gdn_prep_body(jnp.minimum(j, last), 0, q_ref, k_ref, v_ref, qp_ref, kp_ref, vp_ref,
                       cwq_ref, cwk_ref, cwv_ref, ba_ref, g_ref, gt_ref, *nxt,
                       qx_ref, kx_ref, vx_ref, ql_ref, kl_ref, vl_ref)
        _gdn_scan_body(0, *cur, gate_ref, nw_ref, o_ref, st_ref)

    @pl.when(j % 2 == 0)
    def _():
        step(set_a, set_b)

    @pl.when(j % 2 == 1)
    def _():
        step(set_b, set_a)


def _gdn(hp3d, conv_w, alog_row, dtb_row, norm_w):
    b, s, _ = hp3d.shape
    ba_col = 4 * C_WIDTH // LANES
    t = GDN_TILE
    nt = s // t
    per = t // SUBLANES
    wd = C_WIDTH
    nc = t // C_CHUNK
    g, gt = _gdn_gates(hp3d, alog_row, dtb_row)

    def prep_tile(j):
        return jnp.minimum(j, nt - 1)

    def scan_tile(j):
        return jnp.maximum(j - 1, 0)

    def main(off):
        return pl.BlockSpec((1, t, wd), lambda bi, j: (bi, prep_tile(j), off))

    def prev(off):
        return pl.BlockSpec((1, SUBLANES, wd), lambda bi, j: (bi, jnp.maximum(prep_tile(j) * per - 1, 0), off))

    def cw(off):
        return pl.BlockSpec((C_CONV, wd), lambda bi, j: (0, off))

    bufs = [pltpu.VMEM((1, t, wd), F32)] + [pltpu.VMEM((1, t, wd), MXU_DTYPE)] * 3 + [
        pltpu.VMEM((1, C_HEADS, t, C_CHUNK), MXU_DTYPE), pltpu.VMEM((1, C_HEADS, max(nc, SUBLANES), LANES), F32)]
    return pl.pallas_call(
        _gdn_kernel,
        grid=(b, nt + 1),
        in_specs=[main(0), main(1), main(2), prev(0), prev(1), prev(2), cw(0), cw(1), cw(2),
                  pl.BlockSpec((1, t, LANES), lambda bi, j: (bi, prep_tile(j), ba_col)),
                  pl.BlockSpec((1, t, LANES), lambda bi, j: (bi, prep_tile(j), 0)),
                  pl.BlockSpec((1, LANES, t), lambda bi, j: (bi, 0, prep_tile(j))),
                  pl.BlockSpec((1, t, wd), lambda bi, j: (bi, scan_tile(j), 3)),
                  pl.BlockSpec((1, C_HEAD_DIM), lambda bi, j: (0, 0))],
        out_specs=pl.BlockSpec((1, t, wd), lambda bi, j: (bi, scan_tile(j), 0)),
        out_shape=jax.ShapeDtypeStruct((b, s, wd), F32),
        scratch_shapes=[pltpu.VMEM((C_HEADS, C_HEAD_DIM, C_HEAD_DIM), F32)]
        + [pltpu.VMEM((t + 2 * SUBLANES, wd), F32)] * 3 + [pltpu.VMEM((t + SUBLANES, wd), F32)] * 3
        + bufs + bufs,
        compiler_params=_params("parallel", "arbitrary"),
        name="gdn",
    )(hp3d, hp3d, hp3d, hp3d, hp3d, hp3d, conv_w, conv_w, conv_w, hp3d, g, gt, hp3d, _row(norm_w))


ROW_TILE = 512


def kernel(x, rel_table, ev_w_in, ev_w_out, s5_lambda_re, s5_lambda_im, s5_b_re, s5_b_im, s5_c_re, s5_c_im, s5_d, s5_log_dt, s5_w_glu, s5_b_glu, od_w_in, od_conv_w, od_a_log, od_dt_bias, od_norm_w, od_w_out, ln_mix_g, ln_mix_b, mlp_w1, mlp_w2, ln_ffn_g, ln_ffn_b):
    bsz, s, d = x.shape
    m = bsz * s
    x2d = x.reshape(m, d)
    table_t = rel_table.T.astype(F32)
    bias = _bias_tiles(table_t)
    for layer in range(DEPTH):
        i = layer // 2
        if layer % 2 == 0:
            hp = _proj(x2d, _mx(ev_w_in[i]), ROW_TILE)
            hp3d = hp.reshape(bsz, s, hp.shape[1])
            attn = _moba(hp3d, bias)
            wx, wy, a_re2, a_im2 = _s5_weights(s5_lambda_re[i], s5_lambda_im[i], s5_b_re[i], s5_b_im[i],
                                               s5_c_re[i], s5_c_im[i], s5_log_dt[i])
            y = _s5_scan(hp3d, wx, wy, a_re2, a_im2)
            x2d = _tail_even(attn.reshape(m, A_WIDTH), y.reshape(m, S5_WIDTH), hp, x2d,
                             s5_d[i], s5_w_glu[i], s5_b_glu[i], ev_w_out[i],
                             ln_mix_g[layer], ln_mix_b[layer], mlp_w1[layer], mlp_w2[layer],
                             ln_ffn_g[layer], ln_ffn_b[layer], ROW_TILE)
        else:
            w_in = od_w_in[i]
            w_all = jnp.pad(w_in, ((0, 0), (0, LANES - 2 * C_HEADS)))
            hp = _proj(x2d, _mx(w_all), ROW_TILE)
            hp3d = hp.reshape(bsz, s, 4 * C_WIDTH + LANES)
            pad8 = jnp.zeros((C_HEADS,), F32)
            padr = jnp.zeros((LANES - 2 * C_HEADS,), F32)
            alog_row = jnp.concatenate([pad8, od_a_log[i].astype(F32), padr]).reshape(1, LANES)
            dtb_row = jnp.concatenate([pad8, od_dt_bias[i].astype(F32), padr]).reshape(1, LANES)
            o = _gdn(hp3d, od_conv_w[i].astype(F32), alog_row, dtb_row, od_norm_w[i])
            x2d = _tail_odd(o.reshape(m, C_WIDTH), x2d, od_w_out[i],
                            ln_mix_g[layer], ln_mix_b[layer], mlp_w1[layer], mlp_w2[layer],
                            ln_ffn_g[layer], ln_ffn_b[layer], ROW_TILE)
    return x2d.reshape(bsz, s, d)
```

```python
import functools
import math

import jax
import jax.numpy as jnp
import numpy as np
from jax import lax
from jax.experimental import pallas as pl
from jax.experimental.pallas import tpu as pltpu

F32 = jnp.float32
MXU_DTYPE = jnp.bfloat16

LANES = 128
SUBLANES = 8
VMEM_LIMIT_BYTES = 56 * 1024 * 1024

A_HEADS = 8
A_HEAD_DIM = 64
A_WIDTH = A_HEADS * A_HEAD_DIM
MOBA_BLOCK = 256
MOBA_TOPK = 3
REL_BUCKETS = 32
REL_MAX_DIST = 128
S5_GROUP = 16
S5_GROUPS = 32
S5_STATE = 64
S5_WIDTH = S5_GROUP * S5_GROUPS
C_HEADS = 8
C_HEAD_DIM = 128
C_WIDTH = C_HEADS * C_HEAD_DIM
C_CONV = 4
C_CHUNK = 64
DEPTH = 2
DN_ALPHA = (2 * DEPTH) ** 0.25
LN_EPS = 1e-5
RMS_EPS = 1e-6
L2_EPS = 1e-6
NEG_INF = -1e30
LOG2E = math.log2(math.e)


def _mx(t):
    return t.astype(MXU_DTYPE)


def _dot(a, b):
    return jnp.dot(_mx(a), _mx(b), preferred_element_type=F32)


def _dot_tn(a, b):
    return lax.dot_general(_mx(a), _mx(b), (((0,), (0,)), ((), ())), preferred_element_type=F32)


def _dot_nt(a, b):
    return lax.dot_general(_mx(a), _mx(b), (((1,), (1,)), ((), ())), preferred_element_type=F32)


def _sigmoid(t):
    return 1.0 / (1.0 + jnp.exp(-t))


def _layer_norm(r, g, b):
    mu = jnp.mean(r, axis=-1, keepdims=True)
    d = r - mu
    var = jnp.mean(d * d, axis=-1, keepdims=True)
    return d * lax.rsqrt(var + LN_EPS) * g + b


def _params(*sem):
    return pltpu.CompilerParams(dimension_semantics=sem, vmem_limit_bytes=VMEM_LIMIT_BYTES)


def _resident(shape):
    return pl.BlockSpec(shape, lambda *_: (0,) * len(shape), pipeline_mode=pl.Buffered(1))


PROJ_CHUNK = 1024


def _proj_kernel(x_ref, w_ref, o_ref):
    xm = _mx(x_ref[...])
    n = w_ref.shape[1]
    for c0 in range(0, n, PROJ_CHUNK):
        c1 = min(c0 + PROJ_CHUNK, n)
        o_ref[:, c0:c1] = jnp.dot(xm, w_ref[:, c0:c1], preferred_element_type=F32)


def _proj(x2d, w, tm):
    m, k = x2d.shape
    n = w.shape[1]
    return pl.pallas_call(
        _proj_kernel,
        grid=(m // tm,),
        in_specs=[pl.BlockSpec((tm, k), lambda i: (i, 0)), _resident((k, n))],
        out_specs=pl.BlockSpec((tm, n), lambda i: (i, 0)),
        out_shape=jax.ShapeDtypeStruct((m, n), F32),
        compiler_params=_params("parallel"),
        name="in_proj",
    )(x2d, w)


def _t5_bucket_np(dist):
    max_exact = REL_BUCKETS // 2
    n = np.maximum(dist, 0)
    nf = np.maximum(n, 1).astype(np.float32)
    large = max_exact + (np.log(nf / np.float32(max_exact)) / np.float32(math.log(REL_MAX_DIST / max_exact))
                         * np.float32(REL_BUCKETS - max_exact)).astype(np.int32)
    large = np.minimum(large, REL_BUCKETS - 1)
    return np.where(n < max_exact, n, large).astype(np.int32)


def _bias_kernel(tab_ref, bkt_ref, o_ref):
    h = pl.program_id(0)
    blk = bkt_ref.shape[-1]
    key = lax.broadcasted_iota(jnp.int32, (blk, blk), 0)
    qry = lax.broadcasted_iota(jnp.int32, (blk, blk), 1)
    far = tab_ref[h, REL_BUCKETS - 1]
    for t in range(2):
        bk = bkt_ref[t]
        acc = jnp.zeros((blk, blk), F32)
        for i in range(REL_BUCKETS):
            acc = jnp.where(bk == i, (tab_ref[h, i] - far) * LOG2E, acc)
        if t == 0:
            acc = jnp.where(qry >= key, acc, NEG_INF)
        o_ref[0, t] = acc


def _bias_tiles(table_t):
    heads = table_t.shape[0]
    offs = np.arange(MOBA_BLOCK)
    dist_own = offs[None, :] - offs[:, None]
    buckets = np.stack([_t5_bucket_np(dist_own), _t5_bucket_np(dist_own + MOBA_BLOCK)])
    return pl.pallas_call(
        _bias_kernel,
        grid=(heads,),
        in_specs=[pl.BlockSpec(memory_space=pltpu.SMEM),
                  pl.BlockSpec((2, MOBA_BLOCK, MOBA_BLOCK), lambda h: (0, 0, 0))],
        out_specs=pl.BlockSpec((1, 2, MOBA_BLOCK, MOBA_BLOCK), lambda h: (h, 0, 0, 0)),
        out_shape=jax.ShapeDtypeStruct((heads, 2, MOBA_BLOCK, MOBA_BLOCK), F32),
        compiler_params=_params("parallel"),
        name="t5_bias",
    )(table_t, jnp.asarray(buckets))


def _moba_kernel(q_ref, k_ref, v_ref, bias_ref, o_ref, kmean_ref, kb_ref, vt_ref,
                 st0_ref, st1_ref, wq_ref, acc_ref):
    st_refs = (st0_ref, st1_ref)
    qb = pl.program_id(2)
    blk = MOBA_BLOCK
    d = A_HEAD_DIM
    nh = LANES // d
    nb = k_ref.shape[1] // blk
    ng = q_ref.shape[2] // LANES
    groups = range(ng)

    @pl.when(qb == 0)
    def _():
        lane = lax.broadcasted_iota(jnp.int32, (blk, LANES - d), 1)
        for g in groups:
            gl = slice(g * LANES, (g + 1) * LANES)
            for j in range(nb + 1):
                onehot = jnp.where(lane == j, 1.0, 0.0).astype(MXU_DTYPE)
                for hh in range(nh):
                    kb_ref[g, hh, j, :, d:LANES] = onehot
            for j in range(nb):
                kj = k_ref[0, j * blk:(j + 1) * blk, gl]
                for hh in range(nh):
                    kb_ref[g, hh, j, :, 0:d] = _mx(kj[:, hh * d:(hh + 1) * d])
                kmean_ref[g, j:j + 1, :] = jnp.mean(kj, axis=0, keepdims=True)
                vt_ref[g, j] = _mx(v_ref[0, j * blk:(j + 1) * blk, gl].T)
            for hh in range(nh):
                kb_ref[g, hh, nb, :, 0:d] = jnp.zeros((blk, d), MXU_DTYPE)
            vt_ref[g, nb] = jnp.zeros((LANES, blk), MXU_DTYPE)

    blk_id = lax.broadcasted_iota(jnp.int32, (nb, blk), 0)
    row = lax.broadcasted_iota(jnp.int32, (LANES - d - nb, blk), 0)
    for g in groups:
        qt = (q_ref[0, :, g * LANES:(g + 1) * LANES] * (d ** -0.5)).T
        for hh in range(nh):
            gate = lax.dot_general(kmean_ref[g, :, hh * d:(hh + 1) * d], qt[hh * d:(hh + 1) * d],
                                   (((1,), (0,)), ((), ())),
                                   precision=lax.Precision.HIGHEST, preferred_element_type=F32)
            gt = jnp.where(blk_id < qb, gate, -jnp.inf)
            sel = jnp.zeros((nb, blk), F32)
            for _ in range(MOBA_TOPK):
                m = jnp.max(gt, axis=0, keepdims=True)
                is_m = jnp.logical_and(gt == m, gt > -jnp.inf)
                idx = jnp.min(jnp.where(is_m, blk_id, nb), axis=0, keepdims=True)
                pick = blk_id == idx
                sel = jnp.where(pick, 1.0, sel)
                gt = jnp.where(pick, -jnp.inf, gt)
            keep = jnp.logical_or(sel > 0.0, blk_id == qb)
            wq_ref[g, hh] = _mx(jnp.concatenate([qt[hh * d:(hh + 1) * d] * LOG2E, jnp.where(keep, 0.0, NEG_INF),
                                                 jnp.where(row == 0, NEG_INF, 0.0)], axis=0))

    def score_stage(slot, ja, jb, with_bias):
        cmaxes = []
        for g in groups:
            for hh in range(nh):
                keys = jnp.concatenate([kb_ref[g, hh, ja], kb_ref[g, hh, jb]], axis=0)
                st = jnp.dot(keys, wq_ref[g, hh], preferred_element_type=F32)
                if with_bias:
                    st = st + jnp.concatenate([bias_ref[g * nh + hh, w] for w in range(2)], axis=0)
                st_refs[slot][g, 0, :, hh * blk:(hh + 1) * blk] = st[:blk]
                st_refs[slot][g, 1, :, hh * blk:(hh + 1) * blk] = st[blk:]
                cmaxes.append(jnp.max(st, axis=0, keepdims=True))
        return tuple(cmaxes)

    def attend_stage(slot, cmaxes, stats, ja, jb):
        new = []
        for g in groups:
            for hh in range(nh):
                cs = slice(hh * blk, (hh + 1) * blk)
                m_new = cmaxes[g * nh + hh]
                if stats is not None:
                    m, l = stats[g * nh + hh]
                    m_new = jnp.maximum(m, m_new)
                ps = [jnp.exp2(st_refs[slot][g, w, :, cs] - m_new) for w in range(2)]
                l_new = jnp.sum(ps[0], axis=0, keepdims=True) + jnp.sum(ps[1], axis=0, keepdims=True)
                vt = jnp.concatenate([vt_ref[g, ja, hh * d:(hh + 1) * d, :],
                                      vt_ref[g, jb, hh * d:(hh + 1) * d, :]], axis=1)
                pv = jnp.dot(vt, jnp.concatenate([_mx(ps[0]), _mx(ps[1])], axis=0),
                             preferred_element_type=F32)
                if stats is None:
                    acc_ref[g, hh] = pv
                    new.append((m_new, l_new))
                else:
                    alpha = jnp.exp2(m - m_new)
                    acc_ref[g, hh] = acc_ref[g, hh] * alpha + pv
                    new.append((m_new, alpha * l + l_new))
        return tuple(new)

    nfar = qb - 1
    npairs = (nfar + 1) // 2

    def far_blocks(i):
        return jnp.where(2 * i < nfar, 2 * i, nb), jnp.where(2 * i + 1 < nfar, 2 * i + 1, nb)

    j_prev = jnp.where(qb > 0, qb - 1, nb)
    cmax0 = score_stage(0, qb, j_prev, True)
    cmax1 = score_stage(1, *far_blocks(0), False)
    init = (attend_stage(0, cmax0, None, qb, j_prev), cmax1)

    def trip(slot, k, carry, prefetch=True):
        stats, cmax = carry
        cmax_next = score_stage(1 - slot, *far_blocks(k), False) if prefetch else cmax
        return attend_stage(slot, cmax, stats, *far_blocks(k - 1)), cmax_next

    def double_trip(i, carry):
        k = 2 * i + 1
        return trip(0, k + 1, trip(1, k, carry))

    carry = lax.fori_loop(0, npairs // 2, double_trip, init)
    stats, _ = lax.cond(npairs % 2 == 1, lambda cr: trip(1, npairs, cr, prefetch=False), lambda cr: cr, carry)
    for g in groups:
        out_t = jnp.concatenate([acc_ref[g, hh] / stats[g * nh + hh][1] for hh in range(nh)], axis=0)
        o_ref[0, :, g * LANES:(g + 1) * LANES] = out_t.T


MOBA_GROUPS = 4


def _moba(hp3d, bias):
    b, s, _ = hp3d.shape
    ng = MOBA_GROUPS
    wd = ng * LANES
    nsteps = A_WIDTH // wd
    hpp = LANES // A_HEAD_DIM
    nb = s // MOBA_BLOCK
    return pl.pallas_call(
        _moba_kernel,
        grid=(b, nsteps, nb),
        in_specs=[pl.BlockSpec((1, MOBA_BLOCK, wd), lambda bi, p, i: (bi, i, p)),
                  pl.BlockSpec((1, s, wd), lambda bi, p, i: (bi, 0, nsteps + p), pipeline_mode=pl.Buffered(1)),
                  pl.BlockSpec((1, s, wd), lambda bi, p, i: (bi, 0, 2 * nsteps + p), pipeline_mode=pl.Buffered(1)),
                  pl.BlockSpec((ng * hpp, 2, MOBA_BLOCK, MOBA_BLOCK), lambda bi, p, i: (p, 0, 0, 0))],
        out_specs=pl.BlockSpec((1, MOBA_BLOCK, wd), lambda bi, p, i: (bi, i, p)),
        out_shape=jax.ShapeDtypeStruct((b, s, A_WIDTH), F32),
        scratch_shapes=[pltpu.VMEM((ng, nb, LANES), F32),
                        pltpu.VMEM((ng, hpp, nb + 1, MOBA_BLOCK, LANES), MXU_DTYPE),
                        pltpu.VMEM((ng, nb + 1, LANES, MOBA_BLOCK), MXU_DTYPE),
                        pltpu.VMEM((ng, 2, MOBA_BLOCK, hpp * MOBA_BLOCK), F32),
                        pltpu.VMEM((ng, 2, MOBA_BLOCK, hpp * MOBA_BLOCK), F32),
                        pltpu.VMEM((ng, hpp, LANES, MOBA_BLOCK), MXU_DTYPE),
                        pltpu.VMEM((ng, hpp, A_HEAD_DIM, MOBA_BLOCK), F32)],
        compiler_params=_params("parallel", "parallel", "arbitrary"),
        name="moba",
    )(hp3d, hp3d, hp3d, bias)


S5_HALF_CH = S5_WIDTH // 2
S5_HALF_ST = (S5_GROUPS // 2) * S5_STATE
S5_TILE = 256
S5_NT = S5_HALF_ST // S5_TILE
S5_STEPS = 128
S5_PITCH = S5_STEPS + SUBLANES
S5_H_PITCH = S5_STEPS + SUBLANES // 2
S5_SLABS = 2 * S5_HALF_ST // LANES


def _s5_kernel(u0_ref, u_ref, wx_ref, wy_ref, a_ref, o_ref, st_ref, x0_scr, x1_scr, h0_scr, h1_scr):
    nb, steps, _ = u_ref.shape
    nseq = 2 * nb
    tile_slabs = S5_TILE // LANES
    nre = S5_SLABS // 2
    i = pl.program_id(0)

    def project_in_items(src_ref, x_scr):
        def item(m, part, n):
            def run():
                slab0 = (part * S5_HALF_ST + n * S5_TILE) // LANES
                for half in range(2):
                    lhs = jnp.concatenate(
                        [_mx(src_ref[bi, :, half * S5_HALF_CH + m * LANES:half * S5_HALF_CH + (m + 1) * LANES])
                         for bi in range(nb)], axis=0)
                    res = jnp.dot(lhs, wx_ref[part, n, half * LANES:(half + 1) * LANES, :],
                                  preferred_element_type=F32)
                    for bi in range(nb):
                        seq = 2 * bi + half
                        for sl in range(tile_slabs):
                            x_scr[slab0 + sl, seq * S5_PITCH:seq * S5_PITCH + steps, :] = (
                                res[bi * steps:(bi + 1) * steps, sl * LANES:(sl + 1) * LANES])
            return run
        return [item(m, part, n) for m in range(S5_NT // 2) for part in range(2) for n in (2 * m, 2 * m + 1)]

    def recur_items(x_scr, h_scr, nitems):
        state = {}

        def item(t0, t1):
            def run():
                a_re = [a_ref[0, :, s * LANES:(s + 1) * LANES] for s in range(nre)]
                a_im = [a_ref[1, :, s * LANES:(s + 1) * LANES] for s in range(nre)]
                if t0 == 0:
                    state["re"] = [st_ref[0, :, s * LANES:(s + 1) * LANES] for s in range(nre)]
                    state["im"] = [st_ref[1, :, s * LANES:(s + 1) * LANES] for s in range(nre)]
                h_re, h_im = state["re"], state["im"]
                for t in range(t0, t1):
                    rows_t = pl.ds(t, nseq, stride=S5_PITCH)
                    for s in range(nre):
                        hr, hi = h_re[s], h_im[s]
                        h_re[s] = a_re[s] * hr - a_im[s] * hi + x_scr[s, rows_t, :]
                        h_im[s] = a_re[s] * hi + a_im[s] * hr + x_scr[nre + s, rows_t, :]
                        h_scr[s, pl.ds(t, nseq, stride=S5_H_PITCH), :] = h_re[s]
                        h_scr[nre + s, pl.ds(t, nseq, stride=S5_H_PITCH), :] = h_im[s]
                if t1 == steps:
                    for s in range(nre):
                        st_ref[0, :, s * LANES:(s + 1) * LANES] = h_re[s]
                        st_ref[1, :, s * LANES:(s + 1) * LANES] = h_im[s]
            return run
        per = steps // nitems
        return [item(k * per, (k + 1) * per) for k in range(nitems)]

    def project_out_items(h_scr):
        def item(m, half):
            def run():
                acc = jnp.zeros((nb * steps, LANES), F32)
                for n in (2 * m, 2 * m + 1):
                    for part in range(2):
                        slab0 = (part * S5_HALF_ST + n * S5_TILE) // LANES
                        h_tile = jnp.concatenate(
                            [jnp.concatenate([h_scr[slab0 + sl,
                                                    (2 * bi + half) * S5_H_PITCH:(2 * bi + half) * S5_H_PITCH + steps, :]
                                              for sl in range(tile_slabs)], axis=1) for bi in range(nb)], axis=0)
                        acc = acc + jnp.dot(_mx(h_tile), wy_ref[part, n, :, half * LANES:(half + 1) * LANES],
                                            preferred_element_type=F32)
                c = half * S5_HALF_CH + m * LANES
                for bi in range(nb):
                    o_ref[bi, :, c:c + LANES] = acc[bi * steps:(bi + 1) * steps, :]
            return run
        return [item(m, half) for m in range(S5_NT // 2) for half in range(2)]

    @pl.when(i == 0)
    def _():
        st_ref[...] = jnp.zeros_like(st_ref)
        h1_scr[...] = jnp.zeros_like(h1_scr)
        for run in project_in_items(u0_ref, x0_scr):
            run()

    def step(x_cur, h_cur, x_nxt, h_prev):
        ins = project_in_items(u_ref, x_nxt)
        outs = project_out_items(h_prev)
        recs = recur_items(x_cur, h_cur, len(ins))
        for k, rec in enumerate(recs):
            rec()
            ins[k]()
            if k % 2 == 1:
                outs[k // 2]()

    @pl.when(i % 2 == 0)
    def _():
        step(x0_scr, h0_scr, x1_scr, h1_scr)

    @pl.when(i % 2 == 1)
    def _():
        step(x1_scr, h1_scr, x0_scr, h0_scr)


def _s5_weights(lam_re, lam_im, b_re, b_im, c_re, c_im, log_dt):
    lr = jnp.minimum(lam_re.astype(F32), -1e-4)
    li = lam_im.astype(F32)
    dt = jnp.exp(log_dt.astype(F32))[:, None]
    mag = jnp.exp(lr * dt)
    a_re = mag * jnp.cos(li * dt)
    a_im = mag * jnp.sin(li * dt)
    den = lr * lr + li * li
    coef_re = ((a_re - 1.0) * lr + a_im * li) / den
    coef_im = (a_im * lr - (a_re - 1.0) * li) / den
    br = b_re.astype(F32)
    bi = b_im.astype(F32)
    bb_re = coef_re[..., None] * br - coef_im[..., None] * bi
    bb_im = coef_re[..., None] * bi + coef_im[..., None] * br
    gh = S5_GROUPS // 2
    eye = jnp.eye(gh, dtype=F32)

    def b_dense(bb):
        t = bb.reshape(2, gh, S5_STATE, S5_GROUP)
        return jnp.einsum('fgph,gk->fghkp', t, eye).reshape(2, S5_HALF_CH, S5_HALF_ST)

    def c_dense(cc):
        t = cc.reshape(2, gh, S5_GROUP, S5_STATE)
        return jnp.einsum('fghp,gk->fgpkh', t, eye).reshape(2, S5_HALF_ST, S5_HALF_CH)

    def x_tiles(bd):
        tiles = []
        for n in range(S5_NT):
            c = (n // 2) * LANES
            tiles.append(jnp.concatenate([bd[0, c:c + LANES, n * S5_TILE:(n + 1) * S5_TILE],
                                          bd[1, c:c + LANES, n * S5_TILE:(n + 1) * S5_TILE]], axis=0))
        return jnp.stack(tiles)

    def y_tiles(cd):
        tiles = []
        for n in range(S5_NT):
            c = (n // 2) * LANES
            tiles.append(jnp.concatenate([cd[0, n * S5_TILE:(n + 1) * S5_TILE, c:c + LANES],
                                          cd[1, n * S5_TILE:(n + 1) * S5_TILE, c:c + LANES]], axis=1))
        return jnp.stack(tiles)

    wx = jnp.stack([x_tiles(b_dense(bb_re)), x_tiles(b_dense(bb_im))]).astype(MXU_DTYPE)
    wy = jnp.stack([y_tiles(c_dense(c_re.astype(F32))), y_tiles(c_dense(-c_im.astype(F32)))]).astype(MXU_DTYPE)
    return wx, wy, a_re.reshape(2, S5_HALF_ST), a_im.reshape(2, S5_HALF_ST)


def _s5_scan(hp3d, wx, wy, a_re2, a_im2):
    b, s, width = hp3d.shape
    nseq = 2 * b
    assert nseq == SUBLANES
    sel = jnp.arange(nseq) % 2
    a = jnp.stack([a_re2[sel], a_im2[sel]])
    scr = pltpu.VMEM((S5_SLABS, nseq * S5_PITCH, LANES), F32)
    hscr = pltpu.VMEM((S5_SLABS, nseq * S5_H_PITCH, LANES), F32)
    last = s // S5_STEPS - 1
    ucol = width // S5_WIDTH - 1
    return pl.pallas_call(
        _s5_kernel,
        grid=(last + 2,),
        in_specs=[pl.BlockSpec((b, S5_STEPS, S5_WIDTH), lambda i: (0, jnp.minimum(i, last), ucol)),
                  pl.BlockSpec((b, S5_STEPS, S5_WIDTH), lambda i: (0, jnp.minimum(i + 1, last), ucol)),
                  _resident(wx.shape), _resident(wy.shape), _resident(a.shape)],
        out_specs=pl.BlockSpec((b, S5_STEPS, S5_WIDTH), lambda i: (0, jnp.maximum(i - 1, 0), 0)),
        out_shape=jax.ShapeDtypeStruct((b, s, S5_WIDTH), F32),
        scratch_shapes=[pltpu.VMEM((2, nseq, S5_HALF_ST), F32), scr, scr, hscr, hscr],
        compiler_params=_params("arbitrary"),
        name="s5_scan",
    )(hp3d, hp3d, wx, wy, a)


MLP_CHUNK = 1024


def _tail_body(mix, x_ref, g1_ref, b1_ref, w1_ref, w2_ref, g2_ref, b2_ref, o_ref):
    x1 = _layer_norm(DN_ALPHA * x_ref[...] + mix, g1_ref[...], b1_ref[...])
    x1m = _mx(x1)
    acc = jnp.zeros(x1.shape, F32)
    for f in range(w1_ref.shape[1] // MLP_CHUNK):
        hid = jnp.dot(x1m, w1_ref[:, f * MLP_CHUNK:(f + 1) * MLP_CHUNK], preferred_element_type=F32)
        hid = jnp.square(jnp.maximum(hid, 0.0))
        acc = acc + jnp.dot(_mx(hid), w2_ref[f * MLP_CHUNK:(f + 1) * MLP_CHUNK, :],
                            preferred_element_type=F32)
    o_ref[...] = _layer_norm(DN_ALPHA * x1 + acc, g2_ref[...], b2_ref[...])


def _tail_even_kernel(attn_ref, y_ref, u_ref, dsk_ref, wg_ref, bg_ref, wo_ref, x_ref,
                      g1_ref, b1_ref, w1_ref, w2_ref, g2_ref, b2_ref, o_ref):
    y = y_ref[...] + dsk_ref[...] * u_ref[...]
    g = 0.5 * y * (1.0 + jnp.tanh(math.sqrt(2.0 / math.pi) * (y + 0.044715 * (y * y * y))))
    ssm = g * _sigmoid(jnp.dot(_mx(g), wg_ref[...], preferred_element_type=F32) + bg_ref[...])
    ka = attn_ref.shape[1]
    mix = (jnp.dot(_mx(attn_ref[...]), wo_ref[0:ka, :], preferred_element_type=F32)
           + jnp.dot(_mx(ssm), wo_ref[ka:, :], preferred_element_type=F32))
    _tail_body(mix, x_ref, g1_ref, b1_ref, w1_ref, w2_ref, g2_ref, b2_ref, o_ref)


def _tail_odd_kernel(a_ref, wo_ref, x_ref, g1_ref, b1_ref, w1_ref, w2_ref, g2_ref, b2_ref, o_ref):
    mix = jnp.dot(_mx(a_ref[...]), wo_ref[...], preferred_element_type=F32)
    _tail_body(mix, x_ref, g1_ref, b1_ref, w1_ref, w2_ref, g2_ref, b2_ref, o_ref)


def _row(v):
    return v.reshape(1, -1).astype(F32)


def _tail_common_specs(tm, d, dff):
    return [_resident((d, d)),
            pl.BlockSpec((tm, d), lambda i: (i, 0)),
            _resident((1, d)), _resident((1, d)),
            _resident((d, dff)), _resident((dff, d)),
            _resident((1, d)), _resident((1, d))]


def _tail_even(attn2d, y2d, hp2d, x2d, d_skip, w_glu, b_glu, w_out, g1, b1, w1, w2, g2, b2, tm):
    m, d = x2d.shape
    dff = w1.shape[1]
    ucol = hp2d.shape[1] // S5_WIDTH - 1
    return pl.pallas_call(
        _tail_even_kernel,
        grid=(m // tm,),
        in_specs=[pl.BlockSpec((tm, A_WIDTH), lambda i: (i, 0)),
                  pl.BlockSpec((tm, S5_WIDTH), lambda i: (i, 0)),
                  pl.BlockSpec((tm, S5_WIDTH), lambda i: (i, ucol)),
                  _resident((1, S5_WIDTH)), _resident((S5_WIDTH, S5_WIDTH)), _resident((1, S5_WIDTH))]
        + _tail_common_specs(tm, d, dff),
        out_specs=pl.BlockSpec((tm, d), lambda i: (i, 0)),
        out_shape=jax.ShapeDtypeStruct((m, d), F32),
        compiler_params=_params("parallel"),
        name="tail_even",
    )(attn2d, y2d, hp2d, _row(d_skip), _mx(w_glu), _row(b_glu), _mx(w_out), x2d,
      _row(g1), _row(b1), _mx(w1), _mx(w2), _row(g2), _row(b2))


def _tail_odd(a2d, x2d, w_out, g1, b1, w1, w2, g2, b2, tm):
    m, d = x2d.shape
    dff = w1.shape[1]
    return pl.pallas_call(
        _tail_odd_kernel,
        grid=(m // tm,),
        in_specs=[pl.BlockSpec((tm, a2d.shape[1]), lambda i: (i, 0))] + _tail_common_specs(tm, d, dff),
        out_specs=pl.BlockSpec((tm, d), lambda i: (i, 0)),
        out_shape=jax.ShapeDtypeStruct((m, d), F32),
        compiler_params=_params("parallel"),
        name="tail_odd",
    )(a2d, _mx(w_out), x2d, _row(g1), _row(b1), _mx(w1), _mx(w2), _row(g2), _row(b2))


GDN_TILE = 256
GDN_GATES_TILE = 2048


def _softplus(t):
    return jnp.maximum(t, 0.0) + jnp.log1p(jnp.exp(-jnp.abs(t)))


def _gdn_prep_body(i, hg, q_ref, k_ref, v_ref, qp_ref, kp_ref, vp_ref, cwq_ref, cwk_ref, cwv_ref,
                   ba_ref, g_ref, gt_ref, u_ref, w_ref, qd_ref, kd_ref, intra_ref, edl_ref,
                   qx_ref, kx_ref, vx_ref, ql_ref, kl_ref, vl_ref):
    assert C_CONV == 4
    t, width = q_ref.shape[1], q_ref.shape[2]
    c = C_CHUNK
    dk = C_HEAD_DIM
    hp = width // dk
    groups = t // SUBLANES

    def conv_silu(main_ref, prev_ref, cw_ref, ext_ref, low_ref):
        x = main_ref[0]
        ext_ref[0:SUBLANES, :] = jnp.zeros((SUBLANES, width), F32)
        ext_ref[SUBLANES:2 * SUBLANES, :] = jnp.where(i > 0, prev_ref[0], 0.0)
        ext_ref[2 * SUBLANES:, :] = x
        ext_rows = t + SUBLANES

        def tap(j):
            return jnp.broadcast_to(cw_ref[j:j + 1, :], (SUBLANES, width))[None]

        def grouped(a):
            return a.reshape(a.shape[0] // SUBLANES, SUBLANES, width)

        x_ext = grouped(ext_ref[SUBLANES:, :])
        x_del = grouped(ext_ref[SUBLANES - 1:SUBLANES - 1 + ext_rows, :])
        low_ref[...] = (tap(1) * x_ext + tap(0) * x_del).reshape(ext_rows, width)
        y = (tap(3) * grouped(x) + tap(2) * x_del[1:]
             + grouped(low_ref[SUBLANES - 2:SUBLANES - 2 + t, :]))
        hy = 0.5 * y.reshape(t, width)
        return hy + hy * jnp.tanh(hy)

    def l2n(z, scale):
        return z * (lax.rsqrt(jnp.sum(z * z, axis=-1, keepdims=True) + L2_EPS) * scale)

    q_all = conv_silu(q_ref, qp_ref, cwq_ref, qx_ref, ql_ref)
    k_all = conv_silu(k_ref, kp_ref, cwk_ref, kx_ref, kl_ref)
    v_all = conv_silu(v_ref, vp_ref, cwv_ref, vx_ref, vl_ref)
    heads = [slice(hh * dk, (hh + 1) * dk) for hh in range(hp)]
    qs = [l2n(q_all[:, cs], dk ** -0.5) for cs in heads]
    ks = [l2n(k_all[:, cs], 1.0) for cs in heads]
    vs = [v_all[:, cs] for cs in heads]

    ba = ba_ref[0]
    lane = lax.broadcasted_iota(jnp.int32, (t, LANES), 1)
    betas, g_cols, g_rows = [], [], []
    for hh in range(hp):
        h = hg * hp + hh
        betas.append(_sigmoid(jnp.sum(jnp.where(lane == h, ba, 0.0), axis=1, keepdims=True)))
        g_cols.append(jnp.sum(jnp.where(lane == C_HEADS + h, g_ref[0], 0.0), axis=1, keepdims=True))
        g_rows.append(gt_ref[0, pl.ds(C_HEADS + h, 1), :])

    units = [(hh, slice(ch * c, (ch + 1) * c)) for hh in range(hp) for ch in range(t // c)]
    _gdn_prep_units(units, qs, ks, vs, betas, g_cols, g_rows, heads, u_ref, w_ref, qd_ref, kd_ref, intra_ref, edl_ref)


def _gdn_prep_units(units, qs, ks, vs, betas, g_cols, g_rows, heads, u_ref, w_ref, qd_ref, kd_ref, intra_ref, edl_ref):
    c = C_CHUNK
    dk = C_HEAD_DIM
    assert 2 * c == LANES and len(units) % 2 == 0
    ii = lax.broadcasted_iota(jnp.int32, (c, c), 0)
    jj = lax.broadcasted_iota(jnp.int32, (c, c), 1)
    row = lax.broadcasted_iota(jnp.int32, (c, 2 * c), 0)
    lane = lax.broadcasted_iota(jnp.int32, (c, 2 * c), 1)
    left = lane < c
    col = jnp.where(left, lane, lane - c)
    low = row >= col
    pairs = [(2 * n, 2 * n + 1) for n in range(len(units) // 2)]

    def side_by_side(a, b):
        za, zb = jnp.zeros_like(a), jnp.zeros_like(b)
        return jnp.concatenate([jnp.concatenate([a, zb], axis=1), jnp.concatenate([za, b], axis=1)], axis=0)

    def block_diag(p):
        return jnp.concatenate([jnp.where(left, p, 0.0), jnp.where(left, 0.0, p)], axis=0)

    gc_cols = [jnp.sum(jnp.where(ii >= jj, g_rows[hh][:, r], 0.0), axis=1, keepdims=True) for hh, r in units]
    gc_rows = [jnp.sum(jnp.where(ii <= jj, g_cols[hh][r], 0.0), axis=0, keepdims=True) for hh, r in units]
    decays = [jnp.where(low, jnp.exp(jnp.where(low, jnp.where(left, gc_cols[a], gc_cols[b])
                                               - jnp.concatenate([gc_rows[a], gc_rows[b]], axis=1), 0.0)), 0.0)
              for a, b in pairs]
    beta_bs = [jnp.broadcast_to(betas[hh][r], (c, dk)) for hh, r in units]
    kbs = [ks[hh][r] * bb for bb, (hh, r) in zip(beta_bs, units)]
    kdiag = [_mx(side_by_side(ks[units[a][0]][units[a][1]], ks[units[b][0]][units[b][1]])) for a, b in pairs]
    kks = [_dot_nt(jnp.concatenate([kbs[a], kbs[b]], axis=1), kd) for (a, b), kd in zip(pairs, kdiag)]
    qks = [_dot_nt(jnp.concatenate([qs[units[a][0]][units[a][1]], qs[units[b][0]][units[b][1]]], axis=1), kd)
           for (a, b), kd in zip(pairs, kdiag)]
    pws = [jnp.where(row > col, -(kk * dec), 0.0) for kk, dec in zip(kks, decays)]
    tms = pws
    pds = [_mx(block_diag(pw)) for pw in pws]
    for _ in range(int(math.log2(c)) - 1):
        pws = [jnp.dot(_mx(pw), pd, preferred_element_type=F32) for pw, pd in zip(pws, pds)]
        pds = [_mx(block_diag(pw)) for pw in pws]
        tms = [tm + pw + jnp.dot(_mx(tm), pd, preferred_element_type=F32) for tm, pw, pd in zip(tms, pws, pds)]
    egcs = [jnp.broadcast_to(jnp.exp(a), (c, dk)) for a in gc_cols]
    vbs = [vs[hh][r] * bb for bb, (hh, r) in zip(beta_bs, units)]
    kbes = [kb * e for kb, e in zip(kbs, egcs)]
    tbs = [_mx(tm) for tm in tms]
    tvs = [jnp.dot(tb, _mx(side_by_side(vbs[a], vbs[b])), preferred_element_type=F32) for tb, (a, b) in zip(tbs, pairs)]
    tks = [jnp.dot(tb, _mx(side_by_side(kbes[a], kbes[b])), preferred_element_type=F32) for tb, (a, b) in zip(tbs, pairs)]
    for n, (hh, r) in enumerate(units):
        gl = gc_cols[n][c - 1:c, :]
        cs = heads[hh]
        ch = r.start // c
        half = slice((n % 2) * dk, (n % 2 + 1) * dk)
        u_ref[0, r, cs] = vbs[n] + tvs[n // 2][:, half]
        w_ref[0, r, cs] = _mx(kbes[n] + tks[n // 2][:, half])
        qd_ref[0, r, cs] = _mx(qs[hh][r] * egcs[n])
        kd_ref[0, r, cs] = _mx(ks[hh][r] * jnp.exp(gl - gc_cols[n]))
        intra_ref[0, hh, r, :] = _mx((qks[n // 2] * decays[n // 2])[:, (n % 2) * c:(n % 2 + 1) * c])
        edl_ref[0, hh, ch:ch + 1, :] = jnp.broadcast_to(jnp.exp(gl), (1, LANES))


def _gdn_gates_kernel(ba_ref, alog_ref, dtb_ref, g_ref, gt_ref):
    g_all = -jnp.exp(alog_ref[...]) * _softplus(ba_ref[0] + dtb_ref[...])
    g_ref[0] = g_all
    gt_ref[0] = g_all.T


def _gdn_gates(hp3d, alog_row, dtb_row):
    b, s, _ = hp3d.shape
    t = min(GDN_GATES_TILE, s)
    ba_col = 4 * C_WIDTH // LANES
    return pl.pallas_call(
        _gdn_gates_kernel,
        grid=(b, s // t),
        in_specs=[pl.BlockSpec((1, t, LANES), lambda bi, i: (bi, i, ba_col)),
                  pl.BlockSpec((1, LANES), lambda bi, i: (0, 0)),
                  pl.BlockSpec((1, LANES), lambda bi, i: (0, 0))],
        out_specs=[pl.BlockSpec((1, t, LANES), lambda bi, i: (bi, i, 0)),
                   pl.BlockSpec((1, LANES, t), lambda bi, i: (bi, 0, i))],
        out_shape=[jax.ShapeDtypeStruct((b, s, LANES), F32), jax.ShapeDtypeStruct((b, LANES, s), F32)],
        compiler_params=_params("parallel", "parallel"),
        name="gdn_gates",
    )(hp3d, alog_row, dtb_row)


def _gdn_scan_body(chunk0, u_ref, w_ref, qd_ref, kd_ref, intra_ref, edl_ref, gate_ref, nw_ref, o_ref, st_ref):
    c = C_CHUNK
    dk = C_HEAD_DIM
    t = u_ref.shape[1]
    nw = nw_ref[...]
    heads = [slice(h * dk, (h + 1) * dk) for h in range(C_HEADS)]
    for ch in range(t // c):
        r = slice(ch * c, (ch + 1) * c)
        sts = [st_ref[h] for h in range(C_HEADS)]
        wss = [_dot(jnp.concatenate([w_ref[0, r, cs], qd_ref[0, r, cs]], axis=0), st)
               for cs, st in zip(heads, sts)]
        v_news = [u_ref[0, r, cs] - ws[:c] for cs, ws in zip(heads, wss)]
        kvs = [_dot_tn(kd_ref[0, r, cs], v_new) for cs, v_new in zip(heads, v_news)]
        os_ = [ws[c:] + _dot(intra_ref[0, h, r, :], v_new) for h, (ws, v_new) in enumerate(zip(wss, v_news))]
        for h, cs in enumerate(heads):
            st_ref[h] = sts[h] * edl_ref[0, h, pl.ds(chunk0 + ch, 1), 0:1] + kvs[h]
            o = os_[h]
            o = o * lax.rsqrt(jnp.mean(o * o, axis=-1, keepdims=True) + RMS_EPS) * nw
            hg = 0.5 * gate_ref[0, r, cs]
            o_ref[0, r, cs] = o * (hg + hg * jnp.tanh(hg))


def _gdn_kernel(q_ref, k_ref, v_ref, qp_ref, kp_ref, vp_ref, cwq_ref, cwk_ref, cwv_ref, ba_ref, g_ref, gt_ref,
                gate_ref, nw_ref, o_ref, st_ref, qx_ref, kx_ref, vx_ref, ql_ref, kl_ref, vl_ref, *bufs):
    j = pl.program_id(1)
    last = pl.num_programs(1) - 2
    set_a, set_b = bufs[:len(bufs) // 2], bufs[len(bufs) // 2:]

    @pl.when(j == 0)
    def _():
        st_ref[...] = jnp.zeros_like(st_ref)
        for ref in bufs:
            ref[...] = jnp.zeros_like(ref)

    def step(cur, nxt):
        _gdn_prep_body(jnp.minimum(j, last), 0, q_ref, k_ref, v_ref, qp_ref, kp_ref, vp_ref,
                       cwq_ref, cwk_ref, cwv_ref, ba_ref, g_ref, gt_ref, *nxt,
                       qx_ref, kx_ref, vx_ref, ql_ref, kl_ref, vl_ref)
        _gdn_scan_body(0, *cur, gate_ref, nw_ref, o_ref, st_ref)

    @pl.when(j % 2 == 0)
    def _():
        step(set_a, set_b)

    @pl.when(j % 2 == 1)
    def _():
        step(set_b, set_a)


def _gdn(hp3d, conv_w, alog_row, dtb_row, norm_w):
    b, s, _ = hp3d.shape
    ba_col = 4 * C_WIDTH // LANES
    t = GDN_TILE
    nt = s // t
    per = t // SUBLANES
    wd = C_WIDTH
    nc = t // C_CHUNK
    g, gt = _gdn_gates(hp3d, alog_row, dtb_row)

    def prep_tile(j):
        return jnp.minimum(j, nt - 1)

    def scan_tile(j):
        return jnp.maximum(j - 1, 0)

    def main(off):
        return pl.BlockSpec((1, t, wd), lambda bi, j: (bi, prep_tile(j), off))

    def prev(off):
        return pl.BlockSpec((1, SUBLANES, wd), lambda bi, j: (bi, jnp.maximum(prep_tile(j) * per - 1, 0), off))

    def cw(off):
        return pl.BlockSpec((C_CONV, wd), lambda bi, j: (0, off))

    bufs = [pltpu.VMEM((1, t, wd), F32)] + [pltpu.VMEM((1, t, wd), MXU_DTYPE)] * 3 + [
        pltpu.VMEM((1, C_HEADS, t, C_CHUNK), MXU_DTYPE), pltpu.VMEM((1, C_HEADS, max(nc, SUBLANES), LANES), F32)]
    return pl.pallas_call(
        _gdn_kernel,
        grid=(b, nt + 1),
        in_specs=[main(0), main(1), main(2), prev(0), prev(1), prev(2), cw(0), cw(1), cw(2),
                  pl.BlockSpec((1, t, LANES), lambda bi, j: (bi, prep_tile(j), ba_col)),
                  pl.BlockSpec((1, t, LANES), lambda bi, j: (bi, prep_tile(j), 0)),
                  pl.BlockSpec((1, LANES, t), lambda bi, j: (bi, 0, prep_tile(j))),
                  pl.BlockSpec((1, t, wd), lambda bi, j: (bi, scan_tile(j), 3)),
                  pl.BlockSpec((1, C_HEAD_DIM), lambda bi, j: (0, 0))],
        out_specs=pl.BlockSpec((1, t, wd), lambda bi, j: (bi, scan_tile(j), 0)),
        out_shape=jax.ShapeDtypeStruct((b, s, wd), F32),
        scratch_shapes=[pltpu.VMEM((C_HEADS, C_HEAD_DIM, C_HEAD_DIM), F32)]
        + [pltpu.VMEM((t + 2 * SUBLANES, wd), F32)] * 3 + [pltpu.VMEM((t + SUBLANES, wd), F32)] * 3
        + bufs + bufs,
        compiler_params=_params("parallel", "arbitrary"),
        name="gdn",
    )(hp3d, hp3d, hp3d, hp3d, hp3d, hp3d, conv_w, conv_w, conv_w, hp3d, g, gt, hp3d, _row(norm_w))


ROW_TILE = 512


def kernel(x, rel_table, ev_w_in, ev_w_out, s5_lambda_re, s5_lambda_im, s5_b_re, s5_b_im, s5_c_re, s5_c_im, s5_d, s5_log_dt, s5_w_glu, s5_b_glu, od_w_in, od_conv_w, od_a_log, od_dt_bias, od_norm_w, od_w_out, ln_mix_g, ln_mix_b, mlp_w1, mlp_w2, ln_ffn_g, ln_ffn_b):
    bsz, s, d = x.shape
    m = bsz * s
    x2d = x.reshape(m, d)
    table_t = rel_table.T.astype(F32)
    bias = _bias_tiles(table_t)
    for layer in range(DEPTH):
        i = layer // 2
        if layer % 2 == 0:
            hp = _proj(x2d, _mx(ev_w_in[i]), ROW_TILE)
            hp3d = hp.reshape(bsz, s, hp.shape[1])
            attn = _moba(hp3d, bias)
            wx, wy, a_re2, a_im2 = _s5_weights(s5_lambda_re[i], s5_lambda_im[i], s5_b_re[i], s5_b_im[i],
                                               s5_c_re[i], s5_c_im[i], s5_log_dt[i])
            y = _s5_scan(hp3d, wx, wy, a_re2, a_im2)
            x2d = _tail_even(attn.reshape(m, A_WIDTH), y.reshape(m, S5_WIDTH), hp, x2d,
                             s5_d[i], s5_w_glu[i], s5_b_glu[i], ev_w_out[i],
                             ln_mix_g[layer], ln_mix_b[layer], mlp_w1[layer], mlp_w2[layer],
                             ln_ffn_g[layer], ln_ffn_b[layer], ROW_TILE)
        else:
            w_in = od_w_in[i]
            w_all = jnp.pad(w_in, ((0, 0), (0, LANES - 2 * C_HEADS)))
            hp = _proj(x2d, _mx(w_all), ROW_TILE)
            hp3d = hp.reshape(bsz, s, 4 * C_WIDTH + LANES)
            pad8 = jnp.zeros((C_HEADS,), F32)
            padr = jnp.zeros((LANES - 2 * C_HEADS,), F32)
            alog_row = jnp.concatenate([pad8, od_a_log[i].astype(F32), padr]).reshape(1, LANES)
            dtb_row = jnp.concatenate([pad8, od_dt_bias[i].astype(F32), padr]).reshape(1, LANES)
            o = _gdn(hp3d, od_conv_w[i].astype(F32), alog_row, dtb_row, od_norm_w[i])
            x2d = _tail_odd(o.reshape(m, C_WIDTH), x2d, od_w_out[i],
                            ln_mix_g[layer], ln_mix_b[layer], mlp_w1[layer], mlp_w2[layer],
                            ln_ffn_g[layer], ln_ffn_b[layer], ROW_TILE)
    return x2d.reshape(bsz, s, d)
```

```python
import functools
import math

import jax
import jax.numpy as jnp
import numpy as np
from jax import lax
from jax.experimental import pallas as pl
from jax.experimental.pallas import tpu as pltpu

F32 = jnp.float32
MXU_DTYPE = jnp.bfloat16

LANES = 128
SUBLANES = 8
VMEM_LIMIT_BYTES = 56 * 1024 * 1024

A_HEADS = 8
A_HEAD_DIM = 64
A_WIDTH = A_HEADS * A_HEAD_DIM
MOBA_BLOCK = 256
MOBA_TOPK = 3
REL_BUCKETS = 32
REL_MAX_DIST = 128
S5_GROUP = 16
S5_GROUPS = 32
S5_STATE = 64
S5_WIDTH = S5_GROUP * S5_GROUPS
C_HEADS = 8
C_HEAD_DIM = 128
C_WIDTH = C_HEADS * C_HEAD_DIM
C_CONV = 4
C_CHUNK = 64
DEPTH = 2
DN_ALPHA = (2 * DEPTH) ** 0.25
LN_EPS = 1e-5
RMS_EPS = 1e-6
L2_EPS = 1e-6
NEG_INF = -1e30
LOG2E = math.log2(math.e)


def _mx(t):
    return t.astype(MXU_DTYPE)


def _dot(a, b):
    return jnp.dot(_mx(a), _mx(b), preferred_element_type=F32)


def _dot_tn(a, b):
    return lax.dot_general(_mx(a), _mx(b), (((0,), (0,)), ((), ())), preferred_element_type=F32)


def _dot_nt(a, b):
    return lax.dot_general(_mx(a), _mx(b), (((1,), (1,)), ((), ())), preferred_element_type=F32)


def _sigmoid(t):
    return 1.0 / (1.0 + jnp.exp(-t))


def _layer_norm(r, g, b):
    mu = jnp.mean(r, axis=-1, keepdims=True)
    d = r - mu
    var = jnp.mean(d * d, axis=-1, keepdims=True)
    return d * lax.rsqrt(var + LN_EPS) * g + b


def _params(*sem):
    return pltpu.CompilerParams(dimension_semantics=sem, vmem_limit_bytes=VMEM_LIMIT_BYTES)


def _resident(shape):
    return pl.BlockSpec(shape, lambda *_: (0,) * len(shape), pipeline_mode=pl.Buffered(1))


PROJ_CHUNK = 1024


def _proj_kernel(x_ref, w_ref, o_ref):
    xm = _mx(x_ref[...])
    n = w_ref.shape[1]
    for c0 in range(0, n, PROJ_CHUNK):
        c1 = min(c0 + PROJ_CHUNK, n)
        o_ref[:, c0:c1] = jnp.dot(xm, w_ref[:, c0:c1], preferred_element_type=F32)


def _proj(x2d, w, tm):
    m, k = x2d.shape
    n = w.shape[1]
    return pl.pallas_call(
        _proj_kernel,
        grid=(m // tm,),
        in_specs=[pl.BlockSpec((tm, k), lambda i: (i, 0)), _resident((k, n))],
        out_specs=pl.BlockSpec((tm, n), lambda i: (i, 0)),
        out_shape=jax.ShapeDtypeStruct((m, n), F32),
        compiler_params=_params("parallel"),
        name="in_proj",
    )(x2d, w)


def _t5_bucket_np(dist):
    max_exact = REL_BUCKETS // 2
    n = np.maximum(dist, 0)
    nf = np.maximum(n, 1).astype(np.float32)
    large = max_exact + (np.log(nf / np.float32(max_exact)) / np.float32(math.log(REL_MAX_DIST / max_exact))
                         * np.float32(REL_BUCKETS - max_exact)).astype(np.int32)
    large = np.minimum(large, REL_BUCKETS - 1)
    return np.where(n < max_exact, n, large).astype(np.int32)


def _bias_kernel(tab_ref, bkt_ref, o_ref):
    h = pl.program_id(0)
    blk = bkt_ref.shape[-1]
    key = lax.broadcasted_iota(jnp.int32, (blk, blk), 0)
    qry = lax.broadcasted_iota(jnp.int32, (blk, blk), 1)
    far = tab_ref[h, REL_BUCKETS - 1]
    for t in range(2):
        bk = bkt_ref[t]
        acc = jnp.zeros((blk, blk), F32)
        for i in range(REL_BUCKETS):
            acc = jnp.where(bk == i, (tab_ref[h, i] - far) * LOG2E, acc)
        if t == 0:
            acc = jnp.where(qry >= key, acc, NEG_INF)
        o_ref[0, t] = acc


def _bias_tiles(table_t):
    heads = table_t.shape[0]
    offs = np.arange(MOBA_BLOCK)
    dist_own = offs[None, :] - offs[:, None]
    buckets = np.stack([_t5_bucket_np(dist_own), _t5_bucket_np(dist_own + MOBA_BLOCK)])
    return pl.pallas_call(
        _bias_kernel,
        grid=(heads,),
        in_specs=[pl.BlockSpec(memory_space=pltpu.SMEM),
                  pl.BlockSpec((2, MOBA_BLOCK, MOBA_BLOCK), lambda h: (0, 0, 0))],
        out_specs=pl.BlockSpec((1, 2, MOBA_BLOCK, MOBA_BLOCK), lambda h: (h, 0, 0, 0)),
        out_shape=jax.ShapeDtypeStruct((heads, 2, MOBA_BLOCK, MOBA_BLOCK), F32),
        compiler_params=_params("parallel"),
        name="t5_bias",
    )(table_t, jnp.asarray(buckets))


def _moba_kernel(q_ref, k_ref, v_ref, bias_ref, o_ref, kmean_ref, kb_ref, vt_ref,
                 st0_ref, st1_ref, wq_ref, acc_ref):
    st_refs = (st0_ref, st1_ref)
    qb = pl.program_id(2)
    blk = MOBA_BLOCK
    d = A_HEAD_DIM
    nh = LANES // d
    nb = k_ref.shape[1] // blk
    ng = q_ref.shape[2] // LANES
    groups = range(ng)

    @pl.when(qb == 0)
    def _():
        lane = lax.broadcasted_iota(jnp.int32, (blk, LANES - d), 1)
        for g in groups:
            gl = slice(g * LANES, (g + 1) * LANES)
            for j in range(nb + 1):
                onehot = jnp.where(lane == j, 1.0, 0.0).astype(MXU_DTYPE)
                for hh in range(nh):
                    kb_ref[g, hh, j, :, d:LANES] = onehot
            for j in range(nb):
                kj = k_ref[0, j * blk:(j + 1) * blk, gl]
                for hh in range(nh):
                    kb_ref[g, hh, j, :, 0:d] = _mx(kj[:, hh * d:(hh + 1) * d])
                kmean_ref[g, j:j + 1, :] = jnp.mean(kj, axis=0, keepdims=True)
                vt_ref[g, j] = _mx(v_ref[0, j * blk:(j + 1) * blk, gl].T)
            for hh in range(nh):
                kb_ref[g, hh, nb, :, 0:d] = jnp.zeros((blk, d), MXU_DTYPE)
            vt_ref[g, nb] = jnp.zeros((LANES, blk), MXU_DTYPE)

    blk_id = lax.broadcasted_iota(jnp.int32, (nb, blk), 0)
    row = lax.broadcasted_iota(jnp.int32, (LANES - d - nb, blk), 0)
    for g in groups:
        qt = (q_ref[0, :, g * LANES:(g + 1) * LANES] * (d ** -0.5)).T
        for hh in range(nh):
            gate = lax.dot_general(kmean_ref[g, :, hh * d:(hh + 1) * d], qt[hh * d:(hh + 1) * d],
                                   (((1,), (0,)), ((), ())),
                                   precision=lax.Precision.HIGHEST, preferred_element_type=F32)
            gt = jnp.where(blk_id < qb, gate, -jnp.inf)
            sel = jnp.zeros((nb, blk), F32)
            for _ in range(MOBA_TOPK):
                m = jnp.max(gt, axis=0, keepdims=True)
                is_m = jnp.logical_and(gt == m, gt > -jnp.inf)
                idx = jnp.min(jnp.where(is_m, blk_id, nb), axis=0, keepdims=True)
                pick = blk_id == idx
                sel = jnp.where(pick, 1.0, sel)
                gt = jnp.where(pick, -jnp.inf, gt)
            keep = jnp.logical_or(sel > 0.0, blk_id == qb)
            wq_ref[g, hh] = _mx(jnp.concatenate([qt[hh * d:(hh + 1) * d] * LOG2E, jnp.where(keep, 0.0, NEG_INF),
                                                 jnp.where(row == 0, NEG_INF, 0.0)], axis=0))

    def score_stage(slot, ja, jb, with_bias):
        cmaxes = []
        for g in groups:
            for hh in range(nh):
                keys = jnp.concatenate([kb_ref[g, hh, ja], kb_ref[g, hh, jb]], axis=0)
                st = jnp.dot(keys, wq_ref[g, hh], preferred_element_type=F32)
                if with_bias:
                    st = st + jnp.concatenate([bias_ref[g * nh + hh, w] for w in range(2)], axis=0)
                st_refs[slot][g, 0, :, hh * blk:(hh + 1) * blk] = st[:blk]
                st_refs[slot][g, 1, :, hh * blk:(hh + 1) * blk] = st[blk:]
                cmaxes.append(jnp.max(st, axis=0, keepdims=True))
        return tuple(cmaxes)

    def attend_stage(slot, cmaxes, stats, ja, jb):
        new = []
        for g in groups:
            for hh in range(nh):
                cs = slice(hh * blk, (hh + 1) * blk)
                m_new = cmaxes[g * nh + hh]
                if stats is not None:
                    m, l = stats[g * nh + hh]
                    m_new = jnp.maximum(m, m_new)
                ps = [jnp.exp2(st_refs[slot][g, w, :, cs] - m_new) for w in range(2)]
                l_new = jnp.sum(ps[0], axis=0, keepdims=True) + jnp.sum(ps[1], axis=0, keepdims=True)
                vt = jnp.concatenate([vt_ref[g, ja, hh * d:(hh + 1) * d, :],
                                      vt_ref[g, jb, hh * d:(hh + 1) * d, :]], axis=1)
                pv = jnp.dot(vt, jnp.concatenate([_mx(ps[0]), _mx(ps[1])], axis=0),
                             preferred_element_type=F32)
                if stats is None:
                    acc_ref[g, hh] = pv
                    new.append((m_new, l_new))
                else:
                    alpha = jnp.exp2(m - m_new)
                    acc_ref[g, hh] = acc_ref[g, hh] * alpha + pv
                    new.append((m_new, alpha * l + l_new))
        return tuple(new)

    nfar = qb - 1
    npairs = (nfar + 1) // 2

    def far_blocks(i):
        return jnp.where(2 * i < nfar, 2 * i, nb), jnp.where(2 * i + 1 < nfar, 2 * i + 1, nb)

    j_prev = jnp.where(qb > 0, qb - 1, nb)
    cmax0 = score_stage(0, qb, j_prev, True)
    cmax1 = score_stage(1, *far_blocks(0), False)
    init = (attend_stage(0, cmax0, None, qb, j_prev), cmax1)

    def trip(slot, k, carry, prefetch=True):
        stats, cmax = carry
        cmax_next = score_stage(1 - slot, *far_blocks(k), False) if prefetch else cmax
        return attend_stage(slot, cmax, stats, *far_blocks(k - 1)), cmax_next

    def double_trip(i, carry):
        k = 2 * i + 1
        return trip(0, k + 1, trip(1, k, carry))

    carry = lax.fori_loop(0, npairs // 2, double_trip, init)
    stats, _ = lax.cond(npairs % 2 == 1, lambda cr: trip(1, npairs, cr, prefetch=False), lambda cr: cr, carry)
    for g in groups:
        out_t = jnp.concatenate([acc_ref[g, hh] / stats[g * nh + hh][1] for hh in range(nh)], axis=0)
        o_ref[0, :, g * LANES:(g + 1) * LANES] = out_t.T


MOBA_GROUPS = 4


def _moba(hp3d, bias):
    b, s, _ = hp3d.shape
    ng = MOBA_GROUPS
    wd = ng * LANES
    nsteps = A_WIDTH // wd
    hpp = LANES // A_HEAD_DIM
    nb = s // MOBA_BLOCK
    return pl.pallas_call(
        _moba_kernel,
        grid=(b, nsteps, nb),
        in_specs=[pl.BlockSpec((1, MOBA_BLOCK, wd), lambda bi, p, i: (bi, i, p)),
                  pl.BlockSpec((1, s, wd), lambda bi, p, i: (bi, 0, nsteps + p), pipeline_mode=pl.Buffered(1)),
                  pl.BlockSpec((1, s, wd), lambda bi, p, i: (bi, 0, 2 * nsteps + p), pipeline_mode=pl.Buffered(1)),
                  pl.BlockSpec((ng * hpp, 2, MOBA_BLOCK, MOBA_BLOCK), lambda bi, p, i: (p, 0, 0, 0))],
        out_specs=pl.BlockSpec((1, MOBA_BLOCK, wd), lambda bi, p, i: (bi, i, p)),
        out_shape=jax.ShapeDtypeStruct((b, s, A_WIDTH), F32),
        scratch_shapes=[pltpu.VMEM((ng, nb, LANES), F32),
                        pltpu.VMEM((ng, hpp, nb + 1, MOBA_BLOCK, LANES), MXU_DTYPE),
                        pltpu.VMEM((ng, nb + 1, LANES, MOBA_BLOCK), MXU_DTYPE),
                        pltpu.VMEM((ng, 2, MOBA_BLOCK, hpp * MOBA_BLOCK), F32),
                        pltpu.VMEM((ng, 2, MOBA_BLOCK, hpp * MOBA_BLOCK), F32),
                        pltpu.VMEM((ng, hpp, LANES, MOBA_BLOCK), MXU_DTYPE),
                        pltpu.VMEM((ng, hpp, A_HEAD_DIM, MOBA_BLOCK), F32)],
        compiler_params=_params("parallel", "parallel", "arbitrary"),
        name="moba",
    )(hp3d, hp3d, hp3d, bias)


S5_HALF_CH = S5_WIDTH // 2
S5_HALF_ST = (S5_GROUPS // 2) * S5_STATE
S5_TILE = 256
S5_NT = S5_HALF_ST // S5_TILE
S5_STEPS = 128
S5_PITCH = S5_STEPS + SUBLANES
S5_H_PITCH = S5_STEPS + SUBLANES // 2
S5_SLABS = 2 * S5_HALF_ST // LANES


def _s5_kernel(u0_ref, u_ref, wx_ref, wy_ref, a_ref, o_ref, st_ref, x0_scr, x1_scr, h0_scr, h1_scr):
    nb, steps, _ = u_ref.shape
    nseq = 2 * nb
    tile_slabs = S5_TILE // LANES
    nre = S5_SLABS // 2
    i = pl.program_id(0)

    def project_in_items(src_ref, x_scr):
        def item(m, part, n):
            def run():
                slab0 = (part * S5_HALF_ST + n * S5_TILE) // LANES
                for half in range(2):
                    lhs = jnp.concatenate(
                        [_mx(src_ref[bi, :, half * S5_HALF_CH + m * LANES:half * S5_HALF_CH + (m + 1) * LANES])
                         for bi in range(nb)], axis=0)
                    res = jnp.dot(lhs, wx_ref[part, n, half * LANES:(half + 1) * LANES, :],
                                  preferred_element_type=F32)
                    for bi in range(nb):
                        seq = 2 * bi + half
                        for sl in range(tile_slabs):
                            x_scr[slab0 + sl, seq * S5_PITCH:seq * S5_PITCH + steps, :] = (
                                res[bi * steps:(bi + 1) * steps, sl * LANES:(sl + 1) * LANES])
            return run
        return [item(m, part, n) for m in range(S5_NT // 2) for part in range(2) for n in (2 * m, 2 * m + 1)]

    def recur_items(x_scr, h_scr, nitems):
        state = {}

        def item(t0, t1):
            def run():
                a_re = [a_ref[0, :, s * LANES:(s + 1) * LANES] for s in range(nre)]
                a_im = [a_ref[1, :, s * LANES:(s + 1) * LANES] for s in range(nre)]
                if t0 == 0:
                    state["re"] = [st_ref[0, :, s * LANES:(s + 1) * LANES] for s in range(nre)]
                    state["im"] = [st_ref[1, :, s * LANES:(s + 1) * LANES] for s in range(nre)]
                h_re, h_im = state["re"], state["im"]
                for t in range(t0, t1):
                    rows_t = pl.ds(t, nseq, stride=S5_PITCH)
                    for s in range(nre):
                        hr, hi = h_re[s], h_im[s]
                        h_re[s] = a_re[s] * hr - a_im[s] * hi + x_scr[s, rows_t, :]
                        h_im[s] = a_re[s] * hi + a_im[s] * hr + x_scr[nre + s, rows_t, :]
                        h_scr[s, pl.ds(t, nseq, stride=S5_H_PITCH), :] = h_re[s]
                        h_scr[nre + s, pl.ds(t, nseq, stride=S5_H_PITCH), :] = h_im[s]
                if t1 == steps:
                    for s in range(nre):
                        st_ref[0, :, s * LANES:(s + 1) * LANES] = h_re[s]
                        st_ref[1, :, s * LANES:(s + 1) * LANES] = h_im[s]
            return run
        per = steps // nitems
        return [item(k * per, (k + 1) * per) for k in range(nitems)]

    def project_out_items(h_scr):
        def item(m, half):
            def run():
                acc = jnp.zeros((nb * steps, LANES), F32)
                for n in (2 * m, 2 * m + 1):
                    for part in range(2):
                        slab0 = (part * S5_HALF_ST + n * S5_TILE) // LANES
                        h_tile = jnp.concatenate(
                            [jnp.concatenate([h_scr[slab0 + sl,
                                                    (2 * bi + half) * S5_H_PITCH:(2 * bi + half) * S5_H_PITCH + steps, :]
                                              for sl in range(tile_slabs)], axis=1) for bi in range(nb)], axis=0)
                        acc = acc + jnp.dot(_mx(h_tile), wy_ref[part, n, :, half * LANES:(half + 1) * LANES],
                                            preferred_element_type=F32)
                c = half * S5_HALF_CH + m * LANES
                for bi in range(nb):
                    o_ref[bi, :, c:c + LANES] = acc[bi * steps:(bi + 1) * steps, :]
            return run
        return [item(m, half) for m in range(S5_NT // 2) for half in range(2)]

    @pl.when(i == 0)
    def _():
        st_ref[...] = jnp.zeros_like(st_ref)
        h1_scr[...] = jnp.zeros_like(h1_scr)
        for run in project_in_items(u0_ref, x0_scr):
            run()

    def step(x_cur, h_cur, x_nxt, h_prev):
        ins = project_in_items(u_ref, x_nxt)
        outs = project_out_items(h_prev)
        recs = recur_items(x_cur, h_cur, len(ins))
        for k, rec in enumerate(recs):
            rec()
            ins[k]()
            if k % 2 == 1:
                outs[k // 2]()

    @pl.when(i % 2 == 0)
    def _():
        step(x0_scr, h0_scr, x1_scr, h1_scr)

    @pl.when(i % 2 == 1)
    def _():
        step(x1_scr, h1_scr, x0_scr, h0_scr)


def _s5_weights(lam_re, lam_im, b_re, b_im, c_re, c_im, log_dt):
    lr = jnp.minimum(lam_re.astype(F32), -1e-4)
    li = lam_im.astype(F32)
    dt = jnp.exp(log_dt.astype(F32))[:, None]
    mag = jnp.exp(lr * dt)
    a_re = mag * jnp.cos(li * dt)
    a_im = mag * jnp.sin(li * dt)
    den = lr * lr + li * li
    coef_re = ((a_re - 1.0) * lr + a_im * li) / den
    coef_im = (a_im * lr - (a_re - 1.0) * li) / den
    br = b_re.astype(F32)
    bi = b_im.astype(F32)
    bb_re = coef_re[..., None] * br - coef_im[..., None] * bi
    bb_im = coef_re[..., None] * bi + coef_im[..., None] * br
    gh = S5_GROUPS // 2
    eye = jnp.eye(gh, dtype=F32)

    def b_dense(bb):
        t = bb.reshape(2, gh, S5_STATE, S5_GROUP)
        return jnp.einsum('fgph,gk->fghkp', t, eye).reshape(2, S5_HALF_CH, S5_HALF_ST)

    def c_dense(cc):
        t = cc.reshape(2, gh, S5_GROUP, S5_STATE)
        return jnp.einsum('fghp,gk->fgpkh', t, eye).reshape(2, S5_HALF_ST, S5_HALF_CH)

    def x_tiles(bd):
        tiles = []
        for n in range(S5_NT):
            c = (n // 2) * LANES
            tiles.append(jnp.concatenate([bd[0, c:c + LANES, n * S5_TILE:(n + 1) * S5_TILE],
                                          bd[1, c:c + LANES, n * S5_TILE:(n + 1) * S5_TILE]], axis=0))
        return jnp.stack(tiles)

    def y_tiles(cd):
        tiles = []
        for n in range(S5_NT):
            c = (n // 2) * LANES
            tiles.append(jnp.concatenate([cd[0, n * S5_TILE:(n + 1) * S5_TILE, c:c + LANES],
                                          cd[1, n * S5_TILE:(n + 1) * S5_TILE, c:c + LANES]], axis=1))
        return jnp.stack(tiles)

    wx = jnp.stack([x_tiles(b_dense(bb_re)), x_tiles(b_dense(bb_im))]).astype(MXU_DTYPE)
    wy = jnp.stack([y_tiles(c_dense(c_re.astype(F32))), y_tiles(c_dense(-c_im.astype(F32)))]).astype(MXU_DTYPE)
    return wx, wy, a_re.reshape(2, S5_HALF_ST), a_im.reshape(2, S5_HALF_ST)


def _s5_scan(hp3d, wx, wy, a_re2, a_im2):
    b, s, width = hp3d.shape
    nseq = 2 * b
    assert nseq == SUBLANES
    sel = jnp.arange(nseq) % 2
    a = jnp.stack([a_re2[sel], a_im2[sel]])
    scr = pltpu.VMEM((S5_SLABS, nseq * S5_PITCH, LANES), F32)
    hscr = pltpu.VMEM((S5_SLABS, nseq * S5_H_PITCH, LANES), F32)
    last = s // S5_STEPS - 1
    ucol = width // S5_WIDTH - 1
    return pl.pallas_call(
        _s5_kernel,
        grid=(last + 2,),
        in_specs=[pl.BlockSpec((b, S5_STEPS, S5_WIDTH), lambda i: (0, jnp.minimum(i, last), ucol)),
                  pl.BlockSpec((b, S5_STEPS, S5_WIDTH), lambda i: (0, jnp.minimum(i + 1, last), ucol)),
                  _resident(wx.shape), _resident(wy.shape), _resident(a.shape)],
        out_specs=pl.BlockSpec((b, S5_STEPS, S5_WIDTH), lambda i: (0, jnp.maximum(i - 1, 0), 0)),
        out_shape=jax.ShapeDtypeStruct((b, s, S5_WIDTH), F32),
        scratch_shapes=[pltpu.VMEM((2, nseq, S5_HALF_ST), F32), scr, scr, hscr, hscr],
        compiler_params=_params("arbitrary"),
        name="s5_scan",
    )(hp3d, hp3d, wx, wy, a)


MLP_CHUNK = 1024
TAIL_SPLIT = 2


def _tail_body(mix, x_ref, g1_ref, b1_ref, w1_ref, w2_ref, g2_ref, b2_ref, o_ref):
    rows = x_ref.shape[0]
    for r0 in range(0, rows, rows // TAIL_SPLIT):
        r = slice(r0, r0 + rows // TAIL_SPLIT)
        x1 = _layer_norm(DN_ALPHA * x_ref[r, :] + mix[r], g1_ref[...], b1_ref[...])
        x1m = _mx(x1)
        acc = jnp.zeros(x1.shape, F32)
        for f in range(w1_ref.shape[1] // MLP_CHUNK):
            hid = jnp.dot(x1m, w1_ref[:, f * MLP_CHUNK:(f + 1) * MLP_CHUNK], preferred_element_type=F32)
            hid = jnp.square(jnp.maximum(hid, 0.0))
            acc = acc + jnp.dot(_mx(hid), w2_ref[f * MLP_CHUNK:(f + 1) * MLP_CHUNK, :],
                                preferred_element_type=F32)
        o_ref[r, :] = _layer_norm(DN_ALPHA * x1 + acc, g2_ref[...], b2_ref[...])


def _tail_even_kernel(attn_ref, y_ref, u_ref, dsk_ref, wg_ref, bg_ref, wo_ref, x_ref,
                      g1_ref, b1_ref, w1_ref, w2_ref, g2_ref, b2_ref, o_ref):
    y = y_ref[...] + dsk_ref[...] * u_ref[...]
    g = 0.5 * y * (1.0 + jnp.tanh(math.sqrt(2.0 / math.pi) * (y + 0.044715 * (y * y * y))))
    ssm = g * _sigmoid(jnp.dot(_mx(g), wg_ref[...], preferred_element_type=F32) + bg_ref[...])
    ka = attn_ref.shape[1]
    mix = (jnp.dot(_mx(attn_ref[...]), wo_ref[0:ka, :], preferred_element_type=F32)
           + jnp.dot(_mx(ssm), wo_ref[ka:, :], preferred_element_type=F32))
    _tail_body(mix, x_ref, g1_ref, b1_ref, w1_ref, w2_ref, g2_ref, b2_ref, o_ref)


def _tail_odd_kernel(a_ref, wo_ref, x_ref, g1_ref, b1_ref, w1_ref, w2_ref, g2_ref, b2_ref, o_ref):
    mix = jnp.dot(_mx(a_ref[...]), wo_ref[...], preferred_element_type=F32)
    _tail_body(mix, x_ref, g1_ref, b1_ref, w1_ref, w2_ref, g2_ref, b2_ref, o_ref)


def _row(v):
    return v.reshape(1, -1).astype(F32)


def _tail_common_specs(tm, d, dff):
    return [_resident((d, d)),
            pl.BlockSpec((tm, d), lambda i: (i, 0)),
            _resident((1, d)), _resident((1, d)),
            _resident((d, dff)), _resident((dff, d)),
            _resident((1, d)), _resident((1, d))]


def _tail_even(attn2d, y2d, hp2d, x2d, d_skip, w_glu, b_glu, w_out, g1, b1, w1, w2, g2, b2, tm):
    m, d = x2d.shape
    dff = w1.shape[1]
    ucol = hp2d.shape[1] // S5_WIDTH - 1
    return pl.pallas_call(
        _tail_even_kernel,
        grid=(m // tm,),
        in_specs=[pl.BlockSpec((tm, A_WIDTH), lambda i: (i, 0)),
                  pl.BlockSpec((tm, S5_WIDTH), lambda i: (i, 0)),
                  pl.BlockSpec((tm, S5_WIDTH), lambda i: (i, ucol)),
                  _resident((1, S5_WIDTH)), _resident((S5_WIDTH, S5_WIDTH)), _resident((1, S5_WIDTH))]
        + _tail_common_specs(tm, d, dff),
        out_specs=pl.BlockSpec((tm, d), lambda i: (i, 0)),
        out_shape=jax.ShapeDtypeStruct((m, d), F32),
        compiler_params=_params("parallel"),
        name="tail_even",
    )(attn2d, y2d, hp2d, _row(d_skip), _mx(w_glu), _row(b_glu), _mx(w_out), x2d,
      _row(g1), _row(b1), _mx(w1), _mx(w2), _row(g2), _row(b2))


def _tail_odd(a2d, x2d, w_out, g1, b1, w1, w2, g2, b2, tm):
    m, d = x2d.shape
    dff = w1.shape[1]
    return pl.pallas_call(
        _tail_odd_kernel,
        grid=(m // tm,),
        in_specs=[pl.BlockSpec((tm, a2d.shape[1]), lambda i: (i, 0))] + _tail_common_specs(tm, d, dff),
        out_specs=pl.BlockSpec((tm, d), lambda i: (i, 0)),
        out_shape=jax.ShapeDtypeStruct((m, d), F32),
        compiler_params=_params("parallel"),
        name="tail_odd",
    )(a2d, _mx(w_out), x2d, _row(g1), _row(b1), _mx(w1), _mx(w2), _row(g2), _row(b2))


GDN_TILE = 256
GDN_GATES_TILE = 2048


def _softplus(t):
    return jnp.maximum(t, 0.0) + jnp.log1p(jnp.exp(-jnp.abs(t)))


def _gdn_prep_body(i, hg, q_ref, k_ref, v_ref, qp_ref, kp_ref, vp_ref, cwq_ref, cwk_ref, cwv_ref,
                   ba_ref, g_ref, gt_ref, u_ref, w_ref, qd_ref, kd_ref, intra_ref, edl_ref,
                   qx_ref, kx_ref, vx_ref, ql_ref, kl_ref, vl_ref):
    assert C_CONV == 4
    t, width = q_ref.shape[1], q_ref.shape[2]
    c = C_CHUNK
    dk = C_HEAD_DIM
    hp = width // dk
    groups = t // SUBLANES

    def conv_silu(main_ref, prev_ref, cw_ref, ext_ref, low_ref):
        x = main_ref[0]
        ext_ref[0:SUBLANES, :] = jnp.zeros((SUBLANES, width), F32)
        ext_ref[SUBLANES:2 * SUBLANES, :] = jnp.where(i > 0, prev_ref[0], 0.0)
        ext_ref[2 * SUBLANES:, :] = x
        ext_rows = t + SUBLANES

        def tap(j):
            return jnp.broadcast_to(cw_ref[j:j + 1, :], (SUBLANES, width))[None]

        def grouped(a):
            return a.reshape(a.shape[0] // SUBLANES, SUBLANES, width)

        x_ext = grouped(ext_ref[SUBLANES:, :])
        x_del = grouped(ext_ref[SUBLANES - 1:SUBLANES - 1 + ext_rows, :])
        low_ref[...] = (tap(1) * x_ext + tap(0) * x_del).reshape(ext_rows, width)
        y = (tap(3) * grouped(x) + tap(2) * x_del[1:]
             + grouped(low_ref[SUBLANES - 2:SUBLANES - 2 + t, :]))
        hy = 0.5 * y.reshape(t, width)
        return hy + hy * jnp.tanh(hy)

    def l2n(z, scale):
        return z * (lax.rsqrt(jnp.sum(z * z, axis=-1, keepdims=True) + L2_EPS) * scale)

    q_all = conv_silu(q_ref, qp_ref, cwq_ref, qx_ref, ql_ref)
    k_all = conv_silu(k_ref, kp_ref, cwk_ref, kx_ref, kl_ref)
    v_all = conv_silu(v_ref, vp_ref, cwv_ref, vx_ref, vl_ref)
    heads = [slice(hh * dk, (hh + 1) * dk) for hh in range(hp)]
    qs = [l2n(q_all[:, cs], dk ** -0.5) for cs in heads]
    ks = [l2n(k_all[:, cs], 1.0) for cs in heads]
    vs = [v_all[:, cs] for cs in heads]

    ba = ba_ref[0]
    lane = lax.broadcasted_iota(jnp.int32, (t, LANES), 1)
    betas, g_cols, g_rows = [], [], []
    for hh in range(hp):
        h = hg * hp + hh
        betas.append(_sigmoid(jnp.sum(jnp.where(lane == h, ba, 0.0), axis=1, keepdims=True)))
        g_cols.append(jnp.sum(jnp.where(lane == C_HEADS + h, g_ref[0], 0.0), axis=1, keepdims=True))
        g_rows.append(gt_ref[0, pl.ds(C_HEADS + h, 1), :])

    units = [(hh, slice(ch * c, (ch + 1) * c)) for hh in range(hp) for ch in range(t // c)]
    _gdn_prep_units(units, qs, ks, vs, betas, g_cols, g_rows, heads, u_ref, w_ref, qd_ref, kd_ref, intra_ref, edl_ref)


def _gdn_prep_units(units, qs, ks, vs, betas, g_cols, g_rows, heads, u_ref, w_ref, qd_ref, kd_ref, intra_ref, edl_ref):
    c = C_CHUNK
    dk = C_HEAD_DIM
    assert 2 * c == LANES and len(units) % 2 == 0
    ii = lax.broadcasted_iota(jnp.int32, (c, c), 0)
    jj = lax.broadcasted_iota(jnp.int32, (c, c), 1)
    row = lax.broadcasted_iota(jnp.int32, (c, 2 * c), 0)
    lane = lax.broadcasted_iota(jnp.int32, (c, 2 * c), 1)
    left = lane < c
    col = jnp.where(left, lane, lane - c)
    low = row >= col
    pairs = [(2 * n, 2 * n + 1) for n in range(len(units) // 2)]

    def side_by_side(a, b):
        za, zb = jnp.zeros_like(a), jnp.zeros_like(b)
        return jnp.concatenate([jnp.concatenate([a, zb], axis=1), jnp.concatenate([za, b], axis=1)], axis=0)

    def block_diag(p):
        return jnp.concatenate([jnp.where(left, p, 0.0), jnp.where(left, 0.0, p)], axis=0)

    gc_cols = [jnp.sum(jnp.where(ii >= jj, g_rows[hh][:, r], 0.0), axis=1, keepdims=True) for hh, r in units]
    gc_rows = [jnp.sum(jnp.where(ii <= jj, g_cols[hh][r], 0.0), axis=0, keepdims=True) for hh, r in units]
    decays = [jnp.where(low, jnp.exp(jnp.where(low, jnp.where(left, gc_cols[a], gc_cols[b])
                                               - jnp.concatenate([gc_rows[a], gc_rows[b]], axis=1), 0.0)), 0.0)
              for a, b in pairs]
    beta_bs = [jnp.broadcast_to(betas[hh][r], (c, dk)) for hh, r in units]
    kbs = [ks[hh][r] * bb for bb, (hh, r) in zip(beta_bs, units)]
    kdiag = [_mx(side_by_side(ks[units[a][0]][units[a][1]], ks[units[b][0]][units[b][1]])) for a, b in pairs]
    kks = [_dot_nt(jnp.concatenate([kbs[a], kbs[b]], axis=1), kd) for (a, b), kd in zip(pairs, kdiag)]
    qks = [_dot_nt(jnp.concatenate([qs[units[a][0]][units[a][1]], qs[units[b][0]][units[b][1]]], axis=1), kd)
           for (a, b), kd in zip(pairs, kdiag)]
    pws = [jnp.where(row > col, -(kk * dec), 0.0) for kk, dec in zip(kks, decays)]
    tms = pws
    pds = [_mx(block_diag(pw)) for pw in pws]
    for _ in range(int(math.log2(c)) - 1):
        pws = [jnp.dot(_mx(pw), pd, preferred_element_type=F32) for pw, pd in zip(pws, pds)]
        pds = [_mx(block_diag(pw)) for pw in pws]
        tms = [tm + pw + jnp.dot(_mx(tm), pd, preferred_element_type=F32) for tm, pw, pd in zip(tms, pws, pds)]
    egcs = [jnp.broadcast_to(jnp.exp(a), (c, dk)) for a in gc_cols]
    vbs = [vs[hh][r] * bb for bb, (hh, r) in zip(beta_bs, units)]
    kbes = [kb * e for kb, e in zip(kbs, egcs)]
    tbs = [_mx(tm) for tm in tms]
    tvs = [jnp.dot(tb, _mx(side_by_side(vbs[a], vbs[b])), preferred_element_type=F32) for tb, (a, b) in zip(tbs, pairs)]
    tks = [jnp.dot(tb, _mx(side_by_side(kbes[a], kbes[b])), preferred_element_type=F32) for tb, (a, b) in zip(tbs, pairs)]
    for n, (hh, r) in enumerate(units):
        gl = gc_cols[n][c - 1:c, :]
        cs = heads[hh]
        ch = r.start // c
        half = slice((n % 2) * dk, (n % 2 + 1) * dk)
        u_ref[0, r, cs] = vbs[n] + tvs[n // 2][:, half]
        w_ref[0, r, cs] = _mx(kbes[n] + tks[n // 2][:, half])
        qd_ref[0, r, cs] = _mx(qs[hh][r] * egcs[n])
        kd_ref[0, r, cs] = _mx(ks[hh][r] * jnp.exp(gl - gc_cols[n]))
        intra_ref[0, hh, r, :] = _mx((qks[n // 2] * decays[n // 2])[:, (n % 2) * c:(n % 2 + 1) * c])
        edl_ref[0, hh, ch:ch + 1, :] = jnp.broadcast_to(jnp.exp(gl), (1, LANES))


def _gdn_gates_kernel(ba_ref, alog_ref, dtb_ref, g_ref, gt_ref):
    g_all = -jnp.exp(alog_ref[...]) * _softplus(ba_ref[0] + dtb_ref[...])
    g_ref[0] = g_all
    gt_ref[0] = g_all.T


def _gdn_gates(hp3d, alog_row, dtb_row):
    b, s, _ = hp3d.shape
    t = min(GDN_GATES_TILE, s)
    ba_col = 4 * C_WIDTH // LANES
    return pl.pallas_call(
        _gdn_gates_kernel,
        grid=(b, s // t),
        in_specs=[pl.BlockSpec((1, t, LANES), lambda bi, i: (bi, i, ba_col)),
                  pl.BlockSpec((1, LANES), lambda bi, i: (0, 0)),
                  pl.BlockSpec((1, LANES), lambda bi, i: (0, 0))],
        out_specs=[pl.BlockSpec((1, t, LANES), lambda bi, i: (bi, i, 0)),
                   pl.BlockSpec((1, LANES, t), lambda bi, i: (bi, 0, i))],
        out_shape=[jax.ShapeDtypeStruct((b, s, LANES), F32), jax.ShapeDtypeStruct((b, LANES, s), F32)],
        compiler_params=_params("parallel", "parallel"),
        name="gdn_gates",
    )(hp3d, alog_row, dtb_row)


def _gdn_scan_body(chunk0, u_ref, w_ref, qd_ref, kd_ref, intra_ref, edl_ref, gate_ref, nw_ref, o_ref, st_ref):
    c = C_CHUNK
    dk = C_HEAD_DIM
    t = u_ref.shape[1]
    nw = nw_ref[...]
    heads = [slice(h * dk, (h + 1) * dk) for h in range(C_HEADS)]
    for ch in range(t // c):
        r = slice(ch * c, (ch + 1) * c)
        sts = [st_ref[h] for h in range(C_HEADS)]
        wss = [_dot(jnp.concatenate([w_ref[0, r, cs], qd_ref[0, r, cs]], axis=0), st)
               for cs, st in zip(heads, sts)]
        v_news = [u_ref[0, r, cs] - ws[:c] for cs, ws in zip(heads, wss)]
        kvs = [_dot_tn(kd_ref[0, r, cs], v_new) for cs, v_new in zip(heads, v_news)]
        os_ = [ws[c:] + _dot(intra_ref[0, h, r, :], v_new) for h, (ws, v_new) in enumerate(zip(wss, v_news))]
        for h, cs in enumerate(heads):
            st_ref[h] = sts[h] * edl_ref[0, h, pl.ds(chunk0 + ch, 1), 0:1] + kvs[h]
            o = os_[h]
            o = o * lax.rsqrt(jnp.mean(o * o, axis=-1, keepdims=True) + RMS_EPS) * nw
            hg = 0.5 * gate_ref[0, r, cs]
            o_ref[0, r, cs] = o * (hg + hg * jnp.tanh(hg))


def _gdn_kernel(q_ref, k_ref, v_ref, qp_ref, kp_ref, vp_ref, cwq_ref, cwk_ref, cwv_ref, ba_ref, g_ref, gt_ref,
                gate_ref, nw_ref, o_ref, st_ref, qx_ref, kx_ref, vx_ref, ql_ref, kl_ref, vl_ref, *bufs):
    j = pl.program_id(1)
    last = pl.num_programs(1) - 2
    set_a, set_b = bufs[:len(bufs) // 2], bufs[len(bufs) // 2:]

    @pl.when(j == 0)
    def _():
        st_ref[...] = jnp.zeros_like(st_ref)
        for ref in bufs:
            ref[...] = jnp.zeros_like(ref)

    def step(cur, nxt):
        _gdn_prep_body(jnp.minimum(j, last), 0, q_ref, k_ref, v_ref, qp_ref, kp_ref, vp_ref,
                       cwq_ref, cwk_ref, cwv_ref, ba_ref, g_ref, gt_ref, *nxt,
                       qx_ref, kx_ref, vx_ref, ql_ref, kl_ref, vl_ref)
        _gdn_scan_body(0, *cur, gate_ref, nw_ref, o_ref, st_ref)

    @pl.when(j % 2 == 0)
    def _():
        step(set_a, set_b)

    @pl.when(j % 2 == 1)
    def _():
        step(set_b, set_a)


def _gdn(hp3d, conv_w, alog_row, dtb_row, norm_w):
    b, s, _ = hp3d.shape
    ba_col = 4 * C_WIDTH // LANES
    t = GDN_TILE
    nt = s // t
    per = t // SUBLANES
    wd = C_WIDTH
    nc = t // C_CHUNK
    g, gt = _gdn_gates(hp3d, alog_row, dtb_row)

    def prep_tile(j):
        return jnp.minimum(j, nt - 1)

    def scan_tile(j):
        return jnp.maximum(j - 1, 0)

    def main(off):
        return pl.BlockSpec((1, t, wd), lambda bi, j: (bi, prep_tile(j), off))

    def prev(off):
        return pl.BlockSpec((1, SUBLANES, wd), lambda bi, j: (bi, jnp.maximum(prep_tile(j) * per - 1, 0), off))

    def cw(off):
        return pl.BlockSpec((C_CONV, wd), lambda bi, j: (0, off))

    bufs = [pltpu.VMEM((1, t, wd), F32)] + [pltpu.VMEM((1, t, wd), MXU_DTYPE)] * 3 + [
        pltpu.VMEM((1, C_HEADS, t, C_CHUNK), MXU_DTYPE), pltpu.VMEM((1, C_HEADS, max(nc, SUBLANES), LANES), F32)]
    return pl.pallas_call(
        _gdn_kernel,
        grid=(b, nt + 1),
        in_specs=[main(0), main(1), main(2), prev(0), prev(1), prev(2), cw(0), cw(1), cw(2),
                  pl.BlockSpec((1, t, LANES), lambda bi, j: (bi, prep_tile(j), ba_col)),
                  pl.BlockSpec((1, t, LANES), lambda bi, j: (bi, prep_tile(j), 0)),
                  pl.BlockSpec((1, LANES, t), lambda bi, j: (bi, 0, prep_tile(j))),
                  pl.BlockSpec((1, t, wd), lambda bi, j: (bi, scan_tile(j), 3)),
                  pl.BlockSpec((1, C_HEAD_DIM), lambda bi, j: (0, 0))],
        out_specs=pl.BlockSpec((1, t, wd), lambda bi, j: (bi, scan_tile(j), 0)),
        out_shape=jax.ShapeDtypeStruct((b, s, wd), F32),
        scratch_shapes=[pltpu.VMEM((C_HEADS, C_HEAD_DIM, C_HEAD_DIM), F32)]
        + [pltpu.VMEM((t + 2 * SUBLANES, wd), F32)] * 3 + [pltpu.VMEM((t + SUBLANES, wd), F32)] * 3
        + bufs + bufs,
        compiler_params=_params("parallel", "arbitrary"),
        name="gdn",
    )(hp3d, hp3d, hp3d, hp3d, hp3d, hp3d, conv_w, conv_w, conv_w, hp3d, g, gt, hp3d, _row(norm_w))


ROW_TILE = 512


def kernel(x, rel_table, ev_w_in, ev_w_out, s5_lambda_re, s5_lambda_im, s5_b_re, s5_b_im, s5_c_re, s5_c_im, s5_d, s5_log_dt, s5_w_glu, s5_b_glu, od_w_in, od_conv_w, od_a_log, od_dt_bias, od_norm_w, od_w_out, ln_mix_g, ln_mix_b, mlp_w1, mlp_w2, ln_ffn_g, ln_ffn_b):
    bsz, s, d = x.shape
    m = bsz * s
    x2d = x.reshape(m, d)
    table_t = rel_table.T.astype(F32)
    bias = _bias_tiles(table_t)
    for layer in range(DEPTH):
        i = layer // 2
        if layer % 2 == 0:
            hp = _proj(x2d, _mx(ev_w_in[i]), ROW_TILE)
            hp3d = hp.reshape(bsz, s, hp.shape[1])
            attn = _moba(hp3d, bias)
            wx, wy, a_re2, a_im2 = _s5_weights(s5_lambda_re[i], s5_lambda_im[i], s5_b_re[i], s5_b_im[i],
                                               s5_c_re[i], s5_c_im[i], s5_log_dt[i])
            y = _s5_scan(hp3d, wx, wy, a_re2, a_im2)
            x2d = _tail_even(attn.reshape(m, A_WIDTH), y.reshape(m, S5_WIDTH), hp, x2d,
                             s5_d[i], s5_w_glu[i], s5_b_glu[i], ev_w_out[i],
                             ln_mix_g[layer], ln_mix_b[layer], mlp_w1[layer], mlp_w2[layer],
                             ln_ffn_g[layer], ln_ffn_b[layer], ROW_TILE)
        else:
            w_in = od_w_in[i]
            w_all = jnp.pad(w_in, ((0, 0), (0, LANES - 2 * C_HEADS)))
            hp = _proj(x2d, _mx(w_all), ROW_TILE)
            hp3d = hp.reshape(bsz, s, 4 * C_WIDTH + LANES)
            pad8 = jnp.zeros((C_HEADS,), F32)
            padr = jnp.zeros((LANES - 2 * C_HEADS,), F32)
            alog_row = jnp.concatenate([pad8, od_a_log[i].astype(F32), padr]).reshape(1, LANES)
            dtb_row = jnp.concatenate([pad8, od_dt_bias[i].astype(F32), padr]).reshape(1, LANES)
            o = _gdn(hp3d, od_conv_w[i].astype(F32), alog_row, dtb_row, od_norm_w[i])
            x2d = _tail_odd(o.reshape(m, C_WIDTH), x2d, od_w_out[i],
                            ln_mix_g[layer], ln_mix_b[layer], mlp_w1[layer], mlp_w2[layer],
                            ln_ffn_g[layer], ln_ffn_b[layer], ROW_TILE)
    return x2d.reshape(bsz, s, d)
```

```python
import functools
import math

import jax
import jax.numpy as jnp
import numpy as np
from jax import lax
from jax.experimental import pallas as pl
from jax.experimental.pallas import tpu as pltpu

F32 = jnp.float32
MXU_DTYPE = jnp.bfloat16

LANES = 128
SUBLANES = 8
VMEM_LIMIT_BYTES = 56 * 1024 * 1024

A_HEADS = 8
A_HEAD_DIM = 64
A_WIDTH = A_HEADS * A_HEAD_DIM
MOBA_BLOCK = 256
MOBA_TOPK = 3
REL_BUCKETS = 32
REL_MAX_DIST = 128
S5_GROUP = 16
S5_GROUPS = 32
S5_STATE = 64
S5_WIDTH = S5_GROUP * S5_GROUPS
C_HEADS = 8
C_HEAD_DIM = 128
C_WIDTH = C_HEADS * C_HEAD_DIM
C_CONV = 4
C_CHUNK = 64
DEPTH = 2
DN_ALPHA = (2 * DEPTH) ** 0.25
LN_EPS = 1e-5
RMS_EPS = 1e-6
L2_EPS = 1e-6
NEG_INF = -1e30
LOG2E = math.log2(math.e)


def _mx(t):
    return t.astype(MXU_DTYPE)


def _dot(a, b):
    return jnp.dot(_mx(a), _mx(b), preferred_element_type=F32)


def _dot_tn(a, b):
    return lax.dot_general(_mx(a), _mx(b), (((0,), (0,)), ((), ())), preferred_element_type=F32)


def _dot_nt(a, b):
    return lax.dot_general(_mx(a), _mx(b), (((1,), (1,)), ((), ())), preferred_element_type=F32)


def _sigmoid(t):
    return 1.0 / (1.0 + jnp.exp(-t))


def _layer_norm(r, g, b):
    mu = jnp.mean(r, axis=-1, keepdims=True)
    d = r - mu
    var = jnp.mean(d * d, axis=-1, keepdims=True)
    return d * lax.rsqrt(var + LN_EPS) * g + b


def _params(*sem):
    return pltpu.CompilerParams(dimension_semantics=sem, vmem_limit_bytes=VMEM_LIMIT_BYTES)


def _resident(shape):
    return pl.BlockSpec(shape, lambda *_: (0,) * len(shape), pipeline_mode=pl.Buffered(1))


PROJ_CHUNK = 1024


def _proj_kernel(x_ref, w_ref, o_ref):
    xm = _mx(x_ref[...])
    n = w_ref.shape[1]
    for c0 in range(0, n, PROJ_CHUNK):
        c1 = min(c0 + PROJ_CHUNK, n)
        o_ref[:, c0:c1] = jnp.dot(xm, w_ref[:, c0:c1], preferred_element_type=F32)


def _proj_split_kernel(x_ref, w_ref, lo_ref, hi_ref):
    xm = _mx(x_ref[...])
    n_lo = lo_ref.shape[1]
    for c0 in range(0, n_lo, PROJ_CHUNK):
        c1 = min(c0 + PROJ_CHUNK, n_lo)
        lo_ref[:, c0:c1] = _mx(jnp.dot(xm, w_ref[:, c0:c1], preferred_element_type=F32))
    hi_ref[...] = jnp.dot(xm, w_ref[:, n_lo:], preferred_element_type=F32)


def _proj_split(x2d, w, n_lo, tm):
    m, k = x2d.shape
    n = w.shape[1]
    return pl.pallas_call(
        _proj_split_kernel,
        grid=(m // tm,),
        in_specs=[pl.BlockSpec((tm, k), lambda i: (i, 0)), _resident((k, n))],
        out_specs=[pl.BlockSpec((tm, n_lo), lambda i: (i, 0)), pl.BlockSpec((tm, n - n_lo), lambda i: (i, 0))],
        out_shape=[jax.ShapeDtypeStruct((m, n_lo), MXU_DTYPE), jax.ShapeDtypeStruct((m, n - n_lo), F32)],
        compiler_params=_params("parallel"),
        name="in_proj_split",
    )(x2d, w)


def _proj(x2d, w, tm):
    m, k = x2d.shape
    n = w.shape[1]
    return pl.pallas_call(
        _proj_kernel,
        grid=(m // tm,),
        in_specs=[pl.BlockSpec((tm, k), lambda i: (i, 0)), _resident((k, n))],
        out_specs=pl.BlockSpec((tm, n), lambda i: (i, 0)),
        out_shape=jax.ShapeDtypeStruct((m, n), F32),
        compiler_params=_params("parallel"),
        name="in_proj",
    )(x2d, w)


def _t5_bucket_np(dist):
    max_exact = REL_BUCKETS // 2
    n = np.maximum(dist, 0)
    nf = np.maximum(n, 1).astype(np.float32)
    large = max_exact + (np.log(nf / np.float32(max_exact)) / np.float32(math.log(REL_MAX_DIST / max_exact))
                         * np.float32(REL_BUCKETS - max_exact)).astype(np.int32)
    large = np.minimum(large, REL_BUCKETS - 1)
    return np.where(n < max_exact, n, large).astype(np.int32)


def _bias_kernel(tab_ref, bkt_ref, o_ref):
    h = pl.program_id(0)
    blk = bkt_ref.shape[-1]
    key = lax.broadcasted_iota(jnp.int32, (blk, blk), 0)
    qry = lax.broadcasted_iota(jnp.int32, (blk, blk), 1)
    far = tab_ref[h, REL_BUCKETS - 1]
    for t in range(2):
        bk = bkt_ref[t]
        acc = jnp.zeros((blk, blk), F32)
        for i in range(REL_BUCKETS):
            acc = jnp.where(bk == i, (tab_ref[h, i] - far) * LOG2E, acc)
        if t == 0:
            acc = jnp.where(qry >= key, acc, NEG_INF)
        o_ref[0, t] = acc


def _bias_tiles(table_t):
    heads = table_t.shape[0]
    offs = np.arange(MOBA_BLOCK)
    dist_own = offs[None, :] - offs[:, None]
    buckets = np.stack([_t5_bucket_np(dist_own), _t5_bucket_np(dist_own + MOBA_BLOCK)])
    return pl.pallas_call(
        _bias_kernel,
        grid=(heads,),
        in_specs=[pl.BlockSpec(memory_space=pltpu.SMEM),
                  pl.BlockSpec((2, MOBA_BLOCK, MOBA_BLOCK), lambda h: (0, 0, 0))],
        out_specs=pl.BlockSpec((1, 2, MOBA_BLOCK, MOBA_BLOCK), lambda h: (h, 0, 0, 0)),
        out_shape=jax.ShapeDtypeStruct((heads, 2, MOBA_BLOCK, MOBA_BLOCK), F32),
        compiler_params=_params("parallel"),
        name="t5_bias",
    )(table_t, jnp.asarray(buckets))


def _moba_kernel(q_ref, k_ref, v_ref, bias_ref, o_ref, kmean_ref, kb_ref, vt_ref,
                 st0_ref, st1_ref, wq_ref, acc_ref):
    st_refs = (st0_ref, st1_ref)
    qb = pl.program_id(2)
    blk = MOBA_BLOCK
    d = A_HEAD_DIM
    nh = LANES // d
    nb = k_ref.shape[1] // blk
    ng = q_ref.shape[2] // LANES
    groups = range(ng)

    @pl.when(qb == 0)
    def _():
        lane = lax.broadcasted_iota(jnp.int32, (blk, LANES - d), 1)
        for g in groups:
            gl = slice(g * LANES, (g + 1) * LANES)
            for j in range(nb + 1):
                onehot = jnp.where(lane == j, 1.0, 0.0).astype(MXU_DTYPE)
                for hh in range(nh):
                    kb_ref[g, hh, j, :, d:LANES] = onehot
            for j in range(nb):
                kj = k_ref[0, j * blk:(j + 1) * blk, gl].astype(F32)
                for hh in range(nh):
                    kb_ref[g, hh, j, :, 0:d] = _mx(kj[:, hh * d:(hh + 1) * d])
                kmean_ref[g, j:j + 1, :] = jnp.mean(kj, axis=0, keepdims=True)
                vt_ref[g, j] = _mx(v_ref[0, j * blk:(j + 1) * blk, gl].astype(F32).T)
            for hh in range(nh):
                kb_ref[g, hh, nb, :, 0:d] = jnp.zeros((blk, d), MXU_DTYPE)
            vt_ref[g, nb] = jnp.zeros((LANES, blk), MXU_DTYPE)

    blk_id = lax.broadcasted_iota(jnp.int32, (nb, blk), 0)
    row = lax.broadcasted_iota(jnp.int32, (LANES - d - nb, blk), 0)
    for g in groups:
        qt = (q_ref[0, :, g * LANES:(g + 1) * LANES].astype(F32) * (d ** -0.5)).T
        for hh in range(nh):
            gate = lax.dot_general(kmean_ref[g, :, hh * d:(hh + 1) * d], qt[hh * d:(hh + 1) * d],
                                   (((1,), (0,)), ((), ())),
                                   precision=lax.Precision.HIGHEST, preferred_element_type=F32)
            gt = jnp.where(blk_id < qb, gate, -jnp.inf)
            sel = jnp.zeros((nb, blk), F32)
            for _ in range(MOBA_TOPK):
                m = jnp.max(gt, axis=0, keepdims=True)
                is_m = jnp.logical_and(gt == m, gt > -jnp.inf)
                idx = jnp.min(jnp.where(is_m, blk_id, nb), axis=0, keepdims=True)
                pick = blk_id == idx
                sel = jnp.where(pick, 1.0, sel)
                gt = jnp.where(pick, -jnp.inf, gt)
            keep = jnp.logical_or(sel > 0.0, blk_id == qb)
            wq_ref[g, hh] = _mx(jnp.concatenate([qt[hh * d:(hh + 1) * d] * LOG2E, jnp.where(keep, 0.0, NEG_INF),
                                                 jnp.where(row == 0, NEG_INF, 0.0)], axis=0))

    def score_stage(slot, ja, jb, with_bias):
        cmaxes = []
        for g in groups:
            for hh in range(nh):
                keys = jnp.concatenate([kb_ref[g, hh, ja], kb_ref[g, hh, jb]], axis=0)
                st = jnp.dot(keys, wq_ref[g, hh], preferred_element_type=F32)
                if with_bias:
                    st = st + jnp.concatenate([bias_ref[g * nh + hh, w] for w in range(2)], axis=0)
                st_refs[slot][g, 0, :, hh * blk:(hh + 1) * blk] = st[:blk]
                st_refs[slot][g, 1, :, hh * blk:(hh + 1) * blk] = st[blk:]
                cmaxes.append(jnp.max(st, axis=0, keepdims=True))
        return tuple(cmaxes)

    def attend_stage(slot, cmaxes, stats, ja, jb):
        new = []
        for g in groups:
            for hh in range(nh):
                cs = slice(hh * blk, (hh + 1) * blk)
                m_new = cmaxes[g * nh + hh]
                if stats is not None:
                    m, l = stats[g * nh + hh]
                    m_new = jnp.maximum(m, m_new)
                ps = [jnp.exp2(st_refs[slot][g, w, :, cs] - m_new) for w in range(2)]
                l_new = jnp.sum(ps[0], axis=0, keepdims=True) + jnp.sum(ps[1], axis=0, keepdims=True)
                vt = jnp.concatenate([vt_ref[g, ja, hh * d:(hh + 1) * d, :],
                                      vt_ref[g, jb, hh * d:(hh + 1) * d, :]], axis=1)
                pv = jnp.dot(vt, jnp.concatenate([_mx(ps[0]), _mx(ps[1])], axis=0),
                             preferred_element_type=F32)
                if stats is None:
                    acc_ref[g, hh] = pv
                    new.append((m_new, l_new))
                else:
                    alpha = jnp.exp2(m - m_new)
                    acc_ref[g, hh] = acc_ref[g, hh] * alpha + pv
                    new.append((m_new, alpha * l + l_new))
        return tuple(new)

    nfar = qb - 1
    npairs = (nfar + 1) // 2

    def far_blocks(i):
        return jnp.where(2 * i < nfar, 2 * i, nb), jnp.where(2 * i + 1 < nfar, 2 * i + 1, nb)

    j_prev = jnp.where(qb > 0, qb - 1, nb)
    cmax0 = score_stage(0, qb, j_prev, True)
    cmax1 = score_stage(1, *far_blocks(0), False)
    init = (attend_stage(0, cmax0, None, qb, j_prev), cmax1)

    def trip(slot, k, carry, prefetch=True):
        stats, cmax = carry
        cmax_next = score_stage(1 - slot, *far_blocks(k), False) if prefetch else cmax
        return attend_stage(slot, cmax, stats, *far_blocks(k - 1)), cmax_next

    def double_trip(i, carry):
        k = 2 * i + 1
        return trip(0, k + 1, trip(1, k, carry))

    carry = lax.fori_loop(0, npairs // 2, double_trip, init)
    stats, _ = lax.cond(npairs % 2 == 1, lambda cr: trip(1, npairs, cr, prefetch=False), lambda cr: cr, carry)
    for g in groups:
        out_t = jnp.concatenate([acc_ref[g, hh] / stats[g * nh + hh][1] for hh in range(nh)], axis=0)
        o_ref[0, :, g * LANES:(g + 1) * LANES] = _mx(out_t.T)


MOBA_GROUPS = 4


def _moba(qkv3d, bias):
    b, s, _ = qkv3d.shape
    ng = MOBA_GROUPS
    wd = ng * LANES
    nsteps = A_WIDTH // wd
    hpp = LANES // A_HEAD_DIM
    nb = s // MOBA_BLOCK
    return pl.pallas_call(
        _moba_kernel,
        grid=(b, nsteps, nb),
        in_specs=[pl.BlockSpec((1, MOBA_BLOCK, wd), lambda bi, p, i: (bi, i, p)),
                  pl.BlockSpec((1, s, wd), lambda bi, p, i: (bi, 0, nsteps + p)),
                  pl.BlockSpec((1, s, wd), lambda bi, p, i: (bi, 0, 2 * nsteps + p)),
                  pl.BlockSpec((ng * hpp, 2, MOBA_BLOCK, MOBA_BLOCK), lambda bi, p, i: (p, 0, 0, 0))],
        out_specs=pl.BlockSpec((1, MOBA_BLOCK, wd), lambda bi, p, i: (bi, i, p)),
        out_shape=jax.ShapeDtypeStruct((b, s, A_WIDTH), MXU_DTYPE),
        scratch_shapes=[pltpu.VMEM((ng, nb, LANES), F32),
                        pltpu.VMEM((ng, hpp, nb + 1, MOBA_BLOCK, LANES), MXU_DTYPE),
                        pltpu.VMEM((ng, nb + 1, LANES, MOBA_BLOCK), MXU_DTYPE),
                        pltpu.VMEM((ng, 2, MOBA_BLOCK, hpp * MOBA_BLOCK), F32),
                        pltpu.VMEM((ng, 2, MOBA_BLOCK, hpp * MOBA_BLOCK), F32),
                        pltpu.VMEM((ng, hpp, LANES, MOBA_BLOCK), MXU_DTYPE),
                        pltpu.VMEM((ng, hpp, A_HEAD_DIM, MOBA_BLOCK), F32)],
        compiler_params=_params("parallel", "parallel", "arbitrary"),
        name="moba",
    )(qkv3d, qkv3d, qkv3d, bias)


S5_HALF_CH = S5_WIDTH // 2
S5_HALF_ST = (S5_GROUPS // 2) * S5_STATE
S5_TILE = 256
S5_NT = S5_HALF_ST // S5_TILE
S5_STEPS = 128
S5_PITCH = S5_STEPS + SUBLANES
S5_H_PITCH = S5_STEPS + SUBLANES // 2
S5_SLABS = 2 * S5_HALF_ST // LANES


def _s5_kernel(u0_ref, u_ref, wx_ref, wy_ref, a_ref, o_ref, st_ref, x0_scr, x1_scr, h0_scr, h1_scr):
    nb, steps, _ = u_ref.shape
    nseq = 2 * nb
    tile_slabs = S5_TILE // LANES
    nre = S5_SLABS // 2
    i = pl.program_id(0)

    def project_in_items(src_ref, x_scr):
        def item(m, part, n):
            def run():
                slab0 = (part * S5_HALF_ST + n * S5_TILE) // LANES
                for half in range(2):
                    lhs = jnp.concatenate(
                        [_mx(src_ref[bi, :, half * S5_HALF_CH + m * LANES:half * S5_HALF_CH + (m + 1) * LANES])
                         for bi in range(nb)], axis=0)
                    res = jnp.dot(lhs, wx_ref[part, n, half * LANES:(half + 1) * LANES, :],
                                  preferred_element_type=F32)
                    for bi in range(nb):
                        seq = 2 * bi + half
                        for sl in range(tile_slabs):
                            x_scr[slab0 + sl, seq * S5_PITCH:seq * S5_PITCH + steps, :] = (
                                res[bi * steps:(bi + 1) * steps, sl * LANES:(sl + 1) * LANES])
            return run
        return [item(m, part, n) for m in range(S5_NT // 2) for part in range(2) for n in (2 * m, 2 * m + 1)]

    def recur_items(x_scr, h_scr, nitems):
        state = {}

        def item(t0, t1):
            def run():
                a_re = [a_ref[0, :, s * LANES:(s + 1) * LANES] for s in range(nre)]
                a_im = [a_ref[1, :, s * LANES:(s + 1) * LANES] for s in range(nre)]
                if t0 == 0:
                    state["re"] = [st_ref[0, :, s * LANES:(s + 1) * LANES] for s in range(nre)]
                    state["im"] = [st_ref[1, :, s * LANES:(s + 1) * LANES] for s in range(nre)]
                h_re, h_im = state["re"], state["im"]
                for t in range(t0, t1):
                    rows_t = pl.ds(t, nseq, stride=S5_PITCH)
                    for s in range(nre):
                        hr, hi = h_re[s], h_im[s]
                        h_re[s] = a_re[s] * hr - a_im[s] * hi + x_scr[s, rows_t, :]
                        h_im[s] = a_re[s] * hi + a_im[s] * hr + x_scr[nre + s, rows_t, :]
                        h_scr[s, pl.ds(t, nseq, stride=S5_H_PITCH), :] = h_re[s]
                        h_scr[nre + s, pl.ds(t, nseq, stride=S5_H_PITCH), :] = h_im[s]
                if t1 == steps:
                    for s in range(nre):
                        st_ref[0, :, s * LANES:(s + 1) * LANES] = h_re[s]
                        st_ref[1, :, s * LANES:(s + 1) * LANES] = h_im[s]
            return run
        per = steps // nitems
        return [item(k * per, (k + 1) * per) for k in range(nitems)]

    def project_out_items(h_scr):
        def item(m, half):
            def run():
                acc = jnp.zeros((nb * steps, LANES), F32)
                for n in (2 * m, 2 * m + 1):
                    for part in range(2):
                        slab0 = (part * S5_HALF_ST + n * S5_TILE) // LANES
                        h_tile = jnp.concatenate(
                            [jnp.concatenate([h_scr[slab0 + sl,
                                                    (2 * bi + half) * S5_H_PITCH:(2 * bi + half) * S5_H_PITCH + steps, :]
                                              for sl in range(tile_slabs)], axis=1) for bi in range(nb)], axis=0)
                        acc = acc + jnp.dot(_mx(h_tile), wy_ref[part, n, :, half * LANES:(half + 1) * LANES],
                                            preferred_element_type=F32)
                c = half * S5_HALF_CH + m * LANES
                for bi in range(nb):
                    o_ref[bi, :, c:c + LANES] = acc[bi * steps:(bi + 1) * steps, :]
            return run
        return [item(m, half) for m in range(S5_NT // 2) for half in range(2)]

    @pl.when(i == 0)
    def _():
        st_ref[...] = jnp.zeros_like(st_ref)
        h1_scr[...] = jnp.zeros_like(h1_scr)
        for run in project_in_items(u0_ref, x0_scr):
            run()

    def step(x_cur, h_cur, x_nxt, h_prev):
        ins = project_in_items(u_ref, x_nxt)
        outs = project_out_items(h_prev)
        recs = recur_items(x_cur, h_cur, len(ins))
        for k, rec in enumerate(recs):
            rec()
            ins[k]()
            if k % 2 == 1:
                outs[k // 2]()

    @pl.when(i % 2 == 0)
    def _():
        step(x0_scr, h0_scr, x1_scr, h1_scr)

    @pl.when(i % 2 == 1)
    def _():
        step(x1_scr, h1_scr, x0_scr, h0_scr)


def _s5_weights(lam_re, lam_im, b_re, b_im, c_re, c_im, log_dt):
    lr = jnp.minimum(lam_re.astype(F32), -1e-4)
    li = lam_im.astype(F32)
    dt = jnp.exp(log_dt.astype(F32))[:, None]
    mag = jnp.exp(lr * dt)
    a_re = mag * jnp.cos(li * dt)
    a_im = mag * jnp.sin(li * dt)
    den = lr * lr + li * li
    coef_re = ((a_re - 1.0) * lr + a_im * li) / den
    coef_im = (a_im * lr - (a_re - 1.0) * li) / den
    br = b_re.astype(F32)
    bi = b_im.astype(F32)
    bb_re = coef_re[..., None] * br - coef_im[..., None] * bi
    bb_im = coef_re[..., None] * bi + coef_im[..., None] * br
    gh = S5_GROUPS // 2
    eye = jnp.eye(gh, dtype=F32)

    def b_dense(bb):
        t = bb.reshape(2, gh, S5_STATE, S5_GROUP)
        return jnp.einsum('fgph,gk->fghkp', t, eye).reshape(2, S5_HALF_CH, S5_HALF_ST)

    def c_dense(cc):
        t = cc.reshape(2, gh, S5_GROUP, S5_STATE)
        return jnp.einsum('fghp,gk->fgpkh', t, eye).reshape(2, S5_HALF_ST, S5_HALF_CH)

    def x_tiles(bd):
        tiles = []
        for n in range(S5_NT):
            c = (n // 2) * LANES
            tiles.append(jnp.concatenate([bd[0, c:c + LANES, n * S5_TILE:(n + 1) * S5_TILE],
                                          bd[1, c:c + LANES, n * S5_TILE:(n + 1) * S5_TILE]], axis=0))
        return jnp.stack(tiles)

    def y_tiles(cd):
        tiles = []
        for n in range(S5_NT):
            c = (n // 2) * LANES
            tiles.append(jnp.concatenate([cd[0, n * S5_TILE:(n + 1) * S5_TILE, c:c + LANES],
                                          cd[1, n * S5_TILE:(n + 1) * S5_TILE, c:c + LANES]], axis=1))
        return jnp.stack(tiles)

    wx = jnp.stack([x_tiles(b_dense(bb_re)), x_tiles(b_dense(bb_im))]).astype(MXU_DTYPE)
    wy = jnp.stack([y_tiles(c_dense(c_re.astype(F32))), y_tiles(c_dense(-c_im.astype(F32)))]).astype(MXU_DTYPE)
    return wx, wy, a_re.reshape(2, S5_HALF_ST), a_im.reshape(2, S5_HALF_ST)


def _s5_scan(hp3d, wx, wy, a_re2, a_im2):
    b, s, width = hp3d.shape
    nseq = 2 * b
    assert nseq == SUBLANES
    sel = jnp.arange(nseq) % 2
    a = jnp.stack([a_re2[sel], a_im2[sel]])
    scr = pltpu.VMEM((S5_SLABS, nseq * S5_PITCH, LANES), F32)
    hscr = pltpu.VMEM((S5_SLABS, nseq * S5_H_PITCH, LANES), F32)
    last = s // S5_STEPS - 1
    ucol = width // S5_WIDTH - 1
    return pl.pallas_call(
        _s5_kernel,
        grid=(last + 2,),
        in_specs=[pl.BlockSpec((b, S5_STEPS, S5_WIDTH), lambda i: (0, jnp.minimum(i, last), ucol)),
                  pl.BlockSpec((b, S5_STEPS, S5_WIDTH), lambda i: (0, jnp.minimum(i + 1, last), ucol)),
                  _resident(wx.shape), _resident(wy.shape), _resident(a.shape)],
        out_specs=pl.BlockSpec((b, S5_STEPS, S5_WIDTH), lambda i: (0, jnp.maximum(i - 1, 0), 0)),
        out_shape=jax.ShapeDtypeStruct((b, s, S5_WIDTH), F32),
        scratch_shapes=[pltpu.VMEM((2, nseq, S5_HALF_ST), F32), scr, scr, hscr, hscr],
        compiler_params=_params("arbitrary"),
        name="s5_scan",
    )(hp3d, hp3d, wx, wy, a)


MLP_CHUNK = 1024
TAIL_SPLIT = 2


def _tail_body(mix, x_ref, g1_ref, b1_ref, w1_ref, w2_ref, g2_ref, b2_ref, o_ref):
    rows = x_ref.shape[0]
    for r0 in range(0, rows, rows // TAIL_SPLIT):
        r = slice(r0, r0 + rows // TAIL_SPLIT)
        x1 = _layer_norm(DN_ALPHA * x_ref[r, :] + mix[r], g1_ref[...], b1_ref[...])
        x1m = _mx(x1)
        acc = jnp.zeros(x1.shape, F32)
        for f in range(w1_ref.shape[1] // MLP_CHUNK):
            hid = jnp.dot(x1m, w1_ref[:, f * MLP_CHUNK:(f + 1) * MLP_CHUNK], preferred_element_type=F32)
            hid = jnp.square(jnp.maximum(hid, 0.0))
            acc = acc + jnp.dot(_mx(hid), w2_ref[f * MLP_CHUNK:(f + 1) * MLP_CHUNK, :],
                                preferred_element_type=F32)
        o_ref[r, :] = _layer_norm(DN_ALPHA * x1 + acc, g2_ref[...], b2_ref[...])


def _tail_even_kernel(attn_ref, y_ref, u_ref, dsk_ref, wg_ref, bg_ref, wo_ref, x_ref,
                      g1_ref, b1_ref, w1_ref, w2_ref, g2_ref, b2_ref, o_ref):
    y = y_ref[...] + dsk_ref[...] * u_ref[...]
    g = 0.5 * y * (1.0 + jnp.tanh(math.sqrt(2.0 / math.pi) * (y + 0.044715 * (y * y * y))))
    ssm = g * _sigmoid(jnp.dot(_mx(g), wg_ref[...], preferred_element_type=F32) + bg_ref[...])
    ka = attn_ref.shape[1]
    mix = (jnp.dot(_mx(attn_ref[...]), wo_ref[0:ka, :], preferred_element_type=F32)
           + jnp.dot(_mx(ssm), wo_ref[ka:, :], preferred_element_type=F32))
    _tail_body(mix, x_ref, g1_ref, b1_ref, w1_ref, w2_ref, g2_ref, b2_ref, o_ref)


def _tail_odd_kernel(a_ref, wo_ref, x_ref, g1_ref, b1_ref, w1_ref, w2_ref, g2_ref, b2_ref, o_ref):
    mix = jnp.dot(_mx(a_ref[...]), wo_ref[...], preferred_element_type=F32)
    _tail_body(mix, x_ref, g1_ref, b1_ref, w1_ref, w2_ref, g2_ref, b2_ref, o_ref)


def _row(v):
    return v.reshape(1, -1).astype(F32)


def _tail_common_specs(tm, d, dff):
    return [_resident((d, d)),
            pl.BlockSpec((tm, d), lambda i: (i, 0)),
            _resident((1, d)), _resident((1, d)),
            _resident((d, dff)), _resident((dff, d)),
            _resident((1, d)), _resident((1, d))]


def _tail_even(attn2d, y2d, hp2d, x2d, d_skip, w_glu, b_glu, w_out, g1, b1, w1, w2, g2, b2, tm):
    m, d = x2d.shape
    dff = w1.shape[1]
    ucol = hp2d.shape[1] // S5_WIDTH - 1
    return pl.pallas_call(
        _tail_even_kernel,
        grid=(m // tm,),
        in_specs=[pl.BlockSpec((tm, A_WIDTH), lambda i: (i, 0)),
                  pl.BlockSpec((tm, S5_WIDTH), lambda i: (i, 0)),
                  pl.BlockSpec((tm, S5_WIDTH), lambda i: (i, ucol)),
                  _resident((1, S5_WIDTH)), _resident((S5_WIDTH, S5_WIDTH)), _resident((1, S5_WIDTH))]
        + _tail_common_specs(tm, d, dff),
        out_specs=pl.BlockSpec((tm, d), lambda i: (i, 0)),
        out_shape=jax.ShapeDtypeStruct((m, d), F32),
        compiler_params=_params("parallel"),
        name="tail_even",
    )(attn2d, y2d, hp2d, _row(d_skip), _mx(w_glu), _row(b_glu), _mx(w_out), x2d,
      _row(g1), _row(b1), _mx(w1), _mx(w2), _row(g2), _row(b2))


def _tail_odd(a2d, x2d, w_out, g1, b1, w1, w2, g2, b2, tm):
    m, d = x2d.shape
    dff = w1.shape[1]
    return pl.pallas_call(
        _tail_odd_kernel,
        grid=(m // tm,),
        in_specs=[pl.BlockSpec((tm, a2d.shape[1]), lambda i: (i, 0))] + _tail_common_specs(tm, d, dff),
        out_specs=pl.BlockSpec((tm, d), lambda i: (i, 0)),
        out_shape=jax.ShapeDtypeStruct((m, d), F32),
        compiler_params=_params("parallel"),
        name="tail_odd",
    )(a2d, _mx(w_out), x2d, _row(g1), _row(b1), _mx(w1), _mx(w2), _row(g2), _row(b2))


GDN_TILE = 256
GDN_GATES_TILE = 2048


def _softplus(t):
    return jnp.maximum(t, 0.0) + jnp.log1p(jnp.exp(-jnp.abs(t)))


def _gdn_prep_body(i, hg, q_ref, k_ref, v_ref, qp_ref, kp_ref, vp_ref, cwq_ref, cwk_ref, cwv_ref,
                   ba_ref, g_ref, gt_ref, u_ref, w_ref, qd_ref, kd_ref, intra_ref, edl_ref,
                   qx_ref, kx_ref, vx_ref, ql_ref, kl_ref, vl_ref):
    assert C_CONV == 4
    t, width = q_ref.shape[1], q_ref.shape[2]
    c = C_CHUNK
    dk = C_HEAD_DIM
    hp = width // dk
    groups = t // SUBLANES

    def conv_silu(main_ref, prev_ref, cw_ref, ext_ref, low_ref):
        x = main_ref[0]
        ext_ref[0:SUBLANES, :] = jnp.zeros((SUBLANES, width), F32)
        ext_ref[SUBLANES:2 * SUBLANES, :] = jnp.where(i > 0, prev_ref[0], 0.0)
        ext_ref[2 * SUBLANES:, :] = x
        ext_rows = t + SUBLANES

        def tap(j):
            return jnp.broadcast_to(0.5 * cw_ref[j:j + 1, :], (SUBLANES, width))[None]

        def grouped(a):
            return a.reshape(a.shape[0] // SUBLANES, SUBLANES, width)

        x_ext = grouped(ext_ref[SUBLANES:, :])
        x_del = grouped(ext_ref[SUBLANES - 1:SUBLANES - 1 + ext_rows, :])
        low_ref[...] = (tap(1) * x_ext + tap(0) * x_del).reshape(ext_rows, width)
        hy = (tap(3) * grouped(x) + tap(2) * x_del[1:]
              + grouped(low_ref[SUBLANES - 2:SUBLANES - 2 + t, :])).reshape(t, width)
        return hy + hy * jnp.tanh(hy)

    def l2n(z, scale):
        return z * (lax.rsqrt(jnp.sum(z * z, axis=-1, keepdims=True) + L2_EPS) * scale)

    q_all = conv_silu(q_ref, qp_ref, cwq_ref, qx_ref, ql_ref)
    k_all = conv_silu(k_ref, kp_ref, cwk_ref, kx_ref, kl_ref)
    v_all = conv_silu(v_ref, vp_ref, cwv_ref, vx_ref, vl_ref)
    heads = [slice(hh * dk, (hh + 1) * dk) for hh in range(hp)]
    qs = [l2n(q_all[:, cs], dk ** -0.5) for cs in heads]
    ks = [l2n(k_all[:, cs], 1.0) for cs in heads]
    vs = [v_all[:, cs] for cs in heads]

    ba = ba_ref[0]
    lane = lax.broadcasted_iota(jnp.int32, (t, LANES), 1)
    betas, g_cols, g_rows = [], [], []
    for hh in range(hp):
        h = hg * hp + hh
        betas.append(_sigmoid(jnp.sum(jnp.where(lane == h, ba, 0.0), axis=1, keepdims=True)))
        g_cols.append(jnp.sum(jnp.where(lane == C_HEADS + h, g_ref[0], 0.0), axis=1, keepdims=True))
        g_rows.append(gt_ref[0, pl.ds(C_HEADS + h, 1), :])

    units = [(hh, slice(ch * c, (ch + 1) * c)) for hh in range(hp) for ch in range(t // c)]
    _gdn_prep_units(units, qs, ks, vs, betas, g_cols, g_rows, heads, u_ref, w_ref, qd_ref, kd_ref, intra_ref, edl_ref)


def _gdn_prep_units(units, qs, ks, vs, betas, g_cols, g_rows, heads, u_ref, w_ref, qd_ref, kd_ref, intra_ref, edl_ref):
    c = C_CHUNK
    dk = C_HEAD_DIM
    assert 2 * c == LANES and len(units) % 2 == 0
    ii = lax.broadcasted_iota(jnp.int32, (c, c), 0)
    jj = lax.broadcasted_iota(jnp.int32, (c, c), 1)
    row = lax.broadcasted_iota(jnp.int32, (c, 2 * c), 0)
    lane = lax.broadcasted_iota(jnp.int32, (c, 2 * c), 1)
    left = lane < c
    col = jnp.where(left, lane, lane - c)
    low = row >= col
    pairs = [(2 * n, 2 * n + 1) for n in range(len(units) // 2)]

    def side_by_side(a, b):
        za, zb = jnp.zeros_like(a), jnp.zeros_like(b)
        return jnp.concatenate([jnp.concatenate([a, zb], axis=1), jnp.concatenate([za, b], axis=1)], axis=0)

    def block_diag(p):
        return jnp.concatenate([jnp.where(left, p, 0.0), jnp.where(left, 0.0, p)], axis=0)

    gc_cols = [jnp.sum(jnp.where(ii >= jj, g_rows[hh][:, r], 0.0), axis=1, keepdims=True) for hh, r in units]
    gc_rows = [jnp.sum(jnp.where(ii <= jj, g_cols[hh][r], 0.0), axis=0, keepdims=True) for hh, r in units]
    decays = [jnp.where(low, jnp.exp(jnp.where(low, jnp.where(left, gc_cols[a], gc_cols[b])
                                               - jnp.concatenate([gc_rows[a], gc_rows[b]], axis=1), 0.0)), 0.0)
              for a, b in pairs]
    beta_bs = [jnp.broadcast_to(betas[hh][r], (c, dk)) for hh, r in units]
    kbs = [ks[hh][r] * bb for bb, (hh, r) in zip(beta_bs, units)]
    kdiag = [_mx(side_by_side(ks[units[a][0]][units[a][1]], ks[units[b][0]][units[b][1]])) for a, b in pairs]
    kks = [_dot_nt(jnp.concatenate([kbs[a], kbs[b]], axis=1), kd) for (a, b), kd in zip(pairs, kdiag)]
    qks = [_dot_nt(jnp.concatenate([qs[units[a][0]][units[a][1]], qs[units[b][0]][units[b][1]]], axis=1), kd)
           for (a, b), kd in zip(pairs, kdiag)]
    pws = [jnp.where(row > col, -(kk * dec), 0.0) for kk, dec in zip(kks, decays)]
    tms = pws
    pds = [_mx(block_diag(pw)) for pw in pws]
    for _ in range(int(math.log2(c)) - 1):
        pws = [jnp.dot(_mx(pw), pd, preferred_element_type=F32) for pw, pd in zip(pws, pds)]
        pds = [_mx(block_diag(pw)) for pw in pws]
        tms = [tm + pw + jnp.dot(_mx(tm), pd, preferred_element_type=F32) for tm, pw, pd in zip(tms, pws, pds)]
    egcs = [jnp.broadcast_to(jnp.exp(a), (c, dk)) for a in gc_cols]
    vbs = [vs[hh][r] * bb for bb, (hh, r) in zip(beta_bs, units)]
    kbes = [kb * e for kb, e in zip(kbs, egcs)]
    tbs = [_mx(tm) for tm in tms]
    tvs = [jnp.dot(tb, _mx(side_by_side(vbs[a], vbs[b])), preferred_element_type=F32) for tb, (a, b) in zip(tbs, pairs)]
    tks = [jnp.dot(tb, _mx(side_by_side(kbes[a], kbes[b])), preferred_element_type=F32) for tb, (a, b) in zip(tbs, pairs)]
    for n, (hh, r) in enumerate(units):
        gl = gc_cols[n][c - 1:c, :]
        cs = heads[hh]
        ch = r.start // c
        half = slice((n % 2) * dk, (n % 2 + 1) * dk)
        u_ref[0, r, cs] = vbs[n] + tvs[n // 2][:, half]
        w_ref[0, r, cs] = _mx(kbes[n] + tks[n // 2][:, half])
        qd_ref[0, r, cs] = _mx(qs[hh][r] * egcs[n])
        kd_ref[0, r, cs] = _mx(ks[hh][r] * jnp.exp(gl - gc_cols[n]))
        intra_ref[0, hh, r, :] = _mx((qks[n // 2] * decays[n // 2])[:, (n % 2) * c:(n % 2 + 1) * c])
        edl_ref[0, hh, ch:ch + 1, :] = jnp.broadcast_to(jnp.exp(gl), (1, LANES))


def _gdn_gates_kernel(ba_ref, alog_ref, dtb_ref, g_ref, gt_ref):
    g_all = -jnp.exp(alog_ref[...]) * _softplus(ba_ref[0] + dtb_ref[...])
    g_ref[0] = g_all
    gt_ref[0] = g_all.T


def _gdn_gates(hp3d, alog_row, dtb_row):
    b, s, _ = hp3d.shape
    t = min(GDN_GATES_TILE, s)
    ba_col = 4 * C_WIDTH // LANES
    return pl.pallas_call(
        _gdn_gates_kernel,
        grid=(b, s // t),
        in_specs=[pl.BlockSpec((1, t, LANES), lambda bi, i: (bi, i, ba_col)),
                  pl.BlockSpec((1, LANES), lambda bi, i: (0, 0)),
                  pl.BlockSpec((1, LANES), lambda bi, i: (0, 0))],
        out_specs=[pl.BlockSpec((1, t, LANES), lambda bi, i: (bi, i, 0)),
                   pl.BlockSpec((1, LANES, t), lambda bi, i: (bi, 0, i))],
        out_shape=[jax.ShapeDtypeStruct((b, s, LANES), F32), jax.ShapeDtypeStruct((b, LANES, s), F32)],
        compiler_params=_params("parallel", "parallel"),
        name="gdn_gates",
    )(hp3d, alog_row, dtb_row)


def _gdn_scan_body(chunk0, u_ref, w_ref, qd_ref, kd_ref, intra_ref, edl_ref, gate_ref, nw_ref, o_ref, st_ref):
    c = C_CHUNK
    dk = C_HEAD_DIM
    t = u_ref.shape[1]
    nw = nw_ref[...]
    heads = [slice(h * dk, (h + 1) * dk) for h in range(C_HEADS)]
    for ch in range(t // c):
        r = slice(ch * c, (ch + 1) * c)
        sts = [st_ref[h] for h in range(C_HEADS)]
        wss = [_dot(jnp.concatenate([w_ref[0, r, cs], qd_ref[0, r, cs]], axis=0), st)
               for cs, st in zip(heads, sts)]
        v_news = [u_ref[0, r, cs] - ws[:c] for cs, ws in zip(heads, wss)]
        kvs = [_dot_tn(kd_ref[0, r, cs], v_new) for cs, v_new in zip(heads, v_news)]
        os_ = [ws[c:] + _dot(intra_ref[0, h, r, :], v_new) for h, (ws, v_new) in enumerate(zip(wss, v_news))]
        for h, cs in enumerate(heads):
            st_ref[h] = sts[h] * edl_ref[0, h, pl.ds(chunk0 + ch, 1), 0:1] + kvs[h]
            o = os_[h]
            o = o * lax.rsqrt(jnp.mean(o * o, axis=-1, keepdims=True) + RMS_EPS) * nw
            hg = 0.5 * gate_ref[0, r, cs]
            o_ref[0, r, cs] = _mx(o * (hg + hg * jnp.tanh(hg)))


def _gdn_kernel(q_ref, k_ref, v_ref, qp_ref, kp_ref, vp_ref, cwq_ref, cwk_ref, cwv_ref, ba_ref, g_ref, gt_ref,
                gate_ref, nw_ref, o_ref, st_ref, qx_ref, kx_ref, vx_ref, ql_ref, kl_ref, vl_ref, *bufs):
    j = pl.program_id(1)
    last = pl.num_programs(1) - 2
    set_a, set_b = bufs[:len(bufs) // 2], bufs[len(bufs) // 2:]

    @pl.when(j == 0)
    def _():
        st_ref[...] = jnp.zeros_like(st_ref)
        for ref in bufs:
            ref[...] = jnp.zeros_like(ref)

    def step(cur, nxt):
        _gdn_prep_body(jnp.minimum(j, last), 0, q_ref, k_ref, v_ref, qp_ref, kp_ref, vp_ref,
                       cwq_ref, cwk_ref, cwv_ref, ba_ref, g_ref, gt_ref, *nxt,
                       qx_ref, kx_ref, vx_ref, ql_ref, kl_ref, vl_ref)
        _gdn_scan_body(0, *cur, gate_ref, nw_ref, o_ref, st_ref)

    @pl.when(j % 2 == 0)
    def _():
        step(set_a, set_b)

    @pl.when(j % 2 == 1)
    def _():
        step(set_b, set_a)


def _gdn(hp3d, conv_w, alog_row, dtb_row, norm_w):
    b, s, _ = hp3d.shape
    ba_col = 4 * C_WIDTH // LANES
    t = GDN_TILE
    nt = s // t
    per = t // SUBLANES
    wd = C_WIDTH
    nc = t // C_CHUNK
    g, gt = _gdn_gates(hp3d, alog_row, dtb_row)

    def prep_tile(j):
        return jnp.minimum(j, nt - 1)

    def scan_tile(j):
        return jnp.maximum(j - 1, 0)

    def main(off):
        return pl.BlockSpec((1, t, wd), lambda bi, j: (bi, prep_tile(j), off))

    def prev(off):
        return pl.BlockSpec((1, SUBLANES, wd), lambda bi, j: (bi, jnp.maximum(prep_tile(j) * per - 1, 0), off))

    def cw(off):
        return pl.BlockSpec((C_CONV, wd), lambda bi, j: (0, off))

    bufs = [pltpu.VMEM((1, t, wd), F32)] + [pltpu.VMEM((1, t, wd), MXU_DTYPE)] * 3 + [
        pltpu.VMEM((1, C_HEADS, t, C_CHUNK), MXU_DTYPE), pltpu.VMEM((1, C_HEADS, max(nc, SUBLANES), LANES), F32)]
    return pl.pallas_call(
        _gdn_kernel,
        grid=(b, nt + 1),
        in_specs=[main(0), main(1), main(2), prev(0), prev(1), prev(2), cw(0), cw(1), cw(2),
                  pl.BlockSpec((1, t, LANES), lambda bi, j: (bi, prep_tile(j), ba_col)),
                  pl.BlockSpec((1, t, LANES), lambda bi, j: (bi, prep_tile(j), 0)),
                  pl.BlockSpec((1, LANES, t), lambda bi, j: (bi, 0, prep_tile(j))),
                  pl.BlockSpec((1, t, wd), lambda bi, j: (bi, scan_tile(j), 3)),
                  pl.BlockSpec((1, C_HEAD_DIM), lambda bi, j: (0, 0))],
        out_specs=pl.BlockSpec((1, t, wd), lambda bi, j: (bi, scan_tile(j), 0)),
        out_shape=jax.ShapeDtypeStruct((b, s, wd), MXU_DTYPE),
        scratch_shapes=[pltpu.VMEM((C_HEADS, C_HEAD_DIM, C_HEAD_DIM), F32)]
        + [pltpu.VMEM((t + 2 * SUBLANES, wd), F32)] * 3 + [pltpu.VMEM((t + SUBLANES, wd), F32)] * 3
        + bufs + bufs,
        compiler_params=_params("parallel", "arbitrary"),
        name="gdn",
    )(hp3d, hp3d, hp3d, hp3d, hp3d, hp3d, conv_w, conv_w, conv_w, hp3d, g, gt, hp3d, _row(norm_w))


ROW_TILE = 512


def kernel(x, rel_table, ev_w_in, ev_w_out, s5_lambda_re, s5_lambda_im, s5_b_re, s5_b_im, s5_c_re, s5_c_im, s5_d, s5_log_dt, s5_w_glu, s5_b_glu, od_w_in, od_conv_w, od_a_log, od_dt_bias, od_norm_w, od_w_out, ln_mix_g, ln_mix_b, mlp_w1, mlp_w2, ln_ffn_g, ln_ffn_b):
    bsz, s, d = x.shape
    m = bsz * s
    x2d = x.reshape(m, d)
    table_t = rel_table.T.astype(F32)
    bias = _bias_tiles(table_t)
    for layer in range(DEPTH):
        i = layer // 2
        if layer % 2 == 0:
            qkv, u = _proj_split(x2d, _mx(ev_w_in[i]), 3 * A_WIDTH, ROW_TILE)
            attn = _moba(qkv.reshape(bsz, s, 3 * A_WIDTH), bias)
            wx, wy, a_re2, a_im2 = _s5_weights(s5_lambda_re[i], s5_lambda_im[i], s5_b_re[i], s5_b_im[i],
                                               s5_c_re[i], s5_c_im[i], s5_log_dt[i])
            y = _s5_scan(u.reshape(bsz, s, S5_WIDTH), wx, wy, a_re2, a_im2)
            x2d = _tail_even(attn.reshape(m, A_WIDTH), y.reshape(m, S5_WIDTH), u, x2d,
                             s5_d[i], s5_w_glu[i], s5_b_glu[i], ev_w_out[i],
                             ln_mix_g[layer], ln_mix_b[layer], mlp_w1[layer], mlp_w2[layer],
                             ln_ffn_g[layer], ln_ffn_b[layer], ROW_TILE)
        else:
            w_in = od_w_in[i]
            w_all = jnp.pad(w_in, ((0, 0), (0, LANES - 2 * C_HEADS)))
            hp = _proj(x2d, _mx(w_all), ROW_TILE)
            hp3d = hp.reshape(bsz, s, 4 * C_WIDTH + LANES)
            pad8 = jnp.zeros((C_HEADS,), F32)
            padr = jnp.zeros((LANES - 2 * C_HEADS,), F32)
            alog_row = jnp.concatenate([pad8, od_a_log[i].astype(F32), padr]).reshape(1, LANES)
            dtb_row = jnp.concatenate([pad8, od_dt_bias[i].astype(F32), padr]).reshape(1, LANES)
            o = _gdn(hp3d, od_conv_w[i].astype(F32), alog_row, dtb_row, od_norm_w[i])
            x2d = _tail_odd(o.reshape(m, C_WIDTH), x2d, od_w_out[i],
                            ln_mix_g[layer], ln_mix_b[layer], mlp_w1[layer], mlp_w2[layer],
                            ln_ffn_g[layer], ln_ffn_b[layer], ROW_TILE)
    return x2d.reshape(bsz, s, d)
```

```python
import functools
import math

import jax
import jax.numpy as jnp
import numpy as np
from jax import lax
from jax.experimental import pallas as pl
from jax.experimental.pallas import tpu as pltpu

F32 = jnp.float32
MXU_DTYPE = jnp.bfloat16

LANES = 128
SUBLANES = 8
VMEM_LIMIT_BYTES = 56 * 1024 * 1024

A_HEADS = 8
A_HEAD_DIM = 64
A_WIDTH = A_HEADS * A_HEAD_DIM
MOBA_BLOCK = 256
MOBA_TOPK = 3
REL_BUCKETS = 32
REL_MAX_DIST = 128
S5_GROUP = 16
S5_GROUPS = 32
S5_STATE = 64
S5_WIDTH = S5_GROUP * S5_GROUPS
C_HEADS = 8
C_HEAD_DIM = 128
C_WIDTH = C_HEADS * C_HEAD_DIM
C_CONV = 4
C_CHUNK = 64
DEPTH = 2
DN_ALPHA = (2 * DEPTH) ** 0.25
LN_EPS = 1e-5
RMS_EPS = 1e-6
L2_EPS = 1e-6
NEG_INF = -1e30
LOG2E = math.log2(math.e)


def _mx(t):
    return t.astype(MXU_DTYPE)


def _dot(a, b):
    return jnp.dot(_mx(a), _mx(b), preferred_element_type=F32)


def _dot_tn(a, b):
    return lax.dot_general(_mx(a), _mx(b), (((0,), (0,)), ((), ())), preferred_element_type=F32)


def _dot_nt(a, b):
    return lax.dot_general(_mx(a), _mx(b), (((1,), (1,)), ((), ())), preferred_element_type=F32)


def _sigmoid(t):
    return 1.0 / (1.0 + jnp.exp(-t))


def _layer_norm(r, g, b):
    mu = jnp.mean(r, axis=-1, keepdims=True)
    d = r - mu
    var = jnp.mean(d * d, axis=-1, keepdims=True)
    return d * lax.rsqrt(var + LN_EPS) * g + b


def _params(*sem):
    return pltpu.CompilerParams(dimension_semantics=sem, vmem_limit_bytes=VMEM_LIMIT_BYTES)


def _resident(shape):
    return pl.BlockSpec(shape, lambda *_: (0,) * len(shape), pipeline_mode=pl.Buffered(1))


PROJ_CHUNK = 1024


def _proj_kernel(x_ref, w_ref, o_ref):
    xm = _mx(x_ref[...])
    n = w_ref.shape[1]
    for c0 in range(0, n, PROJ_CHUNK):
        c1 = min(c0 + PROJ_CHUNK, n)
        o_ref[:, c0:c1] = jnp.dot(xm, w_ref[:, c0:c1], preferred_element_type=F32)


def _proj_split_kernel(x_ref, w_ref, lo_ref, hi_ref):
    xm = _mx(x_ref[...])
    n_lo = lo_ref.shape[1]
    for c0 in range(0, n_lo, PROJ_CHUNK):
        c1 = min(c0 + PROJ_CHUNK, n_lo)
        lo_ref[:, c0:c1] = _mx(jnp.dot(xm, w_ref[:, c0:c1], preferred_element_type=F32))
    hi_ref[...] = jnp.dot(xm, w_ref[:, n_lo:], preferred_element_type=F32)


def _proj_split(x2d, w, n_lo, tm):
    m, k = x2d.shape
    n = w.shape[1]
    return pl.pallas_call(
        _proj_split_kernel,
        grid=(m // tm,),
        in_specs=[pl.BlockSpec((tm, k), lambda i: (i, 0)), _resident((k, n))],
        out_specs=[pl.BlockSpec((tm, n_lo), lambda i: (i, 0)), pl.BlockSpec((tm, n - n_lo), lambda i: (i, 0))],
        out_shape=[jax.ShapeDtypeStruct((m, n_lo), MXU_DTYPE), jax.ShapeDtypeStruct((m, n - n_lo), F32)],
        compiler_params=_params("parallel"),
        name="in_proj_split",
    )(x2d, w)


def _proj(x2d, w, tm):
    m, k = x2d.shape
    n = w.shape[1]
    return pl.pallas_call(
        _proj_kernel,
        grid=(m // tm,),
        in_specs=[pl.BlockSpec((tm, k), lambda i: (i, 0)), _resident((k, n))],
        out_specs=pl.BlockSpec((tm, n), lambda i: (i, 0)),
        out_shape=jax.ShapeDtypeStruct((m, n), F32),
        compiler_params=_params("parallel"),
        name="in_proj",
    )(x2d, w)


def _t5_bucket_np(dist):
    max_exact = REL_BUCKETS // 2
    n = np.maximum(dist, 0)
    nf = np.maximum(n, 1).astype(np.float32)
    large = max_exact + (np.log(nf / np.float32(max_exact)) / np.float32(math.log(REL_MAX_DIST / max_exact))
                         * np.float32(REL_BUCKETS - max_exact)).astype(np.int32)
    large = np.minimum(large, REL_BUCKETS - 1)
    return np.where(n < max_exact, n, large).astype(np.int32)


def _bias_kernel(tab_ref, bkt_ref, o_ref):
    h = pl.program_id(0)
    blk = bkt_ref.shape[-1]
    key = lax.broadcasted_iota(jnp.int32, (blk, blk), 0)
    qry = lax.broadcasted_iota(jnp.int32, (blk, blk), 1)
    far = tab_ref[h, REL_BUCKETS - 1]
    for t in range(2):
        bk = bkt_ref[t]
        acc = jnp.zeros((blk, blk), F32)
        for i in range(REL_BUCKETS):
            acc = jnp.where(bk == i, (tab_ref[h, i] - far) * LOG2E, acc)
        if t == 0:
            acc = jnp.where(qry >= key, acc, NEG_INF)
        o_ref[0, t] = acc


def _bias_tiles(table_t):
    heads = table_t.shape[0]
    offs = np.arange(MOBA_BLOCK)
    dist_own = offs[None, :] - offs[:, None]
    buckets = np.stack([_t5_bucket_np(dist_own), _t5_bucket_np(dist_own + MOBA_BLOCK)])
    return pl.pallas_call(
        _bias_kernel,
        grid=(heads,),
        in_specs=[pl.BlockSpec(memory_space=pltpu.SMEM),
                  pl.BlockSpec((2, MOBA_BLOCK, MOBA_BLOCK), lambda h: (0, 0, 0))],
        out_specs=pl.BlockSpec((1, 2, MOBA_BLOCK, MOBA_BLOCK), lambda h: (h, 0, 0, 0)),
        out_shape=jax.ShapeDtypeStruct((heads, 2, MOBA_BLOCK, MOBA_BLOCK), F32),
        compiler_params=_params("parallel"),
        name="t5_bias",
    )(table_t, jnp.asarray(buckets))


def _moba_kernel(q_ref, k_ref, v_ref, bias_ref, o_ref, kmean_ref, kb_ref, vt_ref,
                 st0_ref, st1_ref, wq_ref, acc_ref):
    st_refs = (st0_ref, st1_ref)
    qb = pl.program_id(2)
    blk = MOBA_BLOCK
    d = A_HEAD_DIM
    nh = LANES // d
    nb = k_ref.shape[1] // blk
    ng = q_ref.shape[2] // LANES
    groups = range(ng)

    @pl.when(qb == 0)
    def _():
        lane = lax.broadcasted_iota(jnp.int32, (blk, LANES - d), 1)
        for g in groups:
            gl = slice(g * LANES, (g + 1) * LANES)
            for j in range(nb + 1):
                onehot = jnp.where(lane == j, 1.0, 0.0).astype(MXU_DTYPE)
                for hh in range(nh):
                    kb_ref[g, hh, j, :, d:LANES] = onehot
            for j in range(nb):
                kj = k_ref[0, j * blk:(j + 1) * blk, gl].astype(F32)
                for hh in range(nh):
                    kb_ref[g, hh, j, :, 0:d] = _mx(kj[:, hh * d:(hh + 1) * d])
                kmean_ref[g, j:j + 1, :] = jnp.mean(kj, axis=0, keepdims=True)
                vt_ref[g, j] = _mx(v_ref[0, j * blk:(j + 1) * blk, gl].astype(F32).T)
            for hh in range(nh):
                kb_ref[g, hh, nb, :, 0:d] = jnp.zeros((blk, d), MXU_DTYPE)
            vt_ref[g, nb] = jnp.zeros((LANES, blk), MXU_DTYPE)

    blk_id = lax.broadcasted_iota(jnp.int32, (nb, blk), 0)
    row = lax.broadcasted_iota(jnp.int32, (LANES - d - nb, blk), 0)
    for g in groups:
        qt = (q_ref[0, :, g * LANES:(g + 1) * LANES].astype(F32) * (d ** -0.5)).T
        for hh in range(nh):
            gate = lax.dot_general(kmean_ref[g, :, hh * d:(hh + 1) * d], qt[hh * d:(hh + 1) * d],
                                   (((1,), (0,)), ((), ())),
                                   precision=lax.Precision.HIGHEST, preferred_element_type=F32)
            gt = jnp.where(blk_id < qb, gate, -jnp.inf)
            sel = jnp.zeros((nb, blk), F32)
            for _ in range(MOBA_TOPK):
                m = jnp.max(gt, axis=0, keepdims=True)
                is_m = jnp.logical_and(gt == m, gt > -jnp.inf)
                idx = jnp.min(jnp.where(is_m, blk_id, nb), axis=0, keepdims=True)
                pick = blk_id == idx
                sel = jnp.where(pick, 1.0, sel)
                gt = jnp.where(pick, -jnp.inf, gt)
            keep = jnp.logical_or(sel > 0.0, blk_id == qb)
            wq_ref[g, hh] = _mx(jnp.concatenate([qt[hh * d:(hh + 1) * d] * LOG2E, jnp.where(keep, 0.0, NEG_INF),
                                                 jnp.where(row == 0, NEG_INF, 0.0)], axis=0))

    def score_stage(slot, ja, jb, with_bias):
        cmaxes = []
        for g in groups:
            for hh in range(nh):
                keys = jnp.concatenate([kb_ref[g, hh, ja], kb_ref[g, hh, jb]], axis=0)
                st = jnp.dot(keys, wq_ref[g, hh], preferred_element_type=F32)
                if with_bias:
                    st = st + jnp.concatenate([bias_ref[g * nh + hh, w] for w in range(2)], axis=0)
                st_refs[slot][g, 0, :, hh * blk:(hh + 1) * blk] = st[:blk]
                st_refs[slot][g, 1, :, hh * blk:(hh + 1) * blk] = st[blk:]
                cmaxes.append(jnp.max(st, axis=0, keepdims=True))
        return tuple(cmaxes)

    def attend_stage(slot, cmaxes, stats, ja, jb):
        new = []
        for g in groups:
            for hh in range(nh):
                cs = slice(hh * blk, (hh + 1) * blk)
                m_new = cmaxes[g * nh + hh]
                if stats is not None:
                    m, l = stats[g * nh + hh]
                    m_new = jnp.maximum(m, m_new)
                ps = [jnp.exp2(st_refs[slot][g, w, :, cs] - m_new) for w in range(2)]
                l_new = jnp.sum(ps[0], axis=0, keepdims=True) + jnp.sum(ps[1], axis=0, keepdims=True)
                vt = jnp.concatenate([vt_ref[g, ja, hh * d:(hh + 1) * d, :],
                                      vt_ref[g, jb, hh * d:(hh + 1) * d, :]], axis=1)
                pv = jnp.dot(vt, jnp.concatenate([_mx(ps[0]), _mx(ps[1])], axis=0),
                             preferred_element_type=F32)
                if stats is None:
                    acc_ref[g, hh] = pv
                    new.append((m_new, l_new))
                else:
                    alpha = jnp.exp2(m - m_new)
                    acc_ref[g, hh] = acc_ref[g, hh] * alpha + pv
                    new.append((m_new, alpha * l + l_new))
        return tuple(new)

    nfar = qb - 1
    npairs = (nfar + 1) // 2

    def far_blocks(i):
        return jnp.where(2 * i < nfar, 2 * i, nb), jnp.where(2 * i + 1 < nfar, 2 * i + 1, nb)

    j_prev = jnp.where(qb > 0, qb - 1, nb)
    cmax0 = score_stage(0, qb, j_prev, True)
    cmax1 = score_stage(1, *far_blocks(0), False)
    init = (attend_stage(0, cmax0, None, qb, j_prev), cmax1)

    def trip(slot, k, carry, prefetch=True):
        stats, cmax = carry
        cmax_next = score_stage(1 - slot, *far_blocks(k), False) if prefetch else cmax
        return attend_stage(slot, cmax, stats, *far_blocks(k - 1)), cmax_next

    def double_trip(i, carry):
        k = 2 * i + 1
        return trip(0, k + 1, trip(1, k, carry))

    carry = lax.fori_loop(0, npairs // 2, double_trip, init)
    stats, _ = lax.cond(npairs % 2 == 1, lambda cr: trip(1, npairs, cr, prefetch=False), lambda cr: cr, carry)
    for g in groups:
        out_t = jnp.concatenate([acc_ref[g, hh] / stats[g * nh + hh][1] for hh in range(nh)], axis=0)
        o_ref[0, :, g * LANES:(g + 1) * LANES] = _mx(out_t.T)


MOBA_GROUPS = 4


def _moba(qkv3d, bias):
    b, s, _ = qkv3d.shape
    ng = MOBA_GROUPS
    wd = ng * LANES
    nsteps = A_WIDTH // wd
    hpp = LANES // A_HEAD_DIM
    nb = s // MOBA_BLOCK
    return pl.pallas_call(
        _moba_kernel,
        grid=(b, nsteps, nb),
        in_specs=[pl.BlockSpec((1, MOBA_BLOCK, wd), lambda bi, p, i: (bi, i, p)),
                  pl.BlockSpec((1, s, wd), lambda bi, p, i: (bi, 0, nsteps + p)),
                  pl.BlockSpec((1, s, wd), lambda bi, p, i: (bi, 0, 2 * nsteps + p)),
                  pl.BlockSpec((ng * hpp, 2, MOBA_BLOCK, MOBA_BLOCK), lambda bi, p, i: (p, 0, 0, 0))],
        out_specs=pl.BlockSpec((1, MOBA_BLOCK, wd), lambda bi, p, i: (bi, i, p)),
        out_shape=jax.ShapeDtypeStruct((b, s, A_WIDTH), MXU_DTYPE),
        scratch_shapes=[pltpu.VMEM((ng, nb, LANES), F32),
                        pltpu.VMEM((ng, hpp, nb + 1, MOBA_BLOCK, LANES), MXU_DTYPE),
                        pltpu.VMEM((ng, nb + 1, LANES, MOBA_BLOCK), MXU_DTYPE),
                        pltpu.VMEM((ng, 2, MOBA_BLOCK, hpp * MOBA_BLOCK), F32),
                        pltpu.VMEM((ng, 2, MOBA_BLOCK, hpp * MOBA_BLOCK), F32),
                        pltpu.VMEM((ng, hpp, LANES, MOBA_BLOCK), MXU_DTYPE),
                        pltpu.VMEM((ng, hpp, A_HEAD_DIM, MOBA_BLOCK), F32)],
        compiler_params=_params("parallel", "parallel", "arbitrary"),
        name="moba",
    )(qkv3d, qkv3d, qkv3d, bias)


S5_HALF_CH = S5_WIDTH // 2
S5_HALF_ST = (S5_GROUPS // 2) * S5_STATE
S5_TILE = 256
S5_NT = S5_HALF_ST // S5_TILE
S5_STEPS = 128
S5_PITCH = S5_STEPS + SUBLANES
S5_H_PITCH = S5_STEPS + SUBLANES // 2
S5_SLABS = 2 * S5_HALF_ST // LANES


def _s5_kernel(u0_ref, u_ref, wx_ref, wy_ref, a_ref, o_ref, st_ref, x0_scr, x1_scr, h0_scr, h1_scr):
    nb, steps, _ = u_ref.shape
    nseq = 2 * nb
    tile_slabs = S5_TILE // LANES
    nre = S5_SLABS // 2
    i = pl.program_id(0)

    def project_in_items(src_ref, x_scr):
        def item(m, part, n):
            def run():
                slab0 = (part * S5_HALF_ST + n * S5_TILE) // LANES
                for half in range(2):
                    lhs = jnp.concatenate(
                        [_mx(src_ref[bi, :, half * S5_HALF_CH + m * LANES:half * S5_HALF_CH + (m + 1) * LANES])
                         for bi in range(nb)], axis=0)
                    res = jnp.dot(lhs, wx_ref[part, n, half * LANES:(half + 1) * LANES, :],
                                  preferred_element_type=F32)
                    for bi in range(nb):
                        seq = 2 * bi + half
                        for sl in range(tile_slabs):
                            x_scr[slab0 + sl, seq * S5_PITCH:seq * S5_PITCH + steps, :] = (
                                res[bi * steps:(bi + 1) * steps, sl * LANES:(sl + 1) * LANES])
            return run
        return [item(m, part, n) for m in range(S5_NT // 2) for part in range(2) for n in (2 * m, 2 * m + 1)]

    def recur_items(x_scr, h_scr, nitems):
        state = {}

        def item(t0, t1):
            def run():
                a_re = [a_ref[0, :, s * LANES:(s + 1) * LANES] for s in range(nre)]
                a_im = [a_ref[1, :, s * LANES:(s + 1) * LANES] for s in range(nre)]
                if t0 == 0:
                    state["re"] = [st_ref[0, :, s * LANES:(s + 1) * LANES] for s in range(nre)]
                    state["im"] = [st_ref[1, :, s * LANES:(s + 1) * LANES] for s in range(nre)]
                h_re, h_im = state["re"], state["im"]
                for t in range(t0, t1):
                    rows_t = pl.ds(t, nseq, stride=S5_PITCH)
                    for s in range(nre):
                        hr, hi = h_re[s], h_im[s]
                        h_re[s] = a_re[s] * hr - a_im[s] * hi + x_scr[s, rows_t, :]
                        h_im[s] = a_re[s] * hi + a_im[s] * hr + x_scr[nre + s, rows_t, :]
                        h_scr[s, pl.ds(t, nseq, stride=S5_H_PITCH), :] = h_re[s]
                        h_scr[nre + s, pl.ds(t, nseq, stride=S5_H_PITCH), :] = h_im[s]
                if t1 == steps:
                    for s in range(nre):
                        st_ref[0, :, s * LANES:(s + 1) * LANES] = h_re[s]
                        st_ref[1, :, s * LANES:(s + 1) * LANES] = h_im[s]
            return run
        per = steps // nitems
        return [item(k * per, (k + 1) * per) for k in range(nitems)]

    def project_out_items(h_scr):
        def item(m, half):
            def run():
                acc = jnp.zeros((nb * steps, LANES), F32)
                for n in (2 * m, 2 * m + 1):
                    for part in range(2):
                        slab0 = (part * S5_HALF_ST + n * S5_TILE) // LANES
                        h_tile = jnp.concatenate(
                            [jnp.concatenate([h_scr[slab0 + sl,
                                                    (2 * bi + half) * S5_H_PITCH:(2 * bi + half) * S5_H_PITCH + steps, :]
                                              for sl in range(tile_slabs)], axis=1) for bi in range(nb)], axis=0)
                        acc = acc + jnp.dot(_mx(h_tile), wy_ref[part, n, :, half * LANES:(half + 1) * LANES],
                                            preferred_element_type=F32)
                c = half * S5_HALF_CH + m * LANES
                for bi in range(nb):
                    o_ref[bi, :, c:c + LANES] = acc[bi * steps:(bi + 1) * steps, :]
            return run
        return [item(m, half) for m in range(S5_NT // 2) for half in range(2)]

    @pl.when(i == 0)
    def _():
        st_ref[...] = jnp.zeros_like(st_ref)
        h1_scr[...] = jnp.zeros_like(h1_scr)
        for run in project_in_items(u0_ref, x0_scr):
            run()

    def step(x_cur, h_cur, x_nxt, h_prev):
        ins = project_in_items(u_ref, x_nxt)
        outs = project_out_items(h_prev)
        recs = recur_items(x_cur, h_cur, len(ins))
        for k, rec in enumerate(recs):
            rec()
            ins[k]()
            if k % 2 == 1:
                outs[k // 2]()

    @pl.when(i % 2 == 0)
    def _():
        step(x0_scr, h0_scr, x1_scr, h1_scr)

    @pl.when(i % 2 == 1)
    def _():
        step(x1_scr, h1_scr, x0_scr, h0_scr)


def _s5_weights(lam_re, lam_im, b_re, b_im, c_re, c_im, log_dt):
    lr = jnp.minimum(lam_re.astype(F32), -1e-4)
    li = lam_im.astype(F32)
    dt = jnp.exp(log_dt.astype(F32))[:, None]
    mag = jnp.exp(lr * dt)
    a_re = mag * jnp.cos(li * dt)
    a_im = mag * jnp.sin(li * dt)
    den = lr * lr + li * li
    coef_re = ((a_re - 1.0) * lr + a_im * li) / den
    coef_im = (a_im * lr - (a_re - 1.0) * li) / den
    br = b_re.astype(F32)
    bi = b_im.astype(F32)
    bb_re = coef_re[..., None] * br - coef_im[..., None] * bi
    bb_im = coef_re[..., None] * bi + coef_im[..., None] * br
    gh = S5_GROUPS // 2
    eye = jnp.eye(gh, dtype=F32)

    def b_dense(bb):
        t = bb.reshape(2, gh, S5_STATE, S5_GROUP)
        return jnp.einsum('fgph,gk->fghkp', t, eye).reshape(2, S5_HALF_CH, S5_HALF_ST)

    def c_dense(cc):
        t = cc.reshape(2, gh, S5_GROUP, S5_STATE)
        return jnp.einsum('fghp,gk->fgpkh', t, eye).reshape(2, S5_HALF_ST, S5_HALF_CH)

    def x_tiles(bd):
        tiles = []
        for n in range(S5_NT):
            c = (n // 2) * LANES
            tiles.append(jnp.concatenate([bd[0, c:c + LANES, n * S5_TILE:(n + 1) * S5_TILE],
                                          bd[1, c:c + LANES, n * S5_TILE:(n + 1) * S5_TILE]], axis=0))
        return jnp.stack(tiles)

    def y_tiles(cd):
        tiles = []
        for n in range(S5_NT):
            c = (n // 2) * LANES
            tiles.append(jnp.concatenate([cd[0, n * S5_TILE:(n + 1) * S5_TILE, c:c + LANES],
                                          cd[1, n * S5_TILE:(n + 1) * S5_TILE, c:c + LANES]], axis=1))
        return jnp.stack(tiles)

    wx = jnp.stack([x_tiles(b_dense(bb_re)), x_tiles(b_dense(bb_im))]).astype(MXU_DTYPE)
    wy = jnp.stack([y_tiles(c_dense(c_re.astype(F32))), y_tiles(c_dense(-c_im.astype(F32)))]).astype(MXU_DTYPE)
    return wx, wy, a_re.reshape(2, S5_HALF_ST), a_im.reshape(2, S5_HALF_ST)


def _s5_scan(hp3d, wx, wy, a_re2, a_im2):
    b, s, width = hp3d.shape
    nseq = 2 * b
    assert nseq == SUBLANES
    sel = jnp.arange(nseq) % 2
    a = jnp.stack([a_re2[sel], a_im2[sel]])
    scr = pltpu.VMEM((S5_SLABS, nseq * S5_PITCH, LANES), F32)
    hscr = pltpu.VMEM((S5_SLABS, nseq * S5_H_PITCH, LANES), F32)
    last = s // S5_STEPS - 1
    ucol = width // S5_WIDTH - 1
    return pl.pallas_call(
        _s5_kernel,
        grid=(last + 2,),
        in_specs=[pl.BlockSpec((b, S5_STEPS, S5_WIDTH), lambda i: (0, jnp.minimum(i, last), ucol)),
                  pl.BlockSpec((b, S5_STEPS, S5_WIDTH), lambda i: (0, jnp.minimum(i + 1, last), ucol)),
                  _resident(wx.shape), _resident(wy.shape), _resident(a.shape)],
        out_specs=pl.BlockSpec((b, S5_STEPS, S5_WIDTH), lambda i: (0, jnp.maximum(i - 1, 0), 0)),
        out_shape=jax.ShapeDtypeStruct((b, s, S5_WIDTH), F32),
        scratch_shapes=[pltpu.VMEM((2, nseq, S5_HALF_ST), F32), scr, scr, hscr, hscr],
        compiler_params=_params("arbitrary"),
        name="s5_scan",
    )(hp3d, hp3d, wx, wy, a)


MLP_CHUNK = 1024
TAIL_SPLIT = 2


def _tail_body(mix, x_ref, g1_ref, b1_ref, w1_ref, w2_ref, g2_ref, b2_ref, o_ref):
    rows = x_ref.shape[0]
    for r0 in range(0, rows, rows // TAIL_SPLIT):
        r = slice(r0, r0 + rows // TAIL_SPLIT)
        x1 = _layer_norm(DN_ALPHA * x_ref[r, :] + mix[r], g1_ref[...], b1_ref[...])
        x1m = _mx(x1)
        acc = jnp.zeros(x1.shape, F32)
        for f in range(w1_ref.shape[1] // MLP_CHUNK):
            hid = jnp.dot(x1m, w1_ref[:, f * MLP_CHUNK:(f + 1) * MLP_CHUNK], preferred_element_type=F32)
            hid = jnp.square(jnp.maximum(hid, 0.0))
            acc = acc + jnp.dot(_mx(hid), w2_ref[f * MLP_CHUNK:(f + 1) * MLP_CHUNK, :],
                                preferred_element_type=F32)
        o_ref[r, :] = _layer_norm(DN_ALPHA * x1 + acc, g2_ref[...], b2_ref[...])


def _tail_even_kernel(attn_ref, y_ref, u_ref, dsk_ref, wg_ref, bg_ref, wo_ref, x_ref,
                      g1_ref, b1_ref, w1_ref, w2_ref, g2_ref, b2_ref, o_ref):
    y = y_ref[...] + dsk_ref[...] * u_ref[...]
    g = 0.5 * y * (1.0 + jnp.tanh(math.sqrt(2.0 / math.pi) * (y + 0.044715 * (y * y * y))))
    ssm = g * _sigmoid(jnp.dot(_mx(g), wg_ref[...], preferred_element_type=F32) + bg_ref[...])
    ka = attn_ref.shape[1]
    mix = (jnp.dot(_mx(attn_ref[...]), wo_ref[0:ka, :], preferred_element_type=F32)
           + jnp.dot(_mx(ssm), wo_ref[ka:, :], preferred_element_type=F32))
    _tail_body(mix, x_ref, g1_ref, b1_ref, w1_ref, w2_ref, g2_ref, b2_ref, o_ref)


def _tail_odd_kernel(a_ref, wo_ref, x_ref, g1_ref, b1_ref, w1_ref, w2_ref, g2_ref, b2_ref, o_ref):
    mix = jnp.dot(_mx(a_ref[...]), wo_ref[...], preferred_element_type=F32)
    _tail_body(mix, x_ref, g1_ref, b1_ref, w1_ref, w2_ref, g2_ref, b2_ref, o_ref)


def _row(v):
    return v.reshape(1, -1).astype(F32)


def _tail_common_specs(tm, d, dff):
    return [_resident((d, d)),
            pl.BlockSpec((tm, d), lambda i: (i, 0)),
            _resident((1, d)), _resident((1, d)),
            _resident((d, dff)), _resident((dff, d)),
            _resident((1, d)), _resident((1, d))]


def _tail_even(attn2d, y2d, hp2d, x2d, d_skip, w_glu, b_glu, w_out, g1, b1, w1, w2, g2, b2, tm):
    m, d = x2d.shape
    dff = w1.shape[1]
    ucol = hp2d.shape[1] // S5_WIDTH - 1
    return pl.pallas_call(
        _tail_even_kernel,
        grid=(m // tm,),
        in_specs=[pl.BlockSpec((tm, A_WIDTH), lambda i: (i, 0)),
                  pl.BlockSpec((tm, S5_WIDTH), lambda i: (i, 0)),
                  pl.BlockSpec((tm, S5_WIDTH), lambda i: (i, ucol)),
                  _resident((1, S5_WIDTH)), _resident((S5_WIDTH, S5_WIDTH)), _resident((1, S5_WIDTH))]
        + _tail_common_specs(tm, d, dff),
        out_specs=pl.BlockSpec((tm, d), lambda i: (i, 0)),
        out_shape=jax.ShapeDtypeStruct((m, d), F32),
        compiler_params=_params("parallel"),
        name="tail_even",
    )(attn2d, y2d, hp2d, _row(d_skip), _mx(w_glu), _row(b_glu), _mx(w_out), x2d,
      _row(g1), _row(b1), _mx(w1), _mx(w2), _row(g2), _row(b2))


def _tail_odd(a2d, x2d, w_out, g1, b1, w1, w2, g2, b2, tm):
    m, d = x2d.shape
    dff = w1.shape[1]
    return pl.pallas_call(
        _tail_odd_kernel,
        grid=(m // tm,),
        in_specs=[pl.BlockSpec((tm, a2d.shape[1]), lambda i: (i, 0))] + _tail_common_specs(tm, d, dff),
        out_specs=pl.BlockSpec((tm, d), lambda i: (i, 0)),
        out_shape=jax.ShapeDtypeStruct((m, d), F32),
        compiler_params=_params("parallel"),
        name="tail_odd",
    )(a2d, _mx(w_out), x2d, _row(g1), _row(b1), _mx(w1), _mx(w2), _row(g2), _row(b2))


GDN_TILE = 256
GDN_GATES_TILE = 2048


def _softplus(t):
    return jnp.maximum(t, 0.0) + jnp.log1p(jnp.exp(-jnp.abs(t)))


def _gdn_prep_body(i, hg, q_ref, k_ref, v_ref, qp_ref, kp_ref, vp_ref, cwq_ref, cwk_ref, cwv_ref,
                   ba_ref, g_ref, gt_ref, u_ref, w_ref, qd_ref, kd_ref, intra_ref, edl_ref,
                   qx_ref, kx_ref, vx_ref, ql_ref, kl_ref, vl_ref):
    assert C_CONV == 4
    t, width = q_ref.shape[1], q_ref.shape[2]
    c = C_CHUNK
    dk = C_HEAD_DIM
    hp = width // dk
    groups = t // SUBLANES

    def conv_silu(main_ref, prev_ref, cw_ref, ext_ref, low_ref):
        x = main_ref[0]
        ext_ref[0:SUBLANES, :] = jnp.zeros((SUBLANES, width), F32)
        ext_ref[SUBLANES:2 * SUBLANES, :] = jnp.where(i > 0, prev_ref[0], 0.0)
        ext_ref[2 * SUBLANES:, :] = x
        ext_rows = t + SUBLANES

        def tap(j):
            return jnp.broadcast_to(0.5 * cw_ref[j:j + 1, :], (SUBLANES, width))[None]

        def grouped(a):
            return a.reshape(a.shape[0] // SUBLANES, SUBLANES, width)

        x_ext = grouped(ext_ref[SUBLANES:, :])
        x_del = grouped(ext_ref[SUBLANES - 1:SUBLANES - 1 + ext_rows, :])
        low_ref[...] = (tap(1) * x_ext + tap(0) * x_del).reshape(ext_rows, width)
        hy = (tap(3) * grouped(x) + tap(2) * x_del[1:]
              + grouped(low_ref[SUBLANES - 2:SUBLANES - 2 + t, :])).reshape(t, width)
        return hy + hy * jnp.tanh(hy)

    def l2n(z, scale):
        return z * (lax.rsqrt(jnp.sum(z * z, axis=-1, keepdims=True) + L2_EPS) * scale)

    q_all = conv_silu(q_ref, qp_ref, cwq_ref, qx_ref, ql_ref)
    k_all = conv_silu(k_ref, kp_ref, cwk_ref, kx_ref, kl_ref)
    v_all = conv_silu(v_ref, vp_ref, cwv_ref, vx_ref, vl_ref)
    heads = [slice(hh * dk, (hh + 1) * dk) for hh in range(hp)]
    qs = [l2n(q_all[:, cs], dk ** -0.5) for cs in heads]
    ks = [l2n(k_all[:, cs], 1.0) for cs in heads]
    vs = [v_all[:, cs] for cs in heads]

    ba = ba_ref[0]
    lane = lax.broadcasted_iota(jnp.int32, (t, LANES), 1)
    betas, g_cols, g_rows = [], [], []
    for hh in range(hp):
        h = hg * hp + hh
        betas.append(_sigmoid(jnp.sum(jnp.where(lane == h, ba, 0.0), axis=1, keepdims=True)))
        g_cols.append(jnp.sum(jnp.where(lane == C_HEADS + h, g_ref[0], 0.0), axis=1, keepdims=True))
        g_rows.append(gt_ref[0, pl.ds(C_HEADS + h, 1), :])

    units = [(hh, slice(ch * c, (ch + 1) * c)) for hh in range(hp) for ch in range(t // c)]
    _gdn_prep_units(units, qs, ks, vs, betas, g_cols, g_rows, heads, u_ref, w_ref, qd_ref, kd_ref, intra_ref, edl_ref)


def _gdn_prep_units(units, qs, ks, vs, betas, g_cols, g_rows, heads, u_ref, w_ref, qd_ref, kd_ref, intra_ref, edl_ref):
    c = C_CHUNK
    dk = C_HEAD_DIM
    assert 2 * c == LANES and len(units) % 2 == 0
    ii = lax.broadcasted_iota(jnp.int32, (c, c), 0)
    jj = lax.broadcasted_iota(jnp.int32, (c, c), 1)
    row = lax.broadcasted_iota(jnp.int32, (c, 2 * c), 0)
    lane = lax.broadcasted_iota(jnp.int32, (c, 2 * c), 1)
    left = lane < c
    col = jnp.where(left, lane, lane - c)
    low = row >= col
    pairs = [(2 * n, 2 * n + 1) for n in range(len(units) // 2)]

    def side_by_side(a, b):
        za, zb = jnp.zeros_like(a), jnp.zeros_like(b)
        return jnp.concatenate([jnp.concatenate([a, zb], axis=1), jnp.concatenate([za, b], axis=1)], axis=0)

    def block_diag(p):
        return jnp.concatenate([jnp.where(left, p, 0.0), jnp.where(left, 0.0, p)], axis=0)

    gc_cols = [jnp.sum(jnp.where(ii >= jj, g_rows[hh][:, r], 0.0), axis=1, keepdims=True) for hh, r in units]
    gc_rows = [jnp.sum(jnp.where(ii <= jj, g_cols[hh][r], 0.0), axis=0, keepdims=True) for hh, r in units]
    decays = [jnp.where(low, jnp.exp(jnp.where(low, jnp.where(left, gc_cols[a], gc_cols[b])
                                               - jnp.concatenate([gc_rows[a], gc_rows[b]], axis=1), 0.0)), 0.0)
              for a, b in pairs]
    beta_bs = [jnp.broadcast_to(betas[hh][r], (c, dk)) for hh, r in units]
    kbs = [ks[hh][r] * bb for bb, (hh, r) in zip(beta_bs, units)]
    kdiag = [_mx(side_by_side(ks[units[a][0]][units[a][1]], ks[units[b][0]][units[b][1]])) for a, b in pairs]
    kks = [_dot_nt(jnp.concatenate([kbs[a], kbs[b]], axis=1), kd) for (a, b), kd in zip(pairs, kdiag)]
    qks = [_dot_nt(jnp.concatenate([qs[units[a][0]][units[a][1]], qs[units[b][0]][units[b][1]]], axis=1), kd)
           for (a, b), kd in zip(pairs, kdiag)]
    pws = [jnp.where(row > col, -(kk * dec), 0.0) for kk, dec in zip(kks, decays)]
    tms = pws
    pds = [_mx(block_diag(pw)) for pw in pws]
    for _ in range(int(math.log2(c)) - 1):
        pws = [jnp.dot(_mx(pw), pd, preferred_element_type=F32) for pw, pd in zip(pws, pds)]
        pds = [_mx(block_diag(pw)) for pw in pws]
        tms = [tm + pw + jnp.dot(_mx(tm), pd, preferred_element_type=F32) for tm, pw, pd in zip(tms, pws, pds)]
    egcs = [jnp.broadcast_to(jnp.exp(a), (c, dk)) for a in gc_cols]
    vbs = [vs[hh][r] * bb for bb, (hh, r) in zip(beta_bs, units)]
    kbes = [kb * e for kb, e in zip(kbs, egcs)]
    tbs = [_mx(tm) for tm in tms]
    tvs = [jnp.dot(tb, _mx(side_by_side(vbs[a], vbs[b])), preferred_element_type=F32) for tb, (a, b) in zip(tbs, pairs)]
    tks = [jnp.dot(tb, _mx(side_by_side(kbes[a], kbes[b])), preferred_element_type=F32) for tb, (a, b) in zip(tbs, pairs)]
    for n, (hh, r) in enumerate(units):
        gl = gc_cols[n][c - 1:c, :]
        cs = heads[hh]
        ch = r.start // c
        half = slice((n % 2) * dk, (n % 2 + 1) * dk)
        u_ref[0, r, cs] = vbs[n] + tvs[n // 2][:, half]
        w_ref[0, r, cs] = _mx(kbes[n] + tks[n // 2][:, half])
        qd_ref[0, r, cs] = _mx(qs[hh][r] * egcs[n])
        kd_ref[0, r, cs] = _mx(ks[hh][r] * jnp.exp(gl - gc_cols[n]))
        intra_ref[0, hh, r, :] = _mx((qks[n // 2] * decays[n // 2])[:, (n % 2) * c:(n % 2 + 1) * c])
        edl_ref[0, hh, ch:ch + 1, :] = jnp.broadcast_to(jnp.exp(gl), (1, LANES))


def _gdn_gates_kernel(ba_ref, alog_ref, dtb_ref, g_ref, gt_ref):
    g_all = -jnp.exp(alog_ref[...]) * _softplus(ba_ref[0] + dtb_ref[...])
    g_ref[0] = g_all
    gt_ref[0] = g_all.T


def _gdn_gates(hp3d, alog_row, dtb_row):
    b, s, _ = hp3d.shape
    t = min(GDN_GATES_TILE, s)
    ba_col = 4 * C_WIDTH // LANES
    return pl.pallas_call(
        _gdn_gates_kernel,
        grid=(b, s // t),
        in_specs=[pl.BlockSpec((1, t, LANES), lambda bi, i: (bi, i, ba_col)),
                  pl.BlockSpec((1, LANES), lambda bi, i: (0, 0)),
                  pl.BlockSpec((1, LANES), lambda bi, i: (0, 0))],
        out_specs=[pl.BlockSpec((1, t, LANES), lambda bi, i: (bi, i, 0)),
                   pl.BlockSpec((1, LANES, t), lambda bi, i: (bi, 0, i))],
        out_shape=[jax.ShapeDtypeStruct((b, s, LANES), F32), jax.ShapeDtypeStruct((b, LANES, s), F32)],
        compiler_params=_params("parallel", "parallel"),
        name="gdn_gates",
    )(hp3d, alog_row, dtb_row)


def _gdn_scan_body(chunk0, u_ref, w_ref, qd_ref, kd_ref, intra_ref, edl_ref, gate_ref, nw_ref, o_ref, st_ref):
    c = C_CHUNK
    dk = C_HEAD_DIM
    t = u_ref.shape[1]
    nw = nw_ref[...]
    heads = [slice(h * dk, (h + 1) * dk) for h in range(C_HEADS)]
    for ch in range(t // c):
        r = slice(ch * c, (ch + 1) * c)
        sts = [st_ref[h] for h in range(C_HEADS)]
        wss = [_dot(jnp.concatenate([w_ref[0, r, cs], qd_ref[0, r, cs]], axis=0), st)
               for cs, st in zip(heads, sts)]
        v_news = [u_ref[0, r, cs] - ws[:c] for cs, ws in zip(heads, wss)]
        kvs = [_dot_tn(kd_ref[0, r, cs], v_new) for cs, v_new in zip(heads, v_news)]
        os_ = [ws[c:] + _dot(intra_ref[0, h, r, :], v_new) for h, (ws, v_new) in enumerate(zip(wss, v_news))]
        for h, cs in enumerate(heads):
            st_ref[h] = sts[h] * edl_ref[0, h, pl.ds(chunk0 + ch, 1), 0:1] + kvs[h]
            o = os_[h]
            o = o * lax.rsqrt(jnp.mean(o * o, axis=-1, keepdims=True) + RMS_EPS) * nw
            hg = gate_ref[0, r, cs]
            o_ref[0, r, cs] = _mx(o * (hg + hg * jnp.tanh(hg)))


def _gdn_kernel(q_ref, k_ref, v_ref, qp_ref, kp_ref, vp_ref, cwq_ref, cwk_ref, cwv_ref, ba_ref, g_ref, gt_ref,
                gate_ref, nw_ref, o_ref, st_ref, qx_ref, kx_ref, vx_ref, ql_ref, kl_ref, vl_ref, *bufs):
    j = pl.program_id(1)
    last = pl.num_programs(1) - 2
    set_a, set_b = bufs[:len(bufs) // 2], bufs[len(bufs) // 2:]

    @pl.when(j == 0)
    def _():
        st_ref[...] = jnp.zeros_like(st_ref)
        for ref in bufs:
            ref[...] = jnp.zeros_like(ref)

    def step(cur, nxt):
        _gdn_prep_body(jnp.minimum(j, last), 0, q_ref, k_ref, v_ref, qp_ref, kp_ref, vp_ref,
                       cwq_ref, cwk_ref, cwv_ref, ba_ref, g_ref, gt_ref, *nxt,
                       qx_ref, kx_ref, vx_ref, ql_ref, kl_ref, vl_ref)
        _gdn_scan_body(0, *cur, gate_ref, nw_ref, o_ref, st_ref)

    @pl.when(j % 2 == 0)
    def _():
        step(set_a, set_b)

    @pl.when(j % 2 == 1)
    def _():
        step(set_b, set_a)


def _gdn(hp3d, conv_w, alog_row, dtb_row, norm_w):
    b, s, _ = hp3d.shape
    ba_col = 4 * C_WIDTH // LANES
    t = GDN_TILE
    nt = s // t
    per = t // SUBLANES
    wd = C_WIDTH
    nc = t // C_CHUNK
    g, gt = _gdn_gates(hp3d, alog_row, dtb_row)

    def prep_tile(j):
        return jnp.minimum(j, nt - 1)

    def scan_tile(j):
        return jnp.maximum(j - 1, 0)

    def main(off):
        return pl.BlockSpec((1, t, wd), lambda bi, j: (bi, prep_tile(j), off))

    def prev(off):
        return pl.BlockSpec((1, SUBLANES, wd), lambda bi, j: (bi, jnp.maximum(prep_tile(j) * per - 1, 0), off))

    def cw(off):
        return pl.BlockSpec((C_CONV, wd), lambda bi, j: (0, off))

    bufs = [pltpu.VMEM((1, t, wd), F32)] + [pltpu.VMEM((1, t, wd), MXU_DTYPE)] * 3 + [
        pltpu.VMEM((1, C_HEADS, t, C_CHUNK), MXU_DTYPE), pltpu.VMEM((1, C_HEADS, max(nc, SUBLANES), LANES), F32)]
    return pl.pallas_call(
        _gdn_kernel,
        grid=(b, nt + 1),
        in_specs=[main(0), main(1), main(2), prev(0), prev(1), prev(2), cw(0), cw(1), cw(2),
                  pl.BlockSpec((1, t, LANES), lambda bi, j: (bi, prep_tile(j), ba_col)),
                  pl.BlockSpec((1, t, LANES), lambda bi, j: (bi, prep_tile(j), 0)),
                  pl.BlockSpec((1, LANES, t), lambda bi, j: (bi, 0, prep_tile(j))),
                  pl.BlockSpec((1, t, wd), lambda bi, j: (bi, scan_tile(j), 3)),
                  pl.BlockSpec((1, C_HEAD_DIM), lambda bi, j: (0, 0))],
        out_specs=pl.BlockSpec((1, t, wd), lambda bi, j: (bi, scan_tile(j), 0)),
        out_shape=jax.ShapeDtypeStruct((b, s, wd), MXU_DTYPE),
        scratch_shapes=[pltpu.VMEM((C_HEADS, C_HEAD_DIM, C_HEAD_DIM), F32)]
        + [pltpu.VMEM((t + 2 * SUBLANES, wd), F32)] * 3 + [pltpu.VMEM((t + SUBLANES, wd), F32)] * 3
        + bufs + bufs,
        compiler_params=_params("parallel", "arbitrary"),
        name="gdn",
    )(hp3d, hp3d, hp3d, hp3d, hp3d, hp3d, conv_w, conv_w, conv_w, hp3d, g, gt, hp3d, _row(norm_w))


ROW_TILE = 512


def kernel(x, rel_table, ev_w_in, ev_w_out, s5_lambda_re, s5_lambda_im, s5_b_re, s5_b_im, s5_c_re, s5_c_im, s5_d, s5_log_dt, s5_w_glu, s5_b_glu, od_w_in, od_conv_w, od_a_log, od_dt_bias, od_norm_w, od_w_out, ln_mix_g, ln_mix_b, mlp_w1, mlp_w2, ln_ffn_g, ln_ffn_b):
    bsz, s, d = x.shape
    m = bsz * s
    x2d = x.reshape(m, d)
    table_t = rel_table.T.astype(F32)
    bias = _bias_tiles(table_t)
    for layer in range(DEPTH):
        i = layer // 2
        if layer % 2 == 0:
            qkv, u = _proj_split(x2d, _mx(ev_w_in[i]), 3 * A_WIDTH, ROW_TILE)
            attn = _moba(qkv.reshape(bsz, s, 3 * A_WIDTH), bias)
            wx, wy, a_re2, a_im2 = _s5_weights(s5_lambda_re[i], s5_lambda_im[i], s5_b_re[i], s5_b_im[i],
                                               s5_c_re[i], s5_c_im[i], s5_log_dt[i])
            y = _s5_scan(u.reshape(bsz, s, S5_WIDTH), wx, wy, a_re2, a_im2)
            x2d = _tail_even(attn.reshape(m, A_WIDTH), y.reshape(m, S5_WIDTH), u, x2d,
                             s5_d[i], s5_w_glu[i], s5_b_glu[i], ev_w_out[i],
                             ln_mix_g[layer], ln_mix_b[layer], mlp_w1[layer], mlp_w2[layer],
                             ln_ffn_g[layer], ln_ffn_b[layer], ROW_TILE)
        else:
            w_in = od_w_in[i]
            col = jnp.arange(w_in.shape[1])
            gate_cols = jnp.logical_and(col >= 3 * C_WIDTH, col < 4 * C_WIDTH)
            w_all = jnp.pad(jnp.where(gate_cols[None, :], 0.5 * w_in, w_in), ((0, 0), (0, LANES - 2 * C_HEADS)))
            hp = _proj(x2d, _mx(w_all), ROW_TILE)
            hp3d = hp.reshape(bsz, s, 4 * C_WIDTH + LANES)
            pad8 = jnp.zeros((C_HEADS,), F32)
            padr = jnp.zeros((LANES - 2 * C_HEADS,), F32)
            alog_row = jnp.concatenate([pad8, od_a_log[i].astype(F32), padr]).reshape(1, LANES)
            dtb_row = jnp.concatenate([pad8, od_dt_bias[i].astype(F32), padr]).reshape(1, LANES)
            o = _gdn(hp3d, od_conv_w[i].astype(F32), alog_row, dtb_row, od_norm_w[i])
            x2d = _tail_odd(o.reshape(m, C_WIDTH), x2d, od_w_out[i],
                            ln_mix_g[layer], ln_mix_b[layer], mlp_w1[layer], mlp_w2[layer],
                            ln_ffn_g[layer], ln_ffn_b[layer], ROW_TILE)
    return x2d.reshape(bsz, s, d)
```

```python
import functools
import math

import jax
import jax.numpy as jnp
import numpy as np
from jax import lax
from jax.experimental import pallas as pl
from jax.experimental.pallas import tpu as pltpu

F32 = jnp.float32
MXU_DTYPE = jnp.bfloat16

LANES = 128
SUBLANES = 8
VMEM_LIMIT_BYTES = 56 * 1024 * 1024

A_HEADS = 8
A_HEAD_DIM = 64
A_WIDTH = A_HEADS * A_HEAD_DIM
MOBA_BLOCK = 256
MOBA_TOPK = 3
REL_BUCKETS = 32
REL_MAX_DIST = 128
S5_GROUP = 16
S5_GROUPS = 32
S5_STATE = 64
S5_WIDTH = S5_GROUP * S5_GROUPS
C_HEADS = 8
C_HEAD_DIM = 128
C_WIDTH = C_HEADS * C_HEAD_DIM
C_CONV = 4
C_CHUNK = 64
DEPTH = 2
DN_ALPHA = (2 * DEPTH) ** 0.25
LN_EPS = 1e-5
RMS_EPS = 1e-6
L2_EPS = 1e-6
NEG_INF = -1e30
LOG2E = math.log2(math.e)


def _mx(t):
    return t.astype(MXU_DTYPE)


def _dot(a, b):
    return jnp.dot(_mx(a), _mx(b), preferred_element_type=F32)


def _dot_tn(a, b):
    return lax.dot_general(_mx(a), _mx(b), (((0,), (0,)), ((), ())), preferred_element_type=F32)


def _dot_nt(a, b):
    return lax.dot_general(_mx(a), _mx(b), (((1,), (1,)), ((), ())), preferred_element_type=F32)


def _sigmoid(t):
    return 1.0 / (1.0 + jnp.exp(-t))


def _layer_norm(r, g, b):
    mu = jnp.mean(r, axis=-1, keepdims=True)
    d = r - mu
    var = jnp.mean(d * d, axis=-1, keepdims=True)
    return d * lax.rsqrt(var + LN_EPS) * g + b


def _params(*sem):
    return pltpu.CompilerParams(dimension_semantics=sem, vmem_limit_bytes=VMEM_LIMIT_BYTES)


def _resident(shape):
    return pl.BlockSpec(shape, lambda *_: (0,) * len(shape), pipeline_mode=pl.Buffered(1))


PROJ_CHUNK = 1024


def _proj_kernel(x_ref, w_ref, o_ref):
    xm = _mx(x_ref[...])
    n = w_ref.shape[1]
    for c0 in range(0, n, PROJ_CHUNK):
        c1 = min(c0 + PROJ_CHUNK, n)
        o_ref[:, c0:c1] = jnp.dot(xm, w_ref[:, c0:c1], preferred_element_type=F32)


def _proj_split_kernel(x_ref, w_ref, lo_ref, hi_ref):
    xm = _mx(x_ref[...])
    n_lo = lo_ref.shape[1]
    for c0 in range(0, n_lo, PROJ_CHUNK):
        c1 = min(c0 + PROJ_CHUNK, n_lo)
        lo_ref[:, c0:c1] = _mx(jnp.dot(xm, w_ref[:, c0:c1], preferred_element_type=F32))
    hi_ref[...] = jnp.dot(xm, w_ref[:, n_lo:], preferred_element_type=F32)


def _proj_split(x2d, w, n_lo, tm):
    m, k = x2d.shape
    n = w.shape[1]
    return pl.pallas_call(
        _proj_split_kernel,
        grid=(m // tm,),
        in_specs=[pl.BlockSpec((tm, k), lambda i: (i, 0)), _resident((k, n))],
        out_specs=[pl.BlockSpec((tm, n_lo), lambda i: (i, 0)), pl.BlockSpec((tm, n - n_lo), lambda i: (i, 0))],
        out_shape=[jax.ShapeDtypeStruct((m, n_lo), MXU_DTYPE), jax.ShapeDtypeStruct((m, n - n_lo), F32)],
        compiler_params=_params("parallel"),
        name="in_proj_split",
    )(x2d, w)


def _proj(x2d, w, tm):
    m, k = x2d.shape
    n = w.shape[1]
    return pl.pallas_call(
        _proj_kernel,
        grid=(m // tm,),
        in_specs=[pl.BlockSpec((tm, k), lambda i: (i, 0)), _resident((k, n))],
        out_specs=pl.BlockSpec((tm, n), lambda i: (i, 0)),
        out_shape=jax.ShapeDtypeStruct((m, n), F32),
        compiler_params=_params("parallel"),
        name="in_proj",
    )(x2d, w)


def _t5_bucket_np(dist):
    max_exact = REL_BUCKETS // 2
    n = np.maximum(dist, 0)
    nf = np.maximum(n, 1).astype(np.float32)
    large = max_exact + (np.log(nf / np.float32(max_exact)) / np.float32(math.log(REL_MAX_DIST / max_exact))
                         * np.float32(REL_BUCKETS - max_exact)).astype(np.int32)
    large = np.minimum(large, REL_BUCKETS - 1)
    return np.where(n < max_exact, n, large).astype(np.int32)


def _bias_kernel(tab_ref, bkt_ref, o_ref):
    h = pl.program_id(0)
    blk = bkt_ref.shape[-1]
    key = lax.broadcasted_iota(jnp.int32, (blk, blk), 0)
    qry = lax.broadcasted_iota(jnp.int32, (blk, blk), 1)
    far = tab_ref[h, REL_BUCKETS - 1]
    for t in range(2):
        bk = bkt_ref[t]
        acc = jnp.zeros((blk, blk), F32)
        for i in range(REL_BUCKETS):
            acc = jnp.where(bk == i, (tab_ref[h, i] - far) * LOG2E, acc)
        if t == 0:
            acc = jnp.where(qry >= key, acc, NEG_INF)
        o_ref[0, t] = acc


def _bias_tiles(table_t):
    heads = table_t.shape[0]
    offs = np.arange(MOBA_BLOCK)
    dist_own = offs[None, :] - offs[:, None]
    buckets = np.stack([_t5_bucket_np(dist_own), _t5_bucket_np(dist_own + MOBA_BLOCK)])
    return pl.pallas_call(
        _bias_kernel,
        grid=(heads,),
        in_specs=[pl.BlockSpec(memory_space=pltpu.SMEM),
                  pl.BlockSpec((2, MOBA_BLOCK, MOBA_BLOCK), lambda h: (0, 0, 0))],
        out_specs=pl.BlockSpec((1, 2, MOBA_BLOCK, MOBA_BLOCK), lambda h: (h, 0, 0, 0)),
        out_shape=jax.ShapeDtypeStruct((heads, 2, MOBA_BLOCK, MOBA_BLOCK), F32),
        compiler_params=_params("parallel"),
        name="t5_bias",
    )(table_t, jnp.asarray(buckets))


def _moba_kernel(q_ref, k_ref, v_ref, bias_ref, o_ref, kmean_ref, kb_ref, vt_ref,
                 st0_ref, st1_ref, wq_ref, acc_ref):
    st_refs = (st0_ref, st1_ref)
    qb = pl.program_id(2)
    blk = MOBA_BLOCK
    d = A_HEAD_DIM
    nh = LANES // d
    nb = k_ref.shape[1] // blk
    ng = q_ref.shape[2] // LANES
    groups = range(ng)

    @pl.when(qb == 0)
    def _():
        lane = lax.broadcasted_iota(jnp.int32, (blk, LANES - d), 1)
        for g in groups:
            gl = slice(g * LANES, (g + 1) * LANES)
            for j in range(nb + 1):
                onehot = jnp.where(lane == j, 1.0, 0.0).astype(MXU_DTYPE)
                for hh in range(nh):
                    kb_ref[g, hh, j, :, d:LANES] = onehot
            for j in range(nb):
                kj = k_ref[0, j * blk:(j + 1) * blk, gl].astype(F32)
                for hh in range(nh):
                    kb_ref[g, hh, j, :, 0:d] = _mx(kj[:, hh * d:(hh + 1) * d])
                kmean_ref[g, j:j + 1, :] = jnp.mean(kj, axis=0, keepdims=True)
                vt_ref[g, j] = _mx(v_ref[0, j * blk:(j + 1) * blk, gl].astype(F32).T)
            for hh in range(nh):
                kb_ref[g, hh, nb, :, 0:d] = jnp.zeros((blk, d), MXU_DTYPE)
            vt_ref[g, nb] = jnp.zeros((LANES, blk), MXU_DTYPE)

    blk_id = lax.broadcasted_iota(jnp.int32, (nb, blk), 0)
    row = lax.broadcasted_iota(jnp.int32, (LANES - d - nb, blk), 0)
    for g in groups:
        qt = (q_ref[0, :, g * LANES:(g + 1) * LANES].astype(F32) * (d ** -0.5)).T
        for hh in range(nh):
            gate = lax.dot_general(kmean_ref[g, :, hh * d:(hh + 1) * d], qt[hh * d:(hh + 1) * d],
                                   (((1,), (0,)), ((), ())),
                                   precision=lax.Precision.HIGHEST, preferred_element_type=F32)
            gt = jnp.where(blk_id < qb, gate, -jnp.inf)
            sel = jnp.zeros((nb, blk), F32)
            for _ in range(MOBA_TOPK):
                m = jnp.max(gt, axis=0, keepdims=True)
                is_m = jnp.logical_and(gt == m, gt > -jnp.inf)
                idx = jnp.min(jnp.where(is_m, blk_id, nb), axis=0, keepdims=True)
                pick = blk_id == idx
                sel = jnp.where(pick, 1.0, sel)
                gt = jnp.where(pick, -jnp.inf, gt)
            keep = jnp.logical_or(sel > 0.0, blk_id == qb)
            wq_ref[g, hh] = _mx(jnp.concatenate([qt[hh * d:(hh + 1) * d] * LOG2E, jnp.where(keep, 0.0, NEG_INF),
                                                 jnp.where(row == 0, NEG_INF, 0.0)], axis=0))

    def score_stage(slot, ja, jb, with_bias):
        cmaxes = []
        for g in groups:
            for hh in range(nh):
                keys = jnp.concatenate([kb_ref[g, hh, ja], kb_ref[g, hh, jb]], axis=0)
                st = jnp.dot(keys, wq_ref[g, hh], preferred_element_type=F32)
                if with_bias:
                    st = st + jnp.concatenate([bias_ref[g * nh + hh, w] for w in range(2)], axis=0)
                st_refs[slot][g, 0, :, hh * blk:(hh + 1) * blk] = st[:blk]
                st_refs[slot][g, 1, :, hh * blk:(hh + 1) * blk] = st[blk:]
                cmaxes.append(jnp.max(st, axis=0, keepdims=True))
        return tuple(cmaxes)

    def attend_stage(slot, cmaxes, stats, ja, jb):
        new = []
        for g in groups:
            for hh in range(nh):
                cs = slice(hh * blk, (hh + 1) * blk)
                m_new = cmaxes[g * nh + hh]
                if stats is not None:
                    m, l = stats[g * nh + hh]
                    m_new = jnp.maximum(m, m_new)
                ps = [jnp.exp2(st_refs[slot][g, w, :, cs] - m_new) for w in range(2)]
                l_new = jnp.sum(ps[0], axis=0, keepdims=True) + jnp.sum(ps[1], axis=0, keepdims=True)
                vt = jnp.concatenate([vt_ref[g, ja, hh * d:(hh + 1) * d, :],
                                      vt_ref[g, jb, hh * d:(hh + 1) * d, :]], axis=1)
                pv = jnp.dot(vt, jnp.concatenate([_mx(ps[0]), _mx(ps[1])], axis=0),
                             preferred_element_type=F32)
                if stats is None:
                    acc_ref[g, hh] = pv
                    new.append((m_new, l_new))
                else:
                    alpha = jnp.exp2(m - m_new)
                    acc_ref[g, hh] = acc_ref[g, hh] * alpha + pv
                    new.append((m_new, alpha * l + l_new))
        return tuple(new)

    nfar = qb - 1
    npairs = (nfar + 1) // 2

    def far_blocks(i):
        return jnp.where(2 * i < nfar, 2 * i, nb), jnp.where(2 * i + 1 < nfar, 2 * i + 1, nb)

    j_prev = jnp.where(qb > 0, qb - 1, nb)
    cmax0 = score_stage(0, qb, j_prev, True)
    cmax1 = score_stage(1, *far_blocks(0), False)
    init = (attend_stage(0, cmax0, None, qb, j_prev), cmax1)

    def trip(slot, k, carry, prefetch=True):
        stats, cmax = carry
        cmax_next = score_stage(1 - slot, *far_blocks(k), False) if prefetch else cmax
        return attend_stage(slot, cmax, stats, *far_blocks(k - 1)), cmax_next

    def double_trip(i, carry):
        k = 2 * i + 1
        return trip(0, k + 1, trip(1, k, carry))

    carry = lax.fori_loop(0, npairs // 2, double_trip, init)
    stats, _ = lax.cond(npairs % 2 == 1, lambda cr: trip(1, npairs, cr, prefetch=False), lambda cr: cr, carry)
    for g in groups:
        out_t = jnp.concatenate([acc_ref[g, hh] / stats[g * nh + hh][1] for hh in range(nh)], axis=0)
        o_ref[0, :, g * LANES:(g + 1) * LANES] = _mx(out_t.T)


MOBA_GROUPS = 4


def _moba(qkv3d, bias):
    b, s, _ = qkv3d.shape
    ng = MOBA_GROUPS
    wd = ng * LANES
    nsteps = A_WIDTH // wd
    hpp = LANES // A_HEAD_DIM
    nb = s // MOBA_BLOCK
    return pl.pallas_call(
        _moba_kernel,
        grid=(b, nsteps, nb),
        in_specs=[pl.BlockSpec((1, MOBA_BLOCK, wd), lambda bi, p, i: (bi, i, p)),
                  pl.BlockSpec((1, s, wd), lambda bi, p, i: (bi, 0, nsteps + p)),
                  pl.BlockSpec((1, s, wd), lambda bi, p, i: (bi, 0, 2 * nsteps + p)),
                  pl.BlockSpec((ng * hpp, 2, MOBA_BLOCK, MOBA_BLOCK), lambda bi, p, i: (p, 0, 0, 0))],
        out_specs=pl.BlockSpec((1, MOBA_BLOCK, wd), lambda bi, p, i: (bi, i, p)),
        out_shape=jax.ShapeDtypeStruct((b, s, A_WIDTH), MXU_DTYPE),
        scratch_shapes=[pltpu.VMEM((ng, nb, LANES), F32),
                        pltpu.VMEM((ng, hpp, nb + 1, MOBA_BLOCK, LANES), MXU_DTYPE),
                        pltpu.VMEM((ng, nb + 1, LANES, MOBA_BLOCK), MXU_DTYPE),
                        pltpu.VMEM((ng, 2, MOBA_BLOCK, hpp * MOBA_BLOCK), F32),
                        pltpu.VMEM((ng, 2, MOBA_BLOCK, hpp * MOBA_BLOCK), F32),
                        pltpu.VMEM((ng, hpp, LANES, MOBA_BLOCK), MXU_DTYPE),
                        pltpu.VMEM((ng, hpp, A_HEAD_DIM, MOBA_BLOCK), F32)],
        compiler_params=_params("parallel", "parallel", "arbitrary"),
        name="moba",
    )(qkv3d, qkv3d, qkv3d, bias)


S5_HALF_CH = S5_WIDTH // 2
S5_HALF_ST = (S5_GROUPS // 2) * S5_STATE
S5_TILE = 256
S5_NT = S5_HALF_ST // S5_TILE
S5_STEPS = 128
S5_PITCH = S5_STEPS + SUBLANES
S5_H_PITCH = S5_STEPS + SUBLANES // 2
S5_SLABS = 2 * S5_HALF_ST // LANES


def _s5_kernel(u0_ref, u_ref, wx_ref, wy_ref, a_ref, o_ref, st_ref, x0_scr, x1_scr, h0_scr, h1_scr):
    nb, steps, _ = u_ref.shape
    nseq = 2 * nb
    tile_slabs = S5_TILE // LANES
    nre = S5_SLABS // 2
    i = pl.program_id(0)

    def project_in_items(src_ref, x_scr):
        def item(m, part, n):
            def run():
                slab0 = (part * S5_HALF_ST + n * S5_TILE) // LANES
                for half in range(2):
                    lhs = jnp.concatenate(
                        [_mx(src_ref[bi, :, half * S5_HALF_CH + m * LANES:half * S5_HALF_CH + (m + 1) * LANES])
                         for bi in range(nb)], axis=0)
                    res = jnp.dot(lhs, wx_ref[part, n, half * LANES:(half + 1) * LANES, :],
                                  preferred_element_type=F32)
                    for bi in range(nb):
                        seq = 2 * bi + half
                        for sl in range(tile_slabs):
                            x_scr[slab0 + sl, seq * S5_PITCH:seq * S5_PITCH + steps, :] = (
                                res[bi * steps:(bi + 1) * steps, sl * LANES:(sl + 1) * LANES])
            return run
        return [item(m, part, n) for m in range(S5_NT // 2) for part in range(2) for n in (2 * m, 2 * m + 1)]

    def recur_items(x_scr, h_scr, nitems):
        state = {}

        def item(t0, t1):
            def run():
                a_re = [a_ref[0, :, s * LANES:(s + 1) * LANES] for s in range(nre)]
                a_im = [a_ref[1, :, s * LANES:(s + 1) * LANES] for s in range(nre)]
                if t0 == 0:
                    state["re"] = [st_ref[0, :, s * LANES:(s + 1) * LANES] for s in range(nre)]
                    state["im"] = [st_ref[1, :, s * LANES:(s + 1) * LANES] for s in range(nre)]
                h_re, h_im = state["re"], state["im"]
                for t in range(t0, t1):
                    rows_t = pl.ds(t, nseq, stride=S5_PITCH)
                    for s in range(nre):
                        hr, hi = h_re[s], h_im[s]
                        h_re[s] = a_re[s] * hr - a_im[s] * hi + x_scr[s, rows_t, :]
                        h_im[s] = a_re[s] * hi + a_im[s] * hr + x_scr[nre + s, rows_t, :]
                        h_scr[s, pl.ds(t, nseq, stride=S5_H_PITCH), :] = h_re[s]
                        h_scr[nre + s, pl.ds(t, nseq, stride=S5_H_PITCH), :] = h_im[s]
                if t1 == steps:
                    for s in range(nre):
                        st_ref[0, :, s * LANES:(s + 1) * LANES] = h_re[s]
                        st_ref[1, :, s * LANES:(s + 1) * LANES] = h_im[s]
            return run
        per = steps // nitems
        return [item(k * per, (k + 1) * per) for k in range(nitems)]

    def project_out_items(h_scr):
        def item(m, half):
            def run():
                acc = jnp.zeros((nb * steps, LANES), F32)
                for n in (2 * m, 2 * m + 1):
                    for part in range(2):
                        slab0 = (part * S5_HALF_ST + n * S5_TILE) // LANES
                        h_tile = jnp.concatenate(
                            [jnp.concatenate([h_scr[slab0 + sl,
                                                    (2 * bi + half) * S5_H_PITCH:(2 * bi + half) * S5_H_PITCH + steps, :]
                                              for sl in range(tile_slabs)], axis=1) for bi in range(nb)], axis=0)
                        acc = acc + jnp.dot(_mx(h_tile), wy_ref[part, n, :, half * LANES:(half + 1) * LANES],
                                            preferred_element_type=F32)
                c = half * S5_HALF_CH + m * LANES
                for bi in range(nb):
                    o_ref[bi, :, c:c + LANES] = acc[bi * steps:(bi + 1) * steps, :]
            return run
        return [item(m, half) for m in range(S5_NT // 2) for half in range(2)]

    @pl.when(i == 0)
    def _():
        st_ref[...] = jnp.zeros_like(st_ref)
        h1_scr[...] = jnp.zeros_like(h1_scr)
        for run in project_in_items(u0_ref, x0_scr):
            run()

    def step(x_cur, h_cur, x_nxt, h_prev):
        ins = project_in_items(u_ref, x_nxt)
        outs = project_out_items(h_prev)
        recs = recur_items(x_cur, h_cur, len(ins))
        for k, rec in enumerate(recs):
            rec()
            ins[k]()
            if k % 2 == 1:
                outs[k // 2]()

    @pl.when(i % 2 == 0)
    def _():
        step(x0_scr, h0_scr, x1_scr, h1_scr)

    @pl.when(i % 2 == 1)
    def _():
        step(x1_scr, h1_scr, x0_scr, h0_scr)


def _s5_weights(lam_re, lam_im, b_re, b_im, c_re, c_im, log_dt):
    lr = jnp.minimum(lam_re.astype(F32), -1e-4)
    li = lam_im.astype(F32)
    dt = jnp.exp(log_dt.astype(F32))[:, None]
    mag = jnp.exp(lr * dt)
    a_re = mag * jnp.cos(li * dt)
    a_im = mag * jnp.sin(li * dt)
    den = lr * lr + li * li
    coef_re = ((a_re - 1.0) * lr + a_im * li) / den
    coef_im = (a_im * lr - (a_re - 1.0) * li) / den
    br = b_re.astype(F32)
    bi = b_im.astype(F32)
    bb_re = coef_re[..., None] * br - coef_im[..., None] * bi
    bb_im = coef_re[..., None] * bi + coef_im[..., None] * br
    gh = S5_GROUPS // 2
    eye = jnp.eye(gh, dtype=F32)

    def b_dense(bb):
        t = bb.reshape(2, gh, S5_STATE, S5_GROUP)
        return jnp.einsum('fgph,gk->fghkp', t, eye).reshape(2, S5_HALF_CH, S5_HALF_ST)

    def c_dense(cc):
        t = cc.reshape(2, gh, S5_GROUP, S5_STATE)
        return jnp.einsum('fghp,gk->fgpkh', t, eye).reshape(2, S5_HALF_ST, S5_HALF_CH)

    def x_tiles(bd):
        tiles = []
        for n in range(S5_NT):
            c = (n // 2) * LANES
            tiles.append(jnp.concatenate([bd[0, c:c + LANES, n * S5_TILE:(n + 1) * S5_TILE],
                                          bd[1, c:c + LANES, n * S5_TILE:(n + 1) * S5_TILE]], axis=0))
        return jnp.stack(tiles)

    def y_tiles(cd):
        tiles = []
        for n in range(S5_NT):
            c = (n // 2) * LANES
            tiles.append(jnp.concatenate([cd[0, n * S5_TILE:(n + 1) * S5_TILE, c:c + LANES],
                                          cd[1, n * S5_TILE:(n + 1) * S5_TILE, c:c + LANES]], axis=1))
        return jnp.stack(tiles)

    wx = jnp.stack([x_tiles(b_dense(bb_re)), x_tiles(b_dense(bb_im))]).astype(MXU_DTYPE)
    wy = jnp.stack([y_tiles(c_dense(c_re.astype(F32))), y_tiles(c_dense(-c_im.astype(F32)))]).astype(MXU_DTYPE)
    return wx, wy, a_re.reshape(2, S5_HALF_ST), a_im.reshape(2, S5_HALF_ST)


def _s5_scan(hp3d, wx, wy, a_re2, a_im2):
    b, s, width = hp3d.shape
    nseq = 2 * b
    assert nseq == SUBLANES
    sel = jnp.arange(nseq) % 2
    a = jnp.stack([a_re2[sel], a_im2[sel]])
    scr = pltpu.VMEM((S5_SLABS, nseq * S5_PITCH, LANES), F32)
    hscr = pltpu.VMEM((S5_SLABS, nseq * S5_H_PITCH, LANES), F32)
    last = s // S5_STEPS - 1
    ucol = width // S5_WIDTH - 1
    return pl.pallas_call(
        _s5_kernel,
        grid=(last + 2,),
        in_specs=[pl.BlockSpec((b, S5_STEPS, S5_WIDTH), lambda i: (0, jnp.minimum(i, last), ucol)),
                  pl.BlockSpec((b, S5_STEPS, S5_WIDTH), lambda i: (0, jnp.minimum(i + 1, last), ucol)),
                  _resident(wx.shape), _resident(wy.shape), _resident(a.shape)],
        out_specs=pl.BlockSpec((b, S5_STEPS, S5_WIDTH), lambda i: (0, jnp.maximum(i - 1, 0), 0)),
        out_shape=jax.ShapeDtypeStruct((b, s, S5_WIDTH), F32),
        scratch_shapes=[pltpu.VMEM((2, nseq, S5_HALF_ST), F32), scr, scr, hscr, hscr],
        compiler_params=_params("arbitrary"),
        name="s5_scan",
    )(hp3d, hp3d, wx, wy, a)


MLP_CHUNK = 1024
TAIL_SPLIT = 2


def _tail_body(mix, x_ref, g1_ref, b1_ref, w1_ref, w2_ref, g2_ref, b2_ref, o_ref):
    rows = x_ref.shape[0]
    for r0 in range(0, rows, rows // TAIL_SPLIT):
        r = slice(r0, r0 + rows // TAIL_SPLIT)
        x1 = _layer_norm(DN_ALPHA * x_ref[r, :] + mix[r], g1_ref[...], b1_ref[...])
        x1m = _mx(x1)
        acc = jnp.zeros(x1.shape, F32)
        for f in range(w1_ref.shape[1] // MLP_CHUNK):
            hid = jnp.dot(x1m, w1_ref[:, f * MLP_CHUNK:(f + 1) * MLP_CHUNK], preferred_element_type=F32)
            hid = jnp.square(jnp.maximum(hid, 0.0))
            acc = acc + jnp.dot(_mx(hid), w2_ref[f * MLP_CHUNK:(f + 1) * MLP_CHUNK, :],
                                preferred_element_type=F32)
        o_ref[r, :] = _layer_norm(DN_ALPHA * x1 + acc, g2_ref[...], b2_ref[...])


def _tail_even_kernel(attn_ref, y_ref, u_ref, dsk_ref, wg_ref, bg_ref, wo_ref, x_ref,
                      g1_ref, b1_ref, w1_ref, w2_ref, g2_ref, b2_ref, o_ref):
    y = y_ref[...] + dsk_ref[...] * u_ref[...]
    g = 0.5 * y * (1.0 + jnp.tanh(math.sqrt(2.0 / math.pi) * (y + 0.044715 * (y * y * y))))
    ssm = g * _sigmoid(jnp.dot(_mx(g), wg_ref[...], preferred_element_type=F32) + bg_ref[...])
    ka = attn_ref.shape[1]
    mix = (jnp.dot(_mx(attn_ref[...]), wo_ref[0:ka, :], preferred_element_type=F32)
           + jnp.dot(_mx(ssm), wo_ref[ka:, :], preferred_element_type=F32))
    _tail_body(mix, x_ref, g1_ref, b1_ref, w1_ref, w2_ref, g2_ref, b2_ref, o_ref)


def _tail_odd_kernel(a_ref, wo_ref, x_ref, g1_ref, b1_ref, w1_ref, w2_ref, g2_ref, b2_ref, o_ref):
    mix = jnp.dot(_mx(a_ref[...]), wo_ref[...], preferred_element_type=F32)
    _tail_body(mix, x_ref, g1_ref, b1_ref, w1_ref, w2_ref, g2_ref, b2_ref, o_ref)


def _row(v):
    return v.reshape(1, -1).astype(F32)


def _tail_common_specs(tm, d, dff):
    return [_resident((d, d)),
            pl.BlockSpec((tm, d), lambda i: (i, 0)),
            _resident((1, d)), _resident((1, d)),
            _resident((d, dff)), _resident((dff, d)),
            _resident((1, d)), _resident((1, d))]


def _tail_even(attn2d, y2d, hp2d, x2d, d_skip, w_glu, b_glu, w_out, g1, b1, w1, w2, g2, b2, tm):
    m, d = x2d.shape
    dff = w1.shape[1]
    ucol = hp2d.shape[1] // S5_WIDTH - 1
    return pl.pallas_call(
        _tail_even_kernel,
        grid=(m // tm,),
        in_specs=[pl.BlockSpec((tm, A_WIDTH), lambda i: (i, 0)),
                  pl.BlockSpec((tm, S5_WIDTH), lambda i: (i, 0)),
                  pl.BlockSpec((tm, S5_WIDTH), lambda i: (i, ucol)),
                  _resident((1, S5_WIDTH)), _resident((S5_WIDTH, S5_WIDTH)), _resident((1, S5_WIDTH))]
        + _tail_common_specs(tm, d, dff),
        out_specs=pl.BlockSpec((tm, d), lambda i: (i, 0)),
        out_shape=jax.ShapeDtypeStruct((m, d), F32),
        compiler_params=_params("parallel"),
        name="tail_even",
    )(attn2d, y2d, hp2d, _row(d_skip), _mx(w_glu), _row(b_glu), _mx(w_out), x2d,
      _row(g1), _row(b1), _mx(w1), _mx(w2), _row(g2), _row(b2))


def _tail_odd(a2d, x2d, w_out, g1, b1, w1, w2, g2, b2, tm):
    m, d = x2d.shape
    dff = w1.shape[1]
    return pl.pallas_call(
        _tail_odd_kernel,
        grid=(m // tm,),
        in_specs=[pl.BlockSpec((tm, a2d.shape[1]), lambda i: (i, 0))] + _tail_common_specs(tm, d, dff),
        out_specs=pl.BlockSpec((tm, d), lambda i: (i, 0)),
        out_shape=jax.ShapeDtypeStruct((m, d), F32),
        compiler_params=_params("parallel"),
        name="tail_odd",
    )(a2d, _mx(w_out), x2d, _row(g1), _row(b1), _mx(w1), _mx(w2), _row(g2), _row(b2))


GDN_TILE = 256
GDN_GATES_TILE = 2048


def _softplus(t):
    return jnp.maximum(t, 0.0) + jnp.log1p(jnp.exp(-jnp.abs(t)))


def _gdn_prep_body(i, hg, q_ref, k_ref, v_ref, qp_ref, kp_ref, vp_ref, cwq_ref, cwk_ref, cwv_ref,
                   ba_ref, g_ref, gt_ref, u_ref, w_ref, qd_ref, kd_ref, intra_ref, edl_ref,
                   qx_ref, kx_ref, vx_ref, ql_ref, kl_ref, vl_ref):
    assert C_CONV == 4
    t, width = q_ref.shape[1], q_ref.shape[2]
    c = C_CHUNK
    dk = C_HEAD_DIM
    hp = width // dk
    groups = t // SUBLANES

    def conv_silu(main_ref, prev_ref, cw_ref, ext_ref, low_ref):
        x = main_ref[0]
        ext_ref[0:SUBLANES, :] = jnp.zeros((SUBLANES, width), F32)
        ext_ref[SUBLANES:2 * SUBLANES, :] = jnp.where(i > 0, prev_ref[0], 0.0)
        ext_ref[2 * SUBLANES:, :] = x
        ext_rows = t + SUBLANES

        def tap(j):
            return jnp.broadcast_to(0.5 * cw_ref[j:j + 1, :], (SUBLANES, width))[None]

        def grouped(a):
            return a.reshape(a.shape[0] // SUBLANES, SUBLANES, width)

        x_ext = grouped(ext_ref[SUBLANES:, :])
        x_del = grouped(ext_ref[SUBLANES - 1:SUBLANES - 1 + ext_rows, :])
        low_ref[...] = (tap(1) * x_ext + tap(0) * x_del).reshape(ext_rows, width)
        hy = (tap(3) * grouped(x) + tap(2) * x_del[1:]
              + grouped(low_ref[SUBLANES - 2:SUBLANES - 2 + t, :])).reshape(t, width)
        return hy + hy * jnp.tanh(hy)

    def l2n(z, scale):
        return z * (lax.rsqrt(jnp.sum(z * z, axis=-1, keepdims=True) + L2_EPS) * scale)

    q_all = conv_silu(q_ref, qp_ref, cwq_ref, qx_ref, ql_ref)
    k_all = conv_silu(k_ref, kp_ref, cwk_ref, kx_ref, kl_ref)
    v_all = conv_silu(v_ref, vp_ref, cwv_ref, vx_ref, vl_ref)
    heads = [slice(hh * dk, (hh + 1) * dk) for hh in range(hp)]
    qs = [l2n(q_all[:, cs], dk ** -0.5) for cs in heads]
    ks = [l2n(k_all[:, cs], 1.0) for cs in heads]
    vs = [v_all[:, cs] for cs in heads]

    ba = ba_ref[0]
    lane = lax.broadcasted_iota(jnp.int32, (t, LANES), 1)
    betas, g_cols, g_rows = [], [], []
    for hh in range(hp):
        h = hg * hp + hh
        betas.append(_sigmoid(jnp.sum(jnp.where(lane == h, ba, 0.0), axis=1, keepdims=True)))
        g_cols.append(jnp.sum(jnp.where(lane == C_HEADS + h, g_ref[0], 0.0), axis=1, keepdims=True))
        g_rows.append(gt_ref[0, pl.ds(C_HEADS + h, 1), :])

    units = [(hh, slice(ch * c, (ch + 1) * c)) for hh in range(hp) for ch in range(t // c)]
    _gdn_prep_units(units, qs, ks, vs, betas, g_cols, g_rows, heads, u_ref, w_ref, qd_ref, kd_ref, intra_ref, edl_ref)


def _gdn_prep_units(units, qs, ks, vs, betas, g_cols, g_rows, heads, u_ref, w_ref, qd_ref, kd_ref, intra_ref, edl_ref):
    c = C_CHUNK
    dk = C_HEAD_DIM
    assert 2 * c == LANES and len(units) % 2 == 0
    ii = lax.broadcasted_iota(jnp.int32, (c, c), 0)
    jj = lax.broadcasted_iota(jnp.int32, (c, c), 1)
    row = lax.broadcasted_iota(jnp.int32, (c, 2 * c), 0)
    lane = lax.broadcasted_iota(jnp.int32, (c, 2 * c), 1)
    left = lane < c
    col = jnp.where(left, lane, lane - c)
    low = row >= col
    pairs = [(2 * n, 2 * n + 1) for n in range(len(units) // 2)]

    def side_by_side(a, b):
        za, zb = jnp.zeros_like(a), jnp.zeros_like(b)
        return jnp.concatenate([jnp.concatenate([a, zb], axis=1), jnp.concatenate([za, b], axis=1)], axis=0)

    def block_diag(p):
        return jnp.concatenate([jnp.where(left, p, 0.0), jnp.where(left, 0.0, p)], axis=0)

    gc_cols = [jnp.sum(jnp.where(ii >= jj, g_rows[hh][:, r], 0.0), axis=1, keepdims=True) for hh, r in units]
    gc_rows = [jnp.sum(jnp.where(ii <= jj, g_cols[hh][r], 0.0), axis=0, keepdims=True) for hh, r in units]
    decays = [jnp.where(low, jnp.exp(jnp.where(low, jnp.where(left, gc_cols[a], gc_cols[b])
                                               - jnp.concatenate([gc_rows[a], gc_rows[b]], axis=1), 0.0)), 0.0)
              for a, b in pairs]
    beta_bs = [jnp.broadcast_to(betas[hh][r], (c, dk)) for hh, r in units]
    kbs = [ks[hh][r] * bb for bb, (hh, r) in zip(beta_bs, units)]
    kdiag = [_mx(side_by_side(ks[units[a][0]][units[a][1]], ks[units[b][0]][units[b][1]])) for a, b in pairs]
    kks = [_dot_nt(jnp.concatenate([kbs[a], kbs[b]], axis=1), kd) for (a, b), kd in zip(pairs, kdiag)]
    qks = [_dot_nt(jnp.concatenate([qs[units[a][0]][units[a][1]], qs[units[b][0]][units[b][1]]], axis=1), kd)
           for (a, b), kd in zip(pairs, kdiag)]
    pws = [jnp.where(row > col, -(kk * dec), 0.0) for kk, dec in zip(kks, decays)]
    tms = pws
    pds = [_mx(block_diag(pw)) for pw in pws]
    for _ in range(int(math.log2(c)) - 1):
        pws = [jnp.dot(_mx(pw), pd, preferred_element_type=F32) for pw, pd in zip(pws, pds)]
        pds = [_mx(block_diag(pw)) for pw in pws]
        tms = [tm + pw + jnp.dot(_mx(tm), pd, preferred_element_type=F32) for tm, pw, pd in zip(tms, pws, pds)]
    egcs = [jnp.broadcast_to(jnp.exp(a), (c, dk)) for a in gc_cols]
    vbs = [vs[hh][r] * bb for bb, (hh, r) in zip(beta_bs, units)]
    kbes = [kb * e for kb, e in zip(kbs, egcs)]
    tbs = [_mx(tm) for tm in tms]
    tvs = [jnp.dot(tb, _mx(side_by_side(vbs[a], vbs[b])), preferred_element_type=F32) for tb, (a, b) in zip(tbs, pairs)]
    tks = [jnp.dot(tb, _mx(side_by_side(kbes[a], kbes[b])), preferred_element_type=F32) for tb, (a, b) in zip(tbs, pairs)]
    for n, (hh, r) in enumerate(units):
        gl = gc_cols[n][c - 1:c, :]
        cs = heads[hh]
        ch = r.start // c
        half = slice((n % 2) * dk, (n % 2 + 1) * dk)
        u_ref[0, r, cs] = vbs[n] + tvs[n // 2][:, half]
        w_ref[0, r, cs] = _mx(kbes[n] + tks[n // 2][:, half])
        qd_ref[0, r, cs] = _mx(qs[hh][r] * egcs[n])
        kd_ref[0, r, cs] = _mx(ks[hh][r] * jnp.exp(gl - gc_cols[n]))
        intra_ref[0, hh, r, :] = _mx((qks[n // 2] * decays[n // 2])[:, (n % 2) * c:(n % 2 + 1) * c])
        edl_ref[0, hh, ch:ch + 1, :] = jnp.broadcast_to(jnp.exp(gl), (1, LANES))


def _gdn_gates_kernel(ba_ref, alog_ref, dtb_ref, g_ref, gt_ref):
    g_all = -jnp.exp(alog_ref[...]) * _softplus(ba_ref[0] + dtb_ref[...])
    g_ref[0] = g_all
    gt_ref[0] = g_all.T


def _gdn_gates(hp3d, alog_row, dtb_row):
    b, s, _ = hp3d.shape
    t = min(GDN_GATES_TILE, s)
    ba_col = 4 * C_WIDTH // LANES
    return pl.pallas_call(
        _gdn_gates_kernel,
        grid=(b, s // t),
        in_specs=[pl.BlockSpec((1, t, LANES), lambda bi, i: (bi, i, ba_col)),
                  pl.BlockSpec((1, LANES), lambda bi, i: (0, 0)),
                  pl.BlockSpec((1, LANES), lambda bi, i: (0, 0))],
        out_specs=[pl.BlockSpec((1, t, LANES), lambda bi, i: (bi, i, 0)),
                   pl.BlockSpec((1, LANES, t), lambda bi, i: (bi, 0, i))],
        out_shape=[jax.ShapeDtypeStruct((b, s, LANES), F32), jax.ShapeDtypeStruct((b, LANES, s), F32)],
        compiler_params=_params("parallel", "parallel"),
        name="gdn_gates",
    )(hp3d, alog_row, dtb_row)


def _gdn_scan_body(chunk0, u_ref, w_ref, qd_ref, kd_ref, intra_ref, edl_ref, gate_ref, nw_ref, o_ref, st_ref):
    c = C_CHUNK
    dk = C_HEAD_DIM
    t = u_ref.shape[1]
    nw = nw_ref[...]
    heads = [slice(h * dk, (h + 1) * dk) for h in range(C_HEADS)]
    for ch in range(t // c):
        r = slice(ch * c, (ch + 1) * c)
        sts = [st_ref[h] for h in range(C_HEADS)]
        wss = [_dot(jnp.concatenate([w_ref[0, r, cs], qd_ref[0, r, cs]], axis=0), st)
               for cs, st in zip(heads, sts)]
        v_news = [u_ref[0, r, cs] - ws[:c] for cs, ws in zip(heads, wss)]
        kvs = [_dot_tn(kd_ref[0, r, cs], v_new) for cs, v_new in zip(heads, v_news)]
        os_ = [ws[c:] + _dot(intra_ref[0, h, r, :], v_new) for h, (ws, v_new) in enumerate(zip(wss, v_news))]
        for h, cs in enumerate(heads):
            st_ref[h] = sts[h] * edl_ref[0, h, pl.ds(chunk0 + ch, 1), 0:1] + kvs[h]
            o = os_[h]
            o = o * lax.rsqrt(jnp.mean(o * o, axis=-1, keepdims=True) + RMS_EPS) * nw
            hg = 0.5 * gate_ref[0, r, cs]
            o_ref[0, r, cs] = _mx(o * (hg + hg * jnp.tanh(hg)))


def _gdn_kernel(q_ref, k_ref, v_ref, qp_ref, kp_ref, vp_ref, cwq_ref, cwk_ref, cwv_ref, ba_ref, g_ref, gt_ref,
                gate_ref, nw_ref, o_ref, st_ref, qx_ref, kx_ref, vx_ref, ql_ref, kl_ref, vl_ref, *bufs):
    j = pl.program_id(1)
    last = pl.num_programs(1) - 2
    set_a, set_b = bufs[:len(bufs) // 2], bufs[len(bufs) // 2:]

    @pl.when(j == 0)
    def _():
        st_ref[...] = jnp.zeros_like(st_ref)
        for ref in bufs:
            ref[...] = jnp.zeros_like(ref)

    def step(cur, nxt):
        _gdn_prep_body(jnp.minimum(j, last), 0, q_ref, k_ref, v_ref, qp_ref, kp_ref, vp_ref,
                       cwq_ref, cwk_ref, cwv_ref, ba_ref, g_ref, gt_ref, *nxt,
                       qx_ref, kx_ref, vx_ref, ql_ref, kl_ref, vl_ref)
        _gdn_scan_body(0, *cur, gate_ref, nw_ref, o_ref, st_ref)

    @pl.when(j % 2 == 0)
    def _():
        step(set_a, set_b)

    @pl.when(j % 2 == 1)
    def _():
        step(set_b, set_a)


def _gdn(hp3d, conv_w, alog_row, dtb_row, norm_w):
    b, s, _ = hp3d.shape
    ba_col = 4 * C_WIDTH // LANES
    t = GDN_TILE
    nt = s // t
    per = t // SUBLANES
    wd = C_WIDTH
    nc = t // C_CHUNK
    g, gt = _gdn_gates(hp3d, alog_row, dtb_row)

    def prep_tile(j):
        return jnp.minimum(j, nt - 1)

    def scan_tile(j):
        return jnp.maximum(j - 1, 0)

    def main(off):
        return pl.BlockSpec((1, t, wd), lambda bi, j: (bi, prep_tile(j), off))

    def prev(off):
        return pl.BlockSpec((1, SUBLANES, wd), lambda bi, j: (bi, jnp.maximum(prep_tile(j) * per - 1, 0), off))

    def cw(off):
        return pl.BlockSpec((C_CONV, wd), lambda bi, j: (0, off))

    bufs = [pltpu.VMEM((1, t, wd), F32)] + [pltpu.VMEM((1, t, wd), MXU_DTYPE)] * 3 + [
        pltpu.VMEM((1, C_HEADS, t, C_CHUNK), MXU_DTYPE), pltpu.VMEM((1, C_HEADS, max(nc, SUBLANES), LANES), F32)]
    return pl.pallas_call(
        _gdn_kernel,
        grid=(b, nt + 1),
        in_specs=[main(0), main(1), main(2), prev(0), prev(1), prev(2), cw(0), cw(1), cw(2),
                  pl.BlockSpec((1, t, LANES), lambda bi, j: (bi, prep_tile(j), ba_col)),
                  pl.BlockSpec((1, t, LANES), lambda bi, j: (bi, prep_tile(j), 0)),
                  pl.BlockSpec((1, LANES, t), lambda bi, j: (bi, 0, prep_tile(j))),
                  pl.BlockSpec((1, t, wd), lambda bi, j: (bi, scan_tile(j), 3)),
                  pl.BlockSpec((1, C_HEAD_DIM), lambda bi, j: (0, 0))],
        out_specs=pl.BlockSpec((1, t, wd), lambda bi, j: (bi, scan_tile(j), 0)),
        out_shape=jax.ShapeDtypeStruct((b, s, wd), MXU_DTYPE),
        scratch_shapes=[pltpu.VMEM((C_HEADS, C_HEAD_DIM, C_HEAD_DIM), F32)]
        + [pltpu.VMEM((t + 2 * SUBLANES, wd), F32)] * 3 + [pltpu.VMEM((t + SUBLANES, wd), F32)] * 3
        + bufs + bufs,
        compiler_params=_params("parallel", "arbitrary"),
        name="gdn",
    )(hp3d, hp3d, hp3d, hp3d, hp3d, hp3d, conv_w, conv_w, conv_w, hp3d, g, gt, hp3d, _row(norm_w))


ROW_TILE = 512
PROJ_ROW_TILE = 1024


def kernel(x, rel_table, ev_w_in, ev_w_out, s5_lambda_re, s5_lambda_im, s5_b_re, s5_b_im, s5_c_re, s5_c_im, s5_d, s5_log_dt, s5_w_glu, s5_b_glu, od_w_in, od_conv_w, od_a_log, od_dt_bias, od_norm_w, od_w_out, ln_mix_g, ln_mix_b, mlp_w1, mlp_w2, ln_ffn_g, ln_ffn_b):
    bsz, s, d = x.shape
    m = bsz * s
    x2d = x.reshape(m, d)
    table_t = rel_table.T.astype(F32)
    bias = _bias_tiles(table_t)
    for layer in range(DEPTH):
        i = layer // 2
        if layer % 2 == 0:
            qkv, u = _proj_split(x2d, _mx(ev_w_in[i]), 3 * A_WIDTH, PROJ_ROW_TILE)
            attn = _moba(qkv.reshape(bsz, s, 3 * A_WIDTH), bias)
            wx, wy, a_re2, a_im2 = _s5_weights(s5_lambda_re[i], s5_lambda_im[i], s5_b_re[i], s5_b_im[i],
                                               s5_c_re[i], s5_c_im[i], s5_log_dt[i])
            y = _s5_scan(u.reshape(bsz, s, S5_WIDTH), wx, wy, a_re2, a_im2)
            x2d = _tail_even(attn.reshape(m, A_WIDTH), y.reshape(m, S5_WIDTH), u, x2d,
                             s5_d[i], s5_w_glu[i], s5_b_glu[i], ev_w_out[i],
                             ln_mix_g[layer], ln_mix_b[layer], mlp_w1[layer], mlp_w2[layer],
                             ln_ffn_g[layer], ln_ffn_b[layer], ROW_TILE)
        else:
            w_in = od_w_in[i]
            w_all = jnp.pad(w_in, ((0, 0), (0, LANES - 2 * C_HEADS)))
            hp = _proj(x2d, _mx(w_all), PROJ_ROW_TILE)
            hp3d = hp.reshape(bsz, s, 4 * C_WIDTH + LANES)
            pad8 = jnp.zeros((C_HEADS,), F32)
            padr = jnp.zeros((LANES - 2 * C_HEADS,), F32)
            alog_row = jnp.concatenate([pad8, od_a_log[i].astype(F32), padr]).reshape(1, LANES)
            dtb_row = jnp.concatenate([pad8, od_dt_bias[i].astype(F32), padr]).reshape(1, LANES)
            o = _gdn(hp3d, od_conv_w[i].astype(F32), alog_row, dtb_row, od_norm_w[i])
            x2d = _tail_odd(o.reshape(m, C_WIDTH), x2d, od_w_out[i],
                            ln_mix_g[layer], ln_mix_b[layer], mlp_w1[layer], mlp_w2[layer],
                            ln_ffn_g[layer], ln_ffn_b[layer], ROW_TILE)
    return x2d.reshape(bsz, s, d)
```

```python
import functools
import math

import jax
import jax.numpy as jnp
import numpy as np
from jax import lax
from jax.experimental import pallas as pl
from jax.experimental.pallas import tpu as pltpu

F32 = jnp.float32
MXU_DTYPE = jnp.bfloat16

LANES = 128
SUBLANES = 8
VMEM_LIMIT_BYTES = 56 * 1024 * 1024

A_HEADS = 8
A_HEAD_DIM = 64
A_WIDTH = A_HEADS * A_HEAD_DIM
MOBA_BLOCK = 256
MOBA_TOPK = 3
REL_BUCKETS = 32
REL_MAX_DIST = 128
S5_GROUP = 16
S5_GROUPS = 32
S5_STATE = 64
S5_WIDTH = S5_GROUP * S5_GROUPS
C_HEADS = 8
C_HEAD_DIM = 128
C_WIDTH = C_HEADS * C_HEAD_DIM
C_CONV = 4
C_CHUNK = 64
DEPTH = 2
DN_ALPHA = (2 * DEPTH) ** 0.25
LN_EPS = 1e-5
RMS_EPS = 1e-6
L2_EPS = 1e-6
NEG_INF = -1e30
LOG2E = math.log2(math.e)


def _mx(t):
    return t.astype(MXU_DTYPE)


def _dot(a, b):
    return jnp.dot(_mx(a), _mx(b), preferred_element_type=F32)


def _dot_tn(a, b):
    return lax.dot_general(_mx(a), _mx(b), (((0,), (0,)), ((), ())), preferred_element_type=F32)


def _dot_nt(a, b):
    return lax.dot_general(_mx(a), _mx(b), (((1,), (1,)), ((), ())), preferred_element_type=F32)


def _sigmoid(t):
    return 1.0 / (1.0 + jnp.exp(-t))


def _layer_norm(r, g, b):
    mu = jnp.mean(r, axis=-1, keepdims=True)
    d = r - mu
    var = jnp.mean(d * d, axis=-1, keepdims=True)
    return d * lax.rsqrt(var + LN_EPS) * g + b


def _params(*sem):
    return pltpu.CompilerParams(dimension_semantics=sem, vmem_limit_bytes=VMEM_LIMIT_BYTES)


def _resident(shape):
    return pl.BlockSpec(shape, lambda *_: (0,) * len(shape), pipeline_mode=pl.Buffered(1))


PROJ_CHUNK = 1024


def _proj_kernel(x_ref, w_ref, o_ref):
    xm = _mx(x_ref[...])
    n = w_ref.shape[1]
    for c0 in range(0, n, PROJ_CHUNK):
        c1 = min(c0 + PROJ_CHUNK, n)
        o_ref[:, c0:c1] = jnp.dot(xm, w_ref[:, c0:c1], preferred_element_type=F32)


def _proj_split_kernel(x_ref, w_ref, lo_ref, hi_ref):
    xm = _mx(x_ref[...])
    n_lo = lo_ref.shape[1]
    for c0 in range(0, n_lo, PROJ_CHUNK):
        c1 = min(c0 + PROJ_CHUNK, n_lo)
        lo_ref[:, c0:c1] = _mx(jnp.dot(xm, w_ref[:, c0:c1], preferred_element_type=F32))
    hi_ref[...] = jnp.dot(xm, w_ref[:, n_lo:], preferred_element_type=F32)


def _proj_split(x2d, w, n_lo, tm):
    m, k = x2d.shape
    n = w.shape[1]
    return pl.pallas_call(
        _proj_split_kernel,
        grid=(m // tm,),
        in_specs=[pl.BlockSpec((tm, k), lambda i: (i, 0)), _resident((k, n))],
        out_specs=[pl.BlockSpec((tm, n_lo), lambda i: (i, 0)), pl.BlockSpec((tm, n - n_lo), lambda i: (i, 0))],
        out_shape=[jax.ShapeDtypeStruct((m, n_lo), MXU_DTYPE), jax.ShapeDtypeStruct((m, n - n_lo), F32)],
        compiler_params=_params("parallel"),
        name="in_proj_split",
    )(x2d, w)


def _proj(x2d, w, tm):
    m, k = x2d.shape
    n = w.shape[1]
    return pl.pallas_call(
        _proj_kernel,
        grid=(m // tm,),
        in_specs=[pl.BlockSpec((tm, k), lambda i: (i, 0)), _resident((k, n))],
        out_specs=pl.BlockSpec((tm, n), lambda i: (i, 0)),
        out_shape=jax.ShapeDtypeStruct((m, n), F32),
        compiler_params=_params("parallel"),
        name="in_proj",
    )(x2d, w)


def _t5_bucket_np(dist):
    max_exact = REL_BUCKETS // 2
    n = np.maximum(dist, 0)
    nf = np.maximum(n, 1).astype(np.float32)
    large = max_exact + (np.log(nf / np.float32(max_exact)) / np.float32(math.log(REL_MAX_DIST / max_exact))
                         * np.float32(REL_BUCKETS - max_exact)).astype(np.int32)
    large = np.minimum(large, REL_BUCKETS - 1)
    return np.where(n < max_exact, n, large).astype(np.int32)


def _bias_kernel(tab_ref, bkt_ref, o_ref):
    h = pl.program_id(0)
    blk = bkt_ref.shape[-1]
    key = lax.broadcasted_iota(jnp.int32, (blk, blk), 0)
    qry = lax.broadcasted_iota(jnp.int32, (blk, blk), 1)
    far = tab_ref[h, REL_BUCKETS - 1]
    for t in range(2):
        bk = bkt_ref[t]
        acc = jnp.zeros((blk, blk), F32)
        for i in range(REL_BUCKETS):
            acc = jnp.where(bk == i, (tab_ref[h, i] - far) * LOG2E, acc)
        if t == 0:
            acc = jnp.where(qry >= key, acc, NEG_INF)
        o_ref[0, t] = acc


def _bias_tiles(table_t):
    heads = table_t.shape[0]
    offs = np.arange(MOBA_BLOCK)
    dist_own = offs[None, :] - offs[:, None]
    buckets = np.stack([_t5_bucket_np(dist_own), _t5_bucket_np(dist_own + MOBA_BLOCK)])
    return pl.pallas_call(
        _bias_kernel,
        grid=(heads,),
        in_specs=[pl.BlockSpec(memory_space=pltpu.SMEM),
                  pl.BlockSpec((2, MOBA_BLOCK, MOBA_BLOCK), lambda h: (0, 0, 0))],
        out_specs=pl.BlockSpec((1, 2, MOBA_BLOCK, MOBA_BLOCK), lambda h: (h, 0, 0, 0)),
        out_shape=jax.ShapeDtypeStruct((heads, 2, MOBA_BLOCK, MOBA_BLOCK), F32),
        compiler_params=_params("parallel"),
        name="t5_bias",
    )(table_t, jnp.asarray(buckets))


def _moba_kernel(q_ref, k_ref, v_ref, bias_ref, o_ref, kmean_ref, kb_ref, vt_ref,
                 st0_ref, st1_ref, wq_ref, acc_ref):
    st_refs = (st0_ref, st1_ref)
    qb = pl.program_id(2)
    blk = MOBA_BLOCK
    d = A_HEAD_DIM
    nh = LANES // d
    nb = k_ref.shape[1] // blk
    ng = q_ref.shape[2] // LANES
    groups = range(ng)

    @pl.when(qb == 0)
    def _():
        lane = lax.broadcasted_iota(jnp.int32, (blk, LANES - d), 1)
        for g in groups:
            gl = slice(g * LANES, (g + 1) * LANES)
            for j in range(nb + 1):
                onehot = jnp.where(lane == j, 1.0, 0.0).astype(MXU_DTYPE)
                for hh in range(nh):
                    kb_ref[g, hh, j, :, d:LANES] = onehot
            for j in range(nb):
                kj = k_ref[0, j * blk:(j + 1) * blk, gl].astype(F32)
                for hh in range(nh):
                    kb_ref[g, hh, j, :, 0:d] = _mx(kj[:, hh * d:(hh + 1) * d])
                kmean_ref[g, j:j + 1, :] = jnp.mean(kj, axis=0, keepdims=True)
                vt_ref[g, j] = _mx(v_ref[0, j * blk:(j + 1) * blk, gl].astype(F32).T)
            for hh in range(nh):
                kb_ref[g, hh, nb, :, 0:d] = jnp.zeros((blk, d), MXU_DTYPE)
            vt_ref[g, nb] = jnp.zeros((LANES, blk), MXU_DTYPE)

    blk_id = lax.broadcasted_iota(jnp.int32, (nb, blk), 0)
    row = lax.broadcasted_iota(jnp.int32, (LANES - d - nb, blk), 0)
    for g in groups:
        qt = (q_ref[0, :, g * LANES:(g + 1) * LANES].astype(F32) * (d ** -0.5)).T
        for hh in range(nh):
            gate = lax.dot_general(kmean_ref[g, :, hh * d:(hh + 1) * d], qt[hh * d:(hh + 1) * d],
                                   (((1,), (0,)), ((), ())),
                                   precision=lax.Precision.HIGHEST, preferred_element_type=F32)
            gt = jnp.where(blk_id < qb, gate, -jnp.inf)
            sel = jnp.zeros((nb, blk), F32)
            for _ in range(MOBA_TOPK):
                m = jnp.max(gt, axis=0, keepdims=True)
                is_m = jnp.logical_and(gt == m, gt > -jnp.inf)
                idx = jnp.min(jnp.where(is_m, blk_id, nb), axis=0, keepdims=True)
                pick = blk_id == idx
                sel = jnp.where(pick, 1.0, sel)
                gt = jnp.where(pick, -jnp.inf, gt)
            keep = jnp.logical_or(sel > 0.0, blk_id == qb)
            wq_ref[g, hh] = _mx(jnp.concatenate([qt[hh * d:(hh + 1) * d] * LOG2E, jnp.where(keep, 0.0, NEG_INF),
                                                 jnp.where(row == 0, NEG_INF, 0.0)], axis=0))

    def score_stage(slot, ja, jb, with_bias):
        cmaxes = []
        for g in groups:
            for hh in range(nh):
                keys = jnp.concatenate([kb_ref[g, hh, ja], kb_ref[g, hh, jb]], axis=0)
                st = jnp.dot(keys, wq_ref[g, hh], preferred_element_type=F32)
                if with_bias:
                    st = st + jnp.concatenate([bias_ref[g * nh + hh, w] for w in range(2)], axis=0)
                st_refs[slot][g, 0, :, hh * blk:(hh + 1) * blk] = st[:blk]
                st_refs[slot][g, 1, :, hh * blk:(hh + 1) * blk] = st[blk:]
                cmaxes.append(jnp.max(st, axis=0, keepdims=True))
        return tuple(cmaxes)

    def attend_stage(slot, cmaxes, stats, ja, jb):
        new = []
        for g in groups:
            for hh in range(nh):
                cs = slice(hh * blk, (hh + 1) * blk)
                m_new = cmaxes[g * nh + hh]
                if stats is not None:
                    m, l = stats[g * nh + hh]
                    m_new = jnp.maximum(m, m_new)
                ps = [jnp.exp2(st_refs[slot][g, w, :, cs] - m_new) for w in range(2)]
                l_new = jnp.sum(ps[0], axis=0, keepdims=True) + jnp.sum(ps[1], axis=0, keepdims=True)
                vt = jnp.concatenate([vt_ref[g, ja, hh * d:(hh + 1) * d, :],
                                      vt_ref[g, jb, hh * d:(hh + 1) * d, :]], axis=1)
                pv = jnp.dot(vt, jnp.concatenate([_mx(ps[0]), _mx(ps[1])], axis=0),
                             preferred_element_type=F32)
                if stats is None:
                    acc_ref[g, hh] = pv
                    new.append((m_new, l_new))
                else:
                    alpha = jnp.exp2(m - m_new)
                    acc_ref[g, hh] = acc_ref[g, hh] * alpha + pv
                    new.append((m_new, alpha * l + l_new))
        return tuple(new)

    nfar = qb - 1
    npairs = (nfar + 1) // 2

    def far_blocks(i):
        return jnp.where(2 * i < nfar, 2 * i, nb), jnp.where(2 * i + 1 < nfar, 2 * i + 1, nb)

    j_prev = jnp.where(qb > 0, qb - 1, nb)
    cmax0 = score_stage(0, qb, j_prev, True)
    cmax1 = score_stage(1, *far_blocks(0), False)
    init = (attend_stage(0, cmax0, None, qb, j_prev), cmax1)

    def trip(slot, k, carry, prefetch=True):
        stats, cmax = carry
        cmax_next = score_stage(1 - slot, *far_blocks(k), False) if prefetch else cmax
        return attend_stage(slot, cmax, stats, *far_blocks(k - 1)), cmax_next

    def double_trip(i, carry):
        k = 2 * i + 1
        return trip(0, k + 1, trip(1, k, carry))

    carry = lax.fori_loop(0, npairs // 2, double_trip, init)
    stats, _ = lax.cond(npairs % 2 == 1, lambda cr: trip(1, npairs, cr, prefetch=False), lambda cr: cr, carry)
    for g in groups:
        out_t = jnp.concatenate([acc_ref[g, hh] / stats[g * nh + hh][1] for hh in range(nh)], axis=0)
        o_ref[0, :, g * LANES:(g + 1) * LANES] = _mx(out_t.T)


MOBA_GROUPS = 4


def _moba(qkv3d, bias):
    b, s, _ = qkv3d.shape
    ng = MOBA_GROUPS
    wd = ng * LANES
    nsteps = A_WIDTH // wd
    hpp = LANES // A_HEAD_DIM
    nb = s // MOBA_BLOCK
    return pl.pallas_call(
        _moba_kernel,
        grid=(b, nsteps, nb),
        in_specs=[pl.BlockSpec((1, MOBA_BLOCK, wd), lambda bi, p, i: (bi, i, p)),
                  pl.BlockSpec((1, s, wd), lambda bi, p, i: (bi, 0, nsteps + p)),
                  pl.BlockSpec((1, s, wd), lambda bi, p, i: (bi, 0, 2 * nsteps + p)),
                  pl.BlockSpec((ng * hpp, 2, MOBA_BLOCK, MOBA_BLOCK), lambda bi, p, i: (p, 0, 0, 0))],
        out_specs=pl.BlockSpec((1, MOBA_BLOCK, wd), lambda bi, p, i: (bi, i, p)),
        out_shape=jax.ShapeDtypeStruct((b, s, A_WIDTH), MXU_DTYPE),
        scratch_shapes=[pltpu.VMEM((ng, nb, LANES), F32),
                        pltpu.VMEM((ng, hpp, nb + 1, MOBA_BLOCK, LANES), MXU_DTYPE),
                        pltpu.VMEM((ng, nb + 1, LANES, MOBA_BLOCK), MXU_DTYPE),
                        pltpu.VMEM((ng, 2, MOBA_BLOCK, hpp * MOBA_BLOCK), F32),
                        pltpu.VMEM((ng, 2, MOBA_BLOCK, hpp * MOBA_BLOCK), F32),
                        pltpu.VMEM((ng, hpp, LANES, MOBA_BLOCK), MXU_DTYPE),
                        pltpu.VMEM((ng, hpp, A_HEAD_DIM, MOBA_BLOCK), F32)],
        compiler_params=_params("parallel", "parallel", "arbitrary"),
        name="moba",
    )(qkv3d, qkv3d, qkv3d, bias)


S5_HALF_CH = S5_WIDTH // 2
S5_HALF_ST = (S5_GROUPS // 2) * S5_STATE
S5_TILE = 256
S5_NT = S5_HALF_ST // S5_TILE
S5_STEPS = 128
S5_PITCH = S5_STEPS + SUBLANES
S5_H_PITCH = S5_STEPS + SUBLANES // 2
S5_SLABS = 2 * S5_HALF_ST // LANES


def _s5_kernel(u0_ref, u_ref, wx_ref, wy_ref, a_ref, o_ref, st_ref, x0_scr, x1_scr, h0_scr, h1_scr):
    nb, steps, _ = u_ref.shape
    nseq = 2 * nb
    tile_slabs = S5_TILE // LANES
    nre = S5_SLABS // 2
    i = pl.program_id(0)

    def project_in_items(src_ref, x_scr):
        def item(m, part, n):
            def run():
                slab0 = (part * S5_HALF_ST + n * S5_TILE) // LANES
                for half in range(2):
                    lhs = jnp.concatenate(
                        [_mx(src_ref[bi, :, half * S5_HALF_CH + m * LANES:half * S5_HALF_CH + (m + 1) * LANES])
                         for bi in range(nb)], axis=0)
                    res = jnp.dot(lhs, wx_ref[part, n, half * LANES:(half + 1) * LANES, :],
                                  preferred_element_type=F32)
                    for bi in range(nb):
                        seq = 2 * bi + half
                        for sl in range(tile_slabs):
                            x_scr[slab0 + sl, seq * S5_PITCH:seq * S5_PITCH + steps, :] = (
                                res[bi * steps:(bi + 1) * steps, sl * LANES:(sl + 1) * LANES])
            return run
        return [item(m, part, n) for m in range(S5_NT // 2) for part in range(2) for n in (2 * m, 2 * m + 1)]

    def recur_items(x_scr, h_scr, nitems):
        state = {}

        def item(t0, t1):
            def run():
                a_re = [a_ref[0, :, s * LANES:(s + 1) * LANES] for s in range(nre)]
                a_im = [a_ref[1, :, s * LANES:(s + 1) * LANES] for s in range(nre)]
                if t0 == 0:
                    state["re"] = [st_ref[0, :, s * LANES:(s + 1) * LANES] for s in range(nre)]
                    state["im"] = [st_ref[1, :, s * LANES:(s + 1) * LANES] for s in range(nre)]
                h_re, h_im = state["re"], state["im"]
                for t in range(t0, t1):
                    rows_t = pl.ds(t, nseq, stride=S5_PITCH)
                    for s in range(nre):
                        hr, hi = h_re[s], h_im[s]
                        h_re[s] = a_re[s] * hr - a_im[s] * hi + x_scr[s, rows_t, :]
                        h_im[s] = a_re[s] * hi + a_im[s] * hr + x_scr[nre + s, rows_t, :]
                        h_scr[s, pl.ds(t, nseq, stride=S5_H_PITCH), :] = h_re[s]
                        h_scr[nre + s, pl.ds(t, nseq, stride=S5_H_PITCH), :] = h_im[s]
                if t1 == steps:
                    for s in range(nre):
                        st_ref[0, :, s * LANES:(s + 1) * LANES] = h_re[s]
                        st_ref[1, :, s * LANES:(s + 1) * LANES] = h_im[s]
            return run
        per = steps // nitems
        return [item(k * per, (k + 1) * per) for k in range(nitems)]

    def project_out_items(h_scr):
        def item(m, half):
            def run():
                acc = jnp.zeros((nb * steps, LANES), F32)
                for n in (2 * m, 2 * m + 1):
                    for part in range(2):
                        slab0 = (part * S5_HALF_ST + n * S5_TILE) // LANES
                        h_tile = jnp.concatenate(
                            [jnp.concatenate([h_scr[slab0 + sl,
                                                    (2 * bi + half) * S5_H_PITCH:(2 * bi + half) * S5_H_PITCH + steps, :]
                                              for sl in range(tile_slabs)], axis=1) for bi in range(nb)], axis=0)
                        acc = acc + jnp.dot(_mx(h_tile), wy_ref[part, n, :, half * LANES:(half + 1) * LANES],
                                            preferred_element_type=F32)
                c = half * S5_HALF_CH + m * LANES
                for bi in range(nb):
                    o_ref[bi, :, c:c + LANES] = acc[bi * steps:(bi + 1) * steps, :]
            return run
        return [item(m, half) for m in range(S5_NT // 2) for half in range(2)]

    @pl.when(i == 0)
    def _():
        st_ref[...] = jnp.zeros_like(st_ref)
        h1_scr[...] = jnp.zeros_like(h1_scr)
        for run in project_in_items(u0_ref, x0_scr):
            run()

    def step(x_cur, h_cur, x_nxt, h_prev):
        ins = project_in_items(u_ref, x_nxt)
        outs = project_out_items(h_prev)
        recs = recur_items(x_cur, h_cur, len(ins))
        for k, rec in enumerate(recs):
            rec()
            ins[k]()
            if k % 2 == 1:
                outs[k // 2]()

    @pl.when(i % 2 == 0)
    def _():
        step(x0_scr, h0_scr, x1_scr, h1_scr)

    @pl.when(i % 2 == 1)
    def _():
        step(x1_scr, h1_scr, x0_scr, h0_scr)


def _s5_weights(lam_re, lam_im, b_re, b_im, c_re, c_im, log_dt):
    lr = jnp.minimum(lam_re.astype(F32), -1e-4)
    li = lam_im.astype(F32)
    dt = jnp.exp(log_dt.astype(F32))[:, None]
    mag = jnp.exp(lr * dt)
    a_re = mag * jnp.cos(li * dt)
    a_im = mag * jnp.sin(li * dt)
    den = lr * lr + li * li
    coef_re = ((a_re - 1.0) * lr + a_im * li) / den
    coef_im = (a_im * lr - (a_re - 1.0) * li) / den
    br = b_re.astype(F32)
    bi = b_im.astype(F32)
    bb_re = coef_re[..., None] * br - coef_im[..., None] * bi
    bb_im = coef_re[..., None] * bi + coef_im[..., None] * br
    gh = S5_GROUPS // 2
    eye = jnp.eye(gh, dtype=F32)

    def b_dense(bb):
        t = bb.reshape(2, gh, S5_STATE, S5_GROUP)
        return jnp.einsum('fgph,gk->fghkp', t, eye).reshape(2, S5_HALF_CH, S5_HALF_ST)

    def c_dense(cc):
        t = cc.reshape(2, gh, S5_GROUP, S5_STATE)
        return jnp.einsum('fghp,gk->fgpkh', t, eye).reshape(2, S5_HALF_ST, S5_HALF_CH)

    def x_tiles(bd):
        tiles = []
        for n in range(S5_NT):
            c = (n // 2) * LANES
            tiles.append(jnp.concatenate([bd[0, c:c + LANES, n * S5_TILE:(n + 1) * S5_TILE],
                                          bd[1, c:c + LANES, n * S5_TILE:(n + 1) * S5_TILE]], axis=0))
        return jnp.stack(tiles)

    def y_tiles(cd):
        tiles = []
        for n in range(S5_NT):
            c = (n // 2) * LANES
            tiles.append(jnp.concatenate([cd[0, n * S5_TILE:(n + 1) * S5_TILE, c:c + LANES],
                                          cd[1, n * S5_TILE:(n + 1) * S5_TILE, c:c + LANES]], axis=1))
        return jnp.stack(tiles)

    wx = jnp.stack([x_tiles(b_dense(bb_re)), x_tiles(b_dense(bb_im))]).astype(MXU_DTYPE)
    wy = jnp.stack([y_tiles(c_dense(c_re.astype(F32))), y_tiles(c_dense(-c_im.astype(F32)))]).astype(MXU_DTYPE)
    return wx, wy, a_re.reshape(2, S5_HALF_ST), a_im.reshape(2, S5_HALF_ST)


def _s5_scan(hp3d, wx, wy, a_re2, a_im2):
    b, s, width = hp3d.shape
    nseq = 2 * b
    assert nseq == SUBLANES
    sel = jnp.arange(nseq) % 2
    a = jnp.stack([a_re2[sel], a_im2[sel]])
    scr = pltpu.VMEM((S5_SLABS, nseq * S5_PITCH, LANES), F32)
    hscr = pltpu.VMEM((S5_SLABS, nseq * S5_H_PITCH, LANES), F32)
    last = s // S5_STEPS - 1
    ucol = width // S5_WIDTH - 1
    return pl.pallas_call(
        _s5_kernel,
        grid=(last + 2,),
        in_specs=[pl.BlockSpec((b, S5_STEPS, S5_WIDTH), lambda i: (0, jnp.minimum(i, last), ucol)),
                  pl.BlockSpec((b, S5_STEPS, S5_WIDTH), lambda i: (0, jnp.minimum(i + 1, last), ucol)),
                  _resident(wx.shape), _resident(wy.shape), _resident(a.shape)],
        out_specs=pl.BlockSpec((b, S5_STEPS, S5_WIDTH), lambda i: (0, jnp.maximum(i - 1, 0), 0)),
        out_shape=jax.ShapeDtypeStruct((b, s, S5_WIDTH), F32),
        scratch_shapes=[pltpu.VMEM((2, nseq, S5_HALF_ST), F32), scr, scr, hscr, hscr],
        compiler_params=_params("arbitrary"),
        name="s5_scan",
    )(hp3d, hp3d, wx, wy, a)


MLP_CHUNK = 1024
TAIL_SPLIT = 2


def _tail_body(mix, x_ref, g1_ref, b1_ref, w1_ref, w2_ref, g2_ref, b2_ref, o_ref):
    rows = x_ref.shape[0]
    for r0 in range(0, rows, rows // TAIL_SPLIT):
        r = slice(r0, r0 + rows // TAIL_SPLIT)
        x1 = _layer_norm(DN_ALPHA * x_ref[r, :] + mix[r], g1_ref[...], b1_ref[...])
        x1m = _mx(x1)
        acc = jnp.zeros(x1.shape, F32)
        for f in range(w1_ref.shape[1] // MLP_CHUNK):
            hid = jnp.dot(x1m, w1_ref[:, f * MLP_CHUNK:(f + 1) * MLP_CHUNK], preferred_element_type=F32)
            hid = jnp.square(jnp.maximum(hid, 0.0))
            acc = acc + jnp.dot(_mx(hid), w2_ref[f * MLP_CHUNK:(f + 1) * MLP_CHUNK, :],
                                preferred_element_type=F32)
        o_ref[r, :] = _layer_norm(DN_ALPHA * x1 + acc, g2_ref[...], b2_ref[...])


def _tail_even_kernel(attn_ref, y_ref, u_ref, dsk_ref, wg_ref, bg_ref, wo_ref, x_ref,
                      g1_ref, b1_ref, w1_ref, w2_ref, g2_ref, b2_ref, o_ref):
    y = y_ref[...] + dsk_ref[...] * u_ref[...]
    g = 0.5 * y * (1.0 + jnp.tanh(math.sqrt(2.0 / math.pi) * (y + 0.044715 * (y * y * y))))
    ssm = g * _sigmoid(jnp.dot(_mx(g), wg_ref[...], preferred_element_type=F32) + bg_ref[...])
    ka = attn_ref.shape[1]
    mix = (jnp.dot(_mx(attn_ref[...]), wo_ref[0:ka, :], preferred_element_type=F32)
           + jnp.dot(_mx(ssm), wo_ref[ka:, :], preferred_element_type=F32))
    _tail_body(mix, x_ref, g1_ref, b1_ref, w1_ref, w2_ref, g2_ref, b2_ref, o_ref)


def _tail_odd_kernel(a_ref, wo_ref, x_ref, g1_ref, b1_ref, w1_ref, w2_ref, g2_ref, b2_ref, o_ref):
    mix = jnp.dot(_mx(a_ref[...]), wo_ref[...], preferred_element_type=F32)
    _tail_body(mix, x_ref, g1_ref, b1_ref, w1_ref, w2_ref, g2_ref, b2_ref, o_ref)


def _row(v):
    return v.reshape(1, -1).astype(F32)


def _tail_common_specs(tm, d, dff):
    return [_resident((d, d)),
            pl.BlockSpec((tm, d), lambda i: (i, 0)),
            _resident((1, d)), _resident((1, d)),
            _resident((d, dff)), _resident((dff, d)),
            _resident((1, d)), _resident((1, d))]


def _tail_even(attn2d, y2d, hp2d, x2d, d_skip, w_glu, b_glu, w_out, g1, b1, w1, w2, g2, b2, tm):
    m, d = x2d.shape
    dff = w1.shape[1]
    ucol = hp2d.shape[1] // S5_WIDTH - 1
    return pl.pallas_call(
        _tail_even_kernel,
        grid=(m // tm,),
        in_specs=[pl.BlockSpec((tm, A_WIDTH), lambda i: (i, 0)),
                  pl.BlockSpec((tm, S5_WIDTH), lambda i: (i, 0)),
                  pl.BlockSpec((tm, S5_WIDTH), lambda i: (i, ucol)),
                  _resident((1, S5_WIDTH)), _resident((S5_WIDTH, S5_WIDTH)), _resident((1, S5_WIDTH))]
        + _tail_common_specs(tm, d, dff),
        out_specs=pl.BlockSpec((tm, d), lambda i: (i, 0)),
        out_shape=jax.ShapeDtypeStruct((m, d), F32),
        compiler_params=_params("parallel"),
        name="tail_even",
    )(attn2d, y2d, hp2d, _row(d_skip), _mx(w_glu), _row(b_glu), _mx(w_out), x2d,
      _row(g1), _row(b1), _mx(w1), _mx(w2), _row(g2), _row(b2))


def _tail_odd(a2d, x2d, w_out, g1, b1, w1, w2, g2, b2, tm):
    m, d = x2d.shape
    dff = w1.shape[1]
    return pl.pallas_call(
        _tail_odd_kernel,
        grid=(m // tm,),
        in_specs=[pl.BlockSpec((tm, a2d.shape[1]), lambda i: (i, 0))] + _tail_common_specs(tm, d, dff),
        out_specs=pl.BlockSpec((tm, d), lambda i: (i, 0)),
        out_shape=jax.ShapeDtypeStruct((m, d), F32),
        compiler_params=_params("parallel"),
        name="tail_odd",
    )(a2d, _mx(w_out), x2d, _row(g1), _row(b1), _mx(w1), _mx(w2), _row(g2), _row(b2))


GDN_TILE = 256
GDN_GATES_TILE = 2048


def _softplus(t):
    return jnp.maximum(t, 0.0) + jnp.log1p(jnp.exp(-jnp.abs(t)))


def _gdn_prep_body(i, hg, q_ref, k_ref, v_ref, qp_ref, kp_ref, vp_ref, cwq_ref, cwk_ref, cwv_ref,
                   ba_ref, g_ref, gt_ref, u_ref, w_ref, qd_ref, kd_ref, intra_ref, edl_ref,
                   qx_ref, kx_ref, vx_ref, ql_ref, kl_ref, vl_ref):
    assert C_CONV == 4
    t, width = q_ref.shape[1], q_ref.shape[2]
    c = C_CHUNK
    dk = C_HEAD_DIM
    hp = width // dk
    groups = t // SUBLANES

    def conv_silu(main_ref, prev_ref, cw_ref, ext_ref, low_ref, cs):
        x = main_ref[0, :, cs]
        ext_ref[0:SUBLANES, cs] = jnp.zeros((SUBLANES, dk), F32)
        ext_ref[SUBLANES:2 * SUBLANES, cs] = jnp.where(i > 0, prev_ref[0, :, cs], 0.0)
        ext_ref[2 * SUBLANES:, cs] = x
        ext_rows = t + SUBLANES

        def tap(j):
            return jnp.broadcast_to(0.5 * cw_ref[j:j + 1, cs], (SUBLANES, dk))[None]

        def grouped(a):
            return a.reshape(a.shape[0] // SUBLANES, SUBLANES, dk)

        x_ext = grouped(ext_ref[SUBLANES:, cs])
        x_del = grouped(ext_ref[SUBLANES - 1:SUBLANES - 1 + ext_rows, cs])
        low_ref[:, cs] = (tap(1) * x_ext + tap(0) * x_del).reshape(ext_rows, dk)
        hy = (tap(3) * grouped(x) + tap(2) * x_del[1:]
              + grouped(low_ref[SUBLANES - 2:SUBLANES - 2 + t, cs])).reshape(t, dk)
        return hy + hy * jnp.tanh(hy)

    def l2n(z, scale):
        return z * (lax.rsqrt(jnp.sum(z * z, axis=-1, keepdims=True) + L2_EPS) * scale)

    heads = [slice(hh * dk, (hh + 1) * dk) for hh in range(hp)]
    qs = [l2n(conv_silu(q_ref, qp_ref, cwq_ref, qx_ref, ql_ref, cs), dk ** -0.5) for cs in heads]
    ks = [l2n(conv_silu(k_ref, kp_ref, cwk_ref, kx_ref, kl_ref, cs), 1.0) for cs in heads]
    vs = [conv_silu(v_ref, vp_ref, cwv_ref, vx_ref, vl_ref, cs) for cs in heads]

    ba = ba_ref[0]
    lane = lax.broadcasted_iota(jnp.int32, (t, LANES), 1)
    betas, g_cols, g_rows = [], [], []
    for hh in range(hp):
        h = hg * hp + hh
        betas.append(_sigmoid(jnp.sum(jnp.where(lane == h, ba, 0.0), axis=1, keepdims=True)))
        g_cols.append(jnp.sum(jnp.where(lane == C_HEADS + h, g_ref[0], 0.0), axis=1, keepdims=True))
        g_rows.append(gt_ref[0, pl.ds(C_HEADS + h, 1), :])

    units = [(hh, slice(ch * c, (ch + 1) * c)) for hh in range(hp) for ch in range(t // c)]
    _gdn_prep_units(units, qs, ks, vs, betas, g_cols, g_rows, heads, u_ref, w_ref, qd_ref, kd_ref, intra_ref, edl_ref)


def _gdn_prep_units(units, qs, ks, vs, betas, g_cols, g_rows, heads, u_ref, w_ref, qd_ref, kd_ref, intra_ref, edl_ref):
    c = C_CHUNK
    dk = C_HEAD_DIM
    assert 2 * c == LANES and len(units) % 2 == 0
    ii = lax.broadcasted_iota(jnp.int32, (c, c), 0)
    jj = lax.broadcasted_iota(jnp.int32, (c, c), 1)
    row = lax.broadcasted_iota(jnp.int32, (c, 2 * c), 0)
    lane = lax.broadcasted_iota(jnp.int32, (c, 2 * c), 1)
    left = lane < c
    col = jnp.where(left, lane, lane - c)
    low = row >= col
    pairs = [(2 * n, 2 * n + 1) for n in range(len(units) // 2)]

    def side_by_side(a, b):
        za, zb = jnp.zeros_like(a), jnp.zeros_like(b)
        return jnp.concatenate([jnp.concatenate([a, zb], axis=1), jnp.concatenate([za, b], axis=1)], axis=0)

    def block_diag(p):
        return jnp.concatenate([jnp.where(left, p, 0.0), jnp.where(left, 0.0, p)], axis=0)

    gc_cols = [jnp.sum(jnp.where(ii >= jj, g_rows[hh][:, r], 0.0), axis=1, keepdims=True) for hh, r in units]
    gc_rows = [jnp.sum(jnp.where(ii <= jj, g_cols[hh][r], 0.0), axis=0, keepdims=True) for hh, r in units]
    decays = [jnp.where(low, jnp.exp(jnp.where(low, jnp.where(left, gc_cols[a], gc_cols[b])
                                               - jnp.concatenate([gc_rows[a], gc_rows[b]], axis=1), 0.0)), 0.0)
              for a, b in pairs]
    beta_bs = [jnp.broadcast_to(betas[hh][r], (c, dk)) for hh, r in units]
    kbs = [ks[hh][r] * bb for bb, (hh, r) in zip(beta_bs, units)]
    kdiag = [_mx(side_by_side(ks[units[a][0]][units[a][1]], ks[units[b][0]][units[b][1]])) for a, b in pairs]
    kks = [_dot_nt(jnp.concatenate([kbs[a], kbs[b]], axis=1), kd) for (a, b), kd in zip(pairs, kdiag)]
    qks = [_dot_nt(jnp.concatenate([qs[units[a][0]][units[a][1]], qs[units[b][0]][units[b][1]]], axis=1), kd)
           for (a, b), kd in zip(pairs, kdiag)]
    pws = [jnp.where(row > col, -(kk * dec), 0.0) for kk, dec in zip(kks, decays)]
    tms = pws
    pds = [_mx(block_diag(pw)) for pw in pws]
    for _ in range(int(math.log2(c)) - 1):
        pws = [jnp.dot(_mx(pw), pd, preferred_element_type=F32) for pw, pd in zip(pws, pds)]
        pds = [_mx(block_diag(pw)) for pw in pws]
        tms = [tm + pw + jnp.dot(_mx(tm), pd, preferred_element_type=F32) for tm, pw, pd in zip(tms, pws, pds)]
    egcs = [jnp.broadcast_to(jnp.exp(a), (c, dk)) for a in gc_cols]
    vbs = [vs[hh][r] * bb for bb, (hh, r) in zip(beta_bs, units)]
    kbes = [kb * e for kb, e in zip(kbs, egcs)]
    tbs = [_mx(tm) for tm in tms]
    tvs = [jnp.dot(tb, _mx(side_by_side(vbs[a], vbs[b])), preferred_element_type=F32) for tb, (a, b) in zip(tbs, pairs)]
    tks = [jnp.dot(tb, _mx(side_by_side(kbes[a], kbes[b])), preferred_element_type=F32) for tb, (a, b) in zip(tbs, pairs)]
    for n, (hh, r) in enumerate(units):
        gl = gc_cols[n][c - 1:c, :]
        cs = heads[hh]
        ch = r.start // c
        half = slice((n % 2) * dk, (n % 2 + 1) * dk)
        u_ref[0, r, cs] = vbs[n] + tvs[n // 2][:, half]
        w_ref[0, r, cs] = _mx(kbes[n] + tks[n // 2][:, half])
        qd_ref[0, r, cs] = _mx(qs[hh][r] * egcs[n])
        kd_ref[0, r, cs] = _mx(ks[hh][r] * jnp.exp(gl - gc_cols[n]))
        intra_ref[0, hh, r, :] = _mx((qks[n // 2] * decays[n // 2])[:, (n % 2) * c:(n % 2 + 1) * c])
        edl_ref[0, hh, ch:ch + 1, :] = jnp.broadcast_to(jnp.exp(gl), (1, LANES))


def _gdn_gates_kernel(ba_ref, alog_ref, dtb_ref, g_ref, gt_ref):
    g_all = -jnp.exp(alog_ref[...]) * _softplus(ba_ref[0] + dtb_ref[...])
    g_ref[0] = g_all
    gt_ref[0] = g_all.T


def _gdn_gates(hp3d, alog_row, dtb_row):
    b, s, _ = hp3d.shape
    t = min(GDN_GATES_TILE, s)
    ba_col = 4 * C_WIDTH // LANES
    return pl.pallas_call(
        _gdn_gates_kernel,
        grid=(b, s // t),
        in_specs=[pl.BlockSpec((1, t, LANES), lambda bi, i: (bi, i, ba_col)),
                  pl.BlockSpec((1, LANES), lambda bi, i: (0, 0)),
                  pl.BlockSpec((1, LANES), lambda bi, i: (0, 0))],
        out_specs=[pl.BlockSpec((1, t, LANES), lambda bi, i: (bi, i, 0)),
                   pl.BlockSpec((1, LANES, t), lambda bi, i: (bi, 0, i))],
        out_shape=[jax.ShapeDtypeStruct((b, s, LANES), F32), jax.ShapeDtypeStruct((b, LANES, s), F32)],
        compiler_params=_params("parallel", "parallel"),
        name="gdn_gates",
    )(hp3d, alog_row, dtb_row)


def _gdn_scan_body(chunk0, u_ref, w_ref, qd_ref, kd_ref, intra_ref, edl_ref, gate_ref, nw_ref, o_ref, st_ref):
    c = C_CHUNK
    dk = C_HEAD_DIM
    t = u_ref.shape[1]
    nw = nw_ref[...]
    heads = [slice(h * dk, (h + 1) * dk) for h in range(C_HEADS)]
    for ch in range(t // c):
        r = slice(ch * c, (ch + 1) * c)
        sts = [st_ref[h] for h in range(C_HEADS)]
        wss = [_dot(jnp.concatenate([w_ref[0, r, cs], qd_ref[0, r, cs]], axis=0), st)
               for cs, st in zip(heads, sts)]
        v_news = [u_ref[0, r, cs] - ws[:c] for cs, ws in zip(heads, wss)]
        kvs = [_dot_tn(kd_ref[0, r, cs], v_new) for cs, v_new in zip(heads, v_news)]
        os_ = [ws[c:] + _dot(intra_ref[0, h, r, :], v_new) for h, (ws, v_new) in enumerate(zip(wss, v_news))]
        for h, cs in enumerate(heads):
            st_ref[h] = sts[h] * edl_ref[0, h, pl.ds(chunk0 + ch, 1), 0:1] + kvs[h]
            o = os_[h]
            o = o * lax.rsqrt(jnp.mean(o * o, axis=-1, keepdims=True) + RMS_EPS) * nw
            hg = 0.5 * gate_ref[0, r, cs]
            o_ref[0, r, cs] = _mx(o * (hg + hg * jnp.tanh(hg)))


def _gdn_kernel(q_ref, k_ref, v_ref, qp_ref, kp_ref, vp_ref, cwq_ref, cwk_ref, cwv_ref, ba_ref, g_ref, gt_ref,
                gate_ref, nw_ref, o_ref, st_ref, qx_ref, kx_ref, vx_ref, ql_ref, kl_ref, vl_ref, *bufs):
    j = pl.program_id(1)
    last = pl.num_programs(1) - 2
    set_a, set_b = bufs[:len(bufs) // 2], bufs[len(bufs) // 2:]

    @pl.when(j == 0)
    def _():
        st_ref[...] = jnp.zeros_like(st_ref)
        for ref in bufs:
            ref[...] = jnp.zeros_like(ref)

    def step(cur, nxt):
        _gdn_prep_body(jnp.minimum(j, last), 0, q_ref, k_ref, v_ref, qp_ref, kp_ref, vp_ref,
                       cwq_ref, cwk_ref, cwv_ref, ba_ref, g_ref, gt_ref, *nxt,
                       qx_ref, kx_ref, vx_ref, ql_ref, kl_ref, vl_ref)
        _gdn_scan_body(0, *cur, gate_ref, nw_ref, o_ref, st_ref)

    @pl.when(j % 2 == 0)
    def _():
        step(set_a, set_b)

    @pl.when(j % 2 == 1)
    def _():
        step(set_b, set_a)


def _gdn(hp3d, conv_w, alog_row, dtb_row, norm_w):
    b, s, _ = hp3d.shape
    ba_col = 4 * C_WIDTH // LANES
    t = GDN_TILE
    nt = s // t
    per = t // SUBLANES
    wd = C_WIDTH
    nc = t // C_CHUNK
    g, gt = _gdn_gates(hp3d, alog_row, dtb_row)

    def prep_tile(j):
        return jnp.minimum(j, nt - 1)

    def scan_tile(j):
        return jnp.maximum(j - 1, 0)

    def main(off):
        return pl.BlockSpec((1, t, wd), lambda bi, j: (bi, prep_tile(j), off))

    def prev(off):
        return pl.BlockSpec((1, SUBLANES, wd), lambda bi, j: (bi, jnp.maximum(prep_tile(j) * per - 1, 0), off))

    def cw(off):
        return pl.BlockSpec((C_CONV, wd), lambda bi, j: (0, off))

    bufs = [pltpu.VMEM((1, t, wd), F32)] + [pltpu.VMEM((1, t, wd), MXU_DTYPE)] * 3 + [
        pltpu.VMEM((1, C_HEADS, t, C_CHUNK), MXU_DTYPE), pltpu.VMEM((1, C_HEADS, max(nc, SUBLANES), LANES), F32)]
    return pl.pallas_call(
        _gdn_kernel,
        grid=(b, nt + 1),
        in_specs=[main(0), main(1), main(2), prev(0), prev(1), prev(2), cw(0), cw(1), cw(2),
                  pl.BlockSpec((1, t, LANES), lambda bi, j: (bi, prep_tile(j), ba_col)),
                  pl.BlockSpec((1, t, LANES), lambda bi, j: (bi, prep_tile(j), 0)),
                  pl.BlockSpec((1, LANES, t), lambda bi, j: (bi, 0, prep_tile(j))),
                  pl.BlockSpec((1, t, wd), lambda bi, j: (bi, scan_tile(j), 3)),
                  pl.BlockSpec((1, C_HEAD_DIM), lambda bi, j: (0, 0))],
        out_specs=pl.BlockSpec((1, t, wd), lambda bi, j: (bi, scan_tile(j), 0)),
        out_shape=jax.ShapeDtypeStruct((b, s, wd), MXU_DTYPE),
        scratch_shapes=[pltpu.VMEM((C_HEADS, C_HEAD_DIM, C_HEAD_DIM), F32)]
        + [pltpu.VMEM((t + 2 * SUBLANES, wd), F32)] * 3 + [pltpu.VMEM((t + SUBLANES, wd), F32)] * 3
        + bufs + bufs,
        compiler_params=_params("parallel", "arbitrary"),
        name="gdn",
    )(hp3d, hp3d, hp3d, hp3d, hp3d, hp3d, conv_w, conv_w, conv_w, hp3d, g, gt, hp3d, _row(norm_w))


ROW_TILE = 512
PROJ_ROW_TILE = 1024


def kernel(x, rel_table, ev_w_in, ev_w_out, s5_lambda_re, s5_lambda_im, s5_b_re, s5_b_im, s5_c_re, s5_c_im, s5_d, s5_log_dt, s5_w_glu, s5_b_glu, od_w_in, od_conv_w, od_a_log, od_dt_bias, od_norm_w, od_w_out, ln_mix_g, ln_mix_b, mlp_w1, mlp_w2, ln_ffn_g, ln_ffn_b):
    bsz, s, d = x.shape
    m = bsz * s
    x2d = x.reshape(m, d)
    table_t = rel_table.T.astype(F32)
    bias = _bias_tiles(table_t)
    for layer in range(DEPTH):
        i = layer // 2
        if layer % 2 == 0:
            qkv, u = _proj_split(x2d, _mx(ev_w_in[i]), 3 * A_WIDTH, PROJ_ROW_TILE)
            attn = _moba(qkv.reshape(bsz, s, 3 * A_WIDTH), bias)
            wx, wy, a_re2, a_im2 = _s5_weights(s5_lambda_re[i], s5_lambda_im[i], s5_b_re[i], s5_b_im[i],
                                               s5_c_re[i], s5_c_im[i], s5_log_dt[i])
            y = _s5_scan(u.reshape(bsz, s, S5_WIDTH), wx, wy, a_re2, a_im2)
            x2d = _tail_even(attn.reshape(m, A_WIDTH), y.reshape(m, S5_WIDTH), u, x2d,
                             s5_d[i], s5_w_glu[i], s5_b_glu[i], ev_w_out[i],
                             ln_mix_g[layer], ln_mix_b[layer], mlp_w1[layer], mlp_w2[layer],
                             ln_ffn_g[layer], ln_ffn_b[layer], ROW_TILE)
        else:
            w_in = od_w_in[i]
            w_all = jnp.pad(w_in, ((0, 0), (0, LANES - 2 * C_HEADS)))
            hp = _proj(x2d, _mx(w_all), PROJ_ROW_TILE)
            hp3d = hp.reshape(bsz, s, 4 * C_WIDTH + LANES)
            pad8 = jnp.zeros((C_HEADS,), F32)
            padr = jnp.zeros((LANES - 2 * C_HEADS,), F32)
            alog_row = jnp.concatenate([pad8, od_a_log[i].astype(F32), padr]).reshape(1, LANES)
            dtb_row = jnp.concatenate([pad8, od_dt_bias[i].astype(F32), padr]).reshape(1, LANES)
            o = _gdn(hp3d, od_conv_w[i].astype(F32), alog_row, dtb_row, od_norm_w[i])
            x2d = _tail_odd(o.reshape(m, C_WIDTH), x2d, od_w_out[i],
                            ln_mix_g[layer], ln_mix_b[layer], mlp_w1[layer], mlp_w2[layer],
                            ln_ffn_g[layer], ln_ffn_b[layer], ROW_TILE)
    return x2d.reshape(bsz, s, d)
```
